```python
import jax, jax.numpy as jnp
from jax import lax
import numpy as np

D_MODEL = 1024
BATCH = 8
SEQ = 2048
DEPTH = 1
DEC_BATCH = 128
DEC_SEQ = 8
PAST_LEN = 16384
PAGE_SIZE = 128

D_MIX = D_MODEL
A_WIDTH = D_MIX // 2
B_WIDTH = D_MIX - A_WIDTH
A_HEADS = 4
A_HEAD_DIM = A_WIDTH // A_HEADS
CHUNK = 128
B_GROUPS = 4
B_GROUP_DIM = B_WIDTH // B_GROUPS
POOL_WINDOWS = (2, 4, 8, 16)
POOL_STATE = max(POOL_WINDOWS) - 1
N_EXPERT_GROUPS = 4
EXPERTS_PER_GROUP = 8
N_EXPERTS = N_EXPERT_GROUPS * EXPERTS_PER_GROUP
TOP_K = 2
D_EXPERT = D_MODEL // 2
MOE_BLOCK = 128
EPS = 1e-6

kernel_name = "hymba_gmlp_pool_hmoe_step"


def rmsnorm(x, g):
    xf = x.astype(jnp.float32)
    r = lax.rsqrt(jnp.mean(xf * xf, axis=-1, keepdims=True) + EPS)
    return (xf * r).astype(x.dtype) * g


def layernorm(x, g, b):
    xf = x.astype(jnp.float32)
    mu = jnp.mean(xf, axis=-1, keepdims=True)
    var = jnp.mean(jnp.square(xf - mu), axis=-1, keepdims=True)
    return ((xf - mu) * lax.rsqrt(var + EPS)).astype(x.dtype) * g + b


def chunk_spatial_gate(u, v, a_ws, a_bs, chunk_len):
    bsz, L, _ = v.shape
    nc = L // chunk_len
    vh = v.reshape(bsz, nc, chunk_len, A_HEADS, A_HEAD_DIM)
    uh = u.reshape(bsz, nc, chunk_len, A_HEADS, A_HEAD_DIM)
    mask = jnp.tril(jnp.ones((chunk_len, chunk_len), dtype=bool))
    ws = jnp.where(mask, a_ws[:, :chunk_len, :chunk_len], jnp.zeros((), a_ws.dtype))
    z = jnp.einsum('hts,bcshd->bcthd', ws, vh) + a_bs[:, :chunk_len].T[None, None, :, :, None]
    return (uh * z).reshape(bsz, L, A_WIDTH)


def multiscale_pool(p, prefix, pos, b_w, b_scale):
    bsz, L, _ = p.shape
    ext = jnp.concatenate([prefix, p], axis=1).astype(jnp.float32)
    c = jnp.cumsum(ext, axis=1)
    c = jnp.concatenate([jnp.zeros((bsz, 1, B_WIDTH), jnp.float32), c], axis=1)
    outs = []
    for g, w in enumerate(POOL_WINDOWS):
        lo, hi = g * B_GROUP_DIM, (g + 1) * B_GROUP_DIM
        s = c[:, POOL_STATE + 1:, lo:hi] - c[:, POOL_STATE + 1 - w:POOL_STATE + 1 - w + L, lo:hi]
        cnt = jnp.minimum(pos + 1, w).astype(jnp.float32)[None, :, None]
        outs.append(s / cnt)
    pooled = jnp.concatenate(outs, axis=-1).astype(p.dtype) - p
    h = jnp.einsum('blgc,gcd->blgd', pooled.reshape(bsz, L, B_GROUPS, B_GROUP_DIM), b_w)
    return h.reshape(bsz, L, B_WIDTH) * b_scale


def hier_route(x2d, r1_w, r1_b, r2_w, r2_b):
    xf = x2d.astype(jnp.float32)
    l1 = xf @ r1_w.astype(jnp.float32) + r1_b.astype(jnp.float32)
    p1 = jax.nn.softmax(l1, axis=-1)
    grp = jnp.argmax(l1, axis=-1)
    pg = jnp.take_along_axis(p1, grp[:, None], axis=1)[:, 0]
    l2 = jnp.einsum('td,gde->tge', xf, r2_w.astype(jnp.float32)) + r2_b.astype(jnp.float32)
    l2g = jnp.take_along_axis(l2, grp[:, None, None], axis=1)[:, 0]
    top_v, top_i = lax.top_k(l2g, TOP_K)
    gates = jax.nn.softmax(top_v, axis=-1) * pg[:, None]
    eidx = grp[:, None] * EXPERTS_PER_GROUP + top_i
    return eidx, gates


def moe_ffn(x2d, eidx, gates, w1, w3, w2):
    T, D = x2d.shape
    A = T * TOP_K
    flat_e = eidx.reshape(-1)
    order = jnp.argsort(flat_e)
    sorted_e = flat_e[order]
    tok = order // TOP_K
    counts = jnp.bincount(flat_e, length=N_EXPERTS)
    padded = (counts + MOE_BLOCK - 1) // MOE_BLOCK * MOE_BLOCK
    pad_end = jnp.cumsum(padded)
    pad_start = pad_end - padded
    seg_start = jnp.cumsum(counts) - counts
    rank = jnp.arange(A, dtype=jnp.int32) - seg_start[sorted_e]
    dest = pad_start[sorted_e] + rank
    nb = (A + N_EXPERTS * (MOE_BLOCK - 1) + MOE_BLOCK - 1) // MOE_BLOCK
    xs = jnp.zeros((nb * MOE_BLOCK, D), x2d.dtype).at[dest].set(x2d[tok])
    block_e = jnp.minimum(jnp.searchsorted(pad_end, jnp.arange(nb) * MOE_BLOCK, side='right'),
                          N_EXPERTS - 1)

    def expert_block(args):
        xb, e = args
        h = jax.nn.silu(xb @ w1[e]) * (xb @ w3[e])
        return h @ w2[e]

    out = lax.map(expert_block, (xs.reshape(nb, MOE_BLOCK, D), block_e)).reshape(nb * MOE_BLOCK, D)
    g_sorted = gates.reshape(-1)[order].astype(x2d.dtype)
    return jax.ops.segment_sum(out[dest] * g_sorted[:, None], tok, num_segments=T)


def layer(x, pool_prefix, pos, chunk_len, norm1_g, w_in, a_norm_g, a_norm_b, a_ws, a_bs,
          b_w, b_scale, w_out, norm2_g, r1_w, r1_b, r2_w, r2_b, exp_w1, exp_w3, exp_w2):
    bsz, L, D = x.shape
    h = rmsnorm(x, norm1_g)
    proj = h @ w_in
    uv = jax.nn.gelu(proj[..., :2 * A_WIDTH], approximate=False)
    u = uv[..., :A_WIDTH]
    v = layernorm(uv[..., A_WIDTH:], a_norm_g, a_norm_b)
    p = proj[..., 2 * A_WIDTH:]
    ya = chunk_spatial_gate(u, v, a_ws, a_bs, chunk_len)
    yb = multiscale_pool(p, pool_prefix, pos, b_w, b_scale)
    x = x + jnp.concatenate([ya, yb], axis=-1) @ w_out
    h2 = rmsnorm(x, norm2_g).reshape(bsz * L, D)
    eidx, gates = hier_route(h2, r1_w, r1_b, r2_w, r2_b)
    x = x + moe_ffn(h2, eidx, gates, exp_w1, exp_w3, exp_w2).reshape(bsz, L, D)
    return x, p, v


def setup_inputs(seed: int = 0) -> dict:
    key = jax.random.key(seed)
    ks = jax.random.split(key, 24)
    f = jnp.float32
    nrm = lambda k, shape, s: jax.random.normal(k, shape, f) * s
    return {
        "x_prompt": nrm(ks[0], (BATCH, SEQ, D_MODEL), 1.0),
        "x_sample": nrm(ks[1], (DEC_BATCH, DEC_SEQ, D_MODEL), 1.0),
        "state_pool": nrm(ks[2], (DEPTH, DEC_BATCH, POOL_STATE, B_WIDTH), 1.0),
        "norm1_g": 1.0 + nrm(ks[3], (DEPTH, D_MODEL), 0.02),
        "w_in": nrm(ks[4], (DEPTH, D_MODEL, 2 * A_WIDTH + B_WIDTH), D_MODEL ** -0.5),
        "a_norm_g": 1.0 + nrm(ks[5], (DEPTH, A_WIDTH), 0.02),
        "a_norm_b": nrm(ks[6], (DEPTH, A_WIDTH), 0.02),
        "a_ws": nrm(ks[7], (DEPTH, A_HEADS, CHUNK, CHUNK), 0.5 * CHUNK ** -0.5),
        "a_bs": 1.0 + nrm(ks[8], (DEPTH, A_HEADS, CHUNK), 0.1),
        "b_w": nrm(ks[9], (DEPTH, B_GROUPS, B_GROUP_DIM, B_GROUP_DIM), B_GROUP_DIM ** -0.5),
        "b_scale": 1.0 + nrm(ks[10], (DEPTH, B_WIDTH), 0.1),
        "w_out": nrm(ks[11], (DEPTH, D_MIX, D_MODEL), D_MIX ** -0.5),
        "norm2_g": 1.0 + nrm(ks[12], (DEPTH, D_MODEL), 0.02),
        "r1_w": nrm(ks[13], (DEPTH, D_MODEL, N_EXPERT_GROUPS), D_MODEL ** -0.5),
        "r1_b": nrm(ks[14], (DEPTH, N_EXPERT_GROUPS), 0.01),
        "r2_w": nrm(ks[15], (DEPTH, N_EXPERT_GROUPS, D_MODEL, EXPERTS_PER_GROUP), D_MODEL ** -0.5),
        "r2_b": nrm(ks[16], (DEPTH, N_EXPERT_GROUPS, EXPERTS_PER_GROUP), 0.01),
        "exp_w1": nrm(ks[17], (DEPTH, N_EXPERTS, D_MODEL, D_EXPERT), D_MODEL ** -0.5),
        "exp_w3": nrm(ks[18], (DEPTH, N_EXPERTS, D_MODEL, D_EXPERT), D_MODEL ** -0.5),
        "exp_w2": nrm(ks[19], (DEPTH, N_EXPERTS, D_EXPERT, D_MODEL), D_EXPERT ** -0.5),
        "normf_g": 1.0 + nrm(ks[20], (D_MODEL,), 0.02),
    }


def reference(x_prompt, x_sample, state_pool, norm1_g, w_in, a_norm_g, a_norm_b, a_ws, a_bs,
              b_w, b_scale, w_out, norm2_g, r1_w, r1_b, r2_w, r2_b, exp_w1, exp_w3, exp_w2, normf_g):
    seq = x_prompt.shape[1]
    dec_seq = x_sample.shape[1]
    pos_prompt = jnp.arange(seq, dtype=jnp.int32)
    pos_sample = PAST_LEN + jnp.arange(dec_seq, dtype=jnp.int32)
    zero_prefix = jnp.zeros((x_prompt.shape[0], POOL_STATE, B_WIDTH), x_prompt.dtype)
    xp, xs = x_prompt, x_sample
    pool_p, pool_s, v_s = [], [], []
    for l in range(DEPTH):
        prm = (norm1_g[l], w_in[l], a_norm_g[l], a_norm_b[l], a_ws[l], a_bs[l], b_w[l], b_scale[l],
               w_out[l], norm2_g[l], r1_w[l], r1_b[l], r2_w[l], r2_b[l], exp_w1[l], exp_w3[l], exp_w2[l])
        xp, p_p, _ = layer(xp, zero_prefix, pos_prompt, CHUNK, *prm)
        xs, p_s, v_new = layer(xs, state_pool[l], pos_sample, dec_seq, *prm)
        pool_p.append(p_p[:, -POOL_STATE:])
        pool_s.append(jnp.concatenate([state_pool[l], p_s], axis=1)[:, -POOL_STATE:])
        v_s.append(v_new)
    y_prompt = rmsnorm(xp, normf_g)
    y_sample = rmsnorm(xs, normf_g)
    pool_state_prompt = jnp.stack(pool_p, axis=0)
    pool_state_sample = jnp.stack(pool_s, axis=0)
    chunk_v_sample = jnp.stack(v_s, axis=0)
    return (y_prompt, y_sample, pool_state_prompt, pool_state_sample, chunk_v_sample)
```

```python
import functools
import math

import jax
import jax.numpy as jnp
from jax import lax
from jax.experimental import pallas as pl
from jax.experimental.pallas import tpu as pltpu

D_MODEL = 1024
A_WIDTH = 512
B_WIDTH = 512
N_HEADS = 4
HEAD_DIM = 128
CHUNK = 128
POOL_WINDOWS = (2, 4, 8, 16)
POOL_STATE = 15
N_GROUPS = 4
EXPERTS_PER_GROUP = 8
N_EXPERTS = 32
D_EXPERT = 512
EPS = 1e-6

SUBLANES = 8
LANES = 128
ROW_TILES = D_MODEL // LANES

TM = 256
TE = 256
N_ROUTER_ROWS = 40
CARRY_ROWS = 16
VMEM_LIMIT = 48 * 1024 * 1024

_INV_SQRT2 = 1.0 / math.sqrt(2.0)


def _rmsnorm(x, g):
    r = lax.rsqrt(jnp.mean(x * x, axis=-1, keepdims=True) + EPS)
    return (x * r) * g


def _gelu(x):
    return 0.5 * x * (1.0 + lax.erf(x * _INV_SQRT2))


def _layernorm(x, g, b):
    mu = jnp.mean(x, axis=-1, keepdims=True)
    xc = x - mu
    var = jnp.mean(xc * xc, axis=-1, keepdims=True)
    return (xc * lax.rsqrt(var + EPS)) * g + b


def _in_proj(x, n1g_ref, win_ref, ang_ref, anb_ref):
    h = _rmsnorm(x, n1g_ref[...]).astype(jnp.bfloat16)
    proj = jnp.dot(h, win_ref[...], preferred_element_type=jnp.float32)
    uv = _gelu(proj[:, : 2 * A_WIDTH])
    u = uv[:, :A_WIDTH]
    v = _layernorm(uv[:, A_WIDTH:], ang_ref[...], anb_ref[...])
    p = proj[:, 2 * A_WIDTH:]
    return u, v, p


def _pool_project(pooled, g, bw_ref, bscale_ref):
    lo, hi = g * HEAD_DIM, (g + 1) * HEAD_DIM
    hb = jnp.dot(pooled.astype(jnp.bfloat16), bw_ref[g], preferred_element_type=jnp.float32)
    return hb * bscale_ref[:, lo:hi]


def _out_proj_and_route(x, mix_ref, wout_ref, n2g_ref, rwt_ref, rb_ref, su_ref, cnt_ref,
                        x1_ref, h2_ref, meta_ref, gcol_ref):
    tm = x.shape[0]
    x1 = x + jnp.dot(mix_ref[...], wout_ref[...], preferred_element_type=jnp.float32)
    x1_ref[...] = x1
    h2 = _rmsnorm(x1, n2g_ref[...])
    for s in range(ROW_TILES):
        h2_ref[:, s, :] = h2[:, s * LANES:(s + 1) * LANES]

    lt = lax.dot_general(rwt_ref[...], h2, (((1,), (1,)), ((), ())),
                         precision=lax.Precision.HIGHEST,
                         preferred_element_type=jnp.float32) + rb_ref[...]
    row = lambda i: lt[i:i + 1, :]
    l1 = [row(i) for i in range(N_GROUPS)]
    m1 = jnp.maximum(jnp.maximum(l1[0], l1[1]), jnp.maximum(l1[2], l1[3]))
    grp = jnp.where(l1[0] == m1, 0, jnp.where(l1[1] == m1, 1, jnp.where(l1[2] == m1, 2, 3)))
    se = (jnp.exp(l1[0] - m1) + jnp.exp(l1[1] - m1)) + (jnp.exp(l1[2] - m1) + jnp.exp(l1[3] - m1))
    pg = 1.0 / se
    l2 = []
    for e in range(EXPERTS_PER_GROUP):
        c = [row(N_GROUPS + g * EXPERTS_PER_GROUP + e) for g in range(N_GROUPS)]
        l2.append(jnp.where(grp == 0, c[0], jnp.where(grp == 1, c[1], jnp.where(grp == 2, c[2], c[3]))))
    v0 = functools.reduce(jnp.maximum, l2)
    i0 = jnp.full_like(grp, EXPERTS_PER_GROUP - 1)
    for e in range(EXPERTS_PER_GROUP - 2, -1, -1):
        i0 = jnp.where(l2[e] == v0, e, i0)
    neg = jnp.float32(-jnp.inf)
    l2m = [jnp.where(i0 == e, neg, l2[e]) for e in range(EXPERTS_PER_GROUP)]
    v1 = functools.reduce(jnp.maximum, l2m)
    i1 = jnp.full_like(grp, EXPERTS_PER_GROUP - 1)
    for e in range(EXPERTS_PER_GROUP - 2, -1, -1):
        i1 = jnp.where((l2m[e] == v1) & (i0 != e), e, i1)
    d = jnp.exp(v1 - v0)
    g0 = pg / (1.0 + d)
    g1 = (pg * d) / (1.0 + d)
    e0 = grp * EXPERTS_PER_GROUP + i0
    e1 = grp * EXPERTS_PER_GROUP + i1

    eiota = lax.broadcasted_iota(jnp.int32, (N_EXPERTS, tm), 0)
    hit0 = eiota == e0
    hit1 = eiota == e1
    onehot = (hit0 | hit1).astype(jnp.bfloat16)
    prefix = jnp.dot(onehot, su_ref[...], preferred_element_type=jnp.float32)
    carry = cnt_ref[...]
    base = prefix + jnp.concatenate([carry] * (tm // LANES), axis=1)
    r0 = jnp.sum(jnp.where(hit0, base, 0.0), axis=0, keepdims=True)
    r1 = jnp.sum(jnp.where(hit1, base, 0.0), axis=0, keepdims=True)
    ones = jnp.ones((tm, LANES), jnp.bfloat16)
    cnt_ref[...] = carry + jnp.dot(onehot, ones, preferred_element_type=jnp.float32)

    meta_ref[0:1, :] = e0
    meta_ref[1:2, :] = e1
    meta_ref[2:3, :] = r0.astype(jnp.int32)
    meta_ref[3:4, :] = r1.astype(jnp.int32)
    meta_ref[4:8, :] = jnp.zeros((4, tm), jnp.int32)

    giota = lax.broadcasted_iota(jnp.int32, (LANES, tm), 0)
    gpad = jnp.where(giota == 0, g0, jnp.where(giota == 1, g1, 0.0))
    gcol_ref[...] = gpad.T


def _mixer_prompt_kernel(x_ref, n1g_ref, win_ref, ang_ref, anb_ref, aws_ref, abias_ref, bw_ref, bscale_ref,
                         wout_ref, n2g_ref, rwt_ref, rb_ref, su_ref,
                         x1_ref, h2_ref, pstate_ref, meta_ref, gcol_ref, counts_ref,
                         mix_ref, pcarry_ref, cnt_ref):
    b = pl.program_id(0)
    j = pl.program_id(1)

    @pl.when((b == 0) & (j == 0))
    def _():
        cnt_ref[...] = jnp.zeros_like(cnt_ref)

    @pl.when(j == 0)
    def _():
        pcarry_ref[...] = jnp.zeros_like(pcarry_ref)

    x = x_ref[0]
    u, v, p = _in_proj(x, n1g_ref, win_ref, ang_ref, anb_ref)

    tri = (lax.broadcasted_iota(jnp.int32, (CHUNK, CHUNK), 0)
           >= lax.broadcasted_iota(jnp.int32, (CHUNK, CHUNK), 1))
    vb = v.astype(jnp.bfloat16)
    for hd in range(N_HEADS):
        lo, hi = hd * HEAD_DIM, (hd + 1) * HEAD_DIM
        w = jnp.where(tri, aws_ref[hd], 0.0).astype(jnp.bfloat16)
        for c in range(TM // CHUNK):
            r0, r1 = c * CHUNK, (c + 1) * CHUNK
            z = jnp.dot(w, vb[r0:r1, lo:hi], preferred_element_type=jnp.float32) + abias_ref[:, lo:hi]
            mix_ref[r0:r1, lo:hi] = (u[r0:r1, lo:hi] * z).astype(jnp.bfloat16)

    pos = j * TM + lax.broadcasted_iota(jnp.int32, (TM, LANES), 0)
    for g, w in enumerate(POOL_WINDOWS):
        lo, hi = g * HEAD_DIM, (g + 1) * HEAD_DIM
        pg = p[:, lo:hi]
        acc = jnp.concatenate([pcarry_ref[:, lo:hi], pg], axis=0)
        shift = 1
        while shift < w:
            acc = acc + pltpu.roll(acc, shift, 0)
            shift *= 2
        cnt = jnp.minimum(pos + 1, w).astype(jnp.float32)
        pooled = acc[CARRY_ROWS:, :] / cnt - pg
        mix_ref[:, A_WIDTH + lo:A_WIDTH + hi] = _pool_project(pooled, g, bw_ref, bscale_ref).astype(jnp.bfloat16)
    tail = p[TM - CARRY_ROWS:, :]
    pcarry_ref[...] = tail
    pstate_ref[0] = tail

    _out_proj_and_route(x, mix_ref, wout_ref, n2g_ref, rwt_ref, rb_ref, su_ref, cnt_ref,
                        x1_ref, h2_ref, meta_ref, gcol_ref)
    counts_ref[...] = cnt_ref[...]


SEQ_BLK = 32


def _mixer_sample_kernel(aws_ref, abs_ref,
                         x_ref, st_ref, cin_ref, n1g_ref, win_ref, ang_ref, anb_ref, bw_ref, bscale_ref,
                         wout_ref, n2g_ref, rwt_ref, rb_ref, su_ref,
                         x1_in, h2_in, meta_in, gcol_in,
                         x1_ref, h2_ref, meta_ref, gcol_ref, pp_ref, vp_ref, counts_ref,
                         mix_ref, cnt_ref):
    del x1_in, h2_in, meta_in, gcol_in
    q = pl.program_id(0)
    n_pos = TM // SEQ_BLK

    @pl.when(q == 0)
    def _():
        cnt_ref[...] = cin_ref[...]

    x = x_ref[...]
    u, v, p = _in_proj(x, n1g_ref, win_ref, ang_ref, anb_ref)
    for i in range(n_pos):
        vp_ref[0, i] = v[i * SEQ_BLK:(i + 1) * SEQ_BLK, :]
        pp_ref[0, i] = p[i * SEQ_BLK:(i + 1) * SEQ_BLK, :]

    for hd in range(N_HEADS):
        lo, hi = hd * HEAD_DIM, (hd + 1) * HEAD_DIM
        vplanes = [v[s * SEQ_BLK:(s + 1) * SEQ_BLK, lo:hi] for s in range(n_pos)]
        for i in range(n_pos):
            z = vplanes[0] * aws_ref[hd * 64 + i * 8]
            for s in range(1, i + 1):
                z = z + vplanes[s] * aws_ref[hd * 64 + i * 8 + s]
            z = z + abs_ref[hd * 8 + i]
            r0, r1 = i * SEQ_BLK, (i + 1) * SEQ_BLK
            mix_ref[r0:r1, lo:hi] = (u[r0:r1, lo:hi] * z).astype(jnp.bfloat16)

    for g, w in enumerate(POOL_WINDOWS):
        lo, hi = g * HEAD_DIM, (g + 1) * HEAD_DIM
        planes = [st_ref[0, k, :, lo:hi] for k in range(POOL_STATE)]
        planes += [p[i * SEQ_BLK:(i + 1) * SEQ_BLK, lo:hi] for i in range(n_pos)]
        pooled = []
        for i in range(n_pos):
            top = POOL_STATE + i
            s = planes[top - w + 1]
            for k in range(top - w + 2, top + 1):
                s = s + planes[k]
            pooled.append(s * (1.0 / w) - planes[top])
        pooled = jnp.concatenate(pooled, axis=0)
        mix_ref[:, A_WIDTH + lo:A_WIDTH + hi] = _pool_project(pooled, g, bw_ref, bscale_ref).astype(jnp.bfloat16)

    _out_proj_and_route(x, mix_ref, wout_ref, n2g_ref, rwt_ref, rb_ref, su_ref, cnt_ref,
                        x1_ref, h2_ref, meta_ref, gcol_ref)
    counts_ref[...] = cnt_ref[...]


def _dispatch_kernel(dest_ref, h2_ref, xs_in, xs_ref, sem):
    del xs_in
    i = pl.program_id(0)
    n = pl.num_programs(0)

    def row_copies(t, d):
        return pltpu.make_async_copy(h2_ref.at[t], xs_ref.at[d], sem)

    def wait_tile():
        pltpu.make_async_copy(h2_ref.at[pl.ds(0, 2 * TM)], xs_ref.at[pl.ds(0, 2 * TM)], sem).wait()

    unroll = 8

    def body(rb, carry):
        for uu in range(unroll):
            r = rb * unroll + uu
            t = i * TM + r
            row_copies(t, dest_ref[0, 0, 2 * r]).start()
            row_copies(t, dest_ref[0, 0, 2 * r + 1]).start()
        return carry

    lax.fori_loop(0, TM // unroll, body, 0)

    @pl.when(i > 0)
    def _():
        wait_tile()

    @pl.when(i == n - 1)
    def _():
        wait_tile()


def _expert_kernel(te_ref, nv_ref, xs_ref, w1_ref, w3_ref, w2_ref, out_ref):
    del te_ref
    i = pl.program_id(0)

    @pl.when(i < nv_ref[0])
    def _():
        x = jnp.concatenate([xs_ref[:, s, :] for s in range(ROW_TILES)], axis=1).astype(jnp.bfloat16)
        w1 = w1_ref[0].astype(jnp.bfloat16)
        w3 = w3_ref[0].astype(jnp.bfloat16)
        a = jnp.dot(x, w1, preferred_element_type=jnp.float32)
        b = jnp.dot(x, w3, preferred_element_type=jnp.float32)
        h = (a * jax.nn.sigmoid(a)) * b
        o = jnp.dot(h.astype(jnp.bfloat16), w2_ref[0].astype(jnp.bfloat16), preferred_element_type=jnp.float32)
        for s in range(ROW_TILES):
            out_ref[:, s, :] = o[:, s * LANES:(s + 1) * LANES]


def _combine_kernel(dest_ref, x1_ref, gcol_ref, nfg_ref, out_hbm, y_ref, buf0, buf1, sem):
    unroll = 8

    def body(rb, carry):
        for uu in range(unroll):
            r = rb * unroll + uu
            pltpu.make_async_copy(out_hbm.at[dest_ref[0, 0, 2 * r]], buf0.at[r], sem).start()
            pltpu.make_async_copy(out_hbm.at[dest_ref[0, 0, 2 * r + 1]], buf1.at[r], sem).start()
        return carry

    lax.fori_loop(0, TM // unroll, body, 0)
    pltpu.make_async_copy(out_hbm.at[pl.ds(0, TM)], buf0, sem).wait()
    pltpu.make_async_copy(out_hbm.at[pl.ds(0, TM)], buf1, sem).wait()

    o0 = jnp.concatenate([buf0[:, s, :] for s in range(ROW_TILES)], axis=1)
    o1 = jnp.concatenate([buf1[:, s, :] for s in range(ROW_TILES)], axis=1)
    g = gcol_ref[...]
    moe = g[:, 0:1] * o0 + g[:, 1:2] * o1
    y_ref[...] = _rmsnorm(x1_ref[...] + moe, nfg_ref[...])


def _const_spec(shape):
    return pl.BlockSpec(shape, lambda *_: (0,) * len(shape))


def kernel(x_prompt, x_sample, state_pool, norm1_g, w_in, a_norm_g, a_norm_b, a_ws, a_bs, b_w, b_scale, w_out,
           norm2_g, r1_w, r1_b, r2_w, r2_b, exp_w1, exp_w3, exp_w2, normf_g):
    f32, bf16, i32 = jnp.float32, jnp.bfloat16, jnp.int32
    n_batch, seq, _ = x_prompt.shape
    dec_batch, dec_seq, _ = x_sample.shape
    assert norm1_g.shape[0] == 1 and seq % TM == 0 and TM % CHUNK == 0
    assert dec_seq * SEQ_BLK == TM and dec_batch % SEQ_BLK == 0 and dec_seq <= CHUNK
    t_prompt = n_batch * seq
    t_sample = dec_batch * dec_seq
    t_total = t_prompt + t_sample
    n_tok_tiles = t_total // TM
    n_prompt_tiles = t_prompt // TM
    n_sample_tiles = t_sample // TM
    seq_tiles = seq // TM
    n_assign = 2 * t_total
    n_exp_tiles = -(-(n_assign + N_EXPERTS * (TE - 1)) // TE)
    p_rows = n_exp_tiles * TE

    n1g = norm1_g[0][None, :]
    n2g = norm2_g[0][None, :]
    nfg = normf_g[None, :]
    win = w_in[0].astype(bf16)
    wout = w_out[0].astype(bf16)
    ang = a_norm_g[0][None, :]
    anb = a_norm_b[0][None, :]
    bw = b_w[0].astype(bf16)
    bscale = b_scale[0][None, :]
    abias = jnp.repeat(a_bs[0][:, :CHUNK].T, HEAD_DIM, axis=1)
    rwt = jnp.concatenate([r1_w[0].T, r2_w[0].transpose(0, 2, 1).reshape(N_EXPERTS, D_MODEL),
                           jnp.zeros((N_ROUTER_ROWS - N_GROUPS - N_EXPERTS, D_MODEL), f32)], axis=0)
    rbias = jnp.concatenate([r1_b[0], r2_b[0].reshape(-1),
                             jnp.zeros((N_ROUTER_ROWS - N_GROUPS - N_EXPERTS,), f32)])
    rb = jnp.broadcast_to(rbias[:, None], (N_ROUTER_ROWS, TM))
    su = (jnp.arange(TM)[:, None] < jnp.arange(TM)[None, :]).astype(bf16)

    cparams = lambda sem: pltpu.CompilerParams(dimension_semantics=sem, vmem_limit_bytes=VMEM_LIMIT)
    weight_specs = [
        _const_spec((1, D_MODEL)), _const_spec((D_MODEL, 3 * A_WIDTH)), _const_spec((1, A_WIDTH)),
        _const_spec((1, A_WIDTH)),
    ]
    tail_specs = [
        _const_spec((N_GROUPS, HEAD_DIM, HEAD_DIM)), _const_spec((1, B_WIDTH)), _const_spec((D_MODEL, D_MODEL)),
        _const_spec((1, D_MODEL)), _const_spec((N_ROUTER_ROWS, D_MODEL)), _const_spec((N_ROUTER_ROWS, TM)),
        _const_spec((TM, TM)),
    ]
    tok_shapes = [
        jax.ShapeDtypeStruct((t_total, D_MODEL), f32),
        jax.ShapeDtypeStruct((t_total, ROW_TILES, LANES), f32),
        jax.ShapeDtypeStruct((SUBLANES, t_total), i32),
        jax.ShapeDtypeStruct((t_total, LANES), f32),
    ]

    tile_p = lambda b, j: b * seq_tiles + j
    x1, h2, pstate, meta, gcol, counts_p = pl.pallas_call(
        _mixer_prompt_kernel,
        grid=(n_batch, seq_tiles),
        in_specs=[pl.BlockSpec((1, TM, D_MODEL), lambda b, j: (b, j, 0))] + weight_specs + [
            _const_spec((N_HEADS, CHUNK, CHUNK)), _const_spec((CHUNK, A_WIDTH))] + tail_specs,
        out_specs=[
            pl.BlockSpec((TM, D_MODEL), lambda b, j: (tile_p(b, j), 0)),
            pl.BlockSpec((TM, ROW_TILES, LANES), lambda b, j: (tile_p(b, j), 0, 0)),
            pl.BlockSpec((1, CARRY_ROWS, B_WIDTH), lambda b, j: (b, 0, 0)),
            pl.BlockSpec((SUBLANES, TM), lambda b, j: (0, tile_p(b, j))),
            pl.BlockSpec((TM, LANES), lambda b, j: (tile_p(b, j), 0)),
            _const_spec((N_EXPERTS, LANES)),
        ],
        out_shape=[tok_shapes[0], tok_shapes[1], jax.ShapeDtypeStruct((n_batch, CARRY_ROWS, B_WIDTH), f32),
                   tok_shapes[2], tok_shapes[3], jax.ShapeDtypeStruct((N_EXPERTS, LANES), f32)],
        scratch_shapes=[pltpu.VMEM((TM, D_MODEL), bf16), pltpu.VMEM((CARRY_ROWS, B_WIDTH), f32),
                        pltpu.VMEM((N_EXPERTS, LANES), f32)],
        compiler_params=cparams(("arbitrary", "arbitrary")),
        name="mixer_prompt",
    )(x_prompt, n1g, win, ang, anb, a_ws[0][:, :CHUNK, :CHUNK], abias, bw, bscale, wout, n2g, rwt, rb, su)

    xs_planes = (x_sample.reshape(n_sample_tiles, SEQ_BLK, dec_seq, D_MODEL)
                 .transpose(0, 2, 1, 3).reshape(t_sample, D_MODEL))
    st_planes = (state_pool[0].reshape(n_sample_tiles, SEQ_BLK, POOL_STATE, B_WIDTH).transpose(0, 2, 1, 3))
    aws_s = a_ws[0][:, :dec_seq, :dec_seq].reshape(-1)
    abs_s = a_bs[0][:, :dec_seq].reshape(-1)
    tile_s = lambda q, *_: n_prompt_tiles + q
    plane_shape = jax.ShapeDtypeStruct((n_sample_tiles, dec_seq, SEQ_BLK, A_WIDTH), f32)
    x1, h2, meta, gcol, pplanes, vplanes, counts = pl.pallas_call(
        _mixer_sample_kernel,
        grid_spec=pltpu.PrefetchScalarGridSpec(
            num_scalar_prefetch=2,
            grid=(n_sample_tiles,),
            in_specs=[pl.BlockSpec((TM, D_MODEL), lambda q, *_: (q, 0)),
                      pl.BlockSpec((1, POOL_STATE, SEQ_BLK, B_WIDTH), lambda q, *_: (q, 0, 0, 0)),
                      _const_spec((N_EXPERTS, LANES))] + weight_specs + tail_specs + [
                pl.BlockSpec(memory_space=pl.ANY)] * 4,
            out_specs=[
                pl.BlockSpec((TM, D_MODEL), lambda q, *_: (tile_s(q), 0)),
                pl.BlockSpec((TM, ROW_TILES, LANES), lambda q, *_: (tile_s(q), 0, 0)),
                pl.BlockSpec((SUBLANES, TM), lambda q, *_: (0, tile_s(q))),
                pl.BlockSpec((TM, LANES), lambda q, *_: (tile_s(q), 0)),
                pl.BlockSpec((1, dec_seq, SEQ_BLK, A_WIDTH), lambda q, *_: (q, 0, 0, 0)),
                pl.BlockSpec((1, dec_seq, SEQ_BLK, A_WIDTH), lambda q, *_: (q, 0, 0, 0)),
                _const_spec((N_EXPERTS, LANES)),
            ],
            scratch_shapes=[pltpu.VMEM((TM, D_MODEL), bf16), pltpu.VMEM((N_EXPERTS, LANES), f32)],
        ),
        out_shape=tok_shapes + [plane_shape, plane_shape, jax.ShapeDtypeStruct((N_EXPERTS, LANES), f32)],
        input_output_aliases={16: 0, 17: 1, 18: 2, 19: 3},
        compiler_params=cparams(("arbitrary",)),
        name="mixer_sample",
    )(aws_s, abs_s, xs_planes, st_planes, counts_p, n1g, win, ang, anb, bw, bscale, wout, n2g, rwt, rb, su,
      x1, h2, meta, gcol)

    cnt = counts[:, 0].astype(i32)
    padded = (cnt + (TE - 1)) // TE * TE
    pad_end = jnp.cumsum(padded)
    pad_start = pad_end - padded
    d0 = pad_start[meta[0]] + meta[2]
    d1 = pad_start[meta[1]] + meta[3]
    dest = jnp.stack([d0, d1], axis=-1).reshape(n_tok_tiles, 1, 2 * TM)
    n_valid = (pad_end[-1] // TE).astype(i32)
    tile_ids = jnp.minimum(jnp.arange(n_exp_tiles, dtype=i32), n_valid - 1)
    tile_e = jnp.sum((pad_end[None, :] <= (tile_ids * TE)[:, None]).astype(i32), axis=1)
    tile_e = jnp.minimum(tile_e, N_EXPERTS - 1)

    dest_spec = pl.BlockSpec((1, 1, 2 * TM), lambda i: (i, 0, 0), memory_space=pltpu.SMEM)

    xs = pl.pallas_call(
        _dispatch_kernel,
        grid=(n_tok_tiles,),
        in_specs=[dest_spec, pl.BlockSpec(memory_space=pl.ANY), pl.BlockSpec(memory_space=pl.ANY)],
        out_specs=pl.BlockSpec(memory_space=pl.ANY),
        out_shape=jax.ShapeDtypeStruct((p_rows, ROW_TILES, LANES), f32),
        scratch_shapes=[pltpu.SemaphoreType.DMA(())],
        input_output_aliases={2: 0},
        compiler_params=cparams(("arbitrary",)),
        name="moe_dispatch",
    )(dest, h2, jnp.zeros((p_rows, ROW_TILES, LANES), f32))

    row_blk = lambda i, te, nv: (jnp.minimum(i, nv[0] - 1), 0, 0)
    w_blk = lambda i, te, nv: (te[i], 0, 0)
    out_sorted = pl.pallas_call(
        _expert_kernel,
        grid_spec=pltpu.PrefetchScalarGridSpec(
            num_scalar_prefetch=2,
            grid=(n_exp_tiles,),
            in_specs=[pl.BlockSpec((TE, ROW_TILES, LANES), row_blk),
                      pl.BlockSpec((1, D_MODEL, D_EXPERT), w_blk),
                      pl.BlockSpec((1, D_MODEL, D_EXPERT), w_blk),
                      pl.BlockSpec((1, D_EXPERT, D_MODEL), w_blk)],
            out_specs=pl.BlockSpec((TE, ROW_TILES, LANES), row_blk),
        ),
        out_shape=jax.ShapeDtypeStruct((p_rows, ROW_TILES, LANES), f32),
        compiler_params=cparams(("arbitrary",)),
        name="moe_experts",
    )(tile_e, n_valid[None], xs, exp_w1[0], exp_w3[0], exp_w2[0])

    y = pl.pallas_call(
        _combine_kernel,
        grid=(n_tok_tiles,),
        in_specs=[dest_spec,
                  pl.BlockSpec((TM, D_MODEL), lambda i: (i, 0)),
                  pl.BlockSpec((TM, LANES), lambda i: (i, 0)),
                  _const_spec((1, D_MODEL)),
                  pl.BlockSpec(memory_space=pl.ANY)],
        out_specs=pl.BlockSpec((TM, D_MODEL), lambda i: (i, 0)),
        out_shape=jax.ShapeDtypeStruct((t_total, D_MODEL), f32),
        scratch_shapes=[pltpu.VMEM((TM, ROW_TILES, LANES), f32), pltpu.VMEM((TM, ROW_TILES, LANES), f32),
                        pltpu.SemaphoreType.DMA(())],
        compiler_params=cparams(("arbitrary",)),
        name="moe_combine",
    )(dest, x1, gcol, nfg, out_sorted)

    unplane = lambda a: a.transpose(0, 2, 1, 3).reshape(dec_batch, dec_seq, a.shape[-1])
    y_prompt = y[:t_prompt].reshape(n_batch, seq, D_MODEL)
    y_sample = unplane(y[t_prompt:].reshape(n_sample_tiles, dec_seq, SEQ_BLK, D_MODEL))
    pool_state_prompt = pstate[None, :, CARRY_ROWS - POOL_STATE:, :]
    p_s = unplane(pplanes)
    pool_state_sample = jnp.concatenate([state_pool[0], p_s], axis=1)[None, :, -POOL_STATE:, :]
    chunk_v_sample = unplane(vplanes)[None]
    return (y_prompt, y_sample, pool_state_prompt, pool_state_sample, chunk_v_sample)
```

```python
import functools
import math

import jax
import jax.numpy as jnp
from jax import lax
from jax.experimental import pallas as pl
from jax.experimental.pallas import tpu as pltpu

D_MODEL = 1024
A_WIDTH = 512
B_WIDTH = 512
N_HEADS = 4
HEAD_DIM = 128
CHUNK = 128
POOL_WINDOWS = (2, 4, 8, 16)
POOL_STATE = 15
N_GROUPS = 4
EXPERTS_PER_GROUP = 8
N_EXPERTS = 32
D_EXPERT = 512
EPS = 1e-6

SUBLANES = 8
LANES = 128
ROW_TILES = D_MODEL // LANES

TM = 256
TE = 256
N_ROUTER_ROWS = 40
CARRY_ROWS = 16
VMEM_LIMIT = 48 * 1024 * 1024

_INV_SQRT2 = 1.0 / math.sqrt(2.0)


def _rmsnorm(x, g):
    r = lax.rsqrt(jnp.mean(x * x, axis=-1, keepdims=True) + EPS)
    return (x * r) * g


def _gelu(x):
    return 0.5 * x * (1.0 + lax.erf(x * _INV_SQRT2))


def _layernorm(x, g, b):
    mu = jnp.mean(x, axis=-1, keepdims=True)
    xc = x - mu
    var = jnp.mean(xc * xc, axis=-1, keepdims=True)
    return (xc * lax.rsqrt(var + EPS)) * g + b


def _in_proj(x, n1g_ref, win_ref, ang_ref, anb_ref):
    h = _rmsnorm(x, n1g_ref[...]).astype(jnp.bfloat16)
    proj = jnp.dot(h, win_ref[...], preferred_element_type=jnp.float32)
    uv = _gelu(proj[:, : 2 * A_WIDTH])
    u = uv[:, :A_WIDTH]
    v = _layernorm(uv[:, A_WIDTH:], ang_ref[...], anb_ref[...])
    p = proj[:, 2 * A_WIDTH:]
    return u, v, p


def _pool_project(pooled, g, bw_ref, bscale_ref):
    lo, hi = g * HEAD_DIM, (g + 1) * HEAD_DIM
    hb = jnp.dot(pooled.astype(jnp.bfloat16), bw_ref[g], preferred_element_type=jnp.float32)
    return hb * bscale_ref[:, lo:hi]


def _out_proj_and_route(x, mix_ref, wout_ref, n2g_ref, rwt_ref, rb_ref, su_ref, cnt_ref,
                        x1_ref, h2_ref, meta_ref, gcol_ref):
    tm = x.shape[0]
    x1 = x + jnp.dot(mix_ref[...], wout_ref[...], preferred_element_type=jnp.float32)
    x1_ref[...] = x1
    h2 = _rmsnorm(x1, n2g_ref[...])
    for s in range(ROW_TILES):
        h2_ref[:, s, :] = h2[:, s * LANES:(s + 1) * LANES]

    lt = lax.dot_general(rwt_ref[...], h2, (((1,), (1,)), ((), ())),
                         precision=lax.Precision.HIGHEST,
                         preferred_element_type=jnp.float32) + rb_ref[...]
    row = lambda i: lt[i:i + 1, :]
    l1 = [row(i) for i in range(N_GROUPS)]
    m1 = jnp.maximum(jnp.maximum(l1[0], l1[1]), jnp.maximum(l1[2], l1[3]))
    grp = jnp.where(l1[0] == m1, 0, jnp.where(l1[1] == m1, 1, jnp.where(l1[2] == m1, 2, 3)))
    se = (jnp.exp(l1[0] - m1) + jnp.exp(l1[1] - m1)) + (jnp.exp(l1[2] - m1) + jnp.exp(l1[3] - m1))
    pg = 1.0 / se
    l2 = []
    for e in range(EXPERTS_PER_GROUP):
        c = [row(N_GROUPS + g * EXPERTS_PER_GROUP + e) for g in range(N_GROUPS)]
        l2.append(jnp.where(grp == 0, c[0], jnp.where(grp == 1, c[1], jnp.where(grp == 2, c[2], c[3]))))
    v0 = functools.reduce(jnp.maximum, l2)
    i0 = jnp.full_like(grp, EXPERTS_PER_GROUP - 1)
    for e in range(EXPERTS_PER_GROUP - 2, -1, -1):
        i0 = jnp.where(l2[e] == v0, e, i0)
    neg = jnp.float32(-jnp.inf)
    l2m = [jnp.where(i0 == e, neg, l2[e]) for e in range(EXPERTS_PER_GROUP)]
    v1 = functools.reduce(jnp.maximum, l2m)
    i1 = jnp.full_like(grp, EXPERTS_PER_GROUP - 1)
    for e in range(EXPERTS_PER_GROUP - 2, -1, -1):
        i1 = jnp.where((l2m[e] == v1) & (i0 != e), e, i1)
    d = jnp.exp(v1 - v0)
    g0 = pg / (1.0 + d)
    g1 = (pg * d) / (1.0 + d)
    e0 = grp * EXPERTS_PER_GROUP + i0
    e1 = grp * EXPERTS_PER_GROUP + i1

    eiota = lax.broadcasted_iota(jnp.int32, (N_EXPERTS, tm), 0)
    hit0 = eiota == e0
    hit1 = eiota == e1
    onehot = (hit0 | hit1).astype(jnp.bfloat16)
    prefix = jnp.dot(onehot, su_ref[...], preferred_element_type=jnp.float32)
    carry = cnt_ref[...]
    base = prefix + jnp.concatenate([carry] * (tm // LANES), axis=1)
    r0 = jnp.sum(jnp.where(hit0, base, 0.0), axis=0, keepdims=True)
    r1 = jnp.sum(jnp.where(hit1, base, 0.0), axis=0, keepdims=True)
    ones = jnp.ones((tm, LANES), jnp.bfloat16)
    cnt_ref[...] = carry + jnp.dot(onehot, ones, preferred_element_type=jnp.float32)

    meta_ref[0:1, :] = e0
    meta_ref[1:2, :] = e1
    meta_ref[2:3, :] = r0.astype(jnp.int32)
    meta_ref[3:4, :] = r1.astype(jnp.int32)
    meta_ref[4:8, :] = jnp.zeros((4, tm), jnp.int32)

    giota = lax.broadcasted_iota(jnp.int32, (LANES, tm), 0)
    gpad = jnp.where(giota == 0, g0, jnp.where(giota == 1, g1, 0.0))
    gcol_ref[...] = gpad.T


def _mixer_prompt_kernel(x_ref, n1g_ref, win_ref, ang_ref, anb_ref, aws_ref, abias_ref, bw_ref, bscale_ref,
                         wout_ref, n2g_ref, rwt_ref, rb_ref, su_ref,
                         x1_ref, h2_ref, pstate_ref, meta_ref, gcol_ref, counts_ref,
                         mix_ref, pcarry_ref, cnt_ref):
    b = pl.program_id(0)
    j = pl.program_id(1)

    @pl.when((b == 0) & (j == 0))
    def _():
        cnt_ref[...] = jnp.zeros_like(cnt_ref)

    @pl.when(j == 0)
    def _():
        pcarry_ref[...] = jnp.zeros_like(pcarry_ref)

    x = x_ref[0]
    u, v, p = _in_proj(x, n1g_ref, win_ref, ang_ref, anb_ref)

    tri = (lax.broadcasted_iota(jnp.int32, (CHUNK, CHUNK), 0)
           >= lax.broadcasted_iota(jnp.int32, (CHUNK, CHUNK), 1))
    vb = v.astype(jnp.bfloat16)
    for hd in range(N_HEADS):
        lo, hi = hd * HEAD_DIM, (hd + 1) * HEAD_DIM
        w = jnp.where(tri, aws_ref[hd], 0.0).astype(jnp.bfloat16)
        for c in range(TM // CHUNK):
            r0, r1 = c * CHUNK, (c + 1) * CHUNK
            z = jnp.dot(w, vb[r0:r1, lo:hi], preferred_element_type=jnp.float32) + abias_ref[:, lo:hi]
            mix_ref[r0:r1, lo:hi] = (u[r0:r1, lo:hi] * z).astype(jnp.bfloat16)

    pos = j * TM + lax.broadcasted_iota(jnp.int32, (TM, LANES), 0)
    for g, w in enumerate(POOL_WINDOWS):
        lo, hi = g * HEAD_DIM, (g + 1) * HEAD_DIM
        pg = p[:, lo:hi]
        acc = jnp.concatenate([pcarry_ref[:, lo:hi], pg], axis=0)
        shift = 1
        while shift < w:
            acc = acc + pltpu.roll(acc, shift, 0)
            shift *= 2
        cnt = jnp.minimum(pos + 1, w).astype(jnp.float32)
        pooled = acc[CARRY_ROWS:, :] / cnt - pg
        mix_ref[:, A_WIDTH + lo:A_WIDTH + hi] = _pool_project(pooled, g, bw_ref, bscale_ref).astype(jnp.bfloat16)
    tail = p[TM - CARRY_ROWS:, :]
    pcarry_ref[...] = tail
    pstate_ref[0] = tail

    _out_proj_and_route(x, mix_ref, wout_ref, n2g_ref, rwt_ref, rb_ref, su_ref, cnt_ref,
                        x1_ref, h2_ref, meta_ref, gcol_ref)
    counts_ref[...] = cnt_ref[...]


SEQ_BLK = 32


def _mixer_sample_kernel(aws_ref, abs_ref,
                         x_ref, st_ref, cin_ref, n1g_ref, win_ref, ang_ref, anb_ref, bw_ref, bscale_ref,
                         wout_ref, n2g_ref, rwt_ref, rb_ref, su_ref,
                         x1_in, h2_in, meta_in, gcol_in,
                         x1_ref, h2_ref, meta_ref, gcol_ref, pp_ref, vp_ref, counts_ref,
                         mix_ref, cnt_ref):
    del x1_in, h2_in, meta_in, gcol_in
    q = pl.program_id(0)
    n_pos = TM // SEQ_BLK

    @pl.when(q == 0)
    def _():
        cnt_ref[...] = cin_ref[...]

    x = x_ref[...]
    u, v, p = _in_proj(x, n1g_ref, win_ref, ang_ref, anb_ref)
    for i in range(n_pos):
        vp_ref[0, i] = v[i * SEQ_BLK:(i + 1) * SEQ_BLK, :]
        pp_ref[0, i] = p[i * SEQ_BLK:(i + 1) * SEQ_BLK, :]

    for hd in range(N_HEADS):
        lo, hi = hd * HEAD_DIM, (hd + 1) * HEAD_DIM
        vplanes = [v[s * SEQ_BLK:(s + 1) * SEQ_BLK, lo:hi] for s in range(n_pos)]
        for i in range(n_pos):
            z = vplanes[0] * aws_ref[hd * 64 + i * 8]
            for s in range(1, i + 1):
                z = z + vplanes[s] * aws_ref[hd * 64 + i * 8 + s]
            z = z + abs_ref[hd * 8 + i]
            r0, r1 = i * SEQ_BLK, (i + 1) * SEQ_BLK
            mix_ref[r0:r1, lo:hi] = (u[r0:r1, lo:hi] * z).astype(jnp.bfloat16)

    for g, w in enumerate(POOL_WINDOWS):
        lo, hi = g * HEAD_DIM, (g + 1) * HEAD_DIM
        planes = [st_ref[0, k, :, lo:hi] for k in range(POOL_STATE)]
        planes += [p[i * SEQ_BLK:(i + 1) * SEQ_BLK, lo:hi] for i in range(n_pos)]
        pooled = []
        for i in range(n_pos):
            top = POOL_STATE + i
            s = planes[top - w + 1]
            for k in range(top - w + 2, top + 1):
                s = s + planes[k]
            pooled.append(s * (1.0 / w) - planes[top])
        pooled = jnp.concatenate(pooled, axis=0)
        mix_ref[:, A_WIDTH + lo:A_WIDTH + hi] = _pool_project(pooled, g, bw_ref, bscale_ref).astype(jnp.bfloat16)

    _out_proj_and_route(x, mix_ref, wout_ref, n2g_ref, rwt_ref, rb_ref, su_ref, cnt_ref,
                        x1_ref, h2_ref, meta_ref, gcol_ref)
    counts_ref[...] = cnt_ref[...]


def _dispatch_kernel(dest_ref, h2_ref, xs_in, xs_ref, sem):
    del xs_in
    unroll = 8

    def body(rb, carry):
        for uu in range(unroll):
            r = rb * unroll + uu
            pltpu.make_async_copy(h2_ref.at[r], xs_ref.at[dest_ref[0, 0, 2 * r]], sem).start()
            pltpu.make_async_copy(h2_ref.at[r], xs_ref.at[dest_ref[0, 0, 2 * r + 1]], sem).start()
        return carry

    lax.fori_loop(0, TM // unroll, body, 0)
    for _ in range(2):
        pltpu.make_async_copy(h2_ref, xs_ref.at[pl.ds(0, TM)], sem).wait()


def _expert_kernel(te_ref, nv_ref, xs_ref, w1_ref, w3_ref, w2_ref, out_ref):
    del te_ref
    i = pl.program_id(0)

    @pl.when(i < nv_ref[0])
    def _():
        x = jnp.concatenate([xs_ref[:, s, :] for s in range(ROW_TILES)], axis=1).astype(jnp.bfloat16)
        w1 = w1_ref[0].astype(jnp.bfloat16)
        w3 = w3_ref[0].astype(jnp.bfloat16)
        a = jnp.dot(x, w1, preferred_element_type=jnp.float32)
        b = jnp.dot(x, w3, preferred_element_type=jnp.float32)
        h = (a * jax.nn.sigmoid(a)) * b
        o = jnp.dot(h.astype(jnp.bfloat16), w2_ref[0].astype(jnp.bfloat16), preferred_element_type=jnp.float32)
        for s in range(ROW_TILES):
            out_ref[:, s, :] = o[:, s * LANES:(s + 1) * LANES]

    @pl.when(i >= nv_ref[0])
    def _():
        out_ref[...] = jnp.zeros_like(out_ref)


def _combine_kernel(dest_ref, x1_ref, gcol_ref, nfg_ref, out_hbm, y_ref, buf0, buf1, sem):
    unroll = 8

    def body(rb, carry):
        for uu in range(unroll):
            r = rb * unroll + uu
            pltpu.make_async_copy(out_hbm.at[dest_ref[0, 0, 2 * r]], buf0.at[r], sem).start()
            pltpu.make_async_copy(out_hbm.at[dest_ref[0, 0, 2 * r + 1]], buf1.at[r], sem).start()
        return carry

    lax.fori_loop(0, TM // unroll, body, 0)
    pltpu.make_async_copy(out_hbm.at[pl.ds(0, TM)], buf0, sem).wait()
    pltpu.make_async_copy(out_hbm.at[pl.ds(0, TM)], buf1, sem).wait()

    o0 = jnp.concatenate([buf0[:, s, :] for s in range(ROW_TILES)], axis=1)
    o1 = jnp.concatenate([buf1[:, s, :] for s in range(ROW_TILES)], axis=1)
    g = gcol_ref[...]
    moe = g[:, 0:1] * o0 + g[:, 1:2] * o1
    y_ref[...] = _rmsnorm(x1_ref[...] + moe, nfg_ref[...])


def _const_spec(shape):
    return pl.BlockSpec(shape, lambda *_: (0,) * len(shape))


def kernel(x_prompt, x_sample, state_pool, norm1_g, w_in, a_norm_g, a_norm_b, a_ws, a_bs, b_w, b_scale, w_out,
           norm2_g, r1_w, r1_b, r2_w, r2_b, exp_w1, exp_w3, exp_w2, normf_g):
    f32, bf16, i32 = jnp.float32, jnp.bfloat16, jnp.int32
    n_batch, seq, _ = x_prompt.shape
    dec_batch, dec_seq, _ = x_sample.shape
    assert norm1_g.shape[0] == 1 and seq % TM == 0 and TM % CHUNK == 0
    assert dec_seq * SEQ_BLK == TM and dec_batch % SEQ_BLK == 0 and dec_seq <= CHUNK
    t_prompt = n_batch * seq
    t_sample = dec_batch * dec_seq
    t_total = t_prompt + t_sample
    n_tok_tiles = t_total // TM
    n_prompt_tiles = t_prompt // TM
    n_sample_tiles = t_sample // TM
    seq_tiles = seq // TM
    n_assign = 2 * t_total
    n_exp_tiles = -(-(n_assign + N_EXPERTS * (TE - 1)) // TE)
    p_rows = n_exp_tiles * TE

    n1g = norm1_g[0][None, :]
    n2g = norm2_g[0][None, :]
    nfg = normf_g[None, :]
    win = w_in[0].astype(bf16)
    wout = w_out[0].astype(bf16)
    ang = a_norm_g[0][None, :]
    anb = a_norm_b[0][None, :]
    bw = b_w[0].astype(bf16)
    bscale = b_scale[0][None, :]
    abias = jnp.repeat(a_bs[0][:, :CHUNK].T, HEAD_DIM, axis=1)
    rwt = jnp.concatenate([r1_w[0].T, r2_w[0].transpose(0, 2, 1).reshape(N_EXPERTS, D_MODEL),
                           jnp.zeros((N_ROUTER_ROWS - N_GROUPS - N_EXPERTS, D_MODEL), f32)], axis=0)
    rbias = jnp.concatenate([r1_b[0], r2_b[0].reshape(-1),
                             jnp.zeros((N_ROUTER_ROWS - N_GROUPS - N_EXPERTS,), f32)])
    rb = jnp.broadcast_to(rbias[:, None], (N_ROUTER_ROWS, TM))
    su = (jnp.arange(TM)[:, None] < jnp.arange(TM)[None, :]).astype(bf16)

    cparams = lambda sem: pltpu.CompilerParams(dimension_semantics=sem, vmem_limit_bytes=VMEM_LIMIT)
    weight_specs = [
        _const_spec((1, D_MODEL)), _const_spec((D_MODEL, 3 * A_WIDTH)), _const_spec((1, A_WIDTH)),
        _const_spec((1, A_WIDTH)),
    ]
    tail_specs = [
        _const_spec((N_GROUPS, HEAD_DIM, HEAD_DIM)), _const_spec((1, B_WIDTH)), _const_spec((D_MODEL, D_MODEL)),
        _const_spec((1, D_MODEL)), _const_spec((N_ROUTER_ROWS, D_MODEL)), _const_spec((N_ROUTER_ROWS, TM)),
        _const_spec((TM, TM)),
    ]
    tok_shapes = [
        jax.ShapeDtypeStruct((t_total, D_MODEL), f32),
        jax.ShapeDtypeStruct((t_total, ROW_TILES, LANES), f32),
        jax.ShapeDtypeStruct((SUBLANES, t_total), i32),
        jax.ShapeDtypeStruct((t_total, LANES), f32),
    ]

    tile_p = lambda b, j: b * seq_tiles + j
    x1, h2, pstate, meta, gcol, counts_p = pl.pallas_call(
        _mixer_prompt_kernel,
        grid=(n_batch, seq_tiles),
        in_specs=[pl.BlockSpec((1, TM, D_MODEL), lambda b, j: (b, j, 0))] + weight_specs + [
            _const_spec((N_HEADS, CHUNK, CHUNK)), _const_spec((CHUNK, A_WIDTH))] + tail_specs,
        out_specs=[
            pl.BlockSpec((TM, D_MODEL), lambda b, j: (tile_p(b, j), 0)),
            pl.BlockSpec((TM, ROW_TILES, LANES), lambda b, j: (tile_p(b, j), 0, 0)),
            pl.BlockSpec((1, CARRY_ROWS, B_WIDTH), lambda b, j: (b, 0, 0)),
            pl.BlockSpec((SUBLANES, TM), lambda b, j: (0, tile_p(b, j))),
            pl.BlockSpec((TM, LANES), lambda b, j: (tile_p(b, j), 0)),
            _const_spec((N_EXPERTS, LANES)),
        ],
        out_shape=[tok_shapes[0], tok_shapes[1], jax.ShapeDtypeStruct((n_batch, CARRY_ROWS, B_WIDTH), f32),
                   tok_shapes[2], tok_shapes[3], jax.ShapeDtypeStruct((N_EXPERTS, LANES), f32)],
        scratch_shapes=[pltpu.VMEM((TM, D_MODEL), bf16), pltpu.VMEM((CARRY_ROWS, B_WIDTH), f32),
                        pltpu.VMEM((N_EXPERTS, LANES), f32)],
        compiler_params=cparams(("arbitrary", "arbitrary")),
        name="mixer_prompt",
    )(x_prompt, n1g, win, ang, anb, a_ws[0][:, :CHUNK, :CHUNK], abias, bw, bscale, wout, n2g, rwt, rb, su)

    xs_planes = (x_sample.reshape(n_sample_tiles, SEQ_BLK, dec_seq, D_MODEL)
                 .transpose(0, 2, 1, 3).reshape(t_sample, D_MODEL))
    st_planes = (state_pool[0].reshape(n_sample_tiles, SEQ_BLK, POOL_STATE, B_WIDTH).transpose(0, 2, 1, 3))
    aws_s = a_ws[0][:, :dec_seq, :dec_seq].reshape(-1)
    abs_s = a_bs[0][:, :dec_seq].reshape(-1)
    tile_s = lambda q, *_: n_prompt_tiles + q
    plane_shape = jax.ShapeDtypeStruct((n_sample_tiles, dec_seq, SEQ_BLK, A_WIDTH), f32)
    x1, h2, meta, gcol, pplanes, vplanes, counts = pl.pallas_call(
        _mixer_sample_kernel,
        grid_spec=pltpu.PrefetchScalarGridSpec(
            num_scalar_prefetch=2,
            grid=(n_sample_tiles,),
            in_specs=[pl.BlockSpec((TM, D_MODEL), lambda q, *_: (q, 0)),
                      pl.BlockSpec((1, POOL_STATE, SEQ_BLK, B_WIDTH), lambda q, *_: (q, 0, 0, 0)),
                      _const_spec((N_EXPERTS, LANES))] + weight_specs + tail_specs + [
                pl.BlockSpec(memory_space=pl.ANY)] * 4,
            out_specs=[
                pl.BlockSpec((TM, D_MODEL), lambda q, *_: (tile_s(q), 0)),
                pl.BlockSpec((TM, ROW_TILES, LANES), lambda q, *_: (tile_s(q), 0, 0)),
                pl.BlockSpec((SUBLANES, TM), lambda q, *_: (0, tile_s(q))),
                pl.BlockSpec((TM, LANES), lambda q, *_: (tile_s(q), 0)),
                pl.BlockSpec((1, dec_seq, SEQ_BLK, A_WIDTH), lambda q, *_: (q, 0, 0, 0)),
                pl.BlockSpec((1, dec_seq, SEQ_BLK, A_WIDTH), lambda q, *_: (q, 0, 0, 0)),
                _const_spec((N_EXPERTS, LANES)),
            ],
            scratch_shapes=[pltpu.VMEM((TM, D_MODEL), bf16), pltpu.VMEM((N_EXPERTS, LANES), f32)],
        ),
        out_shape=tok_shapes + [plane_shape, plane_shape, jax.ShapeDtypeStruct((N_EXPERTS, LANES), f32)],
        input_output_aliases={16: 0, 17: 1, 18: 2, 19: 3},
        compiler_params=cparams(("arbitrary",)),
        name="mixer_sample",
    )(aws_s, abs_s, xs_planes, st_planes, counts_p, n1g, win, ang, anb, bw, bscale, wout, n2g, rwt, rb, su,
      x1, h2, meta, gcol)

    cnt = counts[:, 0].astype(i32)
    padded = (cnt + (TE - 1)) // TE * TE
    pad_end = jnp.cumsum(padded)
    pad_start = pad_end - padded
    d0 = pad_start[meta[0]] + meta[2]
    d1 = pad_start[meta[1]] + meta[3]
    dest = jnp.stack([d0, d1], axis=-1).reshape(n_tok_tiles, 1, 2 * TM)
    n_valid = (pad_end[-1] // TE).astype(i32)
    tile_ids = jnp.minimum(jnp.arange(n_exp_tiles, dtype=i32), n_valid - 1)
    tile_e = jnp.sum((pad_end[None, :] <= (tile_ids * TE)[:, None]).astype(i32), axis=1)
    tile_e = jnp.minimum(tile_e, N_EXPERTS - 1)

    dest_spec = pl.BlockSpec((1, 1, 2 * TM), lambda i: (i, 0, 0), memory_space=pltpu.SMEM)

    xs = pl.pallas_call(
        _dispatch_kernel,
        grid=(n_tok_tiles,),
        in_specs=[dest_spec, pl.BlockSpec((TM, ROW_TILES, LANES), lambda i: (i, 0, 0)),
                  pl.BlockSpec(memory_space=pl.ANY)],
        out_specs=pl.BlockSpec(memory_space=pl.ANY),
        out_shape=jax.ShapeDtypeStruct((p_rows, ROW_TILES, LANES), f32),
        scratch_shapes=[pltpu.SemaphoreType.DMA(())],
        input_output_aliases={2: 0},
        compiler_params=cparams(("arbitrary",)),
        name="moe_dispatch",
    )(dest, h2, jnp.zeros((p_rows, ROW_TILES, LANES), f32))

    row_blk = lambda i, te, nv: (jnp.minimum(i, nv[0] - 1), 0, 0)
    w_blk = lambda i, te, nv: (te[i], 0, 0)
    out_sorted = pl.pallas_call(
        _expert_kernel,
        grid_spec=pltpu.PrefetchScalarGridSpec(
            num_scalar_prefetch=2,
            grid=(n_exp_tiles,),
            in_specs=[pl.BlockSpec((TE, ROW_TILES, LANES), row_blk),
                      pl.BlockSpec((1, D_MODEL, D_EXPERT), w_blk),
                      pl.BlockSpec((1, D_MODEL, D_EXPERT), w_blk),
                      pl.BlockSpec((1, D_EXPERT, D_MODEL), w_blk)],
            out_specs=pl.BlockSpec((TE, ROW_TILES, LANES), lambda i, te, nv: (i, 0, 0)),
        ),
        out_shape=jax.ShapeDtypeStruct((p_rows, ROW_TILES, LANES), f32),
        compiler_params=cparams(("arbitrary",)),
        name="moe_experts",
    )(tile_e, n_valid[None], xs, exp_w1[0], exp_w3[0], exp_w2[0])

    y = pl.pallas_call(
        _combine_kernel,
        grid=(n_tok_tiles,),
        in_specs=[dest_spec,
                  pl.BlockSpec((TM, D_MODEL), lambda i: (i, 0)),
                  pl.BlockSpec((TM, LANES), lambda i: (i, 0)),
                  _const_spec((1, D_MODEL)),
                  pl.BlockSpec(memory_space=pl.ANY)],
        out_specs=pl.BlockSpec((TM, D_MODEL), lambda i: (i, 0)),
        out_shape=jax.ShapeDtypeStruct((t_total, D_MODEL), f32),
        scratch_shapes=[pltpu.VMEM((TM, ROW_TILES, LANES), f32), pltpu.VMEM((TM, ROW_TILES, LANES), f32),
                        pltpu.SemaphoreType.DMA(())],
        compiler_params=cparams(("arbitrary",)),
        name="moe_combine",
    )(dest, x1, gcol, nfg, out_sorted)

    unplane = lambda a: a.transpose(0, 2, 1, 3).reshape(dec_batch, dec_seq, a.shape[-1])
    y_prompt = y[:t_prompt].reshape(n_batch, seq, D_MODEL)
    y_sample = unplane(y[t_prompt:].reshape(n_sample_tiles, dec_seq, SEQ_BLK, D_MODEL))
    pool_state_prompt = pstate[None, :, CARRY_ROWS - POOL_STATE:, :]
    p_s = unplane(pplanes)
    pool_state_sample = jnp.concatenate([state_pool[0], p_s], axis=1)[None, :, -POOL_STATE:, :]
    chunk_v_sample = unplane(vplanes)[None]
    return (y_prompt, y_sample, pool_state_prompt, pool_state_sample, chunk_v_sample)
```

```python
import functools
import math

import jax
import jax.numpy as jnp
from jax import lax
from jax.experimental import pallas as pl
from jax.experimental.pallas import tpu as pltpu

D_MODEL = 1024
A_WIDTH = 512
B_WIDTH = 512
N_HEADS = 4
HEAD_DIM = 128
CHUNK = 128
POOL_WINDOWS = (2, 4, 8, 16)
POOL_STATE = 15
N_GROUPS = 4
EXPERTS_PER_GROUP = 8
N_EXPERTS = 32
TOP_K = 2
D_EXPERT = 512
EPS = 1e-6

SUBLANES = 8
LANES = 128
ROW_TILES = D_MODEL // LANES

TM = 256
TE = 256
SEQ_BLK = 32
N_ROUTER_ROWS = 40
CARRY_ROWS = 16
DMA_UNROLL = 8
VMEM_LIMIT = 48 * 1024 * 1024

_INV_SQRT2 = 1.0 / math.sqrt(2.0)


def _rmsnorm(x, g):
    r = lax.rsqrt(jnp.mean(x * x, axis=-1, keepdims=True) + EPS)
    return (x * r) * g


def _gelu(x):
    return 0.5 * x * (1.0 + lax.erf(x * _INV_SQRT2))


def _layernorm(x, g, b):
    mu = jnp.mean(x, axis=-1, keepdims=True)
    xc = x - mu
    var = jnp.mean(xc * xc, axis=-1, keepdims=True)
    return (xc * lax.rsqrt(var + EPS)) * g + b


def _row_slab(ref, s, n):
    return ref[pl.ds(s, n, stride=ROW_TILES), :]


def _pool_project(pooled, g, bw_ref, bscale_ref):
    lo, hi = g * HEAD_DIM, (g + 1) * HEAD_DIM
    hb = jnp.dot(pooled.astype(jnp.bfloat16), bw_ref[g], preferred_element_type=jnp.float32)
    return hb * bscale_ref[:, lo:hi]


def _prompt_mixers(j, u, v, p, aws_ref, abias_ref, bw_ref, bscale_ref, mix_ref, pcarry_ref, pstate_ref):
    tri = (lax.broadcasted_iota(jnp.int32, (CHUNK, CHUNK), 0)
           >= lax.broadcasted_iota(jnp.int32, (CHUNK, CHUNK), 1))
    vb = v.astype(jnp.bfloat16)
    for hd in range(N_HEADS):
        lo, hi = hd * HEAD_DIM, (hd + 1) * HEAD_DIM
        w = jnp.where(tri, aws_ref[hd], 0.0).astype(jnp.bfloat16)
        for c in range(TM // CHUNK):
            r0, r1 = c * CHUNK, (c + 1) * CHUNK
            z = jnp.dot(w, vb[r0:r1, lo:hi], preferred_element_type=jnp.float32) + abias_ref[:, lo:hi]
            mix_ref[r0:r1, lo:hi] = (u[r0:r1, lo:hi] * z).astype(jnp.bfloat16)

    pos = j * TM + lax.broadcasted_iota(jnp.int32, (TM, LANES), 0)
    for g, w in enumerate(POOL_WINDOWS):
        lo, hi = g * HEAD_DIM, (g + 1) * HEAD_DIM
        pg = p[:, lo:hi]
        acc = jnp.concatenate([pcarry_ref[:, lo:hi], pg], axis=0)
        shift = 1
        while shift < w:
            acc = acc + pltpu.roll(acc, shift, 0)
            shift *= 2
        cnt = jnp.minimum(pos + 1, w).astype(jnp.float32)
        pooled = acc[CARRY_ROWS:, :] / cnt - pg
        mix_ref[:, A_WIDTH + lo:A_WIDTH + hi] = _pool_project(pooled, g, bw_ref, bscale_ref).astype(jnp.bfloat16)
    tail = p[TM - CARRY_ROWS:, :]
    pcarry_ref[...] = tail
    pstate_ref[0] = tail


def _sample_mixers(u, v, p, aws_ref, abs_ref, st_ref, bw_ref, bscale_ref, mix_ref, pp_ref, vp_ref):
    n_pos = TM // SEQ_BLK
    for i in range(n_pos):
        vp_ref[0, i] = v[i * SEQ_BLK:(i + 1) * SEQ_BLK, :]
        pp_ref[0, i] = p[i * SEQ_BLK:(i + 1) * SEQ_BLK, :]

    for hd in range(N_HEADS):
        lo, hi = hd * HEAD_DIM, (hd + 1) * HEAD_DIM
        vplanes = [v[s * SEQ_BLK:(s + 1) * SEQ_BLK, lo:hi] for s in range(n_pos)]
        for i in range(n_pos):
            z = vplanes[0] * aws_ref[hd * 64 + i * 8]
            for s in range(1, i + 1):
                z = z + vplanes[s] * aws_ref[hd * 64 + i * 8 + s]
            z = z + abs_ref[hd * 8 + i]
            r0, r1 = i * SEQ_BLK, (i + 1) * SEQ_BLK
            mix_ref[r0:r1, lo:hi] = (u[r0:r1, lo:hi] * z).astype(jnp.bfloat16)

    for g, w in enumerate(POOL_WINDOWS):
        lo, hi = g * HEAD_DIM, (g + 1) * HEAD_DIM
        planes = [st_ref[0, k, :, lo:hi] for k in range(POOL_STATE)]
        planes += [p[i * SEQ_BLK:(i + 1) * SEQ_BLK, lo:hi] for i in range(n_pos)]
        pooled = []
        for i in range(n_pos):
            top = POOL_STATE + i
            s = planes[top - w + 1]
            for k in range(top - w + 2, top + 1):
                s = s + planes[k]
            pooled.append(s * (1.0 / w) - planes[top])
        pooled = jnp.concatenate(pooled, axis=0)
        mix_ref[:, A_WIDTH + lo:A_WIDTH + hi] = _pool_project(pooled, g, bw_ref, bscale_ref).astype(jnp.bfloat16)


def _route(h2, rwt_ref, rb_ref, su_ref, cnt_ref, meta_ref, gcol_ref):
    tm = h2.shape[0]
    lt = lax.dot_general(rwt_ref[...], h2, (((1,), (1,)), ((), ())),
                         precision=lax.Precision.HIGHEST,
                         preferred_element_type=jnp.float32) + rb_ref[...]
    row = lambda i: lt[i:i + 1, :]
    l1 = [row(i) for i in range(N_GROUPS)]
    m1 = jnp.maximum(jnp.maximum(l1[0], l1[1]), jnp.maximum(l1[2], l1[3]))
    grp = jnp.where(l1[0] == m1, 0, jnp.where(l1[1] == m1, 1, jnp.where(l1[2] == m1, 2, 3)))
    se = (jnp.exp(l1[0] - m1) + jnp.exp(l1[1] - m1)) + (jnp.exp(l1[2] - m1) + jnp.exp(l1[3] - m1))
    pg = 1.0 / se
    l2 = []
    for e in range(EXPERTS_PER_GROUP):
        c = [row(N_GROUPS + g * EXPERTS_PER_GROUP + e) for g in range(N_GROUPS)]
        l2.append(jnp.where(grp == 0, c[0], jnp.where(grp == 1, c[1], jnp.where(grp == 2, c[2], c[3]))))
    v0 = functools.reduce(jnp.maximum, l2)
    i0 = jnp.full_like(grp, EXPERTS_PER_GROUP - 1)
    for e in range(EXPERTS_PER_GROUP - 2, -1, -1):
        i0 = jnp.where(l2[e] == v0, e, i0)
    neg = jnp.float32(-jnp.inf)
    l2m = [jnp.where(i0 == e, neg, l2[e]) for e in range(EXPERTS_PER_GROUP)]
    v1 = functools.reduce(jnp.maximum, l2m)
    i1 = jnp.full_like(grp, EXPERTS_PER_GROUP - 1)
    for e in range(EXPERTS_PER_GROUP - 2, -1, -1):
        i1 = jnp.where((l2m[e] == v1) & (i0 != e), e, i1)
    d = jnp.exp(v1 - v0)
    g0 = pg / (1.0 + d)
    g1 = (pg * d) / (1.0 + d)
    e0 = grp * EXPERTS_PER_GROUP + i0
    e1 = grp * EXPERTS_PER_GROUP + i1

    eiota = lax.broadcasted_iota(jnp.int32, (N_EXPERTS, tm), 0)
    hit0 = eiota == e0
    hit1 = eiota == e1
    onehot = (hit0 | hit1).astype(jnp.bfloat16)
    prefix = jnp.dot(onehot, su_ref[...], preferred_element_type=jnp.float32)
    carry = cnt_ref[...]
    base = prefix + jnp.concatenate([carry] * (tm // LANES), axis=1)
    r0 = jnp.sum(jnp.where(hit0, base, 0.0), axis=0, keepdims=True)
    r1 = jnp.sum(jnp.where(hit1, base, 0.0), axis=0, keepdims=True)
    ones = jnp.ones((tm, LANES), jnp.bfloat16)
    cnt_ref[...] = carry + jnp.dot(onehot, ones, preferred_element_type=jnp.float32)

    meta_ref[0:1, :] = e0
    meta_ref[1:2, :] = e1
    meta_ref[2:3, :] = r0.astype(jnp.int32)
    meta_ref[3:4, :] = r1.astype(jnp.int32)
    meta_ref[4:8, :] = jnp.zeros((4, tm), jnp.int32)

    giota = lax.broadcasted_iota(jnp.int32, (LANES, tm), 0)
    gpad = jnp.where(giota == 0, g0, jnp.where(giota == 1, g1, 0.0))
    gcol_ref[...] = gpad.T


def _mixer_kernel(aws_s_ref, abs_s_ref,
                  xp_ref, xs_ref, st_ref, n1g_ref, win_ref, ang_ref, anb_ref, aws_ref, abias_ref, bw_ref, bscale_ref,
                  wout_ref, n2g_ref, rwt_ref, rb_ref, su_ref,
                  x1_ref, h2_ref, meta_ref, gcol_ref, counts_ref, pstate_ref, pp_ref, vp_ref,
                  mix_ref, pcarry_ref, cnt_ref, *, n_prompt_tiles, seq_tiles):
    i = pl.program_id(0)
    is_prompt = i < n_prompt_tiles
    j = i % seq_tiles

    @pl.when(i == 0)
    def _():
        cnt_ref[...] = jnp.zeros_like(cnt_ref)

    @pl.when(is_prompt & (j == 0))
    def _():
        pcarry_ref[...] = jnp.zeros_like(pcarry_ref)

    x = jnp.where(is_prompt, xp_ref[0], xs_ref[...])
    h = _rmsnorm(x, n1g_ref[...]).astype(jnp.bfloat16)
    proj = jnp.dot(h, win_ref[...], preferred_element_type=jnp.float32)
    uv = _gelu(proj[:, : 2 * A_WIDTH])
    u = uv[:, :A_WIDTH]
    v = _layernorm(uv[:, A_WIDTH:], ang_ref[...], anb_ref[...])
    p = proj[:, 2 * A_WIDTH:]

    @pl.when(is_prompt)
    def _():
        _prompt_mixers(j, u, v, p, aws_ref, abias_ref, bw_ref, bscale_ref, mix_ref, pcarry_ref, pstate_ref)

    @pl.when(jnp.logical_not(is_prompt))
    def _():
        _sample_mixers(u, v, p, aws_s_ref, abs_s_ref, st_ref, bw_ref, bscale_ref, mix_ref, pp_ref, vp_ref)

    x1 = x + jnp.dot(mix_ref[...], wout_ref[...], preferred_element_type=jnp.float32)
    x1_ref[...] = x1
    h2 = _rmsnorm(x1, n2g_ref[...])
    for s in range(ROW_TILES):
        h2_ref[pl.ds(s, TM, stride=ROW_TILES), :] = h2[:, s * LANES:(s + 1) * LANES]
    _route(h2, rwt_ref, rb_ref, su_ref, cnt_ref, meta_ref, gcol_ref)
    counts_ref[...] = cnt_ref[...]


def _expert_kernel(te_ref, nv_ref, nrows_ref,
                   asg_ref, asg_next_ref,
                   h2_hbm, w1_ref, w3_ref, w2_ref,
                   out_hbm,
                   xbuf, obuf, w1b, w3b, w2b, gsem, ssem, *, plane_rows):
    i = pl.program_id(0)
    nv = nv_ref[0]
    slot = i % 2
    slab = lambda r: pl.ds(pl.multiple_of(r * ROW_TILES, ROW_TILES), ROW_TILES)

    def issue_gather(idx_ref, dst_slot):
        def body(rb, carry):
            for uu in range(DMA_UNROLL):
                r = rb * DMA_UNROLL + uu
                tok = jnp.maximum(idx_ref[0, 0, r], 0) >> 1
                pltpu.make_async_copy(h2_hbm.at[slab(tok), :], xbuf.at[dst_slot, slab(r), :],
                                      gsem.at[dst_slot]).start()
            return carry
        lax.fori_loop(0, TE // DMA_UNROLL, body, 0)

    def scatter_copy(r, a, src_slot):
        dst = (a & 1) * plane_rows + (a >> 1) * ROW_TILES
        return pltpu.make_async_copy(obuf.at[src_slot, slab(r), :],
                                     out_hbm.at[pl.ds(pl.multiple_of(dst, ROW_TILES), ROW_TILES), :],
                                     ssem.at[src_slot])

    def wait_scatter(tile, src_slot):
        n = nrows_ref[tile] * ROW_TILES
        pltpu.make_async_copy(obuf.at[src_slot, pl.ds(0, n), :], out_hbm.at[pl.ds(0, n), :],
                              ssem.at[src_slot]).wait()

    @pl.when(i == 0)
    def _():
        issue_gather(asg_ref, 0)

    @pl.when(i + 1 < nv)
    def _():
        issue_gather(asg_next_ref, 1 - slot)

    @pl.when(i < nv)
    def _():
        @pl.when((i == 0) | (te_ref[i] != te_ref[jnp.maximum(i - 1, 0)]))
        def _():
            w1b[...] = w1_ref[0].astype(jnp.bfloat16)
            w3b[...] = w3_ref[0].astype(jnp.bfloat16)
            w2b[...] = w2_ref[0].astype(jnp.bfloat16)

        pltpu.make_async_copy(h2_hbm.at[pl.ds(0, TE * ROW_TILES), :], xbuf.at[slot], gsem.at[slot]).wait()

        @pl.when(i >= 2)
        def _():
            wait_scatter(i - 2, slot)

        xs = xbuf.at[slot]
        x = jnp.concatenate([_row_slab(xs, s, TE) for s in range(ROW_TILES)], axis=1).astype(jnp.bfloat16)
        a = jnp.dot(x, w1b[...], preferred_element_type=jnp.float32)
        b = jnp.dot(x, w3b[...], preferred_element_type=jnp.float32)
        h = (a * jax.nn.sigmoid(a)) * b
        o = jnp.dot(h.astype(jnp.bfloat16), w2b[...], preferred_element_type=jnp.float32)
        os = obuf.at[slot]
        for s in range(ROW_TILES):
            os[pl.ds(s, TE, stride=ROW_TILES), :] = o[:, s * LANES:(s + 1) * LANES]

        n = nrows_ref[i]
        n_full = n // DMA_UNROLL

        def body(rb, carry):
            for uu in range(DMA_UNROLL):
                r = rb * DMA_UNROLL + uu
                scatter_copy(r, asg_ref[0, 0, r], slot).start()
            return carry
        lax.fori_loop(0, n_full, body, 0)
        for uu in range(DMA_UNROLL - 1):
            r = n_full * DMA_UNROLL + uu

            @pl.when(r < n)
            def _():
                scatter_copy(r, asg_ref[0, 0, r], slot).start()

        @pl.when(i == nv - 1)
        def _():
            @pl.when(i >= 1)
            def _():
                wait_scatter(i - 1, 1 - slot)
            wait_scatter(i, slot)


def _combine_kernel(x1_ref, gcol_ref, nfg_ref, o0_ref, o1_ref, yp_ref, ys_ref, *, n_prompt_tiles):
    i = pl.program_id(0)
    o0 = jnp.concatenate([_row_slab(o0_ref, s, TM) for s in range(ROW_TILES)], axis=1)
    o1 = jnp.concatenate([_row_slab(o1_ref, s, TM) for s in range(ROW_TILES)], axis=1)
    g = gcol_ref[...]
    moe = g[:, 0:1] * o0 + g[:, 1:2] * o1
    y = _rmsnorm(x1_ref[...] + moe, nfg_ref[...])

    @pl.when(i < n_prompt_tiles)
    def _():
        yp_ref[...] = y

    @pl.when(i >= n_prompt_tiles)
    def _():
        ys_ref[...] = y


def _const_spec(shape):
    return pl.BlockSpec(shape, lambda *_: (0,) * len(shape))


def kernel(x_prompt, x_sample, state_pool, norm1_g, w_in, a_norm_g, a_norm_b, a_ws, a_bs, b_w, b_scale, w_out,
           norm2_g, r1_w, r1_b, r2_w, r2_b, exp_w1, exp_w3, exp_w2, normf_g):
    f32, bf16, i32 = jnp.float32, jnp.bfloat16, jnp.int32
    n_batch, seq, _ = x_prompt.shape
    dec_batch, dec_seq, _ = x_sample.shape
    assert norm1_g.shape[0] == 1 and seq % TM == 0 and TM % CHUNK == 0
    assert dec_seq * SEQ_BLK == TM and dec_batch % SEQ_BLK == 0 and dec_seq <= CHUNK
    t_prompt = n_batch * seq
    t_sample = dec_batch * dec_seq
    t_total = t_prompt + t_sample
    n_tok_tiles = t_total // TM
    n_prompt_tiles = t_prompt // TM
    n_sample_tiles = t_sample // TM
    seq_tiles = seq // TM
    n_assign = TOP_K * t_total
    n_exp_tiles = -(-(n_assign + N_EXPERTS * (TE - 1)) // TE)
    p_rows = n_exp_tiles * TE
    plane_rows = t_total * ROW_TILES

    n1g = norm1_g[0][None, :]
    n2g = norm2_g[0][None, :]
    nfg = normf_g[None, :]
    win = w_in[0].astype(bf16)
    wout = w_out[0].astype(bf16)
    ang = a_norm_g[0][None, :]
    anb = a_norm_b[0][None, :]
    bw = b_w[0].astype(bf16)
    bscale = b_scale[0][None, :]
    abias = jnp.repeat(a_bs[0][:, :CHUNK].T, HEAD_DIM, axis=1)
    rwt = jnp.concatenate([r1_w[0].T, r2_w[0].transpose(0, 2, 1).reshape(N_EXPERTS, D_MODEL),
                           jnp.zeros((N_ROUTER_ROWS - N_GROUPS - N_EXPERTS, D_MODEL), f32)], axis=0)
    rbias = jnp.concatenate([r1_b[0], r2_b[0].reshape(-1),
                             jnp.zeros((N_ROUTER_ROWS - N_GROUPS - N_EXPERTS,), f32)])
    rb = jnp.broadcast_to(rbias[:, None], (N_ROUTER_ROWS, TM))
    su = (jnp.arange(TM)[:, None] < jnp.arange(TM)[None, :]).astype(bf16)
    xs_planes = (x_sample.reshape(n_sample_tiles, SEQ_BLK, dec_seq, D_MODEL)
                 .transpose(0, 2, 1, 3).reshape(t_sample, D_MODEL))
    st_planes = state_pool[0].reshape(n_sample_tiles, SEQ_BLK, POOL_STATE, B_WIDTH).transpose(0, 2, 1, 3)
    aws_s = a_ws[0][:, :dec_seq, :dec_seq].reshape(-1)
    abs_s = a_bs[0][:, :dec_seq].reshape(-1)

    cparams = pltpu.CompilerParams(dimension_semantics=("arbitrary",), vmem_limit_bytes=VMEM_LIMIT)

    p_tile = lambda i: jnp.minimum(i, n_prompt_tiles - 1)
    s_tile = lambda i: jnp.maximum(i - n_prompt_tiles, 0)
    plane_shape = jax.ShapeDtypeStruct((n_sample_tiles, dec_seq, SEQ_BLK, A_WIDTH), f32)
    x1, h2, meta, gcol, counts, pstate, pplanes, vplanes = pl.pallas_call(
        functools.partial(_mixer_kernel, n_prompt_tiles=n_prompt_tiles, seq_tiles=seq_tiles),
        grid_spec=pltpu.PrefetchScalarGridSpec(
            num_scalar_prefetch=2,
            grid=(n_tok_tiles,),
            in_specs=[
                pl.BlockSpec((1, TM, D_MODEL), lambda i, *_: (p_tile(i) // seq_tiles, p_tile(i) % seq_tiles, 0)),
                pl.BlockSpec((TM, D_MODEL), lambda i, *_: (s_tile(i), 0)),
                pl.BlockSpec((1, POOL_STATE, SEQ_BLK, B_WIDTH), lambda i, *_: (s_tile(i), 0, 0, 0)),
                _const_spec((1, D_MODEL)), _const_spec((D_MODEL, 3 * A_WIDTH)), _const_spec((1, A_WIDTH)),
                _const_spec((1, A_WIDTH)), _const_spec((N_HEADS, CHUNK, CHUNK)), _const_spec((CHUNK, A_WIDTH)),
                _const_spec((N_GROUPS, HEAD_DIM, HEAD_DIM)), _const_spec((1, B_WIDTH)),
                _const_spec((D_MODEL, D_MODEL)), _const_spec((1, D_MODEL)), _const_spec((N_ROUTER_ROWS, D_MODEL)),
                _const_spec((N_ROUTER_ROWS, TM)), _const_spec((TM, TM)),
            ],
            out_specs=[
                pl.BlockSpec((TM, D_MODEL), lambda i, *_: (i, 0)),
                pl.BlockSpec((TM * ROW_TILES, LANES), lambda i, *_: (i, 0)),
                pl.BlockSpec((SUBLANES, TM), lambda i, *_: (0, i)),
                pl.BlockSpec((TM, LANES), lambda i, *_: (i, 0)),
                _const_spec((N_EXPERTS, LANES)),
                pl.BlockSpec((1, CARRY_ROWS, B_WIDTH), lambda i, *_: (p_tile(i) // seq_tiles, 0, 0)),
                pl.BlockSpec((1, dec_seq, SEQ_BLK, A_WIDTH), lambda i, *_: (s_tile(i), 0, 0, 0)),
                pl.BlockSpec((1, dec_seq, SEQ_BLK, A_WIDTH), lambda i, *_: (s_tile(i), 0, 0, 0)),
            ],
            scratch_shapes=[pltpu.VMEM((TM, D_MODEL), bf16), pltpu.VMEM((CARRY_ROWS, B_WIDTH), f32),
                            pltpu.VMEM((N_EXPERTS, LANES), f32)],
        ),
        out_shape=[
            jax.ShapeDtypeStruct((t_total, D_MODEL), f32),
            jax.ShapeDtypeStruct((plane_rows, LANES), f32),
            jax.ShapeDtypeStruct((SUBLANES, t_total), i32),
            jax.ShapeDtypeStruct((t_total, LANES), f32),
            jax.ShapeDtypeStruct((N_EXPERTS, LANES), f32),
            jax.ShapeDtypeStruct((n_batch, CARRY_ROWS, B_WIDTH), f32),
            plane_shape, plane_shape,
        ],
        compiler_params=cparams,
        name="mixer",
    )(aws_s, abs_s, x_prompt, xs_planes, st_planes, n1g, win, ang, anb, a_ws[0][:, :CHUNK, :CHUNK], abias, bw,
      bscale, wout, n2g, rwt, rb, su)

    cnt = counts[:, 0].astype(i32)
    padded = (cnt + (TE - 1)) // TE * TE
    pad_end = jnp.cumsum(padded)
    pad_start = pad_end - padded
    dest = jnp.stack([pad_start[meta[0]] + meta[2], pad_start[meta[1]] + meta[3]], axis=-1).reshape(-1)
    asg = jnp.full((p_rows,), -1, i32).at[dest].set(jnp.arange(n_assign, dtype=i32), unique_indices=True)
    asg = asg.reshape(n_exp_tiles, 1, TE)
    n_valid = jnp.maximum(pad_end[-1] // TE, 1).astype(i32)
    tile_ids = jnp.minimum(jnp.arange(n_exp_tiles, dtype=i32), n_valid - 1)
    tile_e = jnp.sum((pad_end[None, :] <= (tile_ids * TE)[:, None]).astype(i32), axis=1)
    tile_e = jnp.minimum(tile_e, N_EXPERTS - 1)
    tile_rows = jnp.clip(cnt[tile_e] - (tile_ids * TE - pad_start[tile_e]), 0, TE).astype(i32)

    cur_blk = lambda i, te, nv, nr: (jnp.minimum(i, nv[0] - 1), 0, 0)
    nxt_blk = lambda i, te, nv, nr: (jnp.minimum(i + 1, nv[0] - 1), 0, 0)
    w_blk = lambda i, te, nv, nr: (te[i], 0, 0)
    smem_blk = lambda imap: pl.BlockSpec((1, 1, TE), imap, memory_space=pltpu.SMEM)
    row_buf = pltpu.VMEM((2, TE * ROW_TILES, LANES), f32)
    out_tok = pl.pallas_call(
        functools.partial(_expert_kernel, plane_rows=plane_rows),
        grid_spec=pltpu.PrefetchScalarGridSpec(
            num_scalar_prefetch=3,
            grid=(n_exp_tiles,),
            in_specs=[smem_blk(cur_blk), smem_blk(nxt_blk),
                      pl.BlockSpec(memory_space=pl.ANY),
                      pl.BlockSpec((1, D_MODEL, D_EXPERT), w_blk),
                      pl.BlockSpec((1, D_MODEL, D_EXPERT), w_blk),
                      pl.BlockSpec((1, D_EXPERT, D_MODEL), w_blk)],
            out_specs=pl.BlockSpec(memory_space=pl.ANY),
            scratch_shapes=[row_buf, row_buf,
                            pltpu.VMEM((D_MODEL, D_EXPERT), bf16), pltpu.VMEM((D_MODEL, D_EXPERT), bf16),
                            pltpu.VMEM((D_EXPERT, D_MODEL), bf16),
                            pltpu.SemaphoreType.DMA((2,)), pltpu.SemaphoreType.DMA((2,))],
        ),
        out_shape=jax.ShapeDtypeStruct((TOP_K * plane_rows, LANES), f32),
        compiler_params=cparams,
        name="moe_experts",
    )(tile_e, n_valid[None], tile_rows, asg, asg, h2, exp_w1[0], exp_w3[0], exp_w2[0])

    y_prompt, y_sample = pl.pallas_call(
        functools.partial(_combine_kernel, n_prompt_tiles=n_prompt_tiles),
        grid=(n_tok_tiles,),
        in_specs=[pl.BlockSpec((TM, D_MODEL), lambda i: (i, 0)),
                  pl.BlockSpec((TM, LANES), lambda i: (i, 0)),
                  _const_spec((1, D_MODEL)),
                  pl.BlockSpec((TM * ROW_TILES, LANES), lambda i: (i, 0)),
                  pl.BlockSpec((TM * ROW_TILES, LANES), lambda i: (n_tok_tiles + i, 0))],
        out_specs=[pl.BlockSpec((TM, D_MODEL), lambda i: (p_tile(i), 0)),
                   pl.BlockSpec((TM, D_MODEL), lambda i: (s_tile(i), 0))],
        out_shape=[jax.ShapeDtypeStruct((t_prompt, D_MODEL), f32), jax.ShapeDtypeStruct((t_sample, D_MODEL), f32)],
        compiler_params=cparams,
        name="moe_combine",
    )(x1, gcol, nfg, out_tok, out_tok)

    unplane = lambda a: a.transpose(0, 2, 1, 3).reshape(dec_batch, dec_seq, a.shape[-1])
    y_prompt = y_prompt.reshape(n_batch, seq, D_MODEL)
    y_sample = unplane(y_sample.reshape(n_sample_tiles, dec_seq, SEQ_BLK, D_MODEL))
    pool_state_prompt = pstate[None, :, CARRY_ROWS - POOL_STATE:, :]
    p_s = unplane(pplanes)
    pool_state_sample = jnp.concatenate([state_pool[0], p_s], axis=1)[None, :, -POOL_STATE:, :]
    chunk_v_sample = unplane(vplanes)[None]
    return (y_prompt, y_sample, pool_state_prompt, pool_state_sample, chunk_v_sample)
```

```python
import functools
import math

import jax
import jax.numpy as jnp
from jax import lax
from jax.experimental import pallas as pl
from jax.experimental.pallas import tpu as pltpu

D_MODEL = 1024
A_WIDTH = 512
B_WIDTH = 512
N_HEADS = 4
HEAD_DIM = 128
CHUNK = 128
POOL_WINDOWS = (2, 4, 8, 16)
POOL_STATE = 15
N_GROUPS = 4
EXPERTS_PER_GROUP = 8
N_EXPERTS = 32
TOP_K = 2
D_EXPERT = 512
EPS = 1e-6

SUBLANES = 8
LANES = 128
ROW_TILES = D_MODEL // LANES

TM = 256
TE = 256
SEQ_BLK = 32
N_ROUTER_ROWS = 40
CARRY_ROWS = 16
DMA_UNROLL = 8
VMEM_LIMIT = 48 * 1024 * 1024

_INV_SQRT2 = 1.0 / math.sqrt(2.0)


def _rmsnorm(x, g):
    r = lax.rsqrt(jnp.mean(x * x, axis=-1, keepdims=True) + EPS)
    return (x * r) * g


def _gelu(x):
    return 0.5 * x * (1.0 + lax.erf(x * _INV_SQRT2))


def _layernorm(x, g, b):
    mu = jnp.mean(x, axis=-1, keepdims=True)
    xc = x - mu
    var = jnp.mean(xc * xc, axis=-1, keepdims=True)
    return (xc * lax.rsqrt(var + EPS)) * g + b


def _row_slab(ref, s, n):
    return ref[pl.ds(s, n, stride=ROW_TILES), :]


def _pool_project(pooled, g, bw_ref, bscale_ref):
    lo, hi = g * HEAD_DIM, (g + 1) * HEAD_DIM
    hb = jnp.dot(pooled.astype(jnp.bfloat16), bw_ref[g], preferred_element_type=jnp.float32)
    return hb * bscale_ref[:, lo:hi]


def _prompt_mixers(j, u, v, p, aws_ref, abias_ref, bw_ref, bscale_ref, mix_ref, pcarry_ref, pstate_ref):
    tri = (lax.broadcasted_iota(jnp.int32, (CHUNK, CHUNK), 0)
           >= lax.broadcasted_iota(jnp.int32, (CHUNK, CHUNK), 1))
    vb = v.astype(jnp.bfloat16)
    for hd in range(N_HEADS):
        lo, hi = hd * HEAD_DIM, (hd + 1) * HEAD_DIM
        w = jnp.where(tri, aws_ref[hd], 0.0).astype(jnp.bfloat16)
        for c in range(TM // CHUNK):
            r0, r1 = c * CHUNK, (c + 1) * CHUNK
            z = jnp.dot(w, vb[r0:r1, lo:hi], preferred_element_type=jnp.float32) + abias_ref[:, lo:hi]
            mix_ref[r0:r1, lo:hi] = (u[r0:r1, lo:hi] * z).astype(jnp.bfloat16)

    pos = j * TM + lax.broadcasted_iota(jnp.int32, (TM, LANES), 0)
    for g, w in enumerate(POOL_WINDOWS):
        lo, hi = g * HEAD_DIM, (g + 1) * HEAD_DIM
        pg = p[:, lo:hi]
        acc = jnp.concatenate([pcarry_ref[:, lo:hi], pg], axis=0)
        shift = 1
        while shift < w:
            acc = acc + pltpu.roll(acc, shift, 0)
            shift *= 2
        cnt = jnp.minimum(pos + 1, w).astype(jnp.float32)
        pooled = acc[CARRY_ROWS:, :] / cnt - pg
        mix_ref[:, A_WIDTH + lo:A_WIDTH + hi] = _pool_project(pooled, g, bw_ref, bscale_ref).astype(jnp.bfloat16)
    tail = p[TM - CARRY_ROWS:, :]
    pcarry_ref[...] = tail
    pstate_ref[0] = tail


def _sample_mixers(u, v, p, aws_ref, abs_ref, st_ref, bw_ref, bscale_ref, mix_ref, pp_ref, vp_ref):
    n_pos = TM // SEQ_BLK
    for i in range(n_pos):
        vp_ref[0, i] = v[i * SEQ_BLK:(i + 1) * SEQ_BLK, :]
        pp_ref[0, i] = p[i * SEQ_BLK:(i + 1) * SEQ_BLK, :]

    for hd in range(N_HEADS):
        lo, hi = hd * HEAD_DIM, (hd + 1) * HEAD_DIM
        vplanes = [v[s * SEQ_BLK:(s + 1) * SEQ_BLK, lo:hi] for s in range(n_pos)]
        for i in range(n_pos):
            z = vplanes[0] * aws_ref[hd * 64 + i * 8]
            for s in range(1, i + 1):
                z = z + vplanes[s] * aws_ref[hd * 64 + i * 8 + s]
            z = z + abs_ref[hd * 8 + i]
            r0, r1 = i * SEQ_BLK, (i + 1) * SEQ_BLK
            mix_ref[r0:r1, lo:hi] = (u[r0:r1, lo:hi] * z).astype(jnp.bfloat16)

    for g, w in enumerate(POOL_WINDOWS):
        lo, hi = g * HEAD_DIM, (g + 1) * HEAD_DIM
        planes = [st_ref[0, k, :, lo:hi] for k in range(POOL_STATE)]
        planes += [p[i * SEQ_BLK:(i + 1) * SEQ_BLK, lo:hi] for i in range(n_pos)]
        pooled = []
        for i in range(n_pos):
            top = POOL_STATE + i
            s = planes[top - w + 1]
            for k in range(top - w + 2, top + 1):
                s = s + planes[k]
            pooled.append(s * (1.0 / w) - planes[top])
        pooled = jnp.concatenate(pooled, axis=0)
        mix_ref[:, A_WIDTH + lo:A_WIDTH + hi] = _pool_project(pooled, g, bw_ref, bscale_ref).astype(jnp.bfloat16)


def _route(h2, rwt_ref, rb_ref, su_ref, cnt_ref, meta_ref, gcol_ref):
    tm = h2.shape[0]
    lt = lax.dot_general(rwt_ref[...], h2, (((1,), (1,)), ((), ())),
                         precision=lax.Precision.HIGHEST,
                         preferred_element_type=jnp.float32) + rb_ref[...]
    row = lambda i: lt[i:i + 1, :]
    l1 = [row(i) for i in range(N_GROUPS)]
    m1 = jnp.maximum(jnp.maximum(l1[0], l1[1]), jnp.maximum(l1[2], l1[3]))
    grp = jnp.where(l1[0] == m1, 0, jnp.where(l1[1] == m1, 1, jnp.where(l1[2] == m1, 2, 3)))
    se = (jnp.exp(l1[0] - m1) + jnp.exp(l1[1] - m1)) + (jnp.exp(l1[2] - m1) + jnp.exp(l1[3] - m1))
    pg = 1.0 / se
    l2 = []
    for e in range(EXPERTS_PER_GROUP):
        c = [row(N_GROUPS + g * EXPERTS_PER_GROUP + e) for g in range(N_GROUPS)]
        l2.append(jnp.where(grp == 0, c[0], jnp.where(grp == 1, c[1], jnp.where(grp == 2, c[2], c[3]))))
    v0 = functools.reduce(jnp.maximum, l2)
    i0 = jnp.full_like(grp, EXPERTS_PER_GROUP - 1)
    for e in range(EXPERTS_PER_GROUP - 2, -1, -1):
        i0 = jnp.where(l2[e] == v0, e, i0)
    neg = jnp.float32(-jnp.inf)
    l2m = [jnp.where(i0 == e, neg, l2[e]) for e in range(EXPERTS_PER_GROUP)]
    v1 = functools.reduce(jnp.maximum, l2m)
    i1 = jnp.full_like(grp, EXPERTS_PER_GROUP - 1)
    for e in range(EXPERTS_PER_GROUP - 2, -1, -1):
        i1 = jnp.where((l2m[e] == v1) & (i0 != e), e, i1)
    d = jnp.exp(v1 - v0)
    g0 = pg / (1.0 + d)
    g1 = (pg * d) / (1.0 + d)
    e0 = grp * EXPERTS_PER_GROUP + i0
    e1 = grp * EXPERTS_PER_GROUP + i1

    eiota = lax.broadcasted_iota(jnp.int32, (N_EXPERTS, tm), 0)
    hit0 = eiota == e0
    hit1 = eiota == e1
    onehot = (hit0 | hit1).astype(jnp.bfloat16)
    prefix = jnp.dot(onehot, su_ref[...], preferred_element_type=jnp.float32)
    carry = cnt_ref[...]
    base = prefix + jnp.concatenate([carry] * (tm // LANES), axis=1)
    r0 = jnp.sum(jnp.where(hit0, base, 0.0), axis=0, keepdims=True)
    r1 = jnp.sum(jnp.where(hit1, base, 0.0), axis=0, keepdims=True)
    ones = jnp.ones((tm, LANES), jnp.bfloat16)
    cnt_ref[...] = carry + jnp.dot(onehot, ones, preferred_element_type=jnp.float32)

    meta_ref[0:1, :] = e0
    meta_ref[1:2, :] = e1
    meta_ref[2:3, :] = r0.astype(jnp.int32)
    meta_ref[3:4, :] = r1.astype(jnp.int32)
    meta_ref[4:8, :] = jnp.zeros((4, tm), jnp.int32)

    giota = lax.broadcasted_iota(jnp.int32, (LANES, tm), 0)
    gpad = jnp.where(giota == 0, g0, jnp.where(giota == 1, g1, 0.0))
    gcol_ref[...] = gpad.T


def _mixer_kernel(aws_s_ref, abs_s_ref,
                  xp_ref, xs_ref, st_ref, n1g_ref, win_ref, ang_ref, anb_ref, aws_ref, abias_ref, bw_ref, bscale_ref,
                  wout_ref, n2g_ref, rwt_ref, rb_ref, su_ref,
                  x1_ref, h2_ref, meta_ref, gcol_ref, counts_ref, pstate_ref, pp_ref, vp_ref,
                  mix_ref, pcarry_ref, cnt_ref, *, n_prompt_tiles, seq_tiles):
    i = pl.program_id(0)
    is_prompt = i < n_prompt_tiles
    j = i % seq_tiles

    @pl.when(i == 0)
    def _():
        cnt_ref[...] = jnp.zeros_like(cnt_ref)

    @pl.when(is_prompt & (j == 0))
    def _():
        pcarry_ref[...] = jnp.zeros_like(pcarry_ref)

    x = jnp.where(is_prompt, xp_ref[0], xs_ref[...])
    h = _rmsnorm(x, n1g_ref[...]).astype(jnp.bfloat16)
    proj = jnp.dot(h, win_ref[...], preferred_element_type=jnp.float32)
    uv = _gelu(proj[:, : 2 * A_WIDTH])
    u = uv[:, :A_WIDTH]
    v = _layernorm(uv[:, A_WIDTH:], ang_ref[...], anb_ref[...])
    p = proj[:, 2 * A_WIDTH:]

    @pl.when(is_prompt)
    def _():
        _prompt_mixers(j, u, v, p, aws_ref, abias_ref, bw_ref, bscale_ref, mix_ref, pcarry_ref, pstate_ref)

    @pl.when(jnp.logical_not(is_prompt))
    def _():
        _sample_mixers(u, v, p, aws_s_ref, abs_s_ref, st_ref, bw_ref, bscale_ref, mix_ref, pp_ref, vp_ref)

    x1 = x + jnp.dot(mix_ref[...], wout_ref[...], preferred_element_type=jnp.float32)
    x1_ref[...] = x1
    h2 = _rmsnorm(x1, n2g_ref[...])
    for s in range(ROW_TILES):
        h2_ref[pl.ds(s, TM, stride=ROW_TILES), :] = h2[:, s * LANES:(s + 1) * LANES]
    _route(h2, rwt_ref, rb_ref, su_ref, cnt_ref, meta_ref, gcol_ref)
    counts_ref[...] = cnt_ref[...]


def _expert_kernel(te_ref, nv_ref, nrows_ref,
                   src_ref, src_next_ref, dst_ref,
                   h2_hbm, w1_ref, w3_ref, w2_ref,
                   out_hbm,
                   xbuf, obuf, w1b, w3b, w2b, gsem, ssem):
    i = pl.program_id(0)
    nv = nv_ref[0]
    slot = i % 2
    slab = lambda r: pl.ds(pl.multiple_of(r * ROW_TILES, ROW_TILES), ROW_TILES)

    def issue_gather(idx_ref, dst_slot):
        def body(rb, carry):
            for uu in range(DMA_UNROLL):
                r = rb * DMA_UNROLL + uu
                pltpu.make_async_copy(h2_hbm.at[idx_ref[0, 0, r]], xbuf.at[dst_slot, slab(r), :],
                                      gsem.at[dst_slot]).start()
            return carry
        lax.fori_loop(0, TE // DMA_UNROLL, body, 0)

    def scatter_copy(r, src_slot):
        return pltpu.make_async_copy(obuf.at[src_slot, slab(r), :], out_hbm.at[dst_ref[0, 0, r]], ssem.at[src_slot])

    def wait_rows(buf, buf_slot, n_rows, sem):
        n = n_rows * ROW_TILES
        pltpu.make_async_copy(buf.at[1 - buf_slot, pl.ds(0, n), :], buf.at[buf_slot, pl.ds(0, n), :], sem).wait()

    def wait_scatter(tile, src_slot):
        wait_rows(obuf, src_slot, nrows_ref[tile], ssem.at[src_slot])

    @pl.when(i == 0)
    def _():
        issue_gather(src_ref, 0)

    @pl.when(i + 1 < nv)
    def _():
        issue_gather(src_next_ref, 1 - slot)

    @pl.when(i < nv)
    def _():
        @pl.when((i == 0) | (te_ref[i] != te_ref[jnp.maximum(i - 1, 0)]))
        def _():
            w1b[...] = w1_ref[0].astype(jnp.bfloat16)
            w3b[...] = w3_ref[0].astype(jnp.bfloat16)
            w2b[...] = w2_ref[0].astype(jnp.bfloat16)

        wait_rows(xbuf, slot, TE, gsem.at[slot])

        @pl.when(i >= 2)
        def _():
            wait_scatter(i - 2, slot)

        xs = xbuf.at[slot]
        x = jnp.concatenate([_row_slab(xs, s, TE) for s in range(ROW_TILES)], axis=1).astype(jnp.bfloat16)
        a = jnp.dot(x, w1b[...], preferred_element_type=jnp.float32)
        b = jnp.dot(x, w3b[...], preferred_element_type=jnp.float32)
        h = (a * jax.nn.sigmoid(a)) * b
        o = jnp.dot(h.astype(jnp.bfloat16), w2b[...], preferred_element_type=jnp.float32)
        os = obuf.at[slot]
        for s in range(ROW_TILES):
            os[pl.ds(s, TE, stride=ROW_TILES), :] = o[:, s * LANES:(s + 1) * LANES]

        n = nrows_ref[i]
        n_full = n // DMA_UNROLL

        def body(rb, carry):
            for uu in range(DMA_UNROLL):
                r = rb * DMA_UNROLL + uu
                scatter_copy(r, slot).start(priority=1)
            return carry
        lax.fori_loop(0, n_full, body, 0)
        for uu in range(DMA_UNROLL - 1):
            r = n_full * DMA_UNROLL + uu

            @pl.when(r < n)
            def _():
                scatter_copy(r, slot).start(priority=1)

        @pl.when(i == nv - 1)
        def _():
            @pl.when(i >= 1)
            def _():
                wait_scatter(i - 1, 1 - slot)
            wait_scatter(i, slot)


def _combine_kernel(x1_ref, gcol_ref, nfg_ref, o0_ref, o1_ref, yp_ref, ys_ref, *, n_prompt_tiles):
    i = pl.program_id(0)
    o0 = jnp.concatenate([_row_slab(o0_ref, s, TM) for s in range(ROW_TILES)], axis=1)
    o1 = jnp.concatenate([_row_slab(o1_ref, s, TM) for s in range(ROW_TILES)], axis=1)
    g = gcol_ref[...]
    moe = g[:, 0:1] * o0 + g[:, 1:2] * o1
    y = _rmsnorm(x1_ref[...] + moe, nfg_ref[...])

    @pl.when(i < n_prompt_tiles)
    def _():
        yp_ref[...] = y

    @pl.when(i >= n_prompt_tiles)
    def _():
        ys_ref[...] = y


def _const_spec(shape):
    return pl.BlockSpec(shape, lambda *_: (0,) * len(shape))


def kernel(x_prompt, x_sample, state_pool, norm1_g, w_in, a_norm_g, a_norm_b, a_ws, a_bs, b_w, b_scale, w_out,
           norm2_g, r1_w, r1_b, r2_w, r2_b, exp_w1, exp_w3, exp_w2, normf_g):
    f32, bf16, i32 = jnp.float32, jnp.bfloat16, jnp.int32
    n_batch, seq, _ = x_prompt.shape
    dec_batch, dec_seq, _ = x_sample.shape
    assert norm1_g.shape[0] == 1 and seq % TM == 0 and TM % CHUNK == 0
    assert dec_seq * SEQ_BLK == TM and dec_batch % SEQ_BLK == 0 and dec_seq <= CHUNK
    t_prompt = n_batch * seq
    t_sample = dec_batch * dec_seq
    t_total = t_prompt + t_sample
    n_tok_tiles = t_total // TM
    n_prompt_tiles = t_prompt // TM
    n_sample_tiles = t_sample // TM
    seq_tiles = seq // TM
    n_assign = TOP_K * t_total
    n_exp_tiles = -(-(n_assign + N_EXPERTS * (TE - 1)) // TE)
    p_rows = n_exp_tiles * TE
    plane_rows = t_total * ROW_TILES

    n1g = norm1_g[0][None, :]
    n2g = norm2_g[0][None, :]
    nfg = normf_g[None, :]
    win = w_in[0].astype(bf16)
    wout = w_out[0].astype(bf16)
    ang = a_norm_g[0][None, :]
    anb = a_norm_b[0][None, :]
    bw = b_w[0].astype(bf16)
    bscale = b_scale[0][None, :]
    abias = jnp.repeat(a_bs[0][:, :CHUNK].T, HEAD_DIM, axis=1)
    rwt = jnp.concatenate([r1_w[0].T, r2_w[0].transpose(0, 2, 1).reshape(N_EXPERTS, D_MODEL),
                           jnp.zeros((N_ROUTER_ROWS - N_GROUPS - N_EXPERTS, D_MODEL), f32)], axis=0)
    rbias = jnp.concatenate([r1_b[0], r2_b[0].reshape(-1),
                             jnp.zeros((N_ROUTER_ROWS - N_GROUPS - N_EXPERTS,), f32)])
    rb = jnp.broadcast_to(rbias[:, None], (N_ROUTER_ROWS, TM))
    su = (jnp.arange(TM)[:, None] < jnp.arange(TM)[None, :]).astype(bf16)
    xs_planes = (x_sample.reshape(n_sample_tiles, SEQ_BLK, dec_seq, D_MODEL)
                 .transpose(0, 2, 1, 3).reshape(t_sample, D_MODEL))
    st_planes = state_pool[0].reshape(n_sample_tiles, SEQ_BLK, POOL_STATE, B_WIDTH).transpose(0, 2, 1, 3)
    aws_s = a_ws[0][:, :dec_seq, :dec_seq].reshape(-1)
    abs_s = a_bs[0][:, :dec_seq].reshape(-1)

    cparams = pltpu.CompilerParams(dimension_semantics=("arbitrary",), vmem_limit_bytes=VMEM_LIMIT)

    p_tile = lambda i: jnp.minimum(i, n_prompt_tiles - 1)
    s_tile = lambda i: jnp.maximum(i - n_prompt_tiles, 0)
    plane_shape = jax.ShapeDtypeStruct((n_sample_tiles, dec_seq, SEQ_BLK, A_WIDTH), f32)
    x1, h2, meta, gcol, counts, pstate, pplanes, vplanes = pl.pallas_call(
        functools.partial(_mixer_kernel, n_prompt_tiles=n_prompt_tiles, seq_tiles=seq_tiles),
        grid_spec=pltpu.PrefetchScalarGridSpec(
            num_scalar_prefetch=2,
            grid=(n_tok_tiles,),
            in_specs=[
                pl.BlockSpec((1, TM, D_MODEL), lambda i, *_: (p_tile(i) // seq_tiles, p_tile(i) % seq_tiles, 0)),
                pl.BlockSpec((TM, D_MODEL), lambda i, *_: (s_tile(i), 0)),
                pl.BlockSpec((1, POOL_STATE, SEQ_BLK, B_WIDTH), lambda i, *_: (s_tile(i), 0, 0, 0)),
                _const_spec((1, D_MODEL)), _const_spec((D_MODEL, 3 * A_WIDTH)), _const_spec((1, A_WIDTH)),
                _const_spec((1, A_WIDTH)), _const_spec((N_HEADS, CHUNK, CHUNK)), _const_spec((CHUNK, A_WIDTH)),
                _const_spec((N_GROUPS, HEAD_DIM, HEAD_DIM)), _const_spec((1, B_WIDTH)),
                _const_spec((D_MODEL, D_MODEL)), _const_spec((1, D_MODEL)), _const_spec((N_ROUTER_ROWS, D_MODEL)),
                _const_spec((N_ROUTER_ROWS, TM)), _const_spec((TM, TM)),
            ],
            out_specs=[
                pl.BlockSpec((TM, D_MODEL), lambda i, *_: (i, 0)),
                pl.BlockSpec((TM * ROW_TILES, LANES), lambda i, *_: (i, 0)),
                pl.BlockSpec((SUBLANES, TM), lambda i, *_: (0, i)),
                pl.BlockSpec((TM, LANES), lambda i, *_: (i, 0)),
                _const_spec((N_EXPERTS, LANES)),
                pl.BlockSpec((1, CARRY_ROWS, B_WIDTH), lambda i, *_: (p_tile(i) // seq_tiles, 0, 0)),
                pl.BlockSpec((1, dec_seq, SEQ_BLK, A_WIDTH), lambda i, *_: (s_tile(i), 0, 0, 0)),
                pl.BlockSpec((1, dec_seq, SEQ_BLK, A_WIDTH), lambda i, *_: (s_tile(i), 0, 0, 0)),
            ],
            scratch_shapes=[pltpu.VMEM((TM, D_MODEL), bf16), pltpu.VMEM((CARRY_ROWS, B_WIDTH), f32),
                            pltpu.VMEM((N_EXPERTS, LANES), f32)],
        ),
        out_shape=[
            jax.ShapeDtypeStruct((t_total, D_MODEL), f32),
            jax.ShapeDtypeStruct((plane_rows, LANES), f32),
            jax.ShapeDtypeStruct((SUBLANES, t_total), i32),
            jax.ShapeDtypeStruct((t_total, LANES), f32),
            jax.ShapeDtypeStruct((N_EXPERTS, LANES), f32),
            jax.ShapeDtypeStruct((n_batch, CARRY_ROWS, B_WIDTH), f32),
            plane_shape, plane_shape,
        ],
        compiler_params=cparams,
        name="mixer",
    )(aws_s, abs_s, x_prompt, xs_planes, st_planes, n1g, win, ang, anb, a_ws[0][:, :CHUNK, :CHUNK], abias, bw,
      bscale, wout, n2g, rwt, rb, su)

    cnt = counts[:, 0].astype(i32)
    padded = (cnt + (TE - 1)) // TE * TE
    pad_end = jnp.cumsum(padded)
    pad_start = pad_end - padded
    dest = jnp.stack([pad_start[meta[0]] + meta[2], pad_start[meta[1]] + meta[3]], axis=-1).reshape(-1)
    asg = jnp.full((p_rows,), -1, i32).at[dest].set(jnp.arange(n_assign, dtype=i32), unique_indices=True)
    row_src = (jnp.maximum(asg, 0) >> 1).reshape(n_exp_tiles, 1, TE)
    row_dst = jnp.where(asg >= 0, (asg & 1) * t_total + (asg >> 1), 0).reshape(n_exp_tiles, 1, TE)
    n_valid = jnp.maximum(pad_end[-1] // TE, 1).astype(i32)
    tile_ids = jnp.minimum(jnp.arange(n_exp_tiles, dtype=i32), n_valid - 1)
    tile_e = jnp.sum((pad_end[None, :] <= (tile_ids * TE)[:, None]).astype(i32), axis=1)
    tile_e = jnp.minimum(tile_e, N_EXPERTS - 1)
    tile_rows = jnp.clip(cnt[tile_e] - (tile_ids * TE - pad_start[tile_e]), 0, TE).astype(i32)

    cur_blk = lambda i, te, nv, nr: (jnp.minimum(i, nv[0] - 1), 0, 0)
    nxt_blk = lambda i, te, nv, nr: (jnp.minimum(i + 1, nv[0] - 1), 0, 0)
    w_blk = lambda i, te, nv, nr: (te[i], 0, 0)
    smem_blk = lambda imap: pl.BlockSpec((1, 1, TE), imap, memory_space=pltpu.SMEM)
    row_buf = pltpu.VMEM((2, TE * ROW_TILES, LANES), f32)
    out_tok = pl.pallas_call(
        _expert_kernel,
        grid_spec=pltpu.PrefetchScalarGridSpec(
            num_scalar_prefetch=3,
            grid=(n_exp_tiles,),
            in_specs=[smem_blk(cur_blk), smem_blk(nxt_blk), smem_blk(cur_blk),
                      pl.BlockSpec(memory_space=pl.ANY),
                      pl.BlockSpec((1, D_MODEL, D_EXPERT), w_blk),
                      pl.BlockSpec((1, D_MODEL, D_EXPERT), w_blk),
                      pl.BlockSpec((1, D_EXPERT, D_MODEL), w_blk)],
            out_specs=pl.BlockSpec(memory_space=pl.ANY),
            scratch_shapes=[row_buf, row_buf,
                            pltpu.VMEM((D_MODEL, D_EXPERT), bf16), pltpu.VMEM((D_MODEL, D_EXPERT), bf16),
                            pltpu.VMEM((D_EXPERT, D_MODEL), bf16),
                            pltpu.SemaphoreType.DMA((2,)), pltpu.SemaphoreType.DMA((2,))],
        ),
        out_shape=jax.ShapeDtypeStruct((TOP_K * t_total, ROW_TILES, LANES), f32),
        compiler_params=cparams,
        name="moe_experts",
    )(tile_e, n_valid[None], tile_rows, row_src, row_src, row_dst, h2.reshape(t_total, ROW_TILES, LANES),
      exp_w1[0], exp_w3[0], exp_w2[0])
    out_tok = out_tok.reshape(TOP_K * plane_rows, LANES)

    y_prompt, y_sample = pl.pallas_call(
        functools.partial(_combine_kernel, n_prompt_tiles=n_prompt_tiles),
        grid=(n_tok_tiles,),
        in_specs=[pl.BlockSpec((TM, D_MODEL), lambda i: (i, 0)),
                  pl.BlockSpec((TM, LANES), lambda i: (i, 0)),
                  _const_spec((1, D_MODEL)),
                  pl.BlockSpec((TM * ROW_TILES, LANES), lambda i: (i, 0)),
                  pl.BlockSpec((TM * ROW_TILES, LANES), lambda i: (n_tok_tiles + i, 0))],
        out_specs=[pl.BlockSpec((TM, D_MODEL), lambda i: (p_tile(i), 0)),
                   pl.BlockSpec((TM, D_MODEL), lambda i: (s_tile(i), 0))],
        out_shape=[jax.ShapeDtypeStruct((t_prompt, D_MODEL), f32), jax.ShapeDtypeStruct((t_sample, D_MODEL), f32)],
        compiler_params=cparams,
        name="moe_combine",
    )(x1, gcol, nfg, out_tok, out_tok)

    unplane = lambda a: a.transpose(0, 2, 1, 3).reshape(dec_batch, dec_seq, a.shape[-1])
    y_prompt = y_prompt.reshape(n_batch, seq, D_MODEL)
    y_sample = unplane(y_sample.reshape(n_sample_tiles, dec_seq, SEQ_BLK, D_MODEL))
    pool_state_prompt = pstate[None, :, CARRY_ROWS - POOL_STATE:, :]
    p_s = unplane(pplanes)
    pool_state_sample = jnp.concatenate([state_pool[0], p_s], axis=1)[None, :, -POOL_STATE:, :]
    chunk_v_sample = unplane(vplanes)[None]
    return (y_prompt, y_sample, pool_state_prompt, pool_state_sample, chunk_v_sample)
```

```python
import functools
import math

import jax
import jax.numpy as jnp
from jax import lax
from jax.experimental import pallas as pl
from jax.experimental.pallas import tpu as pltpu

D_MODEL = 1024
A_WIDTH = 512
B_WIDTH = 512
N_HEADS = 4
HEAD_DIM = 128
CHUNK = 128
POOL_WINDOWS = (2, 4, 8, 16)
POOL_STATE = 15
N_GROUPS = 4
EXPERTS_PER_GROUP = 8
N_EXPERTS = 32
TOP_K = 2
D_EXPERT = 512
EPS = 1e-6

SUBLANES = 8
LANES = 128
ROW_TILES = D_MODEL // LANES

TM = 256
TE = 256
SEQ_BLK = 32
N_ROUTER_ROWS = 40
CARRY_ROWS = 16
DMA_UNROLL = 8
VMEM_LIMIT = 48 * 1024 * 1024

_INV_SQRT2 = 1.0 / math.sqrt(2.0)


def _rmsnorm(x, g):
    r = lax.rsqrt(jnp.mean(x * x, axis=-1, keepdims=True) + EPS)
    return (x * r) * g


def _gelu(x):
    return 0.5 * x * (1.0 + lax.erf(x * _INV_SQRT2))


def _layernorm(x, g, b):
    mu = jnp.mean(x, axis=-1, keepdims=True)
    xc = x - mu
    var = jnp.mean(xc * xc, axis=-1, keepdims=True)
    return (xc * lax.rsqrt(var + EPS)) * g + b


def _row_slab(ref, s, n):
    return ref[pl.ds(s, n, stride=ROW_TILES), :]


def _pool_project(pooled, g, bw_ref, bscale_ref):
    lo, hi = g * HEAD_DIM, (g + 1) * HEAD_DIM
    hb = jnp.dot(pooled.astype(jnp.bfloat16), bw_ref[g], preferred_element_type=jnp.float32)
    return hb * bscale_ref[:, lo:hi]


def _prompt_mixers(j, u, v, p, aws_ref, abias_ref, bw_ref, bscale_ref, mix_ref, pcarry_ref, pstate_ref):
    tri = (lax.broadcasted_iota(jnp.int32, (CHUNK, CHUNK), 0)
           >= lax.broadcasted_iota(jnp.int32, (CHUNK, CHUNK), 1))
    vb = v.astype(jnp.bfloat16)
    for hd in range(N_HEADS):
        lo, hi = hd * HEAD_DIM, (hd + 1) * HEAD_DIM
        w = jnp.where(tri, aws_ref[hd], 0.0).astype(jnp.bfloat16)
        for c in range(TM // CHUNK):
            r0, r1 = c * CHUNK, (c + 1) * CHUNK
            z = jnp.dot(w, vb[r0:r1, lo:hi], preferred_element_type=jnp.float32) + abias_ref[:, lo:hi]
            mix_ref[r0:r1, lo:hi] = (u[r0:r1, lo:hi] * z).astype(jnp.bfloat16)

    pos = j * TM + lax.broadcasted_iota(jnp.int32, (TM, LANES), 0)
    for g, w in enumerate(POOL_WINDOWS):
        lo, hi = g * HEAD_DIM, (g + 1) * HEAD_DIM
        pg = p[:, lo:hi]
        acc = jnp.concatenate([pcarry_ref[:, lo:hi], pg], axis=0)
        shift = 1
        while shift < w:
            acc = acc + pltpu.roll(acc, shift, 0)
            shift *= 2
        cnt = jnp.minimum(pos + 1, w).astype(jnp.float32)
        pooled = acc[CARRY_ROWS:, :] / cnt - pg
        mix_ref[:, A_WIDTH + lo:A_WIDTH + hi] = _pool_project(pooled, g, bw_ref, bscale_ref).astype(jnp.bfloat16)
    tail = p[TM - CARRY_ROWS:, :]
    pcarry_ref[...] = tail
    pstate_ref[0] = tail


def _sample_mixers(u, v, p, aws_ref, abs_ref, st_ref, bw_ref, bscale_ref, mix_ref, pp_ref, vp_ref):
    n_pos = TM // SEQ_BLK
    for i in range(n_pos):
        vp_ref[0, i] = v[i * SEQ_BLK:(i + 1) * SEQ_BLK, :]
        pp_ref[0, i] = p[i * SEQ_BLK:(i + 1) * SEQ_BLK, :]

    for hd in range(N_HEADS):
        lo, hi = hd * HEAD_DIM, (hd + 1) * HEAD_DIM
        vplanes = [v[s * SEQ_BLK:(s + 1) * SEQ_BLK, lo:hi] for s in range(n_pos)]
        for i in range(n_pos):
            z = vplanes[0] * aws_ref[hd * 64 + i * 8]
            for s in range(1, i + 1):
                z = z + vplanes[s] * aws_ref[hd * 64 + i * 8 + s]
            z = z + abs_ref[hd * 8 + i]
            r0, r1 = i * SEQ_BLK, (i + 1) * SEQ_BLK
            mix_ref[r0:r1, lo:hi] = (u[r0:r1, lo:hi] * z).astype(jnp.bfloat16)

    for g, w in enumerate(POOL_WINDOWS):
        lo, hi = g * HEAD_DIM, (g + 1) * HEAD_DIM
        planes = [st_ref[0, k, :, lo:hi] for k in range(POOL_STATE)]
        planes += [p[i * SEQ_BLK:(i + 1) * SEQ_BLK, lo:hi] for i in range(n_pos)]
        pooled = []
        for i in range(n_pos):
            top = POOL_STATE + i
            s = planes[top - w + 1]
            for k in range(top - w + 2, top + 1):
                s = s + planes[k]
            pooled.append(s * (1.0 / w) - planes[top])
        pooled = jnp.concatenate(pooled, axis=0)
        mix_ref[:, A_WIDTH + lo:A_WIDTH + hi] = _pool_project(pooled, g, bw_ref, bscale_ref).astype(jnp.bfloat16)


def _route(h2, rwt_ref, rb_ref, su_ref, cnt_ref, meta_ref, gcol_ref):
    tm = h2.shape[0]
    h_hi = h2.astype(jnp.bfloat16)
    h_lo = (h2 - h_hi.astype(jnp.float32)).astype(jnp.bfloat16)
    s = (jnp.dot(h_hi, rwt_ref[...], preferred_element_type=jnp.float32)
         + jnp.dot(h_lo, rwt_ref[...], preferred_element_type=jnp.float32))
    st = s.T
    lt = st[0:N_ROUTER_ROWS, :] + st[N_ROUTER_ROWS:2 * N_ROUTER_ROWS, :] + rb_ref[...]
    row = lambda i: lt[i:i + 1, :]
    l1 = [row(i) for i in range(N_GROUPS)]
    m1 = jnp.maximum(jnp.maximum(l1[0], l1[1]), jnp.maximum(l1[2], l1[3]))
    grp = jnp.where(l1[0] == m1, 0, jnp.where(l1[1] == m1, 1, jnp.where(l1[2] == m1, 2, 3)))
    se = (jnp.exp(l1[0] - m1) + jnp.exp(l1[1] - m1)) + (jnp.exp(l1[2] - m1) + jnp.exp(l1[3] - m1))
    pg = 1.0 / se
    l2 = []
    for e in range(EXPERTS_PER_GROUP):
        c = [row(N_GROUPS + g * EXPERTS_PER_GROUP + e) for g in range(N_GROUPS)]
        l2.append(jnp.where(grp == 0, c[0], jnp.where(grp == 1, c[1], jnp.where(grp == 2, c[2], c[3]))))
    v0 = functools.reduce(jnp.maximum, l2)
    i0 = jnp.full_like(grp, EXPERTS_PER_GROUP - 1)
    for e in range(EXPERTS_PER_GROUP - 2, -1, -1):
        i0 = jnp.where(l2[e] == v0, e, i0)
    neg = jnp.float32(-jnp.inf)
    l2m = [jnp.where(i0 == e, neg, l2[e]) for e in range(EXPERTS_PER_GROUP)]
    v1 = functools.reduce(jnp.maximum, l2m)
    i1 = jnp.full_like(grp, EXPERTS_PER_GROUP - 1)
    for e in range(EXPERTS_PER_GROUP - 2, -1, -1):
        i1 = jnp.where((l2m[e] == v1) & (i0 != e), e, i1)
    d = jnp.exp(v1 - v0)
    g0 = pg / (1.0 + d)
    g1 = (pg * d) / (1.0 + d)
    e0 = grp * EXPERTS_PER_GROUP + i0
    e1 = grp * EXPERTS_PER_GROUP + i1

    eiota = lax.broadcasted_iota(jnp.int32, (N_EXPERTS, tm), 0)
    hit0 = eiota == e0
    hit1 = eiota == e1
    onehot = (hit0 | hit1).astype(jnp.bfloat16)
    prefix = jnp.dot(onehot, su_ref[...], preferred_element_type=jnp.float32)
    carry = cnt_ref[...]
    base = prefix + jnp.concatenate([carry] * (tm // LANES), axis=1)
    r0 = jnp.sum(jnp.where(hit0, base, 0.0), axis=0, keepdims=True)
    r1 = jnp.sum(jnp.where(hit1, base, 0.0), axis=0, keepdims=True)
    ones = jnp.ones((tm, LANES), jnp.bfloat16)
    cnt_ref[...] = carry + jnp.dot(onehot, ones, preferred_element_type=jnp.float32)

    meta_ref[0:1, :] = e0
    meta_ref[1:2, :] = e1
    meta_ref[2:3, :] = r0.astype(jnp.int32)
    meta_ref[3:4, :] = r1.astype(jnp.int32)
    meta_ref[4:8, :] = jnp.zeros((4, tm), jnp.int32)

    giota = lax.broadcasted_iota(jnp.int32, (LANES, tm), 0)
    gpad = jnp.where(giota == 0, g0, jnp.where(giota == 1, g1, 0.0))
    gcol_ref[...] = gpad.T


def _mixer_kernel(aws_s_ref, abs_s_ref,
                  xp_ref, xs_ref, st_ref, n1g_ref, win_ref, ang_ref, anb_ref, aws_ref, abias_ref, bw_ref, bscale_ref,
                  wout_ref, n2g_ref, rwt_ref, rb_ref, su_ref,
                  x1_ref, h2_ref, meta_ref, gcol_ref, counts_ref, pstate_ref, pp_ref, vp_ref,
                  mix_ref, pcarry_ref, cnt_ref, *, n_prompt_tiles, seq_tiles):
    i = pl.program_id(0)
    is_prompt = i < n_prompt_tiles
    j = i % seq_tiles

    @pl.when(i == 0)
    def _():
        cnt_ref[...] = jnp.zeros_like(cnt_ref)

    @pl.when(is_prompt & (j == 0))
    def _():
        pcarry_ref[...] = jnp.zeros_like(pcarry_ref)

    x = jnp.where(is_prompt, xp_ref[0], xs_ref[...])
    h = _rmsnorm(x, n1g_ref[...]).astype(jnp.bfloat16)
    proj = jnp.dot(h, win_ref[...], preferred_element_type=jnp.float32)
    uv = _gelu(proj[:, : 2 * A_WIDTH])
    u = uv[:, :A_WIDTH]
    v = _layernorm(uv[:, A_WIDTH:], ang_ref[...], anb_ref[...])
    p = proj[:, 2 * A_WIDTH:]

    @pl.when(is_prompt)
    def _():
        _prompt_mixers(j, u, v, p, aws_ref, abias_ref, bw_ref, bscale_ref, mix_ref, pcarry_ref, pstate_ref)

    @pl.when(jnp.logical_not(is_prompt))
    def _():
        _sample_mixers(u, v, p, aws_s_ref, abs_s_ref, st_ref, bw_ref, bscale_ref, mix_ref, pp_ref, vp_ref)

    x1 = x + jnp.dot(mix_ref[...], wout_ref[...], preferred_element_type=jnp.float32)
    x1_ref[...] = x1
    h2 = _rmsnorm(x1, n2g_ref[...])
    for s in range(ROW_TILES):
        h2_ref[pl.ds(s, TM, stride=ROW_TILES), :] = h2[:, s * LANES:(s + 1) * LANES]
    _route(h2, rwt_ref, rb_ref, su_ref, cnt_ref, meta_ref, gcol_ref)
    counts_ref[...] = cnt_ref[...]


def _expert_kernel(te_ref, nv_ref, nrows_ref,
                   src_ref, src_next_ref, dst_ref,
                   h2_hbm, w1_ref, w3_ref, w2_ref,
                   out_hbm,
                   xbuf, obuf, w1b, w3b, w2b, gsem, ssem):
    i = pl.program_id(0)
    nv = nv_ref[0]
    slot = i % 2
    slab = lambda r: pl.ds(pl.multiple_of(r * ROW_TILES, ROW_TILES), ROW_TILES)

    def issue_gather(idx_ref, dst_slot):
        def body(rb, carry):
            for uu in range(DMA_UNROLL):
                r = rb * DMA_UNROLL + uu
                pltpu.make_async_copy(h2_hbm.at[idx_ref[0, 0, r]], xbuf.at[dst_slot, slab(r), :],
                                      gsem.at[dst_slot]).start()
            return carry
        lax.fori_loop(0, TE // DMA_UNROLL, body, 0)

    def scatter_copy(r, src_slot):
        return pltpu.make_async_copy(obuf.at[src_slot, slab(r), :], out_hbm.at[dst_ref[0, 0, r]], ssem.at[src_slot])

    def wait_rows(buf, buf_slot, n_rows, sem):
        n = n_rows * ROW_TILES
        pltpu.make_async_copy(buf.at[1 - buf_slot, pl.ds(0, n), :], buf.at[buf_slot, pl.ds(0, n), :], sem).wait()

    def wait_scatter(tile, src_slot):
        wait_rows(obuf, src_slot, nrows_ref[tile], ssem.at[src_slot])

    @pl.when(i == 0)
    def _():
        issue_gather(src_ref, 0)

    @pl.when(i + 1 < nv)
    def _():
        issue_gather(src_next_ref, 1 - slot)

    @pl.when(i < nv)
    def _():
        @pl.when((i == 0) | (te_ref[i] != te_ref[jnp.maximum(i - 1, 0)]))
        def _():
            w1b[...] = w1_ref[0].astype(jnp.bfloat16)
            w3b[...] = w3_ref[0].astype(jnp.bfloat16)
            w2b[...] = w2_ref[0].astype(jnp.bfloat16)

        wait_rows(xbuf, slot, TE, gsem.at[slot])

        @pl.when(i >= 2)
        def _():
            wait_scatter(i - 2, slot)

        xs = xbuf.at[slot]
        x = jnp.concatenate([_row_slab(xs, s, TE) for s in range(ROW_TILES)], axis=1).astype(jnp.bfloat16)
        a = jnp.dot(x, w1b[...], preferred_element_type=jnp.float32)
        b = jnp.dot(x, w3b[...], preferred_element_type=jnp.float32)
        h = (a * jax.nn.sigmoid(a)) * b
        o = jnp.dot(h.astype(jnp.bfloat16), w2b[...], preferred_element_type=jnp.float32)
        os = obuf.at[slot]
        for s in range(ROW_TILES):
            os[pl.ds(s, TE, stride=ROW_TILES), :] = o[:, s * LANES:(s + 1) * LANES]

        n = nrows_ref[i]
        n_full = n // DMA_UNROLL

        def body(rb, carry):
            for uu in range(DMA_UNROLL):
                r = rb * DMA_UNROLL + uu
                scatter_copy(r, slot).start(priority=1)
            return carry
        lax.fori_loop(0, n_full, body, 0)
        for uu in range(DMA_UNROLL - 1):
            r = n_full * DMA_UNROLL + uu

            @pl.when(r < n)
            def _():
                scatter_copy(r, slot).start(priority=1)

        @pl.when(i == nv - 1)
        def _():
            @pl.when(i >= 1)
            def _():
                wait_scatter(i - 1, 1 - slot)
            wait_scatter(i, slot)


def _combine_kernel(x1_ref, gcol_ref, nfg_ref, o0_ref, o1_ref, yp_ref, ys_ref, *, n_prompt_tiles):
    i = pl.program_id(0)
    o0 = jnp.concatenate([_row_slab(o0_ref, s, TM) for s in range(ROW_TILES)], axis=1)
    o1 = jnp.concatenate([_row_slab(o1_ref, s, TM) for s in range(ROW_TILES)], axis=1)
    g = gcol_ref[...]
    moe = g[:, 0:1] * o0 + g[:, 1:2] * o1
    y = _rmsnorm(x1_ref[...] + moe, nfg_ref[...])

    @pl.when(i < n_prompt_tiles)
    def _():
        yp_ref[...] = y

    @pl.when(i >= n_prompt_tiles)
    def _():
        ys_ref[...] = y


def _const_spec(shape):
    return pl.BlockSpec(shape, lambda *_: (0,) * len(shape))


def kernel(x_prompt, x_sample, state_pool, norm1_g, w_in, a_norm_g, a_norm_b, a_ws, a_bs, b_w, b_scale, w_out,
           norm2_g, r1_w, r1_b, r2_w, r2_b, exp_w1, exp_w3, exp_w2, normf_g):
    f32, bf16, i32 = jnp.float32, jnp.bfloat16, jnp.int32
    n_batch, seq, _ = x_prompt.shape
    dec_batch, dec_seq, _ = x_sample.shape
    assert norm1_g.shape[0] == 1 and seq % TM == 0 and TM % CHUNK == 0
    assert dec_seq * SEQ_BLK == TM and dec_batch % SEQ_BLK == 0 and dec_seq <= CHUNK
    t_prompt = n_batch * seq
    t_sample = dec_batch * dec_seq
    t_total = t_prompt + t_sample
    n_tok_tiles = t_total // TM
    n_prompt_tiles = t_prompt // TM
    n_sample_tiles = t_sample // TM
    seq_tiles = seq // TM
    n_assign = TOP_K * t_total
    n_exp_tiles = -(-(n_assign + N_EXPERTS * (TE - 1)) // TE)
    p_rows = n_exp_tiles * TE
    plane_rows = t_total * ROW_TILES

    n1g = norm1_g[0][None, :]
    n2g = norm2_g[0][None, :]
    nfg = normf_g[None, :]
    win = w_in[0].astype(bf16)
    wout = w_out[0].astype(bf16)
    ang = a_norm_g[0][None, :]
    anb = a_norm_b[0][None, :]
    bw = b_w[0].astype(bf16)
    bscale = b_scale[0][None, :]
    abias = jnp.repeat(a_bs[0][:, :CHUNK].T, HEAD_DIM, axis=1)
    rw = jnp.concatenate([r1_w[0], r2_w[0].transpose(1, 0, 2).reshape(D_MODEL, N_EXPERTS),
                          jnp.zeros((D_MODEL, N_ROUTER_ROWS - N_GROUPS - N_EXPERTS), f32)], axis=1)
    rw_hi = rw.astype(bf16)
    rw_lo = (rw - rw_hi.astype(f32)).astype(bf16)
    rwt = jnp.concatenate([rw_hi, rw_lo, jnp.zeros((D_MODEL, LANES - 2 * N_ROUTER_ROWS), bf16)], axis=1)
    rbias = jnp.concatenate([r1_b[0], r2_b[0].reshape(-1),
                             jnp.zeros((N_ROUTER_ROWS - N_GROUPS - N_EXPERTS,), f32)])
    rb = jnp.broadcast_to(rbias[:, None], (N_ROUTER_ROWS, TM))
    su = (jnp.arange(TM)[:, None] < jnp.arange(TM)[None, :]).astype(bf16)
    xs_planes = (x_sample.reshape(n_sample_tiles, SEQ_BLK, dec_seq, D_MODEL)
                 .transpose(0, 2, 1, 3).reshape(t_sample, D_MODEL))
    st_planes = state_pool[0].reshape(n_sample_tiles, SEQ_BLK, POOL_STATE, B_WIDTH).transpose(0, 2, 1, 3)
    aws_s = a_ws[0][:, :dec_seq, :dec_seq].reshape(-1)
    abs_s = a_bs[0][:, :dec_seq].reshape(-1)

    cparams = pltpu.CompilerParams(dimension_semantics=("arbitrary",), vmem_limit_bytes=VMEM_LIMIT)

    p_tile = lambda i: jnp.minimum(i, n_prompt_tiles - 1)
    s_tile = lambda i: jnp.maximum(i - n_prompt_tiles, 0)
    plane_shape = jax.ShapeDtypeStruct((n_sample_tiles, dec_seq, SEQ_BLK, A_WIDTH), f32)
    x1, h2, meta, gcol, counts, pstate, pplanes, vplanes = pl.pallas_call(
        functools.partial(_mixer_kernel, n_prompt_tiles=n_prompt_tiles, seq_tiles=seq_tiles),
        grid_spec=pltpu.PrefetchScalarGridSpec(
            num_scalar_prefetch=2,
            grid=(n_tok_tiles,),
            in_specs=[
                pl.BlockSpec((1, TM, D_MODEL), lambda i, *_: (p_tile(i) // seq_tiles, p_tile(i) % seq_tiles, 0)),
                pl.BlockSpec((TM, D_MODEL), lambda i, *_: (s_tile(i), 0)),
                pl.BlockSpec((1, POOL_STATE, SEQ_BLK, B_WIDTH), lambda i, *_: (s_tile(i), 0, 0, 0)),
                _const_spec((1, D_MODEL)), _const_spec((D_MODEL, 3 * A_WIDTH)), _const_spec((1, A_WIDTH)),
                _const_spec((1, A_WIDTH)), _const_spec((N_HEADS, CHUNK, CHUNK)), _const_spec((CHUNK, A_WIDTH)),
                _const_spec((N_GROUPS, HEAD_DIM, HEAD_DIM)), _const_spec((1, B_WIDTH)),
                _const_spec((D_MODEL, D_MODEL)), _const_spec((1, D_MODEL)), _const_spec((D_MODEL, LANES)),
                _const_spec((N_ROUTER_ROWS, TM)), _const_spec((TM, TM)),
            ],
            out_specs=[
                pl.BlockSpec((TM, D_MODEL), lambda i, *_: (i, 0)),
                pl.BlockSpec((TM * ROW_TILES, LANES), lambda i, *_: (i, 0)),
                pl.BlockSpec((SUBLANES, TM), lambda i, *_: (0, i)),
                pl.BlockSpec((TM, LANES), lambda i, *_: (i, 0)),
                _const_spec((N_EXPERTS, LANES)),
                pl.BlockSpec((1, CARRY_ROWS, B_WIDTH), lambda i, *_: (p_tile(i) // seq_tiles, 0, 0)),
                pl.BlockSpec((1, dec_seq, SEQ_BLK, A_WIDTH), lambda i, *_: (s_tile(i), 0, 0, 0)),
                pl.BlockSpec((1, dec_seq, SEQ_BLK, A_WIDTH), lambda i, *_: (s_tile(i), 0, 0, 0)),
            ],
            scratch_shapes=[pltpu.VMEM((TM, D_MODEL), bf16), pltpu.VMEM((CARRY_ROWS, B_WIDTH), f32),
                            pltpu.VMEM((N_EXPERTS, LANES), f32)],
        ),
        out_shape=[
            jax.ShapeDtypeStruct((t_total, D_MODEL), f32),
            jax.ShapeDtypeStruct((plane_rows, LANES), f32),
            jax.ShapeDtypeStruct((SUBLANES, t_total), i32),
            jax.ShapeDtypeStruct((t_total, LANES), f32),
            jax.ShapeDtypeStruct((N_EXPERTS, LANES), f32),
            jax.ShapeDtypeStruct((n_batch, CARRY_ROWS, B_WIDTH), f32),
            plane_shape, plane_shape,
        ],
        compiler_params=cparams,
        name="mixer",
    )(aws_s, abs_s, x_prompt, xs_planes, st_planes, n1g, win, ang, anb, a_ws[0][:, :CHUNK, :CHUNK], abias, bw,
      bscale, wout, n2g, rwt, rb, su)

    cnt = counts[:, 0].astype(i32)
    padded = (cnt + (TE - 1)) // TE * TE
    pad_end = jnp.cumsum(padded)
    pad_start = pad_end - padded
    dest = jnp.stack([pad_start[meta[0]] + meta[2], pad_start[meta[1]] + meta[3]], axis=-1).reshape(-1)
    asg = jnp.full((p_rows,), -1, i32).at[dest].set(jnp.arange(n_assign, dtype=i32), unique_indices=True)
    row_src = (jnp.maximum(asg, 0) >> 1).reshape(n_exp_tiles, 1, TE)
    row_dst = jnp.where(asg >= 0, (asg & 1) * t_total + (asg >> 1), 0).reshape(n_exp_tiles, 1, TE)
    n_valid = jnp.maximum(pad_end[-1] // TE, 1).astype(i32)
    tile_ids = jnp.minimum(jnp.arange(n_exp_tiles, dtype=i32), n_valid - 1)
    tile_e = jnp.sum((pad_end[None, :] <= (tile_ids * TE)[:, None]).astype(i32), axis=1)
    tile_e = jnp.minimum(tile_e, N_EXPERTS - 1)
    tile_rows = jnp.clip(cnt[tile_e] - (tile_ids * TE - pad_start[tile_e]), 0, TE).astype(i32)

    cur_blk = lambda i, te, nv, nr: (jnp.minimum(i, nv[0] - 1), 0, 0)
    nxt_blk = lambda i, te, nv, nr: (jnp.minimum(i + 1, nv[0] - 1), 0, 0)
    w_blk = lambda i, te, nv, nr: (te[i], 0, 0)
    smem_blk = lambda imap: pl.BlockSpec((1, 1, TE), imap, memory_space=pltpu.SMEM)
    row_buf = pltpu.VMEM((2, TE * ROW_TILES, LANES), f32)
    out_tok = pl.pallas_call(
        _expert_kernel,
        grid_spec=pltpu.PrefetchScalarGridSpec(
            num_scalar_prefetch=3,
            grid=(n_exp_tiles,),
            in_specs=[smem_blk(cur_blk), smem_blk(nxt_blk), smem_blk(cur_blk),
                      pl.BlockSpec(memory_space=pl.ANY),
                      pl.BlockSpec((1, D_MODEL, D_EXPERT), w_blk),
                      pl.BlockSpec((1, D_MODEL, D_EXPERT), w_blk),
                      pl.BlockSpec((1, D_EXPERT, D_MODEL), w_blk)],
            out_specs=pl.BlockSpec(memory_space=pl.ANY),
            scratch_shapes=[row_buf, row_buf,
                            pltpu.VMEM((D_MODEL, D_EXPERT), bf16), pltpu.VMEM((D_MODEL, D_EXPERT), bf16),
                            pltpu.VMEM((D_EXPERT, D_MODEL), bf16),
                            pltpu.SemaphoreType.DMA((2,)), pltpu.SemaphoreType.DMA((2,))],
        ),
        out_shape=jax.ShapeDtypeStruct((TOP_K * t_total, ROW_TILES, LANES), f32),
        compiler_params=cparams,
        name="moe_experts",
    )(tile_e, n_valid[None], tile_rows, row_src, row_src, row_dst, h2.reshape(t_total, ROW_TILES, LANES),
      exp_w1[0], exp_w3[0], exp_w2[0])
    out_tok = out_tok.reshape(TOP_K * plane_rows, LANES)

    y_prompt, y_sample = pl.pallas_call(
        functools.partial(_combine_kernel, n_prompt_tiles=n_prompt_tiles),
        grid=(n_tok_tiles,),
        in_specs=[pl.BlockSpec((TM, D_MODEL), lambda i: (i, 0)),
                  pl.BlockSpec((TM, LANES), lambda i: (i, 0)),
                  _const_spec((1, D_MODEL)),
                  pl.BlockSpec((TM * ROW_TILES, LANES), lambda i: (i, 0)),
                  pl.BlockSpec((TM * ROW_TILES, LANES), lambda i: (n_tok_tiles + i, 0))],
        out_specs=[pl.BlockSpec((TM, D_MODEL), lambda i: (p_tile(i), 0)),
                   pl.BlockSpec((TM, D_MODEL), lambda i: (s_tile(i), 0))],
        out_shape=[jax.ShapeDtypeStruct((t_prompt, D_MODEL), f32), jax.ShapeDtypeStruct((t_sample, D_MODEL), f32)],
        compiler_params=cparams,
        name="moe_combine",
    )(x1, gcol, nfg, out_tok, out_tok)

    unplane = lambda a: a.transpose(0, 2, 1, 3).reshape(dec_batch, dec_seq, a.shape[-1])
    y_prompt = y_prompt.reshape(n_batch, seq, D_MODEL)
    y_sample = unplane(y_sample.reshape(n_sample_tiles, dec_seq, SEQ_BLK, D_MODEL))
    pool_state_prompt = pstate[None, :, CARRY_ROWS - POOL_STATE:, :]
    p_s = unplane(pplanes)
    pool_state_sample = jnp.concatenate([state_pool[0], p_s], axis=1)[None, :, -POOL_STATE:, :]
    chunk_v_sample = unplane(vplanes)[None]
    return (y_prompt, y_sample, pool_state_prompt, pool_state_sample, chunk_v_sample)
```

```python
import functools
import math

import jax
import jax.numpy as jnp
from jax import lax
from jax.experimental import pallas as pl
from jax.experimental.pallas import tpu as pltpu

D_MODEL = 1024
A_WIDTH = 512
B_WIDTH = 512
N_HEADS = 4
HEAD_DIM = 128
CHUNK = 128
POOL_WINDOWS = (2, 4, 8, 16)
POOL_STATE = 15
N_GROUPS = 4
EXPERTS_PER_GROUP = 8
N_EXPERTS = 32
TOP_K = 2
D_EXPERT = 512
EPS = 1e-6

SUBLANES = 8
LANES = 128
ROW_TILES = D_MODEL // LANES

TM = 256
TE = 256
SEQ_BLK = 32
N_ROUTER_ROWS = 40
CARRY_ROWS = 16
DMA_UNROLL = 8
VMEM_LIMIT = 48 * 1024 * 1024
N_PARTS = 2
EXPERT_VMEM_LIMIT = 62 * 1024 * 1024

_INV_SQRT2 = 1.0 / math.sqrt(2.0)


def _rmsnorm(x, g):
    r = lax.rsqrt(jnp.mean(x * x, axis=-1, keepdims=True) + EPS)
    return (x * r) * g


def _gelu(x):
    return 0.5 * x * (1.0 + lax.erf(x * _INV_SQRT2))


def _layernorm(x, g, b):
    mu = jnp.mean(x, axis=-1, keepdims=True)
    xc = x - mu
    var = jnp.mean(xc * xc, axis=-1, keepdims=True)
    return (xc * lax.rsqrt(var + EPS)) * g + b


def _row_slab(ref, s, n):
    return ref[pl.ds(s, n, stride=ROW_TILES), :]


def _pool_project(pooled, g, bw_ref, bscale_ref):
    lo, hi = g * HEAD_DIM, (g + 1) * HEAD_DIM
    hb = jnp.dot(pooled.astype(jnp.bfloat16), bw_ref[g], preferred_element_type=jnp.float32)
    return hb * bscale_ref[:, lo:hi]


def _prompt_mixers(j, u, v, p, aws_ref, abias_ref, bw_ref, bscale_ref, mix_ref, pcarry_ref, pstate_ref):
    tri = (lax.broadcasted_iota(jnp.int32, (CHUNK, CHUNK), 0)
           >= lax.broadcasted_iota(jnp.int32, (CHUNK, CHUNK), 1))
    vb = v.astype(jnp.bfloat16)
    for hd in range(N_HEADS):
        lo, hi = hd * HEAD_DIM, (hd + 1) * HEAD_DIM
        w = jnp.where(tri, aws_ref[hd], 0.0).astype(jnp.bfloat16)
        for c in range(TM // CHUNK):
            r0, r1 = c * CHUNK, (c + 1) * CHUNK
            z = jnp.dot(w, vb[r0:r1, lo:hi], preferred_element_type=jnp.float32) + abias_ref[:, lo:hi]
            mix_ref[r0:r1, lo:hi] = (u[r0:r1, lo:hi] * z).astype(jnp.bfloat16)

    pos = j * TM + lax.broadcasted_iota(jnp.int32, (TM, LANES), 0)
    for g, w in enumerate(POOL_WINDOWS):
        lo, hi = g * HEAD_DIM, (g + 1) * HEAD_DIM
        pg = p[:, lo:hi]
        acc = jnp.concatenate([pcarry_ref[:, lo:hi], pg], axis=0)
        shift = 1
        while shift < w:
            acc = acc + pltpu.roll(acc, shift, 0)
            shift *= 2
        cnt = jnp.minimum(pos + 1, w).astype(jnp.float32)
        pooled = acc[CARRY_ROWS:, :] / cnt - pg
        mix_ref[:, A_WIDTH + lo:A_WIDTH + hi] = _pool_project(pooled, g, bw_ref, bscale_ref).astype(jnp.bfloat16)
    tail = p[TM - CARRY_ROWS:, :]
    pcarry_ref[...] = tail
    pstate_ref[0] = tail


def _sample_mixers(u, v, p, aws_ref, abs_ref, st_ref, bw_ref, bscale_ref, mix_ref, pp_ref, vp_ref):
    n_pos = TM // SEQ_BLK
    for i in range(n_pos):
        vp_ref[0, i] = v[i * SEQ_BLK:(i + 1) * SEQ_BLK, :]
        pp_ref[0, i] = p[i * SEQ_BLK:(i + 1) * SEQ_BLK, :]

    for hd in range(N_HEADS):
        lo, hi = hd * HEAD_DIM, (hd + 1) * HEAD_DIM
        vplanes = [v[s * SEQ_BLK:(s + 1) * SEQ_BLK, lo:hi] for s in range(n_pos)]
        for i in range(n_pos):
            z = vplanes[0] * aws_ref[hd * 64 + i * 8]
            for s in range(1, i + 1):
                z = z + vplanes[s] * aws_ref[hd * 64 + i * 8 + s]
            z = z + abs_ref[hd * 8 + i]
            r0, r1 = i * SEQ_BLK, (i + 1) * SEQ_BLK
            mix_ref[r0:r1, lo:hi] = (u[r0:r1, lo:hi] * z).astype(jnp.bfloat16)

    for g, w in enumerate(POOL_WINDOWS):
        lo, hi = g * HEAD_DIM, (g + 1) * HEAD_DIM
        planes = [st_ref[0, k, :, lo:hi] for k in range(POOL_STATE)]
        planes += [p[i * SEQ_BLK:(i + 1) * SEQ_BLK, lo:hi] for i in range(n_pos)]
        pooled = []
        for i in range(n_pos):
            top = POOL_STATE + i
            s = planes[top - w + 1]
            for k in range(top - w + 2, top + 1):
                s = s + planes[k]
            pooled.append(s * (1.0 / w) - planes[top])
        pooled = jnp.concatenate(pooled, axis=0)
        mix_ref[:, A_WIDTH + lo:A_WIDTH + hi] = _pool_project(pooled, g, bw_ref, bscale_ref).astype(jnp.bfloat16)


def _route(h2, rwt_ref, rb_ref, su_ref, cnt_ref, meta_ref, gcol_ref):
    tm = h2.shape[0]
    h_hi = h2.astype(jnp.bfloat16)
    h_lo = (h2 - h_hi.astype(jnp.float32)).astype(jnp.bfloat16)
    s = (jnp.dot(h_hi, rwt_ref[...], preferred_element_type=jnp.float32)
         + jnp.dot(h_lo, rwt_ref[...], preferred_element_type=jnp.float32))
    st = s.T
    lt = st[0:N_ROUTER_ROWS, :] + st[N_ROUTER_ROWS:2 * N_ROUTER_ROWS, :] + rb_ref[...]
    row = lambda i: lt[i:i + 1, :]
    l1 = [row(i) for i in range(N_GROUPS)]
    m1 = jnp.maximum(jnp.maximum(l1[0], l1[1]), jnp.maximum(l1[2], l1[3]))
    grp = jnp.where(l1[0] == m1, 0, jnp.where(l1[1] == m1, 1, jnp.where(l1[2] == m1, 2, 3)))
    se = (jnp.exp(l1[0] - m1) + jnp.exp(l1[1] - m1)) + (jnp.exp(l1[2] - m1) + jnp.exp(l1[3] - m1))
    pg = 1.0 / se
    l2 = []
    for e in range(EXPERTS_PER_GROUP):
        c = [row(N_GROUPS + g * EXPERTS_PER_GROUP + e) for g in range(N_GROUPS)]
        l2.append(jnp.where(grp == 0, c[0], jnp.where(grp == 1, c[1], jnp.where(grp == 2, c[2], c[3]))))
    v0 = functools.reduce(jnp.maximum, l2)
    i0 = jnp.full_like(grp, EXPERTS_PER_GROUP - 1)
    for e in range(EXPERTS_PER_GROUP - 2, -1, -1):
        i0 = jnp.where(l2[e] == v0, e, i0)
    neg = jnp.float32(-jnp.inf)
    l2m = [jnp.where(i0 == e, neg, l2[e]) for e in range(EXPERTS_PER_GROUP)]
    v1 = functools.reduce(jnp.maximum, l2m)
    i1 = jnp.full_like(grp, EXPERTS_PER_GROUP - 1)
    for e in range(EXPERTS_PER_GROUP - 2, -1, -1):
        i1 = jnp.where((l2m[e] == v1) & (i0 != e), e, i1)
    d = jnp.exp(v1 - v0)
    g0 = pg / (1.0 + d)
    g1 = (pg * d) / (1.0 + d)
    e0 = grp * EXPERTS_PER_GROUP + i0
    e1 = grp * EXPERTS_PER_GROUP + i1

    eiota = lax.broadcasted_iota(jnp.int32, (N_EXPERTS, tm), 0)
    hit0 = eiota == e0
    hit1 = eiota == e1
    onehot = (hit0 | hit1).astype(jnp.bfloat16)
    prefix = jnp.dot(onehot, su_ref[...], preferred_element_type=jnp.float32)
    carry = cnt_ref[...]
    base = prefix + jnp.concatenate([carry] * (tm // LANES), axis=1)
    r0 = jnp.sum(jnp.where(hit0, base, 0.0), axis=0, keepdims=True)
    r1 = jnp.sum(jnp.where(hit1, base, 0.0), axis=0, keepdims=True)
    ones = jnp.ones((tm, LANES), jnp.bfloat16)
    cnt_ref[...] = carry + jnp.dot(onehot, ones, preferred_element_type=jnp.float32)

    meta_ref[0:1, :] = e0
    meta_ref[1:2, :] = e1
    meta_ref[2:3, :] = r0.astype(jnp.int32)
    meta_ref[3:4, :] = r1.astype(jnp.int32)
    meta_ref[4:8, :] = jnp.zeros((4, tm), jnp.int32)

    giota = lax.broadcasted_iota(jnp.int32, (LANES, tm), 0)
    gpad = jnp.where(giota == 0, g0, jnp.where(giota == 1, g1, 0.0))
    gcol_ref[...] = gpad.T


def _mixer_kernel(aws_s_ref, abs_s_ref,
                  xp_ref, xs_ref, st_ref, n1g_ref, win_ref, ang_ref, anb_ref, aws_ref, abias_ref, bw_ref, bscale_ref,
                  wout_ref, n2g_ref, rwt_ref, rb_ref, su_ref,
                  x1_ref, h2_ref, meta_ref, gcol_ref, counts_ref, pstate_ref, pp_ref, vp_ref,
                  mix_ref, pcarry_ref, cnt_ref, *, n_prompt_tiles, seq_tiles, part_tiles):
    i = pl.program_id(0)
    is_prompt = i < n_prompt_tiles
    j = i % seq_tiles

    @pl.when(i % part_tiles == 0)
    def _():
        cnt_ref[...] = jnp.zeros_like(cnt_ref)

    @pl.when(is_prompt & (j == 0))
    def _():
        pcarry_ref[...] = jnp.zeros_like(pcarry_ref)

    x = jnp.where(is_prompt, xp_ref[0], xs_ref[...])
    h = _rmsnorm(x, n1g_ref[...]).astype(jnp.bfloat16)
    proj = jnp.dot(h, win_ref[...], preferred_element_type=jnp.float32)
    uv = _gelu(proj[:, : 2 * A_WIDTH])
    u = uv[:, :A_WIDTH]
    v = _layernorm(uv[:, A_WIDTH:], ang_ref[...], anb_ref[...])
    p = proj[:, 2 * A_WIDTH:]

    @pl.when(is_prompt)
    def _():
        _prompt_mixers(j, u, v, p, aws_ref, abias_ref, bw_ref, bscale_ref, mix_ref, pcarry_ref, pstate_ref)

    @pl.when(jnp.logical_not(is_prompt))
    def _():
        _sample_mixers(u, v, p, aws_s_ref, abs_s_ref, st_ref, bw_ref, bscale_ref, mix_ref, pp_ref, vp_ref)

    x1 = x + jnp.dot(mix_ref[...], wout_ref[...], preferred_element_type=jnp.float32)
    x1_ref[...] = x1
    h2 = _rmsnorm(x1, n2g_ref[...])
    for s in range(ROW_TILES):
        h2_ref[pl.ds(s, TM, stride=ROW_TILES), :] = h2[:, s * LANES:(s + 1) * LANES]
    _route(h2, rwt_ref, rb_ref, su_ref, cnt_ref, meta_ref, gcol_ref)
    counts_ref[0] = cnt_ref[...]


def _expert_kernel(te_ref, nv_ref, nrows_ref,
                   src_ref, dst_ref,
                   h2_ref, w1_ref, w3_ref, w2_ref,
                   out_hbm,
                   xt, obuf, w1b, w3b, w2b, ssem):
    i = pl.program_id(1)
    g = pl.program_id(0) * pl.num_programs(1) + i
    nv = nv_ref[pl.program_id(0)]
    slot = i % 2
    slab = lambda r: pl.ds(pl.multiple_of(r * ROW_TILES, ROW_TILES), ROW_TILES)

    def scatter_copy(r, src_slot):
        return pltpu.make_async_copy(obuf.at[src_slot, slab(r), :], out_hbm.at[dst_ref[0, 0, r]], ssem.at[src_slot])

    def wait_scatter(tile, src_slot):
        n = nrows_ref[tile] * ROW_TILES
        pltpu.make_async_copy(obuf.at[1 - src_slot, pl.ds(0, n), :], obuf.at[src_slot, pl.ds(0, n), :],
                              ssem.at[src_slot]).wait()

    @pl.when(i < nv)
    def _():
        @pl.when((i == 0) | (te_ref[g] != te_ref[jnp.maximum(g - 1, 0)]))
        def _():
            w1b[...] = w1_ref[0].astype(jnp.bfloat16)
            w3b[...] = w3_ref[0].astype(jnp.bfloat16)
            w2b[...] = w2_ref[0].astype(jnp.bfloat16)

        for r in range(TE):
            xt[r * ROW_TILES:(r + 1) * ROW_TILES, :] = h2_ref[slab(src_ref[0, 0, r]), :]

        @pl.when(i >= 2)
        def _():
            wait_scatter(g - 2, slot)

        x = jnp.concatenate([_row_slab(xt, s, TE) for s in range(ROW_TILES)], axis=1).astype(jnp.bfloat16)
        a = jnp.dot(x, w1b[...], preferred_element_type=jnp.float32)
        b = jnp.dot(x, w3b[...], preferred_element_type=jnp.float32)
        h = (a * jax.nn.sigmoid(a)) * b
        o = jnp.dot(h.astype(jnp.bfloat16), w2b[...], preferred_element_type=jnp.float32)
        os = obuf.at[slot]
        for s in range(ROW_TILES):
            os[pl.ds(s, TE, stride=ROW_TILES), :] = o[:, s * LANES:(s + 1) * LANES]

        n = nrows_ref[g]
        n_full = n // DMA_UNROLL

        def body(rb, carry):
            for uu in range(DMA_UNROLL):
                r = rb * DMA_UNROLL + uu
                scatter_copy(r, slot).start(priority=1)
            return carry
        lax.fori_loop(0, n_full, body, 0)
        for uu in range(DMA_UNROLL - 1):
            r = n_full * DMA_UNROLL + uu

            @pl.when(r < n)
            def _():
                scatter_copy(r, slot).start(priority=1)

        @pl.when(i == nv - 1)
        def _():
            @pl.when(i >= 1)
            def _():
                wait_scatter(g - 1, 1 - slot)
            wait_scatter(g, slot)


def _combine_kernel(x1_ref, gcol_ref, nfg_ref, o0_ref, o1_ref, yp_ref, ys_ref, *, n_prompt_tiles):
    i = pl.program_id(0)
    o0 = jnp.concatenate([_row_slab(o0_ref, s, TM) for s in range(ROW_TILES)], axis=1)
    o1 = jnp.concatenate([_row_slab(o1_ref, s, TM) for s in range(ROW_TILES)], axis=1)
    g = gcol_ref[...]
    moe = g[:, 0:1] * o0 + g[:, 1:2] * o1
    y = _rmsnorm(x1_ref[...] + moe, nfg_ref[...])

    @pl.when(i < n_prompt_tiles)
    def _():
        yp_ref[...] = y

    @pl.when(i >= n_prompt_tiles)
    def _():
        ys_ref[...] = y


def _const_spec(shape):
    return pl.BlockSpec(shape, lambda *_: (0,) * len(shape))


def kernel(x_prompt, x_sample, state_pool, norm1_g, w_in, a_norm_g, a_norm_b, a_ws, a_bs, b_w, b_scale, w_out,
           norm2_g, r1_w, r1_b, r2_w, r2_b, exp_w1, exp_w3, exp_w2, normf_g):
    f32, bf16, i32 = jnp.float32, jnp.bfloat16, jnp.int32
    n_batch, seq, _ = x_prompt.shape
    dec_batch, dec_seq, _ = x_sample.shape
    assert norm1_g.shape[0] == 1 and seq % TM == 0 and TM % CHUNK == 0
    assert dec_seq * SEQ_BLK == TM and dec_batch % SEQ_BLK == 0 and dec_seq <= CHUNK
    t_prompt = n_batch * seq
    t_sample = dec_batch * dec_seq
    t_total = t_prompt + t_sample
    n_tok_tiles = t_total // TM
    n_prompt_tiles = t_prompt // TM
    n_sample_tiles = t_sample // TM
    seq_tiles = seq // TM
    n_assign = TOP_K * t_total
    plane_rows = t_total * ROW_TILES
    assert n_tok_tiles % N_PARTS == 0
    part_tiles = n_tok_tiles // N_PARTS
    t_part = part_tiles * TM
    n_exp_tiles = -(-(TOP_K * t_part + N_EXPERTS * (TE - 1)) // TE)
    p_rows = n_exp_tiles * TE

    n1g = norm1_g[0][None, :]
    n2g = norm2_g[0][None, :]
    nfg = normf_g[None, :]
    win = w_in[0].astype(bf16)
    wout = w_out[0].astype(bf16)
    ang = a_norm_g[0][None, :]
    anb = a_norm_b[0][None, :]
    bw = b_w[0].astype(bf16)
    bscale = b_scale[0][None, :]
    abias = jnp.repeat(a_bs[0][:, :CHUNK].T, HEAD_DIM, axis=1)
    rw = jnp.concatenate([r1_w[0], r2_w[0].transpose(1, 0, 2).reshape(D_MODEL, N_EXPERTS),
                          jnp.zeros((D_MODEL, N_ROUTER_ROWS - N_GROUPS - N_EXPERTS), f32)], axis=1)
    rw_hi = rw.astype(bf16)
    rw_lo = (rw - rw_hi.astype(f32)).astype(bf16)
    rwt = jnp.concatenate([rw_hi, rw_lo, jnp.zeros((D_MODEL, LANES - 2 * N_ROUTER_ROWS), bf16)], axis=1)
    rbias = jnp.concatenate([r1_b[0], r2_b[0].reshape(-1),
                             jnp.zeros((N_ROUTER_ROWS - N_GROUPS - N_EXPERTS,), f32)])
    rb = jnp.broadcast_to(rbias[:, None], (N_ROUTER_ROWS, TM))
    su = (jnp.arange(TM)[:, None] < jnp.arange(TM)[None, :]).astype(bf16)
    xs_planes = (x_sample.reshape(n_sample_tiles, SEQ_BLK, dec_seq, D_MODEL)
                 .transpose(0, 2, 1, 3).reshape(t_sample, D_MODEL))
    st_planes = state_pool[0].reshape(n_sample_tiles, SEQ_BLK, POOL_STATE, B_WIDTH).transpose(0, 2, 1, 3)
    aws_s = a_ws[0][:, :dec_seq, :dec_seq].reshape(-1)
    abs_s = a_bs[0][:, :dec_seq].reshape(-1)

    cparams = pltpu.CompilerParams(dimension_semantics=("arbitrary",), vmem_limit_bytes=VMEM_LIMIT)

    p_tile = lambda i: jnp.minimum(i, n_prompt_tiles - 1)
    s_tile = lambda i: jnp.maximum(i - n_prompt_tiles, 0)
    plane_shape = jax.ShapeDtypeStruct((n_sample_tiles, dec_seq, SEQ_BLK, A_WIDTH), f32)
    x1, h2, meta, gcol, counts, pstate, pplanes, vplanes = pl.pallas_call(
        functools.partial(_mixer_kernel, n_prompt_tiles=n_prompt_tiles, seq_tiles=seq_tiles, part_tiles=part_tiles),
        grid_spec=pltpu.PrefetchScalarGridSpec(
            num_scalar_prefetch=2,
            grid=(n_tok_tiles,),
            in_specs=[
                pl.BlockSpec((1, TM, D_MODEL), lambda i, *_: (p_tile(i) // seq_tiles, p_tile(i) % seq_tiles, 0)),
                pl.BlockSpec((TM, D_MODEL), lambda i, *_: (s_tile(i), 0)),
                pl.BlockSpec((1, POOL_STATE, SEQ_BLK, B_WIDTH), lambda i, *_: (s_tile(i), 0, 0, 0)),
                _const_spec((1, D_MODEL)), _const_spec((D_MODEL, 3 * A_WIDTH)), _const_spec((1, A_WIDTH)),
                _const_spec((1, A_WIDTH)), _const_spec((N_HEADS, CHUNK, CHUNK)), _const_spec((CHUNK, A_WIDTH)),
                _const_spec((N_GROUPS, HEAD_DIM, HEAD_DIM)), _const_spec((1, B_WIDTH)),
                _const_spec((D_MODEL, D_MODEL)), _const_spec((1, D_MODEL)), _const_spec((D_MODEL, LANES)),
                _const_spec((N_ROUTER_ROWS, TM)), _const_spec((TM, TM)),
            ],
            out_specs=[
                pl.BlockSpec((TM, D_MODEL), lambda i, *_: (i, 0)),
                pl.BlockSpec((TM * ROW_TILES, LANES), lambda i, *_: (i, 0)),
                pl.BlockSpec((SUBLANES, TM), lambda i, *_: (0, i)),
                pl.BlockSpec((TM, LANES), lambda i, *_: (i, 0)),
                pl.BlockSpec((1, N_EXPERTS, LANES), lambda i, *_: (i // part_tiles, 0, 0)),
                pl.BlockSpec((1, CARRY_ROWS, B_WIDTH), lambda i, *_: (p_tile(i) // seq_tiles, 0, 0)),
                pl.BlockSpec((1, dec_seq, SEQ_BLK, A_WIDTH), lambda i, *_: (s_tile(i), 0, 0, 0)),
                pl.BlockSpec((1, dec_seq, SEQ_BLK, A_WIDTH), lambda i, *_: (s_tile(i), 0, 0, 0)),
            ],
            scratch_shapes=[pltpu.VMEM((TM, D_MODEL), bf16), pltpu.VMEM((CARRY_ROWS, B_WIDTH), f32),
                            pltpu.VMEM((N_EXPERTS, LANES), f32)],
        ),
        out_shape=[
            jax.ShapeDtypeStruct((t_total, D_MODEL), f32),
            jax.ShapeDtypeStruct((plane_rows, LANES), f32),
            jax.ShapeDtypeStruct((SUBLANES, t_total), i32),
            jax.ShapeDtypeStruct((t_total, LANES), f32),
            jax.ShapeDtypeStruct((N_PARTS, N_EXPERTS, LANES), f32),
            jax.ShapeDtypeStruct((n_batch, CARRY_ROWS, B_WIDTH), f32),
            plane_shape, plane_shape,
        ],
        compiler_params=cparams,
        name="mixer",
    )(aws_s, abs_s, x_prompt, xs_planes, st_planes, n1g, win, ang, anb, a_ws[0][:, :CHUNK, :CHUNK], abias, bw,
      bscale, wout, n2g, rwt, rb, su)

    cnt = counts[:, :, 0].astype(i32)
    padded = (cnt + (TE - 1)) // TE * TE
    pad_end = jnp.cumsum(padded, axis=1)
    pad_start = pad_end - padded
    tok = jnp.arange(t_total, dtype=i32)
    part = tok // t_part
    dest = jnp.stack([pad_start[part, meta[0]] + meta[2], pad_start[part, meta[1]] + meta[3]], axis=-1)
    dest = (part[:, None] * p_rows + dest).reshape(-1)
    asg = jnp.full((N_PARTS * p_rows,), -1, i32).at[dest].set(jnp.arange(n_assign, dtype=i32), unique_indices=True)
    row_part = jnp.arange(N_PARTS * p_rows, dtype=i32) // p_rows
    row_src = jnp.where(asg >= 0, (asg >> 1) - row_part * t_part, 0).reshape(N_PARTS * n_exp_tiles, 1, TE)
    row_dst = jnp.where(asg >= 0, (asg & 1) * t_total + (asg >> 1), 0).reshape(N_PARTS * n_exp_tiles, 1, TE)
    n_valid = jnp.maximum(pad_end[:, -1] // TE, 1).astype(i32)
    tile_ids = jnp.minimum(jnp.arange(n_exp_tiles, dtype=i32)[None, :], n_valid[:, None] - 1)
    tile_e = jnp.sum((pad_end[:, None, :] <= (tile_ids * TE)[:, :, None]).astype(i32), axis=2)
    tile_e = jnp.minimum(tile_e, N_EXPERTS - 1)
    seg_start = jnp.take_along_axis(pad_start, tile_e, axis=1)
    seg_cnt = jnp.take_along_axis(cnt, tile_e, axis=1)
    tile_rows = jnp.clip(seg_cnt - (tile_ids * TE - seg_start), 0, TE).astype(i32)

    flat = lambda h, i: h * n_exp_tiles + i
    cur_blk = lambda h, i, te, nv, nr: (flat(h, jnp.minimum(i, nv[h] - 1)), 0, 0)
    w_blk = lambda h, i, te, nv, nr: (te[flat(h, i)], 0, 0)
    smem_blk = lambda imap: pl.BlockSpec((1, 1, TE), imap, memory_space=pltpu.SMEM)
    out_tok = pl.pallas_call(
        _expert_kernel,
        grid_spec=pltpu.PrefetchScalarGridSpec(
            num_scalar_prefetch=3,
            grid=(N_PARTS, n_exp_tiles),
            in_specs=[smem_blk(cur_blk), smem_blk(cur_blk),
                      pl.BlockSpec((t_part * ROW_TILES, LANES), lambda h, i, *_: (h, 0),
                                   pipeline_mode=pl.Buffered(1)),
                      pl.BlockSpec((1, D_MODEL, D_EXPERT), w_blk),
                      pl.BlockSpec((1, D_MODEL, D_EXPERT), w_blk),
                      pl.BlockSpec((1, D_EXPERT, D_MODEL), w_blk)],
            out_specs=pl.BlockSpec(memory_space=pl.ANY),
            scratch_shapes=[pltpu.VMEM((TE * ROW_TILES, LANES), f32),
                            pltpu.VMEM((2, TE * ROW_TILES, LANES), f32),
                            pltpu.VMEM((D_MODEL, D_EXPERT), bf16), pltpu.VMEM((D_MODEL, D_EXPERT), bf16),
                            pltpu.VMEM((D_EXPERT, D_MODEL), bf16),
                            pltpu.SemaphoreType.DMA((2,))],
        ),
        out_shape=jax.ShapeDtypeStruct((TOP_K * t_total, ROW_TILES, LANES), f32),
        compiler_params=pltpu.CompilerParams(dimension_semantics=("arbitrary", "arbitrary"),
                                             vmem_limit_bytes=EXPERT_VMEM_LIMIT),
        name="moe_experts",
    )(tile_e.reshape(-1), n_valid, tile_rows.reshape(-1), row_src, row_dst, h2, exp_w1[0], exp_w3[0], exp_w2[0])
    out_tok = out_tok.reshape(TOP_K * plane_rows, LANES)

    y_prompt, y_sample = pl.pallas_call(
        functools.partial(_combine_kernel, n_prompt_tiles=n_prompt_tiles),
        grid=(n_tok_tiles,),
        in_specs=[pl.BlockSpec((TM, D_MODEL), lambda i: (i, 0)),
                  pl.BlockSpec((TM, LANES), lambda i: (i, 0)),
                  _const_spec((1, D_MODEL)),
                  pl.BlockSpec((TM * ROW_TILES, LANES), lambda i: (i, 0)),
                  pl.BlockSpec((TM * ROW_TILES, LANES), lambda i: (n_tok_tiles + i, 0))],
        out_specs=[pl.BlockSpec((TM, D_MODEL), lambda i: (p_tile(i), 0)),
                   pl.BlockSpec((TM, D_MODEL), lambda i: (s_tile(i), 0))],
        out_shape=[jax.ShapeDtypeStruct((t_prompt, D_MODEL), f32), jax.ShapeDtypeStruct((t_sample, D_MODEL), f32)],
        compiler_params=cparams,
        name="moe_combine",
    )(x1, gcol, nfg, out_tok, out_tok)

    unplane = lambda a: a.transpose(0, 2, 1, 3).reshape(dec_batch, dec_seq, a.shape[-1])
    y_prompt = y_prompt.reshape(n_batch, seq, D_MODEL)
    y_sample = unplane(y_sample.reshape(n_sample_tiles, dec_seq, SEQ_BLK, D_MODEL))
    pool_state_prompt = pstate[None, :, CARRY_ROWS - POOL_STATE:, :]
    p_s = unplane(pplanes)
    pool_state_sample = jnp.concatenate([state_pool[0], p_s], axis=1)[None, :, -POOL_STATE:, :]
    chunk_v_sample = unplane(vplanes)[None]
    return (y_prompt, y_sample, pool_state_prompt, pool_state_sample, chunk_v_sample)
```

```python
import functools
import math

import jax
import jax.numpy as jnp
from jax import lax
from jax.experimental import pallas as pl
from jax.experimental.pallas import tpu as pltpu

D_MODEL = 1024
A_WIDTH = 512
B_WIDTH = 512
N_HEADS = 4
HEAD_DIM = 128
CHUNK = 128
POOL_WINDOWS = (2, 4, 8, 16)
POOL_STATE = 15
N_GROUPS = 4
EXPERTS_PER_GROUP = 8
N_EXPERTS = 32
TOP_K = 2
D_EXPERT = 512
EPS = 1e-6

SUBLANES = 8
LANES = 128
ROW_TILES = D_MODEL // LANES

TM = 256
TE = 256
SEQ_BLK = 32
N_ROUTER_ROWS = 40
CARRY_ROWS = 16
DMA_UNROLL = 8
VMEM_LIMIT = 48 * 1024 * 1024
N_PARTS = 2
EXPERT_VMEM_LIMIT = 62 * 1024 * 1024

_INV_SQRT2 = 1.0 / math.sqrt(2.0)


def _rmsnorm(x, g):
    r = lax.rsqrt(jnp.mean(x * x, axis=-1, keepdims=True) + EPS)
    return (x * r) * g


def _gelu(x):
    return 0.5 * x * (1.0 + lax.erf(x * _INV_SQRT2))


def _layernorm(x, g, b):
    mu = jnp.mean(x, axis=-1, keepdims=True)
    xc = x - mu
    var = jnp.mean(xc * xc, axis=-1, keepdims=True)
    return (xc * lax.rsqrt(var + EPS)) * g + b


def _row_slab(ref, s, n):
    return ref[pl.ds(s, n, stride=ROW_TILES), :]


def _pool_project(pooled, g, bw_ref, bscale_ref):
    lo, hi = g * HEAD_DIM, (g + 1) * HEAD_DIM
    hb = jnp.dot(pooled.astype(jnp.bfloat16), bw_ref[g], preferred_element_type=jnp.float32)
    return hb * bscale_ref[:, lo:hi]


def _prompt_mixers(j, u, v, p, aws_ref, abias_ref, bw_ref, bscale_ref, mix_ref, pcarry_ref, pstate_ref):
    tri = (lax.broadcasted_iota(jnp.int32, (CHUNK, CHUNK), 0)
           >= lax.broadcasted_iota(jnp.int32, (CHUNK, CHUNK), 1))
    vb = v.astype(jnp.bfloat16)
    for hd in range(N_HEADS):
        lo, hi = hd * HEAD_DIM, (hd + 1) * HEAD_DIM
        w = jnp.where(tri, aws_ref[hd], 0.0).astype(jnp.bfloat16)
        for c in range(TM // CHUNK):
            r0, r1 = c * CHUNK, (c + 1) * CHUNK
            z = jnp.dot(w, vb[r0:r1, lo:hi], preferred_element_type=jnp.float32) + abias_ref[:, lo:hi]
            mix_ref[r0:r1, lo:hi] = (u[r0:r1, lo:hi] * z).astype(jnp.bfloat16)

    pos = j * TM + lax.broadcasted_iota(jnp.int32, (TM, LANES), 0)
    for g, w in enumerate(POOL_WINDOWS):
        lo, hi = g * HEAD_DIM, (g + 1) * HEAD_DIM
        pg = p[:, lo:hi]
        acc = jnp.concatenate([pcarry_ref[:, lo:hi], pg], axis=0)
        shift = 1
        while shift < w:
            acc = acc + pltpu.roll(acc, shift, 0)
            shift *= 2
        cnt = jnp.minimum(pos + 1, w).astype(jnp.float32)
        pooled = acc[CARRY_ROWS:, :] / cnt - pg
        mix_ref[:, A_WIDTH + lo:A_WIDTH + hi] = _pool_project(pooled, g, bw_ref, bscale_ref).astype(jnp.bfloat16)
    tail = p[TM - CARRY_ROWS:, :]
    pcarry_ref[...] = tail
    pstate_ref[0] = tail


def _sample_mixers(u, v, p, aws_ref, abs_ref, st_ref, bw_ref, bscale_ref, mix_ref, pp_ref, vp_ref):
    n_pos = TM // SEQ_BLK
    for i in range(n_pos):
        vp_ref[0, i] = v[i * SEQ_BLK:(i + 1) * SEQ_BLK, :]
        pp_ref[0, i] = p[i * SEQ_BLK:(i + 1) * SEQ_BLK, :]

    for hd in range(N_HEADS):
        lo, hi = hd * HEAD_DIM, (hd + 1) * HEAD_DIM
        vplanes = [v[s * SEQ_BLK:(s + 1) * SEQ_BLK, lo:hi] for s in range(n_pos)]
        for i in range(n_pos):
            z = vplanes[0] * aws_ref[hd * 64 + i * 8]
            for s in range(1, i + 1):
                z = z + vplanes[s] * aws_ref[hd * 64 + i * 8 + s]
            z = z + abs_ref[hd * 8 + i]
            r0, r1 = i * SEQ_BLK, (i + 1) * SEQ_BLK
            mix_ref[r0:r1, lo:hi] = (u[r0:r1, lo:hi] * z).astype(jnp.bfloat16)

    for g, w in enumerate(POOL_WINDOWS):
        lo, hi = g * HEAD_DIM, (g + 1) * HEAD_DIM
        planes = [st_ref[0, k, :, lo:hi] for k in range(POOL_STATE)]
        planes += [p[i * SEQ_BLK:(i + 1) * SEQ_BLK, lo:hi] for i in range(n_pos)]
        pooled = []
        for i in range(n_pos):
            top = POOL_STATE + i
            s = planes[top - w + 1]
            for k in range(top - w + 2, top + 1):
                s = s + planes[k]
            pooled.append(s * (1.0 / w) - planes[top])
        pooled = jnp.concatenate(pooled, axis=0)
        mix_ref[:, A_WIDTH + lo:A_WIDTH + hi] = _pool_project(pooled, g, bw_ref, bscale_ref).astype(jnp.bfloat16)


def _route(h2, rwt_ref, rb_ref, su_ref, cnt_ref, meta_ref, gcol_ref):
    tm = h2.shape[0]
    h_hi = h2.astype(jnp.bfloat16)
    h_lo = (h2 - h_hi.astype(jnp.float32)).astype(jnp.bfloat16)
    s = (jnp.dot(h_hi, rwt_ref[...], preferred_element_type=jnp.float32)
         + jnp.dot(h_lo, rwt_ref[...], preferred_element_type=jnp.float32))
    st = s.T
    lt = st[0:N_ROUTER_ROWS, :] + st[N_ROUTER_ROWS:2 * N_ROUTER_ROWS, :] + rb_ref[...]
    row = lambda i: lt[i:i + 1, :]
    l1 = [row(i) for i in range(N_GROUPS)]
    m1 = jnp.maximum(jnp.maximum(l1[0], l1[1]), jnp.maximum(l1[2], l1[3]))
    grp = jnp.where(l1[0] == m1, 0, jnp.where(l1[1] == m1, 1, jnp.where(l1[2] == m1, 2, 3)))
    se = (jnp.exp(l1[0] - m1) + jnp.exp(l1[1] - m1)) + (jnp.exp(l1[2] - m1) + jnp.exp(l1[3] - m1))
    pg = 1.0 / se
    l2 = []
    for e in range(EXPERTS_PER_GROUP):
        c = [row(N_GROUPS + g * EXPERTS_PER_GROUP + e) for g in range(N_GROUPS)]
        l2.append(jnp.where(grp == 0, c[0], jnp.where(grp == 1, c[1], jnp.where(grp == 2, c[2], c[3]))))
    v0 = functools.reduce(jnp.maximum, l2)
    i0 = jnp.full_like(grp, EXPERTS_PER_GROUP - 1)
    for e in range(EXPERTS_PER_GROUP - 2, -1, -1):
        i0 = jnp.where(l2[e] == v0, e, i0)
    neg = jnp.float32(-jnp.inf)
    l2m = [jnp.where(i0 == e, neg, l2[e]) for e in range(EXPERTS_PER_GROUP)]
    v1 = functools.reduce(jnp.maximum, l2m)
    i1 = jnp.full_like(grp, EXPERTS_PER_GROUP - 1)
    for e in range(EXPERTS_PER_GROUP - 2, -1, -1):
        i1 = jnp.where((l2m[e] == v1) & (i0 != e), e, i1)
    d = jnp.exp(v1 - v0)
    g0 = pg / (1.0 + d)
    g1 = (pg * d) / (1.0 + d)
    e0 = grp * EXPERTS_PER_GROUP + i0
    e1 = grp * EXPERTS_PER_GROUP + i1

    eiota = lax.broadcasted_iota(jnp.int32, (N_EXPERTS, tm), 0)
    hit0 = eiota == e0
    hit1 = eiota == e1
    onehot = (hit0 | hit1).astype(jnp.bfloat16)
    prefix = jnp.dot(onehot, su_ref[...], preferred_element_type=jnp.float32)
    carry = cnt_ref[...]
    base = prefix + jnp.concatenate([carry] * (tm // LANES), axis=1)
    r0 = jnp.sum(jnp.where(hit0, base, 0.0), axis=0, keepdims=True)
    r1 = jnp.sum(jnp.where(hit1, base, 0.0), axis=0, keepdims=True)
    ones = jnp.ones((tm, LANES), jnp.bfloat16)
    cnt_ref[...] = carry + jnp.dot(onehot, ones, preferred_element_type=jnp.float32)

    meta_ref[0:1, :] = e0
    meta_ref[1:2, :] = e1
    meta_ref[2:3, :] = r0.astype(jnp.int32)
    meta_ref[3:4, :] = r1.astype(jnp.int32)
    meta_ref[4:8, :] = jnp.zeros((4, tm), jnp.int32)

    giota = lax.broadcasted_iota(jnp.int32, (LANES, tm), 0)
    gpad = jnp.where(giota == 0, g0, jnp.where(giota == 1, g1, 0.0))
    gcol_ref[...] = gpad.T


def _mixer_kernel(aws_s_ref, abs_s_ref,
                  xp_ref, xs_ref, st_ref, n1g_ref, win_ref, ang_ref, anb_ref, aws_ref, abias_ref, bw_ref, bscale_ref,
                  wout_ref, n2g_ref, rwt_ref, rb_ref, su_ref,
                  x1_ref, h2_ref, meta_ref, gcol_ref, counts_ref, pstate_ref, pp_ref, vp_ref,
                  mix_ref, pcarry_ref, cnt_ref, *, n_prompt_tiles, seq_tiles, part_tiles):
    i = pl.program_id(0)
    is_prompt = i < n_prompt_tiles
    j = i % seq_tiles

    @pl.when(i % part_tiles == 0)
    def _():
        cnt_ref[...] = jnp.zeros_like(cnt_ref)

    @pl.when(is_prompt & (j == 0))
    def _():
        pcarry_ref[...] = jnp.zeros_like(pcarry_ref)

    x = jnp.where(is_prompt, xp_ref[0], xs_ref[...])
    h = _rmsnorm(x, n1g_ref[...]).astype(jnp.bfloat16)
    proj = jnp.dot(h, win_ref[...], preferred_element_type=jnp.float32)
    uv = _gelu(proj[:, : 2 * A_WIDTH])
    u = uv[:, :A_WIDTH]
    v = _layernorm(uv[:, A_WIDTH:], ang_ref[...], anb_ref[...])
    p = proj[:, 2 * A_WIDTH:]

    @pl.when(is_prompt)
    def _():
        _prompt_mixers(j, u, v, p, aws_ref, abias_ref, bw_ref, bscale_ref, mix_ref, pcarry_ref, pstate_ref)

    @pl.when(jnp.logical_not(is_prompt))
    def _():
        _sample_mixers(u, v, p, aws_s_ref, abs_s_ref, st_ref, bw_ref, bscale_ref, mix_ref, pp_ref, vp_ref)

    x1 = x + jnp.dot(mix_ref[...], wout_ref[...], preferred_element_type=jnp.float32)
    x1_ref[...] = x1
    h2 = _rmsnorm(x1, n2g_ref[...])
    for s in range(ROW_TILES):
        h2_ref[pl.ds(s, TM, stride=ROW_TILES), :] = h2[:, s * LANES:(s + 1) * LANES]
    _route(h2, rwt_ref, rb_ref, su_ref, cnt_ref, meta_ref, gcol_ref)
    counts_ref[0] = cnt_ref[...]


def _expert_kernel(te_ref, nv_ref, nrows_ref,
                   src_ref, dst_ref,
                   h2_ref, w1_ref, w3_ref, w2_ref,
                   out_hbm,
                   xt, obuf, w1b, w3b, w2b, ssem):
    i = pl.program_id(1)
    g = pl.program_id(0) * pl.num_programs(1) + i
    nv = nv_ref[pl.program_id(0)]
    slot = i % 2
    slab = lambda r: pl.ds(pl.multiple_of(r * ROW_TILES, ROW_TILES), ROW_TILES)

    def scatter_copy(r, src_slot):
        return pltpu.make_async_copy(obuf.at[src_slot, slab(r), :], out_hbm.at[dst_ref[0, 0, r]], ssem.at[src_slot])

    def wait_scatter(tile, src_slot):
        n = nrows_ref[tile] * ROW_TILES
        pltpu.make_async_copy(obuf.at[1 - src_slot, pl.ds(0, n), :], obuf.at[src_slot, pl.ds(0, n), :],
                              ssem.at[src_slot]).wait()

    @pl.when(i < nv)
    def _():
        @pl.when((i == 0) | (te_ref[g] != te_ref[jnp.maximum(g - 1, 0)]))
        def _():
            w1b[...] = w1_ref[0].astype(jnp.bfloat16)
            w3b[...] = w3_ref[0].astype(jnp.bfloat16)
            w2b[...] = w2_ref[0].astype(jnp.bfloat16)

        for r in range(TE):
            xt[r * ROW_TILES:(r + 1) * ROW_TILES, :] = h2_ref[slab(src_ref[0, 0, r]), :]

        @pl.when(i >= 2)
        def _():
            wait_scatter(g - 2, slot)

        x = jnp.concatenate([_row_slab(xt, s, TE) for s in range(ROW_TILES)], axis=1).astype(jnp.bfloat16)
        a = jnp.dot(x, w1b[...], preferred_element_type=jnp.float32)
        b = jnp.dot(x, w3b[...], preferred_element_type=jnp.float32)
        h = (a * jax.nn.sigmoid(a)) * b
        o = jnp.dot(h.astype(jnp.bfloat16), w2b[...], preferred_element_type=jnp.float32)
        os = obuf.at[slot]
        for s in range(ROW_TILES):
            os[pl.ds(s, TE, stride=ROW_TILES), :] = o[:, s * LANES:(s + 1) * LANES]

        n = nrows_ref[g]
        n_full = n // DMA_UNROLL

        def body(rb, carry):
            for uu in range(DMA_UNROLL):
                r = rb * DMA_UNROLL + uu
                scatter_copy(r, slot).start(priority=1)
            return carry
        lax.fori_loop(0, n_full, body, 0)
        for uu in range(DMA_UNROLL - 1):
            r = n_full * DMA_UNROLL + uu

            @pl.when(r < n)
            def _():
                scatter_copy(r, slot).start(priority=1)

        @pl.when(i == nv - 1)
        def _():
            @pl.when(i >= 1)
            def _():
                wait_scatter(g - 1, 1 - slot)
            wait_scatter(g, slot)


def _combine_kernel(x1_ref, gcol_ref, nfg_ref, o0_ref, o1_ref, yp_ref, ys_ref, *, n_prompt_tiles):
    i = pl.program_id(0)
    o0 = jnp.concatenate([_row_slab(o0_ref, s, TM) for s in range(ROW_TILES)], axis=1)
    o1 = jnp.concatenate([_row_slab(o1_ref, s, TM) for s in range(ROW_TILES)], axis=1)
    g = gcol_ref[...]
    moe = g[:, 0:1] * o0 + g[:, 1:2] * o1
    y = _rmsnorm(x1_ref[...] + moe, nfg_ref[...])

    @pl.when(i < n_prompt_tiles)
    def _():
        yp_ref[...] = y

    @pl.when(i >= n_prompt_tiles)
    def _():
        ys_ref[...] = y


def _const_spec(shape):
    return pl.BlockSpec(shape, lambda *_: (0,) * len(shape))


def kernel(x_prompt, x_sample, state_pool, norm1_g, w_in, a_norm_g, a_norm_b, a_ws, a_bs, b_w, b_scale, w_out,
           norm2_g, r1_w, r1_b, r2_w, r2_b, exp_w1, exp_w3, exp_w2, normf_g):
    f32, bf16, i32 = jnp.float32, jnp.bfloat16, jnp.int32
    n_batch, seq, _ = x_prompt.shape
    dec_batch, dec_seq, _ = x_sample.shape
    assert norm1_g.shape[0] == 1 and seq % TM == 0 and TM % CHUNK == 0
    assert dec_seq * SEQ_BLK == TM and dec_batch % SEQ_BLK == 0 and dec_seq <= CHUNK
    t_prompt = n_batch * seq
    t_sample = dec_batch * dec_seq
    t_total = t_prompt + t_sample
    n_tok_tiles = t_total // TM
    n_prompt_tiles = t_prompt // TM
    n_sample_tiles = t_sample // TM
    seq_tiles = seq // TM
    n_assign = TOP_K * t_total
    plane_rows = t_total * ROW_TILES
    assert n_tok_tiles % N_PARTS == 0
    part_tiles = n_tok_tiles // N_PARTS
    t_part = part_tiles * TM
    n_exp_tiles = -(-(TOP_K * t_part + N_EXPERTS * (TE - 1)) // TE)
    p_rows = n_exp_tiles * TE

    n1g = norm1_g[0][None, :]
    n2g = norm2_g[0][None, :]
    nfg = normf_g[None, :]
    win = w_in[0].astype(bf16)
    wout = w_out[0].astype(bf16)
    ang = a_norm_g[0][None, :]
    anb = a_norm_b[0][None, :]
    bw = b_w[0].astype(bf16)
    bscale = b_scale[0][None, :]
    abias = jnp.repeat(a_bs[0][:, :CHUNK].T, HEAD_DIM, axis=1)
    rw = jnp.concatenate([r1_w[0], r2_w[0].transpose(1, 0, 2).reshape(D_MODEL, N_EXPERTS),
                          jnp.zeros((D_MODEL, N_ROUTER_ROWS - N_GROUPS - N_EXPERTS), f32)], axis=1)
    rw_hi = rw.astype(bf16)
    rw_lo = (rw - rw_hi.astype(f32)).astype(bf16)
    rwt = jnp.concatenate([rw_hi, rw_lo, jnp.zeros((D_MODEL, LANES - 2 * N_ROUTER_ROWS), bf16)], axis=1)
    rbias = jnp.concatenate([r1_b[0], r2_b[0].reshape(-1),
                             jnp.zeros((N_ROUTER_ROWS - N_GROUPS - N_EXPERTS,), f32)])
    rb = jnp.broadcast_to(rbias[:, None], (N_ROUTER_ROWS, TM))
    su = (jnp.arange(TM)[:, None] < jnp.arange(TM)[None, :]).astype(bf16)
    xs_planes = (x_sample.reshape(n_sample_tiles, SEQ_BLK, dec_seq, D_MODEL)
                 .transpose(0, 2, 1, 3).reshape(t_sample, D_MODEL))
    st_planes = state_pool[0].reshape(n_sample_tiles, SEQ_BLK, POOL_STATE, B_WIDTH).transpose(0, 2, 1, 3)
    aws_s = a_ws[0][:, :dec_seq, :dec_seq].reshape(-1)
    abs_s = a_bs[0][:, :dec_seq].reshape(-1)

    cparams = pltpu.CompilerParams(dimension_semantics=("arbitrary",), vmem_limit_bytes=VMEM_LIMIT)

    p_tile = lambda i: jnp.minimum(i, n_prompt_tiles - 1)
    s_tile = lambda i: jnp.maximum(i - n_prompt_tiles, 0)
    plane_shape = jax.ShapeDtypeStruct((n_sample_tiles, dec_seq, SEQ_BLK, A_WIDTH), f32)
    x1, h2, meta, gcol, counts, pstate, pplanes, vplanes = pl.pallas_call(
        functools.partial(_mixer_kernel, n_prompt_tiles=n_prompt_tiles, seq_tiles=seq_tiles, part_tiles=part_tiles),
        grid_spec=pltpu.PrefetchScalarGridSpec(
            num_scalar_prefetch=2,
            grid=(n_tok_tiles,),
            in_specs=[
                pl.BlockSpec((1, TM, D_MODEL), lambda i, *_: (p_tile(i) // seq_tiles, p_tile(i) % seq_tiles, 0)),
                pl.BlockSpec((TM, D_MODEL), lambda i, *_: (s_tile(i), 0)),
                pl.BlockSpec((1, POOL_STATE, SEQ_BLK, B_WIDTH), lambda i, *_: (s_tile(i), 0, 0, 0)),
                _const_spec((1, D_MODEL)), _const_spec((D_MODEL, 3 * A_WIDTH)), _const_spec((1, A_WIDTH)),
                _const_spec((1, A_WIDTH)), _const_spec((N_HEADS, CHUNK, CHUNK)), _const_spec((CHUNK, A_WIDTH)),
                _const_spec((N_GROUPS, HEAD_DIM, HEAD_DIM)), _const_spec((1, B_WIDTH)),
                _const_spec((D_MODEL, D_MODEL)), _const_spec((1, D_MODEL)), _const_spec((D_MODEL, LANES)),
                _const_spec((N_ROUTER_ROWS, TM)), _const_spec((TM, TM)),
            ],
            out_specs=[
                pl.BlockSpec((TM, D_MODEL), lambda i, *_: (i, 0)),
                pl.BlockSpec((TM * ROW_TILES, LANES), lambda i, *_: (i, 0)),
                pl.BlockSpec((SUBLANES, TM), lambda i, *_: (0, i)),
                pl.BlockSpec((TM, LANES), lambda i, *_: (i, 0)),
                pl.BlockSpec((1, N_EXPERTS, LANES), lambda i, *_: (i // part_tiles, 0, 0)),
                pl.BlockSpec((1, CARRY_ROWS, B_WIDTH), lambda i, *_: (p_tile(i) // seq_tiles, 0, 0)),
                pl.BlockSpec((1, dec_seq, SEQ_BLK, A_WIDTH), lambda i, *_: (s_tile(i), 0, 0, 0)),
                pl.BlockSpec((1, dec_seq, SEQ_BLK, A_WIDTH), lambda i, *_: (s_tile(i), 0, 0, 0)),
            ],
            scratch_shapes=[pltpu.VMEM((TM, D_MODEL), bf16), pltpu.VMEM((CARRY_ROWS, B_WIDTH), f32),
                            pltpu.VMEM((N_EXPERTS, LANES), f32)],
        ),
        out_shape=[
            jax.ShapeDtypeStruct((t_total, D_MODEL), f32),
            jax.ShapeDtypeStruct((plane_rows, LANES), f32),
            jax.ShapeDtypeStruct((SUBLANES, t_total), i32),
            jax.ShapeDtypeStruct((t_total, LANES), f32),
            jax.ShapeDtypeStruct((N_PARTS, N_EXPERTS, LANES), f32),
            jax.ShapeDtypeStruct((n_batch, CARRY_ROWS, B_WIDTH), f32),
            plane_shape, plane_shape,
        ],
        compiler_params=cparams,
        name="mixer",
    )(aws_s, abs_s, x_prompt, xs_planes, st_planes, n1g, win, ang, anb, a_ws[0][:, :CHUNK, :CHUNK], abias, bw,
      bscale, wout, n2g, rwt, rb, su)

    cnt = counts[:, :, 0].astype(i32)
    padded = (cnt + (TE - 1)) // TE * TE
    pad_end = jnp.cumsum(padded, axis=1)
    pad_start = pad_end - padded
    dest = []
    for h in range(N_PARTS):
        m = meta[:, h * t_part:(h + 1) * t_part]
        base = pad_start[h] + h * p_rows
        dest.append(jnp.stack([base[m[0]] + m[2], base[m[1]] + m[3]], axis=-1).reshape(-1))
    dest = jnp.concatenate(dest)
    asg = jnp.full((N_PARTS * p_rows,), -1, i32).at[dest].set(jnp.arange(n_assign, dtype=i32), unique_indices=True)
    row_part = jnp.arange(N_PARTS * p_rows, dtype=i32) // p_rows
    row_src = jnp.where(asg >= 0, (asg >> 1) - row_part * t_part, 0).reshape(N_PARTS * n_exp_tiles, 1, TE)
    row_dst = jnp.where(asg >= 0, (asg & 1) * t_total + (asg >> 1), 0).reshape(N_PARTS * n_exp_tiles, 1, TE)
    n_valid = jnp.maximum(pad_end[:, -1] // TE, 1).astype(i32)
    tile_ids = jnp.minimum(jnp.arange(n_exp_tiles, dtype=i32)[None, :], n_valid[:, None] - 1)
    tile_e = jnp.sum((pad_end[:, None, :] <= (tile_ids * TE)[:, :, None]).astype(i32), axis=2)
    tile_e = jnp.minimum(tile_e, N_EXPERTS - 1)
    is_e = tile_e[:, :, None] == jnp.arange(N_EXPERTS, dtype=i32)[None, None, :]
    seg_start = jnp.sum(jnp.where(is_e, pad_start[:, None, :], 0), axis=2)
    seg_cnt = jnp.sum(jnp.where(is_e, cnt[:, None, :], 0), axis=2)
    tile_rows = jnp.clip(seg_cnt - (tile_ids * TE - seg_start), 0, TE).astype(i32)

    flat = lambda h, i: h * n_exp_tiles + i
    cur_blk = lambda h, i, te, nv, nr: (flat(h, jnp.minimum(i, nv[h] - 1)), 0, 0)
    w_blk = lambda h, i, te, nv, nr: (te[flat(h, i)], 0, 0)
    smem_blk = lambda imap: pl.BlockSpec((1, 1, TE), imap, memory_space=pltpu.SMEM)
    out_tok = pl.pallas_call(
        _expert_kernel,
        grid_spec=pltpu.PrefetchScalarGridSpec(
            num_scalar_prefetch=3,
            grid=(N_PARTS, n_exp_tiles),
            in_specs=[smem_blk(cur_blk), smem_blk(cur_blk),
                      pl.BlockSpec((t_part * ROW_TILES, LANES), lambda h, i, *_: (h, 0),
                                   pipeline_mode=pl.Buffered(1)),
                      pl.BlockSpec((1, D_MODEL, D_EXPERT), w_blk),
                      pl.BlockSpec((1, D_MODEL, D_EXPERT), w_blk),
                      pl.BlockSpec((1, D_EXPERT, D_MODEL), w_blk)],
            out_specs=pl.BlockSpec(memory_space=pl.ANY),
            scratch_shapes=[pltpu.VMEM((TE * ROW_TILES, LANES), f32),
                            pltpu.VMEM((2, TE * ROW_TILES, LANES), f32),
                            pltpu.VMEM((D_MODEL, D_EXPERT), bf16), pltpu.VMEM((D_MODEL, D_EXPERT), bf16),
                            pltpu.VMEM((D_EXPERT, D_MODEL), bf16),
                            pltpu.SemaphoreType.DMA((2,))],
        ),
        out_shape=jax.ShapeDtypeStruct((TOP_K * t_total, ROW_TILES, LANES), f32),
        compiler_params=pltpu.CompilerParams(dimension_semantics=("arbitrary", "arbitrary"),
                                             vmem_limit_bytes=EXPERT_VMEM_LIMIT),
        name="moe_experts",
    )(tile_e.reshape(-1), n_valid, tile_rows.reshape(-1), row_src, row_dst, h2, exp_w1[0], exp_w3[0], exp_w2[0])
    out_tok = out_tok.reshape(TOP_K * plane_rows, LANES)

    y_prompt, y_sample = pl.pallas_call(
        functools.partial(_combine_kernel, n_prompt_tiles=n_prompt_tiles),
        grid=(n_tok_tiles,),
        in_specs=[pl.BlockSpec((TM, D_MODEL), lambda i: (i, 0)),
                  pl.BlockSpec((TM, LANES), lambda i: (i, 0)),
                  _const_spec((1, D_MODEL)),
                  pl.BlockSpec((TM * ROW_TILES, LANES), lambda i: (i, 0)),
                  pl.BlockSpec((TM * ROW_TILES, LANES), lambda i: (n_tok_tiles + i, 0))],
        out_specs=[pl.BlockSpec((TM, D_MODEL), lambda i: (p_tile(i), 0)),
                   pl.BlockSpec((TM, D_MODEL), lambda i: (s_tile(i), 0))],
        out_shape=[jax.ShapeDtypeStruct((t_prompt, D_MODEL), f32), jax.ShapeDtypeStruct((t_sample, D_MODEL), f32)],
        compiler_params=cparams,
        name="moe_combine",
    )(x1, gcol, nfg, out_tok, out_tok)

    unplane = lambda a: a.transpose(0, 2, 1, 3).reshape(dec_batch, dec_seq, a.shape[-1])
    y_prompt = y_prompt.reshape(n_batch, seq, D_MODEL)
    y_sample = unplane(y_sample.reshape(n_sample_tiles, dec_seq, SEQ_BLK, D_MODEL))
    pool_state_prompt = pstate[None, :, CARRY_ROWS - POOL_STATE:, :]
    p_s = unplane(pplanes)
    pool_state_sample = jnp.concatenate([state_pool[0], p_s], axis=1)[None, :, -POOL_STATE:, :]
    chunk_v_sample = unplane(vplanes)[None]
    return (y_prompt, y_sample, pool_state_prompt, pool_state_sample, chunk_v_sample)
```

```python
import functools
import math

import jax
import jax.numpy as jnp
from jax import lax
from jax.experimental import pallas as pl
from jax.experimental.pallas import tpu as pltpu

D_MODEL = 1024
A_WIDTH = 512
B_WIDTH = 512
N_HEADS = 4
HEAD_DIM = 128
CHUNK = 128
POOL_WINDOWS = (2, 4, 8, 16)
POOL_STATE = 15
N_GROUPS = 4
EXPERTS_PER_GROUP = 8
N_EXPERTS = 32
TOP_K = 2
D_EXPERT = 512
EPS = 1e-6

SUBLANES = 8
LANES = 128
ROW_TILES = D_MODEL // LANES

TM = 256
TE = 256
SEQ_BLK = 32
N_ROUTER_ROWS = 40
CARRY_ROWS = 16
DMA_UNROLL = 8
VMEM_LIMIT = 48 * 1024 * 1024
N_PARTS = 2
EXPERT_VMEM_LIMIT = 62 * 1024 * 1024

_INV_SQRT2 = 1.0 / math.sqrt(2.0)


def _rmsnorm(x, g):
    r = lax.rsqrt(jnp.mean(x * x, axis=-1, keepdims=True) + EPS)
    return (x * r) * g


def _gelu(x):
    return 0.5 * x * (1.0 + lax.erf(x * _INV_SQRT2))


def _layernorm(x, g, b):
    mu = jnp.mean(x, axis=-1, keepdims=True)
    xc = x - mu
    var = jnp.mean(xc * xc, axis=-1, keepdims=True)
    return (xc * lax.rsqrt(var + EPS)) * g + b


def _row_slab(ref, s, n):
    return ref[pl.ds(s, n, stride=ROW_TILES), :]


def _pool_project(pooled, g, bw_ref, bscale_ref):
    lo, hi = g * HEAD_DIM, (g + 1) * HEAD_DIM
    hb = jnp.dot(pooled.astype(jnp.bfloat16), bw_ref[g], preferred_element_type=jnp.float32)
    return hb * bscale_ref[:, lo:hi]


def _prompt_mixers(j, u, v, p, aws_ref, abias_ref, bw_ref, bscale_ref, mix_ref, pcarry_ref, pstate_ref):
    tri = (lax.broadcasted_iota(jnp.int32, (CHUNK, CHUNK), 0)
           >= lax.broadcasted_iota(jnp.int32, (CHUNK, CHUNK), 1))
    vb = v.astype(jnp.bfloat16)
    for hd in range(N_HEADS):
        lo, hi = hd * HEAD_DIM, (hd + 1) * HEAD_DIM
        w = jnp.where(tri, aws_ref[hd], 0.0).astype(jnp.bfloat16)
        for c in range(TM // CHUNK):
            r0, r1 = c * CHUNK, (c + 1) * CHUNK
            z = jnp.dot(w, vb[r0:r1, lo:hi], preferred_element_type=jnp.float32) + abias_ref[:, lo:hi]
            mix_ref[r0:r1, lo:hi] = (u[r0:r1, lo:hi] * z).astype(jnp.bfloat16)

    pos = j * TM + lax.broadcasted_iota(jnp.int32, (TM, LANES), 0)
    for g, w in enumerate(POOL_WINDOWS):
        lo, hi = g * HEAD_DIM, (g + 1) * HEAD_DIM
        pg = p[:, lo:hi]
        acc = jnp.concatenate([pcarry_ref[:, lo:hi], pg], axis=0)
        shift = 1
        while shift < w:
            acc = acc + pltpu.roll(acc, shift, 0)
            shift *= 2
        cnt = jnp.minimum(pos + 1, w).astype(jnp.float32)
        pooled = acc[CARRY_ROWS:, :] / cnt - pg
        mix_ref[:, A_WIDTH + lo:A_WIDTH + hi] = _pool_project(pooled, g, bw_ref, bscale_ref).astype(jnp.bfloat16)
    tail = p[TM - CARRY_ROWS:, :]
    pcarry_ref[...] = tail
    pstate_ref[0] = tail


def _sample_mixers(u, v, p, aws_ref, abs_ref, st_ref, bw_ref, bscale_ref, mix_ref, pp_ref, vp_ref):
    n_pos = TM // SEQ_BLK
    for i in range(n_pos):
        vp_ref[0, i] = v[i * SEQ_BLK:(i + 1) * SEQ_BLK, :]
        pp_ref[0, i] = p[i * SEQ_BLK:(i + 1) * SEQ_BLK, :]

    for hd in range(N_HEADS):
        lo, hi = hd * HEAD_DIM, (hd + 1) * HEAD_DIM
        vplanes = [v[s * SEQ_BLK:(s + 1) * SEQ_BLK, lo:hi] for s in range(n_pos)]
        for i in range(n_pos):
            z = vplanes[0] * aws_ref[hd * 64 + i * 8]
            for s in range(1, i + 1):
                z = z + vplanes[s] * aws_ref[hd * 64 + i * 8 + s]
            z = z + abs_ref[hd * 8 + i]
            r0, r1 = i * SEQ_BLK, (i + 1) * SEQ_BLK
            mix_ref[r0:r1, lo:hi] = (u[r0:r1, lo:hi] * z).astype(jnp.bfloat16)

    for g, w in enumerate(POOL_WINDOWS):
        lo, hi = g * HEAD_DIM, (g + 1) * HEAD_DIM
        planes = [st_ref[0, k, :, lo:hi] for k in range(POOL_STATE)]
        planes += [p[i * SEQ_BLK:(i + 1) * SEQ_BLK, lo:hi] for i in range(n_pos)]
        pooled = []
        for i in range(n_pos):
            top = POOL_STATE + i
            s = planes[top - w + 1]
            for k in range(top - w + 2, top + 1):
                s = s + planes[k]
            pooled.append(s * (1.0 / w) - planes[top])
        pooled = jnp.concatenate(pooled, axis=0)
        mix_ref[:, A_WIDTH + lo:A_WIDTH + hi] = _pool_project(pooled, g, bw_ref, bscale_ref).astype(jnp.bfloat16)


def _route(h2, rwt_ref, rb_ref, su_ref, cnt_ref, meta_ref, gcol_ref):
    tm = h2.shape[0]
    h_hi = h2.astype(jnp.bfloat16)
    h_lo = (h2 - h_hi.astype(jnp.float32)).astype(jnp.bfloat16)
    s = (jnp.dot(h_hi, rwt_ref[...], preferred_element_type=jnp.float32)
         + jnp.dot(h_lo, rwt_ref[...], preferred_element_type=jnp.float32))
    st = s.T
    lt = st[0:N_ROUTER_ROWS, :] + st[N_ROUTER_ROWS:2 * N_ROUTER_ROWS, :] + rb_ref[...]
    row = lambda i: lt[i:i + 1, :]
    l1 = [row(i) for i in range(N_GROUPS)]
    m1 = jnp.maximum(jnp.maximum(l1[0], l1[1]), jnp.maximum(l1[2], l1[3]))
    grp = jnp.where(l1[0] == m1, 0, jnp.where(l1[1] == m1, 1, jnp.where(l1[2] == m1, 2, 3)))
    se = (jnp.exp(l1[0] - m1) + jnp.exp(l1[1] - m1)) + (jnp.exp(l1[2] - m1) + jnp.exp(l1[3] - m1))
    pg = 1.0 / se
    l2 = []
    for e in range(EXPERTS_PER_GROUP):
        c = [row(N_GROUPS + g * EXPERTS_PER_GROUP + e) for g in range(N_GROUPS)]
        l2.append(jnp.where(grp == 0, c[0], jnp.where(grp == 1, c[1], jnp.where(grp == 2, c[2], c[3]))))
    v0 = functools.reduce(jnp.maximum, l2)
    i0 = jnp.full_like(grp, EXPERTS_PER_GROUP - 1)
    for e in range(EXPERTS_PER_GROUP - 2, -1, -1):
        i0 = jnp.where(l2[e] == v0, e, i0)
    neg = jnp.float32(-jnp.inf)
    l2m = [jnp.where(i0 == e, neg, l2[e]) for e in range(EXPERTS_PER_GROUP)]
    v1 = functools.reduce(jnp.maximum, l2m)
    i1 = jnp.full_like(grp, EXPERTS_PER_GROUP - 1)
    for e in range(EXPERTS_PER_GROUP - 2, -1, -1):
        i1 = jnp.where((l2m[e] == v1) & (i0 != e), e, i1)
    d = jnp.exp(v1 - v0)
    g0 = pg / (1.0 + d)
    g1 = (pg * d) / (1.0 + d)
    e0 = grp * EXPERTS_PER_GROUP + i0
    e1 = grp * EXPERTS_PER_GROUP + i1

    eiota = lax.broadcasted_iota(jnp.int32, (N_EXPERTS, tm), 0)
    hit0 = eiota == e0
    hit1 = eiota == e1
    onehot = (hit0 | hit1).astype(jnp.bfloat16)
    prefix = jnp.dot(onehot, su_ref[...], preferred_element_type=jnp.float32)
    carry = cnt_ref[...]
    base = prefix + jnp.concatenate([carry] * (tm // LANES), axis=1)
    r0 = jnp.sum(jnp.where(hit0, base, 0.0), axis=0, keepdims=True)
    r1 = jnp.sum(jnp.where(hit1, base, 0.0), axis=0, keepdims=True)
    ones = jnp.ones((tm, LANES), jnp.bfloat16)
    cnt_ref[...] = carry + jnp.dot(onehot, ones, preferred_element_type=jnp.float32)

    meta_ref[0:1, :] = e0
    meta_ref[1:2, :] = e1
    meta_ref[2:3, :] = r0.astype(jnp.int32)
    meta_ref[3:4, :] = r1.astype(jnp.int32)
    meta_ref[4:8, :] = jnp.zeros((4, tm), jnp.int32)

    giota = lax.broadcasted_iota(jnp.int32, (LANES, tm), 0)
    gpad = jnp.where(giota == 0, g0, jnp.where(giota == 1, g1, 0.0))
    gcol_ref[...] = gpad.T


def _mixer_kernel(aws_s_ref, abs_s_ref,
                  xp_ref, xs_ref, st_ref, n1g_ref, win_ref, ang_ref, anb_ref, aws_ref, abias_ref, bw_ref, bscale_ref,
                  wout_ref, n2g_ref, rwt_ref, rb_ref, su_ref,
                  x1_ref, h2_ref, meta_ref, gcol_ref, counts_ref, pstate_ref, pp_ref, vp_ref,
                  mix_ref, pcarry_ref, cnt_ref, *, n_prompt_tiles, seq_tiles, part_tiles):
    i = pl.program_id(0)
    is_prompt = i < n_prompt_tiles
    j = i % seq_tiles

    @pl.when(i % part_tiles == 0)
    def _():
        cnt_ref[...] = jnp.zeros_like(cnt_ref)

    @pl.when(is_prompt & (j == 0))
    def _():
        pcarry_ref[...] = jnp.zeros_like(pcarry_ref)

    x = jnp.where(is_prompt, xp_ref[0], xs_ref[...])
    h = _rmsnorm(x, n1g_ref[...]).astype(jnp.bfloat16)
    proj = jnp.dot(h, win_ref[...], preferred_element_type=jnp.float32)
    uv = _gelu(proj[:, : 2 * A_WIDTH])
    u = uv[:, :A_WIDTH]
    v = _layernorm(uv[:, A_WIDTH:], ang_ref[...], anb_ref[...])
    p = proj[:, 2 * A_WIDTH:]

    @pl.when(is_prompt)
    def _():
        _prompt_mixers(j, u, v, p, aws_ref, abias_ref, bw_ref, bscale_ref, mix_ref, pcarry_ref, pstate_ref)

    @pl.when(jnp.logical_not(is_prompt))
    def _():
        _sample_mixers(u, v, p, aws_s_ref, abs_s_ref, st_ref, bw_ref, bscale_ref, mix_ref, pp_ref, vp_ref)

    x1 = x + jnp.dot(mix_ref[...], wout_ref[...], preferred_element_type=jnp.float32)
    x1_ref[...] = x1
    h2 = _rmsnorm(x1, n2g_ref[...])
    for s in range(ROW_TILES):
        h2_ref[pl.ds(s, TM, stride=ROW_TILES), :] = h2[:, s * LANES:(s + 1) * LANES]
    _route(h2, rwt_ref, rb_ref, su_ref, cnt_ref, meta_ref, gcol_ref)
    counts_ref[0] = cnt_ref[...]


def _tables_kernel(counts_ref, meta_ref, dest_ref, tab_ref, *, p_rows):
    h = pl.program_id(0)
    t_part = meta_ref.shape[1]
    cnt = counts_ref[0]
    padded = jnp.floor((cnt + (TE - 1)) * (1.0 / TE)) * TE
    sub = lax.broadcasted_iota(jnp.int32, cnt.shape, 0)
    pad_end = padded
    shift = 1
    while shift < N_EXPERTS:
        pad_end = pad_end + jnp.where(sub >= shift, pltpu.roll(pad_end, shift, 0), 0.0)
        shift *= 2
    pad_start = pad_end - padded
    base = pad_start.astype(jnp.int32) + h * p_rows

    for k in range(TOP_K):
        e = meta_ref[k:k + 1, :]
        d = meta_ref[TOP_K + k:TOP_K + k + 1, :]
        for ex in range(N_EXPERTS):
            row = jnp.concatenate([base[ex:ex + 1, :]] * (t_part // LANES), axis=1)
            d = d + jnp.where(e == ex, row, 0)
        dest_ref[k:k + 1, :] = d

    n_valid = jnp.maximum(pad_end[N_EXPERTS - 1:, :] * (1.0 / TE), 1.0)
    tile = jnp.minimum(lax.broadcasted_iota(jnp.int32, (1, LANES), 1).astype(jnp.float32), n_valid - 1.0)
    tile_start = tile * TE
    tile_e = jnp.minimum(jnp.sum((pad_end <= tile_start).astype(jnp.float32), axis=0, keepdims=True),
                         N_EXPERTS - 1.0)
    is_e = sub.astype(jnp.float32) == tile_e
    seg_start = jnp.sum(jnp.where(is_e, pad_start, 0.0), axis=0, keepdims=True)
    seg_cnt = jnp.sum(jnp.where(is_e, cnt, 0.0), axis=0, keepdims=True)
    rows = jnp.clip(seg_cnt - (tile_start - seg_start), 0.0, TE)
    tab_ref[0, 0:1, :] = tile_e.astype(jnp.int32)
    tab_ref[0, 1:2, :] = rows.astype(jnp.int32)
    tab_ref[0, 2:3, :] = n_valid.astype(jnp.int32)
    tab_ref[0, 3:, :] = jnp.zeros((SUBLANES - 3, LANES), jnp.int32)


def _expert_kernel(te_ref, nv_ref, nrows_ref,
                   src_ref, dst_ref,
                   h2_ref, w1_ref, w3_ref, w2_ref,
                   out_hbm,
                   xt, obuf, w1b, w3b, w2b, ssem):
    i = pl.program_id(1)
    g = pl.program_id(0) * pl.num_programs(1) + i
    nv = nv_ref[pl.program_id(0)]
    slot = i % 2
    slab = lambda r: pl.ds(pl.multiple_of(r * ROW_TILES, ROW_TILES), ROW_TILES)

    def scatter_copy(r, src_slot):
        return pltpu.make_async_copy(obuf.at[src_slot, slab(r), :], out_hbm.at[dst_ref[0, 0, r]], ssem.at[src_slot])

    def wait_scatter(tile, src_slot):
        n = nrows_ref[tile] * ROW_TILES
        pltpu.make_async_copy(obuf.at[1 - src_slot, pl.ds(0, n), :], obuf.at[src_slot, pl.ds(0, n), :],
                              ssem.at[src_slot]).wait()

    @pl.when(i < nv)
    def _():
        @pl.when((i == 0) | (te_ref[g] != te_ref[jnp.maximum(g - 1, 0)]))
        def _():
            w1b[...] = w1_ref[0].astype(jnp.bfloat16)
            w3b[...] = w3_ref[0].astype(jnp.bfloat16)
            w2b[...] = w2_ref[0].astype(jnp.bfloat16)

        for r in range(TE):
            xt[r * ROW_TILES:(r + 1) * ROW_TILES, :] = h2_ref[slab(src_ref[0, 0, r]), :]

        @pl.when(i >= 2)
        def _():
            wait_scatter(g - 2, slot)

        x = jnp.concatenate([_row_slab(xt, s, TE) for s in range(ROW_TILES)], axis=1).astype(jnp.bfloat16)
        a = jnp.dot(x, w1b[...], preferred_element_type=jnp.float32)
        b = jnp.dot(x, w3b[...], preferred_element_type=jnp.float32)
        h = (a * jax.nn.sigmoid(a)) * b
        o = jnp.dot(h.astype(jnp.bfloat16), w2b[...], preferred_element_type=jnp.float32)
        os = obuf.at[slot]
        for s in range(ROW_TILES):
            os[pl.ds(s, TE, stride=ROW_TILES), :] = o[:, s * LANES:(s + 1) * LANES]

        n = nrows_ref[g]
        n_full = n // DMA_UNROLL

        def body(rb, carry):
            for uu in range(DMA_UNROLL):
                r = rb * DMA_UNROLL + uu
                scatter_copy(r, slot).start(priority=1)
            return carry
        lax.fori_loop(0, n_full, body, 0)
        for uu in range(DMA_UNROLL - 1):
            r = n_full * DMA_UNROLL + uu

            @pl.when(r < n)
            def _():
                scatter_copy(r, slot).start(priority=1)

        @pl.when(i == nv - 1)
        def _():
            @pl.when(i >= 1)
            def _():
                wait_scatter(g - 1, 1 - slot)
            wait_scatter(g, slot)


def _combine_kernel(x1_ref, gcol_ref, nfg_ref, o0_ref, o1_ref, yp_ref, ys_ref, *, n_prompt_tiles):
    i = pl.program_id(0)
    o0 = jnp.concatenate([_row_slab(o0_ref, s, TM) for s in range(ROW_TILES)], axis=1)
    o1 = jnp.concatenate([_row_slab(o1_ref, s, TM) for s in range(ROW_TILES)], axis=1)
    g = gcol_ref[...]
    moe = g[:, 0:1] * o0 + g[:, 1:2] * o1
    y = _rmsnorm(x1_ref[...] + moe, nfg_ref[...])

    @pl.when(i < n_prompt_tiles)
    def _():
        yp_ref[...] = y

    @pl.when(i >= n_prompt_tiles)
    def _():
        ys_ref[...] = y


def _const_spec(shape):
    return pl.BlockSpec(shape, lambda *_: (0,) * len(shape))


def kernel(x_prompt, x_sample, state_pool, norm1_g, w_in, a_norm_g, a_norm_b, a_ws, a_bs, b_w, b_scale, w_out,
           norm2_g, r1_w, r1_b, r2_w, r2_b, exp_w1, exp_w3, exp_w2, normf_g):
    f32, bf16, i32 = jnp.float32, jnp.bfloat16, jnp.int32
    n_batch, seq, _ = x_prompt.shape
    dec_batch, dec_seq, _ = x_sample.shape
    assert norm1_g.shape[0] == 1 and seq % TM == 0 and TM % CHUNK == 0
    assert dec_seq * SEQ_BLK == TM and dec_batch % SEQ_BLK == 0 and dec_seq <= CHUNK
    t_prompt = n_batch * seq
    t_sample = dec_batch * dec_seq
    t_total = t_prompt + t_sample
    n_tok_tiles = t_total // TM
    n_prompt_tiles = t_prompt // TM
    n_sample_tiles = t_sample // TM
    seq_tiles = seq // TM
    n_assign = TOP_K * t_total
    plane_rows = t_total * ROW_TILES
    assert n_tok_tiles % N_PARTS == 0
    part_tiles = n_tok_tiles // N_PARTS
    t_part = part_tiles * TM
    n_exp_tiles = -(-(TOP_K * t_part + N_EXPERTS * (TE - 1)) // TE)
    p_rows = n_exp_tiles * TE

    n1g = norm1_g[0][None, :]
    n2g = norm2_g[0][None, :]
    nfg = normf_g[None, :]
    win = w_in[0].astype(bf16)
    wout = w_out[0].astype(bf16)
    ang = a_norm_g[0][None, :]
    anb = a_norm_b[0][None, :]
    bw = b_w[0].astype(bf16)
    bscale = b_scale[0][None, :]
    abias = jnp.repeat(a_bs[0][:, :CHUNK].T, HEAD_DIM, axis=1)
    rw = jnp.concatenate([r1_w[0], r2_w[0].transpose(1, 0, 2).reshape(D_MODEL, N_EXPERTS),
                          jnp.zeros((D_MODEL, N_ROUTER_ROWS - N_GROUPS - N_EXPERTS), f32)], axis=1)
    rw_hi = rw.astype(bf16)
    rw_lo = (rw - rw_hi.astype(f32)).astype(bf16)
    rwt = jnp.concatenate([rw_hi, rw_lo, jnp.zeros((D_MODEL, LANES - 2 * N_ROUTER_ROWS), bf16)], axis=1)
    rbias = jnp.concatenate([r1_b[0], r2_b[0].reshape(-1),
                             jnp.zeros((N_ROUTER_ROWS - N_GROUPS - N_EXPERTS,), f32)])
    rb = jnp.broadcast_to(rbias[:, None], (N_ROUTER_ROWS, TM))
    su = (jnp.arange(TM)[:, None] < jnp.arange(TM)[None, :]).astype(bf16)
    xs_planes = (x_sample.reshape(n_sample_tiles, SEQ_BLK, dec_seq, D_MODEL)
                 .transpose(0, 2, 1, 3).reshape(t_sample, D_MODEL))
    st_planes = state_pool[0].reshape(n_sample_tiles, SEQ_BLK, POOL_STATE, B_WIDTH).transpose(0, 2, 1, 3)
    aws_s = a_ws[0][:, :dec_seq, :dec_seq].reshape(-1)
    abs_s = a_bs[0][:, :dec_seq].reshape(-1)

    cparams = pltpu.CompilerParams(dimension_semantics=("arbitrary",), vmem_limit_bytes=VMEM_LIMIT)

    p_tile = lambda i: jnp.minimum(i, n_prompt_tiles - 1)
    s_tile = lambda i: jnp.maximum(i - n_prompt_tiles, 0)
    plane_shape = jax.ShapeDtypeStruct((n_sample_tiles, dec_seq, SEQ_BLK, A_WIDTH), f32)
    x1, h2, meta, gcol, counts, pstate, pplanes, vplanes = pl.pallas_call(
        functools.partial(_mixer_kernel, n_prompt_tiles=n_prompt_tiles, seq_tiles=seq_tiles, part_tiles=part_tiles),
        grid_spec=pltpu.PrefetchScalarGridSpec(
            num_scalar_prefetch=2,
            grid=(n_tok_tiles,),
            in_specs=[
                pl.BlockSpec((1, TM, D_MODEL), lambda i, *_: (p_tile(i) // seq_tiles, p_tile(i) % seq_tiles, 0)),
                pl.BlockSpec((TM, D_MODEL), lambda i, *_: (s_tile(i), 0)),
                pl.BlockSpec((1, POOL_STATE, SEQ_BLK, B_WIDTH), lambda i, *_: (s_tile(i), 0, 0, 0)),
                _const_spec((1, D_MODEL)), _const_spec((D_MODEL, 3 * A_WIDTH)), _const_spec((1, A_WIDTH)),
                _const_spec((1, A_WIDTH)), _const_spec((N_HEADS, CHUNK, CHUNK)), _const_spec((CHUNK, A_WIDTH)),
                _const_spec((N_GROUPS, HEAD_DIM, HEAD_DIM)), _const_spec((1, B_WIDTH)),
                _const_spec((D_MODEL, D_MODEL)), _const_spec((1, D_MODEL)), _const_spec((D_MODEL, LANES)),
                _const_spec((N_ROUTER_ROWS, TM)), _const_spec((TM, TM)),
            ],
            out_specs=[
                pl.BlockSpec((TM, D_MODEL), lambda i, *_: (i, 0)),
                pl.BlockSpec((TM * ROW_TILES, LANES), lambda i, *_: (i, 0)),
                pl.BlockSpec((SUBLANES, TM), lambda i, *_: (0, i)),
                pl.BlockSpec((TM, LANES), lambda i, *_: (i, 0)),
                pl.BlockSpec((1, N_EXPERTS, LANES), lambda i, *_: (i // part_tiles, 0, 0)),
                pl.BlockSpec((1, CARRY_ROWS, B_WIDTH), lambda i, *_: (p_tile(i) // seq_tiles, 0, 0)),
                pl.BlockSpec((1, dec_seq, SEQ_BLK, A_WIDTH), lambda i, *_: (s_tile(i), 0, 0, 0)),
                pl.BlockSpec((1, dec_seq, SEQ_BLK, A_WIDTH), lambda i, *_: (s_tile(i), 0, 0, 0)),
            ],
            scratch_shapes=[pltpu.VMEM((TM, D_MODEL), bf16), pltpu.VMEM((CARRY_ROWS, B_WIDTH), f32),
                            pltpu.VMEM((N_EXPERTS, LANES), f32)],
        ),
        out_shape=[
            jax.ShapeDtypeStruct((t_total, D_MODEL), f32),
            jax.ShapeDtypeStruct((plane_rows, LANES), f32),
            jax.ShapeDtypeStruct((SUBLANES, t_total), i32),
            jax.ShapeDtypeStruct((t_total, LANES), f32),
            jax.ShapeDtypeStruct((N_PARTS, N_EXPERTS, LANES), f32),
            jax.ShapeDtypeStruct((n_batch, CARRY_ROWS, B_WIDTH), f32),
            plane_shape, plane_shape,
        ],
        compiler_params=cparams,
        name="mixer",
    )(aws_s, abs_s, x_prompt, xs_planes, st_planes, n1g, win, ang, anb, a_ws[0][:, :CHUNK, :CHUNK], abias, bw,
      bscale, wout, n2g, rwt, rb, su)

    assert n_exp_tiles <= LANES and t_part % LANES == 0
    dest, tab = pl.pallas_call(
        functools.partial(_tables_kernel, p_rows=p_rows),
        grid=(N_PARTS,),
        in_specs=[pl.BlockSpec((1, N_EXPERTS, LANES), lambda h: (h, 0, 0)),
                  pl.BlockSpec((SUBLANES, t_part), lambda h: (0, h))],
        out_specs=[pl.BlockSpec((TOP_K, t_part), lambda h: (0, h)),
                   pl.BlockSpec((1, SUBLANES, LANES), lambda h: (h, 0, 0))],
        out_shape=[jax.ShapeDtypeStruct((TOP_K, t_total), i32),
                   jax.ShapeDtypeStruct((N_PARTS, SUBLANES, LANES), i32)],
        compiler_params=cparams,
        name="route_tables",
    )(counts, meta)
    tile_e = tab[:, 0, :n_exp_tiles].reshape(-1)
    tile_rows = tab[:, 1, :n_exp_tiles].reshape(-1)
    n_valid = tab[:, 2, 0]
    asg = jnp.full((N_PARTS * p_rows,), -1, i32).at[dest.reshape(-1)].set(
        jnp.arange(n_assign, dtype=i32), unique_indices=True)
    row_part = jnp.arange(N_PARTS * p_rows, dtype=i32) // p_rows
    row_tok = jnp.where(asg >= t_total, asg - t_total, asg)
    row_src = jnp.where(asg >= 0, row_tok - row_part * t_part, 0).reshape(N_PARTS * n_exp_tiles, 1, TE)
    row_dst = jnp.maximum(asg, 0).reshape(N_PARTS * n_exp_tiles, 1, TE)

    flat = lambda h, i: h * n_exp_tiles + i
    cur_blk = lambda h, i, te, nv, nr: (flat(h, jnp.minimum(i, nv[h] - 1)), 0, 0)
    w_blk = lambda h, i, te, nv, nr: (te[flat(h, i)], 0, 0)
    smem_blk = lambda imap: pl.BlockSpec((1, 1, TE), imap, memory_space=pltpu.SMEM)
    out_tok = pl.pallas_call(
        _expert_kernel,
        grid_spec=pltpu.PrefetchScalarGridSpec(
            num_scalar_prefetch=3,
            grid=(N_PARTS, n_exp_tiles),
            in_specs=[smem_blk(cur_blk), smem_blk(cur_blk),
                      pl.BlockSpec((t_part * ROW_TILES, LANES), lambda h, i, *_: (h, 0),
                                   pipeline_mode=pl.Buffered(1)),
                      pl.BlockSpec((1, D_MODEL, D_EXPERT), w_blk),
                      pl.BlockSpec((1, D_MODEL, D_EXPERT), w_blk),
                      pl.BlockSpec((1, D_EXPERT, D_MODEL), w_blk)],
            out_specs=pl.BlockSpec(memory_space=pl.ANY),
            scratch_shapes=[pltpu.VMEM((TE * ROW_TILES, LANES), f32),
                            pltpu.VMEM((2, TE * ROW_TILES, LANES), f32),
                            pltpu.VMEM((D_MODEL, D_EXPERT), bf16), pltpu.VMEM((D_MODEL, D_EXPERT), bf16),
                            pltpu.VMEM((D_EXPERT, D_MODEL), bf16),
                            pltpu.SemaphoreType.DMA((2,))],
        ),
        out_shape=jax.ShapeDtypeStruct((TOP_K * t_total, ROW_TILES, LANES), f32),
        compiler_params=pltpu.CompilerParams(dimension_semantics=("arbitrary", "arbitrary"),
                                             vmem_limit_bytes=EXPERT_VMEM_LIMIT),
        name="moe_experts",
    )(tile_e, n_valid, tile_rows, row_src, row_dst, h2, exp_w1[0], exp_w3[0], exp_w2[0])
    out_tok = out_tok.reshape(TOP_K * plane_rows, LANES)

    y_prompt, y_sample = pl.pallas_call(
        functools.partial(_combine_kernel, n_prompt_tiles=n_prompt_tiles),
        grid=(n_tok_tiles,),
        in_specs=[pl.BlockSpec((TM, D_MODEL), lambda i: (i, 0)),
                  pl.BlockSpec((TM, LANES), lambda i: (i, 0)),
                  _const_spec((1, D_MODEL)),
                  pl.BlockSpec((TM * ROW_TILES, LANES), lambda i: (i, 0)),
                  pl.BlockSpec((TM * ROW_TILES, LANES), lambda i: (n_tok_tiles + i, 0))],
        out_specs=[pl.BlockSpec((TM, D_MODEL), lambda i: (p_tile(i), 0)),
                   pl.BlockSpec((TM, D_MODEL), lambda i: (s_tile(i), 0))],
        out_shape=[jax.ShapeDtypeStruct((t_prompt, D_MODEL), f32), jax.ShapeDtypeStruct((t_sample, D_MODEL), f32)],
        compiler_params=cparams,
        name="moe_combine",
    )(x1, gcol, nfg, out_tok, out_tok)

    unplane = lambda a: a.transpose(0, 2, 1, 3).reshape(dec_batch, dec_seq, a.shape[-1])
    y_prompt = y_prompt.reshape(n_batch, seq, D_MODEL)
    y_sample = unplane(y_sample.reshape(n_sample_tiles, dec_seq, SEQ_BLK, D_MODEL))
    pool_state_prompt = pstate[None, :, CARRY_ROWS - POOL_STATE:, :]
    p_s = unplane(pplanes)
    pool_state_sample = jnp.concatenate([state_pool[0], p_s], axis=1)[None, :, -POOL_STATE:, :]
    chunk_v_sample = unplane(vplanes)[None]
    return (y_prompt, y_sample, pool_state_prompt, pool_state_sample, chunk_v_sample)
```

```python
import functools
import math

import jax
import jax.numpy as jnp
from jax import lax
from jax.experimental import pallas as pl
from jax.experimental.pallas import tpu as pltpu

D_MODEL = 1024
A_WIDTH = 512
B_WIDTH = 512
N_HEADS = 4
HEAD_DIM = 128
CHUNK = 128
POOL_WINDOWS = (2, 4, 8, 16)
POOL_STATE = 15
N_GROUPS = 4
EXPERTS_PER_GROUP = 8
N_EXPERTS = 32
TOP_K = 2
D_EXPERT = 512
EPS = 1e-6

SUBLANES = 8
LANES = 128
ROW_TILES = D_MODEL // LANES

TM = 256
TE = 256
SEQ_BLK = 32
N_ROUTER_ROWS = 40
CARRY_ROWS = 16
DMA_UNROLL = 32
VMEM_LIMIT = 48 * 1024 * 1024
N_PARTS = 2
EXPERT_VMEM_LIMIT = 62 * 1024 * 1024

_INV_SQRT2 = 1.0 / math.sqrt(2.0)


def _rmsnorm(x, g):
    r = lax.rsqrt(jnp.mean(x * x, axis=-1, keepdims=True) + EPS)
    return (x * r) * g


def _gelu(x):
    return 0.5 * x * (1.0 + lax.erf(x * _INV_SQRT2))


def _layernorm(x, g, b):
    mu = jnp.mean(x, axis=-1, keepdims=True)
    xc = x - mu
    var = jnp.mean(xc * xc, axis=-1, keepdims=True)
    return (xc * lax.rsqrt(var + EPS)) * g + b


def _row_slab(ref, s, n):
    return ref[pl.ds(s, n, stride=ROW_TILES), :]


def _pool_project(pooled, g, bw_ref, bscale_ref):
    lo, hi = g * HEAD_DIM, (g + 1) * HEAD_DIM
    hb = jnp.dot(pooled.astype(jnp.bfloat16), bw_ref[g], preferred_element_type=jnp.float32)
    return hb * bscale_ref[:, lo:hi]


def _prompt_mixers(j, u, v, p, aws_ref, abias_ref, bw_ref, bscale_ref, mix_ref, pcarry_ref, pstate_ref):
    tri = (lax.broadcasted_iota(jnp.int32, (CHUNK, CHUNK), 0)
           >= lax.broadcasted_iota(jnp.int32, (CHUNK, CHUNK), 1))
    vb = v.astype(jnp.bfloat16)
    for hd in range(N_HEADS):
        lo, hi = hd * HEAD_DIM, (hd + 1) * HEAD_DIM
        w = jnp.where(tri, aws_ref[hd], 0.0).astype(jnp.bfloat16)
        for c in range(TM // CHUNK):
            r0, r1 = c * CHUNK, (c + 1) * CHUNK
            z = jnp.dot(w, vb[r0:r1, lo:hi], preferred_element_type=jnp.float32) + abias_ref[:, lo:hi]
            mix_ref[r0:r1, lo:hi] = (u[r0:r1, lo:hi] * z).astype(jnp.bfloat16)

    pos = j * TM + lax.broadcasted_iota(jnp.int32, (TM, LANES), 0)
    for g, w in enumerate(POOL_WINDOWS):
        lo, hi = g * HEAD_DIM, (g + 1) * HEAD_DIM
        pg = p[:, lo:hi]
        acc = jnp.concatenate([pcarry_ref[:, lo:hi], pg], axis=0)
        shift = 1
        while shift < w:
            acc = acc + pltpu.roll(acc, shift, 0)
            shift *= 2
        cnt = jnp.minimum(pos + 1, w).astype(jnp.float32)
        pooled = acc[CARRY_ROWS:, :] / cnt - pg
        mix_ref[:, A_WIDTH + lo:A_WIDTH + hi] = _pool_project(pooled, g, bw_ref, bscale_ref).astype(jnp.bfloat16)
    tail = p[TM - CARRY_ROWS:, :]
    pcarry_ref[...] = tail
    pstate_ref[0] = tail


def _sample_mixers(u, v, p, aws_ref, abs_ref, st_ref, bw_ref, bscale_ref, mix_ref, pp_ref, vp_ref):
    n_pos = TM // SEQ_BLK
    for i in range(n_pos):
        vp_ref[0, i] = v[i * SEQ_BLK:(i + 1) * SEQ_BLK, :]
        pp_ref[0, i] = p[i * SEQ_BLK:(i + 1) * SEQ_BLK, :]

    for hd in range(N_HEADS):
        lo, hi = hd * HEAD_DIM, (hd + 1) * HEAD_DIM
        vplanes = [v[s * SEQ_BLK:(s + 1) * SEQ_BLK, lo:hi] for s in range(n_pos)]
        for i in range(n_pos):
            z = vplanes[0] * aws_ref[hd * 64 + i * 8]
            for s in range(1, i + 1):
                z = z + vplanes[s] * aws_ref[hd * 64 + i * 8 + s]
            z = z + abs_ref[hd * 8 + i]
            r0, r1 = i * SEQ_BLK, (i + 1) * SEQ_BLK
            mix_ref[r0:r1, lo:hi] = (u[r0:r1, lo:hi] * z).astype(jnp.bfloat16)

    for g, w in enumerate(POOL_WINDOWS):
        lo, hi = g * HEAD_DIM, (g + 1) * HEAD_DIM
        planes = [st_ref[0, k, :, lo:hi] for k in range(POOL_STATE)]
        planes += [p[i * SEQ_BLK:(i + 1) * SEQ_BLK, lo:hi] for i in range(n_pos)]
        pooled = []
        for i in range(n_pos):
            top = POOL_STATE + i
            s = planes[top - w + 1]
            for k in range(top - w + 2, top + 1):
                s = s + planes[k]
            pooled.append(s * (1.0 / w) - planes[top])
        pooled = jnp.concatenate(pooled, axis=0)
        mix_ref[:, A_WIDTH + lo:A_WIDTH + hi] = _pool_project(pooled, g, bw_ref, bscale_ref).astype(jnp.bfloat16)


def _route(h2, rwt_ref, rb_ref, su_ref, cnt_ref, meta_ref, gcol_ref):
    tm = h2.shape[0]
    h_hi = h2.astype(jnp.bfloat16)
    h_lo = (h2 - h_hi.astype(jnp.float32)).astype(jnp.bfloat16)
    s = (jnp.dot(h_hi, rwt_ref[...], preferred_element_type=jnp.float32)
         + jnp.dot(h_lo, rwt_ref[...], preferred_element_type=jnp.float32))
    st = s.T
    lt = st[0:N_ROUTER_ROWS, :] + st[N_ROUTER_ROWS:2 * N_ROUTER_ROWS, :] + rb_ref[...]
    row = lambda i: lt[i:i + 1, :]
    l1 = [row(i) for i in range(N_GROUPS)]
    m1 = jnp.maximum(jnp.maximum(l1[0], l1[1]), jnp.maximum(l1[2], l1[3]))
    grp = jnp.where(l1[0] == m1, 0, jnp.where(l1[1] == m1, 1, jnp.where(l1[2] == m1, 2, 3)))
    se = (jnp.exp(l1[0] - m1) + jnp.exp(l1[1] - m1)) + (jnp.exp(l1[2] - m1) + jnp.exp(l1[3] - m1))
    pg = 1.0 / se
    l2 = []
    for e in range(EXPERTS_PER_GROUP):
        c = [row(N_GROUPS + g * EXPERTS_PER_GROUP + e) for g in range(N_GROUPS)]
        l2.append(jnp.where(grp == 0, c[0], jnp.where(grp == 1, c[1], jnp.where(grp == 2, c[2], c[3]))))
    v0 = functools.reduce(jnp.maximum, l2)
    i0 = jnp.full_like(grp, EXPERTS_PER_GROUP - 1)
    for e in range(EXPERTS_PER_GROUP - 2, -1, -1):
        i0 = jnp.where(l2[e] == v0, e, i0)
    neg = jnp.float32(-jnp.inf)
    l2m = [jnp.where(i0 == e, neg, l2[e]) for e in range(EXPERTS_PER_GROUP)]
    v1 = functools.reduce(jnp.maximum, l2m)
    i1 = jnp.full_like(grp, EXPERTS_PER_GROUP - 1)
    for e in range(EXPERTS_PER_GROUP - 2, -1, -1):
        i1 = jnp.where((l2m[e] == v1) & (i0 != e), e, i1)
    d = jnp.exp(v1 - v0)
    g0 = pg / (1.0 + d)
    g1 = (pg * d) / (1.0 + d)
    e0 = grp * EXPERTS_PER_GROUP + i0
    e1 = grp * EXPERTS_PER_GROUP + i1

    eiota = lax.broadcasted_iota(jnp.int32, (N_EXPERTS, tm), 0)
    hit0 = eiota == e0
    hit1 = eiota == e1
    onehot = (hit0 | hit1).astype(jnp.bfloat16)
    prefix = jnp.dot(onehot, su_ref[...], preferred_element_type=jnp.float32)
    carry = cnt_ref[...]
    base = prefix + jnp.concatenate([carry] * (tm // LANES), axis=1)
    r0 = jnp.sum(jnp.where(hit0, base, 0.0), axis=0, keepdims=True)
    r1 = jnp.sum(jnp.where(hit1, base, 0.0), axis=0, keepdims=True)
    ones = jnp.ones((tm, LANES), jnp.bfloat16)
    cnt_ref[...] = carry + jnp.dot(onehot, ones, preferred_element_type=jnp.float32)

    meta_ref[0:1, :] = e0
    meta_ref[1:2, :] = e1
    meta_ref[2:3, :] = r0.astype(jnp.int32)
    meta_ref[3:4, :] = r1.astype(jnp.int32)
    meta_ref[4:8, :] = jnp.zeros((4, tm), jnp.int32)

    giota = lax.broadcasted_iota(jnp.int32, (LANES, tm), 0)
    gpad = jnp.where(giota == 0, g0, jnp.where(giota == 1, g1, 0.0))
    gcol_ref[...] = gpad.T


def _mixer_kernel(aws_s_ref, abs_s_ref,
                  xp_ref, xs_ref, st_ref, n1g_ref, win_ref, ang_ref, anb_ref, aws_ref, abias_ref, bw_ref, bscale_ref,
                  wout_ref, n2g_ref, rwt_ref, rb_ref, su_ref,
                  x1_ref, h2_ref, meta_ref, gcol_ref, counts_ref, pstate_ref, pp_ref, vp_ref,
                  mix_ref, pcarry_ref, cnt_ref, *, n_prompt_tiles, seq_tiles, part_tiles):
    i = pl.program_id(0)
    is_prompt = i < n_prompt_tiles
    j = i % seq_tiles

    @pl.when(i % part_tiles == 0)
    def _():
        cnt_ref[...] = jnp.zeros_like(cnt_ref)

    @pl.when(is_prompt & (j == 0))
    def _():
        pcarry_ref[...] = jnp.zeros_like(pcarry_ref)

    x = jnp.where(is_prompt, xp_ref[0], xs_ref[...])
    h = _rmsnorm(x, n1g_ref[...]).astype(jnp.bfloat16)
    proj = jnp.dot(h, win_ref[...], preferred_element_type=jnp.float32)
    uv = _gelu(proj[:, : 2 * A_WIDTH])
    u = uv[:, :A_WIDTH]
    v = _layernorm(uv[:, A_WIDTH:], ang_ref[...], anb_ref[...])
    p = proj[:, 2 * A_WIDTH:]

    @pl.when(is_prompt)
    def _():
        _prompt_mixers(j, u, v, p, aws_ref, abias_ref, bw_ref, bscale_ref, mix_ref, pcarry_ref, pstate_ref)

    @pl.when(jnp.logical_not(is_prompt))
    def _():
        _sample_mixers(u, v, p, aws_s_ref, abs_s_ref, st_ref, bw_ref, bscale_ref, mix_ref, pp_ref, vp_ref)

    x1 = x + jnp.dot(mix_ref[...], wout_ref[...], preferred_element_type=jnp.float32)
    x1_ref[...] = x1
    h2 = _rmsnorm(x1, n2g_ref[...])
    for s in range(ROW_TILES):
        h2_ref[pl.ds(s, TM, stride=ROW_TILES), :] = h2[:, s * LANES:(s + 1) * LANES]
    _route(h2, rwt_ref, rb_ref, su_ref, cnt_ref, meta_ref, gcol_ref)
    counts_ref[0] = cnt_ref[...]


def _tables_kernel(counts_ref, meta_ref, asg_ref, tab_ref, dest_vmem, fill_vmem, *dest_smem, t_total):
    h = pl.program_id(0)
    t_part = meta_ref.shape[1]
    cnt = counts_ref[0]
    padded = jnp.floor((cnt + (TE - 1)) * (1.0 / TE)) * TE
    sub = lax.broadcasted_iota(jnp.int32, cnt.shape, 0)
    pad_end = padded
    shift = 1
    while shift < N_EXPERTS:
        pad_end = pad_end + jnp.where(sub >= shift, pltpu.roll(pad_end, shift, 0), 0.0)
        shift *= 2
    pad_start = pad_end - padded
    base = pad_start.astype(jnp.int32)

    for k in range(TOP_K):
        e = meta_ref[k:k + 1, :]
        d = meta_ref[TOP_K + k:TOP_K + k + 1, :]
        for ex in range(N_EXPERTS):
            row = jnp.concatenate([base[ex:ex + 1, :]] * (t_part // LANES), axis=1)
            d = d + jnp.where(e == ex, row, 0)
        dest_vmem[k] = d
    for k, dsm in enumerate(dest_smem):
        pltpu.sync_copy(dest_vmem.at[k], dsm)

    fill_vmem[...] = jnp.full(fill_vmem.shape, -1, jnp.int32)
    pltpu.sync_copy(fill_vmem, asg_ref)
    unroll = 16
    for k, dsm in enumerate(dest_smem):
        def invert(_, carry, dsm=dsm):
            t, a = carry
            for uu in range(unroll):
                asg_ref[0, 0, dsm[0, t + uu]] = a + uu
            return t + unroll, a + unroll
        lax.fori_loop(0, t_part // unroll, invert, (jnp.int32(0), k * t_total + h * t_part))

    n_valid = jnp.maximum(pad_end[N_EXPERTS - 1:, :] * (1.0 / TE), 1.0)
    tile = jnp.minimum(lax.broadcasted_iota(jnp.int32, (1, LANES), 1).astype(jnp.float32), n_valid - 1.0)
    tile_start = tile * TE
    tile_e = jnp.minimum(jnp.sum((pad_end <= tile_start).astype(jnp.float32), axis=0, keepdims=True),
                         N_EXPERTS - 1.0)
    is_e = sub.astype(jnp.float32) == tile_e
    seg_start = jnp.sum(jnp.where(is_e, pad_start, 0.0), axis=0, keepdims=True)
    seg_cnt = jnp.sum(jnp.where(is_e, cnt, 0.0), axis=0, keepdims=True)
    rows = jnp.clip(seg_cnt - (tile_start - seg_start), 0.0, TE)
    tab_ref[0, 0:1, :] = tile_e.astype(jnp.int32)
    tab_ref[0, 1:2, :] = rows.astype(jnp.int32)
    tab_ref[0, 2:3, :] = n_valid.astype(jnp.int32)
    tab_ref[0, 3:, :] = jnp.zeros((SUBLANES - 3, LANES), jnp.int32)


def _expert_kernel(te_ref, nv_ref, nrows_ref,
                   src_ref, dst_ref,
                   h2_ref, w1_ref, w3_ref, w2_ref,
                   out_hbm,
                   xt, obuf, w1b, w3b, w2b, ssem):
    i = pl.program_id(1)
    g = pl.program_id(0) * pl.num_programs(1) + i
    nv = nv_ref[pl.program_id(0)]
    slot = i % 2
    slab = lambda r: pl.ds(pl.multiple_of(r * ROW_TILES, ROW_TILES), ROW_TILES)

    def scatter_copy(r, src_slot):
        return pltpu.make_async_copy(obuf.at[src_slot, slab(r), :], out_hbm.at[dst_ref[0, 0, r]], ssem.at[src_slot])

    def wait_scatter(tile, src_slot):
        n = nrows_ref[tile] * ROW_TILES
        pltpu.make_async_copy(obuf.at[1 - src_slot, pl.ds(0, n), :], obuf.at[src_slot, pl.ds(0, n), :],
                              ssem.at[src_slot]).wait()

    @pl.when(i < nv)
    def _():
        @pl.when((i == 0) | (te_ref[g] != te_ref[jnp.maximum(g - 1, 0)]))
        def _():
            w1b[...] = w1_ref[0].astype(jnp.bfloat16)
            w3b[...] = w3_ref[0].astype(jnp.bfloat16)
            w2b[...] = w2_ref[0].astype(jnp.bfloat16)

        for r in range(TE):
            xt[r * ROW_TILES:(r + 1) * ROW_TILES, :] = h2_ref[slab(src_ref[0, 0, r]), :]

        @pl.when(i >= 2)
        def _():
            wait_scatter(g - 2, slot)

        x = jnp.concatenate([_row_slab(xt, s, TE) for s in range(ROW_TILES)], axis=1).astype(jnp.bfloat16)
        a = jnp.dot(x, w1b[...], preferred_element_type=jnp.float32)
        b = jnp.dot(x, w3b[...], preferred_element_type=jnp.float32)
        h = (a * jax.nn.sigmoid(a)) * b
        o = jnp.dot(h.astype(jnp.bfloat16), w2b[...], preferred_element_type=jnp.float32)
        os = obuf.at[slot]
        for s in range(ROW_TILES):
            os[pl.ds(s, TE, stride=ROW_TILES), :] = o[:, s * LANES:(s + 1) * LANES]

        n = nrows_ref[g]
        n_full = n // DMA_UNROLL

        def body(rb, carry):
            for uu in range(DMA_UNROLL):
                r = rb * DMA_UNROLL + uu
                scatter_copy(r, slot).start(priority=1)
            return carry
        lax.fori_loop(0, n_full, body, 0)

        def tail(r, carry):
            scatter_copy(r, slot).start(priority=1)
            return carry
        lax.fori_loop(n_full * DMA_UNROLL, n, tail, 0)

        @pl.when(i == nv - 1)
        def _():
            @pl.when(i >= 1)
            def _():
                wait_scatter(g - 1, 1 - slot)
            wait_scatter(g, slot)


def _combine_kernel(x1_ref, gcol_ref, nfg_ref, o0_ref, o1_ref, yp_ref, ys_ref, *, n_prompt_tiles):
    i = pl.program_id(0)
    o0 = jnp.concatenate([_row_slab(o0_ref, s, TM) for s in range(ROW_TILES)], axis=1)
    o1 = jnp.concatenate([_row_slab(o1_ref, s, TM) for s in range(ROW_TILES)], axis=1)
    g = gcol_ref[...]
    moe = g[:, 0:1] * o0 + g[:, 1:2] * o1
    y = _rmsnorm(x1_ref[...] + moe, nfg_ref[...])

    @pl.when(i < n_prompt_tiles)
    def _():
        yp_ref[...] = y

    @pl.when(i >= n_prompt_tiles)
    def _():
        ys_ref[...] = y


def _const_spec(shape):
    return pl.BlockSpec(shape, lambda *_: (0,) * len(shape))


def kernel(x_prompt, x_sample, state_pool, norm1_g, w_in, a_norm_g, a_norm_b, a_ws, a_bs, b_w, b_scale, w_out,
           norm2_g, r1_w, r1_b, r2_w, r2_b, exp_w1, exp_w3, exp_w2, normf_g):
    f32, bf16, i32 = jnp.float32, jnp.bfloat16, jnp.int32
    n_batch, seq, _ = x_prompt.shape
    dec_batch, dec_seq, _ = x_sample.shape
    assert norm1_g.shape[0] == 1 and seq % TM == 0 and TM % CHUNK == 0
    assert dec_seq * SEQ_BLK == TM and dec_batch % SEQ_BLK == 0 and dec_seq <= CHUNK
    t_prompt = n_batch * seq
    t_sample = dec_batch * dec_seq
    t_total = t_prompt + t_sample
    n_tok_tiles = t_total // TM
    n_prompt_tiles = t_prompt // TM
    n_sample_tiles = t_sample // TM
    seq_tiles = seq // TM
    n_assign = TOP_K * t_total
    plane_rows = t_total * ROW_TILES
    assert n_tok_tiles % N_PARTS == 0
    part_tiles = n_tok_tiles // N_PARTS
    t_part = part_tiles * TM
    n_exp_tiles = -(-(TOP_K * t_part + N_EXPERTS * (TE - 1)) // TE)
    p_rows = n_exp_tiles * TE

    n1g = norm1_g[0][None, :]
    n2g = norm2_g[0][None, :]
    nfg = normf_g[None, :]
    win = w_in[0].astype(bf16)
    wout = w_out[0].astype(bf16)
    ang = a_norm_g[0][None, :]
    anb = a_norm_b[0][None, :]
    bw = b_w[0].astype(bf16)
    bscale = b_scale[0][None, :]
    abias = jnp.repeat(a_bs[0][:, :CHUNK].T, HEAD_DIM, axis=1)
    rw = jnp.concatenate([r1_w[0], r2_w[0].transpose(1, 0, 2).reshape(D_MODEL, N_EXPERTS),
                          jnp.zeros((D_MODEL, N_ROUTER_ROWS - N_GROUPS - N_EXPERTS), f32)], axis=1)
    rw_hi = rw.astype(bf16)
    rw_lo = (rw - rw_hi.astype(f32)).astype(bf16)
    rwt = jnp.concatenate([rw_hi, rw_lo, jnp.zeros((D_MODEL, LANES - 2 * N_ROUTER_ROWS), bf16)], axis=1)
    rbias = jnp.concatenate([r1_b[0], r2_b[0].reshape(-1),
                             jnp.zeros((N_ROUTER_ROWS - N_GROUPS - N_EXPERTS,), f32)])
    rb = jnp.broadcast_to(rbias[:, None], (N_ROUTER_ROWS, TM))
    su = (jnp.arange(TM)[:, None] < jnp.arange(TM)[None, :]).astype(bf16)
    xs_planes = (x_sample.reshape(n_sample_tiles, SEQ_BLK, dec_seq, D_MODEL)
                 .transpose(0, 2, 1, 3).reshape(t_sample, D_MODEL))
    st_planes = state_pool[0].reshape(n_sample_tiles, SEQ_BLK, POOL_STATE, B_WIDTH).transpose(0, 2, 1, 3)
    aws_s = a_ws[0][:, :dec_seq, :dec_seq].reshape(-1)
    abs_s = a_bs[0][:, :dec_seq].reshape(-1)

    cparams = pltpu.CompilerParams(dimension_semantics=("arbitrary",), vmem_limit_bytes=VMEM_LIMIT)

    p_tile = lambda i: jnp.minimum(i, n_prompt_tiles - 1)
    s_tile = lambda i: jnp.maximum(i - n_prompt_tiles, 0)
    plane_shape = jax.ShapeDtypeStruct((n_sample_tiles, dec_seq, SEQ_BLK, A_WIDTH), f32)
    x1, h2, meta, gcol, counts, pstate, pplanes, vplanes = pl.pallas_call(
        functools.partial(_mixer_kernel, n_prompt_tiles=n_prompt_tiles, seq_tiles=seq_tiles, part_tiles=part_tiles),
        grid_spec=pltpu.PrefetchScalarGridSpec(
            num_scalar_prefetch=2,
            grid=(n_tok_tiles,),
            in_specs=[
                pl.BlockSpec((1, TM, D_MODEL), lambda i, *_: (p_tile(i) // seq_tiles, p_tile(i) % seq_tiles, 0)),
                pl.BlockSpec((TM, D_MODEL), lambda i, *_: (s_tile(i), 0)),
                pl.BlockSpec((1, POOL_STATE, SEQ_BLK, B_WIDTH), lambda i, *_: (s_tile(i), 0, 0, 0)),
                _const_spec((1, D_MODEL)), _const_spec((D_MODEL, 3 * A_WIDTH)), _const_spec((1, A_WIDTH)),
                _const_spec((1, A_WIDTH)), _const_spec((N_HEADS, CHUNK, CHUNK)), _const_spec((CHUNK, A_WIDTH)),
                _const_spec((N_GROUPS, HEAD_DIM, HEAD_DIM)), _const_spec((1, B_WIDTH)),
                _const_spec((D_MODEL, D_MODEL)), _const_spec((1, D_MODEL)), _const_spec((D_MODEL, LANES)),
                _const_spec((N_ROUTER_ROWS, TM)), _const_spec((TM, TM)),
            ],
            out_specs=[
                pl.BlockSpec((TM, D_MODEL), lambda i, *_: (i, 0)),
                pl.BlockSpec((TM * ROW_TILES, LANES), lambda i, *_: (i, 0)),
                pl.BlockSpec((SUBLANES, TM), lambda i, *_: (0, i)),
                pl.BlockSpec((TM, LANES), lambda i, *_: (i, 0)),
                pl.BlockSpec((1, N_EXPERTS, LANES), lambda i, *_: (i // part_tiles, 0, 0)),
                pl.BlockSpec((1, CARRY_ROWS, B_WIDTH), lambda i, *_: (p_tile(i) // seq_tiles, 0, 0)),
                pl.BlockSpec((1, dec_seq, SEQ_BLK, A_WIDTH), lambda i, *_: (s_tile(i), 0, 0, 0)),
                pl.BlockSpec((1, dec_seq, SEQ_BLK, A_WIDTH), lambda i, *_: (s_tile(i), 0, 0, 0)),
            ],
            scratch_shapes=[pltpu.VMEM((TM, D_MODEL), bf16), pltpu.VMEM((CARRY_ROWS, B_WIDTH), f32),
                            pltpu.VMEM((N_EXPERTS, LANES), f32)],
        ),
        out_shape=[
            jax.ShapeDtypeStruct((t_total, D_MODEL), f32),
            jax.ShapeDtypeStruct((plane_rows, LANES), f32),
            jax.ShapeDtypeStruct((SUBLANES, t_total), i32),
            jax.ShapeDtypeStruct((t_total, LANES), f32),
            jax.ShapeDtypeStruct((N_PARTS, N_EXPERTS, LANES), f32),
            jax.ShapeDtypeStruct((n_batch, CARRY_ROWS, B_WIDTH), f32),
            plane_shape, plane_shape,
        ],
        compiler_params=cparams,
        name="mixer",
    )(aws_s, abs_s, x_prompt, xs_planes, st_planes, n1g, win, ang, anb, a_ws[0][:, :CHUNK, :CHUNK], abias, bw,
      bscale, wout, n2g, rwt, rb, su)

    assert n_exp_tiles <= LANES and t_part % LANES == 0
    asg, tab = pl.pallas_call(
        functools.partial(_tables_kernel, t_total=t_total),
        grid=(N_PARTS,),
        in_specs=[pl.BlockSpec((1, N_EXPERTS, LANES), lambda h: (h, 0, 0)),
                  pl.BlockSpec((SUBLANES, t_part), lambda h: (0, h))],
        out_specs=[pl.BlockSpec((1, 1, p_rows), lambda h: (h, 0, 0), memory_space=pltpu.SMEM),
                   pl.BlockSpec((1, SUBLANES, LANES), lambda h: (h, 0, 0))],
        scratch_shapes=[pltpu.VMEM((TOP_K, 1, t_part), i32), pltpu.VMEM((1, 1, p_rows), i32)]
        + [pltpu.SMEM((1, t_part), i32)] * TOP_K,
        out_shape=[jax.ShapeDtypeStruct((N_PARTS, 1, p_rows), i32),
                   jax.ShapeDtypeStruct((N_PARTS, SUBLANES, LANES), i32)],
        compiler_params=cparams,
        name="route_tables",
    )(counts, meta)
    tile_e = tab[:, 0, :n_exp_tiles].reshape(-1)
    tile_rows = tab[:, 1, :n_exp_tiles].reshape(-1)
    n_valid = tab[:, 2, 0]
    asg = asg.reshape(-1)
    row_part = jnp.arange(N_PARTS * p_rows, dtype=i32) // p_rows
    row_tok = jnp.where(asg >= t_total, asg - t_total, asg)
    row_src = jnp.where(asg >= 0, row_tok - row_part * t_part, 0).reshape(N_PARTS * n_exp_tiles, 1, TE)
    row_dst = jnp.maximum(asg, 0).reshape(N_PARTS * n_exp_tiles, 1, TE)

    flat = lambda h, i: h * n_exp_tiles + i
    cur_blk = lambda h, i, te, nv, nr: (flat(h, jnp.minimum(i, nv[h] - 1)), 0, 0)
    w_blk = lambda h, i, te, nv, nr: (te[flat(h, i)], 0, 0)
    smem_blk = lambda imap: pl.BlockSpec((1, 1, TE), imap, memory_space=pltpu.SMEM)
    out_tok = pl.pallas_call(
        _expert_kernel,
        grid_spec=pltpu.PrefetchScalarGridSpec(
            num_scalar_prefetch=3,
            grid=(N_PARTS, n_exp_tiles),
            in_specs=[smem_blk(cur_blk), smem_blk(cur_blk),
                      pl.BlockSpec((t_part * ROW_TILES, LANES), lambda h, i, *_: (h, 0),
                                   pipeline_mode=pl.Buffered(1)),
                      pl.BlockSpec((1, D_MODEL, D_EXPERT), w_blk),
                      pl.BlockSpec((1, D_MODEL, D_EXPERT), w_blk),
                      pl.BlockSpec((1, D_EXPERT, D_MODEL), w_blk)],
            out_specs=pl.BlockSpec(memory_space=pl.ANY),
            scratch_shapes=[pltpu.VMEM((TE * ROW_TILES, LANES), f32),
                            pltpu.VMEM((2, TE * ROW_TILES, LANES), f32),
                            pltpu.VMEM((D_MODEL, D_EXPERT), bf16), pltpu.VMEM((D_MODEL, D_EXPERT), bf16),
                            pltpu.VMEM((D_EXPERT, D_MODEL), bf16),
                            pltpu.SemaphoreType.DMA((2,))],
        ),
        out_shape=jax.ShapeDtypeStruct((TOP_K * t_total, ROW_TILES, LANES), f32),
        compiler_params=pltpu.CompilerParams(dimension_semantics=("arbitrary", "arbitrary"),
                                             vmem_limit_bytes=EXPERT_VMEM_LIMIT),
        name="moe_experts",
    )(tile_e, n_valid, tile_rows, row_src, row_dst, h2, exp_w1[0], exp_w3[0], exp_w2[0])
    out_tok = out_tok.reshape(TOP_K * plane_rows, LANES)

    y_prompt, y_sample = pl.pallas_call(
        functools.partial(_combine_kernel, n_prompt_tiles=n_prompt_tiles),
        grid=(n_tok_tiles,),
        in_specs=[pl.BlockSpec((TM, D_MODEL), lambda i: (i, 0)),
                  pl.BlockSpec((TM, LANES), lambda i: (i, 0)),
                  _const_spec((1, D_MODEL)),
                  pl.BlockSpec((TM * ROW_TILES, LANES), lambda i: (i, 0)),
                  pl.BlockSpec((TM * ROW_TILES, LANES), lambda i: (n_tok_tiles + i, 0))],
        out_specs=[pl.BlockSpec((TM, D_MODEL), lambda i: (p_tile(i), 0)),
                   pl.BlockSpec((TM, D_MODEL), lambda i: (s_tile(i), 0))],
        out_shape=[jax.ShapeDtypeStruct((t_prompt, D_MODEL), f32), jax.ShapeDtypeStruct((t_sample, D_MODEL), f32)],
        compiler_params=cparams,
        name="moe_combine",
    )(x1, gcol, nfg, out_tok, out_tok)

    unplane = lambda a: a.transpose(0, 2, 1, 3).reshape(dec_batch, dec_seq, a.shape[-1])
    y_prompt = y_prompt.reshape(n_batch, seq, D_MODEL)
    y_sample = unplane(y_sample.reshape(n_sample_tiles, dec_seq, SEQ_BLK, D_MODEL))
    pool_state_prompt = pstate[None, :, CARRY_ROWS - POOL_STATE:, :]
    p_s = unplane(pplanes)
    pool_state_sample = jnp.concatenate([state_pool[0], p_s], axis=1)[None, :, -POOL_STATE:, :]
    chunk_v_sample = unplane(vplanes)[None]
    return (y_prompt, y_sample, pool_state_prompt, pool_state_sample, chunk_v_sample)
```

```python
import functools
import math

import jax
import jax.numpy as jnp
from jax import lax
from jax.experimental import pallas as pl
from jax.experimental.pallas import tpu as pltpu

D_MODEL = 1024
A_WIDTH = 512
B_WIDTH = 512
N_HEADS = 4
HEAD_DIM = 128
CHUNK = 128
POOL_WINDOWS = (2, 4, 8, 16)
POOL_STATE = 15
N_GROUPS = 4
EXPERTS_PER_GROUP = 8
N_EXPERTS = 32
TOP_K = 2
D_EXPERT = 512
EPS = 1e-6

SUBLANES = 8
LANES = 128
ROW_TILES = D_MODEL // LANES

TM = 256
TE = 256
SEQ_BLK = 32
N_ROUTER_ROWS = 40
CARRY_ROWS = 16
DMA_UNROLL = 32
VMEM_LIMIT = 48 * 1024 * 1024
N_PARTS = 2
EXPERT_VMEM_LIMIT = 62 * 1024 * 1024

_INV_SQRT2 = 1.0 / math.sqrt(2.0)


def _rmsnorm(x, g):
    r = lax.rsqrt(jnp.mean(x * x, axis=-1, keepdims=True) + EPS)
    return (x * r) * g


def _gelu(x):
    return 0.5 * x * (1.0 + lax.erf(x * _INV_SQRT2))


def _layernorm(x, g, b):
    mu = jnp.mean(x, axis=-1, keepdims=True)
    xc = x - mu
    var = jnp.mean(xc * xc, axis=-1, keepdims=True)
    return (xc * lax.rsqrt(var + EPS)) * g + b


def _row_slab(ref, s, n):
    return ref[pl.ds(s, n, stride=ROW_TILES), :]


def _pool_project(pooled, g, bw_ref, bscale_ref):
    lo, hi = g * HEAD_DIM, (g + 1) * HEAD_DIM
    hb = jnp.dot(pooled.astype(jnp.bfloat16), bw_ref[g], preferred_element_type=jnp.float32)
    return hb * bscale_ref[:, lo:hi]


def _prompt_mixers(j, u, v, p, aws_ref, abias_ref, bw_ref, bscale_ref, mix_ref, pcarry_ref, pstate_ref):
    tri = (lax.broadcasted_iota(jnp.int32, (CHUNK, CHUNK), 0)
           >= lax.broadcasted_iota(jnp.int32, (CHUNK, CHUNK), 1))
    vb = v.astype(jnp.bfloat16)
    for hd in range(N_HEADS):
        lo, hi = hd * HEAD_DIM, (hd + 1) * HEAD_DIM
        w = jnp.where(tri, aws_ref[hd], 0.0).astype(jnp.bfloat16)
        for c in range(TM // CHUNK):
            r0, r1 = c * CHUNK, (c + 1) * CHUNK
            z = jnp.dot(w, vb[r0:r1, lo:hi], preferred_element_type=jnp.float32) + abias_ref[:, lo:hi]
            mix_ref[r0:r1, lo:hi] = (u[r0:r1, lo:hi] * z).astype(jnp.bfloat16)

    head_pos = j * TM + lax.broadcasted_iota(jnp.int32, (CARRY_ROWS, LANES), 0)
    for g, w in enumerate(POOL_WINDOWS):
        lo, hi = g * HEAD_DIM, (g + 1) * HEAD_DIM
        pg = p[:, lo:hi]
        acc = jnp.concatenate([pcarry_ref[:, lo:hi], pg], axis=0)
        shift = 1
        while shift < w:
            acc = acc + pltpu.roll(acc, shift, 0)
            shift *= 2
        head = acc[CARRY_ROWS:2 * CARRY_ROWS, :] / jnp.minimum(head_pos + 1, w).astype(jnp.float32)
        mean = jnp.concatenate([head, acc[2 * CARRY_ROWS:, :] * (1.0 / w)], axis=0)
        pooled = mean - pg
        mix_ref[:, A_WIDTH + lo:A_WIDTH + hi] = _pool_project(pooled, g, bw_ref, bscale_ref).astype(jnp.bfloat16)
    tail = p[TM - CARRY_ROWS:, :]
    pcarry_ref[...] = tail
    pstate_ref[0] = tail


def _sample_mixers(u, v, p, aws_ref, abs_ref, st_ref, bw_ref, bscale_ref, mix_ref, pp_ref, vp_ref):
    n_pos = TM // SEQ_BLK
    for i in range(n_pos):
        vp_ref[0, i] = v[i * SEQ_BLK:(i + 1) * SEQ_BLK, :]
        pp_ref[0, i] = p[i * SEQ_BLK:(i + 1) * SEQ_BLK, :]

    for hd in range(N_HEADS):
        lo, hi = hd * HEAD_DIM, (hd + 1) * HEAD_DIM
        vplanes = [v[s * SEQ_BLK:(s + 1) * SEQ_BLK, lo:hi] for s in range(n_pos)]
        for i in range(n_pos):
            z = vplanes[0] * aws_ref[hd * 64 + i * 8]
            for s in range(1, i + 1):
                z = z + vplanes[s] * aws_ref[hd * 64 + i * 8 + s]
            z = z + abs_ref[hd * 8 + i]
            r0, r1 = i * SEQ_BLK, (i + 1) * SEQ_BLK
            mix_ref[r0:r1, lo:hi] = (u[r0:r1, lo:hi] * z).astype(jnp.bfloat16)

    for g, w in enumerate(POOL_WINDOWS):
        lo, hi = g * HEAD_DIM, (g + 1) * HEAD_DIM
        planes = [st_ref[0, k, :, lo:hi] for k in range(POOL_STATE)]
        planes += [p[i * SEQ_BLK:(i + 1) * SEQ_BLK, lo:hi] for i in range(n_pos)]
        pooled = []
        for i in range(n_pos):
            top = POOL_STATE + i
            s = planes[top - w + 1]
            for k in range(top - w + 2, top + 1):
                s = s + planes[k]
            pooled.append(s * (1.0 / w) - planes[top])
        pooled = jnp.concatenate(pooled, axis=0)
        mix_ref[:, A_WIDTH + lo:A_WIDTH + hi] = _pool_project(pooled, g, bw_ref, bscale_ref).astype(jnp.bfloat16)


def _route(h2, rwt_ref, rb_ref, su_ref, cnt_ref, meta_ref, gcol_ref):
    tm = h2.shape[0]
    h_hi = h2.astype(jnp.bfloat16)
    h_lo = (h2 - h_hi.astype(jnp.float32)).astype(jnp.bfloat16)
    s = (jnp.dot(h_hi, rwt_ref[...], preferred_element_type=jnp.float32)
         + jnp.dot(h_lo, rwt_ref[...], preferred_element_type=jnp.float32))
    st = s.T
    lt = st[0:N_ROUTER_ROWS, :] + st[N_ROUTER_ROWS:2 * N_ROUTER_ROWS, :] + rb_ref[...]
    row = lambda i: lt[i:i + 1, :]
    l1 = [row(i) for i in range(N_GROUPS)]
    m1 = jnp.maximum(jnp.maximum(l1[0], l1[1]), jnp.maximum(l1[2], l1[3]))
    grp = jnp.where(l1[0] == m1, 0, jnp.where(l1[1] == m1, 1, jnp.where(l1[2] == m1, 2, 3)))
    se = (jnp.exp(l1[0] - m1) + jnp.exp(l1[1] - m1)) + (jnp.exp(l1[2] - m1) + jnp.exp(l1[3] - m1))
    pg = 1.0 / se
    l2 = []
    for e in range(EXPERTS_PER_GROUP):
        c = [row(N_GROUPS + g * EXPERTS_PER_GROUP + e) for g in range(N_GROUPS)]
        l2.append(jnp.where(grp == 0, c[0], jnp.where(grp == 1, c[1], jnp.where(grp == 2, c[2], c[3]))))
    v0 = functools.reduce(jnp.maximum, l2)
    i0 = jnp.full_like(grp, EXPERTS_PER_GROUP - 1)
    for e in range(EXPERTS_PER_GROUP - 2, -1, -1):
        i0 = jnp.where(l2[e] == v0, e, i0)
    neg = jnp.float32(-jnp.inf)
    l2m = [jnp.where(i0 == e, neg, l2[e]) for e in range(EXPERTS_PER_GROUP)]
    v1 = functools.reduce(jnp.maximum, l2m)
    i1 = jnp.full_like(grp, EXPERTS_PER_GROUP - 1)
    for e in range(EXPERTS_PER_GROUP - 2, -1, -1):
        i1 = jnp.where((l2m[e] == v1) & (i0 != e), e, i1)
    d = jnp.exp(v1 - v0)
    g0 = pg / (1.0 + d)
    g1 = (pg * d) / (1.0 + d)
    e0 = grp * EXPERTS_PER_GROUP + i0
    e1 = grp * EXPERTS_PER_GROUP + i1

    eiota = lax.broadcasted_iota(jnp.int32, (N_EXPERTS, tm), 0)
    hit0 = eiota == e0
    hit1 = eiota == e1
    onehot = (hit0 | hit1).astype(jnp.bfloat16)
    prefix = jnp.dot(onehot, su_ref[...], preferred_element_type=jnp.float32)
    carry = cnt_ref[...]
    base = prefix + jnp.concatenate([carry] * (tm // LANES), axis=1)
    r0 = jnp.sum(jnp.where(hit0, base, 0.0), axis=0, keepdims=True)
    r1 = jnp.sum(jnp.where(hit1, base, 0.0), axis=0, keepdims=True)
    ones = jnp.ones((tm, LANES), jnp.bfloat16)
    cnt_ref[...] = carry + jnp.dot(onehot, ones, preferred_element_type=jnp.float32)

    meta_ref[0:1, :] = e0
    meta_ref[1:2, :] = e1
    meta_ref[2:3, :] = r0.astype(jnp.int32)
    meta_ref[3:4, :] = r1.astype(jnp.int32)
    meta_ref[4:8, :] = jnp.zeros((4, tm), jnp.int32)

    giota = lax.broadcasted_iota(jnp.int32, (LANES, tm), 0)
    gpad = jnp.where(giota == 0, g0, jnp.where(giota == 1, g1, 0.0))
    gcol_ref[...] = gpad.T


def _mixer_kernel(aws_s_ref, abs_s_ref,
                  xp_ref, xs_ref, st_ref, n1g_ref, win_ref, ang_ref, anb_ref, aws_ref, abias_ref, bw_ref, bscale_ref,
                  wout_ref, n2g_ref, rwt_ref, rb_ref, su_ref,
                  x1_ref, h2_ref, meta_ref, gcol_ref, counts_ref, pstate_ref, pp_ref, vp_ref,
                  mix_ref, pcarry_ref, cnt_ref, *, n_prompt_tiles, seq_tiles, part_tiles):
    i = pl.program_id(0)
    is_prompt = i < n_prompt_tiles
    j = i % seq_tiles

    @pl.when(i % part_tiles == 0)
    def _():
        cnt_ref[...] = jnp.zeros_like(cnt_ref)

    @pl.when(is_prompt & (j == 0))
    def _():
        pcarry_ref[...] = jnp.zeros_like(pcarry_ref)

    x = jnp.where(is_prompt, xp_ref[0], xs_ref[...])
    h = _rmsnorm(x, n1g_ref[...]).astype(jnp.bfloat16)
    proj = jnp.dot(h, win_ref[...], preferred_element_type=jnp.float32)
    uv = _gelu(proj[:, : 2 * A_WIDTH])
    u = uv[:, :A_WIDTH]
    v = _layernorm(uv[:, A_WIDTH:], ang_ref[...], anb_ref[...])
    p = proj[:, 2 * A_WIDTH:]

    @pl.when(is_prompt)
    def _():
        _prompt_mixers(j, u, v, p, aws_ref, abias_ref, bw_ref, bscale_ref, mix_ref, pcarry_ref, pstate_ref)

    @pl.when(jnp.logical_not(is_prompt))
    def _():
        _sample_mixers(u, v, p, aws_s_ref, abs_s_ref, st_ref, bw_ref, bscale_ref, mix_ref, pp_ref, vp_ref)

    x1 = x + jnp.dot(mix_ref[...], wout_ref[...], preferred_element_type=jnp.float32)
    x1_ref[...] = x1
    h2 = _rmsnorm(x1, n2g_ref[...])
    for s in range(ROW_TILES):
        h2_ref[pl.ds(s, TM, stride=ROW_TILES), :] = h2[:, s * LANES:(s + 1) * LANES]
    _route(h2, rwt_ref, rb_ref, su_ref, cnt_ref, meta_ref, gcol_ref)
    counts_ref[0] = cnt_ref[...]


def _tables_kernel(counts_ref, meta_ref, asg_ref, tab_ref, dest_vmem, fill_vmem, *dest_smem, t_total):
    h = pl.program_id(0)
    t_part = meta_ref.shape[1]
    cnt = counts_ref[0]
    padded = jnp.floor((cnt + (TE - 1)) * (1.0 / TE)) * TE
    sub = lax.broadcasted_iota(jnp.int32, cnt.shape, 0)
    pad_end = padded
    shift = 1
    while shift < N_EXPERTS:
        pad_end = pad_end + jnp.where(sub >= shift, pltpu.roll(pad_end, shift, 0), 0.0)
        shift *= 2
    pad_start = pad_end - padded
    base = pad_start.astype(jnp.int32)

    for k in range(TOP_K):
        e = meta_ref[k:k + 1, :]
        d = meta_ref[TOP_K + k:TOP_K + k + 1, :]
        for ex in range(N_EXPERTS):
            row = jnp.concatenate([base[ex:ex + 1, :]] * (t_part // LANES), axis=1)
            d = d + jnp.where(e == ex, row, 0)
        dest_vmem[k] = d
    for k, dsm in enumerate(dest_smem):
        pltpu.sync_copy(dest_vmem.at[k], dsm)

    fill_vmem[...] = jnp.full(fill_vmem.shape, -1, jnp.int32)
    pltpu.sync_copy(fill_vmem, asg_ref)
    unroll = 16
    for k, dsm in enumerate(dest_smem):
        def invert(_, carry, dsm=dsm):
            t, a = carry
            for uu in range(unroll):
                asg_ref[0, 0, dsm[0, t + uu]] = a + uu
            return t + unroll, a + unroll
        lax.fori_loop(0, t_part // unroll, invert, (jnp.int32(0), k * t_total + h * t_part))

    n_valid = jnp.maximum(pad_end[N_EXPERTS - 1:, :] * (1.0 / TE), 1.0)
    tile = jnp.minimum(lax.broadcasted_iota(jnp.int32, (1, LANES), 1).astype(jnp.float32), n_valid - 1.0)
    tile_start = tile * TE
    tile_e = jnp.minimum(jnp.sum((pad_end <= tile_start).astype(jnp.float32), axis=0, keepdims=True),
                         N_EXPERTS - 1.0)
    is_e = sub.astype(jnp.float32) == tile_e
    seg_start = jnp.sum(jnp.where(is_e, pad_start, 0.0), axis=0, keepdims=True)
    seg_cnt = jnp.sum(jnp.where(is_e, cnt, 0.0), axis=0, keepdims=True)
    rows = jnp.clip(seg_cnt - (tile_start - seg_start), 0.0, TE)
    tab_ref[0, 0:1, :] = tile_e.astype(jnp.int32)
    tab_ref[0, 1:2, :] = rows.astype(jnp.int32)
    tab_ref[0, 2:3, :] = n_valid.astype(jnp.int32)
    tab_ref[0, 3:, :] = jnp.zeros((SUBLANES - 3, LANES), jnp.int32)


def _expert_kernel(te_ref, nv_ref, nrows_ref,
                   src_ref, dst_ref,
                   h2_ref, w1_ref, w3_ref, w2_ref,
                   out_hbm,
                   xt, obuf, w1b, w3b, w2b, ssem):
    i = pl.program_id(1)
    g = pl.program_id(0) * pl.num_programs(1) + i
    nv = nv_ref[pl.program_id(0)]
    slot = i % 2
    slab = lambda r: pl.ds(pl.multiple_of(r * ROW_TILES, ROW_TILES), ROW_TILES)

    def scatter_copy(r, src_slot):
        return pltpu.make_async_copy(obuf.at[src_slot, slab(r), :], out_hbm.at[dst_ref[0, 0, r]], ssem.at[src_slot])

    def wait_scatter(tile, src_slot):
        n = nrows_ref[tile] * ROW_TILES
        pltpu.make_async_copy(obuf.at[1 - src_slot, pl.ds(0, n), :], obuf.at[src_slot, pl.ds(0, n), :],
                              ssem.at[src_slot]).wait()

    @pl.when(i < nv)
    def _():
        @pl.when((i == 0) | (te_ref[g] != te_ref[jnp.maximum(g - 1, 0)]))
        def _():
            w1b[...] = w1_ref[0].astype(jnp.bfloat16)
            w3b[...] = w3_ref[0].astype(jnp.bfloat16)
            w2b[...] = w2_ref[0].astype(jnp.bfloat16)

        for r in range(TE):
            xt[r * ROW_TILES:(r + 1) * ROW_TILES, :] = h2_ref[slab(src_ref[0, 0, r]), :]

        @pl.when(i >= 2)
        def _():
            wait_scatter(g - 2, slot)

        x = jnp.concatenate([_row_slab(xt, s, TE) for s in range(ROW_TILES)], axis=1).astype(jnp.bfloat16)
        a = jnp.dot(x, w1b[...], preferred_element_type=jnp.float32)
        b = jnp.dot(x, w3b[...], preferred_element_type=jnp.float32)
        h = (a * jax.nn.sigmoid(a)) * b
        o = jnp.dot(h.astype(jnp.bfloat16), w2b[...], preferred_element_type=jnp.float32)
        os = obuf.at[slot]
        for s in range(ROW_TILES):
            os[pl.ds(s, TE, stride=ROW_TILES), :] = o[:, s * LANES:(s + 1) * LANES]

        n = nrows_ref[g]
        n_full = n // DMA_UNROLL

        def body(rb, carry):
            for uu in range(DMA_UNROLL):
                r = rb * DMA_UNROLL + uu
                scatter_copy(r, slot).start(priority=1)
            return carry
        lax.fori_loop(0, n_full, body, 0)

        def tail(r, carry):
            scatter_copy(r, slot).start(priority=1)
            return carry
        lax.fori_loop(n_full * DMA_UNROLL, n, tail, 0)

        @pl.when(i == nv - 1)
        def _():
            @pl.when(i >= 1)
            def _():
                wait_scatter(g - 1, 1 - slot)
            wait_scatter(g, slot)


def _combine_kernel(x1_ref, gcol_ref, nfg_ref, o0_ref, o1_ref, yp_ref, ys_ref, *, n_prompt_tiles):
    i = pl.program_id(0)
    o0 = jnp.concatenate([_row_slab(o0_ref, s, TM) for s in range(ROW_TILES)], axis=1)
    o1 = jnp.concatenate([_row_slab(o1_ref, s, TM) for s in range(ROW_TILES)], axis=1)
    g = gcol_ref[...]
    moe = g[:, 0:1] * o0 + g[:, 1:2] * o1
    y = _rmsnorm(x1_ref[...] + moe, nfg_ref[...])

    @pl.when(i < n_prompt_tiles)
    def _():
        yp_ref[...] = y

    @pl.when(i >= n_prompt_tiles)
    def _():
        ys_ref[...] = y


def _const_spec(shape):
    return pl.BlockSpec(shape, lambda *_: (0,) * len(shape))


def kernel(x_prompt, x_sample, state_pool, norm1_g, w_in, a_norm_g, a_norm_b, a_ws, a_bs, b_w, b_scale, w_out,
           norm2_g, r1_w, r1_b, r2_w, r2_b, exp_w1, exp_w3, exp_w2, normf_g):
    f32, bf16, i32 = jnp.float32, jnp.bfloat16, jnp.int32
    n_batch, seq, _ = x_prompt.shape
    dec_batch, dec_seq, _ = x_sample.shape
    assert norm1_g.shape[0] == 1 and seq % TM == 0 and TM % CHUNK == 0
    assert dec_seq * SEQ_BLK == TM and dec_batch % SEQ_BLK == 0 and dec_seq <= CHUNK
    t_prompt = n_batch * seq
    t_sample = dec_batch * dec_seq
    t_total = t_prompt + t_sample
    n_tok_tiles = t_total // TM
    n_prompt_tiles = t_prompt // TM
    n_sample_tiles = t_sample // TM
    seq_tiles = seq // TM
    n_assign = TOP_K * t_total
    plane_rows = t_total * ROW_TILES
    assert n_tok_tiles % N_PARTS == 0
    part_tiles = n_tok_tiles // N_PARTS
    t_part = part_tiles * TM
    n_exp_tiles = -(-(TOP_K * t_part + N_EXPERTS * (TE - 1)) // TE)
    p_rows = n_exp_tiles * TE

    n1g = norm1_g[0][None, :]
    n2g = norm2_g[0][None, :]
    nfg = normf_g[None, :]
    win = w_in[0].astype(bf16)
    wout = w_out[0].astype(bf16)
    ang = a_norm_g[0][None, :]
    anb = a_norm_b[0][None, :]
    bw = b_w[0].astype(bf16)
    bscale = b_scale[0][None, :]
    abias = jnp.repeat(a_bs[0][:, :CHUNK].T, HEAD_DIM, axis=1)
    rw = jnp.concatenate([r1_w[0], r2_w[0].transpose(1, 0, 2).reshape(D_MODEL, N_EXPERTS),
                          jnp.zeros((D_MODEL, N_ROUTER_ROWS - N_GROUPS - N_EXPERTS), f32)], axis=1)
    rw_hi = rw.astype(bf16)
    rw_lo = (rw - rw_hi.astype(f32)).astype(bf16)
    rwt = jnp.concatenate([rw_hi, rw_lo, jnp.zeros((D_MODEL, LANES - 2 * N_ROUTER_ROWS), bf16)], axis=1)
    rbias = jnp.concatenate([r1_b[0], r2_b[0].reshape(-1),
                             jnp.zeros((N_ROUTER_ROWS - N_GROUPS - N_EXPERTS,), f32)])
    rb = jnp.broadcast_to(rbias[:, None], (N_ROUTER_ROWS, TM))
    su = (jnp.arange(TM)[:, None] < jnp.arange(TM)[None, :]).astype(bf16)
    xs_planes = (x_sample.reshape(n_sample_tiles, SEQ_BLK, dec_seq, D_MODEL)
                 .transpose(0, 2, 1, 3).reshape(t_sample, D_MODEL))
    st_planes = state_pool[0].reshape(n_sample_tiles, SEQ_BLK, POOL_STATE, B_WIDTH).transpose(0, 2, 1, 3)
    aws_s = a_ws[0][:, :dec_seq, :dec_seq].reshape(-1)
    abs_s = a_bs[0][:, :dec_seq].reshape(-1)

    cparams = pltpu.CompilerParams(dimension_semantics=("arbitrary",), vmem_limit_bytes=VMEM_LIMIT)

    p_tile = lambda i: jnp.minimum(i, n_prompt_tiles - 1)
    s_tile = lambda i: jnp.maximum(i - n_prompt_tiles, 0)
    plane_shape = jax.ShapeDtypeStruct((n_sample_tiles, dec_seq, SEQ_BLK, A_WIDTH), f32)
    x1, h2, meta, gcol, counts, pstate, pplanes, vplanes = pl.pallas_call(
        functools.partial(_mixer_kernel, n_prompt_tiles=n_prompt_tiles, seq_tiles=seq_tiles, part_tiles=part_tiles),
        grid_spec=pltpu.PrefetchScalarGridSpec(
            num_scalar_prefetch=2,
            grid=(n_tok_tiles,),
            in_specs=[
                pl.BlockSpec((1, TM, D_MODEL), lambda i, *_: (p_tile(i) // seq_tiles, p_tile(i) % seq_tiles, 0)),
                pl.BlockSpec((TM, D_MODEL), lambda i, *_: (s_tile(i), 0)),
                pl.BlockSpec((1, POOL_STATE, SEQ_BLK, B_WIDTH), lambda i, *_: (s_tile(i), 0, 0, 0)),
                _const_spec((1, D_MODEL)), _const_spec((D_MODEL, 3 * A_WIDTH)), _const_spec((1, A_WIDTH)),
                _const_spec((1, A_WIDTH)), _const_spec((N_HEADS, CHUNK, CHUNK)), _const_spec((CHUNK, A_WIDTH)),
                _const_spec((N_GROUPS, HEAD_DIM, HEAD_DIM)), _const_spec((1, B_WIDTH)),
                _const_spec((D_MODEL, D_MODEL)), _const_spec((1, D_MODEL)), _const_spec((D_MODEL, LANES)),
                _const_spec((N_ROUTER_ROWS, TM)), _const_spec((TM, TM)),
            ],
            out_specs=[
                pl.BlockSpec((TM, D_MODEL), lambda i, *_: (i, 0)),
                pl.BlockSpec((TM * ROW_TILES, LANES), lambda i, *_: (i, 0)),
                pl.BlockSpec((SUBLANES, TM), lambda i, *_: (0, i)),
                pl.BlockSpec((TM, LANES), lambda i, *_: (i, 0)),
                pl.BlockSpec((1, N_EXPERTS, LANES), lambda i, *_: (i // part_tiles, 0, 0)),
                pl.BlockSpec((1, CARRY_ROWS, B_WIDTH), lambda i, *_: (p_tile(i) // seq_tiles, 0, 0)),
                pl.BlockSpec((1, dec_seq, SEQ_BLK, A_WIDTH), lambda i, *_: (s_tile(i), 0, 0, 0)),
                pl.BlockSpec((1, dec_seq, SEQ_BLK, A_WIDTH), lambda i, *_: (s_tile(i), 0, 0, 0)),
            ],
            scratch_shapes=[pltpu.VMEM((TM, D_MODEL), bf16), pltpu.VMEM((CARRY_ROWS, B_WIDTH), f32),
                            pltpu.VMEM((N_EXPERTS, LANES), f32)],
        ),
        out_shape=[
            jax.ShapeDtypeStruct((t_total, D_MODEL), f32),
            jax.ShapeDtypeStruct((plane_rows, LANES), f32),
            jax.ShapeDtypeStruct((SUBLANES, t_total), i32),
            jax.ShapeDtypeStruct((t_total, LANES), f32),
            jax.ShapeDtypeStruct((N_PARTS, N_EXPERTS, LANES), f32),
            jax.ShapeDtypeStruct((n_batch, CARRY_ROWS, B_WIDTH), f32),
            plane_shape, plane_shape,
        ],
        compiler_params=cparams,
        name="mixer",
    )(aws_s, abs_s, x_prompt, xs_planes, st_planes, n1g, win, ang, anb, a_ws[0][:, :CHUNK, :CHUNK], abias, bw,
      bscale, wout, n2g, rwt, rb, su)

    assert n_exp_tiles <= LANES and t_part % LANES == 0
    asg, tab = pl.pallas_call(
        functools.partial(_tables_kernel, t_total=t_total),
        grid=(N_PARTS,),
        in_specs=[pl.BlockSpec((1, N_EXPERTS, LANES), lambda h: (h, 0, 0)),
                  pl.BlockSpec((SUBLANES, t_part), lambda h: (0, h))],
        out_specs=[pl.BlockSpec((1, 1, p_rows), lambda h: (h, 0, 0), memory_space=pltpu.SMEM),
                   pl.BlockSpec((1, SUBLANES, LANES), lambda h: (h, 0, 0))],
        scratch_shapes=[pltpu.VMEM((TOP_K, 1, t_part), i32), pltpu.VMEM((1, 1, p_rows), i32)]
        + [pltpu.SMEM((1, t_part), i32)] * TOP_K,
        out_shape=[jax.ShapeDtypeStruct((N_PARTS, 1, p_rows), i32),
                   jax.ShapeDtypeStruct((N_PARTS, SUBLANES, LANES), i32)],
        compiler_params=cparams,
        name="route_tables",
    )(counts, meta)
    tile_e = tab[:, 0, :n_exp_tiles].reshape(-1)
    tile_rows = tab[:, 1, :n_exp_tiles].reshape(-1)
    n_valid = tab[:, 2, 0]
    asg = asg.reshape(-1)
    row_part = jnp.arange(N_PARTS * p_rows, dtype=i32) // p_rows
    row_tok = jnp.where(asg >= t_total, asg - t_total, asg)
    row_src = jnp.where(asg >= 0, row_tok - row_part * t_part, 0).reshape(N_PARTS * n_exp_tiles, 1, TE)
    row_dst = jnp.maximum(asg, 0).reshape(N_PARTS * n_exp_tiles, 1, TE)

    flat = lambda h, i: h * n_exp_tiles + i
    cur_blk = lambda h, i, te, nv, nr: (flat(h, jnp.minimum(i, nv[h] - 1)), 0, 0)
    w_blk = lambda h, i, te, nv, nr: (te[flat(h, i)], 0, 0)
    smem_blk = lambda imap: pl.BlockSpec((1, 1, TE), imap, memory_space=pltpu.SMEM)
    out_tok = pl.pallas_call(
        _expert_kernel,
        grid_spec=pltpu.PrefetchScalarGridSpec(
            num_scalar_prefetch=3,
            grid=(N_PARTS, n_exp_tiles),
            in_specs=[smem_blk(cur_blk), smem_blk(cur_blk),
                      pl.BlockSpec((t_part * ROW_TILES, LANES), lambda h, i, *_: (h, 0),
                                   pipeline_mode=pl.Buffered(1)),
                      pl.BlockSpec((1, D_MODEL, D_EXPERT), w_blk),
                      pl.BlockSpec((1, D_MODEL, D_EXPERT), w_blk),
                      pl.BlockSpec((1, D_EXPERT, D_MODEL), w_blk)],
            out_specs=pl.BlockSpec(memory_space=pl.ANY),
            scratch_shapes=[pltpu.VMEM((TE * ROW_TILES, LANES), f32),
                            pltpu.VMEM((2, TE * ROW_TILES, LANES), f32),
                            pltpu.VMEM((D_MODEL, D_EXPERT), bf16), pltpu.VMEM((D_MODEL, D_EXPERT), bf16),
                            pltpu.VMEM((D_EXPERT, D_MODEL), bf16),
                            pltpu.SemaphoreType.DMA((2,))],
        ),
        out_shape=jax.ShapeDtypeStruct((TOP_K * t_total, ROW_TILES, LANES), f32),
        compiler_params=pltpu.CompilerParams(dimension_semantics=("arbitrary", "arbitrary"),
                                             vmem_limit_bytes=EXPERT_VMEM_LIMIT),
        name="moe_experts",
    )(tile_e, n_valid, tile_rows, row_src, row_dst, h2, exp_w1[0], exp_w3[0], exp_w2[0])
    out_tok = out_tok.reshape(TOP_K * plane_rows, LANES)

    y_prompt, y_sample = pl.pallas_call(
        functools.partial(_combine_kernel, n_prompt_tiles=n_prompt_tiles),
        grid=(n_tok_tiles,),
        in_specs=[pl.BlockSpec((TM, D_MODEL), lambda i: (i, 0)),
                  pl.BlockSpec((TM, LANES), lambda i: (i, 0)),
                  _const_spec((1, D_MODEL)),
                  pl.BlockSpec((TM * ROW_TILES, LANES), lambda i: (i, 0)),
                  pl.BlockSpec((TM * ROW_TILES, LANES), lambda i: (n_tok_tiles + i, 0))],
        out_specs=[pl.BlockSpec((TM, D_MODEL), lambda i: (p_tile(i), 0)),
                   pl.BlockSpec((TM, D_MODEL), lambda i: (s_tile(i), 0))],
        out_shape=[jax.ShapeDtypeStruct((t_prompt, D_MODEL), f32), jax.ShapeDtypeStruct((t_sample, D_MODEL), f32)],
        compiler_params=cparams,
        name="moe_combine",
    )(x1, gcol, nfg, out_tok, out_tok)

    unplane = lambda a: a.transpose(0, 2, 1, 3).reshape(dec_batch, dec_seq, a.shape[-1])
    y_prompt = y_prompt.reshape(n_batch, seq, D_MODEL)
    y_sample = unplane(y_sample.reshape(n_sample_tiles, dec_seq, SEQ_BLK, D_MODEL))
    pool_state_prompt = pstate[None, :, CARRY_ROWS - POOL_STATE:, :]
    p_s = unplane(pplanes)
    pool_state_sample = jnp.concatenate([state_pool[0], p_s], axis=1)[None, :, -POOL_STATE:, :]
    chunk_v_sample = unplane(vplanes)[None]
    return (y_prompt, y_sample, pool_state_prompt, pool_state_sample, chunk_v_sample)
```

```python
import functools
import math

import jax
import jax.numpy as jnp
from jax import lax
from jax.experimental import pallas as pl
from jax.experimental.pallas import tpu as pltpu

D_MODEL = 1024
A_WIDTH = 512
B_WIDTH = 512
N_HEADS = 4
HEAD_DIM = 128
CHUNK = 128
POOL_WINDOWS = (2, 4, 8, 16)
POOL_STATE = 15
N_GROUPS = 4
EXPERTS_PER_GROUP = 8
N_EXPERTS = 32
TOP_K = 2
D_EXPERT = 512
EPS = 1e-6

SUBLANES = 8
LANES = 128
ROW_TILES = D_MODEL // LANES

TM = 256
TE = 256
SEQ_BLK = 32
N_ROUTER_ROWS = 40
CARRY_ROWS = 16
DMA_UNROLL = 32
VMEM_LIMIT = 48 * 1024 * 1024
N_PARTS = 2
EXPERT_VMEM_LIMIT = 62 * 1024 * 1024

_INV_SQRT2 = 1.0 / math.sqrt(2.0)


def _rmsnorm(x, g):
    r = lax.rsqrt(jnp.mean(x * x, axis=-1, keepdims=True) + EPS)
    return (x * r) * g


def _gelu(x):
    return 0.5 * x * (1.0 + lax.erf(x * _INV_SQRT2))


def _layernorm(x, g, b):
    mu = jnp.mean(x, axis=-1, keepdims=True)
    xc = x - mu
    var = jnp.mean(xc * xc, axis=-1, keepdims=True)
    return (xc * lax.rsqrt(var + EPS)) * g + b


def _row_slab(ref, s, n):
    return ref[pl.ds(s, n, stride=ROW_TILES), :]


def _pool_project(pooled, g, bw_ref, bscale_ref):
    lo, hi = g * HEAD_DIM, (g + 1) * HEAD_DIM
    hb = jnp.dot(pooled.astype(jnp.bfloat16), bw_ref[g], preferred_element_type=jnp.float32)
    return hb * bscale_ref[:, lo:hi]


def _prompt_mixers(j, u, v, p, aws_ref, abias_ref, bw_ref, bscale_ref, mix_ref, pcarry_ref, pstate_ref):
    tri = (lax.broadcasted_iota(jnp.int32, (CHUNK, CHUNK), 0)
           >= lax.broadcasted_iota(jnp.int32, (CHUNK, CHUNK), 1))
    vb = v.astype(jnp.bfloat16)
    for hd in range(N_HEADS):
        lo, hi = hd * HEAD_DIM, (hd + 1) * HEAD_DIM
        w = jnp.where(tri, aws_ref[hd], 0.0).astype(jnp.bfloat16)
        for c in range(TM // CHUNK):
            r0, r1 = c * CHUNK, (c + 1) * CHUNK
            z = jnp.dot(w, vb[r0:r1, lo:hi], preferred_element_type=jnp.float32) + abias_ref[:, lo:hi]
            mix_ref[r0:r1, lo:hi] = (u[r0:r1, lo:hi] * z).astype(jnp.bfloat16)

    head_pos = j * TM + lax.broadcasted_iota(jnp.int32, (CARRY_ROWS, LANES), 0)
    for g, w in enumerate(POOL_WINDOWS):
        lo, hi = g * HEAD_DIM, (g + 1) * HEAD_DIM
        pg = p[:, lo:hi]
        acc = jnp.concatenate([pcarry_ref[:, lo:hi], pg], axis=0)
        shift = 1
        while shift < w:
            acc = acc + pltpu.roll(acc, shift, 0)
            shift *= 2
        head = acc[CARRY_ROWS:2 * CARRY_ROWS, :] / jnp.minimum(head_pos + 1, w).astype(jnp.float32)
        mean = jnp.concatenate([head, acc[2 * CARRY_ROWS:, :] * (1.0 / w)], axis=0)
        pooled = mean - pg
        mix_ref[:, A_WIDTH + lo:A_WIDTH + hi] = _pool_project(pooled, g, bw_ref, bscale_ref).astype(jnp.bfloat16)
    tail = p[TM - CARRY_ROWS:, :]
    pcarry_ref[...] = tail
    pstate_ref[0] = tail


def _sample_mixers(u, v, p, aws_ref, abs_ref, st_ref, bw_ref, bscale_ref, mix_ref, pp_ref, vp_ref):
    n_pos = TM // SEQ_BLK
    for i in range(n_pos):
        vp_ref[0, i] = v[i * SEQ_BLK:(i + 1) * SEQ_BLK, :]
        pp_ref[0, i] = p[i * SEQ_BLK:(i + 1) * SEQ_BLK, :]

    for hd in range(N_HEADS):
        lo, hi = hd * HEAD_DIM, (hd + 1) * HEAD_DIM
        vplanes = [v[s * SEQ_BLK:(s + 1) * SEQ_BLK, lo:hi] for s in range(n_pos)]
        for i in range(n_pos):
            z = vplanes[0] * aws_ref[hd * 64 + i * 8]
            for s in range(1, i + 1):
                z = z + vplanes[s] * aws_ref[hd * 64 + i * 8 + s]
            z = z + abs_ref[hd * 8 + i]
            r0, r1 = i * SEQ_BLK, (i + 1) * SEQ_BLK
            mix_ref[r0:r1, lo:hi] = (u[r0:r1, lo:hi] * z).astype(jnp.bfloat16)

    for g, w in enumerate(POOL_WINDOWS):
        lo, hi = g * HEAD_DIM, (g + 1) * HEAD_DIM
        planes = [st_ref[0, k, :, lo:hi] for k in range(POOL_STATE)]
        planes += [p[i * SEQ_BLK:(i + 1) * SEQ_BLK, lo:hi] for i in range(n_pos)]
        pooled = []
        for i in range(n_pos):
            top = POOL_STATE + i
            s = planes[top - w + 1]
            for k in range(top - w + 2, top + 1):
                s = s + planes[k]
            pooled.append(s * (1.0 / w) - planes[top])
        pooled = jnp.concatenate(pooled, axis=0)
        mix_ref[:, A_WIDTH + lo:A_WIDTH + hi] = _pool_project(pooled, g, bw_ref, bscale_ref).astype(jnp.bfloat16)


def _route(h2, rwt_ref, rb_ref, su_ref, cnt_ref, meta_ref, gcol_ref):
    tm = h2.shape[0]
    h_hi = h2.astype(jnp.bfloat16)
    h_lo = (h2 - h_hi.astype(jnp.float32)).astype(jnp.bfloat16)
    s = (jnp.dot(h_hi, rwt_ref[...], preferred_element_type=jnp.float32)
         + jnp.dot(h_lo, rwt_ref[...], preferred_element_type=jnp.float32))
    st = s.T
    lt = st[0:N_ROUTER_ROWS, :] + st[N_ROUTER_ROWS:2 * N_ROUTER_ROWS, :] + rb_ref[...]
    row = lambda i: lt[i:i + 1, :]
    l1 = [row(i) for i in range(N_GROUPS)]
    m1 = jnp.maximum(jnp.maximum(l1[0], l1[1]), jnp.maximum(l1[2], l1[3]))
    grp = jnp.where(l1[0] == m1, 0, jnp.where(l1[1] == m1, 1, jnp.where(l1[2] == m1, 2, 3)))
    se = (jnp.exp(l1[0] - m1) + jnp.exp(l1[1] - m1)) + (jnp.exp(l1[2] - m1) + jnp.exp(l1[3] - m1))
    pg = 1.0 / se
    l2 = []
    for e in range(EXPERTS_PER_GROUP):
        c = [row(N_GROUPS + g * EXPERTS_PER_GROUP + e) for g in range(N_GROUPS)]
        l2.append(jnp.where(grp == 0, c[0], jnp.where(grp == 1, c[1], jnp.where(grp == 2, c[2], c[3]))))
    v0 = functools.reduce(jnp.maximum, l2)
    i0 = jnp.full_like(grp, EXPERTS_PER_GROUP - 1)
    for e in range(EXPERTS_PER_GROUP - 2, -1, -1):
        i0 = jnp.where(l2[e] == v0, e, i0)
    neg = jnp.float32(-jnp.inf)
    l2m = [jnp.where(i0 == e, neg, l2[e]) for e in range(EXPERTS_PER_GROUP)]
    v1 = functools.reduce(jnp.maximum, l2m)
    i1 = jnp.full_like(grp, EXPERTS_PER_GROUP - 1)
    for e in range(EXPERTS_PER_GROUP - 2, -1, -1):
        i1 = jnp.where((l2m[e] == v1) & (i0 != e), e, i1)
    d = jnp.exp(v1 - v0)
    g0 = pg / (1.0 + d)
    g1 = (pg * d) / (1.0 + d)
    e0 = grp * EXPERTS_PER_GROUP + i0
    e1 = grp * EXPERTS_PER_GROUP + i1

    eiota = lax.broadcasted_iota(jnp.int32, (N_EXPERTS, tm), 0)
    hit0 = eiota == e0
    hit1 = eiota == e1
    onehot = (hit0 | hit1).astype(jnp.bfloat16)
    prefix = jnp.dot(onehot, su_ref[...], preferred_element_type=jnp.float32)
    carry = cnt_ref[...]
    base = prefix + jnp.concatenate([carry] * (tm // LANES), axis=1)
    r0 = jnp.sum(jnp.where(hit0, base, 0.0), axis=0, keepdims=True)
    r1 = jnp.sum(jnp.where(hit1, base, 0.0), axis=0, keepdims=True)
    ones = jnp.ones((tm, LANES), jnp.bfloat16)
    cnt_ref[...] = carry + jnp.dot(onehot, ones, preferred_element_type=jnp.float32)

    meta_ref[0:1, :] = e0
    meta_ref[1:2, :] = e1
    meta_ref[2:3, :] = r0.astype(jnp.int32)
    meta_ref[3:4, :] = r1.astype(jnp.int32)
    meta_ref[4:8, :] = jnp.zeros((4, tm), jnp.int32)

    giota = lax.broadcasted_iota(jnp.int32, (LANES, tm), 0)
    gpad = jnp.where(giota == 0, g0, jnp.where(giota == 1, g1, 0.0))
    gcol_ref[...] = gpad.T


def _mixer_kernel(aws_s_ref, abs_s_ref,
                  xp_ref, xs_ref, st_ref, n1g_ref, win_ref, ang_ref, anb_ref, aws_ref, abias_ref, bw_ref, bscale_ref,
                  wout_ref, n2g_ref, rwt_ref, rb_ref, su_ref,
                  x1_ref, h2_ref, meta_ref, gcol_ref, counts_ref, pstate_ref, pp_ref, vp_ref,
                  mix_ref, pcarry_ref, cnt_ref, *, n_prompt_tiles, seq_tiles, part_tiles):
    i = pl.program_id(0)
    is_prompt = i < n_prompt_tiles
    j = i % seq_tiles

    @pl.when(i % part_tiles == 0)
    def _():
        cnt_ref[...] = jnp.zeros_like(cnt_ref)

    @pl.when(is_prompt & (j == 0))
    def _():
        pcarry_ref[...] = jnp.zeros_like(pcarry_ref)

    x = jnp.where(is_prompt, xp_ref[0], xs_ref[...])
    h = _rmsnorm(x, n1g_ref[...]).astype(jnp.bfloat16)
    proj = jnp.dot(h, win_ref[...], preferred_element_type=jnp.float32)
    uv = _gelu(proj[:, : 2 * A_WIDTH])
    u = uv[:, :A_WIDTH]
    v = _layernorm(uv[:, A_WIDTH:], ang_ref[...], anb_ref[...])
    p = proj[:, 2 * A_WIDTH:]

    @pl.when(is_prompt)
    def _():
        _prompt_mixers(j, u, v, p, aws_ref, abias_ref, bw_ref, bscale_ref, mix_ref, pcarry_ref, pstate_ref)

    @pl.when(jnp.logical_not(is_prompt))
    def _():
        _sample_mixers(u, v, p, aws_s_ref, abs_s_ref, st_ref, bw_ref, bscale_ref, mix_ref, pp_ref, vp_ref)

    x1 = x + jnp.dot(mix_ref[...], wout_ref[...], preferred_element_type=jnp.float32)
    x1_ref[...] = x1
    h2 = _rmsnorm(x1, n2g_ref[...])
    for s in range(ROW_TILES):
        h2_ref[pl.ds(s, TM, stride=ROW_TILES), :] = h2[:, s * LANES:(s + 1) * LANES]
    _route(h2, rwt_ref, rb_ref, su_ref, cnt_ref, meta_ref, gcol_ref)
    counts_ref[0] = cnt_ref[...]


def _tables_kernel(counts_ref, meta_ref, asg_ref, tab_ref, dest_vmem, fill_vmem, *dest_smem, t_total):
    h = pl.program_id(0)
    t_part = meta_ref.shape[1]
    cnt = counts_ref[0]
    padded = jnp.floor((cnt + (TE - 1)) * (1.0 / TE)) * TE
    sub = lax.broadcasted_iota(jnp.int32, cnt.shape, 0)
    pad_end = padded
    shift = 1
    while shift < N_EXPERTS:
        pad_end = pad_end + jnp.where(sub >= shift, pltpu.roll(pad_end, shift, 0), 0.0)
        shift *= 2
    pad_start = pad_end - padded
    base = pad_start.astype(jnp.int32)

    for k in range(TOP_K):
        e = meta_ref[k:k + 1, :]
        d = meta_ref[TOP_K + k:TOP_K + k + 1, :]
        for ex in range(N_EXPERTS):
            row = jnp.concatenate([base[ex:ex + 1, :]] * (t_part // LANES), axis=1)
            d = d + jnp.where(e == ex, row, 0)
        dest_vmem[k] = d
    for k, dsm in enumerate(dest_smem):
        pltpu.sync_copy(dest_vmem.at[k], dsm)

    fill_vmem[...] = jnp.full(fill_vmem.shape, -1, jnp.int32)
    pltpu.sync_copy(fill_vmem, asg_ref)
    unroll = 16
    for k, dsm in enumerate(dest_smem):
        def invert(_, carry, dsm=dsm):
            t, a = carry
            for uu in range(unroll):
                asg_ref[0, 0, dsm[0, t + uu]] = a + uu
            return t + unroll, a + unroll
        lax.fori_loop(0, t_part // unroll, invert, (jnp.int32(0), k * t_total + h * t_part))

    n_valid = jnp.maximum(pad_end[N_EXPERTS - 1:, :] * (1.0 / TE), 1.0)
    tile = jnp.minimum(lax.broadcasted_iota(jnp.int32, (1, LANES), 1).astype(jnp.float32), n_valid - 1.0)
    tile_start = tile * TE
    tile_e = jnp.minimum(jnp.sum((pad_end <= tile_start).astype(jnp.float32), axis=0, keepdims=True),
                         N_EXPERTS - 1.0)
    is_e = sub.astype(jnp.float32) == tile_e
    seg_start = jnp.sum(jnp.where(is_e, pad_start, 0.0), axis=0, keepdims=True)
    seg_cnt = jnp.sum(jnp.where(is_e, cnt, 0.0), axis=0, keepdims=True)
    rows = jnp.clip(seg_cnt - (tile_start - seg_start), 0.0, TE)
    tab_ref[0, 0:1, :] = tile_e.astype(jnp.int32)
    tab_ref[0, 1:2, :] = rows.astype(jnp.int32)
    tab_ref[0, 2:3, :] = n_valid.astype(jnp.int32)
    tab_ref[0, 3:, :] = jnp.zeros((SUBLANES - 3, LANES), jnp.int32)


def _expert_kernel(te_ref, nv_ref, nrows_ref,
                   src_ref, src_next_ref, dst_ref,
                   h2_ref, w1_ref, w3_ref, w2_ref,
                   out_hbm,
                   xt0, xt1, ob0, ob1, w1b, w3b, w2b, ssem):
    i = pl.program_id(1)
    g = pl.program_id(0) * pl.num_programs(1) + i
    nv = nv_ref[pl.program_id(0)]
    slab = lambda r: pl.ds(pl.multiple_of(r * ROW_TILES, ROW_TILES), ROW_TILES)

    def gather(idx_ref, xt):
        for r in range(TE):
            xt[r * ROW_TILES:(r + 1) * ROW_TILES, :] = h2_ref[slab(idx_ref[0, 0, r]), :]

    def wait_scatter(tile, ob, other, sem):
        n = nrows_ref[tile] * ROW_TILES
        pltpu.make_async_copy(other.at[pl.ds(0, n), :], ob.at[pl.ds(0, n), :], sem).wait()

    def tile_body(xt, xt_next, ob, ob_other, sem, sem_other):
        @pl.when(i >= 2)
        def _():
            wait_scatter(g - 2, ob, ob_other, sem)

        gather(src_next_ref, xt_next)
        x = jnp.concatenate([_row_slab(xt, s, TE) for s in range(ROW_TILES)], axis=1).astype(jnp.bfloat16)
        a = jnp.dot(x, w1b[...], preferred_element_type=jnp.float32)
        b = jnp.dot(x, w3b[...], preferred_element_type=jnp.float32)
        h = (a * jax.nn.sigmoid(a)) * b
        o = jnp.dot(h.astype(jnp.bfloat16), w2b[...], preferred_element_type=jnp.float32)
        for s in range(ROW_TILES):
            ob[pl.ds(s, TE, stride=ROW_TILES), :] = o[:, s * LANES:(s + 1) * LANES]

        n = nrows_ref[g]
        n_full = n // DMA_UNROLL
        copy = lambda r: pltpu.make_async_copy(ob.at[slab(r), :], out_hbm.at[dst_ref[0, 0, r]], sem)

        def body(rb, carry):
            for uu in range(DMA_UNROLL):
                copy(rb * DMA_UNROLL + uu).start(priority=1)
            return carry
        lax.fori_loop(0, n_full, body, 0)

        def tail(r, carry):
            copy(r).start(priority=1)
            return carry
        lax.fori_loop(n_full * DMA_UNROLL, n, tail, 0)

        @pl.when(i == nv - 1)
        def _():
            @pl.when(i >= 1)
            def _():
                wait_scatter(g - 1, ob_other, ob, sem_other)
            wait_scatter(g, ob, ob_other, sem)

    @pl.when(i < nv)
    def _():
        @pl.when((i == 0) | (te_ref[g] != te_ref[jnp.maximum(g - 1, 0)]))
        def _():
            w1b[...] = w1_ref[0].astype(jnp.bfloat16)
            w3b[...] = w3_ref[0].astype(jnp.bfloat16)
            w2b[...] = w2_ref[0].astype(jnp.bfloat16)

        @pl.when(i == 0)
        def _():
            gather(src_ref, xt0)

        @pl.when(i % 2 == 0)
        def _():
            tile_body(xt0, xt1, ob0, ob1, ssem.at[0], ssem.at[1])

        @pl.when(i % 2 == 1)
        def _():
            tile_body(xt1, xt0, ob1, ob0, ssem.at[1], ssem.at[0])


def _combine_kernel(x1_ref, gcol_ref, nfg_ref, o0_ref, o1_ref, yp_ref, ys_ref, *, n_prompt_tiles):
    i = pl.program_id(0)
    o0 = jnp.concatenate([_row_slab(o0_ref, s, TM) for s in range(ROW_TILES)], axis=1)
    o1 = jnp.concatenate([_row_slab(o1_ref, s, TM) for s in range(ROW_TILES)], axis=1)
    g = gcol_ref[...]
    moe = g[:, 0:1] * o0 + g[:, 1:2] * o1
    y = _rmsnorm(x1_ref[...] + moe, nfg_ref[...])

    @pl.when(i < n_prompt_tiles)
    def _():
        yp_ref[...] = y

    @pl.when(i >= n_prompt_tiles)
    def _():
        ys_ref[...] = y


def _const_spec(shape):
    return pl.BlockSpec(shape, lambda *_: (0,) * len(shape))


def kernel(x_prompt, x_sample, state_pool, norm1_g, w_in, a_norm_g, a_norm_b, a_ws, a_bs, b_w, b_scale, w_out,
           norm2_g, r1_w, r1_b, r2_w, r2_b, exp_w1, exp_w3, exp_w2, normf_g):
    f32, bf16, i32 = jnp.float32, jnp.bfloat16, jnp.int32
    n_batch, seq, _ = x_prompt.shape
    dec_batch, dec_seq, _ = x_sample.shape
    assert norm1_g.shape[0] == 1 and seq % TM == 0 and TM % CHUNK == 0
    assert dec_seq * SEQ_BLK == TM and dec_batch % SEQ_BLK == 0 and dec_seq <= CHUNK
    t_prompt = n_batch * seq
    t_sample = dec_batch * dec_seq
    t_total = t_prompt + t_sample
    n_tok_tiles = t_total // TM
    n_prompt_tiles = t_prompt // TM
    n_sample_tiles = t_sample // TM
    seq_tiles = seq // TM
    n_assign = TOP_K * t_total
    plane_rows = t_total * ROW_TILES
    assert n_tok_tiles % N_PARTS == 0
    part_tiles = n_tok_tiles // N_PARTS
    t_part = part_tiles * TM
    n_exp_tiles = -(-(TOP_K * t_part + N_EXPERTS * (TE - 1)) // TE)
    p_rows = n_exp_tiles * TE

    n1g = norm1_g[0][None, :]
    n2g = norm2_g[0][None, :]
    nfg = normf_g[None, :]
    win = w_in[0].astype(bf16)
    wout = w_out[0].astype(bf16)
    ang = a_norm_g[0][None, :]
    anb = a_norm_b[0][None, :]
    bw = b_w[0].astype(bf16)
    bscale = b_scale[0][None, :]
    abias = jnp.repeat(a_bs[0][:, :CHUNK].T, HEAD_DIM, axis=1)
    rw = jnp.concatenate([r1_w[0], r2_w[0].transpose(1, 0, 2).reshape(D_MODEL, N_EXPERTS),
                          jnp.zeros((D_MODEL, N_ROUTER_ROWS - N_GROUPS - N_EXPERTS), f32)], axis=1)
    rw_hi = rw.astype(bf16)
    rw_lo = (rw - rw_hi.astype(f32)).astype(bf16)
    rwt = jnp.concatenate([rw_hi, rw_lo, jnp.zeros((D_MODEL, LANES - 2 * N_ROUTER_ROWS), bf16)], axis=1)
    rbias = jnp.concatenate([r1_b[0], r2_b[0].reshape(-1),
                             jnp.zeros((N_ROUTER_ROWS - N_GROUPS - N_EXPERTS,), f32)])
    rb = jnp.broadcast_to(rbias[:, None], (N_ROUTER_ROWS, TM))
    su = (jnp.arange(TM)[:, None] < jnp.arange(TM)[None, :]).astype(bf16)
    xs_planes = (x_sample.reshape(n_sample_tiles, SEQ_BLK, dec_seq, D_MODEL)
                 .transpose(0, 2, 1, 3).reshape(t_sample, D_MODEL))
    st_planes = state_pool[0].reshape(n_sample_tiles, SEQ_BLK, POOL_STATE, B_WIDTH).transpose(0, 2, 1, 3)
    aws_s = a_ws[0][:, :dec_seq, :dec_seq].reshape(-1)
    abs_s = a_bs[0][:, :dec_seq].reshape(-1)

    cparams = pltpu.CompilerParams(dimension_semantics=("arbitrary",), vmem_limit_bytes=VMEM_LIMIT)

    p_tile = lambda i: jnp.minimum(i, n_prompt_tiles - 1)
    s_tile = lambda i: jnp.maximum(i - n_prompt_tiles, 0)
    plane_shape = jax.ShapeDtypeStruct((n_sample_tiles, dec_seq, SEQ_BLK, A_WIDTH), f32)
    x1, h2, meta, gcol, counts, pstate, pplanes, vplanes = pl.pallas_call(
        functools.partial(_mixer_kernel, n_prompt_tiles=n_prompt_tiles, seq_tiles=seq_tiles, part_tiles=part_tiles),
        grid_spec=pltpu.PrefetchScalarGridSpec(
            num_scalar_prefetch=2,
            grid=(n_tok_tiles,),
            in_specs=[
                pl.BlockSpec((1, TM, D_MODEL), lambda i, *_: (p_tile(i) // seq_tiles, p_tile(i) % seq_tiles, 0)),
                pl.BlockSpec((TM, D_MODEL), lambda i, *_: (s_tile(i), 0)),
                pl.BlockSpec((1, POOL_STATE, SEQ_BLK, B_WIDTH), lambda i, *_: (s_tile(i), 0, 0, 0)),
                _const_spec((1, D_MODEL)), _const_spec((D_MODEL, 3 * A_WIDTH)), _const_spec((1, A_WIDTH)),
                _const_spec((1, A_WIDTH)), _const_spec((N_HEADS, CHUNK, CHUNK)), _const_spec((CHUNK, A_WIDTH)),
                _const_spec((N_GROUPS, HEAD_DIM, HEAD_DIM)), _const_spec((1, B_WIDTH)),
                _const_spec((D_MODEL, D_MODEL)), _const_spec((1, D_MODEL)), _const_spec((D_MODEL, LANES)),
                _const_spec((N_ROUTER_ROWS, TM)), _const_spec((TM, TM)),
            ],
            out_specs=[
                pl.BlockSpec((TM, D_MODEL), lambda i, *_: (i, 0)),
                pl.BlockSpec((TM * ROW_TILES, LANES), lambda i, *_: (i, 0)),
                pl.BlockSpec((SUBLANES, TM), lambda i, *_: (0, i)),
                pl.BlockSpec((TM, LANES), lambda i, *_: (i, 0)),
                pl.BlockSpec((1, N_EXPERTS, LANES), lambda i, *_: (i // part_tiles, 0, 0)),
                pl.BlockSpec((1, CARRY_ROWS, B_WIDTH), lambda i, *_: (p_tile(i) // seq_tiles, 0, 0)),
                pl.BlockSpec((1, dec_seq, SEQ_BLK, A_WIDTH), lambda i, *_: (s_tile(i), 0, 0, 0)),
                pl.BlockSpec((1, dec_seq, SEQ_BLK, A_WIDTH), lambda i, *_: (s_tile(i), 0, 0, 0)),
            ],
            scratch_shapes=[pltpu.VMEM((TM, D_MODEL), bf16), pltpu.VMEM((CARRY_ROWS, B_WIDTH), f32),
                            pltpu.VMEM((N_EXPERTS, LANES), f32)],
        ),
        out_shape=[
            jax.ShapeDtypeStruct((t_total, D_MODEL), f32),
            jax.ShapeDtypeStruct((plane_rows, LANES), f32),
            jax.ShapeDtypeStruct((SUBLANES, t_total), i32),
            jax.ShapeDtypeStruct((t_total, LANES), f32),
            jax.ShapeDtypeStruct((N_PARTS, N_EXPERTS, LANES), f32),
            jax.ShapeDtypeStruct((n_batch, CARRY_ROWS, B_WIDTH), f32),
            plane_shape, plane_shape,
        ],
        compiler_params=cparams,
        name="mixer",
    )(aws_s, abs_s, x_prompt, xs_planes, st_planes, n1g, win, ang, anb, a_ws[0][:, :CHUNK, :CHUNK], abias, bw,
      bscale, wout, n2g, rwt, rb, su)

    assert n_exp_tiles <= LANES and t_part % LANES == 0
    asg, tab = pl.pallas_call(
        functools.partial(_tables_kernel, t_total=t_total),
        grid=(N_PARTS,),
        in_specs=[pl.BlockSpec((1, N_EXPERTS, LANES), lambda h: (h, 0, 0)),
                  pl.BlockSpec((SUBLANES, t_part), lambda h: (0, h))],
        out_specs=[pl.BlockSpec((1, 1, p_rows), lambda h: (h, 0, 0), memory_space=pltpu.SMEM),
                   pl.BlockSpec((1, SUBLANES, LANES), lambda h: (h, 0, 0))],
        scratch_shapes=[pltpu.VMEM((TOP_K, 1, t_part), i32), pltpu.VMEM((1, 1, p_rows), i32)]
        + [pltpu.SMEM((1, t_part), i32)] * TOP_K,
        out_shape=[jax.ShapeDtypeStruct((N_PARTS, 1, p_rows), i32),
                   jax.ShapeDtypeStruct((N_PARTS, SUBLANES, LANES), i32)],
        compiler_params=cparams,
        name="route_tables",
    )(counts, meta)
    tile_e = tab[:, 0, :n_exp_tiles].reshape(-1)
    tile_rows = tab[:, 1, :n_exp_tiles].reshape(-1)
    n_valid = tab[:, 2, 0]
    asg = asg.reshape(-1)
    row_part = jnp.arange(N_PARTS * p_rows, dtype=i32) // p_rows
    row_tok = jnp.where(asg >= t_total, asg - t_total, asg)
    row_src = jnp.where(asg >= 0, row_tok - row_part * t_part, 0).reshape(N_PARTS * n_exp_tiles, 1, TE)
    row_dst = jnp.maximum(asg, 0).reshape(N_PARTS * n_exp_tiles, 1, TE)

    flat = lambda h, i: h * n_exp_tiles + i
    cur_blk = lambda h, i, te, nv, nr: (flat(h, jnp.minimum(i, nv[h] - 1)), 0, 0)
    nxt_blk = lambda h, i, te, nv, nr: (flat(h, jnp.minimum(i + 1, nv[h] - 1)), 0, 0)
    w_blk = lambda h, i, te, nv, nr: (te[flat(h, i)], 0, 0)
    smem_blk = lambda imap: pl.BlockSpec((1, 1, TE), imap, memory_space=pltpu.SMEM)
    row_buf = pltpu.VMEM((TE * ROW_TILES, LANES), f32)
    out_tok = pl.pallas_call(
        _expert_kernel,
        grid_spec=pltpu.PrefetchScalarGridSpec(
            num_scalar_prefetch=3,
            grid=(N_PARTS, n_exp_tiles),
            in_specs=[smem_blk(cur_blk), smem_blk(nxt_blk), smem_blk(cur_blk),
                      pl.BlockSpec((t_part * ROW_TILES, LANES), lambda h, i, *_: (h, 0),
                                   pipeline_mode=pl.Buffered(1)),
                      pl.BlockSpec((1, D_MODEL, D_EXPERT), w_blk),
                      pl.BlockSpec((1, D_MODEL, D_EXPERT), w_blk),
                      pl.BlockSpec((1, D_EXPERT, D_MODEL), w_blk)],
            out_specs=pl.BlockSpec(memory_space=pl.ANY),
            scratch_shapes=[row_buf, row_buf, row_buf, row_buf,
                            pltpu.VMEM((D_MODEL, D_EXPERT), bf16), pltpu.VMEM((D_MODEL, D_EXPERT), bf16),
                            pltpu.VMEM((D_EXPERT, D_MODEL), bf16),
                            pltpu.SemaphoreType.DMA((2,))],
        ),
        out_shape=jax.ShapeDtypeStruct((TOP_K * t_total, ROW_TILES, LANES), f32),
        compiler_params=pltpu.CompilerParams(dimension_semantics=("arbitrary", "arbitrary"),
                                             vmem_limit_bytes=EXPERT_VMEM_LIMIT),
        name="moe_experts",
    )(tile_e, n_valid, tile_rows, row_src, row_src, row_dst, h2, exp_w1[0], exp_w3[0], exp_w2[0])
    out_tok = out_tok.reshape(TOP_K * plane_rows, LANES)

    y_prompt, y_sample = pl.pallas_call(
        functools.partial(_combine_kernel, n_prompt_tiles=n_prompt_tiles),
        grid=(n_tok_tiles,),
        in_specs=[pl.BlockSpec((TM, D_MODEL), lambda i: (i, 0)),
                  pl.BlockSpec((TM, LANES), lambda i: (i, 0)),
                  _const_spec((1, D_MODEL)),
                  pl.BlockSpec((TM * ROW_TILES, LANES), lambda i: (i, 0)),
                  pl.BlockSpec((TM * ROW_TILES, LANES), lambda i: (n_tok_tiles + i, 0))],
        out_specs=[pl.BlockSpec((TM, D_MODEL), lambda i: (p_tile(i), 0)),
                   pl.BlockSpec((TM, D_MODEL), lambda i: (s_tile(i), 0))],
        out_shape=[jax.ShapeDtypeStruct((t_prompt, D_MODEL), f32), jax.ShapeDtypeStruct((t_sample, D_MODEL), f32)],
        compiler_params=cparams,
        name="moe_combine",
    )(x1, gcol, nfg, out_tok, out_tok)

    unplane = lambda a: a.transpose(0, 2, 1, 3).reshape(dec_batch, dec_seq, a.shape[-1])
    y_prompt = y_prompt.reshape(n_batch, seq, D_MODEL)
    y_sample = unplane(y_sample.reshape(n_sample_tiles, dec_seq, SEQ_BLK, D_MODEL))
    pool_state_prompt = pstate[None, :, CARRY_ROWS - POOL_STATE:, :]
    p_s = unplane(pplanes)
    pool_state_sample = jnp.concatenate([state_pool[0], p_s], axis=1)[None, :, -POOL_STATE:, :]
    chunk_v_sample = unplane(vplanes)[None]
    return (y_prompt, y_sample, pool_state_prompt, pool_state_sample, chunk_v_sample)
```

```python
import functools
import math

import jax
import jax.numpy as jnp
from jax import lax
from jax.experimental import pallas as pl
from jax.experimental.pallas import tpu as pltpu

D_MODEL = 1024
A_WIDTH = 512
B_WIDTH = 512
N_HEADS = 4
HEAD_DIM = 128
CHUNK = 128
POOL_WINDOWS = (2, 4, 8, 16)
POOL_STATE = 15
N_GROUPS = 4
EXPERTS_PER_GROUP = 8
N_EXPERTS = 32
TOP_K = 2
D_EXPERT = 512
EPS = 1e-6

SUBLANES = 8
LANES = 128
ROW_TILES = D_MODEL // LANES

TM = 256
TE = 256
SEQ_BLK = 32
N_ROUTER_ROWS = 40
CARRY_ROWS = 16
DMA_UNROLL = 32
VMEM_LIMIT = 48 * 1024 * 1024
N_PARTS = 2
EXPERT_VMEM_LIMIT = 62 * 1024 * 1024

_INV_SQRT2 = 1.0 / math.sqrt(2.0)


def _rmsnorm(x, g):
    r = lax.rsqrt(jnp.mean(x * x, axis=-1, keepdims=True) + EPS)
    return (x * r) * g


def _gelu(x):
    return 0.5 * x * (1.0 + lax.erf(x * _INV_SQRT2))


def _layernorm(x, g, b):
    mu = jnp.mean(x, axis=-1, keepdims=True)
    xc = x - mu
    var = jnp.mean(xc * xc, axis=-1, keepdims=True)
    return (xc * lax.rsqrt(var + EPS)) * g + b


def _row_slab(ref, s, n):
    return ref[pl.ds(s, n, stride=ROW_TILES), :]


def _pool_project(pooled, g, bw_ref, bscale_ref):
    lo, hi = g * HEAD_DIM, (g + 1) * HEAD_DIM
    hb = jnp.dot(pooled.astype(jnp.bfloat16), bw_ref[g], preferred_element_type=jnp.float32)
    return hb * bscale_ref[:, lo:hi]


def _prompt_mixers(j, u, v, p, aws_ref, abias_ref, bw_ref, bscale_ref, mix_ref, pcarry_ref, pstate_ref):
    tri = (lax.broadcasted_iota(jnp.int32, (CHUNK, CHUNK), 0)
           >= lax.broadcasted_iota(jnp.int32, (CHUNK, CHUNK), 1))
    vb = v.astype(jnp.bfloat16)
    for hd in range(N_HEADS):
        lo, hi = hd * HEAD_DIM, (hd + 1) * HEAD_DIM
        w = jnp.where(tri, aws_ref[hd], 0.0).astype(jnp.bfloat16)
        for c in range(TM // CHUNK):
            r0, r1 = c * CHUNK, (c + 1) * CHUNK
            z = jnp.dot(w, vb[r0:r1, lo:hi], preferred_element_type=jnp.float32) + abias_ref[:, lo:hi]
            mix_ref[r0:r1, lo:hi] = (u[r0:r1, lo:hi] * z).astype(jnp.bfloat16)

    head_pos = j * TM + lax.broadcasted_iota(jnp.int32, (CARRY_ROWS, LANES), 0)
    for g, w in enumerate(POOL_WINDOWS):
        lo, hi = g * HEAD_DIM, (g + 1) * HEAD_DIM
        pg = p[:, lo:hi]
        acc = jnp.concatenate([pcarry_ref[:, lo:hi], pg], axis=0)
        shift = 1
        while shift < w:
            acc = acc + pltpu.roll(acc, shift, 0)
            shift *= 2
        head = acc[CARRY_ROWS:2 * CARRY_ROWS, :] / jnp.minimum(head_pos + 1, w).astype(jnp.float32)
        mean = jnp.concatenate([head, acc[2 * CARRY_ROWS:, :] * (1.0 / w)], axis=0)
        pooled = mean - pg
        mix_ref[:, A_WIDTH + lo:A_WIDTH + hi] = _pool_project(pooled, g, bw_ref, bscale_ref).astype(jnp.bfloat16)
    tail = p[TM - CARRY_ROWS:, :]
    pcarry_ref[...] = tail
    pstate_ref[0] = tail


def _sample_mixers(u, v, p, aws_ref, abs_ref, st_ref, bw_ref, bscale_ref, mix_ref, pp_ref, vp_ref):
    n_pos = TM // SEQ_BLK
    for i in range(n_pos):
        vp_ref[0, i] = v[i * SEQ_BLK:(i + 1) * SEQ_BLK, :]
        pp_ref[0, i] = p[i * SEQ_BLK:(i + 1) * SEQ_BLK, :]

    for hd in range(N_HEADS):
        lo, hi = hd * HEAD_DIM, (hd + 1) * HEAD_DIM
        vplanes = [v[s * SEQ_BLK:(s + 1) * SEQ_BLK, lo:hi] for s in range(n_pos)]
        for i in range(n_pos):
            z = vplanes[0] * aws_ref[hd * 64 + i * 8]
            for s in range(1, i + 1):
                z = z + vplanes[s] * aws_ref[hd * 64 + i * 8 + s]
            z = z + abs_ref[hd * 8 + i]
            r0, r1 = i * SEQ_BLK, (i + 1) * SEQ_BLK
            mix_ref[r0:r1, lo:hi] = (u[r0:r1, lo:hi] * z).astype(jnp.bfloat16)

    for g, w in enumerate(POOL_WINDOWS):
        lo, hi = g * HEAD_DIM, (g + 1) * HEAD_DIM
        planes = [st_ref[0, k, :, lo:hi] for k in range(POOL_STATE)]
        planes += [p[i * SEQ_BLK:(i + 1) * SEQ_BLK, lo:hi] for i in range(n_pos)]
        pooled = []
        for i in range(n_pos):
            top = POOL_STATE + i
            s = planes[top - w + 1]
            for k in range(top - w + 2, top + 1):
                s = s + planes[k]
            pooled.append(s * (1.0 / w) - planes[top])
        pooled = jnp.concatenate(pooled, axis=0)
        mix_ref[:, A_WIDTH + lo:A_WIDTH + hi] = _pool_project(pooled, g, bw_ref, bscale_ref).astype(jnp.bfloat16)


def _route(h2, rwt_ref, rb_ref, su_ref, cnt_ref, meta_ref, gcol_ref):
    tm = h2.shape[0]
    h_hi = h2.astype(jnp.bfloat16)
    h_lo = (h2 - h_hi.astype(jnp.float32)).astype(jnp.bfloat16)
    s = (jnp.dot(h_hi, rwt_ref[...], preferred_element_type=jnp.float32)
         + jnp.dot(h_lo, rwt_ref[...], preferred_element_type=jnp.float32))
    st = s.T
    lt = st[0:N_ROUTER_ROWS, :] + st[N_ROUTER_ROWS:2 * N_ROUTER_ROWS, :] + rb_ref[...]
    row = lambda i: lt[i:i + 1, :]
    l1 = [row(i) for i in range(N_GROUPS)]
    m1 = jnp.maximum(jnp.maximum(l1[0], l1[1]), jnp.maximum(l1[2], l1[3]))
    grp = jnp.where(l1[0] == m1, 0, jnp.where(l1[1] == m1, 1, jnp.where(l1[2] == m1, 2, 3)))
    se = (jnp.exp(l1[0] - m1) + jnp.exp(l1[1] - m1)) + (jnp.exp(l1[2] - m1) + jnp.exp(l1[3] - m1))
    pg = 1.0 / se
    l2 = []
    for e in range(EXPERTS_PER_GROUP):
        c = [row(N_GROUPS + g * EXPERTS_PER_GROUP + e) for g in range(N_GROUPS)]
        l2.append(jnp.where(grp == 0, c[0], jnp.where(grp == 1, c[1], jnp.where(grp == 2, c[2], c[3]))))
    v0 = functools.reduce(jnp.maximum, l2)
    i0 = jnp.full_like(grp, EXPERTS_PER_GROUP - 1)
    for e in range(EXPERTS_PER_GROUP - 2, -1, -1):
        i0 = jnp.where(l2[e] == v0, e, i0)
    neg = jnp.float32(-jnp.inf)
    l2m = [jnp.where(i0 == e, neg, l2[e]) for e in range(EXPERTS_PER_GROUP)]
    v1 = functools.reduce(jnp.maximum, l2m)
    i1 = jnp.full_like(grp, EXPERTS_PER_GROUP - 1)
    for e in range(EXPERTS_PER_GROUP - 2, -1, -1):
        i1 = jnp.where((l2m[e] == v1) & (i0 != e), e, i1)
    d = jnp.exp(v1 - v0)
    g0 = pg / (1.0 + d)
    g1 = (pg * d) / (1.0 + d)
    e0 = grp * EXPERTS_PER_GROUP + i0
    e1 = grp * EXPERTS_PER_GROUP + i1

    eiota = lax.broadcasted_iota(jnp.int32, (N_EXPERTS, tm), 0)
    hit0 = eiota == e0
    hit1 = eiota == e1
    onehot = (hit0 | hit1).astype(jnp.bfloat16)
    prefix = jnp.dot(onehot, su_ref[...], preferred_element_type=jnp.float32)
    carry = cnt_ref[...]
    base = prefix + jnp.concatenate([carry] * (tm // LANES), axis=1)
    r0 = jnp.sum(jnp.where(hit0, base, 0.0), axis=0, keepdims=True)
    r1 = jnp.sum(jnp.where(hit1, base, 0.0), axis=0, keepdims=True)
    ones = jnp.ones((tm, LANES), jnp.bfloat16)
    cnt_ref[...] = carry + jnp.dot(onehot, ones, preferred_element_type=jnp.float32)

    meta_ref[0:1, :] = e0
    meta_ref[1:2, :] = e1
    meta_ref[2:3, :] = r0.astype(jnp.int32)
    meta_ref[3:4, :] = r1.astype(jnp.int32)
    meta_ref[4:8, :] = jnp.zeros((4, tm), jnp.int32)

    giota = lax.broadcasted_iota(jnp.int32, (LANES, tm), 0)
    gpad = jnp.where(giota == 0, g0, jnp.where(giota == 1, g1, 0.0))
    gcol_ref[...] = gpad.T


def _mixer_kernel(aws_s_ref, abs_s_ref,
                  xp_ref, xs_ref, st_ref, n1g_ref, win_ref, ang_ref, anb_ref, aws_ref, abias_ref, bw_ref, bscale_ref,
                  wout_ref, n2g_ref, rwt_ref, rb_ref, su_ref,
                  x1_ref, h2_ref, meta_ref, gcol_ref, counts_ref, pstate_ref, pp_ref, vp_ref,
                  mix_ref, pcarry_ref, cnt_ref, *, n_prompt_tiles, seq_tiles, part_tiles):
    i = pl.program_id(0)
    is_prompt = i < n_prompt_tiles
    j = i % seq_tiles

    @pl.when(i % part_tiles == 0)
    def _():
        cnt_ref[...] = jnp.zeros_like(cnt_ref)

    @pl.when(is_prompt & (j == 0))
    def _():
        pcarry_ref[...] = jnp.zeros_like(pcarry_ref)

    x = jnp.where(is_prompt, xp_ref[0], xs_ref[...])
    h = _rmsnorm(x, n1g_ref[...]).astype(jnp.bfloat16)
    proj = jnp.dot(h, win_ref[...], preferred_element_type=jnp.float32)
    uv = _gelu(proj[:, : 2 * A_WIDTH])
    u = uv[:, :A_WIDTH]
    v = _layernorm(uv[:, A_WIDTH:], ang_ref[...], anb_ref[...])
    p = proj[:, 2 * A_WIDTH:]

    @pl.when(is_prompt)
    def _():
        _prompt_mixers(j, u, v, p, aws_ref, abias_ref, bw_ref, bscale_ref, mix_ref, pcarry_ref, pstate_ref)

    @pl.when(jnp.logical_not(is_prompt))
    def _():
        _sample_mixers(u, v, p, aws_s_ref, abs_s_ref, st_ref, bw_ref, bscale_ref, mix_ref, pp_ref, vp_ref)

    x1 = x + jnp.dot(mix_ref[...], wout_ref[...], preferred_element_type=jnp.float32)
    x1_ref[...] = x1
    h2 = _rmsnorm(x1, n2g_ref[...])
    for s in range(ROW_TILES):
        h2_ref[pl.ds(s, TM, stride=ROW_TILES), :] = h2[:, s * LANES:(s + 1) * LANES]
    _route(h2, rwt_ref, rb_ref, su_ref, cnt_ref, meta_ref, gcol_ref)
    counts_ref[0] = cnt_ref[...]


def _tables_kernel(counts_ref, meta_ref, asg_ref, tab_ref, dest_vmem, fill_vmem, *dest_smem, t_total):
    h = pl.program_id(0)
    t_part = meta_ref.shape[1]
    cnt = counts_ref[0]
    padded = jnp.floor((cnt + (TE - 1)) * (1.0 / TE)) * TE
    sub = lax.broadcasted_iota(jnp.int32, cnt.shape, 0)
    pad_end = padded
    shift = 1
    while shift < N_EXPERTS:
        pad_end = pad_end + jnp.where(sub >= shift, pltpu.roll(pad_end, shift, 0), 0.0)
        shift *= 2
    pad_start = pad_end - padded
    base = pad_start.astype(jnp.int32)

    for k in range(TOP_K):
        e = meta_ref[k:k + 1, :]
        d = meta_ref[TOP_K + k:TOP_K + k + 1, :]
        for ex in range(N_EXPERTS):
            row = jnp.concatenate([base[ex:ex + 1, :]] * (t_part // LANES), axis=1)
            d = d + jnp.where(e == ex, row, 0)
        dest_vmem[k] = d
    for k, dsm in enumerate(dest_smem):
        pltpu.sync_copy(dest_vmem.at[k], dsm)

    fill_vmem[...] = jnp.full(fill_vmem.shape, -1, jnp.int32)
    pltpu.sync_copy(fill_vmem, asg_ref)
    unroll = 16
    for k, dsm in enumerate(dest_smem):
        def invert(_, carry, dsm=dsm):
            t, a = carry
            for uu in range(unroll):
                asg_ref[0, 0, dsm[0, t + uu]] = a + uu
            return t + unroll, a + unroll
        lax.fori_loop(0, t_part // unroll, invert, (jnp.int32(0), k * t_total + h * t_part))

    n_valid = jnp.maximum(pad_end[N_EXPERTS - 1:, :] * (1.0 / TE), 1.0)
    tile = jnp.minimum(lax.broadcasted_iota(jnp.int32, (1, LANES), 1).astype(jnp.float32), n_valid - 1.0)
    tile_start = tile * TE
    tile_e = jnp.minimum(jnp.sum((pad_end <= tile_start).astype(jnp.float32), axis=0, keepdims=True),
                         N_EXPERTS - 1.0)
    is_e = sub.astype(jnp.float32) == tile_e
    seg_start = jnp.sum(jnp.where(is_e, pad_start, 0.0), axis=0, keepdims=True)
    seg_cnt = jnp.sum(jnp.where(is_e, cnt, 0.0), axis=0, keepdims=True)
    rows = jnp.clip(seg_cnt - (tile_start - seg_start), 0.0, TE)
    tab_ref[0, 0:1, :] = tile_e.astype(jnp.int32)
    tab_ref[0, 1:2, :] = rows.astype(jnp.int32)
    tab_ref[0, 2:3, :] = n_valid.astype(jnp.int32)
    tab_ref[0, 3:, :] = jnp.zeros((SUBLANES - 3, LANES), jnp.int32)


def _expert_kernel(te_ref, nv_ref, nrows_ref,
                   src_ref, src_next_ref, dst_ref,
                   h2_ref, w1_ref, w3_ref, w2_ref,
                   out_hbm,
                   xt0, xt1, ob0, ob1, w1b, w3b, w2b, ssem):
    i = pl.program_id(1)
    g = pl.program_id(0) * pl.num_programs(1) + i
    nv = nv_ref[pl.program_id(0)]
    slab = lambda r: pl.ds(pl.multiple_of(r * ROW_TILES, ROW_TILES), ROW_TILES)

    def gather(idx_ref, xt):
        for r in range(TE):
            xt[r * ROW_TILES:(r + 1) * ROW_TILES, :] = h2_ref[slab(idx_ref[0, 0, r]), :]

    def wait_scatter(tile, ob, other, sem):
        n = nrows_ref[tile] * ROW_TILES
        pltpu.make_async_copy(other.at[pl.ds(0, n), :], ob.at[pl.ds(0, n), :], sem).wait()

    def tile_body(xt, xt_next, ob, ob_other, sem, sem_other):
        @pl.when(i >= 2)
        def _():
            wait_scatter(g - 2, ob, ob_other, sem)

        gather(src_next_ref, xt_next)
        x = jnp.concatenate([_row_slab(xt, s, TE) for s in range(ROW_TILES)], axis=1).astype(jnp.bfloat16)
        a = jnp.dot(x, w1b[...], preferred_element_type=jnp.float32)
        b = jnp.dot(x, w3b[...], preferred_element_type=jnp.float32)
        h = (a * jax.nn.sigmoid(a)) * b
        o = jnp.dot(h.astype(jnp.bfloat16), w2b[...], preferred_element_type=jnp.float32)
        for s in range(ROW_TILES):
            ob[pl.ds(s, TE, stride=ROW_TILES), :] = o[:, s * LANES:(s + 1) * LANES]

        n = nrows_ref[g]
        n_full = n // DMA_UNROLL
        copy = lambda r: pltpu.make_async_copy(ob.at[slab(r), :], out_hbm.at[dst_ref[0, 0, r]], sem)

        def body(rb, carry):
            for uu in range(DMA_UNROLL):
                copy(rb * DMA_UNROLL + uu).start(priority=uu % 2)
            return carry
        lax.fori_loop(0, n_full, body, 0)

        def tail(r, carry):
            copy(r).start(priority=1)
            return carry
        lax.fori_loop(n_full * DMA_UNROLL, n, tail, 0)

        @pl.when(i == nv - 1)
        def _():
            @pl.when(i >= 1)
            def _():
                wait_scatter(g - 1, ob_other, ob, sem_other)
            wait_scatter(g, ob, ob_other, sem)

    @pl.when(i < nv)
    def _():
        @pl.when((i == 0) | (te_ref[g] != te_ref[jnp.maximum(g - 1, 0)]))
        def _():
            w1b[...] = w1_ref[0].astype(jnp.bfloat16)
            w3b[...] = w3_ref[0].astype(jnp.bfloat16)
            w2b[...] = w2_ref[0].astype(jnp.bfloat16)

        @pl.when(i == 0)
        def _():
            gather(src_ref, xt0)

        @pl.when(i % 2 == 0)
        def _():
            tile_body(xt0, xt1, ob0, ob1, ssem.at[0], ssem.at[1])

        @pl.when(i % 2 == 1)
        def _():
            tile_body(xt1, xt0, ob1, ob0, ssem.at[1], ssem.at[0])


def _combine_kernel(x1_ref, gcol_ref, nfg_ref, o0_ref, o1_ref, yp_ref, ys_ref, *, n_prompt_tiles):
    i = pl.program_id(0)
    o0 = jnp.concatenate([_row_slab(o0_ref, s, TM) for s in range(ROW_TILES)], axis=1)
    o1 = jnp.concatenate([_row_slab(o1_ref, s, TM) for s in range(ROW_TILES)], axis=1)
    g = gcol_ref[...]
    moe = g[:, 0:1] * o0 + g[:, 1:2] * o1
    y = _rmsnorm(x1_ref[...] + moe, nfg_ref[...])

    @pl.when(i < n_prompt_tiles)
    def _():
        yp_ref[...] = y

    @pl.when(i >= n_prompt_tiles)
    def _():
        ys_ref[...] = y


def _const_spec(shape):
    return pl.BlockSpec(shape, lambda *_: (0,) * len(shape))


def kernel(x_prompt, x_sample, state_pool, norm1_g, w_in, a_norm_g, a_norm_b, a_ws, a_bs, b_w, b_scale, w_out,
           norm2_g, r1_w, r1_b, r2_w, r2_b, exp_w1, exp_w3, exp_w2, normf_g):
    f32, bf16, i32 = jnp.float32, jnp.bfloat16, jnp.int32
    n_batch, seq, _ = x_prompt.shape
    dec_batch, dec_seq, _ = x_sample.shape
    assert norm1_g.shape[0] == 1 and seq % TM == 0 and TM % CHUNK == 0
    assert dec_seq * SEQ_BLK == TM and dec_batch % SEQ_BLK == 0 and dec_seq <= CHUNK
    t_prompt = n_batch * seq
    t_sample = dec_batch * dec_seq
    t_total = t_prompt + t_sample
    n_tok_tiles = t_total // TM
    n_prompt_tiles = t_prompt // TM
    n_sample_tiles = t_sample // TM
    seq_tiles = seq // TM
    n_assign = TOP_K * t_total
    plane_rows = t_total * ROW_TILES
    assert n_tok_tiles % N_PARTS == 0
    part_tiles = n_tok_tiles // N_PARTS
    t_part = part_tiles * TM
    n_exp_tiles = -(-(TOP_K * t_part + N_EXPERTS * (TE - 1)) // TE)
    p_rows = n_exp_tiles * TE

    n1g = norm1_g[0][None, :]
    n2g = norm2_g[0][None, :]
    nfg = normf_g[None, :]
    win = w_in[0].astype(bf16)
    wout = w_out[0].astype(bf16)
    ang = a_norm_g[0][None, :]
    anb = a_norm_b[0][None, :]
    bw = b_w[0].astype(bf16)
    bscale = b_scale[0][None, :]
    abias = jnp.repeat(a_bs[0][:, :CHUNK].T, HEAD_DIM, axis=1)
    rw = jnp.concatenate([r1_w[0], r2_w[0].transpose(1, 0, 2).reshape(D_MODEL, N_EXPERTS),
                          jnp.zeros((D_MODEL, N_ROUTER_ROWS - N_GROUPS - N_EXPERTS), f32)], axis=1)
    rw_hi = rw.astype(bf16)
    rw_lo = (rw - rw_hi.astype(f32)).astype(bf16)
    rwt = jnp.concatenate([rw_hi, rw_lo, jnp.zeros((D_MODEL, LANES - 2 * N_ROUTER_ROWS), bf16)], axis=1)
    rbias = jnp.concatenate([r1_b[0], r2_b[0].reshape(-1),
                             jnp.zeros((N_ROUTER_ROWS - N_GROUPS - N_EXPERTS,), f32)])
    rb = jnp.broadcast_to(rbias[:, None], (N_ROUTER_ROWS, TM))
    su = (jnp.arange(TM)[:, None] < jnp.arange(TM)[None, :]).astype(bf16)
    xs_planes = (x_sample.reshape(n_sample_tiles, SEQ_BLK, dec_seq, D_MODEL)
                 .transpose(0, 2, 1, 3).reshape(t_sample, D_MODEL))
    st_planes = state_pool[0].reshape(n_sample_tiles, SEQ_BLK, POOL_STATE, B_WIDTH).transpose(0, 2, 1, 3)
    aws_s = a_ws[0][:, :dec_seq, :dec_seq].reshape(-1)
    abs_s = a_bs[0][:, :dec_seq].reshape(-1)

    cparams = pltpu.CompilerParams(dimension_semantics=("arbitrary",), vmem_limit_bytes=VMEM_LIMIT)

    p_tile = lambda i: jnp.minimum(i, n_prompt_tiles - 1)
    s_tile = lambda i: jnp.maximum(i - n_prompt_tiles, 0)
    plane_shape = jax.ShapeDtypeStruct((n_sample_tiles, dec_seq, SEQ_BLK, A_WIDTH), f32)
    x1, h2, meta, gcol, counts, pstate, pplanes, vplanes = pl.pallas_call(
        functools.partial(_mixer_kernel, n_prompt_tiles=n_prompt_tiles, seq_tiles=seq_tiles, part_tiles=part_tiles),
        grid_spec=pltpu.PrefetchScalarGridSpec(
            num_scalar_prefetch=2,
            grid=(n_tok_tiles,),
            in_specs=[
                pl.BlockSpec((1, TM, D_MODEL), lambda i, *_: (p_tile(i) // seq_tiles, p_tile(i) % seq_tiles, 0)),
                pl.BlockSpec((TM, D_MODEL), lambda i, *_: (s_tile(i), 0)),
                pl.BlockSpec((1, POOL_STATE, SEQ_BLK, B_WIDTH), lambda i, *_: (s_tile(i), 0, 0, 0)),
                _const_spec((1, D_MODEL)), _const_spec((D_MODEL, 3 * A_WIDTH)), _const_spec((1, A_WIDTH)),
                _const_spec((1, A_WIDTH)), _const_spec((N_HEADS, CHUNK, CHUNK)), _const_spec((CHUNK, A_WIDTH)),
                _const_spec((N_GROUPS, HEAD_DIM, HEAD_DIM)), _const_spec((1, B_WIDTH)),
                _const_spec((D_MODEL, D_MODEL)), _const_spec((1, D_MODEL)), _const_spec((D_MODEL, LANES)),
                _const_spec((N_ROUTER_ROWS, TM)), _const_spec((TM, TM)),
            ],
            out_specs=[
                pl.BlockSpec((TM, D_MODEL), lambda i, *_: (i, 0)),
                pl.BlockSpec((TM * ROW_TILES, LANES), lambda i, *_: (i, 0)),
                pl.BlockSpec((SUBLANES, TM), lambda i, *_: (0, i)),
                pl.BlockSpec((TM, LANES), lambda i, *_: (i, 0)),
                pl.BlockSpec((1, N_EXPERTS, LANES), lambda i, *_: (i // part_tiles, 0, 0)),
                pl.BlockSpec((1, CARRY_ROWS, B_WIDTH), lambda i, *_: (p_tile(i) // seq_tiles, 0, 0)),
                pl.BlockSpec((1, dec_seq, SEQ_BLK, A_WIDTH), lambda i, *_: (s_tile(i), 0, 0, 0)),
                pl.BlockSpec((1, dec_seq, SEQ_BLK, A_WIDTH), lambda i, *_: (s_tile(i), 0, 0, 0)),
            ],
            scratch_shapes=[pltpu.VMEM((TM, D_MODEL), bf16), pltpu.VMEM((CARRY_ROWS, B_WIDTH), f32),
                            pltpu.VMEM((N_EXPERTS, LANES), f32)],
        ),
        out_shape=[
            jax.ShapeDtypeStruct((t_total, D_MODEL), f32),
            jax.ShapeDtypeStruct((plane_rows, LANES), f32),
            jax.ShapeDtypeStruct((SUBLANES, t_total), i32),
            jax.ShapeDtypeStruct((t_total, LANES), f32),
            jax.ShapeDtypeStruct((N_PARTS, N_EXPERTS, LANES), f32),
            jax.ShapeDtypeStruct((n_batch, CARRY_ROWS, B_WIDTH), f32),
            plane_shape, plane_shape,
        ],
        compiler_params=cparams,
        name="mixer",
    )(aws_s, abs_s, x_prompt, xs_planes, st_planes, n1g, win, ang, anb, a_ws[0][:, :CHUNK, :CHUNK], abias, bw,
      bscale, wout, n2g, rwt, rb, su)

    assert n_exp_tiles <= LANES and t_part % LANES == 0
    asg, tab = pl.pallas_call(
        functools.partial(_tables_kernel, t_total=t_total),
        grid=(N_PARTS,),
        in_specs=[pl.BlockSpec((1, N_EXPERTS, LANES), lambda h: (h, 0, 0)),
                  pl.BlockSpec((SUBLANES, t_part), lambda h: (0, h))],
        out_specs=[pl.BlockSpec((1, 1, p_rows), lambda h: (h, 0, 0), memory_space=pltpu.SMEM),
                   pl.BlockSpec((1, SUBLANES, LANES), lambda h: (h, 0, 0))],
        scratch_shapes=[pltpu.VMEM((TOP_K, 1, t_part), i32), pltpu.VMEM((1, 1, p_rows), i32)]
        + [pltpu.SMEM((1, t_part), i32)] * TOP_K,
        out_shape=[jax.ShapeDtypeStruct((N_PARTS, 1, p_rows), i32),
                   jax.ShapeDtypeStruct((N_PARTS, SUBLANES, LANES), i32)],
        compiler_params=cparams,
        name="route_tables",
    )(counts, meta)
    tile_e = tab[:, 0, :n_exp_tiles].reshape(-1)
    tile_rows = tab[:, 1, :n_exp_tiles].reshape(-1)
    n_valid = tab[:, 2, 0]
    asg = asg.reshape(-1)
    row_part = jnp.arange(N_PARTS * p_rows, dtype=i32) // p_rows
    row_tok = jnp.where(asg >= t_total, asg - t_total, asg)
    row_src = jnp.where(asg >= 0, row_tok - row_part * t_part, 0).reshape(N_PARTS * n_exp_tiles, 1, TE)
    row_dst = jnp.maximum(asg, 0).reshape(N_PARTS * n_exp_tiles, 1, TE)

    flat = lambda h, i: h * n_exp_tiles + i
    cur_blk = lambda h, i, te, nv, nr: (flat(h, jnp.minimum(i, nv[h] - 1)), 0, 0)
    nxt_blk = lambda h, i, te, nv, nr: (flat(h, jnp.minimum(i + 1, nv[h] - 1)), 0, 0)
    w_blk = lambda h, i, te, nv, nr: (te[flat(h, i)], 0, 0)
    smem_blk = lambda imap: pl.BlockSpec((1, 1, TE), imap, memory_space=pltpu.SMEM)
    row_buf = pltpu.VMEM((TE * ROW_TILES, LANES), f32)
    out_tok = pl.pallas_call(
        _expert_kernel,
        grid_spec=pltpu.PrefetchScalarGridSpec(
            num_scalar_prefetch=3,
            grid=(N_PARTS, n_exp_tiles),
            in_specs=[smem_blk(cur_blk), smem_blk(nxt_blk), smem_blk(cur_blk),
                      pl.BlockSpec((t_part * ROW_TILES, LANES), lambda h, i, *_: (h, 0),
                                   pipeline_mode=pl.Buffered(1)),
                      pl.BlockSpec((1, D_MODEL, D_EXPERT), w_blk),
                      pl.BlockSpec((1, D_MODEL, D_EXPERT), w_blk),
                      pl.BlockSpec((1, D_EXPERT, D_MODEL), w_blk)],
            out_specs=pl.BlockSpec(memory_space=pl.ANY),
            scratch_shapes=[row_buf, row_buf, row_buf, row_buf,
                            pltpu.VMEM((D_MODEL, D_EXPERT), bf16), pltpu.VMEM((D_MODEL, D_EXPERT), bf16),
                            pltpu.VMEM((D_EXPERT, D_MODEL), bf16),
                            pltpu.SemaphoreType.DMA((2,))],
        ),
        out_shape=jax.ShapeDtypeStruct((TOP_K * t_total, ROW_TILES, LANES), f32),
        compiler_params=pltpu.CompilerParams(dimension_semantics=("arbitrary", "arbitrary"),
                                             vmem_limit_bytes=EXPERT_VMEM_LIMIT),
        name="moe_experts",
    )(tile_e, n_valid, tile_rows, row_src, row_src, row_dst, h2, exp_w1[0], exp_w3[0], exp_w2[0])
    out_tok = out_tok.reshape(TOP_K * plane_rows, LANES)

    y_prompt, y_sample = pl.pallas_call(
        functools.partial(_combine_kernel, n_prompt_tiles=n_prompt_tiles),
        grid=(n_tok_tiles,),
        in_specs=[pl.BlockSpec((TM, D_MODEL), lambda i: (i, 0)),
                  pl.BlockSpec((TM, LANES), lambda i: (i, 0)),
                  _const_spec((1, D_MODEL)),
                  pl.BlockSpec((TM * ROW_TILES, LANES), lambda i: (i, 0)),
                  pl.BlockSpec((TM * ROW_TILES, LANES), lambda i: (n_tok_tiles + i, 0))],
        out_specs=[pl.BlockSpec((TM, D_MODEL), lambda i: (p_tile(i), 0)),
                   pl.BlockSpec((TM, D_MODEL), lambda i: (s_tile(i), 0))],
        out_shape=[jax.ShapeDtypeStruct((t_prompt, D_MODEL), f32), jax.ShapeDtypeStruct((t_sample, D_MODEL), f32)],
        compiler_params=cparams,
        name="moe_combine",
    )(x1, gcol, nfg, out_tok, out_tok)

    unplane = lambda a: a.transpose(0, 2, 1, 3).reshape(dec_batch, dec_seq, a.shape[-1])
    y_prompt = y_prompt.reshape(n_batch, seq, D_MODEL)
    y_sample = unplane(y_sample.reshape(n_sample_tiles, dec_seq, SEQ_BLK, D_MODEL))
    pool_state_prompt = pstate[None, :, CARRY_ROWS - POOL_STATE:, :]
    p_s = unplane(pplanes)
    pool_state_sample = jnp.concatenate([state_pool[0], p_s], axis=1)[None, :, -POOL_STATE:, :]
    chunk_v_sample = unplane(vplanes)[None]
    return (y_prompt, y_sample, pool_state_prompt, pool_state_sample, chunk_v_sample)
```

```python
import functools
import math

import jax
import jax.numpy as jnp
from jax import lax
from jax.experimental import pallas as pl
from jax.experimental.pallas import tpu as pltpu

D_MODEL = 1024
A_WIDTH = 512
B_WIDTH = 512
N_HEADS = 4
HEAD_DIM = 128
CHUNK = 128
POOL_WINDOWS = (2, 4, 8, 16)
POOL_STATE = 15
N_GROUPS = 4
EXPERTS_PER_GROUP = 8
N_EXPERTS = 32
TOP_K = 2
D_EXPERT = 512
EPS = 1e-6

SUBLANES = 8
LANES = 128
ROW_TILES = D_MODEL // LANES

TM = 256
TE = 512
SEQ_BLK = 32
N_ROUTER_ROWS = 40
CARRY_ROWS = 16
DMA_UNROLL = 32
VMEM_LIMIT = 48 * 1024 * 1024
N_PARTS = 2
EXPERT_VMEM_LIMIT = 62 * 1024 * 1024

_INV_SQRT2 = 1.0 / math.sqrt(2.0)


def _rmsnorm(x, g):
    r = lax.rsqrt(jnp.mean(x * x, axis=-1, keepdims=True) + EPS)
    return (x * r) * g


def _gelu(x):
    return 0.5 * x * (1.0 + lax.erf(x * _INV_SQRT2))


def _layernorm(x, g, b):
    mu = jnp.mean(x, axis=-1, keepdims=True)
    xc = x - mu
    var = jnp.mean(xc * xc, axis=-1, keepdims=True)
    return (xc * lax.rsqrt(var + EPS)) * g + b


def _row_slab(ref, s, n):
    return ref[pl.ds(s, n, stride=ROW_TILES), :]


def _pool_project(pooled, g, bw_ref, bscale_ref):
    lo, hi = g * HEAD_DIM, (g + 1) * HEAD_DIM
    hb = jnp.dot(pooled.astype(jnp.bfloat16), bw_ref[g], preferred_element_type=jnp.float32)
    return hb * bscale_ref[:, lo:hi]


def _prompt_mixers(j, u, v, p, aws_ref, abias_ref, bw_ref, bscale_ref, mix_ref, pcarry_ref, pstate_ref):
    tri = (lax.broadcasted_iota(jnp.int32, (CHUNK, CHUNK), 0)
           >= lax.broadcasted_iota(jnp.int32, (CHUNK, CHUNK), 1))
    vb = v.astype(jnp.bfloat16)
    for hd in range(N_HEADS):
        lo, hi = hd * HEAD_DIM, (hd + 1) * HEAD_DIM
        w = jnp.where(tri, aws_ref[hd], 0.0).astype(jnp.bfloat16)
        for c in range(TM // CHUNK):
            r0, r1 = c * CHUNK, (c + 1) * CHUNK
            z = jnp.dot(w, vb[r0:r1, lo:hi], preferred_element_type=jnp.float32) + abias_ref[:, lo:hi]
            mix_ref[r0:r1, lo:hi] = (u[r0:r1, lo:hi] * z).astype(jnp.bfloat16)

    head_pos = j * TM + lax.broadcasted_iota(jnp.int32, (CARRY_ROWS, LANES), 0)
    for g, w in enumerate(POOL_WINDOWS):
        lo, hi = g * HEAD_DIM, (g + 1) * HEAD_DIM
        pg = p[:, lo:hi]
        acc = jnp.concatenate([pcarry_ref[:, lo:hi], pg], axis=0)
        shift = 1
        while shift < w:
            acc = acc + pltpu.roll(acc, shift, 0)
            shift *= 2
        head = acc[CARRY_ROWS:2 * CARRY_ROWS, :] / jnp.minimum(head_pos + 1, w).astype(jnp.float32)
        mean = jnp.concatenate([head, acc[2 * CARRY_ROWS:, :] * (1.0 / w)], axis=0)
        pooled = mean - pg
        mix_ref[:, A_WIDTH + lo:A_WIDTH + hi] = _pool_project(pooled, g, bw_ref, bscale_ref).astype(jnp.bfloat16)
    tail = p[TM - CARRY_ROWS:, :]
    pcarry_ref[...] = tail
    pstate_ref[0] = tail


def _sample_mixers(u, v, p, aws_ref, abs_ref, st_ref, bw_ref, bscale_ref, mix_ref, pp_ref, vp_ref):
    n_pos = TM // SEQ_BLK
    for i in range(n_pos):
        vp_ref[0, i] = v[i * SEQ_BLK:(i + 1) * SEQ_BLK, :]
        pp_ref[0, i] = p[i * SEQ_BLK:(i + 1) * SEQ_BLK, :]

    for hd in range(N_HEADS):
        lo, hi = hd * HEAD_DIM, (hd + 1) * HEAD_DIM
        vplanes = [v[s * SEQ_BLK:(s + 1) * SEQ_BLK, lo:hi] for s in range(n_pos)]
        for i in range(n_pos):
            z = vplanes[0] * aws_ref[hd * 64 + i * 8]
            for s in range(1, i + 1):
                z = z + vplanes[s] * aws_ref[hd * 64 + i * 8 + s]
            z = z + abs_ref[hd * 8 + i]
            r0, r1 = i * SEQ_BLK, (i + 1) * SEQ_BLK
            mix_ref[r0:r1, lo:hi] = (u[r0:r1, lo:hi] * z).astype(jnp.bfloat16)

    for g, w in enumerate(POOL_WINDOWS):
        lo, hi = g * HEAD_DIM, (g + 1) * HEAD_DIM
        planes = [st_ref[0, k, :, lo:hi] for k in range(POOL_STATE)]
        planes += [p[i * SEQ_BLK:(i + 1) * SEQ_BLK, lo:hi] for i in range(n_pos)]
        pooled = []
        for i in range(n_pos):
            top = POOL_STATE + i
            s = planes[top - w + 1]
            for k in range(top - w + 2, top + 1):
                s = s + planes[k]
            pooled.append(s * (1.0 / w) - planes[top])
        pooled = jnp.concatenate(pooled, axis=0)
        mix_ref[:, A_WIDTH + lo:A_WIDTH + hi] = _pool_project(pooled, g, bw_ref, bscale_ref).astype(jnp.bfloat16)


def _route(h2, rwt_ref, rb_ref, su_ref, cnt_ref, meta_ref, gcol_ref):
    tm = h2.shape[0]
    h_hi = h2.astype(jnp.bfloat16)
    h_lo = (h2 - h_hi.astype(jnp.float32)).astype(jnp.bfloat16)
    s = (jnp.dot(h_hi, rwt_ref[...], preferred_element_type=jnp.float32)
         + jnp.dot(h_lo, rwt_ref[...], preferred_element_type=jnp.float32))
    st = s.T
    lt = st[0:N_ROUTER_ROWS, :] + st[N_ROUTER_ROWS:2 * N_ROUTER_ROWS, :] + rb_ref[...]
    row = lambda i: lt[i:i + 1, :]
    l1 = [row(i) for i in range(N_GROUPS)]
    m1 = jnp.maximum(jnp.maximum(l1[0], l1[1]), jnp.maximum(l1[2], l1[3]))
    grp = jnp.where(l1[0] == m1, 0, jnp.where(l1[1] == m1, 1, jnp.where(l1[2] == m1, 2, 3)))
    se = (jnp.exp(l1[0] - m1) + jnp.exp(l1[1] - m1)) + (jnp.exp(l1[2] - m1) + jnp.exp(l1[3] - m1))
    pg = 1.0 / se
    l2 = []
    for e in range(EXPERTS_PER_GROUP):
        c = [row(N_GROUPS + g * EXPERTS_PER_GROUP + e) for g in range(N_GROUPS)]
        l2.append(jnp.where(grp == 0, c[0], jnp.where(grp == 1, c[1], jnp.where(grp == 2, c[2], c[3]))))
    v0 = functools.reduce(jnp.maximum, l2)
    i0 = jnp.full_like(grp, EXPERTS_PER_GROUP - 1)
    for e in range(EXPERTS_PER_GROUP - 2, -1, -1):
        i0 = jnp.where(l2[e] == v0, e, i0)
    neg = jnp.float32(-jnp.inf)
    l2m = [jnp.where(i0 == e, neg, l2[e]) for e in range(EXPERTS_PER_GROUP)]
    v1 = functools.reduce(jnp.maximum, l2m)
    i1 = jnp.full_like(grp, EXPERTS_PER_GROUP - 1)
    for e in range(EXPERTS_PER_GROUP - 2, -1, -1):
        i1 = jnp.where((l2m[e] == v1) & (i0 != e), e, i1)
    d = jnp.exp(v1 - v0)
    g0 = pg / (1.0 + d)
    g1 = (pg * d) / (1.0 + d)
    e0 = grp * EXPERTS_PER_GROUP + i0
    e1 = grp * EXPERTS_PER_GROUP + i1

    eiota = lax.broadcasted_iota(jnp.int32, (N_EXPERTS, tm), 0)
    hit0 = eiota == e0
    hit1 = eiota == e1
    onehot = (hit0 | hit1).astype(jnp.bfloat16)
    prefix = jnp.dot(onehot, su_ref[...], preferred_element_type=jnp.float32)
    carry = cnt_ref[...]
    base = prefix + jnp.concatenate([carry] * (tm // LANES), axis=1)
    r0 = jnp.sum(jnp.where(hit0, base, 0.0), axis=0, keepdims=True)
    r1 = jnp.sum(jnp.where(hit1, base, 0.0), axis=0, keepdims=True)
    ones = jnp.ones((tm, LANES), jnp.bfloat16)
    cnt_ref[...] = carry + jnp.dot(onehot, ones, preferred_element_type=jnp.float32)

    meta_ref[0:1, :] = e0
    meta_ref[1:2, :] = e1
    meta_ref[2:3, :] = r0.astype(jnp.int32)
    meta_ref[3:4, :] = r1.astype(jnp.int32)
    meta_ref[4:8, :] = jnp.zeros((4, tm), jnp.int32)

    giota = lax.broadcasted_iota(jnp.int32, (LANES, tm), 0)
    gpad = jnp.where(giota == 0, g0, jnp.where(giota == 1, g1, 0.0))
    gcol_ref[...] = gpad.T


def _mixer_kernel(aws_s_ref, abs_s_ref,
                  xp_ref, xs_ref, st_ref, n1g_ref, win_ref, ang_ref, anb_ref, aws_ref, abias_ref, bw_ref, bscale_ref,
                  wout_ref, n2g_ref, rwt_ref, rb_ref, su_ref,
                  x1_ref, h2_ref, meta_ref, gcol_ref, counts_ref, pstate_ref, pp_ref, vp_ref,
                  mix_ref, pcarry_ref, cnt_ref, *, n_prompt_tiles, seq_tiles, part_tiles):
    i = pl.program_id(0)
    is_prompt = i < n_prompt_tiles
    j = i % seq_tiles

    @pl.when(i % part_tiles == 0)
    def _():
        cnt_ref[...] = jnp.zeros_like(cnt_ref)

    @pl.when(is_prompt & (j == 0))
    def _():
        pcarry_ref[...] = jnp.zeros_like(pcarry_ref)

    x = jnp.where(is_prompt, xp_ref[0], xs_ref[...])
    h = _rmsnorm(x, n1g_ref[...]).astype(jnp.bfloat16)
    proj = jnp.dot(h, win_ref[...], preferred_element_type=jnp.float32)
    uv = _gelu(proj[:, : 2 * A_WIDTH])
    u = uv[:, :A_WIDTH]
    v = _layernorm(uv[:, A_WIDTH:], ang_ref[...], anb_ref[...])
    p = proj[:, 2 * A_WIDTH:]

    @pl.when(is_prompt)
    def _():
        _prompt_mixers(j, u, v, p, aws_ref, abias_ref, bw_ref, bscale_ref, mix_ref, pcarry_ref, pstate_ref)

    @pl.when(jnp.logical_not(is_prompt))
    def _():
        _sample_mixers(u, v, p, aws_s_ref, abs_s_ref, st_ref, bw_ref, bscale_ref, mix_ref, pp_ref, vp_ref)

    x1 = x + jnp.dot(mix_ref[...], wout_ref[...], preferred_element_type=jnp.float32)
    x1_ref[...] = x1
    h2 = _rmsnorm(x1, n2g_ref[...])
    for s in range(ROW_TILES):
        h2_ref[pl.ds(s, TM, stride=ROW_TILES), :] = h2[:, s * LANES:(s + 1) * LANES]
    _route(h2, rwt_ref, rb_ref, su_ref, cnt_ref, meta_ref, gcol_ref)
    counts_ref[0] = cnt_ref[...]


def _tables_kernel(counts_ref, meta_ref, asg_ref, tab_ref, dest_vmem, fill_vmem, *dest_smem, t_total):
    h = pl.program_id(0)
    t_part = meta_ref.shape[1]
    cnt = counts_ref[0]
    padded = jnp.floor((cnt + (TE - 1)) * (1.0 / TE)) * TE
    sub = lax.broadcasted_iota(jnp.int32, cnt.shape, 0)
    pad_end = padded
    shift = 1
    while shift < N_EXPERTS:
        pad_end = pad_end + jnp.where(sub >= shift, pltpu.roll(pad_end, shift, 0), 0.0)
        shift *= 2
    pad_start = pad_end - padded
    base = pad_start.astype(jnp.int32)

    for k in range(TOP_K):
        e = meta_ref[k:k + 1, :]
        d = meta_ref[TOP_K + k:TOP_K + k + 1, :]
        for ex in range(N_EXPERTS):
            row = jnp.concatenate([base[ex:ex + 1, :]] * (t_part // LANES), axis=1)
            d = d + jnp.where(e == ex, row, 0)
        dest_vmem[k] = d
    for k, dsm in enumerate(dest_smem):
        pltpu.sync_copy(dest_vmem.at[k], dsm)

    fill_vmem[...] = jnp.full(fill_vmem.shape, -1, jnp.int32)
    pltpu.sync_copy(fill_vmem, asg_ref)
    unroll = 16
    for k, dsm in enumerate(dest_smem):
        def invert(_, carry, dsm=dsm):
            t, a = carry
            for uu in range(unroll):
                asg_ref[0, 0, dsm[0, t + uu]] = a + uu
            return t + unroll, a + unroll
        lax.fori_loop(0, t_part // unroll, invert, (jnp.int32(0), k * t_total + h * t_part))

    n_valid = jnp.maximum(pad_end[N_EXPERTS - 1:, :] * (1.0 / TE), 1.0)
    tile = jnp.minimum(lax.broadcasted_iota(jnp.int32, (1, LANES), 1).astype(jnp.float32), n_valid - 1.0)
    tile_start = tile * TE
    tile_e = jnp.minimum(jnp.sum((pad_end <= tile_start).astype(jnp.float32), axis=0, keepdims=True),
                         N_EXPERTS - 1.0)
    is_e = sub.astype(jnp.float32) == tile_e
    seg_start = jnp.sum(jnp.where(is_e, pad_start, 0.0), axis=0, keepdims=True)
    seg_cnt = jnp.sum(jnp.where(is_e, cnt, 0.0), axis=0, keepdims=True)
    rows = jnp.clip(seg_cnt - (tile_start - seg_start), 0.0, TE)
    tab_ref[0, 0:1, :] = tile_e.astype(jnp.int32)
    tab_ref[0, 1:2, :] = rows.astype(jnp.int32)
    tab_ref[0, 2:3, :] = n_valid.astype(jnp.int32)
    tab_ref[0, 3:, :] = jnp.zeros((SUBLANES - 3, LANES), jnp.int32)


def _expert_kernel(te_ref, nv_ref, nrows_ref,
                   src_ref, src_next_ref, dst_ref,
                   h2_ref, w1_ref, w3_ref, w2_ref,
                   out_hbm,
                   xt0, xt1, ob0, ob1, w1b, w3b, w2b, ssem):
    i = pl.program_id(1)
    g = pl.program_id(0) * pl.num_programs(1) + i
    nv = nv_ref[pl.program_id(0)]
    slab = lambda r: pl.ds(pl.multiple_of(r * ROW_TILES, ROW_TILES), ROW_TILES)

    def gather(idx_ref, xt):
        for r in range(TE):
            xt[r * ROW_TILES:(r + 1) * ROW_TILES, :] = h2_ref[slab(idx_ref[0, 0, r]), :]

    def wait_scatter(tile, ob, other, sem):
        n = nrows_ref[tile] * ROW_TILES
        pltpu.make_async_copy(other.at[pl.ds(0, n), :], ob.at[pl.ds(0, n), :], sem).wait()

    def tile_body(xt, xt_next, ob, ob_other, sem, sem_other):
        @pl.when(i >= 2)
        def _():
            wait_scatter(g - 2, ob, ob_other, sem)

        gather(src_next_ref, xt_next)
        x = jnp.concatenate([_row_slab(xt, s, TE) for s in range(ROW_TILES)], axis=1).astype(jnp.bfloat16)
        a = jnp.dot(x, w1b[...], preferred_element_type=jnp.float32)
        b = jnp.dot(x, w3b[...], preferred_element_type=jnp.float32)
        h = (a * jax.nn.sigmoid(a)) * b
        o = jnp.dot(h.astype(jnp.bfloat16), w2b[...], preferred_element_type=jnp.float32)
        for s in range(ROW_TILES):
            ob[pl.ds(s, TE, stride=ROW_TILES), :] = o[:, s * LANES:(s + 1) * LANES]

        n = nrows_ref[g]
        n_full = n // DMA_UNROLL
        copy = lambda r: pltpu.make_async_copy(ob.at[slab(r), :], out_hbm.at[dst_ref[0, 0, r]], sem)

        def body(rb, carry):
            for uu in range(DMA_UNROLL):
                copy(rb * DMA_UNROLL + uu).start(priority=uu % 2)
            return carry
        lax.fori_loop(0, n_full, body, 0)

        def tail(r, carry):
            copy(r).start(priority=1)
            return carry
        lax.fori_loop(n_full * DMA_UNROLL, n, tail, 0)

        @pl.when(i == nv - 1)
        def _():
            @pl.when(i >= 1)
            def _():
                wait_scatter(g - 1, ob_other, ob, sem_other)
            wait_scatter(g, ob, ob_other, sem)

    @pl.when(i < nv)
    def _():
        @pl.when((i == 0) | (te_ref[g] != te_ref[jnp.maximum(g - 1, 0)]))
        def _():
            w1b[...] = w1_ref[0].astype(jnp.bfloat16)
            w3b[...] = w3_ref[0].astype(jnp.bfloat16)
            w2b[...] = w2_ref[0].astype(jnp.bfloat16)

        @pl.when(i == 0)
        def _():
            gather(src_ref, xt0)

        @pl.when(i % 2 == 0)
        def _():
            tile_body(xt0, xt1, ob0, ob1, ssem.at[0], ssem.at[1])

        @pl.when(i % 2 == 1)
        def _():
            tile_body(xt1, xt0, ob1, ob0, ssem.at[1], ssem.at[0])


def _combine_kernel(x1_ref, gcol_ref, nfg_ref, o0_ref, o1_ref, yp_ref, ys_ref, *, n_prompt_tiles):
    i = pl.program_id(0)
    o0 = jnp.concatenate([_row_slab(o0_ref, s, TM) for s in range(ROW_TILES)], axis=1)
    o1 = jnp.concatenate([_row_slab(o1_ref, s, TM) for s in range(ROW_TILES)], axis=1)
    g = gcol_ref[...]
    moe = g[:, 0:1] * o0 + g[:, 1:2] * o1
    y = _rmsnorm(x1_ref[...] + moe, nfg_ref[...])

    @pl.when(i < n_prompt_tiles)
    def _():
        yp_ref[...] = y

    @pl.when(i >= n_prompt_tiles)
    def _():
        ys_ref[...] = y


def _const_spec(shape):
    return pl.BlockSpec(shape, lambda *_: (0,) * len(shape))


def kernel(x_prompt, x_sample, state_pool, norm1_g, w_in, a_norm_g, a_norm_b, a_ws, a_bs, b_w, b_scale, w_out,
           norm2_g, r1_w, r1_b, r2_w, r2_b, exp_w1, exp_w3, exp_w2, normf_g):
    f32, bf16, i32 = jnp.float32, jnp.bfloat16, jnp.int32
    n_batch, seq, _ = x_prompt.shape
    dec_batch, dec_seq, _ = x_sample.shape
    assert norm1_g.shape[0] == 1 and seq % TM == 0 and TM % CHUNK == 0
    assert dec_seq * SEQ_BLK == TM and dec_batch % SEQ_BLK == 0 and dec_seq <= CHUNK
    t_prompt = n_batch * seq
    t_sample = dec_batch * dec_seq
    t_total = t_prompt + t_sample
    n_tok_tiles = t_total // TM
    n_prompt_tiles = t_prompt // TM
    n_sample_tiles = t_sample // TM
    seq_tiles = seq // TM
    n_assign = TOP_K * t_total
    plane_rows = t_total * ROW_TILES
    assert n_tok_tiles % N_PARTS == 0
    part_tiles = n_tok_tiles // N_PARTS
    t_part = part_tiles * TM
    n_exp_tiles = -(-(TOP_K * t_part + N_EXPERTS * (TE - 1)) // TE)
    p_rows = n_exp_tiles * TE

    n1g = norm1_g[0][None, :]
    n2g = norm2_g[0][None, :]
    nfg = normf_g[None, :]
    win = w_in[0].astype(bf16)
    wout = w_out[0].astype(bf16)
    ang = a_norm_g[0][None, :]
    anb = a_norm_b[0][None, :]
    bw = b_w[0].astype(bf16)
    bscale = b_scale[0][None, :]
    abias = jnp.repeat(a_bs[0][:, :CHUNK].T, HEAD_DIM, axis=1)
    rw = jnp.concatenate([r1_w[0], r2_w[0].transpose(1, 0, 2).reshape(D_MODEL, N_EXPERTS),
                          jnp.zeros((D_MODEL, N_ROUTER_ROWS - N_GROUPS - N_EXPERTS), f32)], axis=1)
    rw_hi = rw.astype(bf16)
    rw_lo = (rw - rw_hi.astype(f32)).astype(bf16)
    rwt = jnp.concatenate([rw_hi, rw_lo, jnp.zeros((D_MODEL, LANES - 2 * N_ROUTER_ROWS), bf16)], axis=1)
    rbias = jnp.concatenate([r1_b[0], r2_b[0].reshape(-1),
                             jnp.zeros((N_ROUTER_ROWS - N_GROUPS - N_EXPERTS,), f32)])
    rb = jnp.broadcast_to(rbias[:, None], (N_ROUTER_ROWS, TM))
    su = (jnp.arange(TM)[:, None] < jnp.arange(TM)[None, :]).astype(bf16)
    xs_planes = (x_sample.reshape(n_sample_tiles, SEQ_BLK, dec_seq, D_MODEL)
                 .transpose(0, 2, 1, 3).reshape(t_sample, D_MODEL))
    st_planes = state_pool[0].reshape(n_sample_tiles, SEQ_BLK, POOL_STATE, B_WIDTH).transpose(0, 2, 1, 3)
    aws_s = a_ws[0][:, :dec_seq, :dec_seq].reshape(-1)
    abs_s = a_bs[0][:, :dec_seq].reshape(-1)

    cparams = pltpu.CompilerParams(dimension_semantics=("arbitrary",), vmem_limit_bytes=VMEM_LIMIT)

    p_tile = lambda i: jnp.minimum(i, n_prompt_tiles - 1)
    s_tile = lambda i: jnp.maximum(i - n_prompt_tiles, 0)
    plane_shape = jax.ShapeDtypeStruct((n_sample_tiles, dec_seq, SEQ_BLK, A_WIDTH), f32)
    x1, h2, meta, gcol, counts, pstate, pplanes, vplanes = pl.pallas_call(
        functools.partial(_mixer_kernel, n_prompt_tiles=n_prompt_tiles, seq_tiles=seq_tiles, part_tiles=part_tiles),
        grid_spec=pltpu.PrefetchScalarGridSpec(
            num_scalar_prefetch=2,
            grid=(n_tok_tiles,),
            in_specs=[
                pl.BlockSpec((1, TM, D_MODEL), lambda i, *_: (p_tile(i) // seq_tiles, p_tile(i) % seq_tiles, 0)),
                pl.BlockSpec((TM, D_MODEL), lambda i, *_: (s_tile(i), 0)),
                pl.BlockSpec((1, POOL_STATE, SEQ_BLK, B_WIDTH), lambda i, *_: (s_tile(i), 0, 0, 0)),
                _const_spec((1, D_MODEL)), _const_spec((D_MODEL, 3 * A_WIDTH)), _const_spec((1, A_WIDTH)),
                _const_spec((1, A_WIDTH)), _const_spec((N_HEADS, CHUNK, CHUNK)), _const_spec((CHUNK, A_WIDTH)),
                _const_spec((N_GROUPS, HEAD_DIM, HEAD_DIM)), _const_spec((1, B_WIDTH)),
                _const_spec((D_MODEL, D_MODEL)), _const_spec((1, D_MODEL)), _const_spec((D_MODEL, LANES)),
                _const_spec((N_ROUTER_ROWS, TM)), _const_spec((TM, TM)),
            ],
            out_specs=[
                pl.BlockSpec((TM, D_MODEL), lambda i, *_: (i, 0)),
                pl.BlockSpec((TM * ROW_TILES, LANES), lambda i, *_: (i, 0)),
                pl.BlockSpec((SUBLANES, TM), lambda i, *_: (0, i)),
                pl.BlockSpec((TM, LANES), lambda i, *_: (i, 0)),
                pl.BlockSpec((1, N_EXPERTS, LANES), lambda i, *_: (i // part_tiles, 0, 0)),
                pl.BlockSpec((1, CARRY_ROWS, B_WIDTH), lambda i, *_: (p_tile(i) // seq_tiles, 0, 0)),
                pl.BlockSpec((1, dec_seq, SEQ_BLK, A_WIDTH), lambda i, *_: (s_tile(i), 0, 0, 0)),
                pl.BlockSpec((1, dec_seq, SEQ_BLK, A_WIDTH), lambda i, *_: (s_tile(i), 0, 0, 0)),
            ],
            scratch_shapes=[pltpu.VMEM((TM, D_MODEL), bf16), pltpu.VMEM((CARRY_ROWS, B_WIDTH), f32),
                            pltpu.VMEM((N_EXPERTS, LANES), f32)],
        ),
        out_shape=[
            jax.ShapeDtypeStruct((t_total, D_MODEL), f32),
            jax.ShapeDtypeStruct((plane_rows, LANES), f32),
            jax.ShapeDtypeStruct((SUBLANES, t_total), i32),
            jax.ShapeDtypeStruct((t_total, LANES), f32),
            jax.ShapeDtypeStruct((N_PARTS, N_EXPERTS, LANES), f32),
            jax.ShapeDtypeStruct((n_batch, CARRY_ROWS, B_WIDTH), f32),
            plane_shape, plane_shape,
        ],
        compiler_params=cparams,
        name="mixer",
    )(aws_s, abs_s, x_prompt, xs_planes, st_planes, n1g, win, ang, anb, a_ws[0][:, :CHUNK, :CHUNK], abias, bw,
      bscale, wout, n2g, rwt, rb, su)

    assert n_exp_tiles <= LANES and t_part % LANES == 0
    asg, tab = pl.pallas_call(
        functools.partial(_tables_kernel, t_total=t_total),
        grid=(N_PARTS,),
        in_specs=[pl.BlockSpec((1, N_EXPERTS, LANES), lambda h: (h, 0, 0)),
                  pl.BlockSpec((SUBLANES, t_part), lambda h: (0, h))],
        out_specs=[pl.BlockSpec((1, 1, p_rows), lambda h: (h, 0, 0), memory_space=pltpu.SMEM),
                   pl.BlockSpec((1, SUBLANES, LANES), lambda h: (h, 0, 0))],
        scratch_shapes=[pltpu.VMEM((TOP_K, 1, t_part), i32), pltpu.VMEM((1, 1, p_rows), i32)]
        + [pltpu.SMEM((1, t_part), i32)] * TOP_K,
        out_shape=[jax.ShapeDtypeStruct((N_PARTS, 1, p_rows), i32),
                   jax.ShapeDtypeStruct((N_PARTS, SUBLANES, LANES), i32)],
        compiler_params=cparams,
        name="route_tables",
    )(counts, meta)
    tile_e = tab[:, 0, :n_exp_tiles].reshape(-1)
    tile_rows = tab[:, 1, :n_exp_tiles].reshape(-1)
    n_valid = tab[:, 2, 0]
    asg = asg.reshape(-1)
    row_part = jnp.arange(N_PARTS * p_rows, dtype=i32) // p_rows
    row_tok = jnp.where(asg >= t_total, asg - t_total, asg)
    row_src = jnp.where(asg >= 0, row_tok - row_part * t_part, 0).reshape(N_PARTS * n_exp_tiles, 1, TE)
    row_dst = jnp.maximum(asg, 0).reshape(N_PARTS * n_exp_tiles, 1, TE)

    flat = lambda h, i: h * n_exp_tiles + i
    cur_blk = lambda h, i, te, nv, nr: (flat(h, jnp.minimum(i, nv[h] - 1)), 0, 0)
    nxt_blk = lambda h, i, te, nv, nr: (flat(h, jnp.minimum(i + 1, nv[h] - 1)), 0, 0)
    w_blk = lambda h, i, te, nv, nr: (te[flat(h, i)], 0, 0)
    smem_blk = lambda imap: pl.BlockSpec((1, 1, TE), imap, memory_space=pltpu.SMEM)
    row_buf = pltpu.VMEM((TE * ROW_TILES, LANES), f32)
    out_tok = pl.pallas_call(
        _expert_kernel,
        grid_spec=pltpu.PrefetchScalarGridSpec(
            num_scalar_prefetch=3,
            grid=(N_PARTS, n_exp_tiles),
            in_specs=[smem_blk(cur_blk), smem_blk(nxt_blk), smem_blk(cur_blk),
                      pl.BlockSpec((t_part * ROW_TILES, LANES), lambda h, i, *_: (h, 0),
                                   pipeline_mode=pl.Buffered(1)),
                      pl.BlockSpec((1, D_MODEL, D_EXPERT), w_blk),
                      pl.BlockSpec((1, D_MODEL, D_EXPERT), w_blk),
                      pl.BlockSpec((1, D_EXPERT, D_MODEL), w_blk)],
            out_specs=pl.BlockSpec(memory_space=pl.ANY),
            scratch_shapes=[row_buf, row_buf, row_buf, row_buf,
                            pltpu.VMEM((D_MODEL, D_EXPERT), bf16), pltpu.VMEM((D_MODEL, D_EXPERT), bf16),
                            pltpu.VMEM((D_EXPERT, D_MODEL), bf16),
                            pltpu.SemaphoreType.DMA((2,))],
        ),
        out_shape=jax.ShapeDtypeStruct((TOP_K * t_total, ROW_TILES, LANES), f32),
        compiler_params=pltpu.CompilerParams(dimension_semantics=("arbitrary", "arbitrary"),
                                             vmem_limit_bytes=EXPERT_VMEM_LIMIT),
        name="moe_experts",
    )(tile_e, n_valid, tile_rows, row_src, row_src, row_dst, h2, exp_w1[0], exp_w3[0], exp_w2[0])
    out_tok = out_tok.reshape(TOP_K * plane_rows, LANES)

    y_prompt, y_sample = pl.pallas_call(
        functools.partial(_combine_kernel, n_prompt_tiles=n_prompt_tiles),
        grid=(n_tok_tiles,),
        in_specs=[pl.BlockSpec((TM, D_MODEL), lambda i: (i, 0)),
                  pl.BlockSpec((TM, LANES), lambda i: (i, 0)),
                  _const_spec((1, D_MODEL)),
                  pl.BlockSpec((TM * ROW_TILES, LANES), lambda i: (i, 0)),
                  pl.BlockSpec((TM * ROW_TILES, LANES), lambda i: (n_tok_tiles + i, 0))],
        out_specs=[pl.BlockSpec((TM, D_MODEL), lambda i: (p_tile(i), 0)),
                   pl.BlockSpec((TM, D_MODEL), lambda i: (s_tile(i), 0))],
        out_shape=[jax.ShapeDtypeStruct((t_prompt, D_MODEL), f32), jax.ShapeDtypeStruct((t_sample, D_MODEL), f32)],
        compiler_params=cparams,
        name="moe_combine",
    )(x1, gcol, nfg, out_tok, out_tok)

    unplane = lambda a: a.transpose(0, 2, 1, 3).reshape(dec_batch, dec_seq, a.shape[-1])
    y_prompt = y_prompt.reshape(n_batch, seq, D_MODEL)
    y_sample = unplane(y_sample.reshape(n_sample_tiles, dec_seq, SEQ_BLK, D_MODEL))
    pool_state_prompt = pstate[None, :, CARRY_ROWS - POOL_STATE:, :]
    p_s = unplane(pplanes)
    pool_state_sample = jnp.concatenate([state_pool[0], p_s], axis=1)[None, :, -POOL_STATE:, :]
    chunk_v_sample = unplane(vplanes)[None]
    return (y_prompt, y_sample, pool_state_prompt, pool_state_sample, chunk_v_sample)
```

```python
import functools
import math

import jax
import jax.numpy as jnp
from jax import lax
from jax.experimental import pallas as pl
from jax.experimental.pallas import tpu as pltpu

D_MODEL = 1024
A_WIDTH = 512
B_WIDTH = 512
N_HEADS = 4
HEAD_DIM = 128
CHUNK = 128
POOL_WINDOWS = (2, 4, 8, 16)
POOL_STATE = 15
N_GROUPS = 4
EXPERTS_PER_GROUP = 8
N_EXPERTS = 32
TOP_K = 2
D_EXPERT = 512
EPS = 1e-6

SUBLANES = 8
LANES = 128
ROW_TILES = D_MODEL // LANES

TM = 256
TE = 512
SEQ_BLK = 32
N_ROUTER_ROWS = 40
CARRY_ROWS = 16
DMA_UNROLL = 32
VMEM_LIMIT = 48 * 1024 * 1024
N_PARTS = 2
EXPERT_VMEM_LIMIT = 62 * 1024 * 1024

_INV_SQRT2 = 1.0 / math.sqrt(2.0)


def _rmsnorm(x, g):
    r = lax.rsqrt(jnp.mean(x * x, axis=-1, keepdims=True) + EPS)
    return (x * r) * g


def _gelu(x):
    return 0.5 * x * (1.0 + lax.erf(x * _INV_SQRT2))


def _layernorm(x, g, b):
    mu = jnp.mean(x, axis=-1, keepdims=True)
    xc = x - mu
    var = jnp.mean(xc * xc, axis=-1, keepdims=True)
    return (xc * lax.rsqrt(var + EPS)) * g + b


def _row_slab(ref, s, n):
    return ref[pl.ds(s, n, stride=ROW_TILES), :]


def _pool_project(pooled, g, bw_ref, bscale_ref):
    lo, hi = g * HEAD_DIM, (g + 1) * HEAD_DIM
    hb = jnp.dot(pooled.astype(jnp.bfloat16), bw_ref[g], preferred_element_type=jnp.float32)
    return hb * bscale_ref[:, lo:hi]


def _prompt_mixers(j, u, v, p, aws_ref, abias_ref, bw_ref, bscale_ref, mix_ref, pcarry_ref, pstate_ref):
    tri = (lax.broadcasted_iota(jnp.int32, (CHUNK, CHUNK), 0)
           >= lax.broadcasted_iota(jnp.int32, (CHUNK, CHUNK), 1))
    vb = v.astype(jnp.bfloat16)
    for hd in range(N_HEADS):
        lo, hi = hd * HEAD_DIM, (hd + 1) * HEAD_DIM
        w = jnp.where(tri, aws_ref[hd], 0.0).astype(jnp.bfloat16)
        for c in range(TM // CHUNK):
            r0, r1 = c * CHUNK, (c + 1) * CHUNK
            z = jnp.dot(w, vb[r0:r1, lo:hi], preferred_element_type=jnp.float32) + abias_ref[:, lo:hi]
            mix_ref[r0:r1, lo:hi] = (u[r0:r1, lo:hi] * z).astype(jnp.bfloat16)

    head_pos = j * TM + lax.broadcasted_iota(jnp.int32, (CARRY_ROWS, LANES), 0)
    for g, w in enumerate(POOL_WINDOWS):
        lo, hi = g * HEAD_DIM, (g + 1) * HEAD_DIM
        pg = p[:, lo:hi]
        acc = jnp.concatenate([pcarry_ref[:, lo:hi], pg], axis=0)
        shift = 1
        while shift < w:
            acc = acc + pltpu.roll(acc, shift, 0)
            shift *= 2
        head = acc[CARRY_ROWS:2 * CARRY_ROWS, :] / jnp.minimum(head_pos + 1, w).astype(jnp.float32)
        mean = jnp.concatenate([head, acc[2 * CARRY_ROWS:, :] * (1.0 / w)], axis=0)
        pooled = mean - pg
        mix_ref[:, A_WIDTH + lo:A_WIDTH + hi] = _pool_project(pooled, g, bw_ref, bscale_ref).astype(jnp.bfloat16)
    tail = p[TM - CARRY_ROWS:, :]
    pcarry_ref[...] = tail
    pstate_ref[0] = tail


def _sample_mixers(u, v, p, aws_ref, abs_ref, st_ref, bw_ref, bscale_ref, mix_ref, pp_ref, vp_ref):
    n_pos = TM // SEQ_BLK
    for i in range(n_pos):
        vp_ref[0, i] = v[i * SEQ_BLK:(i + 1) * SEQ_BLK, :]
        pp_ref[0, i] = p[i * SEQ_BLK:(i + 1) * SEQ_BLK, :]

    for hd in range(N_HEADS):
        lo, hi = hd * HEAD_DIM, (hd + 1) * HEAD_DIM
        vplanes = [v[s * SEQ_BLK:(s + 1) * SEQ_BLK, lo:hi] for s in range(n_pos)]
        for i in range(n_pos):
            z = vplanes[0] * aws_ref[hd * 64 + i * 8]
            for s in range(1, i + 1):
                z = z + vplanes[s] * aws_ref[hd * 64 + i * 8 + s]
            z = z + abs_ref[hd * 8 + i]
            r0, r1 = i * SEQ_BLK, (i + 1) * SEQ_BLK
            mix_ref[r0:r1, lo:hi] = (u[r0:r1, lo:hi] * z).astype(jnp.bfloat16)

    for g, w in enumerate(POOL_WINDOWS):
        lo, hi = g * HEAD_DIM, (g + 1) * HEAD_DIM
        planes = [st_ref[0, k, :, lo:hi] for k in range(POOL_STATE)]
        planes += [p[i * SEQ_BLK:(i + 1) * SEQ_BLK, lo:hi] for i in range(n_pos)]
        pooled = []
        for i in range(n_pos):
            top = POOL_STATE + i
            s = planes[top - w + 1]
            for k in range(top - w + 2, top + 1):
                s = s + planes[k]
            pooled.append(s * (1.0 / w) - planes[top])
        pooled = jnp.concatenate(pooled, axis=0)
        mix_ref[:, A_WIDTH + lo:A_WIDTH + hi] = _pool_project(pooled, g, bw_ref, bscale_ref).astype(jnp.bfloat16)


def _route(h2, rwt_ref, rb_ref, su_ref, cnt_ref, meta_ref, gcol_ref):
    tm = h2.shape[0]
    h_hi = h2.astype(jnp.bfloat16)
    h_lo = (h2 - h_hi.astype(jnp.float32)).astype(jnp.bfloat16)
    s = (jnp.dot(h_hi, rwt_ref[...], preferred_element_type=jnp.float32)
         + jnp.dot(h_lo, rwt_ref[...], preferred_element_type=jnp.float32))
    st = s.T
    lt = st[0:N_ROUTER_ROWS, :] + st[N_ROUTER_ROWS:2 * N_ROUTER_ROWS, :] + rb_ref[...]
    row = lambda i: lt[i:i + 1, :]
    l1 = [row(i) for i in range(N_GROUPS)]
    m1 = jnp.maximum(jnp.maximum(l1[0], l1[1]), jnp.maximum(l1[2], l1[3]))
    grp = jnp.where(l1[0] == m1, 0, jnp.where(l1[1] == m1, 1, jnp.where(l1[2] == m1, 2, 3)))
    se = (jnp.exp(l1[0] - m1) + jnp.exp(l1[1] - m1)) + (jnp.exp(l1[2] - m1) + jnp.exp(l1[3] - m1))
    pg = 1.0 / se
    l2 = []
    for e in range(EXPERTS_PER_GROUP):
        c = [row(N_GROUPS + g * EXPERTS_PER_GROUP + e) for g in range(N_GROUPS)]
        l2.append(jnp.where(grp == 0, c[0], jnp.where(grp == 1, c[1], jnp.where(grp == 2, c[2], c[3]))))
    v0 = functools.reduce(jnp.maximum, l2)
    i0 = jnp.full_like(grp, EXPERTS_PER_GROUP - 1)
    for e in range(EXPERTS_PER_GROUP - 2, -1, -1):
        i0 = jnp.where(l2[e] == v0, e, i0)
    neg = jnp.float32(-jnp.inf)
    l2m = [jnp.where(i0 == e, neg, l2[e]) for e in range(EXPERTS_PER_GROUP)]
    v1 = functools.reduce(jnp.maximum, l2m)
    i1 = jnp.full_like(grp, EXPERTS_PER_GROUP - 1)
    for e in range(EXPERTS_PER_GROUP - 2, -1, -1):
        i1 = jnp.where((l2m[e] == v1) & (i0 != e), e, i1)
    d = jnp.exp(v1 - v0)
    g0 = pg / (1.0 + d)
    g1 = (pg * d) / (1.0 + d)
    e0 = grp * EXPERTS_PER_GROUP + i0
    e1 = grp * EXPERTS_PER_GROUP + i1

    eiota = lax.broadcasted_iota(jnp.int32, (N_EXPERTS, tm), 0)
    hit0 = eiota == e0
    hit1 = eiota == e1
    onehot = (hit0 | hit1).astype(jnp.bfloat16)
    prefix = jnp.dot(onehot, su_ref[...], preferred_element_type=jnp.float32)
    carry = cnt_ref[...]
    base = prefix + jnp.concatenate([carry] * (tm // LANES), axis=1)
    r0 = jnp.sum(jnp.where(hit0, base, 0.0), axis=0, keepdims=True)
    r1 = jnp.sum(jnp.where(hit1, base, 0.0), axis=0, keepdims=True)
    ones = jnp.ones((tm, LANES), jnp.bfloat16)
    cnt_ref[...] = carry + jnp.dot(onehot, ones, preferred_element_type=jnp.float32)

    meta_ref[0:1, :] = e0
    meta_ref[1:2, :] = e1
    meta_ref[2:3, :] = r0.astype(jnp.int32)
    meta_ref[3:4, :] = r1.astype(jnp.int32)
    meta_ref[4:8, :] = jnp.zeros((4, tm), jnp.int32)

    giota = lax.broadcasted_iota(jnp.int32, (LANES, tm), 0)
    gpad = jnp.where(giota == 0, g0, jnp.where(giota == 1, g1, 0.0))
    gcol_ref[...] = gpad.T


def _mixer_kernel(aws_s_ref, abs_s_ref,
                  xp_ref, xs_ref, st_ref, n1g_ref, win_ref, ang_ref, anb_ref, aws_ref, abias_ref, bw_ref, bscale_ref,
                  wout_ref, n2g_ref, rwt_ref, rb_ref, su_ref,
                  x1_ref, h2_ref, meta_ref, gcol_ref, counts_ref, pstate_ref, pp_ref, vp_ref,
                  mix_ref, pcarry_ref, cnt_ref, *, n_prompt_tiles, seq_tiles, part_tiles):
    i = pl.program_id(0)
    is_prompt = i < n_prompt_tiles
    j = i % seq_tiles

    @pl.when(i % part_tiles == 0)
    def _():
        cnt_ref[...] = jnp.zeros_like(cnt_ref)

    @pl.when(is_prompt & (j == 0))
    def _():
        pcarry_ref[...] = jnp.zeros_like(pcarry_ref)

    x = jnp.where(is_prompt, xp_ref[0], xs_ref[...])
    h = _rmsnorm(x, n1g_ref[...]).astype(jnp.bfloat16)
    proj = jnp.dot(h, win_ref[...], preferred_element_type=jnp.float32)
    uv = _gelu(proj[:, : 2 * A_WIDTH])
    u = uv[:, :A_WIDTH]
    v = _layernorm(uv[:, A_WIDTH:], ang_ref[...], anb_ref[...])
    p = proj[:, 2 * A_WIDTH:]

    @pl.when(is_prompt)
    def _():
        _prompt_mixers(j, u, v, p, aws_ref, abias_ref, bw_ref, bscale_ref, mix_ref, pcarry_ref, pstate_ref)

    @pl.when(jnp.logical_not(is_prompt))
    def _():
        _sample_mixers(u, v, p, aws_s_ref, abs_s_ref, st_ref, bw_ref, bscale_ref, mix_ref, pp_ref, vp_ref)

    x1 = x + jnp.dot(mix_ref[...], wout_ref[...], preferred_element_type=jnp.float32)
    x1_ref[...] = x1
    h2 = _rmsnorm(x1, n2g_ref[...])
    for s in range(ROW_TILES):
        h2_ref[pl.ds(s, TM, stride=ROW_TILES), :] = h2[:, s * LANES:(s + 1) * LANES]
    _route(h2, rwt_ref, rb_ref, su_ref, cnt_ref, meta_ref, gcol_ref)
    counts_ref[0] = cnt_ref[...]


def _tables_kernel(counts_ref, meta_ref, asg_ref, tab_ref, dest_vmem, fill_vmem, *dest_smem, t_total):
    h = pl.program_id(0)
    t_part = meta_ref.shape[1]
    cnt = counts_ref[0]
    padded = jnp.floor((cnt + (TE - 1)) * (1.0 / TE)) * TE
    sub = lax.broadcasted_iota(jnp.int32, cnt.shape, 0)
    pad_end = padded
    shift = 1
    while shift < N_EXPERTS:
        pad_end = pad_end + jnp.where(sub >= shift, pltpu.roll(pad_end, shift, 0), 0.0)
        shift *= 2
    pad_start = pad_end - padded
    base = pad_start.astype(jnp.int32)

    for k in range(TOP_K):
        e = meta_ref[k:k + 1, :]
        d = meta_ref[TOP_K + k:TOP_K + k + 1, :]
        for ex in range(N_EXPERTS):
            row = jnp.concatenate([base[ex:ex + 1, :]] * (t_part // LANES), axis=1)
            d = d + jnp.where(e == ex, row, 0)
        dest_vmem[k] = d
    for k, dsm in enumerate(dest_smem):
        pltpu.sync_copy(dest_vmem.at[k], dsm)

    fill_vmem[...] = jnp.full(fill_vmem.shape, -1, jnp.int32)
    pltpu.sync_copy(fill_vmem, asg_ref)
    unroll = 16
    for k, dsm in enumerate(dest_smem):
        def invert(_, carry, dsm=dsm):
            t, a = carry
            for uu in range(unroll):
                asg_ref[0, 0, dsm[0, t + uu]] = a + uu
            return t + unroll, a + unroll
        lax.fori_loop(0, t_part // unroll, invert, (jnp.int32(0), k * t_total + h * t_part))

    n_valid = jnp.maximum(pad_end[N_EXPERTS - 1:, :] * (1.0 / TE), 1.0)
    tile = jnp.minimum(lax.broadcasted_iota(jnp.int32, (1, LANES), 1).astype(jnp.float32), n_valid - 1.0)
    tile_start = tile * TE
    tile_e = jnp.minimum(jnp.sum((pad_end <= tile_start).astype(jnp.float32), axis=0, keepdims=True),
                         N_EXPERTS - 1.0)
    is_e = sub.astype(jnp.float32) == tile_e
    seg_start = jnp.sum(jnp.where(is_e, pad_start, 0.0), axis=0, keepdims=True)
    seg_cnt = jnp.sum(jnp.where(is_e, cnt, 0.0), axis=0, keepdims=True)
    rows = jnp.clip(seg_cnt - (tile_start - seg_start), 0.0, TE)
    tab_ref[0, 0:1, :] = tile_e.astype(jnp.int32)
    tab_ref[0, 1:2, :] = rows.astype(jnp.int32)
    tab_ref[0, 2:3, :] = n_valid.astype(jnp.int32)
    tab_ref[0, 3:, :] = jnp.zeros((SUBLANES - 3, LANES), jnp.int32)


def _expert_kernel(te_ref, nv_ref,
                   src_ref, src_next_ref,
                   dst_ref, dst_prev_ref,
                   h2_ref, w1_ref, w3_ref, w2_ref,
                   out_hbm,
                   xt0, xt1, ob0, ob1, w1b, w3b, w2b, ssem, *, dump_row):
    i = pl.program_id(1)
    g = pl.program_id(0) * pl.num_programs(1) + i
    nv = nv_ref[pl.program_id(0)]
    slab = lambda r: pl.ds(pl.multiple_of(r * ROW_TILES, ROW_TILES), ROW_TILES)

    def gather(idx_ref, xt):
        for r in range(TE):
            xt[r * ROW_TILES:(r + 1) * ROW_TILES, :] = h2_ref[slab(idx_ref[0, 0, r]), :]

    def row_copy(ob, idx_ref, r, sem):
        dst = pl.ds(pl.multiple_of(idx_ref[0, 0, r], ROW_TILES), ROW_TILES)
        return pltpu.make_async_copy(ob.at[slab(r), :], out_hbm.at[dst, :], sem)

    def wait_tile(ob, other, sem):
        pltpu.make_async_copy(other, ob, sem).wait()

    def tile_body(xt, xt_next, ob, ob_other, sem, sem_other, scatter_prev):
        @pl.when(i >= 2)
        def _():
            wait_tile(ob, ob_other, sem)

        gather(src_next_ref, xt_next)
        x = jnp.concatenate([_row_slab(xt, s, TE) for s in range(ROW_TILES)], axis=1).astype(jnp.bfloat16)
        a = jnp.dot(x, w1b[...], preferred_element_type=jnp.float32)
        b = jnp.dot(x, w3b[...], preferred_element_type=jnp.float32)
        if scatter_prev:
            for r in range(TE):
                row_copy(ob_other, dst_prev_ref, r, sem_other).start(priority=r % 2)
        h = (a * jax.nn.sigmoid(a)) * b
        o = jnp.dot(h.astype(jnp.bfloat16), w2b[...], preferred_element_type=jnp.float32)
        for s in range(ROW_TILES):
            ob[pl.ds(s, TE, stride=ROW_TILES), :] = o[:, s * LANES:(s + 1) * LANES]

        @pl.when(i == nv - 1)
        def _():
            def body(rb, carry):
                for uu in range(DMA_UNROLL):
                    row_copy(ob, dst_ref, rb * DMA_UNROLL + uu, sem).start(priority=uu % 2)
                return carry
            lax.fori_loop(0, TE // DMA_UNROLL, body, 0)
            if scatter_prev:
                wait_tile(ob_other, ob, sem_other)
            wait_tile(ob, ob_other, sem)

    @pl.when(i < nv)
    def _():
        @pl.when((i == 0) | (te_ref[g] != te_ref[jnp.maximum(g - 1, 0)]))
        def _():
            w1b[...] = w1_ref[0].astype(jnp.bfloat16)
            w3b[...] = w3_ref[0].astype(jnp.bfloat16)
            w2b[...] = w2_ref[0].astype(jnp.bfloat16)

        @pl.when(g == 0)
        def _():
            ob1[...] = jnp.zeros_like(ob1)
            for half in range(2):
                spare = out_hbm.at[pl.ds(dump_row + half * TE * ROW_TILES, TE * ROW_TILES), :]
                pltpu.sync_copy(ob1, spare)

        @pl.when(i == 0)
        def _():
            gather(src_ref, xt0)
            tile_body(xt0, xt1, ob0, ob1, ssem.at[0], ssem.at[1], scatter_prev=False)

        @pl.when((i > 0) & (i % 2 == 0))
        def _():
            tile_body(xt0, xt1, ob0, ob1, ssem.at[0], ssem.at[1], scatter_prev=True)

        @pl.when(i % 2 == 1)
        def _():
            tile_body(xt1, xt0, ob1, ob0, ssem.at[1], ssem.at[0], scatter_prev=True)


def _combine_kernel(x1_ref, gcol_ref, nfg_ref, o0_ref, o1_ref, yp_ref, ys_ref, *, n_prompt_tiles):
    i = pl.program_id(0)
    o0 = jnp.concatenate([_row_slab(o0_ref, s, TM) for s in range(ROW_TILES)], axis=1)
    o1 = jnp.concatenate([_row_slab(o1_ref, s, TM) for s in range(ROW_TILES)], axis=1)
    g = gcol_ref[...]
    moe = g[:, 0:1] * o0 + g[:, 1:2] * o1
    y = _rmsnorm(x1_ref[...] + moe, nfg_ref[...])

    @pl.when(i < n_prompt_tiles)
    def _():
        yp_ref[...] = y

    @pl.when(i >= n_prompt_tiles)
    def _():
        ys_ref[...] = y


def _const_spec(shape):
    return pl.BlockSpec(shape, lambda *_: (0,) * len(shape))


def kernel(x_prompt, x_sample, state_pool, norm1_g, w_in, a_norm_g, a_norm_b, a_ws, a_bs, b_w, b_scale, w_out,
           norm2_g, r1_w, r1_b, r2_w, r2_b, exp_w1, exp_w3, exp_w2, normf_g):
    f32, bf16, i32 = jnp.float32, jnp.bfloat16, jnp.int32
    n_batch, seq, _ = x_prompt.shape
    dec_batch, dec_seq, _ = x_sample.shape
    assert norm1_g.shape[0] == 1 and seq % TM == 0 and TM % CHUNK == 0
    assert dec_seq * SEQ_BLK == TM and dec_batch % SEQ_BLK == 0 and dec_seq <= CHUNK
    t_prompt = n_batch * seq
    t_sample = dec_batch * dec_seq
    t_total = t_prompt + t_sample
    n_tok_tiles = t_total // TM
    n_prompt_tiles = t_prompt // TM
    n_sample_tiles = t_sample // TM
    seq_tiles = seq // TM
    n_assign = TOP_K * t_total
    plane_rows = t_total * ROW_TILES
    assert n_tok_tiles % N_PARTS == 0
    part_tiles = n_tok_tiles // N_PARTS
    t_part = part_tiles * TM
    n_exp_tiles = -(-(TOP_K * t_part + N_EXPERTS * (TE - 1)) // TE)
    p_rows = n_exp_tiles * TE

    n1g = norm1_g[0][None, :]
    n2g = norm2_g[0][None, :]
    nfg = normf_g[None, :]
    win = w_in[0].astype(bf16)
    wout = w_out[0].astype(bf16)
    ang = a_norm_g[0][None, :]
    anb = a_norm_b[0][None, :]
    bw = b_w[0].astype(bf16)
    bscale = b_scale[0][None, :]
    abias = jnp.repeat(a_bs[0][:, :CHUNK].T, HEAD_DIM, axis=1)
    rw = jnp.concatenate([r1_w[0], r2_w[0].transpose(1, 0, 2).reshape(D_MODEL, N_EXPERTS),
                          jnp.zeros((D_MODEL, N_ROUTER_ROWS - N_GROUPS - N_EXPERTS), f32)], axis=1)
    rw_hi = rw.astype(bf16)
    rw_lo = (rw - rw_hi.astype(f32)).astype(bf16)
    rwt = jnp.concatenate([rw_hi, rw_lo, jnp.zeros((D_MODEL, LANES - 2 * N_ROUTER_ROWS), bf16)], axis=1)
    rbias = jnp.concatenate([r1_b[0], r2_b[0].reshape(-1),
                             jnp.zeros((N_ROUTER_ROWS - N_GROUPS - N_EXPERTS,), f32)])
    rb = jnp.broadcast_to(rbias[:, None], (N_ROUTER_ROWS, TM))
    su = (jnp.arange(TM)[:, None] < jnp.arange(TM)[None, :]).astype(bf16)
    xs_planes = (x_sample.reshape(n_sample_tiles, SEQ_BLK, dec_seq, D_MODEL)
                 .transpose(0, 2, 1, 3).reshape(t_sample, D_MODEL))
    st_planes = state_pool[0].reshape(n_sample_tiles, SEQ_BLK, POOL_STATE, B_WIDTH).transpose(0, 2, 1, 3)
    aws_s = a_ws[0][:, :dec_seq, :dec_seq].reshape(-1)
    abs_s = a_bs[0][:, :dec_seq].reshape(-1)

    cparams = pltpu.CompilerParams(dimension_semantics=("arbitrary",), vmem_limit_bytes=VMEM_LIMIT)

    p_tile = lambda i: jnp.minimum(i, n_prompt_tiles - 1)
    s_tile = lambda i: jnp.maximum(i - n_prompt_tiles, 0)
    plane_shape = jax.ShapeDtypeStruct((n_sample_tiles, dec_seq, SEQ_BLK, A_WIDTH), f32)
    x1, h2, meta, gcol, counts, pstate, pplanes, vplanes = pl.pallas_call(
        functools.partial(_mixer_kernel, n_prompt_tiles=n_prompt_tiles, seq_tiles=seq_tiles, part_tiles=part_tiles),
        grid_spec=pltpu.PrefetchScalarGridSpec(
            num_scalar_prefetch=2,
            grid=(n_tok_tiles,),
            in_specs=[
                pl.BlockSpec((1, TM, D_MODEL), lambda i, *_: (p_tile(i) // seq_tiles, p_tile(i) % seq_tiles, 0)),
                pl.BlockSpec((TM, D_MODEL), lambda i, *_: (s_tile(i), 0)),
                pl.BlockSpec((1, POOL_STATE, SEQ_BLK, B_WIDTH), lambda i, *_: (s_tile(i), 0, 0, 0)),
                _const_spec((1, D_MODEL)), _const_spec((D_MODEL, 3 * A_WIDTH)), _const_spec((1, A_WIDTH)),
                _const_spec((1, A_WIDTH)), _const_spec((N_HEADS, CHUNK, CHUNK)), _const_spec((CHUNK, A_WIDTH)),
                _const_spec((N_GROUPS, HEAD_DIM, HEAD_DIM)), _const_spec((1, B_WIDTH)),
                _const_spec((D_MODEL, D_MODEL)), _const_spec((1, D_MODEL)), _const_spec((D_MODEL, LANES)),
                _const_spec((N_ROUTER_ROWS, TM)), _const_spec((TM, TM)),
            ],
            out_specs=[
                pl.BlockSpec((TM, D_MODEL), lambda i, *_: (i, 0)),
                pl.BlockSpec((TM * ROW_TILES, LANES), lambda i, *_: (i, 0)),
                pl.BlockSpec((SUBLANES, TM), lambda i, *_: (0, i)),
                pl.BlockSpec((TM, LANES), lambda i, *_: (i, 0)),
                pl.BlockSpec((1, N_EXPERTS, LANES), lambda i, *_: (i // part_tiles, 0, 0)),
                pl.BlockSpec((1, CARRY_ROWS, B_WIDTH), lambda i, *_: (p_tile(i) // seq_tiles, 0, 0)),
                pl.BlockSpec((1, dec_seq, SEQ_BLK, A_WIDTH), lambda i, *_: (s_tile(i), 0, 0, 0)),
                pl.BlockSpec((1, dec_seq, SEQ_BLK, A_WIDTH), lambda i, *_: (s_tile(i), 0, 0, 0)),
            ],
            scratch_shapes=[pltpu.VMEM((TM, D_MODEL), bf16), pltpu.VMEM((CARRY_ROWS, B_WIDTH), f32),
                            pltpu.VMEM((N_EXPERTS, LANES), f32)],
        ),
        out_shape=[
            jax.ShapeDtypeStruct((t_total, D_MODEL), f32),
            jax.ShapeDtypeStruct((plane_rows, LANES), f32),
            jax.ShapeDtypeStruct((SUBLANES, t_total), i32),
            jax.ShapeDtypeStruct((t_total, LANES), f32),
            jax.ShapeDtypeStruct((N_PARTS, N_EXPERTS, LANES), f32),
            jax.ShapeDtypeStruct((n_batch, CARRY_ROWS, B_WIDTH), f32),
            plane_shape, plane_shape,
        ],
        compiler_params=cparams,
        name="mixer",
    )(aws_s, abs_s, x_prompt, xs_planes, st_planes, n1g, win, ang, anb, a_ws[0][:, :CHUNK, :CHUNK], abias, bw,
      bscale, wout, n2g, rwt, rb, su)

    assert n_exp_tiles <= LANES and t_part % LANES == 0
    asg, tab = pl.pallas_call(
        functools.partial(_tables_kernel, t_total=t_total),
        grid=(N_PARTS,),
        in_specs=[pl.BlockSpec((1, N_EXPERTS, LANES), lambda h: (h, 0, 0)),
                  pl.BlockSpec((SUBLANES, t_part), lambda h: (0, h))],
        out_specs=[pl.BlockSpec((1, 1, p_rows), lambda h: (h, 0, 0), memory_space=pltpu.SMEM),
                   pl.BlockSpec((1, SUBLANES, LANES), lambda h: (h, 0, 0))],
        scratch_shapes=[pltpu.VMEM((TOP_K, 1, t_part), i32), pltpu.VMEM((1, 1, p_rows), i32)]
        + [pltpu.SMEM((1, t_part), i32)] * TOP_K,
        out_shape=[jax.ShapeDtypeStruct((N_PARTS, 1, p_rows), i32),
                   jax.ShapeDtypeStruct((N_PARTS, SUBLANES, LANES), i32)],
        compiler_params=cparams,
        name="route_tables",
    )(counts, meta)
    tile_e = tab[:, 0, :n_exp_tiles].reshape(-1)
    n_valid = tab[:, 2, 0]
    asg = asg.reshape(-1)
    row = jnp.arange(N_PARTS * p_rows, dtype=i32)
    row_part = row // p_rows
    row_tok = jnp.where(asg >= t_total, asg - t_total, asg)
    row_src = jnp.where(asg >= 0, row_tok - row_part * t_part, 0).reshape(N_PARTS * n_exp_tiles, 1, TE)
    dump_slab = TOP_K * t_total
    spare = dump_slab + ((row % p_rows) // TE % 2) * TE + row % TE
    row_dst = (jnp.where(asg >= 0, asg, spare) * ROW_TILES).reshape(N_PARTS * n_exp_tiles, 1, TE)

    flat = lambda h, i: h * n_exp_tiles + i
    cur_blk = lambda h, i, te, nv: (flat(h, jnp.minimum(i, nv[h] - 1)), 0, 0)
    nxt_blk = lambda h, i, te, nv: (flat(h, jnp.minimum(i + 1, nv[h] - 1)), 0, 0)
    prv_blk = lambda h, i, te, nv: (flat(h, jnp.maximum(jnp.minimum(i, nv[h] - 1) - 1, 0)), 0, 0)
    w_blk = lambda h, i, te, nv: (te[flat(h, i)], 0, 0)
    smem_blk = lambda imap: pl.BlockSpec((1, 1, TE), imap, memory_space=pltpu.SMEM)
    row_buf = pltpu.VMEM((TE * ROW_TILES, LANES), f32)
    out_tok = pl.pallas_call(
        functools.partial(_expert_kernel, dump_row=dump_slab * ROW_TILES),
        grid_spec=pltpu.PrefetchScalarGridSpec(
            num_scalar_prefetch=2,
            grid=(N_PARTS, n_exp_tiles),
            in_specs=[smem_blk(cur_blk), smem_blk(nxt_blk), smem_blk(cur_blk), smem_blk(prv_blk),
                      pl.BlockSpec((t_part * ROW_TILES, LANES), lambda h, i, *_: (h, 0),
                                   pipeline_mode=pl.Buffered(1)),
                      pl.BlockSpec((1, D_MODEL, D_EXPERT), w_blk),
                      pl.BlockSpec((1, D_MODEL, D_EXPERT), w_blk),
                      pl.BlockSpec((1, D_EXPERT, D_MODEL), w_blk)],
            out_specs=pl.BlockSpec(memory_space=pl.ANY),
            scratch_shapes=[row_buf, row_buf, row_buf, row_buf,
                            pltpu.VMEM((D_MODEL, D_EXPERT), bf16), pltpu.VMEM((D_MODEL, D_EXPERT), bf16),
                            pltpu.VMEM((D_EXPERT, D_MODEL), bf16),
                            pltpu.SemaphoreType.DMA((2,))],
        ),
        out_shape=jax.ShapeDtypeStruct(((TOP_K * t_total + 2 * TE) * ROW_TILES, LANES), f32),
        compiler_params=pltpu.CompilerParams(dimension_semantics=("arbitrary", "arbitrary"),
                                             vmem_limit_bytes=EXPERT_VMEM_LIMIT),
        name="moe_experts",
    )(tile_e, n_valid, row_src, row_src, row_dst, row_dst, h2, exp_w1[0], exp_w3[0], exp_w2[0])

    y_prompt, y_sample = pl.pallas_call(
        functools.partial(_combine_kernel, n_prompt_tiles=n_prompt_tiles),
        grid=(n_tok_tiles,),
        in_specs=[pl.BlockSpec((TM, D_MODEL), lambda i: (i, 0)),
                  pl.BlockSpec((TM, LANES), lambda i: (i, 0)),
                  _const_spec((1, D_MODEL)),
                  pl.BlockSpec((TM * ROW_TILES, LANES), lambda i: (i, 0)),
                  pl.BlockSpec((TM * ROW_TILES, LANES), lambda i: (n_tok_tiles + i, 0))],
        out_specs=[pl.BlockSpec((TM, D_MODEL), lambda i: (p_tile(i), 0)),
                   pl.BlockSpec((TM, D_MODEL), lambda i: (s_tile(i), 0))],
        out_shape=[jax.ShapeDtypeStruct((t_prompt, D_MODEL), f32), jax.ShapeDtypeStruct((t_sample, D_MODEL), f32)],
        compiler_params=cparams,
        name="moe_combine",
    )(x1, gcol, nfg, out_tok, out_tok)

    unplane = lambda a: a.transpose(0, 2, 1, 3).reshape(dec_batch, dec_seq, a.shape[-1])
    y_prompt = y_prompt.reshape(n_batch, seq, D_MODEL)
    y_sample = unplane(y_sample.reshape(n_sample_tiles, dec_seq, SEQ_BLK, D_MODEL))
    pool_state_prompt = pstate[None, :, CARRY_ROWS - POOL_STATE:, :]
    p_s = unplane(pplanes)
    pool_state_sample = jnp.concatenate([state_pool[0], p_s], axis=1)[None, :, -POOL_STATE:, :]
    chunk_v_sample = unplane(vplanes)[None]
    return (y_prompt, y_sample, pool_state_prompt, pool_state_sample, chunk_v_sample)
```

```python
import functools
import math

import jax
import jax.numpy as jnp
from jax import lax
from jax.experimental import pallas as pl
from jax.experimental.pallas import tpu as pltpu

D_MODEL = 1024
A_WIDTH = 512
B_WIDTH = 512
N_HEADS = 4
HEAD_DIM = 128
CHUNK = 128
POOL_WINDOWS = (2, 4, 8, 16)
POOL_STATE = 15
N_GROUPS = 4
EXPERTS_PER_GROUP = 8
N_EXPERTS = 32
TOP_K = 2
D_EXPERT = 512
EPS = 1e-6

SUBLANES = 8
LANES = 128
ROW_TILES = D_MODEL // LANES

TM = 256
TC = 512
TE = 512
SEQ_BLK = 32
N_ROUTER_ROWS = 40
CARRY_ROWS = 16
DMA_UNROLL = 32
VMEM_LIMIT = 48 * 1024 * 1024
N_PARTS = 2
EXPERT_VMEM_LIMIT = 62 * 1024 * 1024

_INV_SQRT2 = 1.0 / math.sqrt(2.0)


def _rmsnorm(x, g):
    r = lax.rsqrt(jnp.mean(x * x, axis=-1, keepdims=True) + EPS)
    return (x * r) * g


def _gelu(x):
    return 0.5 * x * (1.0 + lax.erf(x * _INV_SQRT2))


def _layernorm(x, g, b):
    mu = jnp.mean(x, axis=-1, keepdims=True)
    xc = x - mu
    var = jnp.mean(xc * xc, axis=-1, keepdims=True)
    return (xc * lax.rsqrt(var + EPS)) * g + b


def _row_slab(ref, s, n):
    return ref[pl.ds(s, n, stride=ROW_TILES), :]


def _pool_project(pooled, g, bw_ref, bscale_ref):
    lo, hi = g * HEAD_DIM, (g + 1) * HEAD_DIM
    hb = jnp.dot(pooled.astype(jnp.bfloat16), bw_ref[g], preferred_element_type=jnp.float32)
    return hb * bscale_ref[:, lo:hi]


def _prompt_mixers(j, u, v, p, aws_ref, abias_ref, bw_ref, bscale_ref, mix_ref, pcarry_ref, pstate_ref):
    tri = (lax.broadcasted_iota(jnp.int32, (CHUNK, CHUNK), 0)
           >= lax.broadcasted_iota(jnp.int32, (CHUNK, CHUNK), 1))
    vb = v.astype(jnp.bfloat16)
    for hd in range(N_HEADS):
        lo, hi = hd * HEAD_DIM, (hd + 1) * HEAD_DIM
        w = jnp.where(tri, aws_ref[hd], 0.0).astype(jnp.bfloat16)
        for c in range(TM // CHUNK):
            r0, r1 = c * CHUNK, (c + 1) * CHUNK
            z = jnp.dot(w, vb[r0:r1, lo:hi], preferred_element_type=jnp.float32) + abias_ref[:, lo:hi]
            mix_ref[r0:r1, lo:hi] = (u[r0:r1, lo:hi] * z).astype(jnp.bfloat16)

    head_pos = j * TM + lax.broadcasted_iota(jnp.int32, (CARRY_ROWS, LANES), 0)
    for g, w in enumerate(POOL_WINDOWS):
        lo, hi = g * HEAD_DIM, (g + 1) * HEAD_DIM
        pg = p[:, lo:hi]
        acc = jnp.concatenate([pcarry_ref[:, lo:hi], pg], axis=0)
        shift = 1
        while shift < w:
            acc = acc + pltpu.roll(acc, shift, 0)
            shift *= 2
        head = acc[CARRY_ROWS:2 * CARRY_ROWS, :] / jnp.minimum(head_pos + 1, w).astype(jnp.float32)
        mean = jnp.concatenate([head, acc[2 * CARRY_ROWS:, :] * (1.0 / w)], axis=0)
        pooled = mean - pg
        mix_ref[:, A_WIDTH + lo:A_WIDTH + hi] = _pool_project(pooled, g, bw_ref, bscale_ref).astype(jnp.bfloat16)
    tail = p[TM - CARRY_ROWS:, :]
    pcarry_ref[...] = tail
    pstate_ref[0] = tail


def _sample_mixers(u, v, p, aws_ref, abs_ref, st_ref, bw_ref, bscale_ref, mix_ref, pp_ref, vp_ref):
    n_pos = TM // SEQ_BLK
    for i in range(n_pos):
        vp_ref[0, i] = v[i * SEQ_BLK:(i + 1) * SEQ_BLK, :]
        pp_ref[0, i] = p[i * SEQ_BLK:(i + 1) * SEQ_BLK, :]

    for hd in range(N_HEADS):
        lo, hi = hd * HEAD_DIM, (hd + 1) * HEAD_DIM
        vplanes = [v[s * SEQ_BLK:(s + 1) * SEQ_BLK, lo:hi] for s in range(n_pos)]
        for i in range(n_pos):
            z = vplanes[0] * aws_ref[(hd * n_pos + i) * n_pos]
            for s in range(1, i + 1):
                z = z + vplanes[s] * aws_ref[(hd * n_pos + i) * n_pos + s]
            z = z + abs_ref[hd * n_pos + i]
            r0, r1 = i * SEQ_BLK, (i + 1) * SEQ_BLK
            mix_ref[r0:r1, lo:hi] = (u[r0:r1, lo:hi] * z).astype(jnp.bfloat16)

    for g, w in enumerate(POOL_WINDOWS):
        lo, hi = g * HEAD_DIM, (g + 1) * HEAD_DIM
        planes = [st_ref[0, k, :, lo:hi] for k in range(POOL_STATE)]
        planes += [p[i * SEQ_BLK:(i + 1) * SEQ_BLK, lo:hi] for i in range(n_pos)]
        pooled = []
        for i in range(n_pos):
            top = POOL_STATE + i
            s = planes[top - w + 1]
            for k in range(top - w + 2, top + 1):
                s = s + planes[k]
            pooled.append(s * (1.0 / w) - planes[top])
        pooled = jnp.concatenate(pooled, axis=0)
        mix_ref[:, A_WIDTH + lo:A_WIDTH + hi] = _pool_project(pooled, g, bw_ref, bscale_ref).astype(jnp.bfloat16)


def _route(h2, rwt_ref, rb_ref, su_ref, cnt_ref, meta_ref, gcol_ref):
    tm = h2.shape[0]
    h_hi = h2.astype(jnp.bfloat16)
    h_lo = (h2 - h_hi.astype(jnp.float32)).astype(jnp.bfloat16)
    s = (jnp.dot(h_hi, rwt_ref[...], preferred_element_type=jnp.float32)
         + jnp.dot(h_lo, rwt_ref[...], preferred_element_type=jnp.float32))
    st = s.T
    lt = st[0:N_ROUTER_ROWS, :] + st[N_ROUTER_ROWS:2 * N_ROUTER_ROWS, :] + rb_ref[...]
    row = lambda i: lt[i:i + 1, :]
    l1 = [row(i) for i in range(N_GROUPS)]
    m1 = jnp.maximum(jnp.maximum(l1[0], l1[1]), jnp.maximum(l1[2], l1[3]))
    grp = jnp.where(l1[0] == m1, 0, jnp.where(l1[1] == m1, 1, jnp.where(l1[2] == m1, 2, 3)))
    se = (jnp.exp(l1[0] - m1) + jnp.exp(l1[1] - m1)) + (jnp.exp(l1[2] - m1) + jnp.exp(l1[3] - m1))
    pg = 1.0 / se
    l2 = []
    for e in range(EXPERTS_PER_GROUP):
        c = [row(N_GROUPS + g * EXPERTS_PER_GROUP + e) for g in range(N_GROUPS)]
        l2.append(jnp.where(grp == 0, c[0], jnp.where(grp == 1, c[1], jnp.where(grp == 2, c[2], c[3]))))
    v0 = functools.reduce(jnp.maximum, l2)
    i0 = jnp.full_like(grp, EXPERTS_PER_GROUP - 1)
    for e in range(EXPERTS_PER_GROUP - 2, -1, -1):
        i0 = jnp.where(l2[e] == v0, e, i0)
    neg = jnp.float32(-jnp.inf)
    l2m = [jnp.where(i0 == e, neg, l2[e]) for e in range(EXPERTS_PER_GROUP)]
    v1 = functools.reduce(jnp.maximum, l2m)
    i1 = jnp.full_like(grp, EXPERTS_PER_GROUP - 1)
    for e in range(EXPERTS_PER_GROUP - 2, -1, -1):
        i1 = jnp.where((l2m[e] == v1) & (i0 != e), e, i1)
    d = jnp.exp(v1 - v0)
    g0 = pg / (1.0 + d)
    g1 = (pg * d) / (1.0 + d)
    e0 = grp * EXPERTS_PER_GROUP + i0
    e1 = grp * EXPERTS_PER_GROUP + i1

    eiota = lax.broadcasted_iota(jnp.int32, (N_EXPERTS, tm), 0)
    hit0 = eiota == e0
    hit1 = eiota == e1
    onehot = (hit0 | hit1).astype(jnp.bfloat16)
    prefix = jnp.dot(onehot, su_ref[...], preferred_element_type=jnp.float32)
    carry = cnt_ref[...]
    base = prefix + jnp.concatenate([carry] * (tm // LANES), axis=1)
    r0 = jnp.sum(jnp.where(hit0, base, 0.0), axis=0, keepdims=True)
    r1 = jnp.sum(jnp.where(hit1, base, 0.0), axis=0, keepdims=True)
    ones = jnp.ones((tm, LANES), jnp.bfloat16)
    cnt_ref[...] = carry + jnp.dot(onehot, ones, preferred_element_type=jnp.float32)

    meta_ref[0:1, :] = e0
    meta_ref[1:2, :] = e1
    meta_ref[2:3, :] = r0.astype(jnp.int32)
    meta_ref[3:4, :] = r1.astype(jnp.int32)
    meta_ref[4:8, :] = jnp.zeros((4, tm), jnp.int32)

    giota = lax.broadcasted_iota(jnp.int32, (LANES, tm), 0)
    gpad = jnp.where(giota == 0, g0, jnp.where(giota == 1, g1, 0.0))
    gcol_ref[...] = gpad.T


def _mixer_kernel(aws_s_ref, abs_s_ref,
                  xp_ref, xs_ref, st_ref, n1g_ref, win_ref, ang_ref, anb_ref, aws_ref, abias_ref, bw_ref, bscale_ref,
                  wout_ref, n2g_ref, rwt_ref, rb_ref, su_ref,
                  x1_ref, h2_ref, meta_ref, gcol_ref, counts_ref, pstate_ref, pp_ref, vp_ref,
                  mix_ref, pcarry_ref, cnt_ref, *, n_prompt_tiles, seq_tiles, part_tiles):
    i = pl.program_id(0)
    is_prompt = i < n_prompt_tiles
    j = i % seq_tiles

    @pl.when(i % part_tiles == 0)
    def _():
        cnt_ref[...] = jnp.zeros_like(cnt_ref)

    @pl.when(is_prompt & (j == 0))
    def _():
        pcarry_ref[...] = jnp.zeros_like(pcarry_ref)

    x = jnp.where(is_prompt, xp_ref[0], xs_ref[...])
    h = _rmsnorm(x, n1g_ref[...]).astype(jnp.bfloat16)
    proj = jnp.dot(h, win_ref[...], preferred_element_type=jnp.float32)
    uv = _gelu(proj[:, : 2 * A_WIDTH])
    u = uv[:, :A_WIDTH]
    v = _layernorm(uv[:, A_WIDTH:], ang_ref[...], anb_ref[...])
    p = proj[:, 2 * A_WIDTH:]

    @pl.when(is_prompt)
    def _():
        _prompt_mixers(j, u, v, p, aws_ref, abias_ref, bw_ref, bscale_ref, mix_ref, pcarry_ref, pstate_ref)

    @pl.when(jnp.logical_not(is_prompt))
    def _():
        _sample_mixers(u, v, p, aws_s_ref, abs_s_ref, st_ref, bw_ref, bscale_ref, mix_ref, pp_ref, vp_ref)

    x1 = x + jnp.dot(mix_ref[...], wout_ref[...], preferred_element_type=jnp.float32)
    x1_ref[...] = x1
    h2 = _rmsnorm(x1, n2g_ref[...])
    for s in range(ROW_TILES):
        h2_ref[pl.ds(s, TM, stride=ROW_TILES), :] = h2[:, s * LANES:(s + 1) * LANES]
    _route(h2, rwt_ref, rb_ref, su_ref, cnt_ref, meta_ref, gcol_ref)
    counts_ref[0] = cnt_ref[...]


def _tables_kernel(counts_ref, meta_ref, asg_ref, tab_ref, dest_vmem, fill_vmem, *dest_smem, t_total):
    h = pl.program_id(0)
    t_part = meta_ref.shape[1]
    cnt = counts_ref[0]
    padded = jnp.floor((cnt + (TE - 1)) * (1.0 / TE)) * TE
    sub = lax.broadcasted_iota(jnp.int32, cnt.shape, 0)
    pad_end = padded
    shift = 1
    while shift < N_EXPERTS:
        pad_end = pad_end + jnp.where(sub >= shift, pltpu.roll(pad_end, shift, 0), 0.0)
        shift *= 2
    pad_start = pad_end - padded
    base = pad_start.astype(jnp.int32)

    for k in range(TOP_K):
        e = meta_ref[k:k + 1, :]
        d = meta_ref[TOP_K + k:TOP_K + k + 1, :]
        for ex in range(N_EXPERTS):
            row = jnp.concatenate([base[ex:ex + 1, :]] * (t_part // LANES), axis=1)
            d = d + jnp.where(e == ex, row, 0)
        dest_vmem[k] = d
    for k, dsm in enumerate(dest_smem):
        pltpu.sync_copy(dest_vmem.at[k], dsm)

    fill_vmem[...] = jnp.full(fill_vmem.shape, -1, jnp.int32)
    pltpu.sync_copy(fill_vmem, asg_ref)
    unroll = 16
    for k, dsm in enumerate(dest_smem):
        def invert(_, carry, dsm=dsm):
            t, a = carry
            for uu in range(unroll):
                asg_ref[0, 0, dsm[0, t + uu]] = a + uu
            return t + unroll, a + unroll
        lax.fori_loop(0, t_part // unroll, invert, (jnp.int32(0), k * t_total + h * t_part))

    n_valid = jnp.maximum(pad_end[N_EXPERTS - 1:, :] * (1.0 / TE), 1.0)
    tile = jnp.minimum(lax.broadcasted_iota(jnp.int32, (1, LANES), 1).astype(jnp.float32), n_valid - 1.0)
    tile_start = tile * TE
    tile_e = jnp.minimum(jnp.sum((pad_end <= tile_start).astype(jnp.float32), axis=0, keepdims=True),
                         N_EXPERTS - 1.0)
    is_e = sub.astype(jnp.float32) == tile_e
    seg_start = jnp.sum(jnp.where(is_e, pad_start, 0.0), axis=0, keepdims=True)
    seg_cnt = jnp.sum(jnp.where(is_e, cnt, 0.0), axis=0, keepdims=True)
    rows = jnp.clip(seg_cnt - (tile_start - seg_start), 0.0, TE)
    tab_ref[0, 0:1, :] = tile_e.astype(jnp.int32)
    tab_ref[0, 1:2, :] = rows.astype(jnp.int32)
    tab_ref[0, 2:3, :] = n_valid.astype(jnp.int32)
    tab_ref[0, 3:, :] = jnp.zeros((SUBLANES - 3, LANES), jnp.int32)


def _expert_kernel(te_ref, nv_ref, nrows_ref,
                   src_ref, src_next_ref, dst_ref,
                   h2_ref, w1_ref, w3_ref, w2_ref,
                   out_hbm,
                   xt0, xt1, ob0, ob1, w1b, w3b, w2b, ssem):
    i = pl.program_id(1)
    g = pl.program_id(0) * pl.num_programs(1) + i
    nv = nv_ref[pl.program_id(0)]
    slab = lambda r: pl.ds(pl.multiple_of(r * ROW_TILES, ROW_TILES), ROW_TILES)

    def gather(idx_ref, xt):
        for r in range(TE):
            xt[r * ROW_TILES:(r + 1) * ROW_TILES, :] = h2_ref[slab(idx_ref[0, 0, r]), :]

    def wait_scatter(tile, ob, other, sem):
        n = nrows_ref[tile] * ROW_TILES
        pltpu.make_async_copy(other.at[pl.ds(0, n), :], ob.at[pl.ds(0, n), :], sem).wait()

    def tile_body(xt, xt_next, ob, ob_other, sem, sem_other):
        @pl.when(i >= 2)
        def _():
            wait_scatter(g - 2, ob, ob_other, sem)

        gather(src_next_ref, xt_next)
        x = jnp.concatenate([_row_slab(xt, s, TE) for s in range(ROW_TILES)], axis=1).astype(jnp.bfloat16)
        a = jnp.dot(x, w1b[...], preferred_element_type=jnp.float32)
        b = jnp.dot(x, w3b[...], preferred_element_type=jnp.float32)
        h = (a * jax.nn.sigmoid(a)) * b
        o = jnp.dot(h.astype(jnp.bfloat16), w2b[...], preferred_element_type=jnp.float32)
        for s in range(ROW_TILES):
            ob[pl.ds(s, TE, stride=ROW_TILES), :] = o[:, s * LANES:(s + 1) * LANES]

        n = nrows_ref[g]
        n_full = n // DMA_UNROLL
        copy = lambda r: pltpu.make_async_copy(ob.at[slab(r), :], out_hbm.at[dst_ref[0, 0, r]], sem)

        def body(rb, carry):
            for uu in range(DMA_UNROLL):
                copy(rb * DMA_UNROLL + uu).start(priority=uu % 2)
            return carry
        lax.fori_loop(0, n_full, body, 0)

        def tail(r, carry):
            copy(r).start(priority=1)
            return carry
        lax.fori_loop(n_full * DMA_UNROLL, n, tail, 0)

        @pl.when(i == nv - 1)
        def _():
            @pl.when(i >= 1)
            def _():
                wait_scatter(g - 1, ob_other, ob, sem_other)
            wait_scatter(g, ob, ob_other, sem)

    @pl.when(i < nv)
    def _():
        @pl.when((i == 0) | (te_ref[g] != te_ref[jnp.maximum(g - 1, 0)]))
        def _():
            w1b[...] = w1_ref[0].astype(jnp.bfloat16)
            w3b[...] = w3_ref[0].astype(jnp.bfloat16)
            w2b[...] = w2_ref[0].astype(jnp.bfloat16)

        @pl.when(i == 0)
        def _():
            gather(src_ref, xt0)

        @pl.when(i % 2 == 0)
        def _():
            tile_body(xt0, xt1, ob0, ob1, ssem.at[0], ssem.at[1])

        @pl.when(i % 2 == 1)
        def _():
            tile_body(xt1, xt0, ob1, ob0, ssem.at[1], ssem.at[0])


def _combine_kernel(x1_ref, gcol_ref, nfg_ref, o0_ref, o1_ref, yp_ref, ys_ref, *, n_prompt_tiles):
    i = pl.program_id(0)
    o0 = jnp.concatenate([_row_slab(o0_ref, s, TC) for s in range(ROW_TILES)], axis=1)
    o1 = jnp.concatenate([_row_slab(o1_ref, s, TC) for s in range(ROW_TILES)], axis=1)
    g = gcol_ref[...]
    moe = g[:, 0:1] * o0 + g[:, 1:2] * o1
    y = _rmsnorm(x1_ref[...] + moe, nfg_ref[...])

    @pl.when(i < n_prompt_tiles)
    def _():
        yp_ref[...] = y

    @pl.when(i >= n_prompt_tiles)
    def _():
        ys_ref[...] = y


def _const_spec(shape):
    return pl.BlockSpec(shape, lambda *_: (0,) * len(shape))


def kernel(x_prompt, x_sample, state_pool, norm1_g, w_in, a_norm_g, a_norm_b, a_ws, a_bs, b_w, b_scale, w_out,
           norm2_g, r1_w, r1_b, r2_w, r2_b, exp_w1, exp_w3, exp_w2, normf_g):
    f32, bf16, i32 = jnp.float32, jnp.bfloat16, jnp.int32
    n_batch, seq, _ = x_prompt.shape
    dec_batch, dec_seq, _ = x_sample.shape
    assert norm1_g.shape[0] == 1 and seq % TM == 0 and TM % CHUNK == 0
    assert dec_seq * SEQ_BLK == TM and dec_batch % SEQ_BLK == 0 and dec_seq <= CHUNK
    t_prompt = n_batch * seq
    t_sample = dec_batch * dec_seq
    t_total = t_prompt + t_sample
    n_tok_tiles = t_total // TM
    n_prompt_tiles = t_prompt // TM
    n_sample_tiles = t_sample // TM
    seq_tiles = seq // TM
    n_assign = TOP_K * t_total
    plane_rows = t_total * ROW_TILES
    assert n_tok_tiles % N_PARTS == 0
    part_tiles = n_tok_tiles // N_PARTS
    t_part = part_tiles * TM
    n_exp_tiles = -(-(TOP_K * t_part + N_EXPERTS * (TE - 1)) // TE)
    p_rows = n_exp_tiles * TE

    n1g = norm1_g[0][None, :]
    n2g = norm2_g[0][None, :]
    nfg = normf_g[None, :]
    win = w_in[0].astype(bf16)
    wout = w_out[0].astype(bf16)
    ang = a_norm_g[0][None, :]
    anb = a_norm_b[0][None, :]
    bw = b_w[0].astype(bf16)
    bscale = b_scale[0][None, :]
    abias = jnp.repeat(a_bs[0][:, :CHUNK].T, HEAD_DIM, axis=1)
    rw = jnp.concatenate([r1_w[0], r2_w[0].transpose(1, 0, 2).reshape(D_MODEL, N_EXPERTS),
                          jnp.zeros((D_MODEL, N_ROUTER_ROWS - N_GROUPS - N_EXPERTS), f32)], axis=1)
    rw_hi = rw.astype(bf16)
    rw_lo = (rw - rw_hi.astype(f32)).astype(bf16)
    rwt = jnp.concatenate([rw_hi, rw_lo, jnp.zeros((D_MODEL, LANES - 2 * N_ROUTER_ROWS), bf16)], axis=1)
    rbias = jnp.concatenate([r1_b[0], r2_b[0].reshape(-1),
                             jnp.zeros((N_ROUTER_ROWS - N_GROUPS - N_EXPERTS,), f32)])
    rb = jnp.broadcast_to(rbias[:, None], (N_ROUTER_ROWS, TM))
    su = (jnp.arange(TM)[:, None] < jnp.arange(TM)[None, :]).astype(bf16)
    xs_planes = (x_sample.reshape(n_sample_tiles, SEQ_BLK, dec_seq, D_MODEL)
                 .transpose(0, 2, 1, 3).reshape(t_sample, D_MODEL))
    st_planes = state_pool[0].reshape(n_sample_tiles, SEQ_BLK, POOL_STATE, B_WIDTH).transpose(0, 2, 1, 3)
    aws_s = a_ws[0][:, :dec_seq, :dec_seq].reshape(-1)
    abs_s = a_bs[0][:, :dec_seq].reshape(-1)

    cparams = pltpu.CompilerParams(dimension_semantics=("arbitrary",), vmem_limit_bytes=VMEM_LIMIT)

    p_tile = lambda i: jnp.minimum(i, n_prompt_tiles - 1)
    s_tile = lambda i: jnp.maximum(i - n_prompt_tiles, 0)
    plane_shape = jax.ShapeDtypeStruct((n_sample_tiles, dec_seq, SEQ_BLK, A_WIDTH), f32)
    x1, h2, meta, gcol, counts, pstate, pplanes, vplanes = pl.pallas_call(
        functools.partial(_mixer_kernel, n_prompt_tiles=n_prompt_tiles, seq_tiles=seq_tiles, part_tiles=part_tiles),
        grid_spec=pltpu.PrefetchScalarGridSpec(
            num_scalar_prefetch=2,
            grid=(n_tok_tiles,),
            in_specs=[
                pl.BlockSpec((1, TM, D_MODEL), lambda i, *_: (p_tile(i) // seq_tiles, p_tile(i) % seq_tiles, 0)),
                pl.BlockSpec((TM, D_MODEL), lambda i, *_: (s_tile(i), 0)),
                pl.BlockSpec((1, POOL_STATE, SEQ_BLK, B_WIDTH), lambda i, *_: (s_tile(i), 0, 0, 0)),
                _const_spec((1, D_MODEL)), _const_spec((D_MODEL, 3 * A_WIDTH)), _const_spec((1, A_WIDTH)),
                _const_spec((1, A_WIDTH)), _const_spec((N_HEADS, CHUNK, CHUNK)), _const_spec((CHUNK, A_WIDTH)),
                _const_spec((N_GROUPS, HEAD_DIM, HEAD_DIM)), _const_spec((1, B_WIDTH)),
                _const_spec((D_MODEL, D_MODEL)), _const_spec((1, D_MODEL)), _const_spec((D_MODEL, LANES)),
                _const_spec((N_ROUTER_ROWS, TM)), _const_spec((TM, TM)),
            ],
            out_specs=[
                pl.BlockSpec((TM, D_MODEL), lambda i, *_: (i, 0)),
                pl.BlockSpec((TM * ROW_TILES, LANES), lambda i, *_: (i, 0)),
                pl.BlockSpec((SUBLANES, TM), lambda i, *_: (0, i)),
                pl.BlockSpec((TM, LANES), lambda i, *_: (i, 0)),
                pl.BlockSpec((1, N_EXPERTS, LANES), lambda i, *_: (i // part_tiles, 0, 0)),
                pl.BlockSpec((1, CARRY_ROWS, B_WIDTH), lambda i, *_: (p_tile(i) // seq_tiles, 0, 0)),
                pl.BlockSpec((1, dec_seq, SEQ_BLK, A_WIDTH), lambda i, *_: (s_tile(i), 0, 0, 0)),
                pl.BlockSpec((1, dec_seq, SEQ_BLK, A_WIDTH), lambda i, *_: (s_tile(i), 0, 0, 0)),
            ],
            scratch_shapes=[pltpu.VMEM((TM, D_MODEL), bf16), pltpu.VMEM((CARRY_ROWS, B_WIDTH), f32),
                            pltpu.VMEM((N_EXPERTS, LANES), f32)],
        ),
        out_shape=[
            jax.ShapeDtypeStruct((t_total, D_MODEL), f32),
            jax.ShapeDtypeStruct((plane_rows, LANES), f32),
            jax.ShapeDtypeStruct((SUBLANES, t_total), i32),
            jax.ShapeDtypeStruct((t_total, LANES), f32),
            jax.ShapeDtypeStruct((N_PARTS, N_EXPERTS, LANES), f32),
            jax.ShapeDtypeStruct((n_batch, CARRY_ROWS, B_WIDTH), f32),
            plane_shape, plane_shape,
        ],
        compiler_params=cparams,
        name="mixer",
    )(aws_s, abs_s, x_prompt, xs_planes, st_planes, n1g, win, ang, anb, a_ws[0][:, :CHUNK, :CHUNK], abias, bw,
      bscale, wout, n2g, rwt, rb, su)

    assert n_exp_tiles <= LANES and t_part % LANES == 0
    asg, tab = pl.pallas_call(
        functools.partial(_tables_kernel, t_total=t_total),
        grid=(N_PARTS,),
        in_specs=[pl.BlockSpec((1, N_EXPERTS, LANES), lambda h: (h, 0, 0)),
                  pl.BlockSpec((SUBLANES, t_part), lambda h: (0, h))],
        out_specs=[pl.BlockSpec((1, 1, p_rows), lambda h: (h, 0, 0), memory_space=pltpu.SMEM),
                   pl.BlockSpec((1, SUBLANES, LANES), lambda h: (h, 0, 0))],
        scratch_shapes=[pltpu.VMEM((TOP_K, 1, t_part), i32), pltpu.VMEM((1, 1, p_rows), i32)]
        + [pltpu.SMEM((1, t_part), i32)] * TOP_K,
        out_shape=[jax.ShapeDtypeStruct((N_PARTS, 1, p_rows), i32),
                   jax.ShapeDtypeStruct((N_PARTS, SUBLANES, LANES), i32)],
        compiler_params=cparams,
        name="route_tables",
    )(counts, meta)
    tile_e = tab[:, 0, :n_exp_tiles].reshape(-1)
    tile_rows = tab[:, 1, :n_exp_tiles].reshape(-1)
    n_valid = tab[:, 2, 0]
    asg = asg.reshape(-1)
    row_part = jnp.arange(N_PARTS * p_rows, dtype=i32) // p_rows
    row_tok = jnp.where(asg >= t_total, asg - t_total, asg)
    row_src = jnp.where(asg >= 0, row_tok - row_part * t_part, 0).reshape(N_PARTS * n_exp_tiles, 1, TE)
    row_dst = jnp.maximum(asg, 0).reshape(N_PARTS * n_exp_tiles, 1, TE)

    flat = lambda h, i: h * n_exp_tiles + i
    cur_blk = lambda h, i, te, nv, nr: (flat(h, jnp.minimum(i, nv[h] - 1)), 0, 0)
    nxt_blk = lambda h, i, te, nv, nr: (flat(h, jnp.minimum(i + 1, nv[h] - 1)), 0, 0)
    w_blk = lambda h, i, te, nv, nr: (te[flat(h, i)], 0, 0)
    smem_blk = lambda imap: pl.BlockSpec((1, 1, TE), imap, memory_space=pltpu.SMEM)
    row_buf = pltpu.VMEM((TE * ROW_TILES, LANES), f32)
    out_tok = pl.pallas_call(
        _expert_kernel,
        grid_spec=pltpu.PrefetchScalarGridSpec(
            num_scalar_prefetch=3,
            grid=(N_PARTS, n_exp_tiles),
            in_specs=[smem_blk(cur_blk), smem_blk(nxt_blk), smem_blk(cur_blk),
                      pl.BlockSpec((t_part * ROW_TILES, LANES), lambda h, i, *_: (h, 0),
                                   pipeline_mode=pl.Buffered(1)),
                      pl.BlockSpec((1, D_MODEL, D_EXPERT), w_blk),
                      pl.BlockSpec((1, D_MODEL, D_EXPERT), w_blk),
                      pl.BlockSpec((1, D_EXPERT, D_MODEL), w_blk)],
            out_specs=pl.BlockSpec(memory_space=pl.ANY),
            scratch_shapes=[row_buf, row_buf, row_buf, row_buf,
                            pltpu.VMEM((D_MODEL, D_EXPERT), bf16), pltpu.VMEM((D_MODEL, D_EXPERT), bf16),
                            pltpu.VMEM((D_EXPERT, D_MODEL), bf16),
                            pltpu.SemaphoreType.DMA((2,))],
        ),
        out_shape=jax.ShapeDtypeStruct((TOP_K * t_total, ROW_TILES, LANES), f32),
        compiler_params=pltpu.CompilerParams(dimension_semantics=("arbitrary", "arbitrary"),
                                             vmem_limit_bytes=EXPERT_VMEM_LIMIT),
        name="moe_experts",
    )(tile_e, n_valid, tile_rows, row_src, row_src, row_dst, h2, exp_w1[0], exp_w3[0], exp_w2[0])
    out_tok = out_tok.reshape(TOP_K * plane_rows, LANES)

    assert t_prompt % TC == 0 and t_sample % TC == 0
    n_c_prompt = t_prompt // TC
    y_prompt, y_sample = pl.pallas_call(
        functools.partial(_combine_kernel, n_prompt_tiles=n_c_prompt),
        grid=(t_total // TC,),
        in_specs=[pl.BlockSpec((TC, D_MODEL), lambda i: (i, 0)),
                  pl.BlockSpec((TC, LANES), lambda i: (i, 0)),
                  _const_spec((1, D_MODEL)),
                  pl.BlockSpec((TC * ROW_TILES, LANES), lambda i: (i, 0)),
                  pl.BlockSpec((TC * ROW_TILES, LANES), lambda i: (t_total // TC + i, 0))],
        out_specs=[pl.BlockSpec((TC, D_MODEL), lambda i: (jnp.minimum(i, n_c_prompt - 1), 0)),
                   pl.BlockSpec((TC, D_MODEL), lambda i: (jnp.maximum(i - n_c_prompt, 0), 0))],
        out_shape=[jax.ShapeDtypeStruct((t_prompt, D_MODEL), f32), jax.ShapeDtypeStruct((t_sample, D_MODEL), f32)],
        compiler_params=cparams,
        name="moe_combine",
    )(x1, gcol, nfg, out_tok, out_tok)

    unplane = lambda a: a.transpose(0, 2, 1, 3).reshape(dec_batch, dec_seq, a.shape[-1])
    y_prompt = y_prompt.reshape(n_batch, seq, D_MODEL)
    y_sample = unplane(y_sample.reshape(n_sample_tiles, dec_seq, SEQ_BLK, D_MODEL))
    pool_state_prompt = pstate[None, :, CARRY_ROWS - POOL_STATE:, :]
    p_s = unplane(pplanes)
    pool_state_sample = jnp.concatenate([state_pool[0], p_s], axis=1)[None, :, -POOL_STATE:, :]
    chunk_v_sample = unplane(vplanes)[None]
    return (y_prompt, y_sample, pool_state_prompt, pool_state_sample, chunk_v_sample)
```

```python
import functools
import math

import jax
import jax.numpy as jnp
from jax import lax
from jax.experimental import pallas as pl
from jax.experimental.pallas import tpu as pltpu

D_MODEL = 1024
A_WIDTH = 512
B_WIDTH = 512
N_HEADS = 4
HEAD_DIM = 128
CHUNK = 128
POOL_WINDOWS = (2, 4, 8, 16)
POOL_STATE = 15
N_GROUPS = 4
EXPERTS_PER_GROUP = 8
N_EXPERTS = 32
TOP_K = 2
D_EXPERT = 512
EPS = 1e-6

SUBLANES = 8
LANES = 128
ROW_TILES = D_MODEL // LANES

TM = 256
TC = 512
TE = 512
SEQ_BLK = 32
N_ROUTER_ROWS = 40
CARRY_ROWS = 16
DMA_UNROLL = 32
VMEM_LIMIT = 48 * 1024 * 1024
N_PARTS = 2
EXPERT_VMEM_LIMIT = 62 * 1024 * 1024

_INV_SQRT2 = 1.0 / math.sqrt(2.0)


def _rmsnorm(x, g):
    r = lax.rsqrt(jnp.mean(x * x, axis=-1, keepdims=True) + EPS)
    return (x * r) * g


def _gelu(x):
    return 0.5 * x * (1.0 + lax.erf(x * _INV_SQRT2))


def _layernorm(x, g, b):
    mu = jnp.mean(x, axis=-1, keepdims=True)
    xc = x - mu
    var = jnp.mean(xc * xc, axis=-1, keepdims=True)
    return (xc * lax.rsqrt(var + EPS)) * g + b


def _row_slab(ref, s, n):
    return ref[pl.ds(s, n, stride=ROW_TILES), :]


def _pool_project(pooled, g, bw_ref, bscale_ref):
    lo, hi = g * HEAD_DIM, (g + 1) * HEAD_DIM
    hb = jnp.dot(pooled.astype(jnp.bfloat16), bw_ref[g], preferred_element_type=jnp.float32)
    return hb * bscale_ref[:, lo:hi]


def _prompt_mixers(j, u, v, p, aws_ref, abias_ref, bw_ref, bscale_ref, mix_ref, pcarry_ref, pstate_ref):
    tri = (lax.broadcasted_iota(jnp.int32, (CHUNK, CHUNK), 0)
           >= lax.broadcasted_iota(jnp.int32, (CHUNK, CHUNK), 1))
    vb = v.astype(jnp.bfloat16)
    for hd in range(N_HEADS):
        lo, hi = hd * HEAD_DIM, (hd + 1) * HEAD_DIM
        w = jnp.where(tri, aws_ref[hd], 0.0).astype(jnp.bfloat16)
        for c in range(TM // CHUNK):
            r0, r1 = c * CHUNK, (c + 1) * CHUNK
            z = jnp.dot(w, vb[r0:r1, lo:hi], preferred_element_type=jnp.float32) + abias_ref[:, lo:hi]
            mix_ref[r0:r1, lo:hi] = (u[r0:r1, lo:hi] * z).astype(jnp.bfloat16)

    head_pos = j * TM + lax.broadcasted_iota(jnp.int32, (CARRY_ROWS, LANES), 0)
    for g, w in enumerate(POOL_WINDOWS):
        lo, hi = g * HEAD_DIM, (g + 1) * HEAD_DIM
        pg = p[:, lo:hi]
        acc = jnp.concatenate([pcarry_ref[:, lo:hi], pg], axis=0)
        shift = 1
        while shift < w:
            acc = acc + pltpu.roll(acc, shift, 0)
            shift *= 2
        head = acc[CARRY_ROWS:2 * CARRY_ROWS, :] / jnp.minimum(head_pos + 1, w).astype(jnp.float32)
        mean = jnp.concatenate([head, acc[2 * CARRY_ROWS:, :] * (1.0 / w)], axis=0)
        pooled = mean - pg
        mix_ref[:, A_WIDTH + lo:A_WIDTH + hi] = _pool_project(pooled, g, bw_ref, bscale_ref).astype(jnp.bfloat16)
    tail = p[TM - CARRY_ROWS:, :]
    pcarry_ref[...] = tail
    pstate_ref[0] = tail


def _sample_mixers(u, v, p, aws_ref, abs_ref, st_ref, bw_ref, bscale_ref, mix_ref, pp_ref, vp_ref):
    n_pos = TM // SEQ_BLK
    for i in range(n_pos):
        vp_ref[0, i] = v[i * SEQ_BLK:(i + 1) * SEQ_BLK, :]
        pp_ref[0, i] = p[i * SEQ_BLK:(i + 1) * SEQ_BLK, :]

    for hd in range(N_HEADS):
        lo, hi = hd * HEAD_DIM, (hd + 1) * HEAD_DIM
        vplanes = [v[s * SEQ_BLK:(s + 1) * SEQ_BLK, lo:hi] for s in range(n_pos)]
        for i in range(n_pos):
            z = vplanes[0] * aws_ref[(hd * n_pos + i) * n_pos]
            for s in range(1, i + 1):
                z = z + vplanes[s] * aws_ref[(hd * n_pos + i) * n_pos + s]
            z = z + abs_ref[hd * n_pos + i]
            r0, r1 = i * SEQ_BLK, (i + 1) * SEQ_BLK
            mix_ref[r0:r1, lo:hi] = (u[r0:r1, lo:hi] * z).astype(jnp.bfloat16)

    for g, w in enumerate(POOL_WINDOWS):
        lo, hi = g * HEAD_DIM, (g + 1) * HEAD_DIM
        planes = [st_ref[0, k, :, lo:hi] for k in range(POOL_STATE)]
        planes += [p[i * SEQ_BLK:(i + 1) * SEQ_BLK, lo:hi] for i in range(n_pos)]
        pooled = []
        for i in range(n_pos):
            top = POOL_STATE + i
            s = planes[top - w + 1]
            for k in range(top - w + 2, top + 1):
                s = s + planes[k]
            pooled.append(s * (1.0 / w) - planes[top])
        pooled = jnp.concatenate(pooled, axis=0)
        mix_ref[:, A_WIDTH + lo:A_WIDTH + hi] = _pool_project(pooled, g, bw_ref, bscale_ref).astype(jnp.bfloat16)


def _route(h2, rwt_ref, rb_ref, su_ref, cnt_ref, meta_ref, gcol_ref):
    tm = h2.shape[0]
    h_hi = h2.astype(jnp.bfloat16)
    h_lo = (h2 - h_hi.astype(jnp.float32)).astype(jnp.bfloat16)
    s = (jnp.dot(h_hi, rwt_ref[...], preferred_element_type=jnp.float32)
         + jnp.dot(h_lo, rwt_ref[...], preferred_element_type=jnp.float32))
    st = s.T
    lt = st[0:N_ROUTER_ROWS, :] + st[N_ROUTER_ROWS:2 * N_ROUTER_ROWS, :] + rb_ref[...]
    row = lambda i: lt[i:i + 1, :]
    l1 = [row(i) for i in range(N_GROUPS)]
    m1 = jnp.maximum(jnp.maximum(l1[0], l1[1]), jnp.maximum(l1[2], l1[3]))
    grp = jnp.where(l1[0] == m1, 0, jnp.where(l1[1] == m1, 1, jnp.where(l1[2] == m1, 2, 3)))
    se = (jnp.exp(l1[0] - m1) + jnp.exp(l1[1] - m1)) + (jnp.exp(l1[2] - m1) + jnp.exp(l1[3] - m1))
    pg = 1.0 / se
    l2 = []
    for e in range(EXPERTS_PER_GROUP):
        c = [row(N_GROUPS + g * EXPERTS_PER_GROUP + e) for g in range(N_GROUPS)]
        l2.append(jnp.where(grp == 0, c[0], jnp.where(grp == 1, c[1], jnp.where(grp == 2, c[2], c[3]))))
    v0 = functools.reduce(jnp.maximum, l2)
    i0 = jnp.full_like(grp, EXPERTS_PER_GROUP - 1)
    for e in range(EXPERTS_PER_GROUP - 2, -1, -1):
        i0 = jnp.where(l2[e] == v0, e, i0)
    neg = jnp.float32(-jnp.inf)
    l2m = [jnp.where(i0 == e, neg, l2[e]) for e in range(EXPERTS_PER_GROUP)]
    v1 = functools.reduce(jnp.maximum, l2m)
    i1 = jnp.full_like(grp, EXPERTS_PER_GROUP - 1)
    for e in range(EXPERTS_PER_GROUP - 2, -1, -1):
        i1 = jnp.where((l2m[e] == v1) & (i0 != e), e, i1)
    d = jnp.exp(v1 - v0)
    g0 = pg / (1.0 + d)
    g1 = (pg * d) / (1.0 + d)
    e0 = grp * EXPERTS_PER_GROUP + i0
    e1 = grp * EXPERTS_PER_GROUP + i1

    eiota = lax.broadcasted_iota(jnp.int32, (N_EXPERTS, tm), 0)
    hit0 = eiota == e0
    hit1 = eiota == e1
    onehot = (hit0 | hit1).astype(jnp.bfloat16)
    prefix = jnp.dot(onehot, su_ref[...], preferred_element_type=jnp.float32)
    carry = cnt_ref[...]
    base = prefix + jnp.concatenate([carry] * (tm // LANES), axis=1)
    r0 = jnp.sum(jnp.where(hit0, base, 0.0), axis=0, keepdims=True)
    r1 = jnp.sum(jnp.where(hit1, base, 0.0), axis=0, keepdims=True)
    ones = jnp.ones((tm, LANES), jnp.bfloat16)
    cnt_ref[...] = carry + jnp.dot(onehot, ones, preferred_element_type=jnp.float32)

    meta_ref[0:1, :] = e0
    meta_ref[1:2, :] = e1
    meta_ref[2:3, :] = r0.astype(jnp.int32)
    meta_ref[3:4, :] = r1.astype(jnp.int32)
    meta_ref[4:8, :] = jnp.zeros((4, tm), jnp.int32)

    giota = lax.broadcasted_iota(jnp.int32, (LANES, tm), 0)
    gpad = jnp.where(giota == 0, g0, jnp.where(giota == 1, g1, 0.0))
    gcol_ref[...] = gpad.T


def _mixer_kernel(aws_s_ref, abs_s_ref,
                  x0_ref, xpn_ref, xsn_ref,
                  st_ref, n1g_ref, win_ref, ang_ref, anb_ref, aws_ref, abias_ref, bw_ref, bscale_ref,
                  wout_ref, n2g_ref, rwt_ref, rb_ref, su_ref,
                  x1_ref, h2_ref, meta_ref, gcol_ref, counts_ref, pstate_ref, pp_ref, vp_ref,
                  mix_ref, pcarry_ref, cnt_ref, x_ref, u_ref, v_ref, p_ref,
                  *, n_prompt_tiles, seq_tiles, part_tiles):
    i = pl.program_id(0)
    is_prompt = i < n_prompt_tiles
    j = i % seq_tiles

    def in_proj(x):
        h = _rmsnorm(x, n1g_ref[...]).astype(jnp.bfloat16)
        proj = jnp.dot(h, win_ref[...], preferred_element_type=jnp.float32)
        uv = _gelu(proj[:, : 2 * A_WIDTH])
        u_ref[...] = uv[:, :A_WIDTH]
        v_ref[...] = _layernorm(uv[:, A_WIDTH:], ang_ref[...], anb_ref[...])
        p_ref[...] = proj[:, 2 * A_WIDTH:]
        x_ref[...] = x

    @pl.when(i == 0)
    def _():
        in_proj(x0_ref[0])

    @pl.when(i % part_tiles == 0)
    def _():
        cnt_ref[...] = jnp.zeros_like(cnt_ref)

    @pl.when(is_prompt & (j == 0))
    def _():
        pcarry_ref[...] = jnp.zeros_like(pcarry_ref)

    @pl.when(is_prompt)
    def _():
        _prompt_mixers(j, u_ref[...], v_ref[...], p_ref[...], aws_ref, abias_ref, bw_ref, bscale_ref, mix_ref,
                       pcarry_ref, pstate_ref)

    @pl.when(jnp.logical_not(is_prompt))
    def _():
        _sample_mixers(u_ref[...], v_ref[...], p_ref[...], aws_s_ref, abs_s_ref, st_ref, bw_ref, bscale_ref, mix_ref,
                       pp_ref, vp_ref)

    x1 = x_ref[...] + jnp.dot(mix_ref[...], wout_ref[...], preferred_element_type=jnp.float32)
    x1_ref[...] = x1
    h2 = _rmsnorm(x1, n2g_ref[...])
    for s in range(ROW_TILES):
        h2_ref[pl.ds(s, TM, stride=ROW_TILES), :] = h2[:, s * LANES:(s + 1) * LANES]
    _route(h2, rwt_ref, rb_ref, su_ref, cnt_ref, meta_ref, gcol_ref)
    counts_ref[0] = cnt_ref[...]

    nxt = jnp.minimum(i + 1, pl.num_programs(0) - 1)
    in_proj(jnp.where(nxt < n_prompt_tiles, xpn_ref[0], xsn_ref[...]))


def _tables_kernel(counts_ref, meta_ref, asg_ref, tab_ref, dest_vmem, fill_vmem, *dest_smem, t_total):
    h = pl.program_id(0)
    t_part = meta_ref.shape[1]
    cnt = counts_ref[0]
    padded = jnp.floor((cnt + (TE - 1)) * (1.0 / TE)) * TE
    sub = lax.broadcasted_iota(jnp.int32, cnt.shape, 0)
    pad_end = padded
    shift = 1
    while shift < N_EXPERTS:
        pad_end = pad_end + jnp.where(sub >= shift, pltpu.roll(pad_end, shift, 0), 0.0)
        shift *= 2
    pad_start = pad_end - padded
    base = pad_start.astype(jnp.int32)

    for k in range(TOP_K):
        e = meta_ref[k:k + 1, :]
        d = meta_ref[TOP_K + k:TOP_K + k + 1, :]
        for ex in range(N_EXPERTS):
            row = jnp.concatenate([base[ex:ex + 1, :]] * (t_part // LANES), axis=1)
            d = d + jnp.where(e == ex, row, 0)
        dest_vmem[k] = d
    for k, dsm in enumerate(dest_smem):
        pltpu.sync_copy(dest_vmem.at[k], dsm)

    fill_vmem[...] = jnp.full(fill_vmem.shape, -1, jnp.int32)
    pltpu.sync_copy(fill_vmem, asg_ref)
    unroll = 16
    for k, dsm in enumerate(dest_smem):
        def invert(_, carry, dsm=dsm):
            t, a = carry
            for uu in range(unroll):
                asg_ref[0, 0, dsm[0, t + uu]] = a + uu
            return t + unroll, a + unroll
        lax.fori_loop(0, t_part // unroll, invert, (jnp.int32(0), k * t_total + h * t_part))

    n_valid = jnp.maximum(pad_end[N_EXPERTS - 1:, :] * (1.0 / TE), 1.0)
    tile = jnp.minimum(lax.broadcasted_iota(jnp.int32, (1, LANES), 1).astype(jnp.float32), n_valid - 1.0)
    tile_start = tile * TE
    tile_e = jnp.minimum(jnp.sum((pad_end <= tile_start).astype(jnp.float32), axis=0, keepdims=True),
                         N_EXPERTS - 1.0)
    is_e = sub.astype(jnp.float32) == tile_e
    seg_start = jnp.sum(jnp.where(is_e, pad_start, 0.0), axis=0, keepdims=True)
    seg_cnt = jnp.sum(jnp.where(is_e, cnt, 0.0), axis=0, keepdims=True)
    rows = jnp.clip(seg_cnt - (tile_start - seg_start), 0.0, TE)
    tab_ref[0, 0:1, :] = tile_e.astype(jnp.int32)
    tab_ref[0, 1:2, :] = rows.astype(jnp.int32)
    tab_ref[0, 2:3, :] = n_valid.astype(jnp.int32)
    tab_ref[0, 3:, :] = jnp.zeros((SUBLANES - 3, LANES), jnp.int32)


def _expert_kernel(te_ref, nv_ref, nrows_ref,
                   src_ref, src_next_ref, dst_ref,
                   h2_ref, w1_ref, w3_ref, w2_ref,
                   out_hbm,
                   xt0, xt1, ob0, ob1, w1b, w3b, w2b, ssem):
    i = pl.program_id(1)
    g = pl.program_id(0) * pl.num_programs(1) + i
    nv = nv_ref[pl.program_id(0)]
    slab = lambda r: pl.ds(pl.multiple_of(r * ROW_TILES, ROW_TILES), ROW_TILES)

    def gather(idx_ref, xt):
        for r in range(TE):
            xt[r * ROW_TILES:(r + 1) * ROW_TILES, :] = h2_ref[slab(idx_ref[0, 0, r]), :]

    def wait_scatter(tile, ob, other, sem):
        n = nrows_ref[tile] * ROW_TILES
        pltpu.make_async_copy(other.at[pl.ds(0, n), :], ob.at[pl.ds(0, n), :], sem).wait()

    def tile_body(xt, xt_next, ob, ob_other, sem, sem_other):
        @pl.when(i >= 2)
        def _():
            wait_scatter(g - 2, ob, ob_other, sem)

        gather(src_next_ref, xt_next)
        x = jnp.concatenate([_row_slab(xt, s, TE) for s in range(ROW_TILES)], axis=1).astype(jnp.bfloat16)
        a = jnp.dot(x, w1b[...], preferred_element_type=jnp.float32)
        b = jnp.dot(x, w3b[...], preferred_element_type=jnp.float32)
        h = (a * jax.nn.sigmoid(a)) * b
        o = jnp.dot(h.astype(jnp.bfloat16), w2b[...], preferred_element_type=jnp.float32)
        for s in range(ROW_TILES):
            ob[pl.ds(s, TE, stride=ROW_TILES), :] = o[:, s * LANES:(s + 1) * LANES]

        n = nrows_ref[g]
        n_full = n // DMA_UNROLL
        copy = lambda r: pltpu.make_async_copy(ob.at[slab(r), :], out_hbm.at[dst_ref[0, 0, r]], sem)

        def body(rb, carry):
            for uu in range(DMA_UNROLL):
                copy(rb * DMA_UNROLL + uu).start(priority=uu % 2)
            return carry
        lax.fori_loop(0, n_full, body, 0)

        def tail(r, carry):
            copy(r).start(priority=1)
            return carry
        lax.fori_loop(n_full * DMA_UNROLL, n, tail, 0)

        @pl.when(i == nv - 1)
        def _():
            @pl.when(i >= 1)
            def _():
                wait_scatter(g - 1, ob_other, ob, sem_other)
            wait_scatter(g, ob, ob_other, sem)

    @pl.when(i < nv)
    def _():
        @pl.when((i == 0) | (te_ref[g] != te_ref[jnp.maximum(g - 1, 0)]))
        def _():
            w1b[...] = w1_ref[0].astype(jnp.bfloat16)
            w3b[...] = w3_ref[0].astype(jnp.bfloat16)
            w2b[...] = w2_ref[0].astype(jnp.bfloat16)

        @pl.when(i == 0)
        def _():
            gather(src_ref, xt0)

        @pl.when(i % 2 == 0)
        def _():
            tile_body(xt0, xt1, ob0, ob1, ssem.at[0], ssem.at[1])

        @pl.when(i % 2 == 1)
        def _():
            tile_body(xt1, xt0, ob1, ob0, ssem.at[1], ssem.at[0])


def _combine_kernel(x1_ref, gcol_ref, nfg_ref, o0_ref, o1_ref, yp_ref, ys_ref, *, n_prompt_tiles):
    i = pl.program_id(0)
    o0 = jnp.concatenate([_row_slab(o0_ref, s, TC) for s in range(ROW_TILES)], axis=1)
    o1 = jnp.concatenate([_row_slab(o1_ref, s, TC) for s in range(ROW_TILES)], axis=1)
    g = gcol_ref[...]
    moe = g[:, 0:1] * o0 + g[:, 1:2] * o1
    y = _rmsnorm(x1_ref[...] + moe, nfg_ref[...])

    @pl.when(i < n_prompt_tiles)
    def _():
        yp_ref[...] = y

    @pl.when(i >= n_prompt_tiles)
    def _():
        ys_ref[...] = y


def _const_spec(shape):
    return pl.BlockSpec(shape, lambda *_: (0,) * len(shape))


def kernel(x_prompt, x_sample, state_pool, norm1_g, w_in, a_norm_g, a_norm_b, a_ws, a_bs, b_w, b_scale, w_out,
           norm2_g, r1_w, r1_b, r2_w, r2_b, exp_w1, exp_w3, exp_w2, normf_g):
    f32, bf16, i32 = jnp.float32, jnp.bfloat16, jnp.int32
    n_batch, seq, _ = x_prompt.shape
    dec_batch, dec_seq, _ = x_sample.shape
    assert norm1_g.shape[0] == 1 and seq % TM == 0 and TM % CHUNK == 0
    assert dec_seq * SEQ_BLK == TM and dec_batch % SEQ_BLK == 0 and dec_seq <= CHUNK
    t_prompt = n_batch * seq
    t_sample = dec_batch * dec_seq
    t_total = t_prompt + t_sample
    n_tok_tiles = t_total // TM
    n_prompt_tiles = t_prompt // TM
    n_sample_tiles = t_sample // TM
    seq_tiles = seq // TM
    n_assign = TOP_K * t_total
    plane_rows = t_total * ROW_TILES
    assert n_tok_tiles % N_PARTS == 0
    part_tiles = n_tok_tiles // N_PARTS
    t_part = part_tiles * TM
    n_exp_tiles = -(-(TOP_K * t_part + N_EXPERTS * (TE - 1)) // TE)
    p_rows = n_exp_tiles * TE

    n1g = norm1_g[0][None, :]
    n2g = norm2_g[0][None, :]
    nfg = normf_g[None, :]
    win = w_in[0].astype(bf16)
    wout = w_out[0].astype(bf16)
    ang = a_norm_g[0][None, :]
    anb = a_norm_b[0][None, :]
    bw = b_w[0].astype(bf16)
    bscale = b_scale[0][None, :]
    abias = jnp.repeat(a_bs[0][:, :CHUNK].T, HEAD_DIM, axis=1)
    rw = jnp.concatenate([r1_w[0], r2_w[0].transpose(1, 0, 2).reshape(D_MODEL, N_EXPERTS),
                          jnp.zeros((D_MODEL, N_ROUTER_ROWS - N_GROUPS - N_EXPERTS), f32)], axis=1)
    rw_hi = rw.astype(bf16)
    rw_lo = (rw - rw_hi.astype(f32)).astype(bf16)
    rwt = jnp.concatenate([rw_hi, rw_lo, jnp.zeros((D_MODEL, LANES - 2 * N_ROUTER_ROWS), bf16)], axis=1)
    rbias = jnp.concatenate([r1_b[0], r2_b[0].reshape(-1),
                             jnp.zeros((N_ROUTER_ROWS - N_GROUPS - N_EXPERTS,), f32)])
    rb = jnp.broadcast_to(rbias[:, None], (N_ROUTER_ROWS, TM))
    su = (jnp.arange(TM)[:, None] < jnp.arange(TM)[None, :]).astype(bf16)
    xs_planes = (x_sample.reshape(n_sample_tiles, SEQ_BLK, dec_seq, D_MODEL)
                 .transpose(0, 2, 1, 3).reshape(t_sample, D_MODEL))
    st_planes = state_pool[0].reshape(n_sample_tiles, SEQ_BLK, POOL_STATE, B_WIDTH).transpose(0, 2, 1, 3)
    aws_s = a_ws[0][:, :dec_seq, :dec_seq].reshape(-1)
    abs_s = a_bs[0][:, :dec_seq].reshape(-1)

    cparams = pltpu.CompilerParams(dimension_semantics=("arbitrary",), vmem_limit_bytes=VMEM_LIMIT)

    p_tile = lambda i: jnp.minimum(i, n_prompt_tiles - 1)
    s_tile = lambda i: jnp.maximum(i - n_prompt_tiles, 0)
    plane_shape = jax.ShapeDtypeStruct((n_sample_tiles, dec_seq, SEQ_BLK, A_WIDTH), f32)
    x1, h2, meta, gcol, counts, pstate, pplanes, vplanes = pl.pallas_call(
        functools.partial(_mixer_kernel, n_prompt_tiles=n_prompt_tiles, seq_tiles=seq_tiles, part_tiles=part_tiles),
        grid_spec=pltpu.PrefetchScalarGridSpec(
            num_scalar_prefetch=2,
            grid=(n_tok_tiles,),
            in_specs=[
                pl.BlockSpec((1, TM, D_MODEL), lambda i, *_: (0, 0, 0)),
                pl.BlockSpec((1, TM, D_MODEL),
                             lambda i, *_: (p_tile(i + 1) // seq_tiles, p_tile(i + 1) % seq_tiles, 0)),
                pl.BlockSpec((TM, D_MODEL), lambda i, *_: (jnp.minimum(s_tile(i + 1), n_sample_tiles - 1), 0)),
                pl.BlockSpec((1, POOL_STATE, SEQ_BLK, B_WIDTH), lambda i, *_: (s_tile(i), 0, 0, 0)),
                _const_spec((1, D_MODEL)), _const_spec((D_MODEL, 3 * A_WIDTH)), _const_spec((1, A_WIDTH)),
                _const_spec((1, A_WIDTH)), _const_spec((N_HEADS, CHUNK, CHUNK)), _const_spec((CHUNK, A_WIDTH)),
                _const_spec((N_GROUPS, HEAD_DIM, HEAD_DIM)), _const_spec((1, B_WIDTH)),
                _const_spec((D_MODEL, D_MODEL)), _const_spec((1, D_MODEL)), _const_spec((D_MODEL, LANES)),
                _const_spec((N_ROUTER_ROWS, TM)), _const_spec((TM, TM)),
            ],
            out_specs=[
                pl.BlockSpec((TM, D_MODEL), lambda i, *_: (i, 0)),
                pl.BlockSpec((TM * ROW_TILES, LANES), lambda i, *_: (i, 0)),
                pl.BlockSpec((SUBLANES, TM), lambda i, *_: (0, i)),
                pl.BlockSpec((TM, LANES), lambda i, *_: (i, 0)),
                pl.BlockSpec((1, N_EXPERTS, LANES), lambda i, *_: (i // part_tiles, 0, 0)),
                pl.BlockSpec((1, CARRY_ROWS, B_WIDTH), lambda i, *_: (p_tile(i) // seq_tiles, 0, 0)),
                pl.BlockSpec((1, dec_seq, SEQ_BLK, A_WIDTH), lambda i, *_: (s_tile(i), 0, 0, 0)),
                pl.BlockSpec((1, dec_seq, SEQ_BLK, A_WIDTH), lambda i, *_: (s_tile(i), 0, 0, 0)),
            ],
            scratch_shapes=[pltpu.VMEM((TM, D_MODEL), bf16), pltpu.VMEM((CARRY_ROWS, B_WIDTH), f32),
                            pltpu.VMEM((N_EXPERTS, LANES), f32), pltpu.VMEM((TM, D_MODEL), f32),
                            pltpu.VMEM((TM, A_WIDTH), f32), pltpu.VMEM((TM, A_WIDTH), f32),
                            pltpu.VMEM((TM, B_WIDTH), f32)],
        ),
        out_shape=[
            jax.ShapeDtypeStruct((t_total, D_MODEL), f32),
            jax.ShapeDtypeStruct((plane_rows, LANES), f32),
            jax.ShapeDtypeStruct((SUBLANES, t_total), i32),
            jax.ShapeDtypeStruct((t_total, LANES), f32),
            jax.ShapeDtypeStruct((N_PARTS, N_EXPERTS, LANES), f32),
            jax.ShapeDtypeStruct((n_batch, CARRY_ROWS, B_WIDTH), f32),
            plane_shape, plane_shape,
        ],
        compiler_params=cparams,
        name="mixer",
    )(aws_s, abs_s, x_prompt, x_prompt, xs_planes, st_planes, n1g, win, ang, anb, a_ws[0][:, :CHUNK, :CHUNK], abias,
      bw, bscale, wout, n2g, rwt, rb, su)

    assert n_exp_tiles <= LANES and t_part % LANES == 0
    asg, tab = pl.pallas_call(
        functools.partial(_tables_kernel, t_total=t_total),
        grid=(N_PARTS,),
        in_specs=[pl.BlockSpec((1, N_EXPERTS, LANES), lambda h: (h, 0, 0)),
                  pl.BlockSpec((SUBLANES, t_part), lambda h: (0, h))],
        out_specs=[pl.BlockSpec((1, 1, p_rows), lambda h: (h, 0, 0), memory_space=pltpu.SMEM),
                   pl.BlockSpec((1, SUBLANES, LANES), lambda h: (h, 0, 0))],
        scratch_shapes=[pltpu.VMEM((TOP_K, 1, t_part), i32), pltpu.VMEM((1, 1, p_rows), i32)]
        + [pltpu.SMEM((1, t_part), i32)] * TOP_K,
        out_shape=[jax.ShapeDtypeStruct((N_PARTS, 1, p_rows), i32),
                   jax.ShapeDtypeStruct((N_PARTS, SUBLANES, LANES), i32)],
        compiler_params=cparams,
        name="route_tables",
    )(counts, meta)
    tile_e = tab[:, 0, :n_exp_tiles].reshape(-1)
    tile_rows = tab[:, 1, :n_exp_tiles].reshape(-1)
    n_valid = tab[:, 2, 0]
    asg = asg.reshape(-1)
    row_part = jnp.arange(N_PARTS * p_rows, dtype=i32) // p_rows
    row_tok = jnp.where(asg >= t_total, asg - t_total, asg)
    row_src = jnp.where(asg >= 0, row_tok - row_part * t_part, 0).reshape(N_PARTS * n_exp_tiles, 1, TE)
    row_dst = jnp.maximum(asg, 0).reshape(N_PARTS * n_exp_tiles, 1, TE)

    flat = lambda h, i: h * n_exp_tiles + i
    cur_blk = lambda h, i, te, nv, nr: (flat(h, jnp.minimum(i, nv[h] - 1)), 0, 0)
    nxt_blk = lambda h, i, te, nv, nr: (flat(h, jnp.minimum(i + 1, nv[h] - 1)), 0, 0)
    w_blk = lambda h, i, te, nv, nr: (te[flat(h, i)], 0, 0)
    smem_blk = lambda imap: pl.BlockSpec((1, 1, TE), imap, memory_space=pltpu.SMEM)
    row_buf = pltpu.VMEM((TE * ROW_TILES, LANES), f32)
    out_tok = pl.pallas_call(
        _expert_kernel,
        grid_spec=pltpu.PrefetchScalarGridSpec(
            num_scalar_prefetch=3,
            grid=(N_PARTS, n_exp_tiles),
            in_specs=[smem_blk(cur_blk), smem_blk(nxt_blk), smem_blk(cur_blk),
                      pl.BlockSpec((t_part * ROW_TILES, LANES), lambda h, i, *_: (h, 0),
                                   pipeline_mode=pl.Buffered(1)),
                      pl.BlockSpec((1, D_MODEL, D_EXPERT), w_blk),
                      pl.BlockSpec((1, D_MODEL, D_EXPERT), w_blk),
                      pl.BlockSpec((1, D_EXPERT, D_MODEL), w_blk)],
            out_specs=pl.BlockSpec(memory_space=pl.ANY),
            scratch_shapes=[row_buf, row_buf, row_buf, row_buf,
                            pltpu.VMEM((D_MODEL, D_EXPERT), bf16), pltpu.VMEM((D_MODEL, D_EXPERT), bf16),
                            pltpu.VMEM((D_EXPERT, D_MODEL), bf16),
                            pltpu.SemaphoreType.DMA((2,))],
        ),
        out_shape=jax.ShapeDtypeStruct((TOP_K * t_total, ROW_TILES, LANES), f32),
        compiler_params=pltpu.CompilerParams(dimension_semantics=("arbitrary", "arbitrary"),
                                             vmem_limit_bytes=EXPERT_VMEM_LIMIT),
        name="moe_experts",
    )(tile_e, n_valid, tile_rows, row_src, row_src, row_dst, h2, exp_w1[0], exp_w3[0], exp_w2[0])
    out_tok = out_tok.reshape(TOP_K * plane_rows, LANES)

    assert t_prompt % TC == 0 and t_sample % TC == 0
    n_c_prompt = t_prompt // TC
    y_prompt, y_sample = pl.pallas_call(
        functools.partial(_combine_kernel, n_prompt_tiles=n_c_prompt),
        grid=(t_total // TC,),
        in_specs=[pl.BlockSpec((TC, D_MODEL), lambda i: (i, 0)),
                  pl.BlockSpec((TC, LANES), lambda i: (i, 0)),
                  _const_spec((1, D_MODEL)),
                  pl.BlockSpec((TC * ROW_TILES, LANES), lambda i: (i, 0)),
                  pl.BlockSpec((TC * ROW_TILES, LANES), lambda i: (t_total // TC + i, 0))],
        out_specs=[pl.BlockSpec((TC, D_MODEL), lambda i: (jnp.minimum(i, n_c_prompt - 1), 0)),
                   pl.BlockSpec((TC, D_MODEL), lambda i: (jnp.maximum(i - n_c_prompt, 0), 0))],
        out_shape=[jax.ShapeDtypeStruct((t_prompt, D_MODEL), f32), jax.ShapeDtypeStruct((t_sample, D_MODEL), f32)],
        compiler_params=cparams,
        name="moe_combine",
    )(x1, gcol, nfg, out_tok, out_tok)

    unplane = lambda a: a.transpose(0, 2, 1, 3).reshape(dec_batch, dec_seq, a.shape[-1])
    y_prompt = y_prompt.reshape(n_batch, seq, D_MODEL)
    y_sample = unplane(y_sample.reshape(n_sample_tiles, dec_seq, SEQ_BLK, D_MODEL))
    pool_state_prompt = pstate[None, :, CARRY_ROWS - POOL_STATE:, :]
    p_s = unplane(pplanes)
    pool_state_sample = jnp.concatenate([state_pool[0], p_s], axis=1)[None, :, -POOL_STATE:, :]
    chunk_v_sample = unplane(vplanes)[None]
    return (y_prompt, y_sample, pool_state_prompt, pool_state_sample, chunk_v_sample)
```

```python
import functools
import math

import jax
import jax.numpy as jnp
from jax import lax
from jax.experimental import pallas as pl
from jax.experimental.pallas import tpu as pltpu

D_MODEL = 1024
A_WIDTH = 512
B_WIDTH = 512
N_HEADS = 4
HEAD_DIM = 128
CHUNK = 128
POOL_WINDOWS = (2, 4, 8, 16)
POOL_STATE = 15
N_GROUPS = 4
EXPERTS_PER_GROUP = 8
N_EXPERTS = 32
TOP_K = 2
D_EXPERT = 512
EPS = 1e-6

SUBLANES = 8
LANES = 128
ROW_TILES = D_MODEL // LANES

TM = 256
TC = 512
TE = 512
SEQ_BLK = 32
N_ROUTER_ROWS = 40
CARRY_ROWS = 16
DMA_UNROLL = 32
VMEM_LIMIT = 48 * 1024 * 1024
N_PARTS = 2
EXPERT_VMEM_LIMIT = 62 * 1024 * 1024

_INV_SQRT2 = 1.0 / math.sqrt(2.0)


def _rmsnorm(x, g):
    r = lax.rsqrt(jnp.mean(x * x, axis=-1, keepdims=True) + EPS)
    return (x * r) * g


def _gelu(x):
    return 0.5 * x * (1.0 + lax.erf(x * _INV_SQRT2))


def _layernorm(x, g, b):
    mu = jnp.mean(x, axis=-1, keepdims=True)
    xc = x - mu
    var = jnp.mean(xc * xc, axis=-1, keepdims=True)
    return (xc * lax.rsqrt(var + EPS)) * g + b


def _row_slab(ref, s, n):
    return ref[pl.ds(s, n, stride=ROW_TILES), :]


def _pool_project(pooled, g, bw_ref, bscale_ref):
    lo, hi = g * HEAD_DIM, (g + 1) * HEAD_DIM
    hb = jnp.dot(pooled.astype(jnp.bfloat16), bw_ref[g], preferred_element_type=jnp.float32)
    return hb * bscale_ref[:, lo:hi]


def _prompt_mixers(j, u, v, p, aws_ref, abias_ref, bw_ref, bscale_ref, mix_ref, pcarry_ref, pstate_ref):
    tri = (lax.broadcasted_iota(jnp.int32, (CHUNK, CHUNK), 0)
           >= lax.broadcasted_iota(jnp.int32, (CHUNK, CHUNK), 1))
    vb = v.astype(jnp.bfloat16)
    for hd in range(N_HEADS):
        lo, hi = hd * HEAD_DIM, (hd + 1) * HEAD_DIM
        w = jnp.where(tri, aws_ref[hd], 0.0).astype(jnp.bfloat16)
        for c in range(TM // CHUNK):
            r0, r1 = c * CHUNK, (c + 1) * CHUNK
            z = jnp.dot(w, vb[r0:r1, lo:hi], preferred_element_type=jnp.float32) + abias_ref[:, lo:hi]
            mix_ref[r0:r1, lo:hi] = (u[r0:r1, lo:hi] * z).astype(jnp.bfloat16)

    head_pos = j * TM + lax.broadcasted_iota(jnp.int32, (CARRY_ROWS, LANES), 0)
    for g, w in enumerate(POOL_WINDOWS):
        lo, hi = g * HEAD_DIM, (g + 1) * HEAD_DIM
        pg = p[:, lo:hi]
        acc = jnp.concatenate([pcarry_ref[:, lo:hi], pg], axis=0)
        shift = 1
        while shift < w:
            acc = acc + pltpu.roll(acc, shift, 0)
            shift *= 2
        head = acc[CARRY_ROWS:2 * CARRY_ROWS, :] / jnp.minimum(head_pos + 1, w).astype(jnp.float32)
        mean = jnp.concatenate([head, acc[2 * CARRY_ROWS:, :] * (1.0 / w)], axis=0)
        pooled = mean - pg
        mix_ref[:, A_WIDTH + lo:A_WIDTH + hi] = _pool_project(pooled, g, bw_ref, bscale_ref).astype(jnp.bfloat16)
    tail = p[TM - CARRY_ROWS:, :]
    pcarry_ref[...] = tail
    pstate_ref[0] = tail


def _sample_mixers(u, v, p, aws_ref, abs_ref, st_ref, bw_ref, bscale_ref, mix_ref, pp_ref, vp_ref):
    n_pos = TM // SEQ_BLK
    for i in range(n_pos):
        vp_ref[0, i] = v[i * SEQ_BLK:(i + 1) * SEQ_BLK, :]
        pp_ref[0, i] = p[i * SEQ_BLK:(i + 1) * SEQ_BLK, :]

    for hd in range(N_HEADS):
        lo, hi = hd * HEAD_DIM, (hd + 1) * HEAD_DIM
        vplanes = [v[s * SEQ_BLK:(s + 1) * SEQ_BLK, lo:hi] for s in range(n_pos)]
        for i in range(n_pos):
            z = vplanes[0] * aws_ref[(hd * n_pos + i) * n_pos]
            for s in range(1, i + 1):
                z = z + vplanes[s] * aws_ref[(hd * n_pos + i) * n_pos + s]
            z = z + abs_ref[hd * n_pos + i]
            r0, r1 = i * SEQ_BLK, (i + 1) * SEQ_BLK
            mix_ref[r0:r1, lo:hi] = (u[r0:r1, lo:hi] * z).astype(jnp.bfloat16)

    for g, w in enumerate(POOL_WINDOWS):
        lo, hi = g * HEAD_DIM, (g + 1) * HEAD_DIM
        planes = [st_ref[0, k, :, lo:hi] for k in range(POOL_STATE)]
        planes += [p[i * SEQ_BLK:(i + 1) * SEQ_BLK, lo:hi] for i in range(n_pos)]
        pooled = []
        for i in range(n_pos):
            top = POOL_STATE + i
            s = planes[top - w + 1]
            for k in range(top - w + 2, top + 1):
                s = s + planes[k]
            pooled.append(s * (1.0 / w) - planes[top])
        pooled = jnp.concatenate(pooled, axis=0)
        mix_ref[:, A_WIDTH + lo:A_WIDTH + hi] = _pool_project(pooled, g, bw_ref, bscale_ref).astype(jnp.bfloat16)


def _route(h2, rwt_ref, rb_ref, su_ref, cnt_ref, meta_ref, gcol_ref):
    tm = h2.shape[0]
    h_hi = h2.astype(jnp.bfloat16)
    h_lo = (h2 - h_hi.astype(jnp.float32)).astype(jnp.bfloat16)
    s = (jnp.dot(h_hi, rwt_ref[...], preferred_element_type=jnp.float32)
         + jnp.dot(h_lo, rwt_ref[...], preferred_element_type=jnp.float32))
    st = s.T
    lt = st[0:N_ROUTER_ROWS, :] + st[N_ROUTER_ROWS:2 * N_ROUTER_ROWS, :] + rb_ref[...]
    row = lambda i: lt[i:i + 1, :]
    l1 = [row(i) for i in range(N_GROUPS)]
    m1 = jnp.maximum(jnp.maximum(l1[0], l1[1]), jnp.maximum(l1[2], l1[3]))
    grp = jnp.where(l1[0] == m1, 0, jnp.where(l1[1] == m1, 1, jnp.where(l1[2] == m1, 2, 3)))
    se = (jnp.exp(l1[0] - m1) + jnp.exp(l1[1] - m1)) + (jnp.exp(l1[2] - m1) + jnp.exp(l1[3] - m1))
    pg = 1.0 / se
    l2 = []
    for e in range(EXPERTS_PER_GROUP):
        c = [row(N_GROUPS + g * EXPERTS_PER_GROUP + e) for g in range(N_GROUPS)]
        l2.append(jnp.where(grp == 0, c[0], jnp.where(grp == 1, c[1], jnp.where(grp == 2, c[2], c[3]))))
    v0 = functools.reduce(jnp.maximum, l2)
    i0 = jnp.full_like(grp, EXPERTS_PER_GROUP - 1)
    for e in range(EXPERTS_PER_GROUP - 2, -1, -1):
        i0 = jnp.where(l2[e] == v0, e, i0)
    neg = jnp.float32(-jnp.inf)
    l2m = [jnp.where(i0 == e, neg, l2[e]) for e in range(EXPERTS_PER_GROUP)]
    v1 = functools.reduce(jnp.maximum, l2m)
    i1 = jnp.full_like(grp, EXPERTS_PER_GROUP - 1)
    for e in range(EXPERTS_PER_GROUP - 2, -1, -1):
        i1 = jnp.where((l2m[e] == v1) & (i0 != e), e, i1)
    d = jnp.exp(v1 - v0)
    g0 = pg / (1.0 + d)
    g1 = (pg * d) / (1.0 + d)
    e0 = grp * EXPERTS_PER_GROUP + i0
    e1 = grp * EXPERTS_PER_GROUP + i1

    eiota = lax.broadcasted_iota(jnp.int32, (N_EXPERTS, tm), 0)
    hit0 = eiota == e0
    hit1 = eiota == e1
    onehot = (hit0 | hit1).astype(jnp.bfloat16)
    prefix = jnp.dot(onehot, su_ref[...], preferred_element_type=jnp.float32)
    carry = cnt_ref[...]
    base = prefix + jnp.concatenate([carry] * (tm // LANES), axis=1)
    r0 = jnp.sum(jnp.where(hit0, base, 0.0), axis=0, keepdims=True)
    r1 = jnp.sum(jnp.where(hit1, base, 0.0), axis=0, keepdims=True)
    ones = jnp.ones((tm, LANES), jnp.bfloat16)
    cnt_ref[...] = carry + jnp.dot(onehot, ones, preferred_element_type=jnp.float32)

    meta_ref[0:1, :] = e0
    meta_ref[1:2, :] = e1
    meta_ref[2:3, :] = r0.astype(jnp.int32)
    meta_ref[3:4, :] = r1.astype(jnp.int32)
    meta_ref[4:8, :] = jnp.zeros((4, tm), jnp.int32)

    giota = lax.broadcasted_iota(jnp.int32, (LANES, tm), 0)
    gpad = jnp.where(giota == 0, g0, jnp.where(giota == 1, g1, 0.0))
    gcol_ref[...] = gpad.T


def _mixer_kernel(aws_s_ref, abs_s_ref,
                  x0_ref, xpn_ref, xsn_ref,
                  st_ref, n1g_ref, win_ref, ang_ref, anb_ref, aws_ref, abias_ref, bw_ref, bscale_ref,
                  wout_ref, n2g_ref, rwt_ref, rb_ref, su_ref,
                  x1_ref, h2_ref, meta_ref, gcol_ref, counts_ref, pstate_ref, pp_ref, vp_ref,
                  mix_ref, pcarry_ref, cnt_ref, x_ref, u_ref, v_ref, p_ref,
                  *, n_prompt_tiles, seq_tiles, part_tiles):
    i = pl.program_id(0)
    is_prompt = i < n_prompt_tiles
    j = i % seq_tiles

    def in_proj(x):
        h = _rmsnorm(x, n1g_ref[...]).astype(jnp.bfloat16)
        proj = jnp.dot(h, win_ref[...], preferred_element_type=jnp.float32)
        uv = _gelu(proj[:, : 2 * A_WIDTH])
        u_ref[...] = uv[:, :A_WIDTH]
        v_ref[...] = _layernorm(uv[:, A_WIDTH:], ang_ref[...], anb_ref[...])
        p_ref[...] = proj[:, 2 * A_WIDTH:]
        x_ref[...] = x

    @pl.when(i == 0)
    def _():
        in_proj(x0_ref[0])

    @pl.when(i % part_tiles == 0)
    def _():
        cnt_ref[...] = jnp.zeros_like(cnt_ref)

    @pl.when(is_prompt & (j == 0))
    def _():
        pcarry_ref[...] = jnp.zeros_like(pcarry_ref)

    @pl.when(is_prompt)
    def _():
        _prompt_mixers(j, u_ref[...], v_ref[...], p_ref[...], aws_ref, abias_ref, bw_ref, bscale_ref, mix_ref,
                       pcarry_ref, pstate_ref)

    @pl.when(jnp.logical_not(is_prompt))
    def _():
        _sample_mixers(u_ref[...], v_ref[...], p_ref[...], aws_s_ref, abs_s_ref, st_ref, bw_ref, bscale_ref, mix_ref,
                       pp_ref, vp_ref)

    x1 = x_ref[...] + jnp.dot(mix_ref[...], wout_ref[...], preferred_element_type=jnp.float32)
    x1_ref[...] = x1
    h2 = _rmsnorm(x1, n2g_ref[...])
    for s in range(ROW_TILES):
        h2_ref[pl.ds(s, TM, stride=ROW_TILES), :] = h2[:, s * LANES:(s + 1) * LANES]
    _route(h2, rwt_ref, rb_ref, su_ref, cnt_ref, meta_ref, gcol_ref)
    counts_ref[0] = cnt_ref[...]

    nxt = jnp.minimum(i + 1, pl.num_programs(0) - 1)
    in_proj(jnp.where(nxt < n_prompt_tiles, xpn_ref[0], xsn_ref[...]))


def _tables_kernel(counts_ref, meta_ref, asg_ref, tab_ref, dest_vmem, fill_vmem, *dest_smem, t_total):
    h = pl.program_id(0)
    t_part = meta_ref.shape[1]
    cnt = counts_ref[0]
    padded = jnp.floor((cnt + (TE - 1)) * (1.0 / TE)) * TE
    sub = lax.broadcasted_iota(jnp.int32, cnt.shape, 0)
    pad_end = padded
    shift = 1
    while shift < N_EXPERTS:
        pad_end = pad_end + jnp.where(sub >= shift, pltpu.roll(pad_end, shift, 0), 0.0)
        shift *= 2
    pad_start = pad_end - padded
    base = pad_start.astype(jnp.int32)

    for k in range(TOP_K):
        e = meta_ref[k:k + 1, :]
        d = meta_ref[TOP_K + k:TOP_K + k + 1, :]
        for ex in range(N_EXPERTS):
            row = jnp.concatenate([base[ex:ex + 1, :]] * (t_part // LANES), axis=1)
            d = d + jnp.where(e == ex, row, 0)
        dest_vmem[k] = d
    for k, dsm in enumerate(dest_smem):
        pltpu.sync_copy(dest_vmem.at[k], dsm)

    fill_vmem[...] = jnp.full(fill_vmem.shape, -1, jnp.int32)
    pltpu.sync_copy(fill_vmem, asg_ref)
    unroll = 16
    for k, dsm in enumerate(dest_smem):
        def invert(_, carry, dsm=dsm):
            t, a = carry
            for uu in range(unroll):
                asg_ref[0, 0, dsm[0, t + uu]] = a + uu
            return t + unroll, a + unroll
        lax.fori_loop(0, t_part // unroll, invert, (jnp.int32(0), k * t_total + h * t_part))

    n_valid = jnp.maximum(pad_end[N_EXPERTS - 1:, :] * (1.0 / TE), 1.0)
    tile = jnp.minimum(lax.broadcasted_iota(jnp.int32, (1, LANES), 1).astype(jnp.float32), n_valid - 1.0)
    tile_start = tile * TE
    tile_e = jnp.minimum(jnp.sum((pad_end <= tile_start).astype(jnp.float32), axis=0, keepdims=True),
                         N_EXPERTS - 1.0)
    is_e = sub.astype(jnp.float32) == tile_e
    seg_start = jnp.sum(jnp.where(is_e, pad_start, 0.0), axis=0, keepdims=True)
    seg_cnt = jnp.sum(jnp.where(is_e, cnt, 0.0), axis=0, keepdims=True)
    rows = jnp.clip(seg_cnt - (tile_start - seg_start), 0.0, TE)
    tab_ref[0, 0:1, :] = tile_e.astype(jnp.int32)
    tab_ref[0, 1:2, :] = rows.astype(jnp.int32)
    tab_ref[0, 2:3, :] = n_valid.astype(jnp.int32)
    tab_ref[0, 3:, :] = jnp.zeros((SUBLANES - 3, LANES), jnp.int32)


def _expert_kernel(te_ref, nv_ref, nrows_ref,
                   src_ref, src_next_ref, dst_ref,
                   h2_ref, w1_ref, w3_ref, w2_ref,
                   out_hbm,
                   xt0, xt1, ob0, ob1, w1b, w3b, w2b, ssem):
    i = pl.program_id(1)
    g = pl.program_id(0) * pl.num_programs(1) + i
    nv = nv_ref[pl.program_id(0)]
    slab = lambda r: pl.ds(pl.multiple_of(r * ROW_TILES, ROW_TILES), ROW_TILES)

    def gather(idx_ref, xt):
        for r in range(TE):
            xt[r * ROW_TILES:(r + 1) * ROW_TILES, :] = h2_ref[slab(idx_ref[0, 0, r]), :]

    def wait_scatter(tile, ob, other, sem):
        n = nrows_ref[tile] * ROW_TILES
        pltpu.make_async_copy(other.at[pl.ds(0, n), :], ob.at[pl.ds(0, n), :], sem).wait()

    def tile_body(xt, xt_next, ob, ob_other, sem, sem_other):
        @pl.when(i >= 2)
        def _():
            wait_scatter(g - 2, ob, ob_other, sem)

        def compute(m):
            gather(src_next_ref, xt_next)
            x = jnp.concatenate([_row_slab(xt, s, m) for s in range(ROW_TILES)], axis=1).astype(jnp.bfloat16)
            a = jnp.dot(x, w1b[...], preferred_element_type=jnp.float32)
            b = jnp.dot(x, w3b[...], preferred_element_type=jnp.float32)
            h = (a * jax.nn.sigmoid(a)) * b
            o = jnp.dot(h.astype(jnp.bfloat16), w2b[...], preferred_element_type=jnp.float32)
            for s in range(ROW_TILES):
                ob[pl.ds(s, m, stride=ROW_TILES), :] = o[:, s * LANES:(s + 1) * LANES]

        n = nrows_ref[g]

        @pl.when(n > TE // 2)
        def _():
            compute(TE)

        @pl.when(n <= TE // 2)
        def _():
            compute(TE // 2)

        n_full = n // DMA_UNROLL
        copy = lambda r: pltpu.make_async_copy(ob.at[slab(r), :], out_hbm.at[dst_ref[0, 0, r]], sem)

        def body(rb, carry):
            for uu in range(DMA_UNROLL):
                copy(rb * DMA_UNROLL + uu).start(priority=uu % 2)
            return carry
        lax.fori_loop(0, n_full, body, 0)

        def tail(r, carry):
            copy(r).start(priority=1)
            return carry
        lax.fori_loop(n_full * DMA_UNROLL, n, tail, 0)

        @pl.when(i == nv - 1)
        def _():
            @pl.when(i >= 1)
            def _():
                wait_scatter(g - 1, ob_other, ob, sem_other)
            wait_scatter(g, ob, ob_other, sem)

    @pl.when(i < nv)
    def _():
        @pl.when((i == 0) | (te_ref[g] != te_ref[jnp.maximum(g - 1, 0)]))
        def _():
            w1b[...] = w1_ref[0].astype(jnp.bfloat16)
            w3b[...] = w3_ref[0].astype(jnp.bfloat16)
            w2b[...] = w2_ref[0].astype(jnp.bfloat16)

        @pl.when(i == 0)
        def _():
            gather(src_ref, xt0)

        @pl.when(i % 2 == 0)
        def _():
            tile_body(xt0, xt1, ob0, ob1, ssem.at[0], ssem.at[1])

        @pl.when(i % 2 == 1)
        def _():
            tile_body(xt1, xt0, ob1, ob0, ssem.at[1], ssem.at[0])


def _combine_kernel(x1_ref, gcol_ref, nfg_ref, o0_ref, o1_ref, yp_ref, ys_ref, *, n_prompt_tiles):
    i = pl.program_id(0)
    o0 = jnp.concatenate([_row_slab(o0_ref, s, TC) for s in range(ROW_TILES)], axis=1)
    o1 = jnp.concatenate([_row_slab(o1_ref, s, TC) for s in range(ROW_TILES)], axis=1)
    g = gcol_ref[...]
    moe = g[:, 0:1] * o0 + g[:, 1:2] * o1
    y = _rmsnorm(x1_ref[...] + moe, nfg_ref[...])

    @pl.when(i < n_prompt_tiles)
    def _():
        yp_ref[...] = y

    @pl.when(i >= n_prompt_tiles)
    def _():
        ys_ref[...] = y


def _const_spec(shape):
    return pl.BlockSpec(shape, lambda *_: (0,) * len(shape))


def kernel(x_prompt, x_sample, state_pool, norm1_g, w_in, a_norm_g, a_norm_b, a_ws, a_bs, b_w, b_scale, w_out,
           norm2_g, r1_w, r1_b, r2_w, r2_b, exp_w1, exp_w3, exp_w2, normf_g):
    f32, bf16, i32 = jnp.float32, jnp.bfloat16, jnp.int32
    n_batch, seq, _ = x_prompt.shape
    dec_batch, dec_seq, _ = x_sample.shape
    assert norm1_g.shape[0] == 1 and seq % TM == 0 and TM % CHUNK == 0
    assert dec_seq * SEQ_BLK == TM and dec_batch % SEQ_BLK == 0 and dec_seq <= CHUNK
    t_prompt = n_batch * seq
    t_sample = dec_batch * dec_seq
    t_total = t_prompt + t_sample
    n_tok_tiles = t_total // TM
    n_prompt_tiles = t_prompt // TM
    n_sample_tiles = t_sample // TM
    seq_tiles = seq // TM
    n_assign = TOP_K * t_total
    plane_rows = t_total * ROW_TILES
    assert n_tok_tiles % N_PARTS == 0
    part_tiles = n_tok_tiles // N_PARTS
    t_part = part_tiles * TM
    n_exp_tiles = -(-(TOP_K * t_part + N_EXPERTS * (TE - 1)) // TE)
    p_rows = n_exp_tiles * TE

    n1g = norm1_g[0][None, :]
    n2g = norm2_g[0][None, :]
    nfg = normf_g[None, :]
    win = w_in[0].astype(bf16)
    wout = w_out[0].astype(bf16)
    ang = a_norm_g[0][None, :]
    anb = a_norm_b[0][None, :]
    bw = b_w[0].astype(bf16)
    bscale = b_scale[0][None, :]
    abias = jnp.repeat(a_bs[0][:, :CHUNK].T, HEAD_DIM, axis=1)
    rw = jnp.concatenate([r1_w[0], r2_w[0].transpose(1, 0, 2).reshape(D_MODEL, N_EXPERTS),
                          jnp.zeros((D_MODEL, N_ROUTER_ROWS - N_GROUPS - N_EXPERTS), f32)], axis=1)
    rw_hi = rw.astype(bf16)
    rw_lo = (rw - rw_hi.astype(f32)).astype(bf16)
    rwt = jnp.concatenate([rw_hi, rw_lo, jnp.zeros((D_MODEL, LANES - 2 * N_ROUTER_ROWS), bf16)], axis=1)
    rbias = jnp.concatenate([r1_b[0], r2_b[0].reshape(-1),
                             jnp.zeros((N_ROUTER_ROWS - N_GROUPS - N_EXPERTS,), f32)])
    rb = jnp.broadcast_to(rbias[:, None], (N_ROUTER_ROWS, TM))
    su = (jnp.arange(TM)[:, None] < jnp.arange(TM)[None, :]).astype(bf16)
    xs_planes = (x_sample.reshape(n_sample_tiles, SEQ_BLK, dec_seq, D_MODEL)
                 .transpose(0, 2, 1, 3).reshape(t_sample, D_MODEL))
    st_planes = state_pool[0].reshape(n_sample_tiles, SEQ_BLK, POOL_STATE, B_WIDTH).transpose(0, 2, 1, 3)
    aws_s = a_ws[0][:, :dec_seq, :dec_seq].reshape(-1)
    abs_s = a_bs[0][:, :dec_seq].reshape(-1)

    cparams = pltpu.CompilerParams(dimension_semantics=("arbitrary",), vmem_limit_bytes=VMEM_LIMIT)

    p_tile = lambda i: jnp.minimum(i, n_prompt_tiles - 1)
    s_tile = lambda i: jnp.maximum(i - n_prompt_tiles, 0)
    plane_shape = jax.ShapeDtypeStruct((n_sample_tiles, dec_seq, SEQ_BLK, A_WIDTH), f32)
    x1, h2, meta, gcol, counts, pstate, pplanes, vplanes = pl.pallas_call(
        functools.partial(_mixer_kernel, n_prompt_tiles=n_prompt_tiles, seq_tiles=seq_tiles, part_tiles=part_tiles),
        grid_spec=pltpu.PrefetchScalarGridSpec(
            num_scalar_prefetch=2,
            grid=(n_tok_tiles,),
            in_specs=[
                pl.BlockSpec((1, TM, D_MODEL), lambda i, *_: (0, 0, 0)),
                pl.BlockSpec((1, TM, D_MODEL),
                             lambda i, *_: (p_tile(i + 1) // seq_tiles, p_tile(i + 1) % seq_tiles, 0)),
                pl.BlockSpec((TM, D_MODEL), lambda i, *_: (jnp.minimum(s_tile(i + 1), n_sample_tiles - 1), 0)),
                pl.BlockSpec((1, POOL_STATE, SEQ_BLK, B_WIDTH), lambda i, *_: (s_tile(i), 0, 0, 0)),
                _const_spec((1, D_MODEL)), _const_spec((D_MODEL, 3 * A_WIDTH)), _const_spec((1, A_WIDTH)),
                _const_spec((1, A_WIDTH)), _const_spec((N_HEADS, CHUNK, CHUNK)), _const_spec((CHUNK, A_WIDTH)),
                _const_spec((N_GROUPS, HEAD_DIM, HEAD_DIM)), _const_spec((1, B_WIDTH)),
                _const_spec((D_MODEL, D_MODEL)), _const_spec((1, D_MODEL)), _const_spec((D_MODEL, LANES)),
                _const_spec((N_ROUTER_ROWS, TM)), _const_spec((TM, TM)),
            ],
            out_specs=[
                pl.BlockSpec((TM, D_MODEL), lambda i, *_: (i, 0)),
                pl.BlockSpec((TM * ROW_TILES, LANES), lambda i, *_: (i, 0)),
                pl.BlockSpec((SUBLANES, TM), lambda i, *_: (0, i)),
                pl.BlockSpec((TM, LANES), lambda i, *_: (i, 0)),
                pl.BlockSpec((1, N_EXPERTS, LANES), lambda i, *_: (i // part_tiles, 0, 0)),
                pl.BlockSpec((1, CARRY_ROWS, B_WIDTH), lambda i, *_: (p_tile(i) // seq_tiles, 0, 0)),
                pl.BlockSpec((1, dec_seq, SEQ_BLK, A_WIDTH), lambda i, *_: (s_tile(i), 0, 0, 0)),
                pl.BlockSpec((1, dec_seq, SEQ_BLK, A_WIDTH), lambda i, *_: (s_tile(i), 0, 0, 0)),
            ],
            scratch_shapes=[pltpu.VMEM((TM, D_MODEL), bf16), pltpu.VMEM((CARRY_ROWS, B_WIDTH), f32),
                            pltpu.VMEM((N_EXPERTS, LANES), f32), pltpu.VMEM((TM, D_MODEL), f32),
                            pltpu.VMEM((TM, A_WIDTH), f32), pltpu.VMEM((TM, A_WIDTH), f32),
                            pltpu.VMEM((TM, B_WIDTH), f32)],
        ),
        out_shape=[
            jax.ShapeDtypeStruct((t_total, D_MODEL), f32),
            jax.ShapeDtypeStruct((plane_rows, LANES), f32),
            jax.ShapeDtypeStruct((SUBLANES, t_total), i32),
            jax.ShapeDtypeStruct((t_total, LANES), f32),
            jax.ShapeDtypeStruct((N_PARTS, N_EXPERTS, LANES), f32),
            jax.ShapeDtypeStruct((n_batch, CARRY_ROWS, B_WIDTH), f32),
            plane_shape, plane_shape,
        ],
        compiler_params=cparams,
        name="mixer",
    )(aws_s, abs_s, x_prompt, x_prompt, xs_planes, st_planes, n1g, win, ang, anb, a_ws[0][:, :CHUNK, :CHUNK], abias,
      bw, bscale, wout, n2g, rwt, rb, su)

    assert n_exp_tiles <= LANES and t_part % LANES == 0
    asg, tab = pl.pallas_call(
        functools.partial(_tables_kernel, t_total=t_total),
        grid=(N_PARTS,),
        in_specs=[pl.BlockSpec((1, N_EXPERTS, LANES), lambda h: (h, 0, 0)),
                  pl.BlockSpec((SUBLANES, t_part), lambda h: (0, h))],
        out_specs=[pl.BlockSpec((1, 1, p_rows), lambda h: (h, 0, 0), memory_space=pltpu.SMEM),
                   pl.BlockSpec((1, SUBLANES, LANES), lambda h: (h, 0, 0))],
        scratch_shapes=[pltpu.VMEM((TOP_K, 1, t_part), i32), pltpu.VMEM((1, 1, p_rows), i32)]
        + [pltpu.SMEM((1, t_part), i32)] * TOP_K,
        out_shape=[jax.ShapeDtypeStruct((N_PARTS, 1, p_rows), i32),
                   jax.ShapeDtypeStruct((N_PARTS, SUBLANES, LANES), i32)],
        compiler_params=cparams,
        name="route_tables",
    )(counts, meta)
    tile_e = tab[:, 0, :n_exp_tiles].reshape(-1)
    tile_rows = tab[:, 1, :n_exp_tiles].reshape(-1)
    n_valid = tab[:, 2, 0]
    asg = asg.reshape(-1)
    row_part = jnp.arange(N_PARTS * p_rows, dtype=i32) // p_rows
    row_tok = jnp.where(asg >= t_total, asg - t_total, asg)
    row_src = jnp.where(asg >= 0, row_tok - row_part * t_part, 0).reshape(N_PARTS * n_exp_tiles, 1, TE)
    row_dst = jnp.maximum(asg, 0).reshape(N_PARTS * n_exp_tiles, 1, TE)

    flat = lambda h, i: h * n_exp_tiles + i
    cur_blk = lambda h, i, te, nv, nr: (flat(h, jnp.minimum(i, nv[h] - 1)), 0, 0)
    nxt_blk = lambda h, i, te, nv, nr: (flat(h, jnp.minimum(i + 1, nv[h] - 1)), 0, 0)
    w_blk = lambda h, i, te, nv, nr: (te[flat(h, i)], 0, 0)
    smem_blk = lambda imap: pl.BlockSpec((1, 1, TE), imap, memory_space=pltpu.SMEM)
    row_buf = pltpu.VMEM((TE * ROW_TILES, LANES), f32)
    out_tok = pl.pallas_call(
        _expert_kernel,
        grid_spec=pltpu.PrefetchScalarGridSpec(
            num_scalar_prefetch=3,
            grid=(N_PARTS, n_exp_tiles),
            in_specs=[smem_blk(cur_blk), smem_blk(nxt_blk), smem_blk(cur_blk),
                      pl.BlockSpec((t_part * ROW_TILES, LANES), lambda h, i, *_: (h, 0),
                                   pipeline_mode=pl.Buffered(1)),
                      pl.BlockSpec((1, D_MODEL, D_EXPERT), w_blk),
                      pl.BlockSpec((1, D_MODEL, D_EXPERT), w_blk),
                      pl.BlockSpec((1, D_EXPERT, D_MODEL), w_blk)],
            out_specs=pl.BlockSpec(memory_space=pl.ANY),
            scratch_shapes=[row_buf, row_buf, row_buf, row_buf,
                            pltpu.VMEM((D_MODEL, D_EXPERT), bf16), pltpu.VMEM((D_MODEL, D_EXPERT), bf16),
                            pltpu.VMEM((D_EXPERT, D_MODEL), bf16),
                            pltpu.SemaphoreType.DMA((2,))],
        ),
        out_shape=jax.ShapeDtypeStruct((TOP_K * t_total, ROW_TILES, LANES), f32),
        compiler_params=pltpu.CompilerParams(dimension_semantics=("arbitrary", "arbitrary"),
                                             vmem_limit_bytes=EXPERT_VMEM_LIMIT),
        name="moe_experts",
    )(tile_e, n_valid, tile_rows, row_src, row_src, row_dst, h2, exp_w1[0], exp_w3[0], exp_w2[0])
    out_tok = out_tok.reshape(TOP_K * plane_rows, LANES)

    assert t_prompt % TC == 0 and t_sample % TC == 0
    n_c_prompt = t_prompt // TC
    y_prompt, y_sample = pl.pallas_call(
        functools.partial(_combine_kernel, n_prompt_tiles=n_c_prompt),
        grid=(t_total // TC,),
        in_specs=[pl.BlockSpec((TC, D_MODEL), lambda i: (i, 0)),
                  pl.BlockSpec((TC, LANES), lambda i: (i, 0)),
                  _const_spec((1, D_MODEL)),
                  pl.BlockSpec((TC * ROW_TILES, LANES), lambda i: (i, 0)),
                  pl.BlockSpec((TC * ROW_TILES, LANES), lambda i: (t_total // TC + i, 0))],
        out_specs=[pl.BlockSpec((TC, D_MODEL), lambda i: (jnp.minimum(i, n_c_prompt - 1), 0)),
                   pl.BlockSpec((TC, D_MODEL), lambda i: (jnp.maximum(i - n_c_prompt, 0), 0))],
        out_shape=[jax.ShapeDtypeStruct((t_prompt, D_MODEL), f32), jax.ShapeDtypeStruct((t_sample, D_MODEL), f32)],
        compiler_params=cparams,
        name="moe_combine",
    )(x1, gcol, nfg, out_tok, out_tok)

    unplane = lambda a: a.transpose(0, 2, 1, 3).reshape(dec_batch, dec_seq, a.shape[-1])
    y_prompt = y_prompt.reshape(n_batch, seq, D_MODEL)
    y_sample = unplane(y_sample.reshape(n_sample_tiles, dec_seq, SEQ_BLK, D_MODEL))
    pool_state_prompt = pstate[None, :, CARRY_ROWS - POOL_STATE:, :]
    p_s = unplane(pplanes)
    pool_state_sample = jnp.concatenate([state_pool[0], p_s], axis=1)[None, :, -POOL_STATE:, :]
    chunk_v_sample = unplane(vplanes)[None]
    return (y_prompt, y_sample, pool_state_prompt, pool_state_sample, chunk_v_sample)
```

```python
import functools
import math

import jax
import jax.numpy as jnp
from jax import lax
from jax.experimental import pallas as pl
from jax.experimental.pallas import tpu as pltpu

D_MODEL = 1024
A_WIDTH = 512
B_WIDTH = 512
N_HEADS = 4
HEAD_DIM = 128
CHUNK = 128
POOL_WINDOWS = (2, 4, 8, 16)
POOL_STATE = 15
N_GROUPS = 4
EXPERTS_PER_GROUP = 8
N_EXPERTS = 32
TOP_K = 2
D_EXPERT = 512
EPS = 1e-6

SUBLANES = 8
LANES = 128
ROW_TILES = D_MODEL // LANES

TM = 256
TC = 512
TE = 512
SEQ_BLK = 32
N_ROUTER_ROWS = 40
CARRY_ROWS = 16
DMA_UNROLL = 32
VMEM_LIMIT = 48 * 1024 * 1024
N_PARTS = 2
EXPERT_VMEM_LIMIT = 62 * 1024 * 1024

_INV_SQRT2 = 1.0 / math.sqrt(2.0)


def _rmsnorm(x, g):
    r = lax.rsqrt(jnp.mean(x * x, axis=-1, keepdims=True) + EPS)
    return (x * r) * g


def _gelu(x):
    return 0.5 * x * (1.0 + lax.erf(x * _INV_SQRT2))


def _layernorm(x, g, b):
    mu = jnp.mean(x, axis=-1, keepdims=True)
    xc = x - mu
    var = jnp.mean(xc * xc, axis=-1, keepdims=True)
    return (xc * lax.rsqrt(var + EPS)) * g + b


def _row_slab(ref, s, n):
    return ref[pl.ds(s, n, stride=ROW_TILES), :]


def _pool_project(pooled, g, bw_ref, bscale_ref):
    lo, hi = g * HEAD_DIM, (g + 1) * HEAD_DIM
    hb = jnp.dot(pooled.astype(jnp.bfloat16), bw_ref[g], preferred_element_type=jnp.float32)
    return hb * bscale_ref[:, lo:hi]


def _prompt_mixers(j, u, v, p, aws_ref, abias_ref, bw_ref, bscale_ref, mix_ref, pcarry_ref, pstate_ref):
    tri = (lax.broadcasted_iota(jnp.int32, (CHUNK, CHUNK), 0)
           >= lax.broadcasted_iota(jnp.int32, (CHUNK, CHUNK), 1))
    vb = v.astype(jnp.bfloat16)
    for hd in range(N_HEADS):
        lo, hi = hd * HEAD_DIM, (hd + 1) * HEAD_DIM
        w = jnp.where(tri, aws_ref[hd], 0.0).astype(jnp.bfloat16)
        for c in range(TM // CHUNK):
            r0, r1 = c * CHUNK, (c + 1) * CHUNK
            z = jnp.dot(w, vb[r0:r1, lo:hi], preferred_element_type=jnp.float32) + abias_ref[:, lo:hi]
            mix_ref[r0:r1, lo:hi] = (u[r0:r1, lo:hi] * z).astype(jnp.bfloat16)

    head_pos = j * TM + lax.broadcasted_iota(jnp.int32, (CARRY_ROWS, LANES), 0)
    for g, w in enumerate(POOL_WINDOWS):
        lo, hi = g * HEAD_DIM, (g + 1) * HEAD_DIM
        pg = p[:, lo:hi]
        acc = jnp.concatenate([pcarry_ref[:, lo:hi], pg], axis=0)
        shift = 1
        while shift < w:
            acc = acc + pltpu.roll(acc, shift, 0)
            shift *= 2
        head = acc[CARRY_ROWS:2 * CARRY_ROWS, :] / jnp.minimum(head_pos + 1, w).astype(jnp.float32)
        mean = jnp.concatenate([head, acc[2 * CARRY_ROWS:, :] * (1.0 / w)], axis=0)
        pooled = mean - pg
        mix_ref[:, A_WIDTH + lo:A_WIDTH + hi] = _pool_project(pooled, g, bw_ref, bscale_ref).astype(jnp.bfloat16)
    tail = p[TM - CARRY_ROWS:, :]
    pcarry_ref[...] = tail
    pstate_ref[0] = tail


def _sample_mixers(u, v, p, aws_ref, abs_ref, st_ref, bw_ref, bscale_ref, mix_ref, pp_ref, vp_ref):
    n_pos = TM // SEQ_BLK
    for i in range(n_pos):
        vp_ref[0, i] = v[i * SEQ_BLK:(i + 1) * SEQ_BLK, :]
        pp_ref[0, i] = p[i * SEQ_BLK:(i + 1) * SEQ_BLK, :]

    for hd in range(N_HEADS):
        lo, hi = hd * HEAD_DIM, (hd + 1) * HEAD_DIM
        vplanes = [v[s * SEQ_BLK:(s + 1) * SEQ_BLK, lo:hi] for s in range(n_pos)]
        for i in range(n_pos):
            z = vplanes[0] * aws_ref[(hd * n_pos + i) * n_pos]
            for s in range(1, i + 1):
                z = z + vplanes[s] * aws_ref[(hd * n_pos + i) * n_pos + s]
            z = z + abs_ref[hd * n_pos + i]
            r0, r1 = i * SEQ_BLK, (i + 1) * SEQ_BLK
            mix_ref[r0:r1, lo:hi] = (u[r0:r1, lo:hi] * z).astype(jnp.bfloat16)

    for g, w in enumerate(POOL_WINDOWS):
        lo, hi = g * HEAD_DIM, (g + 1) * HEAD_DIM
        planes = [st_ref[0, k, :, lo:hi] for k in range(POOL_STATE)]
        planes += [p[i * SEQ_BLK:(i + 1) * SEQ_BLK, lo:hi] for i in range(n_pos)]
        pooled = []
        for i in range(n_pos):
            top = POOL_STATE + i
            s = planes[top - w + 1]
            for k in range(top - w + 2, top + 1):
                s = s + planes[k]
            pooled.append(s * (1.0 / w) - planes[top])
        pooled = jnp.concatenate(pooled, axis=0)
        mix_ref[:, A_WIDTH + lo:A_WIDTH + hi] = _pool_project(pooled, g, bw_ref, bscale_ref).astype(jnp.bfloat16)


def _route(h2, rwt_ref, rb_ref, su_ref, cnt_ref, meta_ref, gcol_ref):
    tm = h2.shape[0]
    h_hi = h2.astype(jnp.bfloat16)
    h_lo = (h2 - h_hi.astype(jnp.float32)).astype(jnp.bfloat16)
    s = (jnp.dot(h_hi, rwt_ref[...], preferred_element_type=jnp.float32)
         + jnp.dot(h_lo, rwt_ref[...], preferred_element_type=jnp.float32))
    st = s.T
    lt = st[0:N_ROUTER_ROWS, :] + st[N_ROUTER_ROWS:2 * N_ROUTER_ROWS, :] + rb_ref[...]
    row = lambda i: lt[i:i + 1, :]
    l1 = [row(i) for i in range(N_GROUPS)]
    m1 = jnp.maximum(jnp.maximum(l1[0], l1[1]), jnp.maximum(l1[2], l1[3]))
    grp = jnp.where(l1[0] == m1, 0, jnp.where(l1[1] == m1, 1, jnp.where(l1[2] == m1, 2, 3)))
    se = (jnp.exp(l1[0] - m1) + jnp.exp(l1[1] - m1)) + (jnp.exp(l1[2] - m1) + jnp.exp(l1[3] - m1))
    pg = 1.0 / se
    l2 = []
    for e in range(EXPERTS_PER_GROUP):
        c = [row(N_GROUPS + g * EXPERTS_PER_GROUP + e) for g in range(N_GROUPS)]
        l2.append(jnp.where(grp == 0, c[0], jnp.where(grp == 1, c[1], jnp.where(grp == 2, c[2], c[3]))))
    v0 = functools.reduce(jnp.maximum, l2)
    i0 = jnp.full_like(grp, EXPERTS_PER_GROUP - 1)
    for e in range(EXPERTS_PER_GROUP - 2, -1, -1):
        i0 = jnp.where(l2[e] == v0, e, i0)
    neg = jnp.float32(-jnp.inf)
    l2m = [jnp.where(i0 == e, neg, l2[e]) for e in range(EXPERTS_PER_GROUP)]
    v1 = functools.reduce(jnp.maximum, l2m)
    i1 = jnp.full_like(grp, EXPERTS_PER_GROUP - 1)
    for e in range(EXPERTS_PER_GROUP - 2, -1, -1):
        i1 = jnp.where((l2m[e] == v1) & (i0 != e), e, i1)
    d = jnp.exp(v1 - v0)
    g0 = pg / (1.0 + d)
    g1 = (pg * d) / (1.0 + d)
    e0 = grp * EXPERTS_PER_GROUP + i0
    e1 = grp * EXPERTS_PER_GROUP + i1

    eiota = lax.broadcasted_iota(jnp.int32, (N_EXPERTS, tm), 0)
    hit0 = eiota == e0
    hit1 = eiota == e1
    onehot = (hit0 | hit1).astype(jnp.bfloat16)
    prefix = jnp.dot(onehot, su_ref[...], preferred_element_type=jnp.float32)
    carry = cnt_ref[...]
    base = prefix + jnp.concatenate([carry] * (tm // LANES), axis=1)
    r0 = jnp.sum(jnp.where(hit0, base, 0.0), axis=0, keepdims=True)
    r1 = jnp.sum(jnp.where(hit1, base, 0.0), axis=0, keepdims=True)
    ones = jnp.ones((tm, LANES), jnp.bfloat16)
    cnt_ref[...] = carry + jnp.dot(onehot, ones, preferred_element_type=jnp.float32)

    meta_ref[0:1, :] = e0
    meta_ref[1:2, :] = e1
    meta_ref[2:3, :] = r0.astype(jnp.int32)
    meta_ref[3:4, :] = r1.astype(jnp.int32)
    meta_ref[4:8, :] = jnp.zeros((4, tm), jnp.int32)

    giota = lax.broadcasted_iota(jnp.int32, (LANES, tm), 0)
    gpad = jnp.where(giota == 0, g0, jnp.where(giota == 1, g1, 0.0))
    gcol_ref[...] = gpad.T


def _mixer_kernel(aws_s_ref, abs_s_ref,
                  x0_ref, xpn_ref, xsn_ref,
                  st_ref, n1g_ref, win_ref, ang_ref, anb_ref, aws_ref, abias_ref, bw_ref, bscale_ref,
                  wout_ref, n2g_ref, rwt_ref, rb_ref, su_ref,
                  x1_ref, h2_ref, meta_ref, gcol_ref, counts_ref, pstate_ref, pp_ref, vp_ref,
                  mix_ref, pcarry_ref, cnt_ref, x_ref, u_ref, v_ref, p_ref,
                  *, n_prompt_tiles, seq_tiles, part_tiles):
    i = pl.program_id(0)
    is_prompt = i < n_prompt_tiles
    j = i % seq_tiles

    def in_proj(x):
        h = _rmsnorm(x, n1g_ref[...]).astype(jnp.bfloat16)
        proj = jnp.dot(h, win_ref[...], preferred_element_type=jnp.float32)
        uv = _gelu(proj[:, : 2 * A_WIDTH])
        u_ref[...] = uv[:, :A_WIDTH]
        v_ref[...] = _layernorm(uv[:, A_WIDTH:], ang_ref[...], anb_ref[...])
        p_ref[...] = proj[:, 2 * A_WIDTH:]
        x_ref[...] = x

    @pl.when(i == 0)
    def _():
        in_proj(x0_ref[0])

    @pl.when(i % part_tiles == 0)
    def _():
        cnt_ref[...] = jnp.zeros_like(cnt_ref)

    @pl.when(is_prompt & (j == 0))
    def _():
        pcarry_ref[...] = jnp.zeros_like(pcarry_ref)

    @pl.when(is_prompt)
    def _():
        _prompt_mixers(j, u_ref[...], v_ref[...], p_ref[...], aws_ref, abias_ref, bw_ref, bscale_ref, mix_ref,
                       pcarry_ref, pstate_ref)

    @pl.when(jnp.logical_not(is_prompt))
    def _():
        _sample_mixers(u_ref[...], v_ref[...], p_ref[...], aws_s_ref, abs_s_ref, st_ref, bw_ref, bscale_ref, mix_ref,
                       pp_ref, vp_ref)

    x1 = x_ref[...] + jnp.dot(mix_ref[...], wout_ref[...], preferred_element_type=jnp.float32)
    x1_ref[...] = x1
    h2 = _rmsnorm(x1, n2g_ref[...])
    for s in range(ROW_TILES):
        h2_ref[pl.ds(s, TM, stride=ROW_TILES), :] = h2[:, s * LANES:(s + 1) * LANES]
    _route(h2, rwt_ref, rb_ref, su_ref, cnt_ref, meta_ref, gcol_ref)
    counts_ref[0] = cnt_ref[...]

    nxt = jnp.minimum(i + 1, pl.num_programs(0) - 1)
    in_proj(jnp.where(nxt < n_prompt_tiles, xpn_ref[0], xsn_ref[...]))


def _tables_kernel(counts_ref, meta_ref, asg_ref, tab_ref, dest_vmem, fill_vmem, *dest_smem, t_total):
    h = pl.program_id(0)
    t_part = meta_ref.shape[1]
    cnt = counts_ref[0]
    padded = jnp.floor((cnt + (TE - 1)) * (1.0 / TE)) * TE
    sub = lax.broadcasted_iota(jnp.int32, cnt.shape, 0)
    pad_end = padded
    shift = 1
    while shift < N_EXPERTS:
        pad_end = pad_end + jnp.where(sub >= shift, pltpu.roll(pad_end, shift, 0), 0.0)
        shift *= 2
    pad_start = pad_end - padded
    base = pad_start.astype(jnp.int32)

    for k in range(TOP_K):
        e = meta_ref[k:k + 1, :]
        d = meta_ref[TOP_K + k:TOP_K + k + 1, :]
        for ex in range(N_EXPERTS):
            row = jnp.concatenate([base[ex:ex + 1, :]] * (t_part // LANES), axis=1)
            d = d + jnp.where(e == ex, row, 0)
        dest_vmem[k] = d
    for k, dsm in enumerate(dest_smem):
        pltpu.sync_copy(dest_vmem.at[k], dsm)

    fill_vmem[...] = jnp.full(fill_vmem.shape, -1, jnp.int32)
    pltpu.sync_copy(fill_vmem, asg_ref)
    unroll = 16
    for k, dsm in enumerate(dest_smem):
        def invert(_, carry, dsm=dsm):
            t, a = carry
            for uu in range(unroll):
                asg_ref[0, 0, dsm[0, t + uu]] = a + uu
            return t + unroll, a + unroll
        lax.fori_loop(0, t_part // unroll, invert, (jnp.int32(0), k * t_total + h * t_part))

    n_valid = jnp.maximum(pad_end[N_EXPERTS - 1:, :] * (1.0 / TE), 1.0)
    tile = jnp.minimum(lax.broadcasted_iota(jnp.int32, (1, LANES), 1).astype(jnp.float32), n_valid - 1.0)
    tile_start = tile * TE
    tile_e = jnp.minimum(jnp.sum((pad_end <= tile_start).astype(jnp.float32), axis=0, keepdims=True),
                         N_EXPERTS - 1.0)
    is_e = sub.astype(jnp.float32) == tile_e
    seg_start = jnp.sum(jnp.where(is_e, pad_start, 0.0), axis=0, keepdims=True)
    seg_cnt = jnp.sum(jnp.where(is_e, cnt, 0.0), axis=0, keepdims=True)
    rows = jnp.clip(seg_cnt - (tile_start - seg_start), 0.0, TE)
    tab_ref[0, 0:1, :] = tile_e.astype(jnp.int32)
    tab_ref[0, 1:2, :] = rows.astype(jnp.int32)
    tab_ref[0, 2:3, :] = n_valid.astype(jnp.int32)
    tab_ref[0, 3:, :] = jnp.zeros((SUBLANES - 3, LANES), jnp.int32)


def _expert_kernel(te_ref, nv_ref, nrows_ref,
                   src_ref, src_next_ref, dst_ref,
                   h2_ref, w1_ref, w3_ref, w2_ref,
                   out_hbm,
                   xt0, xt1, ob0, ob1, ssem):
    i = pl.program_id(1)
    g = pl.program_id(0) * pl.num_programs(1) + i
    nv = nv_ref[pl.program_id(0)]
    slab = lambda r: pl.ds(pl.multiple_of(r * ROW_TILES, ROW_TILES), ROW_TILES)

    def gather(idx_ref, xt):
        for r in range(TE):
            xt[r * ROW_TILES:(r + 1) * ROW_TILES, :] = h2_ref[slab(idx_ref[0, 0, r]), :]

    def wait_scatter(tile, ob, other, sem):
        n = nrows_ref[tile] * ROW_TILES
        pltpu.make_async_copy(other.at[pl.ds(0, n), :], ob.at[pl.ds(0, n), :], sem).wait()

    def tile_body(xt, xt_next, ob, ob_other, sem, sem_other):
        @pl.when(i >= 2)
        def _():
            wait_scatter(g - 2, ob, ob_other, sem)

        def compute(m):
            gather(src_next_ref, xt_next)
            x = jnp.concatenate([_row_slab(xt, s, m) for s in range(ROW_TILES)], axis=1).astype(jnp.bfloat16)
            a = jnp.dot(x, w1_ref[0].astype(jnp.bfloat16), preferred_element_type=jnp.float32)
            b = jnp.dot(x, w3_ref[0].astype(jnp.bfloat16), preferred_element_type=jnp.float32)
            h = (a * jax.nn.sigmoid(a)) * b
            o = jnp.dot(h.astype(jnp.bfloat16), w2_ref[0].astype(jnp.bfloat16), preferred_element_type=jnp.float32)
            for s in range(ROW_TILES):
                ob[pl.ds(s, m, stride=ROW_TILES), :] = o[:, s * LANES:(s + 1) * LANES]

        n = nrows_ref[g]

        @pl.when(n > TE // 2)
        def _():
            compute(TE)

        @pl.when(n <= TE // 2)
        def _():
            compute(TE // 2)

        n_full = n // DMA_UNROLL
        copy = lambda r: pltpu.make_async_copy(ob.at[slab(r), :], out_hbm.at[dst_ref[0, 0, r]], sem)

        def body(rb, carry):
            for uu in range(DMA_UNROLL):
                copy(rb * DMA_UNROLL + uu).start(priority=uu % 2)
            return carry
        lax.fori_loop(0, n_full, body, 0)

        def tail(r, carry):
            copy(r).start(priority=1)
            return carry
        lax.fori_loop(n_full * DMA_UNROLL, n, tail, 0)

        @pl.when(i == nv - 1)
        def _():
            @pl.when(i >= 1)
            def _():
                wait_scatter(g - 1, ob_other, ob, sem_other)
            wait_scatter(g, ob, ob_other, sem)

    @pl.when(i < nv)
    def _():
        @pl.when(i == 0)
        def _():
            gather(src_ref, xt0)

        @pl.when(i % 2 == 0)
        def _():
            tile_body(xt0, xt1, ob0, ob1, ssem.at[0], ssem.at[1])

        @pl.when(i % 2 == 1)
        def _():
            tile_body(xt1, xt0, ob1, ob0, ssem.at[1], ssem.at[0])


def _combine_kernel(x1_ref, gcol_ref, nfg_ref, o0_ref, o1_ref, yp_ref, ys_ref, *, n_prompt_tiles):
    i = pl.program_id(0)
    o0 = jnp.concatenate([_row_slab(o0_ref, s, TC) for s in range(ROW_TILES)], axis=1)
    o1 = jnp.concatenate([_row_slab(o1_ref, s, TC) for s in range(ROW_TILES)], axis=1)
    g = gcol_ref[...]
    moe = g[:, 0:1] * o0 + g[:, 1:2] * o1
    y = _rmsnorm(x1_ref[...] + moe, nfg_ref[...])

    @pl.when(i < n_prompt_tiles)
    def _():
        yp_ref[...] = y

    @pl.when(i >= n_prompt_tiles)
    def _():
        ys_ref[...] = y


def _const_spec(shape):
    return pl.BlockSpec(shape, lambda *_: (0,) * len(shape))


def kernel(x_prompt, x_sample, state_pool, norm1_g, w_in, a_norm_g, a_norm_b, a_ws, a_bs, b_w, b_scale, w_out,
           norm2_g, r1_w, r1_b, r2_w, r2_b, exp_w1, exp_w3, exp_w2, normf_g):
    f32, bf16, i32 = jnp.float32, jnp.bfloat16, jnp.int32
    n_batch, seq, _ = x_prompt.shape
    dec_batch, dec_seq, _ = x_sample.shape
    assert norm1_g.shape[0] == 1 and seq % TM == 0 and TM % CHUNK == 0
    assert dec_seq * SEQ_BLK == TM and dec_batch % SEQ_BLK == 0 and dec_seq <= CHUNK
    t_prompt = n_batch * seq
    t_sample = dec_batch * dec_seq
    t_total = t_prompt + t_sample
    n_tok_tiles = t_total // TM
    n_prompt_tiles = t_prompt // TM
    n_sample_tiles = t_sample // TM
    seq_tiles = seq // TM
    n_assign = TOP_K * t_total
    plane_rows = t_total * ROW_TILES
    assert n_tok_tiles % N_PARTS == 0
    part_tiles = n_tok_tiles // N_PARTS
    t_part = part_tiles * TM
    n_exp_tiles = -(-(TOP_K * t_part + N_EXPERTS * (TE - 1)) // TE)
    p_rows = n_exp_tiles * TE

    n1g = norm1_g[0][None, :]
    n2g = norm2_g[0][None, :]
    nfg = normf_g[None, :]
    win = w_in[0].astype(bf16)
    wout = w_out[0].astype(bf16)
    ang = a_norm_g[0][None, :]
    anb = a_norm_b[0][None, :]
    bw = b_w[0].astype(bf16)
    bscale = b_scale[0][None, :]
    abias = jnp.repeat(a_bs[0][:, :CHUNK].T, HEAD_DIM, axis=1)
    rw = jnp.concatenate([r1_w[0], r2_w[0].transpose(1, 0, 2).reshape(D_MODEL, N_EXPERTS),
                          jnp.zeros((D_MODEL, N_ROUTER_ROWS - N_GROUPS - N_EXPERTS), f32)], axis=1)
    rw_hi = rw.astype(bf16)
    rw_lo = (rw - rw_hi.astype(f32)).astype(bf16)
    rwt = jnp.concatenate([rw_hi, rw_lo, jnp.zeros((D_MODEL, LANES - 2 * N_ROUTER_ROWS), bf16)], axis=1)
    rbias = jnp.concatenate([r1_b[0], r2_b[0].reshape(-1),
                             jnp.zeros((N_ROUTER_ROWS - N_GROUPS - N_EXPERTS,), f32)])
    rb = jnp.broadcast_to(rbias[:, None], (N_ROUTER_ROWS, TM))
    su = (jnp.arange(TM)[:, None] < jnp.arange(TM)[None, :]).astype(bf16)
    xs_planes = (x_sample.reshape(n_sample_tiles, SEQ_BLK, dec_seq, D_MODEL)
                 .transpose(0, 2, 1, 3).reshape(t_sample, D_MODEL))
    st_planes = state_pool[0].reshape(n_sample_tiles, SEQ_BLK, POOL_STATE, B_WIDTH).transpose(0, 2, 1, 3)
    aws_s = a_ws[0][:, :dec_seq, :dec_seq].reshape(-1)
    abs_s = a_bs[0][:, :dec_seq].reshape(-1)

    cparams = pltpu.CompilerParams(dimension_semantics=("arbitrary",), vmem_limit_bytes=VMEM_LIMIT)

    p_tile = lambda i: jnp.minimum(i, n_prompt_tiles - 1)
    s_tile = lambda i: jnp.maximum(i - n_prompt_tiles, 0)
    plane_shape = jax.ShapeDtypeStruct((n_sample_tiles, dec_seq, SEQ_BLK, A_WIDTH), f32)
    x1, h2, meta, gcol, counts, pstate, pplanes, vplanes = pl.pallas_call(
        functools.partial(_mixer_kernel, n_prompt_tiles=n_prompt_tiles, seq_tiles=seq_tiles, part_tiles=part_tiles),
        grid_spec=pltpu.PrefetchScalarGridSpec(
            num_scalar_prefetch=2,
            grid=(n_tok_tiles,),
            in_specs=[
                pl.BlockSpec((1, TM, D_MODEL), lambda i, *_: (0, 0, 0)),
                pl.BlockSpec((1, TM, D_MODEL),
                             lambda i, *_: (p_tile(i + 1) // seq_tiles, p_tile(i + 1) % seq_tiles, 0)),
                pl.BlockSpec((TM, D_MODEL), lambda i, *_: (jnp.minimum(s_tile(i + 1), n_sample_tiles - 1), 0)),
                pl.BlockSpec((1, POOL_STATE, SEQ_BLK, B_WIDTH), lambda i, *_: (s_tile(i), 0, 0, 0)),
                _const_spec((1, D_MODEL)), _const_spec((D_MODEL, 3 * A_WIDTH)), _const_spec((1, A_WIDTH)),
                _const_spec((1, A_WIDTH)), _const_spec((N_HEADS, CHUNK, CHUNK)), _const_spec((CHUNK, A_WIDTH)),
                _const_spec((N_GROUPS, HEAD_DIM, HEAD_DIM)), _const_spec((1, B_WIDTH)),
                _const_spec((D_MODEL, D_MODEL)), _const_spec((1, D_MODEL)), _const_spec((D_MODEL, LANES)),
                _const_spec((N_ROUTER_ROWS, TM)), _const_spec((TM, TM)),
            ],
            out_specs=[
                pl.BlockSpec((TM, D_MODEL), lambda i, *_: (i, 0)),
                pl.BlockSpec((TM * ROW_TILES, LANES), lambda i, *_: (i, 0)),
                pl.BlockSpec((SUBLANES, TM), lambda i, *_: (0, i)),
                pl.BlockSpec((TM, LANES), lambda i, *_: (i, 0)),
                pl.BlockSpec((1, N_EXPERTS, LANES), lambda i, *_: (i // part_tiles, 0, 0)),
                pl.BlockSpec((1, CARRY_ROWS, B_WIDTH), lambda i, *_: (p_tile(i) // seq_tiles, 0, 0)),
                pl.BlockSpec((1, dec_seq, SEQ_BLK, A_WIDTH), lambda i, *_: (s_tile(i), 0, 0, 0)),
                pl.BlockSpec((1, dec_seq, SEQ_BLK, A_WIDTH), lambda i, *_: (s_tile(i), 0, 0, 0)),
            ],
            scratch_shapes=[pltpu.VMEM((TM, D_MODEL), bf16), pltpu.VMEM((CARRY_ROWS, B_WIDTH), f32),
                            pltpu.VMEM((N_EXPERTS, LANES), f32), pltpu.VMEM((TM, D_MODEL), f32),
                            pltpu.VMEM((TM, A_WIDTH), f32), pltpu.VMEM((TM, A_WIDTH), f32),
                            pltpu.VMEM((TM, B_WIDTH), f32)],
        ),
        out_shape=[
            jax.ShapeDtypeStruct((t_total, D_MODEL), f32),
            jax.ShapeDtypeStruct((plane_rows, LANES), f32),
            jax.ShapeDtypeStruct((SUBLANES, t_total), i32),
            jax.ShapeDtypeStruct((t_total, LANES), f32),
            jax.ShapeDtypeStruct((N_PARTS, N_EXPERTS, LANES), f32),
            jax.ShapeDtypeStruct((n_batch, CARRY_ROWS, B_WIDTH), f32),
            plane_shape, plane_shape,
        ],
        compiler_params=cparams,
        name="mixer",
    )(aws_s, abs_s, x_prompt, x_prompt, xs_planes, st_planes, n1g, win, ang, anb, a_ws[0][:, :CHUNK, :CHUNK], abias,
      bw, bscale, wout, n2g, rwt, rb, su)

    assert n_exp_tiles <= LANES and t_part % LANES == 0
    asg, tab = pl.pallas_call(
        functools.partial(_tables_kernel, t_total=t_total),
        grid=(N_PARTS,),
        in_specs=[pl.BlockSpec((1, N_EXPERTS, LANES), lambda h: (h, 0, 0)),
                  pl.BlockSpec((SUBLANES, t_part), lambda h: (0, h))],
        out_specs=[pl.BlockSpec((1, 1, p_rows), lambda h: (h, 0, 0), memory_space=pltpu.SMEM),
                   pl.BlockSpec((1, SUBLANES, LANES), lambda h: (h, 0, 0))],
        scratch_shapes=[pltpu.VMEM((TOP_K, 1, t_part), i32), pltpu.VMEM((1, 1, p_rows), i32)]
        + [pltpu.SMEM((1, t_part), i32)] * TOP_K,
        out_shape=[jax.ShapeDtypeStruct((N_PARTS, 1, p_rows), i32),
                   jax.ShapeDtypeStruct((N_PARTS, SUBLANES, LANES), i32)],
        compiler_params=cparams,
        name="route_tables",
    )(counts, meta)
    tile_e = tab[:, 0, :n_exp_tiles].reshape(-1)
    tile_rows = tab[:, 1, :n_exp_tiles].reshape(-1)
    n_valid = tab[:, 2, 0]
    asg = asg.reshape(-1)
    row_part = jnp.arange(N_PARTS * p_rows, dtype=i32) // p_rows
    row_tok = jnp.where(asg >= t_total, asg - t_total, asg)
    row_src = jnp.where(asg >= 0, row_tok - row_part * t_part, 0).reshape(N_PARTS * n_exp_tiles, 1, TE)
    row_dst = jnp.maximum(asg, 0).reshape(N_PARTS * n_exp_tiles, 1, TE)

    flat = lambda h, i: h * n_exp_tiles + i
    cur_blk = lambda h, i, te, nv, nr: (flat(h, jnp.minimum(i, nv[h] - 1)), 0, 0)
    nxt_blk = lambda h, i, te, nv, nr: (flat(h, jnp.minimum(i + 1, nv[h] - 1)), 0, 0)
    w_blk = lambda h, i, te, nv, nr: (te[flat(h, i)], 0, 0)
    smem_blk = lambda imap: pl.BlockSpec((1, 1, TE), imap, memory_space=pltpu.SMEM)
    row_buf = pltpu.VMEM((TE * ROW_TILES, LANES), f32)
    out_tok = pl.pallas_call(
        _expert_kernel,
        grid_spec=pltpu.PrefetchScalarGridSpec(
            num_scalar_prefetch=3,
            grid=(N_PARTS, n_exp_tiles),
            in_specs=[smem_blk(cur_blk), smem_blk(nxt_blk), smem_blk(cur_blk),
                      pl.BlockSpec((t_part * ROW_TILES, LANES), lambda h, i, *_: (h, 0),
                                   pipeline_mode=pl.Buffered(1)),
                      pl.BlockSpec((1, D_MODEL, D_EXPERT), w_blk),
                      pl.BlockSpec((1, D_MODEL, D_EXPERT), w_blk),
                      pl.BlockSpec((1, D_EXPERT, D_MODEL), w_blk)],
            out_specs=pl.BlockSpec(memory_space=pl.ANY),
            scratch_shapes=[row_buf, row_buf, row_buf, row_buf, pltpu.SemaphoreType.DMA((2,))],
        ),
        out_shape=jax.ShapeDtypeStruct((TOP_K * t_total, ROW_TILES, LANES), f32),
        compiler_params=pltpu.CompilerParams(dimension_semantics=("arbitrary", "arbitrary"),
                                             vmem_limit_bytes=EXPERT_VMEM_LIMIT),
        name="moe_experts",
    )(tile_e, n_valid, tile_rows, row_src, row_src, row_dst, h2, exp_w1[0], exp_w3[0], exp_w2[0])
    out_tok = out_tok.reshape(TOP_K * plane_rows, LANES)

    assert t_prompt % TC == 0 and t_sample % TC == 0
    n_c_prompt = t_prompt // TC
    y_prompt, y_sample = pl.pallas_call(
        functools.partial(_combine_kernel, n_prompt_tiles=n_c_prompt),
        grid=(t_total // TC,),
        in_specs=[pl.BlockSpec((TC, D_MODEL), lambda i: (i, 0)),
                  pl.BlockSpec((TC, LANES), lambda i: (i, 0)),
                  _const_spec((1, D_MODEL)),
                  pl.BlockSpec((TC * ROW_TILES, LANES), lambda i: (i, 0)),
                  pl.BlockSpec((TC * ROW_TILES, LANES), lambda i: (t_total // TC + i, 0))],
        out_specs=[pl.BlockSpec((TC, D_MODEL), lambda i: (jnp.minimum(i, n_c_prompt - 1), 0)),
                   pl.BlockSpec((TC, D_MODEL), lambda i: (jnp.maximum(i - n_c_prompt, 0), 0))],
        out_shape=[jax.ShapeDtypeStruct((t_prompt, D_MODEL), f32), jax.ShapeDtypeStruct((t_sample, D_MODEL), f32)],
        compiler_params=cparams,
        name="moe_combine",
    )(x1, gcol, nfg, out_tok, out_tok)

    unplane = lambda a: a.transpose(0, 2, 1, 3).reshape(dec_batch, dec_seq, a.shape[-1])
    y_prompt = y_prompt.reshape(n_batch, seq, D_MODEL)
    y_sample = unplane(y_sample.reshape(n_sample_tiles, dec_seq, SEQ_BLK, D_MODEL))
    pool_state_prompt = pstate[None, :, CARRY_ROWS - POOL_STATE:, :]
    p_s = unplane(pplanes)
    pool_state_sample = jnp.concatenate([state_pool[0], p_s], axis=1)[None, :, -POOL_STATE:, :]
    chunk_v_sample = unplane(vplanes)[None]
    return (y_prompt, y_sample, pool_state_prompt, pool_state_sample, chunk_v_sample)
```

```python
import functools
import math

import jax
import jax.numpy as jnp
from jax import lax
from jax.experimental import pallas as pl
from jax.experimental.pallas import tpu as pltpu

D_MODEL = 1024
A_WIDTH = 512
B_WIDTH = 512
N_HEADS = 4
HEAD_DIM = 128
CHUNK = 128
POOL_WINDOWS = (2, 4, 8, 16)
POOL_STATE = 15
N_GROUPS = 4
EXPERTS_PER_GROUP = 8
N_EXPERTS = 32
TOP_K = 2
D_EXPERT = 512
EPS = 1e-6

SUBLANES = 8
LANES = 128
ROW_TILES = D_MODEL // LANES

TM = 256
TC = 512
TE = 512
SEQ_BLK = 32
N_ROUTER_ROWS = 40
CARRY_ROWS = 16
DMA_UNROLL = 32
VMEM_LIMIT = 48 * 1024 * 1024
N_PARTS = 2
EXPERT_VMEM_LIMIT = 62 * 1024 * 1024

_INV_SQRT2 = 1.0 / math.sqrt(2.0)


def _rmsnorm(x, g):
    r = lax.rsqrt(jnp.mean(x * x, axis=-1, keepdims=True) + EPS)
    return (x * r) * g


def _gelu(x):
    return 0.5 * x * (1.0 + lax.erf(x * _INV_SQRT2))


def _layernorm(x, g, b):
    mu = jnp.mean(x, axis=-1, keepdims=True)
    xc = x - mu
    var = jnp.mean(xc * xc, axis=-1, keepdims=True)
    return (xc * lax.rsqrt(var + EPS)) * g + b


def _row_slab(ref, s, n):
    return ref[pl.ds(s, n, stride=ROW_TILES), :]


def _pool_project(pooled, g, bw_ref, bscale_ref):
    lo, hi = g * HEAD_DIM, (g + 1) * HEAD_DIM
    hb = jnp.dot(pooled.astype(jnp.bfloat16), bw_ref[g], preferred_element_type=jnp.float32)
    return hb * bscale_ref[:, lo:hi]


def _prompt_mixers(j, u, v, p, aws_ref, abias_ref, bw_ref, bscale_ref, mix_ref, pcarry_ref, pstate_ref):
    tri = (lax.broadcasted_iota(jnp.int32, (CHUNK, CHUNK), 0)
           >= lax.broadcasted_iota(jnp.int32, (CHUNK, CHUNK), 1))
    vb = v.astype(jnp.bfloat16)
    for hd in range(N_HEADS):
        lo, hi = hd * HEAD_DIM, (hd + 1) * HEAD_DIM
        w = jnp.where(tri, aws_ref[hd], 0.0).astype(jnp.bfloat16)
        for c in range(TM // CHUNK):
            r0, r1 = c * CHUNK, (c + 1) * CHUNK
            z = jnp.dot(w, vb[r0:r1, lo:hi], preferred_element_type=jnp.float32) + abias_ref[:, lo:hi]
            mix_ref[r0:r1, lo:hi] = (u[r0:r1, lo:hi] * z).astype(jnp.bfloat16)

    head_pos = j * TM + lax.broadcasted_iota(jnp.int32, (CARRY_ROWS, LANES), 0)
    for g, w in enumerate(POOL_WINDOWS):
        lo, hi = g * HEAD_DIM, (g + 1) * HEAD_DIM
        pg = p[:, lo:hi]
        acc = jnp.concatenate([pcarry_ref[:, lo:hi], pg], axis=0)
        shift = 1
        while shift < w:
            acc = acc + pltpu.roll(acc, shift, 0)
            shift *= 2
        head = acc[CARRY_ROWS:2 * CARRY_ROWS, :] / jnp.minimum(head_pos + 1, w).astype(jnp.float32)
        mean = jnp.concatenate([head, acc[2 * CARRY_ROWS:, :] * (1.0 / w)], axis=0)
        pooled = mean - pg
        mix_ref[:, A_WIDTH + lo:A_WIDTH + hi] = _pool_project(pooled, g, bw_ref, bscale_ref).astype(jnp.bfloat16)
    tail = p[TM - CARRY_ROWS:, :]
    pcarry_ref[...] = tail
    pstate_ref[0] = tail


def _sample_mixers(u, v, p, aws_ref, abs_ref, st_ref, bw_ref, bscale_ref, mix_ref, pp_ref, vp_ref):
    n_pos = TM // SEQ_BLK
    for i in range(n_pos):
        vp_ref[0, i] = v[i * SEQ_BLK:(i + 1) * SEQ_BLK, :]
        pp_ref[0, i] = p[i * SEQ_BLK:(i + 1) * SEQ_BLK, :]

    for hd in range(N_HEADS):
        lo, hi = hd * HEAD_DIM, (hd + 1) * HEAD_DIM
        vplanes = [v[s * SEQ_BLK:(s + 1) * SEQ_BLK, lo:hi] for s in range(n_pos)]
        for i in range(n_pos):
            z = vplanes[0] * aws_ref[(hd * n_pos + i) * n_pos]
            for s in range(1, i + 1):
                z = z + vplanes[s] * aws_ref[(hd * n_pos + i) * n_pos + s]
            z = z + abs_ref[hd * n_pos + i]
            r0, r1 = i * SEQ_BLK, (i + 1) * SEQ_BLK
            mix_ref[r0:r1, lo:hi] = (u[r0:r1, lo:hi] * z).astype(jnp.bfloat16)

    for g, w in enumerate(POOL_WINDOWS):
        lo, hi = g * HEAD_DIM, (g + 1) * HEAD_DIM
        planes = [st_ref[0, k, :, lo:hi] for k in range(POOL_STATE)]
        planes += [p[i * SEQ_BLK:(i + 1) * SEQ_BLK, lo:hi] for i in range(n_pos)]
        pooled = []
        for i in range(n_pos):
            top = POOL_STATE + i
            s = planes[top - w + 1]
            for k in range(top - w + 2, top + 1):
                s = s + planes[k]
            pooled.append(s * (1.0 / w) - planes[top])
        pooled = jnp.concatenate(pooled, axis=0)
        mix_ref[:, A_WIDTH + lo:A_WIDTH + hi] = _pool_project(pooled, g, bw_ref, bscale_ref).astype(jnp.bfloat16)


def _route(h2, rwt_ref, rb_ref, su_ref, cnt_ref, meta_ref, gcol_ref):
    tm = h2.shape[0]
    h_hi = h2.astype(jnp.bfloat16)
    h_lo = (h2 - h_hi.astype(jnp.float32)).astype(jnp.bfloat16)
    s = (jnp.dot(h_hi, rwt_ref[...], preferred_element_type=jnp.float32)
         + jnp.dot(h_lo, rwt_ref[...], preferred_element_type=jnp.float32))
    st = s.T
    lt = st[0:N_ROUTER_ROWS, :] + st[N_ROUTER_ROWS:2 * N_ROUTER_ROWS, :] + rb_ref[...]
    row = lambda i: lt[i:i + 1, :]
    l1 = [row(i) for i in range(N_GROUPS)]
    m1 = jnp.maximum(jnp.maximum(l1[0], l1[1]), jnp.maximum(l1[2], l1[3]))
    grp = jnp.where(l1[0] == m1, 0, jnp.where(l1[1] == m1, 1, jnp.where(l1[2] == m1, 2, 3)))
    se = (jnp.exp(l1[0] - m1) + jnp.exp(l1[1] - m1)) + (jnp.exp(l1[2] - m1) + jnp.exp(l1[3] - m1))
    pg = 1.0 / se
    l2 = []
    for e in range(EXPERTS_PER_GROUP):
        c = [row(N_GROUPS + g * EXPERTS_PER_GROUP + e) for g in range(N_GROUPS)]
        l2.append(jnp.where(grp == 0, c[0], jnp.where(grp == 1, c[1], jnp.where(grp == 2, c[2], c[3]))))
    v0 = functools.reduce(jnp.maximum, l2)
    i0 = jnp.full_like(grp, EXPERTS_PER_GROUP - 1)
    for e in range(EXPERTS_PER_GROUP - 2, -1, -1):
        i0 = jnp.where(l2[e] == v0, e, i0)
    neg = jnp.float32(-jnp.inf)
    l2m = [jnp.where(i0 == e, neg, l2[e]) for e in range(EXPERTS_PER_GROUP)]
    v1 = functools.reduce(jnp.maximum, l2m)
    i1 = jnp.full_like(grp, EXPERTS_PER_GROUP - 1)
    for e in range(EXPERTS_PER_GROUP - 2, -1, -1):
        i1 = jnp.where((l2m[e] == v1) & (i0 != e), e, i1)
    d = jnp.exp(v1 - v0)
    g0 = pg / (1.0 + d)
    g1 = (pg * d) / (1.0 + d)
    e0 = grp * EXPERTS_PER_GROUP + i0
    e1 = grp * EXPERTS_PER_GROUP + i1

    eiota = lax.broadcasted_iota(jnp.int32, (N_EXPERTS, tm), 0)
    hit0 = eiota == e0
    hit1 = eiota == e1
    onehot = (hit0 | hit1).astype(jnp.bfloat16)
    prefix = jnp.dot(onehot, su_ref[...], preferred_element_type=jnp.float32)
    carry = cnt_ref[...]
    base = prefix + jnp.concatenate([carry] * (tm // LANES), axis=1)
    r0 = jnp.sum(jnp.where(hit0, base, 0.0), axis=0, keepdims=True)
    r1 = jnp.sum(jnp.where(hit1, base, 0.0), axis=0, keepdims=True)
    ones = jnp.ones((tm, LANES), jnp.bfloat16)
    cnt_ref[...] = carry + jnp.dot(onehot, ones, preferred_element_type=jnp.float32)

    meta_ref[0:1, :] = e0
    meta_ref[1:2, :] = e1
    meta_ref[2:3, :] = r0.astype(jnp.int32)
    meta_ref[3:4, :] = r1.astype(jnp.int32)
    meta_ref[4:8, :] = jnp.zeros((4, tm), jnp.int32)

    giota = lax.broadcasted_iota(jnp.int32, (LANES, tm), 0)
    gpad = jnp.where(giota == 0, g0, jnp.where(giota == 1, g1, 0.0))
    gcol_ref[...] = gpad.T


def _mixer_kernel(aws_s_ref, abs_s_ref,
                  x0_ref, xpn_ref, xsn_ref,
                  st_ref, n1g_ref, win_ref, ang_ref, anb_ref, aws_ref, abias_ref, bw_ref, bscale_ref,
                  wout_ref, n2g_ref, rwt_ref, rb_ref, su_ref,
                  x1_ref, h2_ref, meta_ref, gcol_ref, counts_ref, pstate_ref, pp_ref, vp_ref,
                  mix_ref, pcarry_ref, cnt_ref, x_ref, u_ref, v_ref, p_ref,
                  *, n_prompt_tiles, seq_tiles, part_tiles):
    i = pl.program_id(0)
    is_prompt = i < n_prompt_tiles
    j = i % seq_tiles

    def in_proj(x):
        h = _rmsnorm(x, n1g_ref[...]).astype(jnp.bfloat16)
        proj = jnp.dot(h, win_ref[...], preferred_element_type=jnp.float32)
        uv = _gelu(proj[:, : 2 * A_WIDTH])
        u_ref[...] = uv[:, :A_WIDTH]
        v_ref[...] = _layernorm(uv[:, A_WIDTH:], ang_ref[...], anb_ref[...])
        p_ref[...] = proj[:, 2 * A_WIDTH:]
        x_ref[...] = x

    @pl.when(i == 0)
    def _():
        in_proj(x0_ref[0])

    @pl.when(i % part_tiles == 0)
    def _():
        cnt_ref[...] = jnp.zeros_like(cnt_ref)

    @pl.when(is_prompt & (j == 0))
    def _():
        pcarry_ref[...] = jnp.zeros_like(pcarry_ref)

    def finish_tile():
        x1 = x_ref[...] + jnp.dot(mix_ref[...], wout_ref[...], preferred_element_type=jnp.float32)
        x1_ref[...] = x1
        h2 = _rmsnorm(x1, n2g_ref[...])
        for s in range(ROW_TILES):
            h2_ref[pl.ds(s, TM, stride=ROW_TILES), :] = h2[:, s * LANES:(s + 1) * LANES]
        _route(h2, rwt_ref, rb_ref, su_ref, cnt_ref, meta_ref, gcol_ref)
        counts_ref[0] = cnt_ref[...]
        nxt = jnp.minimum(i + 1, pl.num_programs(0) - 1)
        in_proj(jnp.where(nxt < n_prompt_tiles, xpn_ref[0], xsn_ref[...]))

    @pl.when(is_prompt)
    def _():
        _prompt_mixers(j, u_ref[...], v_ref[...], p_ref[...], aws_ref, abias_ref, bw_ref, bscale_ref, mix_ref,
                       pcarry_ref, pstate_ref)
        finish_tile()

    @pl.when(jnp.logical_not(is_prompt))
    def _():
        _sample_mixers(u_ref[...], v_ref[...], p_ref[...], aws_s_ref, abs_s_ref, st_ref, bw_ref, bscale_ref, mix_ref,
                       pp_ref, vp_ref)
        finish_tile()


def _tables_kernel(counts_ref, meta_ref, asg_ref, tab_ref, dest_vmem, fill_vmem, *dest_smem, t_total):
    h = pl.program_id(0)
    t_part = meta_ref.shape[1]
    cnt = counts_ref[0]
    padded = jnp.floor((cnt + (TE - 1)) * (1.0 / TE)) * TE
    sub = lax.broadcasted_iota(jnp.int32, cnt.shape, 0)
    pad_end = padded
    shift = 1
    while shift < N_EXPERTS:
        pad_end = pad_end + jnp.where(sub >= shift, pltpu.roll(pad_end, shift, 0), 0.0)
        shift *= 2
    pad_start = pad_end - padded
    base = pad_start.astype(jnp.int32)

    for k in range(TOP_K):
        e = meta_ref[k:k + 1, :]
        d = meta_ref[TOP_K + k:TOP_K + k + 1, :]
        for ex in range(N_EXPERTS):
            row = jnp.concatenate([base[ex:ex + 1, :]] * (t_part // LANES), axis=1)
            d = d + jnp.where(e == ex, row, 0)
        dest_vmem[k] = d
    for k, dsm in enumerate(dest_smem):
        pltpu.sync_copy(dest_vmem.at[k], dsm)

    fill_vmem[...] = jnp.full(fill_vmem.shape, -1, jnp.int32)
    pltpu.sync_copy(fill_vmem, asg_ref)
    unroll = 16
    for k, dsm in enumerate(dest_smem):
        def invert(_, carry, dsm=dsm):
            t, a = carry
            for uu in range(unroll):
                asg_ref[0, 0, dsm[0, t + uu]] = a + uu
            return t + unroll, a + unroll
        lax.fori_loop(0, t_part // unroll, invert, (jnp.int32(0), k * t_total + h * t_part))

    n_valid = jnp.maximum(pad_end[N_EXPERTS - 1:, :] * (1.0 / TE), 1.0)
    tile = jnp.minimum(lax.broadcasted_iota(jnp.int32, (1, LANES), 1).astype(jnp.float32), n_valid - 1.0)
    tile_start = tile * TE
    tile_e = jnp.minimum(jnp.sum((pad_end <= tile_start).astype(jnp.float32), axis=0, keepdims=True),
                         N_EXPERTS - 1.0)
    is_e = sub.astype(jnp.float32) == tile_e
    seg_start = jnp.sum(jnp.where(is_e, pad_start, 0.0), axis=0, keepdims=True)
    seg_cnt = jnp.sum(jnp.where(is_e, cnt, 0.0), axis=0, keepdims=True)
    rows = jnp.clip(seg_cnt - (tile_start - seg_start), 0.0, TE)
    tab_ref[0, 0:1, :] = tile_e.astype(jnp.int32)
    tab_ref[0, 1:2, :] = rows.astype(jnp.int32)
    tab_ref[0, 2:3, :] = n_valid.astype(jnp.int32)
    tab_ref[0, 3:, :] = jnp.zeros((SUBLANES - 3, LANES), jnp.int32)


def _expert_kernel(te_ref, nv_ref, nrows_ref,
                   src_ref, src_next_ref, dst_ref,
                   h2_ref, w1_ref, w3_ref, w2_ref,
                   out_hbm,
                   xt0, xt1, ob0, ob1, ssem):
    i = pl.program_id(1)
    g = pl.program_id(0) * pl.num_programs(1) + i
    nv = nv_ref[pl.program_id(0)]
    slab = lambda r: pl.ds(pl.multiple_of(r * ROW_TILES, ROW_TILES), ROW_TILES)

    def gather(idx_ref, xt):
        for r in range(TE):
            xt[r * ROW_TILES:(r + 1) * ROW_TILES, :] = h2_ref[slab(idx_ref[0, 0, r]), :]

    def wait_scatter(tile, ob, other, sem):
        n = nrows_ref[tile] * ROW_TILES
        pltpu.make_async_copy(other.at[pl.ds(0, n), :], ob.at[pl.ds(0, n), :], sem).wait()

    def tile_body(xt, xt_next, ob, ob_other, sem, sem_other):
        @pl.when(i >= 2)
        def _():
            wait_scatter(g - 2, ob, ob_other, sem)

        def compute(m):
            gather(src_next_ref, xt_next)
            x = jnp.concatenate([_row_slab(xt, s, m) for s in range(ROW_TILES)], axis=1).astype(jnp.bfloat16)
            a = jnp.dot(x, w1_ref[0].astype(jnp.bfloat16), preferred_element_type=jnp.float32)
            b = jnp.dot(x, w3_ref[0].astype(jnp.bfloat16), preferred_element_type=jnp.float32)
            h = (a * jax.nn.sigmoid(a)) * b
            o = jnp.dot(h.astype(jnp.bfloat16), w2_ref[0].astype(jnp.bfloat16), preferred_element_type=jnp.float32)
            for s in range(ROW_TILES):
                ob[pl.ds(s, m, stride=ROW_TILES), :] = o[:, s * LANES:(s + 1) * LANES]

        n = nrows_ref[g]

        @pl.when(n > TE // 2)
        def _():
            compute(TE)

        @pl.when(n <= TE // 2)
        def _():
            compute(TE // 2)

        n_full = n // DMA_UNROLL
        copy = lambda r: pltpu.make_async_copy(ob.at[slab(r), :], out_hbm.at[dst_ref[0, 0, r]], sem)

        def body(rb, carry):
            for uu in range(DMA_UNROLL):
                copy(rb * DMA_UNROLL + uu).start(priority=uu % 2)
            return carry
        lax.fori_loop(0, n_full, body, 0)

        def tail(r, carry):
            copy(r).start(priority=1)
            return carry
        lax.fori_loop(n_full * DMA_UNROLL, n, tail, 0)

        @pl.when(i == nv - 1)
        def _():
            @pl.when(i >= 1)
            def _():
                wait_scatter(g - 1, ob_other, ob, sem_other)
            wait_scatter(g, ob, ob_other, sem)

    @pl.when(i < nv)
    def _():
        @pl.when(i == 0)
        def _():
            gather(src_ref, xt0)

        @pl.when(i % 2 == 0)
        def _():
            tile_body(xt0, xt1, ob0, ob1, ssem.at[0], ssem.at[1])

        @pl.when(i % 2 == 1)
        def _():
            tile_body(xt1, xt0, ob1, ob0, ssem.at[1], ssem.at[0])


def _combine_kernel(x1_ref, gcol_ref, nfg_ref, o0_ref, o1_ref, yp_ref, ys_ref, *, n_prompt_tiles):
    i = pl.program_id(0)
    o0 = jnp.concatenate([_row_slab(o0_ref, s, TC) for s in range(ROW_TILES)], axis=1)
    o1 = jnp.concatenate([_row_slab(o1_ref, s, TC) for s in range(ROW_TILES)], axis=1)
    g = gcol_ref[...]
    moe = g[:, 0:1] * o0 + g[:, 1:2] * o1
    y = _rmsnorm(x1_ref[...] + moe, nfg_ref[...])

    @pl.when(i < n_prompt_tiles)
    def _():
        yp_ref[...] = y

    @pl.when(i >= n_prompt_tiles)
    def _():
        ys_ref[...] = y


def _const_spec(shape):
    return pl.BlockSpec(shape, lambda *_: (0,) * len(shape))


def kernel(x_prompt, x_sample, state_pool, norm1_g, w_in, a_norm_g, a_norm_b, a_ws, a_bs, b_w, b_scale, w_out,
           norm2_g, r1_w, r1_b, r2_w, r2_b, exp_w1, exp_w3, exp_w2, normf_g):
    f32, bf16, i32 = jnp.float32, jnp.bfloat16, jnp.int32
    n_batch, seq, _ = x_prompt.shape
    dec_batch, dec_seq, _ = x_sample.shape
    assert norm1_g.shape[0] == 1 and seq % TM == 0 and TM % CHUNK == 0
    assert dec_seq * SEQ_BLK == TM and dec_batch % SEQ_BLK == 0 and dec_seq <= CHUNK
    t_prompt = n_batch * seq
    t_sample = dec_batch * dec_seq
    t_total = t_prompt + t_sample
    n_tok_tiles = t_total // TM
    n_prompt_tiles = t_prompt // TM
    n_sample_tiles = t_sample // TM
    seq_tiles = seq // TM
    n_assign = TOP_K * t_total
    plane_rows = t_total * ROW_TILES
    assert n_tok_tiles % N_PARTS == 0
    part_tiles = n_tok_tiles // N_PARTS
    t_part = part_tiles * TM
    n_exp_tiles = -(-(TOP_K * t_part + N_EXPERTS * (TE - 1)) // TE)
    p_rows = n_exp_tiles * TE

    n1g = norm1_g[0][None, :]
    n2g = norm2_g[0][None, :]
    nfg = normf_g[None, :]
    win = w_in[0].astype(bf16)
    wout = w_out[0].astype(bf16)
    ang = a_norm_g[0][None, :]
    anb = a_norm_b[0][None, :]
    bw = b_w[0].astype(bf16)
    bscale = b_scale[0][None, :]
    abias = jnp.repeat(a_bs[0][:, :CHUNK].T, HEAD_DIM, axis=1)
    rw = jnp.concatenate([r1_w[0], r2_w[0].transpose(1, 0, 2).reshape(D_MODEL, N_EXPERTS),
                          jnp.zeros((D_MODEL, N_ROUTER_ROWS - N_GROUPS - N_EXPERTS), f32)], axis=1)
    rw_hi = rw.astype(bf16)
    rw_lo = (rw - rw_hi.astype(f32)).astype(bf16)
    rwt = jnp.concatenate([rw_hi, rw_lo, jnp.zeros((D_MODEL, LANES - 2 * N_ROUTER_ROWS), bf16)], axis=1)
    rbias = jnp.concatenate([r1_b[0], r2_b[0].reshape(-1),
                             jnp.zeros((N_ROUTER_ROWS - N_GROUPS - N_EXPERTS,), f32)])
    rb = jnp.broadcast_to(rbias[:, None], (N_ROUTER_ROWS, TM))
    su = (jnp.arange(TM)[:, None] < jnp.arange(TM)[None, :]).astype(bf16)
    xs_planes = (x_sample.reshape(n_sample_tiles, SEQ_BLK, dec_seq, D_MODEL)
                 .transpose(0, 2, 1, 3).reshape(t_sample, D_MODEL))
    st_planes = state_pool[0].reshape(n_sample_tiles, SEQ_BLK, POOL_STATE, B_WIDTH).transpose(0, 2, 1, 3)
    aws_s = a_ws[0][:, :dec_seq, :dec_seq].reshape(-1)
    abs_s = a_bs[0][:, :dec_seq].reshape(-1)

    cparams = pltpu.CompilerParams(dimension_semantics=("arbitrary",), vmem_limit_bytes=VMEM_LIMIT)

    p_tile = lambda i: jnp.minimum(i, n_prompt_tiles - 1)
    s_tile = lambda i: jnp.maximum(i - n_prompt_tiles, 0)
    plane_shape = jax.ShapeDtypeStruct((n_sample_tiles, dec_seq, SEQ_BLK, A_WIDTH), f32)
    x1, h2, meta, gcol, counts, pstate, pplanes, vplanes = pl.pallas_call(
        functools.partial(_mixer_kernel, n_prompt_tiles=n_prompt_tiles, seq_tiles=seq_tiles, part_tiles=part_tiles),
        grid_spec=pltpu.PrefetchScalarGridSpec(
            num_scalar_prefetch=2,
            grid=(n_tok_tiles,),
            in_specs=[
                pl.BlockSpec((1, TM, D_MODEL), lambda i, *_: (0, 0, 0)),
                pl.BlockSpec((1, TM, D_MODEL),
                             lambda i, *_: (p_tile(i + 1) // seq_tiles, p_tile(i + 1) % seq_tiles, 0)),
                pl.BlockSpec((TM, D_MODEL), lambda i, *_: (jnp.minimum(s_tile(i + 1), n_sample_tiles - 1), 0)),
                pl.BlockSpec((1, POOL_STATE, SEQ_BLK, B_WIDTH), lambda i, *_: (s_tile(i), 0, 0, 0)),
                _const_spec((1, D_MODEL)), _const_spec((D_MODEL, 3 * A_WIDTH)), _const_spec((1, A_WIDTH)),
                _const_spec((1, A_WIDTH)), _const_spec((N_HEADS, CHUNK, CHUNK)), _const_spec((CHUNK, A_WIDTH)),
                _const_spec((N_GROUPS, HEAD_DIM, HEAD_DIM)), _const_spec((1, B_WIDTH)),
                _const_spec((D_MODEL, D_MODEL)), _const_spec((1, D_MODEL)), _const_spec((D_MODEL, LANES)),
                _const_spec((N_ROUTER_ROWS, TM)), _const_spec((TM, TM)),
            ],
            out_specs=[
                pl.BlockSpec((TM, D_MODEL), lambda i, *_: (i, 0)),
                pl.BlockSpec((TM * ROW_TILES, LANES), lambda i, *_: (i, 0)),
                pl.BlockSpec((SUBLANES, TM), lambda i, *_: (0, i)),
                pl.BlockSpec((TM, LANES), lambda i, *_: (i, 0)),
                pl.BlockSpec((1, N_EXPERTS, LANES), lambda i, *_: (i // part_tiles, 0, 0)),
                pl.BlockSpec((1, CARRY_ROWS, B_WIDTH), lambda i, *_: (p_tile(i) // seq_tiles, 0, 0)),
                pl.BlockSpec((1, dec_seq, SEQ_BLK, A_WIDTH), lambda i, *_: (s_tile(i), 0, 0, 0)),
                pl.BlockSpec((1, dec_seq, SEQ_BLK, A_WIDTH), lambda i, *_: (s_tile(i), 0, 0, 0)),
            ],
            scratch_shapes=[pltpu.VMEM((TM, D_MODEL), bf16), pltpu.VMEM((CARRY_ROWS, B_WIDTH), f32),
                            pltpu.VMEM((N_EXPERTS, LANES), f32), pltpu.VMEM((TM, D_MODEL), f32),
                            pltpu.VMEM((TM, A_WIDTH), f32), pltpu.VMEM((TM, A_WIDTH), f32),
                            pltpu.VMEM((TM, B_WIDTH), f32)],
        ),
        out_shape=[
            jax.ShapeDtypeStruct((t_total, D_MODEL), f32),
            jax.ShapeDtypeStruct((plane_rows, LANES), f32),
            jax.ShapeDtypeStruct((SUBLANES, t_total), i32),
            jax.ShapeDtypeStruct((t_total, LANES), f32),
            jax.ShapeDtypeStruct((N_PARTS, N_EXPERTS, LANES), f32),
            jax.ShapeDtypeStruct((n_batch, CARRY_ROWS, B_WIDTH), f32),
            plane_shape, plane_shape,
        ],
        compiler_params=cparams,
        name="mixer",
    )(aws_s, abs_s, x_prompt, x_prompt, xs_planes, st_planes, n1g, win, ang, anb, a_ws[0][:, :CHUNK, :CHUNK], abias,
      bw, bscale, wout, n2g, rwt, rb, su)

    assert n_exp_tiles <= LANES and t_part % LANES == 0
    asg, tab = pl.pallas_call(
        functools.partial(_tables_kernel, t_total=t_total),
        grid=(N_PARTS,),
        in_specs=[pl.BlockSpec((1, N_EXPERTS, LANES), lambda h: (h, 0, 0)),
                  pl.BlockSpec((SUBLANES, t_part), lambda h: (0, h))],
        out_specs=[pl.BlockSpec((1, 1, p_rows), lambda h: (h, 0, 0), memory_space=pltpu.SMEM),
                   pl.BlockSpec((1, SUBLANES, LANES), lambda h: (h, 0, 0))],
        scratch_shapes=[pltpu.VMEM((TOP_K, 1, t_part), i32), pltpu.VMEM((1, 1, p_rows), i32)]
        + [pltpu.SMEM((1, t_part), i32)] * TOP_K,
        out_shape=[jax.ShapeDtypeStruct((N_PARTS, 1, p_rows), i32),
                   jax.ShapeDtypeStruct((N_PARTS, SUBLANES, LANES), i32)],
        compiler_params=cparams,
        name="route_tables",
    )(counts, meta)
    tile_e = tab[:, 0, :n_exp_tiles].reshape(-1)
    tile_rows = tab[:, 1, :n_exp_tiles].reshape(-1)
    n_valid = tab[:, 2, 0]
    asg = asg.reshape(-1)
    row_part = jnp.arange(N_PARTS * p_rows, dtype=i32) // p_rows
    row_tok = jnp.where(asg >= t_total, asg - t_total, asg)
    row_src = jnp.where(asg >= 0, row_tok - row_part * t_part, 0).reshape(N_PARTS * n_exp_tiles, 1, TE)
    row_dst = jnp.maximum(asg, 0).reshape(N_PARTS * n_exp_tiles, 1, TE)

    flat = lambda h, i: h * n_exp_tiles + i
    cur_blk = lambda h, i, te, nv, nr: (flat(h, jnp.minimum(i, nv[h] - 1)), 0, 0)
    nxt_blk = lambda h, i, te, nv, nr: (flat(h, jnp.minimum(i + 1, nv[h] - 1)), 0, 0)
    w_blk = lambda h, i, te, nv, nr: (te[flat(h, i)], 0, 0)
    smem_blk = lambda imap: pl.BlockSpec((1, 1, TE), imap, memory_space=pltpu.SMEM)
    row_buf = pltpu.VMEM((TE * ROW_TILES, LANES), f32)
    out_tok = pl.pallas_call(
        _expert_kernel,
        grid_spec=pltpu.PrefetchScalarGridSpec(
            num_scalar_prefetch=3,
            grid=(N_PARTS, n_exp_tiles),
            in_specs=[smem_blk(cur_blk), smem_blk(nxt_blk), smem_blk(cur_blk),
                      pl.BlockSpec((t_part * ROW_TILES, LANES), lambda h, i, *_: (h, 0),
                                   pipeline_mode=pl.Buffered(1)),
                      pl.BlockSpec((1, D_MODEL, D_EXPERT), w_blk),
                      pl.BlockSpec((1, D_MODEL, D_EXPERT), w_blk),
                      pl.BlockSpec((1, D_EXPERT, D_MODEL), w_blk)],
            out_specs=pl.BlockSpec(memory_space=pl.ANY),
            scratch_shapes=[row_buf, row_buf, row_buf, row_buf, pltpu.SemaphoreType.DMA((2,))],
        ),
        out_shape=jax.ShapeDtypeStruct((TOP_K * t_total, ROW_TILES, LANES), f32),
        compiler_params=pltpu.CompilerParams(dimension_semantics=("arbitrary", "arbitrary"),
                                             vmem_limit_bytes=EXPERT_VMEM_LIMIT),
        name="moe_experts",
    )(tile_e, n_valid, tile_rows, row_src, row_src, row_dst, h2, exp_w1[0], exp_w3[0], exp_w2[0])
    out_tok = out_tok.reshape(TOP_K * plane_rows, LANES)

    assert t_prompt % TC == 0 and t_sample % TC == 0
    n_c_prompt = t_prompt // TC
    y_prompt, y_sample = pl.pallas_call(
        functools.partial(_combine_kernel, n_prompt_tiles=n_c_prompt),
        grid=(t_total // TC,),
        in_specs=[pl.BlockSpec((TC, D_MODEL), lambda i: (i, 0)),
                  pl.BlockSpec((TC, LANES), lambda i: (i, 0)),
                  _const_spec((1, D_MODEL)),
                  pl.BlockSpec((TC * ROW_TILES, LANES), lambda i: (i, 0)),
                  pl.BlockSpec((TC * ROW_TILES, LANES), lambda i: (t_total // TC + i, 0))],
        out_specs=[pl.BlockSpec((TC, D_MODEL), lambda i: (jnp.minimum(i, n_c_prompt - 1), 0)),
                   pl.BlockSpec((TC, D_MODEL), lambda i: (jnp.maximum(i - n_c_prompt, 0), 0))],
        out_shape=[jax.ShapeDtypeStruct((t_prompt, D_MODEL), f32), jax.ShapeDtypeStruct((t_sample, D_MODEL), f32)],
        compiler_params=cparams,
        name="moe_combine",
    )(x1, gcol, nfg, out_tok, out_tok)

    unplane = lambda a: a.transpose(0, 2, 1, 3).reshape(dec_batch, dec_seq, a.shape[-1])
    y_prompt = y_prompt.reshape(n_batch, seq, D_MODEL)
    y_sample = unplane(y_sample.reshape(n_sample_tiles, dec_seq, SEQ_BLK, D_MODEL))
    pool_state_prompt = pstate[None, :, CARRY_ROWS - POOL_STATE:, :]
    p_s = unplane(pplanes)
    pool_state_sample = jnp.concatenate([state_pool[0], p_s], axis=1)[None, :, -POOL_STATE:, :]
    chunk_v_sample = unplane(vplanes)[None]
    return (y_prompt, y_sample, pool_state_prompt, pool_state_sample, chunk_v_sample)
```

```python
import functools
import math

import jax
import jax.numpy as jnp
from jax import lax
from jax.experimental import pallas as pl
from jax.experimental.pallas import tpu as pltpu

D_MODEL = 1024
A_WIDTH = 512
B_WIDTH = 512
N_HEADS = 4
HEAD_DIM = 128
CHUNK = 128
POOL_WINDOWS = (2, 4, 8, 16)
POOL_STATE = 15
N_GROUPS = 4
EXPERTS_PER_GROUP = 8
N_EXPERTS = 32
TOP_K = 2
D_EXPERT = 512
EPS = 1e-6

SUBLANES = 8
LANES = 128
ROW_TILES = D_MODEL // LANES

TM = 256
TC = 512
TE = 512
SEQ_BLK = 32
N_ROUTER_ROWS = 40
CARRY_ROWS = 16
DMA_UNROLL = 32
VMEM_LIMIT = 48 * 1024 * 1024
N_PARTS = 2
EXPERT_VMEM_LIMIT = 62 * 1024 * 1024

_INV_SQRT2 = 1.0 / math.sqrt(2.0)


def _rmsnorm(x, g):
    r = lax.rsqrt(jnp.mean(x * x, axis=-1, keepdims=True) + EPS)
    return (x * r) * g


def _gelu(x):
    return 0.5 * x * (1.0 + lax.erf(x * _INV_SQRT2))


def _layernorm(x, g, b):
    mu = jnp.mean(x, axis=-1, keepdims=True)
    xc = x - mu
    var = jnp.mean(xc * xc, axis=-1, keepdims=True)
    return (xc * lax.rsqrt(var + EPS)) * g + b


def _row_slab(ref, s, n):
    return ref[pl.ds(s, n, stride=ROW_TILES), :]


def _pool_project(pooled, g, bw_ref, bscale_ref):
    lo, hi = g * HEAD_DIM, (g + 1) * HEAD_DIM
    hb = jnp.dot(pooled.astype(jnp.bfloat16), bw_ref[g], preferred_element_type=jnp.float32)
    return hb * bscale_ref[:, lo:hi]


def _prompt_mixers(j, u, v, p, aws_ref, abias_ref, bw_ref, bscale_ref, mix_ref, pcarry_ref, pstate_ref):
    tri = (lax.broadcasted_iota(jnp.int32, (CHUNK, CHUNK), 0)
           >= lax.broadcasted_iota(jnp.int32, (CHUNK, CHUNK), 1))
    vb = v.astype(jnp.bfloat16)
    for hd in range(N_HEADS):
        lo, hi = hd * HEAD_DIM, (hd + 1) * HEAD_DIM
        w = jnp.where(tri, aws_ref[hd], 0.0).astype(jnp.bfloat16)
        for c in range(TM // CHUNK):
            r0, r1 = c * CHUNK, (c + 1) * CHUNK
            z = jnp.dot(w, vb[r0:r1, lo:hi], preferred_element_type=jnp.float32) + abias_ref[:, lo:hi]
            mix_ref[r0:r1, lo:hi] = (u[r0:r1, lo:hi] * z).astype(jnp.bfloat16)

    head_pos = j * TM + lax.broadcasted_iota(jnp.int32, (CARRY_ROWS, LANES), 0)
    for g, w in enumerate(POOL_WINDOWS):
        lo, hi = g * HEAD_DIM, (g + 1) * HEAD_DIM
        pg = p[:, lo:hi]
        acc = jnp.concatenate([pcarry_ref[:, lo:hi], pg], axis=0)
        shift = 1
        while shift < w:
            acc = acc + pltpu.roll(acc, shift, 0)
            shift *= 2
        head = acc[CARRY_ROWS:2 * CARRY_ROWS, :] / jnp.minimum(head_pos + 1, w).astype(jnp.float32)
        mean = jnp.concatenate([head, acc[2 * CARRY_ROWS:, :] * (1.0 / w)], axis=0)
        pooled = mean - pg
        mix_ref[:, A_WIDTH + lo:A_WIDTH + hi] = _pool_project(pooled, g, bw_ref, bscale_ref).astype(jnp.bfloat16)
    tail = p[TM - CARRY_ROWS:, :]
    pcarry_ref[...] = tail
    pstate_ref[0] = tail


def _sample_mixers(u, v, p, aws_ref, abs_ref, st_ref, bw_ref, bscale_ref, mix_ref, pp_ref, vp_ref):
    n_pos = TM // SEQ_BLK
    for i in range(n_pos):
        vp_ref[0, i] = v[i * SEQ_BLK:(i + 1) * SEQ_BLK, :]
        pp_ref[0, i] = p[i * SEQ_BLK:(i + 1) * SEQ_BLK, :]

    for hd in range(N_HEADS):
        lo, hi = hd * HEAD_DIM, (hd + 1) * HEAD_DIM
        vplanes = [v[s * SEQ_BLK:(s + 1) * SEQ_BLK, lo:hi] for s in range(n_pos)]
        for i in range(n_pos):
            z = vplanes[0] * aws_ref[(hd * n_pos + i) * n_pos]
            for s in range(1, i + 1):
                z = z + vplanes[s] * aws_ref[(hd * n_pos + i) * n_pos + s]
            z = z + abs_ref[hd * n_pos + i]
            r0, r1 = i * SEQ_BLK, (i + 1) * SEQ_BLK
            mix_ref[r0:r1, lo:hi] = (u[r0:r1, lo:hi] * z).astype(jnp.bfloat16)

    for g, w in enumerate(POOL_WINDOWS):
        lo, hi = g * HEAD_DIM, (g + 1) * HEAD_DIM
        planes = [st_ref[0, k, :, lo:hi] for k in range(POOL_STATE)]
        planes += [p[i * SEQ_BLK:(i + 1) * SEQ_BLK, lo:hi] for i in range(n_pos)]
        pooled = []
        for i in range(n_pos):
            top = POOL_STATE + i
            s = planes[top - w + 1]
            for k in range(top - w + 2, top + 1):
                s = s + planes[k]
            pooled.append(s * (1.0 / w) - planes[top])
        pooled = jnp.concatenate(pooled, axis=0)
        mix_ref[:, A_WIDTH + lo:A_WIDTH + hi] = _pool_project(pooled, g, bw_ref, bscale_ref).astype(jnp.bfloat16)


def _route(h2, rwt_ref, rb_ref, su_ref, cnt_ref, meta_ref, gcol_ref):
    tm = h2.shape[0]
    h_hi = h2.astype(jnp.bfloat16)
    h_lo = (h2 - h_hi.astype(jnp.float32)).astype(jnp.bfloat16)
    s = (jnp.dot(h_hi, rwt_ref[...], preferred_element_type=jnp.float32)
         + jnp.dot(h_lo, rwt_ref[...], preferred_element_type=jnp.float32))
    st = s.T
    lt = st[0:N_ROUTER_ROWS, :] + st[N_ROUTER_ROWS:2 * N_ROUTER_ROWS, :] + rb_ref[...]
    row = lambda i: lt[i:i + 1, :]
    l1 = [row(i) for i in range(N_GROUPS)]
    m1 = jnp.maximum(jnp.maximum(l1[0], l1[1]), jnp.maximum(l1[2], l1[3]))
    grp = jnp.where(l1[0] == m1, 0, jnp.where(l1[1] == m1, 1, jnp.where(l1[2] == m1, 2, 3)))
    se = (jnp.exp(l1[0] - m1) + jnp.exp(l1[1] - m1)) + (jnp.exp(l1[2] - m1) + jnp.exp(l1[3] - m1))
    pg = 1.0 / se
    l2 = []
    for e in range(EXPERTS_PER_GROUP):
        c = [row(N_GROUPS + g * EXPERTS_PER_GROUP + e) for g in range(N_GROUPS)]
        l2.append(jnp.where(grp == 0, c[0], jnp.where(grp == 1, c[1], jnp.where(grp == 2, c[2], c[3]))))
    v0 = functools.reduce(jnp.maximum, l2)
    i0 = jnp.full_like(grp, EXPERTS_PER_GROUP - 1)
    for e in range(EXPERTS_PER_GROUP - 2, -1, -1):
        i0 = jnp.where(l2[e] == v0, e, i0)
    neg = jnp.float32(-jnp.inf)
    l2m = [jnp.where(i0 == e, neg, l2[e]) for e in range(EXPERTS_PER_GROUP)]
    v1 = functools.reduce(jnp.maximum, l2m)
    i1 = jnp.full_like(grp, EXPERTS_PER_GROUP - 1)
    for e in range(EXPERTS_PER_GROUP - 2, -1, -1):
        i1 = jnp.where((l2m[e] == v1) & (i0 != e), e, i1)
    d = jnp.exp(v1 - v0)
    g0 = pg / (1.0 + d)
    g1 = (pg * d) / (1.0 + d)
    e0 = grp * EXPERTS_PER_GROUP + i0
    e1 = grp * EXPERTS_PER_GROUP + i1

    eiota = lax.broadcasted_iota(jnp.int32, (N_EXPERTS, tm), 0)
    hit0 = eiota == e0
    hit1 = eiota == e1
    onehot = (hit0 | hit1).astype(jnp.bfloat16)
    prefix = jnp.dot(onehot, su_ref[...], preferred_element_type=jnp.float32)
    carry = cnt_ref[...]
    base = prefix + jnp.concatenate([carry] * (tm // LANES), axis=1)
    r0 = jnp.sum(jnp.where(hit0, base, 0.0), axis=0, keepdims=True)
    r1 = jnp.sum(jnp.where(hit1, base, 0.0), axis=0, keepdims=True)
    ones = jnp.ones((tm, LANES), jnp.bfloat16)
    cnt_ref[...] = carry + jnp.dot(onehot, ones, preferred_element_type=jnp.float32)

    meta_ref[0:1, :] = e0
    meta_ref[1:2, :] = e1
    meta_ref[2:3, :] = r0.astype(jnp.int32)
    meta_ref[3:4, :] = r1.astype(jnp.int32)
    meta_ref[4:8, :] = jnp.zeros((4, tm), jnp.int32)

    giota = lax.broadcasted_iota(jnp.int32, (LANES, tm), 0)
    gpad = jnp.where(giota == 0, g0, jnp.where(giota == 1, g1, 0.0))
    gcol_ref[...] = gpad.T


def _mixer_kernel(aws_s_ref, abs_s_ref,
                  x0_ref, xpn_ref, xsn_ref,
                  st_ref, n1g_ref, win_ref, ang_ref, anb_ref, aws_ref, abias_ref, bw_ref, bscale_ref,
                  wout_ref, n2g_ref, rwt_ref, rb_ref, su_ref,
                  x1_ref, h2_ref, meta_ref, gcol_ref, counts_ref, pstate_ref, pp_ref, vp_ref,
                  mix_ref, pcarry_ref, cnt_ref, x_ref, u_ref, v_ref, p_ref,
                  *, n_prompt_tiles, seq_tiles, part_tiles):
    i = pl.program_id(0)
    is_prompt = i < n_prompt_tiles
    j = i % seq_tiles

    def in_proj(x):
        h = _rmsnorm(x, n1g_ref[...]).astype(jnp.bfloat16)
        proj = jnp.dot(h, win_ref[...], preferred_element_type=jnp.float32)
        uv = _gelu(proj[:, : 2 * A_WIDTH])
        u_ref[...] = uv[:, :A_WIDTH]
        v_ref[...] = _layernorm(uv[:, A_WIDTH:], ang_ref[...], anb_ref[...])
        p_ref[...] = proj[:, 2 * A_WIDTH:]
        x_ref[...] = x

    @pl.when(i == 0)
    def _():
        in_proj(x0_ref[0])

    @pl.when(i % part_tiles == 0)
    def _():
        cnt_ref[...] = jnp.zeros_like(cnt_ref)

    @pl.when(is_prompt & (j == 0))
    def _():
        pcarry_ref[...] = jnp.zeros_like(pcarry_ref)

    def finish_tile():
        x1 = x_ref[...] + jnp.dot(mix_ref[...], wout_ref[...], preferred_element_type=jnp.float32)
        x1_ref[...] = x1
        h2 = _rmsnorm(x1, n2g_ref[...])
        for s in range(ROW_TILES):
            h2_ref[pl.ds(s, TM, stride=ROW_TILES), :] = h2[:, s * LANES:(s + 1) * LANES]
        _route(h2, rwt_ref, rb_ref, su_ref, cnt_ref, meta_ref, gcol_ref)
        counts_ref[0] = cnt_ref[...]
        nxt = jnp.minimum(i + 1, pl.num_programs(0) - 1)
        in_proj(jnp.where(nxt < n_prompt_tiles, xpn_ref[0], xsn_ref[...]))

    @pl.when(is_prompt)
    def _():
        _prompt_mixers(j, u_ref[...], v_ref[...], p_ref[...], aws_ref, abias_ref, bw_ref, bscale_ref, mix_ref,
                       pcarry_ref, pstate_ref)
        finish_tile()

    @pl.when(jnp.logical_not(is_prompt))
    def _():
        _sample_mixers(u_ref[...], v_ref[...], p_ref[...], aws_s_ref, abs_s_ref, st_ref, bw_ref, bscale_ref, mix_ref,
                       pp_ref, vp_ref)
        finish_tile()


def _tables_kernel(counts_ref, meta_ref, asg_ref, tab_ref, dest_vmem, fill_vmem, *dest_smem, t_total):
    h = pl.program_id(0)
    t_part = meta_ref.shape[1]
    cnt = counts_ref[0]
    padded = jnp.floor((cnt + (TE - 1)) * (1.0 / TE)) * TE
    sub = lax.broadcasted_iota(jnp.int32, cnt.shape, 0)
    pad_end = padded
    shift = 1
    while shift < N_EXPERTS:
        pad_end = pad_end + jnp.where(sub >= shift, pltpu.roll(pad_end, shift, 0), 0.0)
        shift *= 2
    pad_start = pad_end - padded
    base = pad_start.astype(jnp.int32)

    for k in range(TOP_K):
        e = meta_ref[k:k + 1, :]
        d = meta_ref[TOP_K + k:TOP_K + k + 1, :]
        for ex in range(N_EXPERTS):
            row = jnp.concatenate([base[ex:ex + 1, :]] * (t_part // LANES), axis=1)
            d = d + jnp.where(e == ex, row, 0)
        dest_vmem[k] = d
    for k, dsm in enumerate(dest_smem):
        pltpu.sync_copy(dest_vmem.at[k], dsm)

    fill_vmem[...] = jnp.full(fill_vmem.shape, -1, jnp.int32)
    pltpu.sync_copy(fill_vmem, asg_ref)
    unroll = 16
    for k, dsm in enumerate(dest_smem):
        def invert(_, carry, dsm=dsm):
            t, a = carry
            for uu in range(unroll):
                asg_ref[0, 0, dsm[0, t + uu]] = a + uu
            return t + unroll, a + unroll
        lax.fori_loop(0, t_part // unroll, invert, (jnp.int32(0), k * t_total + h * t_part))

    n_valid = jnp.maximum(pad_end[N_EXPERTS - 1:, :] * (1.0 / TE), 1.0)
    tile = jnp.minimum(lax.broadcasted_iota(jnp.int32, (1, LANES), 1).astype(jnp.float32), n_valid - 1.0)
    tile_start = tile * TE
    tile_e = jnp.minimum(jnp.sum((pad_end <= tile_start).astype(jnp.float32), axis=0, keepdims=True),
                         N_EXPERTS - 1.0)
    is_e = sub.astype(jnp.float32) == tile_e
    seg_start = jnp.sum(jnp.where(is_e, pad_start, 0.0), axis=0, keepdims=True)
    seg_cnt = jnp.sum(jnp.where(is_e, cnt, 0.0), axis=0, keepdims=True)
    rows = jnp.clip(seg_cnt - (tile_start - seg_start), 0.0, TE)
    tab_ref[0, 0:1, :] = tile_e.astype(jnp.int32)
    tab_ref[0, 1:2, :] = rows.astype(jnp.int32)
    tab_ref[0, 2:3, :] = n_valid.astype(jnp.int32)
    tab_ref[0, 3:, :] = jnp.zeros((SUBLANES - 3, LANES), jnp.int32)


def _expert_kernel(te_ref, nv_ref, nrows_ref,
                   src_ref, src_next_ref, dst_ref,
                   h2_ref, w1_ref, w3_ref, w2_ref,
                   out_hbm,
                   xt0, xt1, ob0, ob1, ssem):
    i = pl.program_id(1)
    g = pl.program_id(0) * pl.num_programs(1) + i
    nv = nv_ref[pl.program_id(0)]
    slab = lambda r: pl.ds(pl.multiple_of(r * ROW_TILES, ROW_TILES), ROW_TILES)

    def gather(idx_ref, xt):
        for r in range(TE):
            xt[r * ROW_TILES:(r + 1) * ROW_TILES, :] = h2_ref[slab(idx_ref[0, 0, r]), :]

    def wait_scatter(tile, ob, other, sem):
        n = nrows_ref[tile] * ROW_TILES
        pltpu.make_async_copy(other.at[pl.ds(0, n), :], ob.at[pl.ds(0, n), :], sem).wait()

    def tile_body(xt, xt_next, ob, ob_other, sem, sem_other):
        @pl.when(i >= 2)
        def _():
            wait_scatter(g - 2, ob, ob_other, sem)

        def compute(m):
            gather(src_next_ref, xt_next)
            x = jnp.concatenate([_row_slab(xt, s, m) for s in range(ROW_TILES)], axis=1).astype(jnp.bfloat16)
            a = jnp.dot(x, w1_ref[0].astype(jnp.bfloat16), preferred_element_type=jnp.float32)
            b = jnp.dot(x, w3_ref[0].astype(jnp.bfloat16), preferred_element_type=jnp.float32)
            h = (a * jax.nn.sigmoid(a)) * b
            o = jnp.dot(h.astype(jnp.bfloat16), w2_ref[0].astype(jnp.bfloat16), preferred_element_type=jnp.float32)
            for s in range(ROW_TILES):
                ob[pl.ds(s, m, stride=ROW_TILES), :] = o[:, s * LANES:(s + 1) * LANES]

        n = nrows_ref[g]

        @pl.when(n > TE // 2)
        def _():
            compute(TE)

        @pl.when(n <= TE // 2)
        def _():
            compute(TE // 2)

        n_full = n // DMA_UNROLL
        copy = lambda r: pltpu.make_async_copy(ob.at[slab(r), :], out_hbm.at[dst_ref[0, 0, r]], sem)

        def body(rb, carry):
            for uu in range(DMA_UNROLL):
                copy(rb * DMA_UNROLL + uu).start(priority=uu % 2)
            return carry
        lax.fori_loop(0, n_full, body, 0)

        def tail(r, carry):
            copy(r).start(priority=1)
            return carry
        lax.fori_loop(n_full * DMA_UNROLL, n, tail, 0)

        @pl.when(i == nv - 1)
        def _():
            @pl.when(i >= 1)
            def _():
                wait_scatter(g - 1, ob_other, ob, sem_other)
            wait_scatter(g, ob, ob_other, sem)

    @pl.when(i < nv)
    def _():
        @pl.when(i == 0)
        def _():
            gather(src_ref, xt0)

        @pl.when(i % 2 == 0)
        def _():
            tile_body(xt0, xt1, ob0, ob1, ssem.at[0], ssem.at[1])

        @pl.when(i % 2 == 1)
        def _():
            tile_body(xt1, xt0, ob1, ob0, ssem.at[1], ssem.at[0])


def _combine_kernel(x1_ref, gcol_ref, nfg_ref, o0_ref, o1_ref, yp_ref, ys_ref, *, n_prompt_tiles):
    i = pl.program_id(0)
    o0 = jnp.concatenate([_row_slab(o0_ref, s, TC) for s in range(ROW_TILES)], axis=1)
    o1 = jnp.concatenate([_row_slab(o1_ref, s, TC) for s in range(ROW_TILES)], axis=1)
    g = gcol_ref[...]
    moe = g[:, 0:1] * o0 + g[:, 1:2] * o1
    y = _rmsnorm(x1_ref[...] + moe, nfg_ref[...])

    @pl.when(i < n_prompt_tiles)
    def _():
        yp_ref[...] = y

    @pl.when(i >= n_prompt_tiles)
    def _():
        ys_ref[...] = y


def _const_spec(shape):
    return pl.BlockSpec(shape, lambda *_: (0,) * len(shape))


def kernel(x_prompt, x_sample, state_pool, norm1_g, w_in, a_norm_g, a_norm_b, a_ws, a_bs, b_w, b_scale, w_out,
           norm2_g, r1_w, r1_b, r2_w, r2_b, exp_w1, exp_w3, exp_w2, normf_g):
    f32, bf16, i32 = jnp.float32, jnp.bfloat16, jnp.int32
    n_batch, seq, _ = x_prompt.shape
    dec_batch, dec_seq, _ = x_sample.shape
    assert norm1_g.shape[0] == 1 and seq % TM == 0 and TM % CHUNK == 0
    assert dec_seq * SEQ_BLK == TM and dec_batch % SEQ_BLK == 0 and dec_seq <= CHUNK
    t_prompt = n_batch * seq
    t_sample = dec_batch * dec_seq
    t_total = t_prompt + t_sample
    n_tok_tiles = t_total // TM
    n_prompt_tiles = t_prompt // TM
    n_sample_tiles = t_sample // TM
    seq_tiles = seq // TM
    plane_rows = t_total * ROW_TILES
    assert n_tok_tiles % N_PARTS == 0
    part_tiles = n_tok_tiles // N_PARTS
    t_part = part_tiles * TM
    n_exp_tiles = -(-(TOP_K * t_part + N_EXPERTS * (TE - 1)) // TE)
    p_rows = n_exp_tiles * TE

    n1g = norm1_g[0][None, :]
    n2g = norm2_g[0][None, :]
    nfg = normf_g[None, :]
    win = w_in[0].astype(bf16)
    wout = w_out[0].astype(bf16)
    ang = a_norm_g[0][None, :]
    anb = a_norm_b[0][None, :]
    bw = b_w[0].astype(bf16)
    bscale = b_scale[0][None, :]
    abias = jnp.repeat(a_bs[0][:, :CHUNK].T, HEAD_DIM, axis=1)
    rw = jnp.concatenate([r1_w[0], r2_w[0].transpose(1, 0, 2).reshape(D_MODEL, N_EXPERTS),
                          jnp.zeros((D_MODEL, N_ROUTER_ROWS - N_GROUPS - N_EXPERTS), f32)], axis=1)
    rw_hi = rw.astype(bf16)
    rw_lo = (rw - rw_hi.astype(f32)).astype(bf16)
    rwt = jnp.concatenate([rw_hi, rw_lo, jnp.zeros((D_MODEL, LANES - 2 * N_ROUTER_ROWS), bf16)], axis=1)
    rbias = jnp.concatenate([r1_b[0], r2_b[0].reshape(-1),
                             jnp.zeros((N_ROUTER_ROWS - N_GROUPS - N_EXPERTS,), f32)])
    rb = jnp.broadcast_to(rbias[:, None], (N_ROUTER_ROWS, TM))
    su = (jnp.arange(TM)[:, None] < jnp.arange(TM)[None, :]).astype(bf16)
    xs_planes = (x_sample.reshape(n_sample_tiles, SEQ_BLK, dec_seq, D_MODEL)
                 .transpose(0, 2, 1, 3).reshape(t_sample, D_MODEL))
    st_planes = state_pool[0].reshape(n_sample_tiles, SEQ_BLK, POOL_STATE, B_WIDTH).transpose(0, 2, 1, 3)
    aws_s = a_ws[0][:, :dec_seq, :dec_seq].reshape(-1)
    abs_s = a_bs[0][:, :dec_seq].reshape(-1)

    cparams = pltpu.CompilerParams(dimension_semantics=("arbitrary",), vmem_limit_bytes=VMEM_LIMIT)

    p_tile = lambda i: jnp.minimum(i, n_prompt_tiles - 1)
    s_tile = lambda i: jnp.maximum(i - n_prompt_tiles, 0)
    plane_shape = jax.ShapeDtypeStruct((n_sample_tiles, dec_seq, SEQ_BLK, A_WIDTH), f32)
    x1, h2, meta, gcol, counts, pstate, pplanes, vplanes = pl.pallas_call(
        functools.partial(_mixer_kernel, n_prompt_tiles=n_prompt_tiles, seq_tiles=seq_tiles, part_tiles=part_tiles),
        grid_spec=pltpu.PrefetchScalarGridSpec(
            num_scalar_prefetch=2,
            grid=(n_tok_tiles,),
            in_specs=[
                pl.BlockSpec((1, TM, D_MODEL), lambda i, *_: (0, 0, 0)),
                pl.BlockSpec((1, TM, D_MODEL),
                             lambda i, *_: (p_tile(i + 1) // seq_tiles, p_tile(i + 1) % seq_tiles, 0)),
                pl.BlockSpec((TM, D_MODEL), lambda i, *_: (jnp.minimum(s_tile(i + 1), n_sample_tiles - 1), 0)),
                pl.BlockSpec((1, POOL_STATE, SEQ_BLK, B_WIDTH), lambda i, *_: (s_tile(i), 0, 0, 0)),
                _const_spec((1, D_MODEL)), _const_spec((D_MODEL, 3 * A_WIDTH)), _const_spec((1, A_WIDTH)),
                _const_spec((1, A_WIDTH)), _const_spec((N_HEADS, CHUNK, CHUNK)), _const_spec((CHUNK, A_WIDTH)),
                _const_spec((N_GROUPS, HEAD_DIM, HEAD_DIM)), _const_spec((1, B_WIDTH)),
                _const_spec((D_MODEL, D_MODEL)), _const_spec((1, D_MODEL)), _const_spec((D_MODEL, LANES)),
                _const_spec((N_ROUTER_ROWS, TM)), _const_spec((TM, TM)),
            ],
            out_specs=[
                pl.BlockSpec((TM, D_MODEL), lambda i, *_: (i, 0)),
                pl.BlockSpec((TM * ROW_TILES, LANES), lambda i, *_: (i, 0)),
                pl.BlockSpec((SUBLANES, TM), lambda i, *_: (0, i)),
                pl.BlockSpec((TM, LANES), lambda i, *_: (i, 0)),
                pl.BlockSpec((1, N_EXPERTS, LANES), lambda i, *_: (i // part_tiles, 0, 0)),
                pl.BlockSpec((1, CARRY_ROWS, B_WIDTH), lambda i, *_: (p_tile(i) // seq_tiles, 0, 0)),
                pl.BlockSpec((1, dec_seq, SEQ_BLK, A_WIDTH), lambda i, *_: (s_tile(i), 0, 0, 0)),
                pl.BlockSpec((1, dec_seq, SEQ_BLK, A_WIDTH), lambda i, *_: (s_tile(i), 0, 0, 0)),
            ],
            scratch_shapes=[pltpu.VMEM((TM, D_MODEL), bf16), pltpu.VMEM((CARRY_ROWS, B_WIDTH), f32),
                            pltpu.VMEM((N_EXPERTS, LANES), f32), pltpu.VMEM((TM, D_MODEL), f32),
                            pltpu.VMEM((TM, A_WIDTH), f32), pltpu.VMEM((TM, A_WIDTH), f32),
                            pltpu.VMEM((TM, B_WIDTH), f32)],
        ),
        out_shape=[
            jax.ShapeDtypeStruct((t_total, D_MODEL), f32),
            jax.ShapeDtypeStruct((plane_rows, LANES), f32),
            jax.ShapeDtypeStruct((SUBLANES, t_total), i32),
            jax.ShapeDtypeStruct((t_total, LANES), f32),
            jax.ShapeDtypeStruct((N_PARTS, N_EXPERTS, LANES), f32),
            jax.ShapeDtypeStruct((n_batch, CARRY_ROWS, B_WIDTH), f32),
            plane_shape, plane_shape,
        ],
        compiler_params=cparams,
        name="mixer",
    )(aws_s, abs_s, x_prompt, x_prompt, xs_planes, st_planes, n1g, win, ang, anb, a_ws[0][:, :CHUNK, :CHUNK], abias,
      bw, bscale, wout, n2g, rwt, rb, su)

    assert n_exp_tiles <= LANES and t_part % LANES == 0
    asg, tab = pl.pallas_call(
        functools.partial(_tables_kernel, t_total=t_total),
        grid=(N_PARTS,),
        in_specs=[pl.BlockSpec((1, N_EXPERTS, LANES), lambda h: (h, 0, 0)),
                  pl.BlockSpec((SUBLANES, t_part), lambda h: (0, h))],
        out_specs=[pl.BlockSpec((1, 1, p_rows), lambda h: (h, 0, 0), memory_space=pltpu.SMEM),
                   pl.BlockSpec((1, SUBLANES, LANES), lambda h: (h, 0, 0))],
        scratch_shapes=[pltpu.VMEM((TOP_K, 1, t_part), i32), pltpu.VMEM((1, 1, p_rows), i32)]
        + [pltpu.SMEM((1, t_part), i32)] * TOP_K,
        out_shape=[jax.ShapeDtypeStruct((N_PARTS, 1, p_rows), i32),
                   jax.ShapeDtypeStruct((N_PARTS, SUBLANES, LANES), i32)],
        compiler_params=cparams,
        name="route_tables",
    )(counts, meta)
    tile_e = tab[:, 0, :n_exp_tiles].reshape(-1)
    tile_rows = tab[:, 1, :n_exp_tiles].reshape(-1)
    n_valid = tab[:, 2, 0]
    asg = asg.reshape(-1)
    row_part = jnp.arange(N_PARTS * p_rows, dtype=i32) // p_rows
    row_tok = jnp.where(asg >= t_total, asg - t_total, asg)
    row_src = jnp.where(asg >= 0, row_tok - row_part * t_part, 0).reshape(N_PARTS * n_exp_tiles, 1, TE)
    row_dst = jnp.maximum(asg, 0).reshape(N_PARTS * n_exp_tiles, 1, TE)

    flat = lambda h, i: h * n_exp_tiles + i
    cur_blk = lambda h, i, te, nv, nr: (flat(h, jnp.minimum(i, nv[h] - 1)), 0, 0)
    nxt_blk = lambda h, i, te, nv, nr: (flat(h, jnp.minimum(i + 1, nv[h] - 1)), 0, 0)
    w_blk = lambda h, i, te, nv, nr: (te[flat(h, i)], 0, 0)
    smem_blk = lambda imap: pl.BlockSpec((1, 1, TE), imap, memory_space=pltpu.SMEM)
    row_buf = pltpu.VMEM((TE * ROW_TILES, LANES), f32)
    out_tok = pl.pallas_call(
        _expert_kernel,
        grid_spec=pltpu.PrefetchScalarGridSpec(
            num_scalar_prefetch=3,
            grid=(N_PARTS, n_exp_tiles),
            in_specs=[smem_blk(cur_blk), smem_blk(nxt_blk), smem_blk(cur_blk),
                      pl.BlockSpec((t_part * ROW_TILES, LANES), lambda h, i, *_: (h, 0),
                                   pipeline_mode=pl.Buffered(1)),
                      pl.BlockSpec((1, D_MODEL, D_EXPERT), w_blk),
                      pl.BlockSpec((1, D_MODEL, D_EXPERT), w_blk),
                      pl.BlockSpec((1, D_EXPERT, D_MODEL), w_blk)],
            out_specs=pl.BlockSpec(memory_space=pl.ANY),
            scratch_shapes=[row_buf, row_buf, row_buf, row_buf, pltpu.SemaphoreType.DMA((2,))],
        ),
        out_shape=jax.ShapeDtypeStruct((TOP_K * t_total, ROW_TILES, LANES), f32),
        compiler_params=pltpu.CompilerParams(dimension_semantics=("arbitrary", "arbitrary"),
                                             vmem_limit_bytes=EXPERT_VMEM_LIMIT),
        name="moe_experts",
    )(tile_e, n_valid, tile_rows, row_src, row_src, row_dst, h2, exp_w1[0], exp_w3[0], exp_w2[0])
    out_tok = out_tok.reshape(TOP_K * plane_rows, LANES)

    assert t_prompt % TC == 0 and t_sample % TC == 0
    n_c_prompt = t_prompt // TC
    y_prompt, y_sample = pl.pallas_call(
        functools.partial(_combine_kernel, n_prompt_tiles=n_c_prompt),
        grid=(t_total // TC,),
        in_specs=[pl.BlockSpec((TC, D_MODEL), lambda i: (i, 0)),
                  pl.BlockSpec((TC, LANES), lambda i: (i, 0)),
                  _const_spec((1, D_MODEL)),
                  pl.BlockSpec((TC * ROW_TILES, LANES), lambda i: (i, 0)),
                  pl.BlockSpec((TC * ROW_TILES, LANES), lambda i: (t_total // TC + i, 0))],
        out_specs=[pl.BlockSpec((TC, D_MODEL), lambda i: (jnp.minimum(i, n_c_prompt - 1), 0)),
                   pl.BlockSpec((TC, D_MODEL), lambda i: (jnp.maximum(i - n_c_prompt, 0), 0))],
        out_shape=[jax.ShapeDtypeStruct((t_prompt, D_MODEL), f32), jax.ShapeDtypeStruct((t_sample, D_MODEL), f32)],
        compiler_params=cparams,
        name="moe_combine",
    )(x1, gcol, nfg, out_tok, out_tok)

    unplane = lambda a: a.transpose(0, 2, 1, 3).reshape(dec_batch, dec_seq, a.shape[-1])
    y_prompt = y_prompt.reshape(n_batch, seq, D_MODEL)
    y_sample = unplane(y_sample.reshape(n_sample_tiles, dec_seq, SEQ_BLK, D_MODEL))
    pool_state_prompt = pstate[None, :, CARRY_ROWS - POOL_STATE:, :]
    p_s = unplane(pplanes)
    pool_state_sample = jnp.concatenate([state_pool[0], p_s], axis=1)[None, :, -POOL_STATE:, :]
    chunk_v_sample = unplane(vplanes)[None]
    return (y_prompt, y_sample, pool_state_prompt, pool_state_sample, chunk_v_sample)
```

```python
import functools
import math

import jax
import jax.numpy as jnp
from jax import lax
from jax.experimental import pallas as pl
from jax.experimental.pallas import tpu as pltpu

D_MODEL = 1024
A_WIDTH = 512
B_WIDTH = 512
N_HEADS = 4
HEAD_DIM = 128
CHUNK = 128
POOL_WINDOWS = (2, 4, 8, 16)
POOL_STATE = 15
N_GROUPS = 4
EXPERTS_PER_GROUP = 8
N_EXPERTS = 32
TOP_K = 2
D_EXPERT = 512
EPS = 1e-6

SUBLANES = 8
LANES = 128
ROW_TILES = D_MODEL // LANES

TM = 256
TC = 512
TE = 512
SEQ_BLK = 32
N_ROUTER_ROWS = 40
CARRY_ROWS = 16
DMA_UNROLL = 32
VMEM_LIMIT = 48 * 1024 * 1024
N_PARTS = 2
EXPERT_VMEM_LIMIT = 62 * 1024 * 1024

_INV_SQRT2 = 1.0 / math.sqrt(2.0)


def _rmsnorm(x, g):
    r = lax.rsqrt(jnp.mean(x * x, axis=-1, keepdims=True) + EPS)
    return (x * r) * g


def _gelu(x):
    return 0.5 * x * (1.0 + lax.erf(x * _INV_SQRT2))


def _layernorm(x, g, b):
    mu = jnp.mean(x, axis=-1, keepdims=True)
    xc = x - mu
    var = jnp.mean(xc * xc, axis=-1, keepdims=True)
    return (xc * lax.rsqrt(var + EPS)) * g + b


def _row_slab(ref, s, n):
    return ref[pl.ds(s, n, stride=ROW_TILES), :]


def _pool_project(pooled, g, bw_ref, bscale_ref):
    lo, hi = g * HEAD_DIM, (g + 1) * HEAD_DIM
    hb = jnp.dot(pooled.astype(jnp.bfloat16), bw_ref[g], preferred_element_type=jnp.float32)
    return hb * bscale_ref[:, lo:hi]


def _prompt_mixers(j, u, v, p, aws_ref, abias_ref, bw_ref, bscale_ref, mix_ref, pcarry_ref, pstate_ref):
    tri = (lax.broadcasted_iota(jnp.int32, (CHUNK, CHUNK), 0)
           >= lax.broadcasted_iota(jnp.int32, (CHUNK, CHUNK), 1))
    vb = v.astype(jnp.bfloat16)
    for hd in range(N_HEADS):
        lo, hi = hd * HEAD_DIM, (hd + 1) * HEAD_DIM
        w = jnp.where(tri, aws_ref[hd], 0.0).astype(jnp.bfloat16)
        for c in range(TM // CHUNK):
            r0, r1 = c * CHUNK, (c + 1) * CHUNK
            z = jnp.dot(w, vb[r0:r1, lo:hi], preferred_element_type=jnp.float32) + abias_ref[:, lo:hi]
            mix_ref[r0:r1, lo:hi] = (u[r0:r1, lo:hi] * z).astype(jnp.bfloat16)

    head_pos = j * TM + lax.broadcasted_iota(jnp.int32, (CARRY_ROWS, LANES), 0)
    for g, w in enumerate(POOL_WINDOWS):
        lo, hi = g * HEAD_DIM, (g + 1) * HEAD_DIM
        pg = p[:, lo:hi]
        acc = jnp.concatenate([pcarry_ref[:, lo:hi], pg], axis=0)
        shift = 1
        while shift < w:
            acc = acc + pltpu.roll(acc, shift, 0)
            shift *= 2
        head = acc[CARRY_ROWS:2 * CARRY_ROWS, :] / jnp.minimum(head_pos + 1, w).astype(jnp.float32)
        mean = jnp.concatenate([head, acc[2 * CARRY_ROWS:, :] * (1.0 / w)], axis=0)
        pooled = mean - pg
        mix_ref[:, A_WIDTH + lo:A_WIDTH + hi] = _pool_project(pooled, g, bw_ref, bscale_ref).astype(jnp.bfloat16)
    tail = p[TM - CARRY_ROWS:, :]
    pcarry_ref[...] = tail
    pstate_ref[0] = tail


def _sample_mixers(u, v, p, aws_ref, abs_ref, st_ref, bw_ref, bscale_ref, mix_ref, pp_ref, vp_ref):
    n_pos = TM // SEQ_BLK
    for i in range(n_pos):
        vp_ref[0, i] = v[i * SEQ_BLK:(i + 1) * SEQ_BLK, :]
        pp_ref[0, i] = p[i * SEQ_BLK:(i + 1) * SEQ_BLK, :]

    for hd in range(N_HEADS):
        lo, hi = hd * HEAD_DIM, (hd + 1) * HEAD_DIM
        vplanes = [v[s * SEQ_BLK:(s + 1) * SEQ_BLK, lo:hi] for s in range(n_pos)]
        for i in range(n_pos):
            z = vplanes[0] * aws_ref[(hd * n_pos + i) * n_pos]
            for s in range(1, i + 1):
                z = z + vplanes[s] * aws_ref[(hd * n_pos + i) * n_pos + s]
            z = z + abs_ref[hd * n_pos + i]
            r0, r1 = i * SEQ_BLK, (i + 1) * SEQ_BLK
            mix_ref[r0:r1, lo:hi] = (u[r0:r1, lo:hi] * z).astype(jnp.bfloat16)

    for g, w in enumerate(POOL_WINDOWS):
        lo, hi = g * HEAD_DIM, (g + 1) * HEAD_DIM
        planes = [st_ref[0, k, :, lo:hi] for k in range(POOL_STATE)]
        planes += [p[i * SEQ_BLK:(i + 1) * SEQ_BLK, lo:hi] for i in range(n_pos)]
        pooled = []
        for i in range(n_pos):
            top = POOL_STATE + i
            s = planes[top - w + 1]
            for k in range(top - w + 2, top + 1):
                s = s + planes[k]
            pooled.append(s * (1.0 / w) - planes[top])
        pooled = jnp.concatenate(pooled, axis=0)
        mix_ref[:, A_WIDTH + lo:A_WIDTH + hi] = _pool_project(pooled, g, bw_ref, bscale_ref).astype(jnp.bfloat16)


def _route(h2, rwt_ref, rb_ref, su_ref, cnt_ref, meta_ref, gcol_ref):
    tm = h2.shape[0]
    h_hi = h2.astype(jnp.bfloat16)
    h_lo = (h2 - h_hi.astype(jnp.float32)).astype(jnp.bfloat16)
    s = (jnp.dot(h_hi, rwt_ref[...], preferred_element_type=jnp.float32)
         + jnp.dot(h_lo, rwt_ref[...], preferred_element_type=jnp.float32))
    st = s.T
    lt = st[0:N_ROUTER_ROWS, :] + st[N_ROUTER_ROWS:2 * N_ROUTER_ROWS, :] + rb_ref[...]
    row = lambda i: lt[i:i + 1, :]
    l1 = [row(i) for i in range(N_GROUPS)]
    m1 = jnp.maximum(jnp.maximum(l1[0], l1[1]), jnp.maximum(l1[2], l1[3]))
    grp = jnp.where(l1[0] == m1, 0, jnp.where(l1[1] == m1, 1, jnp.where(l1[2] == m1, 2, 3)))
    se = (jnp.exp(l1[0] - m1) + jnp.exp(l1[1] - m1)) + (jnp.exp(l1[2] - m1) + jnp.exp(l1[3] - m1))
    pg = 1.0 / se
    l2 = []
    for e in range(EXPERTS_PER_GROUP):
        c = [row(N_GROUPS + g * EXPERTS_PER_GROUP + e) for g in range(N_GROUPS)]
        l2.append(jnp.where(grp == 0, c[0], jnp.where(grp == 1, c[1], jnp.where(grp == 2, c[2], c[3]))))
    v0 = functools.reduce(jnp.maximum, l2)
    i0 = jnp.full_like(grp, EXPERTS_PER_GROUP - 1)
    for e in range(EXPERTS_PER_GROUP - 2, -1, -1):
        i0 = jnp.where(l2[e] == v0, e, i0)
    neg = jnp.float32(-jnp.inf)
    l2m = [jnp.where(i0 == e, neg, l2[e]) for e in range(EXPERTS_PER_GROUP)]
    v1 = functools.reduce(jnp.maximum, l2m)
    i1 = jnp.full_like(grp, EXPERTS_PER_GROUP - 1)
    for e in range(EXPERTS_PER_GROUP - 2, -1, -1):
        i1 = jnp.where((l2m[e] == v1) & (i0 != e), e, i1)
    d = jnp.exp(v1 - v0)
    g0 = pg / (1.0 + d)
    g1 = (pg * d) / (1.0 + d)
    e0 = grp * EXPERTS_PER_GROUP + i0
    e1 = grp * EXPERTS_PER_GROUP + i1

    eiota = lax.broadcasted_iota(jnp.int32, (N_EXPERTS, tm), 0)
    hit0 = eiota == e0
    hit1 = eiota == e1
    onehot = (hit0 | hit1).astype(jnp.bfloat16)
    prefix = jnp.dot(onehot, su_ref[...], preferred_element_type=jnp.float32)
    carry = cnt_ref[...]
    base = prefix + jnp.concatenate([carry] * (tm // LANES), axis=1)
    r0 = jnp.sum(jnp.where(hit0, base, 0.0), axis=0, keepdims=True)
    r1 = jnp.sum(jnp.where(hit1, base, 0.0), axis=0, keepdims=True)
    ones = jnp.ones((tm, LANES), jnp.bfloat16)
    cnt_ref[...] = carry + jnp.dot(onehot, ones, preferred_element_type=jnp.float32)

    meta_ref[0:1, :] = e0
    meta_ref[1:2, :] = e1
    meta_ref[2:3, :] = r0.astype(jnp.int32)
    meta_ref[3:4, :] = r1.astype(jnp.int32)
    meta_ref[4:8, :] = jnp.zeros((4, tm), jnp.int32)

    giota = lax.broadcasted_iota(jnp.int32, (LANES, tm), 0)
    gpad = jnp.where(giota == 0, g0, jnp.where(giota == 1, g1, 0.0))
    gcol_ref[...] = gpad.T


def _mixer_kernel(aws_s_ref, abs_s_ref,
                  x0_ref, xpn_ref, xsn_ref,
                  st_ref, n1g_ref, win_ref, ang_ref, anb_ref, aws_ref, abias_ref, bw_ref, bscale_ref,
                  wout_ref, n2g_ref, rwt_ref, rb_ref, su_ref,
                  x1_ref, h2_ref, meta_ref, gcol_ref, counts_ref, pstate_ref, pp_ref, vp_ref, asg_ref,
                  mix_ref, pcarry_ref, cnt_ref, x_ref, u_ref, v_ref, p_ref, part_meta, dest_vmem, fill_vmem, *dest_smem,
                  n_prompt_tiles, seq_tiles, part_tiles, t_total):
    i = pl.program_id(0)
    is_prompt = i < n_prompt_tiles
    j = i % seq_tiles
    part = i // part_tiles
    tile_in_part = i % part_tiles

    @pl.when((i > 0) & (tile_in_part == 0))
    def _():
        _, _, pad_start = _segment_offsets(cnt_ref[...])
        _stage_inversion(part_meta, pad_start, dest_vmem, dest_smem, fill_vmem, asg_ref)

    def in_proj(x):
        h = _rmsnorm(x, n1g_ref[...]).astype(jnp.bfloat16)
        proj = jnp.dot(h, win_ref[...], preferred_element_type=jnp.float32)
        uv = _gelu(proj[:, : 2 * A_WIDTH])
        u_ref[...] = uv[:, :A_WIDTH]
        v_ref[...] = _layernorm(uv[:, A_WIDTH:], ang_ref[...], anb_ref[...])
        p_ref[...] = proj[:, 2 * A_WIDTH:]
        x_ref[...] = x

    @pl.when(i == 0)
    def _():
        in_proj(x0_ref[0])

    @pl.when(i % part_tiles == 0)
    def _():
        cnt_ref[...] = jnp.zeros_like(cnt_ref)

    @pl.when(is_prompt & (j == 0))
    def _():
        pcarry_ref[...] = jnp.zeros_like(pcarry_ref)

    def finish_tile(invert_previous_part):
        x1 = x_ref[...] + jnp.dot(mix_ref[...], wout_ref[...], preferred_element_type=jnp.float32)
        x1_ref[...] = x1
        h2 = _rmsnorm(x1, n2g_ref[...])
        for s in range(ROW_TILES):
            h2_ref[pl.ds(s, TM, stride=ROW_TILES), :] = h2[:, s * LANES:(s + 1) * LANES]
        _route(h2, rwt_ref, rb_ref, su_ref, cnt_ref, meta_ref, gcol_ref)
        counts_ref[0] = cnt_ref[...]
        part_meta[:, pl.ds(pl.multiple_of(tile_in_part * TM, TM), TM)] = meta_ref[0:2 * TOP_K, :]
        if invert_previous_part:
            first_tok = tile_in_part * TM
            for k, dsm in enumerate(dest_smem):
                first_asg = k * t_total + (part - 1) * (part_tiles * TM) + first_tok
                for t in range(TM):
                    asg_ref[0, 0, dsm[0, first_tok + t]] = first_asg + t
        nxt = jnp.minimum(i + 1, pl.num_programs(0) - 1)
        in_proj(jnp.where(nxt < n_prompt_tiles, xpn_ref[0], xsn_ref[...]))

    @pl.when(is_prompt & (part == 0))
    def _():
        _prompt_mixers(j, u_ref[...], v_ref[...], p_ref[...], aws_ref, abias_ref, bw_ref, bscale_ref, mix_ref,
                       pcarry_ref, pstate_ref)
        finish_tile(False)

    @pl.when(is_prompt & (part > 0))
    def _():
        _prompt_mixers(j, u_ref[...], v_ref[...], p_ref[...], aws_ref, abias_ref, bw_ref, bscale_ref, mix_ref,
                       pcarry_ref, pstate_ref)
        finish_tile(True)

    @pl.when(jnp.logical_not(is_prompt))
    def _():
        _sample_mixers(u_ref[...], v_ref[...], p_ref[...], aws_s_ref, abs_s_ref, st_ref, bw_ref, bscale_ref, mix_ref,
                       pp_ref, vp_ref)
        finish_tile(True)


def _segment_offsets(cnt):
    padded = jnp.floor((cnt + (TE - 1)) * (1.0 / TE)) * TE
    sub = lax.broadcasted_iota(jnp.int32, cnt.shape, 0)
    pad_end = padded
    shift = 1
    while shift < N_EXPERTS:
        pad_end = pad_end + jnp.where(sub >= shift, pltpu.roll(pad_end, shift, 0), 0.0)
        shift *= 2
    return padded, pad_end, pad_end - padded


def _stage_inversion(meta_ref, pad_start, dest_vmem, dest_smem, fill_vmem, asg_ref):
    t_part = meta_ref.shape[1]
    base = pad_start.astype(jnp.int32)
    for k in range(TOP_K):
        e = meta_ref[k:k + 1, :]
        d = meta_ref[TOP_K + k:TOP_K + k + 1, :]
        for ex in range(N_EXPERTS):
            row = jnp.concatenate([base[ex:ex + 1, :]] * (t_part // LANES), axis=1)
            d = d + jnp.where(e == ex, row, 0)
        dest_vmem[k] = d
    for k, dsm in enumerate(dest_smem):
        pltpu.sync_copy(dest_vmem.at[k], dsm)
    fill_vmem[...] = jnp.full(fill_vmem.shape, -1, jnp.int32)
    pltpu.sync_copy(fill_vmem, asg_ref)


def _tables_kernel(counts_ref, meta_ref, asg_ref, tab_ref, dest_vmem, fill_vmem, *dest_smem, t_total):
    h = pl.program_id(0)
    t_part = meta_ref.shape[1]
    cnt = counts_ref[0]
    sub = lax.broadcasted_iota(jnp.int32, cnt.shape, 0)
    padded, pad_end, pad_start = _segment_offsets(cnt)

    @pl.when(h == pl.num_programs(0) - 1)
    def _():
        _stage_inversion(meta_ref, pad_start, dest_vmem, dest_smem, fill_vmem, asg_ref)
        unroll = 16
        for k, dsm in enumerate(dest_smem):
            def invert(_, carry, dsm=dsm):
                t, a = carry
                for uu in range(unroll):
                    asg_ref[0, 0, dsm[0, t + uu]] = a + uu
                return t + unroll, a + unroll
            lax.fori_loop(0, t_part // unroll, invert, (jnp.int32(0), k * t_total + h * t_part))

    n_valid = jnp.maximum(pad_end[N_EXPERTS - 1:, :] * (1.0 / TE), 1.0)
    tile = jnp.minimum(lax.broadcasted_iota(jnp.int32, (1, LANES), 1).astype(jnp.float32), n_valid - 1.0)
    tile_start = tile * TE
    tile_e = jnp.minimum(jnp.sum((pad_end <= tile_start).astype(jnp.float32), axis=0, keepdims=True),
                         N_EXPERTS - 1.0)
    is_e = sub.astype(jnp.float32) == tile_e
    seg_start = jnp.sum(jnp.where(is_e, pad_start, 0.0), axis=0, keepdims=True)
    seg_cnt = jnp.sum(jnp.where(is_e, cnt, 0.0), axis=0, keepdims=True)
    rows = jnp.clip(seg_cnt - (tile_start - seg_start), 0.0, TE)
    tab_ref[0, 0:1, :] = tile_e.astype(jnp.int32)
    tab_ref[0, 1:2, :] = rows.astype(jnp.int32)
    tab_ref[0, 2:3, :] = n_valid.astype(jnp.int32)
    tab_ref[0, 3:, :] = jnp.zeros((SUBLANES - 3, LANES), jnp.int32)


def _expert_kernel(te_ref, nv_ref, nrows_ref,
                   src_ref, src_next_ref, dst_ref,
                   h2_ref, w1_ref, w3_ref, w2_ref,
                   out_hbm,
                   xt0, xt1, ob0, ob1, ssem):
    i = pl.program_id(1)
    g = pl.program_id(0) * pl.num_programs(1) + i
    nv = nv_ref[pl.program_id(0)]
    slab = lambda r: pl.ds(pl.multiple_of(r * ROW_TILES, ROW_TILES), ROW_TILES)

    def gather(idx_ref, xt):
        for r in range(TE):
            xt[r * ROW_TILES:(r + 1) * ROW_TILES, :] = h2_ref[slab(idx_ref[0, 0, r]), :]

    def wait_scatter(tile, ob, other, sem):
        n = nrows_ref[tile] * ROW_TILES
        pltpu.make_async_copy(other.at[pl.ds(0, n), :], ob.at[pl.ds(0, n), :], sem).wait()

    def tile_body(xt, xt_next, ob, ob_other, sem, sem_other):
        @pl.when(i >= 2)
        def _():
            wait_scatter(g - 2, ob, ob_other, sem)

        def compute(m):
            gather(src_next_ref, xt_next)
            x = jnp.concatenate([_row_slab(xt, s, m) for s in range(ROW_TILES)], axis=1).astype(jnp.bfloat16)
            a = jnp.dot(x, w1_ref[0].astype(jnp.bfloat16), preferred_element_type=jnp.float32)
            b = jnp.dot(x, w3_ref[0].astype(jnp.bfloat16), preferred_element_type=jnp.float32)
            h = (a * jax.nn.sigmoid(a)) * b
            o = jnp.dot(h.astype(jnp.bfloat16), w2_ref[0].astype(jnp.bfloat16), preferred_element_type=jnp.float32)
            for s in range(ROW_TILES):
                ob[pl.ds(s, m, stride=ROW_TILES), :] = o[:, s * LANES:(s + 1) * LANES]

        n = nrows_ref[g]

        @pl.when(n > TE // 2)
        def _():
            compute(TE)

        @pl.when(n <= TE // 2)
        def _():
            compute(TE // 2)

        n_full = n // DMA_UNROLL
        copy = lambda r: pltpu.make_async_copy(ob.at[slab(r), :], out_hbm.at[dst_ref[0, 0, r]], sem)

        def body(rb, carry):
            for uu in range(DMA_UNROLL):
                copy(rb * DMA_UNROLL + uu).start(priority=uu % 2)
            return carry
        lax.fori_loop(0, n_full, body, 0)

        def tail(r, carry):
            copy(r).start(priority=1)
            return carry
        lax.fori_loop(n_full * DMA_UNROLL, n, tail, 0)

        @pl.when(i == nv - 1)
        def _():
            @pl.when(i >= 1)
            def _():
                wait_scatter(g - 1, ob_other, ob, sem_other)
            wait_scatter(g, ob, ob_other, sem)

    @pl.when(i < nv)
    def _():
        @pl.when(i == 0)
        def _():
            gather(src_ref, xt0)

        @pl.when(i % 2 == 0)
        def _():
            tile_body(xt0, xt1, ob0, ob1, ssem.at[0], ssem.at[1])

        @pl.when(i % 2 == 1)
        def _():
            tile_body(xt1, xt0, ob1, ob0, ssem.at[1], ssem.at[0])


def _combine_kernel(x1_ref, gcol_ref, nfg_ref, o0_ref, o1_ref, yp_ref, ys_ref, *, n_prompt_tiles):
    i = pl.program_id(0)
    o0 = jnp.concatenate([_row_slab(o0_ref, s, TC) for s in range(ROW_TILES)], axis=1)
    o1 = jnp.concatenate([_row_slab(o1_ref, s, TC) for s in range(ROW_TILES)], axis=1)
    g = gcol_ref[...]
    moe = g[:, 0:1] * o0 + g[:, 1:2] * o1
    y = _rmsnorm(x1_ref[...] + moe, nfg_ref[...])

    @pl.when(i < n_prompt_tiles)
    def _():
        yp_ref[...] = y

    @pl.when(i >= n_prompt_tiles)
    def _():
        ys_ref[...] = y


def _const_spec(shape):
    return pl.BlockSpec(shape, lambda *_: (0,) * len(shape))


def kernel(x_prompt, x_sample, state_pool, norm1_g, w_in, a_norm_g, a_norm_b, a_ws, a_bs, b_w, b_scale, w_out,
           norm2_g, r1_w, r1_b, r2_w, r2_b, exp_w1, exp_w3, exp_w2, normf_g):
    f32, bf16, i32 = jnp.float32, jnp.bfloat16, jnp.int32
    n_batch, seq, _ = x_prompt.shape
    dec_batch, dec_seq, _ = x_sample.shape
    assert norm1_g.shape[0] == 1 and seq % TM == 0 and TM % CHUNK == 0
    assert dec_seq * SEQ_BLK == TM and dec_batch % SEQ_BLK == 0 and dec_seq <= CHUNK
    t_prompt = n_batch * seq
    t_sample = dec_batch * dec_seq
    t_total = t_prompt + t_sample
    n_tok_tiles = t_total // TM
    n_prompt_tiles = t_prompt // TM
    n_sample_tiles = t_sample // TM
    seq_tiles = seq // TM
    plane_rows = t_total * ROW_TILES
    assert n_tok_tiles % N_PARTS == 0
    part_tiles = n_tok_tiles // N_PARTS
    t_part = part_tiles * TM
    n_exp_tiles = -(-(TOP_K * t_part + N_EXPERTS * (TE - 1)) // TE)
    p_rows = n_exp_tiles * TE

    n1g = norm1_g[0][None, :]
    n2g = norm2_g[0][None, :]
    nfg = normf_g[None, :]
    win = w_in[0].astype(bf16)
    wout = w_out[0].astype(bf16)
    ang = a_norm_g[0][None, :]
    anb = a_norm_b[0][None, :]
    bw = b_w[0].astype(bf16)
    bscale = b_scale[0][None, :]
    abias = jnp.repeat(a_bs[0][:, :CHUNK].T, HEAD_DIM, axis=1)
    rw = jnp.concatenate([r1_w[0], r2_w[0].transpose(1, 0, 2).reshape(D_MODEL, N_EXPERTS),
                          jnp.zeros((D_MODEL, N_ROUTER_ROWS - N_GROUPS - N_EXPERTS), f32)], axis=1)
    rw_hi = rw.astype(bf16)
    rw_lo = (rw - rw_hi.astype(f32)).astype(bf16)
    rwt = jnp.concatenate([rw_hi, rw_lo, jnp.zeros((D_MODEL, LANES - 2 * N_ROUTER_ROWS), bf16)], axis=1)
    rbias = jnp.concatenate([r1_b[0], r2_b[0].reshape(-1),
                             jnp.zeros((N_ROUTER_ROWS - N_GROUPS - N_EXPERTS,), f32)])
    rb = jnp.broadcast_to(rbias[:, None], (N_ROUTER_ROWS, TM))
    su = (jnp.arange(TM)[:, None] < jnp.arange(TM)[None, :]).astype(bf16)
    xs_planes = (x_sample.reshape(n_sample_tiles, SEQ_BLK, dec_seq, D_MODEL)
                 .transpose(0, 2, 1, 3).reshape(t_sample, D_MODEL))
    st_planes = state_pool[0].reshape(n_sample_tiles, SEQ_BLK, POOL_STATE, B_WIDTH).transpose(0, 2, 1, 3)
    aws_s = a_ws[0][:, :dec_seq, :dec_seq].reshape(-1)
    abs_s = a_bs[0][:, :dec_seq].reshape(-1)

    cparams = pltpu.CompilerParams(dimension_semantics=("arbitrary",), vmem_limit_bytes=VMEM_LIMIT)

    p_tile = lambda i: jnp.minimum(i, n_prompt_tiles - 1)
    s_tile = lambda i: jnp.maximum(i - n_prompt_tiles, 0)
    plane_shape = jax.ShapeDtypeStruct((n_sample_tiles, dec_seq, SEQ_BLK, A_WIDTH), f32)
    assert N_PARTS >= 2 and n_prompt_tiles >= part_tiles and t_part % TM == 0
    x1, h2, meta, gcol, counts, pstate, pplanes, vplanes, asg_head = pl.pallas_call(
        functools.partial(_mixer_kernel, n_prompt_tiles=n_prompt_tiles, seq_tiles=seq_tiles, part_tiles=part_tiles,
                          t_total=t_total),
        grid_spec=pltpu.PrefetchScalarGridSpec(
            num_scalar_prefetch=2,
            grid=(n_tok_tiles,),
            in_specs=[
                pl.BlockSpec((1, TM, D_MODEL), lambda i, *_: (0, 0, 0)),
                pl.BlockSpec((1, TM, D_MODEL),
                             lambda i, *_: (p_tile(i + 1) // seq_tiles, p_tile(i + 1) % seq_tiles, 0)),
                pl.BlockSpec((TM, D_MODEL), lambda i, *_: (jnp.minimum(s_tile(i + 1), n_sample_tiles - 1), 0)),
                pl.BlockSpec((1, POOL_STATE, SEQ_BLK, B_WIDTH), lambda i, *_: (s_tile(i), 0, 0, 0)),
                _const_spec((1, D_MODEL)), _const_spec((D_MODEL, 3 * A_WIDTH)), _const_spec((1, A_WIDTH)),
                _const_spec((1, A_WIDTH)), _const_spec((N_HEADS, CHUNK, CHUNK)), _const_spec((CHUNK, A_WIDTH)),
                _const_spec((N_GROUPS, HEAD_DIM, HEAD_DIM)), _const_spec((1, B_WIDTH)),
                _const_spec((D_MODEL, D_MODEL)), _const_spec((1, D_MODEL)), _const_spec((D_MODEL, LANES)),
                _const_spec((N_ROUTER_ROWS, TM)), _const_spec((TM, TM)),
            ],
            out_specs=[
                pl.BlockSpec((TM, D_MODEL), lambda i, *_: (i, 0)),
                pl.BlockSpec((TM * ROW_TILES, LANES), lambda i, *_: (i, 0)),
                pl.BlockSpec((SUBLANES, TM), lambda i, *_: (0, i)),
                pl.BlockSpec((TM, LANES), lambda i, *_: (i, 0)),
                pl.BlockSpec((1, N_EXPERTS, LANES), lambda i, *_: (i // part_tiles, 0, 0)),
                pl.BlockSpec((1, CARRY_ROWS, B_WIDTH), lambda i, *_: (p_tile(i) // seq_tiles, 0, 0)),
                pl.BlockSpec((1, dec_seq, SEQ_BLK, A_WIDTH), lambda i, *_: (s_tile(i), 0, 0, 0)),
                pl.BlockSpec((1, dec_seq, SEQ_BLK, A_WIDTH), lambda i, *_: (s_tile(i), 0, 0, 0)),
                pl.BlockSpec((1, 1, p_rows), lambda i, *_: (jnp.maximum(i // part_tiles - 1, 0), 0, 0),
                             memory_space=pltpu.SMEM),
            ],
            scratch_shapes=[pltpu.VMEM((TM, D_MODEL), bf16), pltpu.VMEM((CARRY_ROWS, B_WIDTH), f32),
                            pltpu.VMEM((N_EXPERTS, LANES), f32), pltpu.VMEM((TM, D_MODEL), f32),
                            pltpu.VMEM((TM, A_WIDTH), f32), pltpu.VMEM((TM, A_WIDTH), f32),
                            pltpu.VMEM((TM, B_WIDTH), f32),
                            pltpu.VMEM((2 * TOP_K, t_part), i32), pltpu.VMEM((TOP_K, 1, t_part), i32),
                            pltpu.VMEM((1, 1, p_rows), i32)] + [pltpu.SMEM((1, t_part), i32)] * TOP_K,
        ),
        out_shape=[
            jax.ShapeDtypeStruct((t_total, D_MODEL), f32),
            jax.ShapeDtypeStruct((plane_rows, LANES), f32),
            jax.ShapeDtypeStruct((SUBLANES, t_total), i32),
            jax.ShapeDtypeStruct((t_total, LANES), f32),
            jax.ShapeDtypeStruct((N_PARTS, N_EXPERTS, LANES), f32),
            jax.ShapeDtypeStruct((n_batch, CARRY_ROWS, B_WIDTH), f32),
            plane_shape, plane_shape,
            jax.ShapeDtypeStruct((N_PARTS - 1, 1, p_rows), i32),
        ],
        compiler_params=cparams,
        name="mixer",
    )(aws_s, abs_s, x_prompt, x_prompt, xs_planes, st_planes, n1g, win, ang, anb, a_ws[0][:, :CHUNK, :CHUNK], abias,
      bw, bscale, wout, n2g, rwt, rb, su)

    assert n_exp_tiles <= LANES and t_part % LANES == 0
    asg_tail, tab = pl.pallas_call(
        functools.partial(_tables_kernel, t_total=t_total),
        grid=(N_PARTS,),
        in_specs=[pl.BlockSpec((1, N_EXPERTS, LANES), lambda h: (h, 0, 0)),
                  pl.BlockSpec((SUBLANES, t_part), lambda h: (0, h))],
        out_specs=[pl.BlockSpec((1, 1, p_rows), lambda h: (0, 0, 0), memory_space=pltpu.SMEM),
                   pl.BlockSpec((1, SUBLANES, LANES), lambda h: (h, 0, 0))],
        scratch_shapes=[pltpu.VMEM((TOP_K, 1, t_part), i32), pltpu.VMEM((1, 1, p_rows), i32)]
        + [pltpu.SMEM((1, t_part), i32)] * TOP_K,
        out_shape=[jax.ShapeDtypeStruct((1, 1, p_rows), i32),
                   jax.ShapeDtypeStruct((N_PARTS, SUBLANES, LANES), i32)],
        compiler_params=cparams,
        name="route_tables",
    )(counts, meta)
    tile_e = tab[:, 0, :n_exp_tiles].reshape(-1)
    tile_rows = tab[:, 1, :n_exp_tiles].reshape(-1)
    n_valid = tab[:, 2, 0]
    asg = jnp.concatenate([asg_head.reshape(-1), asg_tail.reshape(-1)])
    row_part = jnp.arange(N_PARTS * p_rows, dtype=i32) // p_rows
    row_tok = jnp.where(asg >= t_total, asg - t_total, asg)
    row_src = jnp.where(asg >= 0, row_tok - row_part * t_part, 0).reshape(N_PARTS * n_exp_tiles, 1, TE)
    row_dst = jnp.maximum(asg, 0).reshape(N_PARTS * n_exp_tiles, 1, TE)

    flat = lambda h, i: h * n_exp_tiles + i
    cur_blk = lambda h, i, te, nv, nr: (flat(h, jnp.minimum(i, nv[h] - 1)), 0, 0)
    nxt_blk = lambda h, i, te, nv, nr: (flat(h, jnp.minimum(i + 1, nv[h] - 1)), 0, 0)
    w_blk = lambda h, i, te, nv, nr: (te[flat(h, i)], 0, 0)
    smem_blk = lambda imap: pl.BlockSpec((1, 1, TE), imap, memory_space=pltpu.SMEM)
    row_buf = pltpu.VMEM((TE * ROW_TILES, LANES), f32)
    out_tok = pl.pallas_call(
        _expert_kernel,
        grid_spec=pltpu.PrefetchScalarGridSpec(
            num_scalar_prefetch=3,
            grid=(N_PARTS, n_exp_tiles),
            in_specs=[smem_blk(cur_blk), smem_blk(nxt_blk), smem_blk(cur_blk),
                      pl.BlockSpec((t_part * ROW_TILES, LANES), lambda h, i, *_: (h, 0),
                                   pipeline_mode=pl.Buffered(1)),
                      pl.BlockSpec((1, D_MODEL, D_EXPERT), w_blk),
                      pl.BlockSpec((1, D_MODEL, D_EXPERT), w_blk),
                      pl.BlockSpec((1, D_EXPERT, D_MODEL), w_blk)],
            out_specs=pl.BlockSpec(memory_space=pl.ANY),
            scratch_shapes=[row_buf, row_buf, row_buf, row_buf, pltpu.SemaphoreType.DMA((2,))],
        ),
        out_shape=jax.ShapeDtypeStruct((TOP_K * t_total, ROW_TILES, LANES), f32),
        compiler_params=pltpu.CompilerParams(dimension_semantics=("arbitrary", "arbitrary"),
                                             vmem_limit_bytes=EXPERT_VMEM_LIMIT),
        name="moe_experts",
    )(tile_e, n_valid, tile_rows, row_src, row_src, row_dst, h2, exp_w1[0], exp_w3[0], exp_w2[0])
    out_tok = out_tok.reshape(TOP_K * plane_rows, LANES)

    assert t_prompt % TC == 0 and t_sample % TC == 0
    n_c_prompt = t_prompt // TC
    y_prompt, y_sample = pl.pallas_call(
        functools.partial(_combine_kernel, n_prompt_tiles=n_c_prompt),
        grid=(t_total // TC,),
        in_specs=[pl.BlockSpec((TC, D_MODEL), lambda i: (i, 0)),
                  pl.BlockSpec((TC, LANES), lambda i: (i, 0)),
                  _const_spec((1, D_MODEL)),
                  pl.BlockSpec((TC * ROW_TILES, LANES), lambda i: (i, 0)),
                  pl.BlockSpec((TC * ROW_TILES, LANES), lambda i: (t_total // TC + i, 0))],
        out_specs=[pl.BlockSpec((TC, D_MODEL), lambda i: (jnp.minimum(i, n_c_prompt - 1), 0)),
                   pl.BlockSpec((TC, D_MODEL), lambda i: (jnp.maximum(i - n_c_prompt, 0), 0))],
        out_shape=[jax.ShapeDtypeStruct((t_prompt, D_MODEL), f32), jax.ShapeDtypeStruct((t_sample, D_MODEL), f32)],
        compiler_params=cparams,
        name="moe_combine",
    )(x1, gcol, nfg, out_tok, out_tok)

    unplane = lambda a: a.transpose(0, 2, 1, 3).reshape(dec_batch, dec_seq, a.shape[-1])
    y_prompt = y_prompt.reshape(n_batch, seq, D_MODEL)
    y_sample = unplane(y_sample.reshape(n_sample_tiles, dec_seq, SEQ_BLK, D_MODEL))
    pool_state_prompt = pstate[None, :, CARRY_ROWS - POOL_STATE:, :]
    p_s = unplane(pplanes)
    pool_state_sample = jnp.concatenate([state_pool[0], p_s], axis=1)[None, :, -POOL_STATE:, :]
    chunk_v_sample = unplane(vplanes)[None]
    return (y_prompt, y_sample, pool_state_prompt, pool_state_sample, chunk_v_sample)
```

```python
import functools
import math

import jax
import jax.numpy as jnp
from jax import lax
from jax.experimental import pallas as pl
from jax.experimental.pallas import tpu as pltpu

D_MODEL = 1024
A_WIDTH = 512
B_WIDTH = 512
N_HEADS = 4
HEAD_DIM = 128
CHUNK = 128
POOL_WINDOWS = (2, 4, 8, 16)
POOL_STATE = 15
N_GROUPS = 4
EXPERTS_PER_GROUP = 8
N_EXPERTS = 32
TOP_K = 2
D_EXPERT = 512
EPS = 1e-6

SUBLANES = 8
LANES = 128
ROW_TILES = D_MODEL // LANES

TM = 256
TC = 512
TE = 512
SEQ_BLK = 32
N_ROUTER_ROWS = 40
CARRY_ROWS = 16
DMA_UNROLL = 32
VMEM_LIMIT = 48 * 1024 * 1024
N_PARTS = 2
EXPERT_VMEM_LIMIT = 62 * 1024 * 1024

_INV_SQRT2 = 1.0 / math.sqrt(2.0)


def _rmsnorm(x, g):
    r = lax.rsqrt(jnp.mean(x * x, axis=-1, keepdims=True) + EPS)
    return (x * r) * g


def _gelu(x):
    return 0.5 * x * (1.0 + lax.erf(x * _INV_SQRT2))


def _layernorm(x, g, b):
    mu = jnp.mean(x, axis=-1, keepdims=True)
    xc = x - mu
    var = jnp.mean(xc * xc, axis=-1, keepdims=True)
    return (xc * lax.rsqrt(var + EPS)) * g + b


def _row_slab(ref, s, n):
    return ref[pl.ds(s, n, stride=ROW_TILES), :]


def _pool_project(pooled, g, bw_ref, bscale_ref):
    lo, hi = g * HEAD_DIM, (g + 1) * HEAD_DIM
    hb = jnp.dot(pooled.astype(jnp.bfloat16), bw_ref[g], preferred_element_type=jnp.float32)
    return hb * bscale_ref[:, lo:hi]


def _prompt_mixers(j, u, v, p, aws_ref, abias_ref, bw_ref, bscale_ref, mix_ref, pcarry_ref, pstate_ref):
    tri = (lax.broadcasted_iota(jnp.int32, (CHUNK, CHUNK), 0)
           >= lax.broadcasted_iota(jnp.int32, (CHUNK, CHUNK), 1))
    vb = v.astype(jnp.bfloat16)
    for hd in range(N_HEADS):
        lo, hi = hd * HEAD_DIM, (hd + 1) * HEAD_DIM
        w = jnp.where(tri, aws_ref[hd], 0.0).astype(jnp.bfloat16)
        for c in range(TM // CHUNK):
            r0, r1 = c * CHUNK, (c + 1) * CHUNK
            z = jnp.dot(w, vb[r0:r1, lo:hi], preferred_element_type=jnp.float32) + abias_ref[:, lo:hi]
            mix_ref[r0:r1, lo:hi] = (u[r0:r1, lo:hi] * z).astype(jnp.bfloat16)

    head_pos = j * TM + lax.broadcasted_iota(jnp.int32, (CARRY_ROWS, LANES), 0)
    for g, w in enumerate(POOL_WINDOWS):
        lo, hi = g * HEAD_DIM, (g + 1) * HEAD_DIM
        pg = p[:, lo:hi]
        acc = jnp.concatenate([pcarry_ref[:, lo:hi], pg], axis=0)
        shift = 1
        while shift < w:
            acc = acc + pltpu.roll(acc, shift, 0)
            shift *= 2
        head = acc[CARRY_ROWS:2 * CARRY_ROWS, :] / jnp.minimum(head_pos + 1, w).astype(jnp.float32)
        mean = jnp.concatenate([head, acc[2 * CARRY_ROWS:, :] * (1.0 / w)], axis=0)
        pooled = mean - pg
        mix_ref[:, A_WIDTH + lo:A_WIDTH + hi] = _pool_project(pooled, g, bw_ref, bscale_ref).astype(jnp.bfloat16)
    tail = p[TM - CARRY_ROWS:, :]
    pcarry_ref[...] = tail
    pstate_ref[0] = tail


def _sample_mixers(u, v, p, aws_ref, abs_ref, st_ref, bw_ref, bscale_ref, mix_ref, pp_ref, vp_ref):
    n_pos = TM // SEQ_BLK
    for i in range(n_pos):
        vp_ref[0, i] = v[i * SEQ_BLK:(i + 1) * SEQ_BLK, :]
        pp_ref[0, i] = p[i * SEQ_BLK:(i + 1) * SEQ_BLK, :]

    for hd in range(N_HEADS):
        lo, hi = hd * HEAD_DIM, (hd + 1) * HEAD_DIM
        vplanes = [v[s * SEQ_BLK:(s + 1) * SEQ_BLK, lo:hi] for s in range(n_pos)]
        for i in range(n_pos):
            z = vplanes[0] * aws_ref[(hd * n_pos + i) * n_pos]
            for s in range(1, i + 1):
                z = z + vplanes[s] * aws_ref[(hd * n_pos + i) * n_pos + s]
            z = z + abs_ref[hd * n_pos + i]
            r0, r1 = i * SEQ_BLK, (i + 1) * SEQ_BLK
            mix_ref[r0:r1, lo:hi] = (u[r0:r1, lo:hi] * z).astype(jnp.bfloat16)

    for g, w in enumerate(POOL_WINDOWS):
        lo, hi = g * HEAD_DIM, (g + 1) * HEAD_DIM
        planes = [st_ref[0, k, :, lo:hi] for k in range(POOL_STATE)]
        planes += [p[i * SEQ_BLK:(i + 1) * SEQ_BLK, lo:hi] for i in range(n_pos)]
        pooled = []
        for i in range(n_pos):
            top = POOL_STATE + i
            s = planes[top - w + 1]
            for k in range(top - w + 2, top + 1):
                s = s + planes[k]
            pooled.append(s * (1.0 / w) - planes[top])
        pooled = jnp.concatenate(pooled, axis=0)
        mix_ref[:, A_WIDTH + lo:A_WIDTH + hi] = _pool_project(pooled, g, bw_ref, bscale_ref).astype(jnp.bfloat16)


def _route(h2, rwt_ref, rb_ref, su_ref, cnt_ref, meta_ref, gcol_ref):
    tm = h2.shape[0]
    h_hi = h2.astype(jnp.bfloat16)
    h_lo = (h2 - h_hi.astype(jnp.float32)).astype(jnp.bfloat16)
    s = (jnp.dot(h_hi, rwt_ref[...], preferred_element_type=jnp.float32)
         + jnp.dot(h_lo, rwt_ref[...], preferred_element_type=jnp.float32))
    st = s.T
    lt = st[0:N_ROUTER_ROWS, :] + st[N_ROUTER_ROWS:2 * N_ROUTER_ROWS, :] + rb_ref[...]
    row = lambda i: lt[i:i + 1, :]
    l1 = [row(i) for i in range(N_GROUPS)]
    m1 = jnp.maximum(jnp.maximum(l1[0], l1[1]), jnp.maximum(l1[2], l1[3]))
    grp = jnp.where(l1[0] == m1, 0, jnp.where(l1[1] == m1, 1, jnp.where(l1[2] == m1, 2, 3)))
    se = (jnp.exp(l1[0] - m1) + jnp.exp(l1[1] - m1)) + (jnp.exp(l1[2] - m1) + jnp.exp(l1[3] - m1))
    pg = 1.0 / se
    l2 = []
    for e in range(EXPERTS_PER_GROUP):
        c = [row(N_GROUPS + g * EXPERTS_PER_GROUP + e) for g in range(N_GROUPS)]
        l2.append(jnp.where(grp == 0, c[0], jnp.where(grp == 1, c[1], jnp.where(grp == 2, c[2], c[3]))))
    v0 = functools.reduce(jnp.maximum, l2)
    i0 = jnp.full_like(grp, EXPERTS_PER_GROUP - 1)
    for e in range(EXPERTS_PER_GROUP - 2, -1, -1):
        i0 = jnp.where(l2[e] == v0, e, i0)
    neg = jnp.float32(-jnp.inf)
    l2m = [jnp.where(i0 == e, neg, l2[e]) for e in range(EXPERTS_PER_GROUP)]
    v1 = functools.reduce(jnp.maximum, l2m)
    i1 = jnp.full_like(grp, EXPERTS_PER_GROUP - 1)
    for e in range(EXPERTS_PER_GROUP - 2, -1, -1):
        i1 = jnp.where((l2m[e] == v1) & (i0 != e), e, i1)
    d = jnp.exp(v1 - v0)
    g0 = pg / (1.0 + d)
    g1 = (pg * d) / (1.0 + d)
    e0 = grp * EXPERTS_PER_GROUP + i0
    e1 = grp * EXPERTS_PER_GROUP + i1

    eiota = lax.broadcasted_iota(jnp.int32, (N_EXPERTS, tm), 0)
    hit0 = eiota == e0
    hit1 = eiota == e1
    onehot = (hit0 | hit1).astype(jnp.bfloat16)
    prefix = jnp.dot(onehot, su_ref[...], preferred_element_type=jnp.float32)
    carry = cnt_ref[...]
    base = prefix + jnp.concatenate([carry] * (tm // LANES), axis=1)
    r0 = jnp.sum(jnp.where(hit0, base, 0.0), axis=0, keepdims=True)
    r1 = jnp.sum(jnp.where(hit1, base, 0.0), axis=0, keepdims=True)
    ones = jnp.ones((tm, LANES), jnp.bfloat16)
    cnt_ref[...] = carry + jnp.dot(onehot, ones, preferred_element_type=jnp.float32)

    meta_ref[0:1, :] = e0
    meta_ref[1:2, :] = e1
    meta_ref[2:3, :] = r0.astype(jnp.int32)
    meta_ref[3:4, :] = r1.astype(jnp.int32)
    meta_ref[4:8, :] = jnp.zeros((4, tm), jnp.int32)

    giota = lax.broadcasted_iota(jnp.int32, (LANES, tm), 0)
    gpad = jnp.where(giota == 0, g0, jnp.where(giota == 1, g1, 0.0))
    gcol_ref[...] = gpad.T


def _mixer_kernel(aws_s_ref, abs_s_ref,
                  x0_ref, xpn_ref, xsn_ref,
                  st_ref, n1g_ref, win_ref, ang_ref, anb_ref, aws_ref, abias_ref, bw_ref, bscale_ref,
                  wout_ref, n2g_ref, rwt_ref, rb_ref, su_ref,
                  x1_ref, h2_ref, meta_ref, gcol_ref, counts_ref, pstate_ref, pp_ref, vp_ref, asg_ref,
                  mix_ref, pcarry_ref, cnt_ref, x_ref, u_ref, v_ref, p_ref, part_meta, dest_vmem, fill_vmem, *dest_smem,
                  n_prompt_tiles, seq_tiles, part_tiles, t_total):
    i = pl.program_id(0)
    is_prompt = i < n_prompt_tiles
    j = i % seq_tiles
    part = i // part_tiles
    tile_in_part = i % part_tiles

    @pl.when((i > 0) & (tile_in_part == 0))
    def _():
        _, _, pad_start = _segment_offsets(cnt_ref[...])
        _stage_inversion(part_meta, pad_start, dest_vmem, dest_smem, fill_vmem, asg_ref)

    def in_proj(x):
        h = _rmsnorm(x, n1g_ref[...]).astype(jnp.bfloat16)
        proj = jnp.dot(h, win_ref[...], preferred_element_type=jnp.float32)
        uv = _gelu(proj[:, : 2 * A_WIDTH])
        u_ref[...] = uv[:, :A_WIDTH]
        v_ref[...] = _layernorm(uv[:, A_WIDTH:], ang_ref[...], anb_ref[...])
        p_ref[...] = proj[:, 2 * A_WIDTH:]
        x_ref[...] = x

    @pl.when(i == 0)
    def _():
        in_proj(x0_ref[0])

    @pl.when(i % part_tiles == 0)
    def _():
        cnt_ref[...] = jnp.zeros_like(cnt_ref)

    @pl.when(is_prompt & (j == 0))
    def _():
        pcarry_ref[...] = jnp.zeros_like(pcarry_ref)

    def finish_tile(invert_previous_part):
        x1 = x_ref[...] + jnp.dot(mix_ref[...], wout_ref[...], preferred_element_type=jnp.float32)
        x1_ref[...] = x1
        h2 = _rmsnorm(x1, n2g_ref[...])
        for s in range(ROW_TILES):
            h2_ref[pl.ds(s, TM, stride=ROW_TILES), :] = h2[:, s * LANES:(s + 1) * LANES]
        _route(h2, rwt_ref, rb_ref, su_ref, cnt_ref, meta_ref, gcol_ref)
        counts_ref[0] = cnt_ref[...]
        part_meta[:, pl.ds(pl.multiple_of(tile_in_part * TM, TM), TM)] = meta_ref[0:2 * TOP_K, :]
        if invert_previous_part:
            first_tok = tile_in_part * TM
            for k, dsm in enumerate(dest_smem):
                first_asg = k * t_total + (part - 1) * (part_tiles * TM) + first_tok
                for t in range(TM):
                    asg_ref[0, 0, dsm[0, first_tok + t]] = first_asg + t
        nxt = jnp.minimum(i + 1, pl.num_programs(0) - 1)
        in_proj(jnp.where(nxt < n_prompt_tiles, xpn_ref[0], xsn_ref[...]))

    @pl.when(is_prompt & (part == 0))
    def _():
        _prompt_mixers(j, u_ref[...], v_ref[...], p_ref[...], aws_ref, abias_ref, bw_ref, bscale_ref, mix_ref,
                       pcarry_ref, pstate_ref)
        finish_tile(False)

    @pl.when(is_prompt & (part > 0))
    def _():
        _prompt_mixers(j, u_ref[...], v_ref[...], p_ref[...], aws_ref, abias_ref, bw_ref, bscale_ref, mix_ref,
                       pcarry_ref, pstate_ref)
        finish_tile(True)

    @pl.when(jnp.logical_not(is_prompt))
    def _():
        _sample_mixers(u_ref[...], v_ref[...], p_ref[...], aws_s_ref, abs_s_ref, st_ref, bw_ref, bscale_ref, mix_ref,
                       pp_ref, vp_ref)
        finish_tile(True)


def _segment_offsets(cnt):
    padded = jnp.floor((cnt + (TE - 1)) * (1.0 / TE)) * TE
    sub = lax.broadcasted_iota(jnp.int32, cnt.shape, 0)
    pad_end = padded
    shift = 1
    while shift < N_EXPERTS:
        pad_end = pad_end + jnp.where(sub >= shift, pltpu.roll(pad_end, shift, 0), 0.0)
        shift *= 2
    return padded, pad_end, pad_end - padded


def _dest_rows(meta_ref, pad_start, dest_ref):
    t_part = meta_ref.shape[1]
    base = pad_start.astype(jnp.int32)
    for k in range(TOP_K):
        e = meta_ref[k:k + 1, :]
        d = meta_ref[TOP_K + k:TOP_K + k + 1, :]
        for ex in range(N_EXPERTS):
            row = jnp.concatenate([base[ex:ex + 1, :]] * (t_part // LANES), axis=1)
            d = d + jnp.where(e == ex, row, 0)
        dest_ref[k] = d


def _stage_inversion(meta_ref, pad_start, dest_vmem, dest_smem, fill_vmem, asg_ref):
    _dest_rows(meta_ref, pad_start, dest_vmem)
    for k, dsm in enumerate(dest_smem):
        pltpu.sync_copy(dest_vmem.at[k], dsm)
    fill_vmem[...] = jnp.full(fill_vmem.shape, -1, jnp.int32)
    pltpu.sync_copy(fill_vmem, asg_ref)


def _tables_kernel(counts_ref, meta_ref, dest_ref, tab_ref):
    h = pl.program_id(0)
    cnt = counts_ref[0]
    sub = lax.broadcasted_iota(jnp.int32, cnt.shape, 0)
    padded, pad_end, pad_start = _segment_offsets(cnt)

    @pl.when(h == pl.num_programs(0) - 1)
    def _():
        _dest_rows(meta_ref, pad_start, dest_ref)

    n_valid = jnp.maximum(pad_end[N_EXPERTS - 1:, :] * (1.0 / TE), 1.0)
    tile = jnp.minimum(lax.broadcasted_iota(jnp.int32, (1, LANES), 1).astype(jnp.float32), n_valid - 1.0)
    tile_start = tile * TE
    tile_e = jnp.minimum(jnp.sum((pad_end <= tile_start).astype(jnp.float32), axis=0, keepdims=True),
                         N_EXPERTS - 1.0)
    is_e = sub.astype(jnp.float32) == tile_e
    seg_start = jnp.sum(jnp.where(is_e, pad_start, 0.0), axis=0, keepdims=True)
    seg_cnt = jnp.sum(jnp.where(is_e, cnt, 0.0), axis=0, keepdims=True)
    rows = jnp.clip(seg_cnt - (tile_start - seg_start), 0.0, TE)
    tab_ref[0, 0:1, :] = tile_e.astype(jnp.int32)
    tab_ref[0, 1:2, :] = rows.astype(jnp.int32)
    tab_ref[0, 2:3, :] = n_valid.astype(jnp.int32)
    tab_ref[0, 3:, :] = jnp.zeros((SUBLANES - 3, LANES), jnp.int32)


def _expert_kernel(te_ref, nv_ref, nrows_ref,
                   src_ref, src_next_ref, dst_ref,
                   dest_last_ref,
                   h2_ref, w1_ref, w3_ref, w2_ref,
                   out_hbm,
                   xt0, xt1, ob0, ob1, fill_vmem, asg_last, ssem, *, t_total, t_part):
    h = pl.program_id(0)
    i = pl.program_id(1)
    g = h * pl.num_programs(1) + i
    nv = nv_ref[h]
    slab = lambda r: pl.ds(pl.multiple_of(r * ROW_TILES, ROW_TILES), ROW_TILES)
    toks_per_tile = TE // TOP_K
    inv_tiles = t_part // toks_per_tile

    def last_part_src(tile):
        def src(r):
            a = asg_last[0, tile * TE + r]
            tok = a - jnp.where(a >= t_total, t_total, 0) - t_part
            return jnp.where(a >= 0, tok, 0)
        return src

    first_tables = (lambda r: src_ref[0, 0, r], lambda r: src_next_ref[0, 0, r], lambda r: dst_ref[0, 0, r])
    last_tables = (last_part_src(i), last_part_src(jnp.minimum(i + 1, nv - 1)), lambda r: asg_last[0, i * TE + r])

    def gather(src, xt):
        for r in range(TE):
            xt[r * ROW_TILES:(r + 1) * ROW_TILES, :] = h2_ref[slab(src(r)), :]

    def invert_share():
        first_tok = jnp.minimum(i, inv_tiles - 1) * toks_per_tile
        for k in range(TOP_K):
            first_asg = k * t_total + t_part + first_tok
            for t in range(toks_per_tile):
                asg_last[0, dest_last_ref[k, 0, first_tok + t]] = first_asg + t

    def wait_scatter(tile, ob, other, sem):
        n = nrows_ref[tile] * ROW_TILES
        pltpu.make_async_copy(other.at[pl.ds(0, n), :], ob.at[pl.ds(0, n), :], sem).wait()

    def tile_body(xt, xt_next, ob, ob_other, sem, sem_other, tables, first_part):
        _, src_next, dst = tables

        @pl.when(i >= 2)
        def _():
            wait_scatter(g - 2, ob, ob_other, sem)

        def compute(m):
            gather(src_next, xt_next)
            if first_part:
                invert_share()
            x = jnp.concatenate([_row_slab(xt, s, m) for s in range(ROW_TILES)], axis=1).astype(jnp.bfloat16)
            a = jnp.dot(x, w1_ref[0].astype(jnp.bfloat16), preferred_element_type=jnp.float32)
            b = jnp.dot(x, w3_ref[0].astype(jnp.bfloat16), preferred_element_type=jnp.float32)
            hmid = (a * jax.nn.sigmoid(a)) * b
            o = jnp.dot(hmid.astype(jnp.bfloat16), w2_ref[0].astype(jnp.bfloat16),
                        preferred_element_type=jnp.float32)
            for s in range(ROW_TILES):
                ob[pl.ds(s, m, stride=ROW_TILES), :] = o[:, s * LANES:(s + 1) * LANES]

        n = nrows_ref[g]

        @pl.when(n > TE // 2)
        def _():
            compute(TE)

        @pl.when(n <= TE // 2)
        def _():
            compute(TE // 2)

        n_full = n // DMA_UNROLL
        copy = lambda r: pltpu.make_async_copy(ob.at[slab(r), :], out_hbm.at[dst(r)], sem)

        def body(rb, carry):
            for uu in range(DMA_UNROLL):
                copy(rb * DMA_UNROLL + uu).start(priority=uu % 2)
            return carry
        lax.fori_loop(0, n_full, body, 0)

        def tail(r, carry):
            copy(r).start(priority=1)
            return carry
        lax.fori_loop(n_full * DMA_UNROLL, n, tail, 0)

        @pl.when(i == nv - 1)
        def _():
            @pl.when(i >= 1)
            def _():
                wait_scatter(g - 1, ob_other, ob, sem_other)
            wait_scatter(g, ob, ob_other, sem)

    def part_body(tables, first_part):
        @pl.when(i == 0)
        def _():
            gather(tables[0], xt0)

        @pl.when(i % 2 == 0)
        def _():
            tile_body(xt0, xt1, ob0, ob1, ssem.at[0], ssem.at[1], tables, first_part)

        @pl.when(i % 2 == 1)
        def _():
            tile_body(xt1, xt0, ob1, ob0, ssem.at[1], ssem.at[0], tables, first_part)

    @pl.when((h == 0) & (i == 0))
    def _():
        fill_vmem[...] = jnp.full(fill_vmem.shape, -1, jnp.int32)
        pltpu.sync_copy(fill_vmem, asg_last)

    @pl.when((i < nv) & (h == 0))
    def _():
        part_body(first_tables, True)

    @pl.when((i < nv) & (h > 0))
    def _():
        part_body(last_tables, False)


def _combine_kernel(x1_ref, gcol_ref, nfg_ref, o0_ref, o1_ref, yp_ref, ys_ref, *, n_prompt_tiles):
    i = pl.program_id(0)
    o0 = jnp.concatenate([_row_slab(o0_ref, s, TC) for s in range(ROW_TILES)], axis=1)
    o1 = jnp.concatenate([_row_slab(o1_ref, s, TC) for s in range(ROW_TILES)], axis=1)
    g = gcol_ref[...]
    moe = g[:, 0:1] * o0 + g[:, 1:2] * o1
    y = _rmsnorm(x1_ref[...] + moe, nfg_ref[...])

    @pl.when(i < n_prompt_tiles)
    def _():
        yp_ref[...] = y

    @pl.when(i >= n_prompt_tiles)
    def _():
        ys_ref[...] = y


def _const_spec(shape):
    return pl.BlockSpec(shape, lambda *_: (0,) * len(shape))


def kernel(x_prompt, x_sample, state_pool, norm1_g, w_in, a_norm_g, a_norm_b, a_ws, a_bs, b_w, b_scale, w_out,
           norm2_g, r1_w, r1_b, r2_w, r2_b, exp_w1, exp_w3, exp_w2, normf_g):
    f32, bf16, i32 = jnp.float32, jnp.bfloat16, jnp.int32
    n_batch, seq, _ = x_prompt.shape
    dec_batch, dec_seq, _ = x_sample.shape
    assert norm1_g.shape[0] == 1 and seq % TM == 0 and TM % CHUNK == 0
    assert dec_seq * SEQ_BLK == TM and dec_batch % SEQ_BLK == 0 and dec_seq <= CHUNK
    t_prompt = n_batch * seq
    t_sample = dec_batch * dec_seq
    t_total = t_prompt + t_sample
    n_tok_tiles = t_total // TM
    n_prompt_tiles = t_prompt // TM
    n_sample_tiles = t_sample // TM
    seq_tiles = seq // TM
    plane_rows = t_total * ROW_TILES
    assert n_tok_tiles % N_PARTS == 0
    part_tiles = n_tok_tiles // N_PARTS
    t_part = part_tiles * TM
    n_exp_tiles = -(-(TOP_K * t_part + N_EXPERTS * (TE - 1)) // TE)
    p_rows = n_exp_tiles * TE

    n1g = norm1_g[0][None, :]
    n2g = norm2_g[0][None, :]
    nfg = normf_g[None, :]
    win = w_in[0].astype(bf16)
    wout = w_out[0].astype(bf16)
    ang = a_norm_g[0][None, :]
    anb = a_norm_b[0][None, :]
    bw = b_w[0].astype(bf16)
    bscale = b_scale[0][None, :]
    abias = jnp.repeat(a_bs[0][:, :CHUNK].T, HEAD_DIM, axis=1)
    rw = jnp.concatenate([r1_w[0], r2_w[0].transpose(1, 0, 2).reshape(D_MODEL, N_EXPERTS),
                          jnp.zeros((D_MODEL, N_ROUTER_ROWS - N_GROUPS - N_EXPERTS), f32)], axis=1)
    rw_hi = rw.astype(bf16)
    rw_lo = (rw - rw_hi.astype(f32)).astype(bf16)
    rwt = jnp.concatenate([rw_hi, rw_lo, jnp.zeros((D_MODEL, LANES - 2 * N_ROUTER_ROWS), bf16)], axis=1)
    rbias = jnp.concatenate([r1_b[0], r2_b[0].reshape(-1),
                             jnp.zeros((N_ROUTER_ROWS - N_GROUPS - N_EXPERTS,), f32)])
    rb = jnp.broadcast_to(rbias[:, None], (N_ROUTER_ROWS, TM))
    su = (jnp.arange(TM)[:, None] < jnp.arange(TM)[None, :]).astype(bf16)
    xs_planes = (x_sample.reshape(n_sample_tiles, SEQ_BLK, dec_seq, D_MODEL)
                 .transpose(0, 2, 1, 3).reshape(t_sample, D_MODEL))
    st_planes = state_pool[0].reshape(n_sample_tiles, SEQ_BLK, POOL_STATE, B_WIDTH).transpose(0, 2, 1, 3)
    aws_s = a_ws[0][:, :dec_seq, :dec_seq].reshape(-1)
    abs_s = a_bs[0][:, :dec_seq].reshape(-1)

    cparams = pltpu.CompilerParams(dimension_semantics=("arbitrary",), vmem_limit_bytes=VMEM_LIMIT)

    p_tile = lambda i: jnp.minimum(i, n_prompt_tiles - 1)
    s_tile = lambda i: jnp.maximum(i - n_prompt_tiles, 0)
    plane_shape = jax.ShapeDtypeStruct((n_sample_tiles, dec_seq, SEQ_BLK, A_WIDTH), f32)
    assert N_PARTS >= 2 and n_prompt_tiles >= part_tiles and t_part % TM == 0
    x1, h2, meta, gcol, counts, pstate, pplanes, vplanes, asg_head = pl.pallas_call(
        functools.partial(_mixer_kernel, n_prompt_tiles=n_prompt_tiles, seq_tiles=seq_tiles, part_tiles=part_tiles,
                          t_total=t_total),
        grid_spec=pltpu.PrefetchScalarGridSpec(
            num_scalar_prefetch=2,
            grid=(n_tok_tiles,),
            in_specs=[
                pl.BlockSpec((1, TM, D_MODEL), lambda i, *_: (0, 0, 0)),
                pl.BlockSpec((1, TM, D_MODEL),
                             lambda i, *_: (p_tile(i + 1) // seq_tiles, p_tile(i + 1) % seq_tiles, 0)),
                pl.BlockSpec((TM, D_MODEL), lambda i, *_: (jnp.minimum(s_tile(i + 1), n_sample_tiles - 1), 0)),
                pl.BlockSpec((1, POOL_STATE, SEQ_BLK, B_WIDTH), lambda i, *_: (s_tile(i), 0, 0, 0)),
                _const_spec((1, D_MODEL)), _const_spec((D_MODEL, 3 * A_WIDTH)), _const_spec((1, A_WIDTH)),
                _const_spec((1, A_WIDTH)), _const_spec((N_HEADS, CHUNK, CHUNK)), _const_spec((CHUNK, A_WIDTH)),
                _const_spec((N_GROUPS, HEAD_DIM, HEAD_DIM)), _const_spec((1, B_WIDTH)),
                _const_spec((D_MODEL, D_MODEL)), _const_spec((1, D_MODEL)), _const_spec((D_MODEL, LANES)),
                _const_spec((N_ROUTER_ROWS, TM)), _const_spec((TM, TM)),
            ],
            out_specs=[
                pl.BlockSpec((TM, D_MODEL), lambda i, *_: (i, 0)),
                pl.BlockSpec((TM * ROW_TILES, LANES), lambda i, *_: (i, 0)),
                pl.BlockSpec((SUBLANES, TM), lambda i, *_: (0, i)),
                pl.BlockSpec((TM, LANES), lambda i, *_: (i, 0)),
                pl.BlockSpec((1, N_EXPERTS, LANES), lambda i, *_: (i // part_tiles, 0, 0)),
                pl.BlockSpec((1, CARRY_ROWS, B_WIDTH), lambda i, *_: (p_tile(i) // seq_tiles, 0, 0)),
                pl.BlockSpec((1, dec_seq, SEQ_BLK, A_WIDTH), lambda i, *_: (s_tile(i), 0, 0, 0)),
                pl.BlockSpec((1, dec_seq, SEQ_BLK, A_WIDTH), lambda i, *_: (s_tile(i), 0, 0, 0)),
                pl.BlockSpec((1, 1, p_rows), lambda i, *_: (jnp.maximum(i // part_tiles - 1, 0), 0, 0),
                             memory_space=pltpu.SMEM),
            ],
            scratch_shapes=[pltpu.VMEM((TM, D_MODEL), bf16), pltpu.VMEM((CARRY_ROWS, B_WIDTH), f32),
                            pltpu.VMEM((N_EXPERTS, LANES), f32), pltpu.VMEM((TM, D_MODEL), f32),
                            pltpu.VMEM((TM, A_WIDTH), f32), pltpu.VMEM((TM, A_WIDTH), f32),
                            pltpu.VMEM((TM, B_WIDTH), f32),
                            pltpu.VMEM((2 * TOP_K, t_part), i32), pltpu.VMEM((TOP_K, 1, t_part), i32),
                            pltpu.VMEM((1, 1, p_rows), i32)] + [pltpu.SMEM((1, t_part), i32)] * TOP_K,
        ),
        out_shape=[
            jax.ShapeDtypeStruct((t_total, D_MODEL), f32),
            jax.ShapeDtypeStruct((plane_rows, LANES), f32),
            jax.ShapeDtypeStruct((SUBLANES, t_total), i32),
            jax.ShapeDtypeStruct((t_total, LANES), f32),
            jax.ShapeDtypeStruct((N_PARTS, N_EXPERTS, LANES), f32),
            jax.ShapeDtypeStruct((n_batch, CARRY_ROWS, B_WIDTH), f32),
            plane_shape, plane_shape,
            jax.ShapeDtypeStruct((N_PARTS - 1, 1, p_rows), i32),
        ],
        compiler_params=cparams,
        name="mixer",
    )(aws_s, abs_s, x_prompt, x_prompt, xs_planes, st_planes, n1g, win, ang, anb, a_ws[0][:, :CHUNK, :CHUNK], abias,
      bw, bscale, wout, n2g, rwt, rb, su)

    assert n_exp_tiles <= LANES and t_part % LANES == 0
    dest_last, tab = pl.pallas_call(
        _tables_kernel,
        grid=(N_PARTS,),
        in_specs=[pl.BlockSpec((1, N_EXPERTS, LANES), lambda h: (h, 0, 0)),
                  pl.BlockSpec((SUBLANES, t_part), lambda h: (0, h))],
        out_specs=[_const_spec((TOP_K, 1, t_part)),
                   pl.BlockSpec((1, SUBLANES, LANES), lambda h: (h, 0, 0))],
        out_shape=[jax.ShapeDtypeStruct((TOP_K, 1, t_part), i32),
                   jax.ShapeDtypeStruct((N_PARTS, SUBLANES, LANES), i32)],
        compiler_params=cparams,
        name="route_tables",
    )(counts, meta)
    tile_e = tab[:, 0, :n_exp_tiles].reshape(-1)
    tile_rows = tab[:, 1, :n_exp_tiles].reshape(-1)
    n_valid = tab[:, 2, 0]
    assert N_PARTS == 2 and (TOP_K * t_part) % TE == 0
    asg = asg_head.reshape(-1)
    row_tok = jnp.where(asg >= t_total, asg - t_total, asg)
    row_src = jnp.where(asg >= 0, row_tok, 0).reshape(n_exp_tiles, 1, TE)
    row_dst = jnp.maximum(asg, 0).reshape(n_exp_tiles, 1, TE)

    flat = lambda h, i: h * n_exp_tiles + i
    cur_blk = lambda h, i, te, nv, nr: (jnp.minimum(i, nv[0] - 1), 0, 0)
    nxt_blk = lambda h, i, te, nv, nr: (jnp.minimum(i + 1, nv[0] - 1), 0, 0)
    w_blk = lambda h, i, te, nv, nr: (te[flat(h, i)], 0, 0)
    smem_blk = lambda imap: pl.BlockSpec((1, 1, TE), imap, memory_space=pltpu.SMEM)
    row_buf = pltpu.VMEM((TE * ROW_TILES, LANES), f32)
    out_tok = pl.pallas_call(
        functools.partial(_expert_kernel, t_total=t_total, t_part=t_part),
        grid_spec=pltpu.PrefetchScalarGridSpec(
            num_scalar_prefetch=3,
            grid=(N_PARTS, n_exp_tiles),
            in_specs=[smem_blk(cur_blk), smem_blk(nxt_blk), smem_blk(cur_blk),
                      pl.BlockSpec((TOP_K, 1, t_part), lambda h, i, *_: (0, 0, 0), memory_space=pltpu.SMEM),
                      pl.BlockSpec((t_part * ROW_TILES, LANES), lambda h, i, *_: (h, 0),
                                   pipeline_mode=pl.Buffered(1)),
                      pl.BlockSpec((1, D_MODEL, D_EXPERT), w_blk),
                      pl.BlockSpec((1, D_MODEL, D_EXPERT), w_blk),
                      pl.BlockSpec((1, D_EXPERT, D_MODEL), w_blk)],
            out_specs=pl.BlockSpec(memory_space=pl.ANY),
            scratch_shapes=[row_buf, row_buf, row_buf, row_buf, pltpu.VMEM((1, p_rows), i32),
                            pltpu.SMEM((1, p_rows), i32), pltpu.SemaphoreType.DMA((2,))],
        ),
        out_shape=jax.ShapeDtypeStruct((TOP_K * t_total, ROW_TILES, LANES), f32),
        compiler_params=pltpu.CompilerParams(dimension_semantics=("arbitrary", "arbitrary"),
                                             vmem_limit_bytes=EXPERT_VMEM_LIMIT),
        name="moe_experts",
    )(tile_e, n_valid, tile_rows, row_src, row_src, row_dst, dest_last, h2, exp_w1[0], exp_w3[0], exp_w2[0])
    out_tok = out_tok.reshape(TOP_K * plane_rows, LANES)

    assert t_prompt % TC == 0 and t_sample % TC == 0
    n_c_prompt = t_prompt // TC
    y_prompt, y_sample = pl.pallas_call(
        functools.partial(_combine_kernel, n_prompt_tiles=n_c_prompt),
        grid=(t_total // TC,),
        in_specs=[pl.BlockSpec((TC, D_MODEL), lambda i: (i, 0)),
                  pl.BlockSpec((TC, LANES), lambda i: (i, 0)),
                  _const_spec((1, D_MODEL)),
                  pl.BlockSpec((TC * ROW_TILES, LANES), lambda i: (i, 0)),
                  pl.BlockSpec((TC * ROW_TILES, LANES), lambda i: (t_total // TC + i, 0))],
        out_specs=[pl.BlockSpec((TC, D_MODEL), lambda i: (jnp.minimum(i, n_c_prompt - 1), 0)),
                   pl.BlockSpec((TC, D_MODEL), lambda i: (jnp.maximum(i - n_c_prompt, 0), 0))],
        out_shape=[jax.ShapeDtypeStruct((t_prompt, D_MODEL), f32), jax.ShapeDtypeStruct((t_sample, D_MODEL), f32)],
        compiler_params=cparams,
        name="moe_combine",
    )(x1, gcol, nfg, out_tok, out_tok)

    unplane = lambda a: a.transpose(0, 2, 1, 3).reshape(dec_batch, dec_seq, a.shape[-1])
    y_prompt = y_prompt.reshape(n_batch, seq, D_MODEL)
    y_sample = unplane(y_sample.reshape(n_sample_tiles, dec_seq, SEQ_BLK, D_MODEL))
    pool_state_prompt = pstate[None, :, CARRY_ROWS - POOL_STATE:, :]
    p_s = unplane(pplanes)
    pool_state_sample = jnp.concatenate([state_pool[0], p_s], axis=1)[None, :, -POOL_STATE:, :]
    chunk_v_sample = unplane(vplanes)[None]
    return (y_prompt, y_sample, pool_state_prompt, pool_state_sample, chunk_v_sample)
```

```python
import functools
import math

import jax
import jax.numpy as jnp
from jax import lax
from jax.experimental import pallas as pl
from jax.experimental.pallas import tpu as pltpu

D_MODEL = 1024
A_WIDTH = 512
B_WIDTH = 512
N_HEADS = 4
HEAD_DIM = 128
CHUNK = 128
POOL_WINDOWS = (2, 4, 8, 16)
POOL_STATE = 15
N_GROUPS = 4
EXPERTS_PER_GROUP = 8
N_EXPERTS = 32
TOP_K = 2
D_EXPERT = 512
EPS = 1e-6

SUBLANES = 8
LANES = 128
ROW_TILES = D_MODEL // LANES

TM = 256
TC = 512
TE = 512
SEQ_BLK = 32
N_ROUTER_ROWS = 40
CARRY_ROWS = 16
DMA_UNROLL = 32
VMEM_LIMIT = 48 * 1024 * 1024
N_PARTS = 2
EXPERT_VMEM_LIMIT = 62 * 1024 * 1024

_INV_SQRT2 = 1.0 / math.sqrt(2.0)


def _rmsnorm(x, g):
    r = lax.rsqrt(jnp.mean(x * x, axis=-1, keepdims=True) + EPS)
    return (x * r) * g


def _gelu(x):
    return 0.5 * x * (1.0 + lax.erf(x * _INV_SQRT2))


def _layernorm(x, g, b):
    mu = jnp.mean(x, axis=-1, keepdims=True)
    xc = x - mu
    var = jnp.mean(xc * xc, axis=-1, keepdims=True)
    return (xc * lax.rsqrt(var + EPS)) * g + b


def _row_slab(ref, s, n):
    return ref[pl.ds(s, n, stride=ROW_TILES), :]


def _pool_project(pooled, g, bw_ref, bscale_ref):
    lo, hi = g * HEAD_DIM, (g + 1) * HEAD_DIM
    hb = jnp.dot(pooled.astype(jnp.bfloat16), bw_ref[g], preferred_element_type=jnp.float32)
    return hb * bscale_ref[:, lo:hi]


def _prompt_mixers(j, u, v, p, aws_ref, abias_ref, bw_ref, bscale_ref, mix_ref, pcarry_ref, pstate_ref):
    tri = (lax.broadcasted_iota(jnp.int32, (CHUNK, CHUNK), 0)
           >= lax.broadcasted_iota(jnp.int32, (CHUNK, CHUNK), 1))
    vb = v.astype(jnp.bfloat16)
    for hd in range(N_HEADS):
        lo, hi = hd * HEAD_DIM, (hd + 1) * HEAD_DIM
        w = jnp.where(tri, aws_ref[hd], 0.0).astype(jnp.bfloat16)
        for c in range(TM // CHUNK):
            r0, r1 = c * CHUNK, (c + 1) * CHUNK
            z = jnp.dot(w, vb[r0:r1, lo:hi], preferred_element_type=jnp.float32) + abias_ref[:, lo:hi]
            mix_ref[r0:r1, lo:hi] = (u[r0:r1, lo:hi] * z).astype(jnp.bfloat16)

    head_pos = j * TM + lax.broadcasted_iota(jnp.int32, (CARRY_ROWS, LANES), 0)
    for g, w in enumerate(POOL_WINDOWS):
        lo, hi = g * HEAD_DIM, (g + 1) * HEAD_DIM
        pg = p[:, lo:hi]
        acc = jnp.concatenate([pcarry_ref[:, lo:hi], pg], axis=0)
        shift = 1
        while shift < w:
            acc = acc + pltpu.roll(acc, shift, 0)
            shift *= 2
        head = acc[CARRY_ROWS:2 * CARRY_ROWS, :] / jnp.minimum(head_pos + 1, w).astype(jnp.float32)
        mean = jnp.concatenate([head, acc[2 * CARRY_ROWS:, :] * (1.0 / w)], axis=0)
        pooled = mean - pg
        mix_ref[:, A_WIDTH + lo:A_WIDTH + hi] = _pool_project(pooled, g, bw_ref, bscale_ref).astype(jnp.bfloat16)
    tail = p[TM - CARRY_ROWS:, :]
    pcarry_ref[...] = tail
    pstate_ref[0] = tail


def _sample_mixers(u, v, p, aws_ref, abs_ref, st_ref, bw_ref, bscale_ref, mix_ref, pp_ref, vp_ref):
    n_pos = TM // SEQ_BLK
    for i in range(n_pos):
        vp_ref[0, i] = v[i * SEQ_BLK:(i + 1) * SEQ_BLK, :]
        pp_ref[0, i] = p[i * SEQ_BLK:(i + 1) * SEQ_BLK, :]

    for hd in range(N_HEADS):
        lo, hi = hd * HEAD_DIM, (hd + 1) * HEAD_DIM
        vplanes = [v[s * SEQ_BLK:(s + 1) * SEQ_BLK, lo:hi] for s in range(n_pos)]
        for i in range(n_pos):
            z = vplanes[0] * aws_ref[(hd * n_pos + i) * n_pos]
            for s in range(1, i + 1):
                z = z + vplanes[s] * aws_ref[(hd * n_pos + i) * n_pos + s]
            z = z + abs_ref[hd * n_pos + i]
            r0, r1 = i * SEQ_BLK, (i + 1) * SEQ_BLK
            mix_ref[r0:r1, lo:hi] = (u[r0:r1, lo:hi] * z).astype(jnp.bfloat16)

    for g, w in enumerate(POOL_WINDOWS):
        lo, hi = g * HEAD_DIM, (g + 1) * HEAD_DIM
        planes = [st_ref[0, k, :, lo:hi] for k in range(POOL_STATE)]
        planes += [p[i * SEQ_BLK:(i + 1) * SEQ_BLK, lo:hi] for i in range(n_pos)]
        pooled = []
        for i in range(n_pos):
            top = POOL_STATE + i
            s = planes[top - w + 1]
            for k in range(top - w + 2, top + 1):
                s = s + planes[k]
            pooled.append(s * (1.0 / w) - planes[top])
        pooled = jnp.concatenate(pooled, axis=0)
        mix_ref[:, A_WIDTH + lo:A_WIDTH + hi] = _pool_project(pooled, g, bw_ref, bscale_ref).astype(jnp.bfloat16)


def _route(h2, rwt_ref, rb_ref, su_ref, cnt_ref, meta_ref, gcol_ref):
    tm = h2.shape[0]
    h_hi = h2.astype(jnp.bfloat16)
    h_lo = (h2 - h_hi.astype(jnp.float32)).astype(jnp.bfloat16)
    s = (jnp.dot(h_hi, rwt_ref[...], preferred_element_type=jnp.float32)
         + jnp.dot(h_lo, rwt_ref[...], preferred_element_type=jnp.float32))
    st = s.T
    lt = st[0:N_ROUTER_ROWS, :] + st[N_ROUTER_ROWS:2 * N_ROUTER_ROWS, :] + rb_ref[...]
    row = lambda i: lt[i:i + 1, :]
    l1 = [row(i) for i in range(N_GROUPS)]
    m1 = jnp.maximum(jnp.maximum(l1[0], l1[1]), jnp.maximum(l1[2], l1[3]))
    grp = jnp.where(l1[0] == m1, 0, jnp.where(l1[1] == m1, 1, jnp.where(l1[2] == m1, 2, 3)))
    se = (jnp.exp(l1[0] - m1) + jnp.exp(l1[1] - m1)) + (jnp.exp(l1[2] - m1) + jnp.exp(l1[3] - m1))
    pg = 1.0 / se
    l2 = []
    for e in range(EXPERTS_PER_GROUP):
        c = [row(N_GROUPS + g * EXPERTS_PER_GROUP + e) for g in range(N_GROUPS)]
        l2.append(jnp.where(grp == 0, c[0], jnp.where(grp == 1, c[1], jnp.where(grp == 2, c[2], c[3]))))
    v0 = functools.reduce(jnp.maximum, l2)
    i0 = jnp.full_like(grp, EXPERTS_PER_GROUP - 1)
    for e in range(EXPERTS_PER_GROUP - 2, -1, -1):
        i0 = jnp.where(l2[e] == v0, e, i0)
    neg = jnp.float32(-jnp.inf)
    l2m = [jnp.where(i0 == e, neg, l2[e]) for e in range(EXPERTS_PER_GROUP)]
    v1 = functools.reduce(jnp.maximum, l2m)
    i1 = jnp.full_like(grp, EXPERTS_PER_GROUP - 1)
    for e in range(EXPERTS_PER_GROUP - 2, -1, -1):
        i1 = jnp.where((l2m[e] == v1) & (i0 != e), e, i1)
    d = jnp.exp(v1 - v0)
    g0 = pg / (1.0 + d)
    g1 = (pg * d) / (1.0 + d)
    e0 = grp * EXPERTS_PER_GROUP + i0
    e1 = grp * EXPERTS_PER_GROUP + i1

    eiota = lax.broadcasted_iota(jnp.int32, (N_EXPERTS, tm), 0)
    hit0 = eiota == e0
    hit1 = eiota == e1
    onehot = (hit0 | hit1).astype(jnp.bfloat16)
    prefix = jnp.dot(onehot, su_ref[...], preferred_element_type=jnp.float32)
    carry = cnt_ref[...]
    base = prefix + jnp.concatenate([carry] * (tm // LANES), axis=1)
    r0 = jnp.sum(jnp.where(hit0, base, 0.0), axis=0, keepdims=True)
    r1 = jnp.sum(jnp.where(hit1, base, 0.0), axis=0, keepdims=True)
    ones = jnp.ones((tm, LANES), jnp.bfloat16)
    cnt_ref[...] = carry + jnp.dot(onehot, ones, preferred_element_type=jnp.float32)

    meta_ref[0:1, :] = e0
    meta_ref[1:2, :] = e1
    meta_ref[2:3, :] = r0.astype(jnp.int32)
    meta_ref[3:4, :] = r1.astype(jnp.int32)
    meta_ref[4:8, :] = jnp.zeros((4, tm), jnp.int32)

    giota = lax.broadcasted_iota(jnp.int32, (LANES, tm), 0)
    gpad = jnp.where(giota == 0, g0, jnp.where(giota == 1, g1, 0.0))
    gcol_ref[...] = gpad.T


def _mixer_kernel(aws_s_ref, abs_s_ref,
                  x0_ref, xpn_ref, xsn_ref,
                  st_ref, n1g_ref, win_ref, ang_ref, anb_ref, aws_ref, abias_ref, bw_ref, bscale_ref,
                  wout_ref, n2g_ref, rwt_ref, rb_ref, su_ref,
                  x1_ref, h2_ref, meta_ref, gcol_ref, counts_ref, pstate_ref, pp_ref, vp_ref, asg_ref,
                  mix_ref, pcarry_ref, cnt_ref, x_ref, u_ref, v_ref, p_ref, part_meta, dest_vmem, fill_vmem, *dest_smem,
                  n_prompt_tiles, seq_tiles, part_tiles, t_total):
    i = pl.program_id(0)
    is_prompt = i < n_prompt_tiles
    j = i % seq_tiles
    part = i // part_tiles
    tile_in_part = i % part_tiles

    @pl.when((i > 0) & (tile_in_part == 0))
    def _():
        _, _, pad_start = _segment_offsets(cnt_ref[...])
        _stage_inversion(part_meta, pad_start, dest_vmem, dest_smem, fill_vmem, asg_ref)

    def in_proj(x):
        h = _rmsnorm(x, n1g_ref[...]).astype(jnp.bfloat16)
        proj = jnp.dot(h, win_ref[...], preferred_element_type=jnp.float32)
        uv = _gelu(proj[:, : 2 * A_WIDTH])
        u_ref[...] = uv[:, :A_WIDTH]
        v_ref[...] = _layernorm(uv[:, A_WIDTH:], ang_ref[...], anb_ref[...])
        p_ref[...] = proj[:, 2 * A_WIDTH:]
        x_ref[...] = x

    @pl.when(i == 0)
    def _():
        in_proj(x0_ref[0])

    @pl.when(i % part_tiles == 0)
    def _():
        cnt_ref[...] = jnp.zeros_like(cnt_ref)

    @pl.when(is_prompt & (j == 0))
    def _():
        pcarry_ref[...] = jnp.zeros_like(pcarry_ref)

    def finish_tile(invert_previous_part):
        x1 = x_ref[...] + jnp.dot(mix_ref[...], wout_ref[...], preferred_element_type=jnp.float32)
        x1_ref[...] = x1
        h2 = _rmsnorm(x1, n2g_ref[...])
        for s in range(ROW_TILES):
            h2_ref[pl.ds(s, TM, stride=ROW_TILES), :] = h2[:, s * LANES:(s + 1) * LANES]
        _route(h2, rwt_ref, rb_ref, su_ref, cnt_ref, meta_ref, gcol_ref)
        counts_ref[0] = cnt_ref[...]
        part_meta[:, pl.ds(pl.multiple_of(tile_in_part * TM, TM), TM)] = meta_ref[0:2 * TOP_K, :]
        if invert_previous_part:
            first_tok = tile_in_part * TM
            for k, dsm in enumerate(dest_smem):
                first_asg = k * t_total + (part - 1) * (part_tiles * TM) + first_tok
                for t in range(TM):
                    asg_ref[0, 0, dsm[0, first_tok + t]] = first_asg + t
        nxt = jnp.minimum(i + 1, pl.num_programs(0) - 1)
        in_proj(jnp.where(nxt < n_prompt_tiles, xpn_ref[0], xsn_ref[...]))

    @pl.when(is_prompt & (part == 0))
    def _():
        _prompt_mixers(j, u_ref[...], v_ref[...], p_ref[...], aws_ref, abias_ref, bw_ref, bscale_ref, mix_ref,
                       pcarry_ref, pstate_ref)
        finish_tile(False)

    @pl.when(is_prompt & (part > 0))
    def _():
        _prompt_mixers(j, u_ref[...], v_ref[...], p_ref[...], aws_ref, abias_ref, bw_ref, bscale_ref, mix_ref,
                       pcarry_ref, pstate_ref)
        finish_tile(True)

    @pl.when(jnp.logical_not(is_prompt))
    def _():
        _sample_mixers(u_ref[...], v_ref[...], p_ref[...], aws_s_ref, abs_s_ref, st_ref, bw_ref, bscale_ref, mix_ref,
                       pp_ref, vp_ref)
        finish_tile(True)


def _segment_offsets(cnt):
    padded = jnp.floor((cnt + (TE - 1)) * (1.0 / TE)) * TE
    sub = lax.broadcasted_iota(jnp.int32, cnt.shape, 0)
    pad_end = padded
    shift = 1
    while shift < N_EXPERTS:
        pad_end = pad_end + jnp.where(sub >= shift, pltpu.roll(pad_end, shift, 0), 0.0)
        shift *= 2
    return padded, pad_end, pad_end - padded


def _dest_rows(meta_ref, pad_start, dest_ref):
    t_part = meta_ref.shape[1]
    base = pad_start.astype(jnp.int32)
    for k in range(TOP_K):
        e = meta_ref[k:k + 1, :]
        d = meta_ref[TOP_K + k:TOP_K + k + 1, :]
        for ex in range(N_EXPERTS):
            row = jnp.concatenate([base[ex:ex + 1, :]] * (t_part // LANES), axis=1)
            d = d + jnp.where(e == ex, row, 0)
        dest_ref[k] = d


def _stage_inversion(meta_ref, pad_start, dest_vmem, dest_smem, fill_vmem, asg_ref):
    _dest_rows(meta_ref, pad_start, dest_vmem)
    for k, dsm in enumerate(dest_smem):
        pltpu.sync_copy(dest_vmem.at[k], dsm)
    fill_vmem[...] = jnp.full(fill_vmem.shape, -1, jnp.int32)
    pltpu.sync_copy(fill_vmem, asg_ref)


def _tables_kernel(counts_ref, meta_ref, dest_ref, tab_ref):
    h = pl.program_id(0)
    cnt = counts_ref[0]
    sub = lax.broadcasted_iota(jnp.int32, cnt.shape, 0)
    padded, pad_end, pad_start = _segment_offsets(cnt)

    @pl.when(h == pl.num_programs(0) - 1)
    def _():
        _dest_rows(meta_ref, pad_start, dest_ref)

    n_valid = jnp.maximum(pad_end[N_EXPERTS - 1:, :] * (1.0 / TE), 1.0)
    tile = jnp.minimum(lax.broadcasted_iota(jnp.int32, (1, LANES), 1).astype(jnp.float32), n_valid - 1.0)
    tile_start = tile * TE
    tile_e = jnp.minimum(jnp.sum((pad_end <= tile_start).astype(jnp.float32), axis=0, keepdims=True),
                         N_EXPERTS - 1.0)
    is_e = sub.astype(jnp.float32) == tile_e
    seg_start = jnp.sum(jnp.where(is_e, pad_start, 0.0), axis=0, keepdims=True)
    seg_cnt = jnp.sum(jnp.where(is_e, cnt, 0.0), axis=0, keepdims=True)
    rows = jnp.clip(seg_cnt - (tile_start - seg_start), 0.0, TE)
    tab_ref[0, 0:1, :] = tile_e.astype(jnp.int32)
    tab_ref[0, 1:2, :] = rows.astype(jnp.int32)
    tab_ref[0, 2:3, :] = n_valid.astype(jnp.int32)
    tab_ref[0, 3:, :] = jnp.zeros((SUBLANES - 3, LANES), jnp.int32)


def _expert_kernel(te_ref, nv_ref, nrows_ref,
                   src_ref, src_next_ref, dst_ref,
                   dest_k0_ref, dest_k1_ref,
                   h2_ref, w1_ref, w3_ref, w2_ref,
                   out_hbm,
                   xt0, xt1, ob0, ob1, fill_vmem, asg_last, ssem, *, t_total, t_part):
    h = pl.program_id(0)
    i = pl.program_id(1)
    g = h * pl.num_programs(1) + i
    nv = nv_ref[h]
    slab = lambda r: pl.ds(pl.multiple_of(r * ROW_TILES, ROW_TILES), ROW_TILES)
    toks_per_tile = TE // TOP_K
    inv_tiles = t_part // toks_per_tile

    def last_part_src(tile):
        def src(r):
            a = asg_last[0, tile * TE + r]
            return a - jnp.where(a >= t_total, t_total + t_part, t_part)
        return src

    first_tables = (lambda r: src_ref[0, 0, r], lambda r: src_next_ref[0, 0, r], lambda r: dst_ref[0, 0, r])
    last_tables = (last_part_src(i), last_part_src(jnp.minimum(i + 1, nv - 1)), lambda r: asg_last[0, i * TE + r])

    def gather(src, xt):
        for r in range(TE):
            xt[r * ROW_TILES:(r + 1) * ROW_TILES, :] = h2_ref[slab(src(r)), :]

    def invert_share():
        first_tok = jnp.minimum(i, inv_tiles - 1) * toks_per_tile
        for k, dest_ref in enumerate((dest_k0_ref, dest_k1_ref)):
            first_asg = k * t_total + t_part + first_tok
            for t in range(toks_per_tile):
                asg_last[0, dest_ref[0, 0, t]] = first_asg + t

    def wait_scatter(tile, ob, other, sem):
        n = nrows_ref[tile] * ROW_TILES
        pltpu.make_async_copy(other.at[pl.ds(0, n), :], ob.at[pl.ds(0, n), :], sem).wait()

    def tile_body(xt, xt_next, ob, ob_other, sem, sem_other, tables, first_part):
        _, src_next, dst = tables

        @pl.when(i >= 2)
        def _():
            wait_scatter(g - 2, ob, ob_other, sem)

        def compute(m):
            gather(src_next, xt_next)
            if first_part:
                invert_share()
            x = jnp.concatenate([_row_slab(xt, s, m) for s in range(ROW_TILES)], axis=1).astype(jnp.bfloat16)
            a = jnp.dot(x, w1_ref[0].astype(jnp.bfloat16), preferred_element_type=jnp.float32)
            b = jnp.dot(x, w3_ref[0].astype(jnp.bfloat16), preferred_element_type=jnp.float32)
            hmid = (a * jax.nn.sigmoid(a)) * b
            o = jnp.dot(hmid.astype(jnp.bfloat16), w2_ref[0].astype(jnp.bfloat16),
                        preferred_element_type=jnp.float32)
            for s in range(ROW_TILES):
                ob[pl.ds(s, m, stride=ROW_TILES), :] = o[:, s * LANES:(s + 1) * LANES]

        n = nrows_ref[g]

        @pl.when(n > TE // 2)
        def _():
            compute(TE)

        @pl.when(n <= TE // 2)
        def _():
            compute(TE // 2)

        n_full = n // DMA_UNROLL
        copy = lambda r: pltpu.make_async_copy(ob.at[slab(r), :], out_hbm.at[dst(r)], sem)

        def body(rb, carry):
            for uu in range(DMA_UNROLL):
                copy(rb * DMA_UNROLL + uu).start(priority=uu % 2)
            return carry
        lax.fori_loop(0, n_full, body, 0)

        def tail(r, carry):
            copy(r).start(priority=1)
            return carry
        lax.fori_loop(n_full * DMA_UNROLL, n, tail, 0)

        @pl.when(i == nv - 1)
        def _():
            @pl.when(i >= 1)
            def _():
                wait_scatter(g - 1, ob_other, ob, sem_other)
            wait_scatter(g, ob, ob_other, sem)

    def part_body(tables, first_part):
        @pl.when(i == 0)
        def _():
            gather(tables[0], xt0)

        @pl.when(i % 2 == 0)
        def _():
            tile_body(xt0, xt1, ob0, ob1, ssem.at[0], ssem.at[1], tables, first_part)

        @pl.when(i % 2 == 1)
        def _():
            tile_body(xt1, xt0, ob1, ob0, ssem.at[1], ssem.at[0], tables, first_part)

    @pl.when((h == 0) & (i == 0))
    def _():
        fill_vmem[...] = jnp.full(fill_vmem.shape, t_part, jnp.int32)
        pltpu.sync_copy(fill_vmem, asg_last)

    @pl.when((i < nv) & (h == 0))
    def _():
        part_body(first_tables, True)

    @pl.when((i < nv) & (h > 0))
    def _():
        part_body(last_tables, False)


def _combine_kernel(x1_ref, gcol_ref, nfg_ref, o0_ref, o1_ref, yp_ref, ys_ref, *, n_prompt_tiles):
    i = pl.program_id(0)
    o0 = jnp.concatenate([_row_slab(o0_ref, s, TC) for s in range(ROW_TILES)], axis=1)
    o1 = jnp.concatenate([_row_slab(o1_ref, s, TC) for s in range(ROW_TILES)], axis=1)
    g = gcol_ref[...]
    moe = g[:, 0:1] * o0 + g[:, 1:2] * o1
    y = _rmsnorm(x1_ref[...] + moe, nfg_ref[...])

    @pl.when(i < n_prompt_tiles)
    def _():
        yp_ref[...] = y

    @pl.when(i >= n_prompt_tiles)
    def _():
        ys_ref[...] = y


def _const_spec(shape):
    return pl.BlockSpec(shape, lambda *_: (0,) * len(shape))


def kernel(x_prompt, x_sample, state_pool, norm1_g, w_in, a_norm_g, a_norm_b, a_ws, a_bs, b_w, b_scale, w_out,
           norm2_g, r1_w, r1_b, r2_w, r2_b, exp_w1, exp_w3, exp_w2, normf_g):
    f32, bf16, i32 = jnp.float32, jnp.bfloat16, jnp.int32
    n_batch, seq, _ = x_prompt.shape
    dec_batch, dec_seq, _ = x_sample.shape
    assert norm1_g.shape[0] == 1 and seq % TM == 0 and TM % CHUNK == 0
    assert dec_seq * SEQ_BLK == TM and dec_batch % SEQ_BLK == 0 and dec_seq <= CHUNK
    t_prompt = n_batch * seq
    t_sample = dec_batch * dec_seq
    t_total = t_prompt + t_sample
    n_tok_tiles = t_total // TM
    n_prompt_tiles = t_prompt // TM
    n_sample_tiles = t_sample // TM
    seq_tiles = seq // TM
    plane_rows = t_total * ROW_TILES
    assert n_tok_tiles % N_PARTS == 0
    part_tiles = n_tok_tiles // N_PARTS
    t_part = part_tiles * TM
    n_exp_tiles = -(-(TOP_K * t_part + N_EXPERTS * (TE - 1)) // TE)
    p_rows = n_exp_tiles * TE

    n1g = norm1_g[0][None, :]
    n2g = norm2_g[0][None, :]
    nfg = normf_g[None, :]
    win = w_in[0].astype(bf16)
    wout = w_out[0].astype(bf16)
    ang = a_norm_g[0][None, :]
    anb = a_norm_b[0][None, :]
    bw = b_w[0].astype(bf16)
    bscale = b_scale[0][None, :]
    abias = jnp.repeat(a_bs[0][:, :CHUNK].T, HEAD_DIM, axis=1)
    rw = jnp.concatenate([r1_w[0], r2_w[0].transpose(1, 0, 2).reshape(D_MODEL, N_EXPERTS),
                          jnp.zeros((D_MODEL, N_ROUTER_ROWS - N_GROUPS - N_EXPERTS), f32)], axis=1)
    rw_hi = rw.astype(bf16)
    rw_lo = (rw - rw_hi.astype(f32)).astype(bf16)
    rwt = jnp.concatenate([rw_hi, rw_lo, jnp.zeros((D_MODEL, LANES - 2 * N_ROUTER_ROWS), bf16)], axis=1)
    rbias = jnp.concatenate([r1_b[0], r2_b[0].reshape(-1),
                             jnp.zeros((N_ROUTER_ROWS - N_GROUPS - N_EXPERTS,), f32)])
    rb = jnp.broadcast_to(rbias[:, None], (N_ROUTER_ROWS, TM))
    su = (jnp.arange(TM)[:, None] < jnp.arange(TM)[None, :]).astype(bf16)
    xs_planes = (x_sample.reshape(n_sample_tiles, SEQ_BLK, dec_seq, D_MODEL)
                 .transpose(0, 2, 1, 3).reshape(t_sample, D_MODEL))
    st_planes = state_pool[0].reshape(n_sample_tiles, SEQ_BLK, POOL_STATE, B_WIDTH).transpose(0, 2, 1, 3)
    aws_s = a_ws[0][:, :dec_seq, :dec_seq].reshape(-1)
    abs_s = a_bs[0][:, :dec_seq].reshape(-1)

    cparams = pltpu.CompilerParams(dimension_semantics=("arbitrary",), vmem_limit_bytes=VMEM_LIMIT)

    p_tile = lambda i: jnp.minimum(i, n_prompt_tiles - 1)
    s_tile = lambda i: jnp.maximum(i - n_prompt_tiles, 0)
    plane_shape = jax.ShapeDtypeStruct((n_sample_tiles, dec_seq, SEQ_BLK, A_WIDTH), f32)
    assert N_PARTS >= 2 and n_prompt_tiles >= part_tiles and t_part % TM == 0
    x1, h2, meta, gcol, counts, pstate, pplanes, vplanes, asg_head = pl.pallas_call(
        functools.partial(_mixer_kernel, n_prompt_tiles=n_prompt_tiles, seq_tiles=seq_tiles, part_tiles=part_tiles,
                          t_total=t_total),
        grid_spec=pltpu.PrefetchScalarGridSpec(
            num_scalar_prefetch=2,
            grid=(n_tok_tiles,),
            in_specs=[
                pl.BlockSpec((1, TM, D_MODEL), lambda i, *_: (0, 0, 0)),
                pl.BlockSpec((1, TM, D_MODEL),
                             lambda i, *_: (p_tile(i + 1) // seq_tiles, p_tile(i + 1) % seq_tiles, 0)),
                pl.BlockSpec((TM, D_MODEL), lambda i, *_: (jnp.minimum(s_tile(i + 1), n_sample_tiles - 1), 0)),
                pl.BlockSpec((1, POOL_STATE, SEQ_BLK, B_WIDTH), lambda i, *_: (s_tile(i), 0, 0, 0)),
                _const_spec((1, D_MODEL)), _const_spec((D_MODEL, 3 * A_WIDTH)), _const_spec((1, A_WIDTH)),
                _const_spec((1, A_WIDTH)), _const_spec((N_HEADS, CHUNK, CHUNK)), _const_spec((CHUNK, A_WIDTH)),
                _const_spec((N_GROUPS, HEAD_DIM, HEAD_DIM)), _const_spec((1, B_WIDTH)),
                _const_spec((D_MODEL, D_MODEL)), _const_spec((1, D_MODEL)), _const_spec((D_MODEL, LANES)),
                _const_spec((N_ROUTER_ROWS, TM)), _const_spec((TM, TM)),
            ],
            out_specs=[
                pl.BlockSpec((TM, D_MODEL), lambda i, *_: (i, 0)),
                pl.BlockSpec((TM * ROW_TILES, LANES), lambda i, *_: (i, 0)),
                pl.BlockSpec((SUBLANES, TM), lambda i, *_: (0, i)),
                pl.BlockSpec((TM, LANES), lambda i, *_: (i, 0)),
                pl.BlockSpec((1, N_EXPERTS, LANES), lambda i, *_: (i // part_tiles, 0, 0)),
                pl.BlockSpec((1, CARRY_ROWS, B_WIDTH), lambda i, *_: (p_tile(i) // seq_tiles, 0, 0)),
                pl.BlockSpec((1, dec_seq, SEQ_BLK, A_WIDTH), lambda i, *_: (s_tile(i), 0, 0, 0)),
                pl.BlockSpec((1, dec_seq, SEQ_BLK, A_WIDTH), lambda i, *_: (s_tile(i), 0, 0, 0)),
                pl.BlockSpec((1, 1, p_rows), lambda i, *_: (jnp.maximum(i // part_tiles - 1, 0), 0, 0),
                             memory_space=pltpu.SMEM),
            ],
            scratch_shapes=[pltpu.VMEM((TM, D_MODEL), bf16), pltpu.VMEM((CARRY_ROWS, B_WIDTH), f32),
                            pltpu.VMEM((N_EXPERTS, LANES), f32), pltpu.VMEM((TM, D_MODEL), f32),
                            pltpu.VMEM((TM, A_WIDTH), f32), pltpu.VMEM((TM, A_WIDTH), f32),
                            pltpu.VMEM((TM, B_WIDTH), f32),
                            pltpu.VMEM((2 * TOP_K, t_part), i32), pltpu.VMEM((TOP_K, 1, t_part), i32),
                            pltpu.VMEM((1, 1, p_rows), i32)] + [pltpu.SMEM((1, t_part), i32)] * TOP_K,
        ),
        out_shape=[
            jax.ShapeDtypeStruct((t_total, D_MODEL), f32),
            jax.ShapeDtypeStruct((plane_rows, LANES), f32),
            jax.ShapeDtypeStruct((SUBLANES, t_total), i32),
            jax.ShapeDtypeStruct((t_total, LANES), f32),
            jax.ShapeDtypeStruct((N_PARTS, N_EXPERTS, LANES), f32),
            jax.ShapeDtypeStruct((n_batch, CARRY_ROWS, B_WIDTH), f32),
            plane_shape, plane_shape,
            jax.ShapeDtypeStruct((N_PARTS - 1, 1, p_rows), i32),
        ],
        compiler_params=cparams,
        name="mixer",
    )(aws_s, abs_s, x_prompt, x_prompt, xs_planes, st_planes, n1g, win, ang, anb, a_ws[0][:, :CHUNK, :CHUNK], abias,
      bw, bscale, wout, n2g, rwt, rb, su)

    assert n_exp_tiles <= LANES and t_part % LANES == 0
    dest_last, tab = pl.pallas_call(
        _tables_kernel,
        grid=(N_PARTS,),
        in_specs=[pl.BlockSpec((1, N_EXPERTS, LANES), lambda h: (h, 0, 0)),
                  pl.BlockSpec((SUBLANES, t_part), lambda h: (0, h))],
        out_specs=[_const_spec((TOP_K, 1, t_part)),
                   pl.BlockSpec((1, SUBLANES, LANES), lambda h: (h, 0, 0))],
        out_shape=[jax.ShapeDtypeStruct((TOP_K, 1, t_part), i32),
                   jax.ShapeDtypeStruct((N_PARTS, SUBLANES, LANES), i32)],
        compiler_params=cparams,
        name="route_tables",
    )(counts, meta)
    tile_e = tab[:, 0, :n_exp_tiles].reshape(-1)
    tile_rows = tab[:, 1, :n_exp_tiles].reshape(-1)
    n_valid = tab[:, 2, 0]
    assert N_PARTS == 2 and (TOP_K * t_part) % TE == 0
    asg = asg_head.reshape(-1)
    row_tok = jnp.where(asg >= t_total, asg - t_total, asg)
    row_src = jnp.where(asg >= 0, row_tok, 0).reshape(n_exp_tiles, 1, TE)
    row_dst = jnp.maximum(asg, 0).reshape(n_exp_tiles, 1, TE)

    flat = lambda h, i: h * n_exp_tiles + i
    cur_blk = lambda h, i, te, nv, nr: (jnp.minimum(i, nv[0] - 1), 0, 0)
    nxt_blk = lambda h, i, te, nv, nr: (jnp.minimum(i + 1, nv[0] - 1), 0, 0)
    w_blk = lambda h, i, te, nv, nr: (te[flat(h, i)], 0, 0)
    smem_blk = lambda imap: pl.BlockSpec((1, 1, TE), imap, memory_space=pltpu.SMEM)
    last_share = t_part // (TE // TOP_K) - 1
    share_blk = lambda k: pl.BlockSpec(
        (1, 1, TE // TOP_K), lambda h, i, *_: (k, 0, jnp.where(h == 0, jnp.minimum(i, last_share), last_share)),
        memory_space=pltpu.SMEM)
    row_buf = pltpu.VMEM((TE * ROW_TILES, LANES), f32)
    out_tok = pl.pallas_call(
        functools.partial(_expert_kernel, t_total=t_total, t_part=t_part),
        grid_spec=pltpu.PrefetchScalarGridSpec(
            num_scalar_prefetch=3,
            grid=(N_PARTS, n_exp_tiles),
            in_specs=[smem_blk(cur_blk), smem_blk(nxt_blk), smem_blk(cur_blk),
                      share_blk(0), share_blk(1),
                      pl.BlockSpec((t_part * ROW_TILES, LANES), lambda h, i, *_: (h, 0),
                                   pipeline_mode=pl.Buffered(1)),
                      pl.BlockSpec((1, D_MODEL, D_EXPERT), w_blk),
                      pl.BlockSpec((1, D_MODEL, D_EXPERT), w_blk),
                      pl.BlockSpec((1, D_EXPERT, D_MODEL), w_blk)],
            out_specs=pl.BlockSpec(memory_space=pl.ANY),
            scratch_shapes=[row_buf, row_buf, row_buf, row_buf, pltpu.VMEM((1, p_rows), i32),
                            pltpu.SMEM((1, p_rows), i32), pltpu.SemaphoreType.DMA((2,))],
        ),
        out_shape=jax.ShapeDtypeStruct((TOP_K * t_total, ROW_TILES, LANES), f32),
        compiler_params=pltpu.CompilerParams(dimension_semantics=("arbitrary", "arbitrary"),
                                             vmem_limit_bytes=EXPERT_VMEM_LIMIT),
        name="moe_experts",
    )(tile_e, n_valid, tile_rows, row_src, row_src, row_dst, dest_last, dest_last, h2, exp_w1[0], exp_w3[0], exp_w2[0])
    out_tok = out_tok.reshape(TOP_K * plane_rows, LANES)

    assert t_prompt % TC == 0 and t_sample % TC == 0
    n_c_prompt = t_prompt // TC
    y_prompt, y_sample = pl.pallas_call(
        functools.partial(_combine_kernel, n_prompt_tiles=n_c_prompt),
        grid=(t_total // TC,),
        in_specs=[pl.BlockSpec((TC, D_MODEL), lambda i: (i, 0)),
                  pl.BlockSpec((TC, LANES), lambda i: (i, 0)),
                  _const_spec((1, D_MODEL)),
                  pl.BlockSpec((TC * ROW_TILES, LANES), lambda i: (i, 0)),
                  pl.BlockSpec((TC * ROW_TILES, LANES), lambda i: (t_total // TC + i, 0))],
        out_specs=[pl.BlockSpec((TC, D_MODEL), lambda i: (jnp.minimum(i, n_c_prompt - 1), 0)),
                   pl.BlockSpec((TC, D_MODEL), lambda i: (jnp.maximum(i - n_c_prompt, 0), 0))],
        out_shape=[jax.ShapeDtypeStruct((t_prompt, D_MODEL), f32), jax.ShapeDtypeStruct((t_sample, D_MODEL), f32)],
        compiler_params=cparams,
        name="moe_combine",
    )(x1, gcol, nfg, out_tok, out_tok)

    unplane = lambda a: a.transpose(0, 2, 1, 3).reshape(dec_batch, dec_seq, a.shape[-1])
    y_prompt = y_prompt.reshape(n_batch, seq, D_MODEL)
    y_sample = unplane(y_sample.reshape(n_sample_tiles, dec_seq, SEQ_BLK, D_MODEL))
    pool_state_prompt = pstate[None, :, CARRY_ROWS - POOL_STATE:, :]
    p_s = unplane(pplanes)
    pool_state_sample = jnp.concatenate([state_pool[0], p_s], axis=1)[None, :, -POOL_STATE:, :]
    chunk_v_sample = unplane(vplanes)[None]
    return (y_prompt, y_sample, pool_state_prompt, pool_state_sample, chunk_v_sample)
```

```python
import functools
import math

import jax
import jax.numpy as jnp
from jax import lax
from jax.experimental import pallas as pl
from jax.experimental.pallas import tpu as pltpu

D_MODEL = 1024
A_WIDTH = 512
B_WIDTH = 512
N_HEADS = 4
HEAD_DIM = 128
CHUNK = 128
POOL_WINDOWS = (2, 4, 8, 16)
POOL_STATE = 15
N_GROUPS = 4
EXPERTS_PER_GROUP = 8
N_EXPERTS = 32
TOP_K = 2
D_EXPERT = 512
EPS = 1e-6

SUBLANES = 8
LANES = 128
ROW_TILES = D_MODEL // LANES

TM = 256
TC = 512
TE = 512
SEQ_BLK = 32
N_ROUTER_ROWS = 40
CARRY_ROWS = 16
DMA_UNROLL = 32
VMEM_LIMIT = 48 * 1024 * 1024
N_PARTS = 2
EXPERT_VMEM_LIMIT = 62 * 1024 * 1024

_INV_SQRT2 = 1.0 / math.sqrt(2.0)


def _rmsnorm(x, g):
    r = lax.rsqrt(jnp.mean(x * x, axis=-1, keepdims=True) + EPS)
    return (x * r) * g


def _gelu(x):
    return 0.5 * x * (1.0 + lax.erf(x * _INV_SQRT2))


def _layernorm(x, g, b):
    mu = jnp.mean(x, axis=-1, keepdims=True)
    xc = x - mu
    var = jnp.mean(xc * xc, axis=-1, keepdims=True)
    return (xc * lax.rsqrt(var + EPS)) * g + b


def _row_slab(ref, s, n):
    return ref[pl.ds(s, n, stride=ROW_TILES), :]


def _pool_project(pooled, g, bw_ref, bscale_ref):
    lo, hi = g * HEAD_DIM, (g + 1) * HEAD_DIM
    hb = jnp.dot(pooled.astype(jnp.bfloat16), bw_ref[g], preferred_element_type=jnp.float32)
    return hb * bscale_ref[:, lo:hi]


def _prompt_mixers(j, u, v, p, aws_ref, abias_ref, bw_ref, bscale_ref, mix_ref, pcarry_ref, pstate_ref):
    tri = (lax.broadcasted_iota(jnp.int32, (CHUNK, CHUNK), 0)
           >= lax.broadcasted_iota(jnp.int32, (CHUNK, CHUNK), 1))
    vb = v.astype(jnp.bfloat16)
    for hd in range(N_HEADS):
        lo, hi = hd * HEAD_DIM, (hd + 1) * HEAD_DIM
        w = jnp.where(tri, aws_ref[hd], 0.0).astype(jnp.bfloat16)
        for c in range(TM // CHUNK):
            r0, r1 = c * CHUNK, (c + 1) * CHUNK
            z = jnp.dot(w, vb[r0:r1, lo:hi], preferred_element_type=jnp.float32) + abias_ref[:, lo:hi]
            mix_ref[r0:r1, lo:hi] = (u[r0:r1, lo:hi] * z).astype(jnp.bfloat16)

    head_pos = j * TM + lax.broadcasted_iota(jnp.int32, (CARRY_ROWS, LANES), 0)
    for g, w in enumerate(POOL_WINDOWS):
        lo, hi = g * HEAD_DIM, (g + 1) * HEAD_DIM
        pg = p[:, lo:hi]
        acc = jnp.concatenate([pcarry_ref[:, lo:hi], pg], axis=0)
        shift = 1
        while shift < w:
            acc = acc + pltpu.roll(acc, shift, 0)
            shift *= 2
        head = acc[CARRY_ROWS:2 * CARRY_ROWS, :] / jnp.minimum(head_pos + 1, w).astype(jnp.float32)
        mean = jnp.concatenate([head, acc[2 * CARRY_ROWS:, :] * (1.0 / w)], axis=0)
        pooled = mean - pg
        mix_ref[:, A_WIDTH + lo:A_WIDTH + hi] = _pool_project(pooled, g, bw_ref, bscale_ref).astype(jnp.bfloat16)
    tail = p[TM - CARRY_ROWS:, :]
    pcarry_ref[...] = tail
    pstate_ref[0] = tail


def _position_major(ref):
    n_pos = TM // SEQ_BLK
    rows = ref.reshape(SEQ_BLK, n_pos, ref.shape[-1])
    return jnp.concatenate([rows[:, q, :] for q in range(n_pos)], axis=0)


def _sample_mixers(u, v, p, aws_ref, abs_ref, st_ref, bw_ref, bscale_ref, mix_ref):
    n_pos = TM // SEQ_BLK

    for hd in range(N_HEADS):
        lo, hi = hd * HEAD_DIM, (hd + 1) * HEAD_DIM
        vplanes = [v[s * SEQ_BLK:(s + 1) * SEQ_BLK, lo:hi] for s in range(n_pos)]
        for i in range(n_pos):
            z = vplanes[0] * aws_ref[(hd * n_pos + i) * n_pos]
            for s in range(1, i + 1):
                z = z + vplanes[s] * aws_ref[(hd * n_pos + i) * n_pos + s]
            z = z + abs_ref[hd * n_pos + i]
            r0, r1 = i * SEQ_BLK, (i + 1) * SEQ_BLK
            mix_ref[r0:r1, lo:hi] = (u[r0:r1, lo:hi] * z).astype(jnp.bfloat16)

    for g, w in enumerate(POOL_WINDOWS):
        lo, hi = g * HEAD_DIM, (g + 1) * HEAD_DIM
        planes = [st_ref[0, k, :, lo:hi] for k in range(POOL_STATE)]
        planes += [p[i * SEQ_BLK:(i + 1) * SEQ_BLK, lo:hi] for i in range(n_pos)]
        pooled = []
        for i in range(n_pos):
            top = POOL_STATE + i
            s = planes[top - w + 1]
            for k in range(top - w + 2, top + 1):
                s = s + planes[k]
            pooled.append(s * (1.0 / w) - planes[top])
        pooled = jnp.concatenate(pooled, axis=0)
        mix_ref[:, A_WIDTH + lo:A_WIDTH + hi] = _pool_project(pooled, g, bw_ref, bscale_ref).astype(jnp.bfloat16)


def _route(h2, rwt_ref, rb_ref, su_ref, cnt_ref, meta_ref, gcol_ref):
    tm = h2.shape[0]
    h_hi = h2.astype(jnp.bfloat16)
    h_lo = (h2 - h_hi.astype(jnp.float32)).astype(jnp.bfloat16)
    s = (jnp.dot(h_hi, rwt_ref[...], preferred_element_type=jnp.float32)
         + jnp.dot(h_lo, rwt_ref[...], preferred_element_type=jnp.float32))
    st = s.T
    lt = st[0:N_ROUTER_ROWS, :] + st[N_ROUTER_ROWS:2 * N_ROUTER_ROWS, :] + rb_ref[...]
    row = lambda i: lt[i:i + 1, :]
    l1 = [row(i) for i in range(N_GROUPS)]
    m1 = jnp.maximum(jnp.maximum(l1[0], l1[1]), jnp.maximum(l1[2], l1[3]))
    grp = jnp.where(l1[0] == m1, 0, jnp.where(l1[1] == m1, 1, jnp.where(l1[2] == m1, 2, 3)))
    se = (jnp.exp(l1[0] - m1) + jnp.exp(l1[1] - m1)) + (jnp.exp(l1[2] - m1) + jnp.exp(l1[3] - m1))
    pg = 1.0 / se
    l2 = []
    for e in range(EXPERTS_PER_GROUP):
        c = [row(N_GROUPS + g * EXPERTS_PER_GROUP + e) for g in range(N_GROUPS)]
        l2.append(jnp.where(grp == 0, c[0], jnp.where(grp == 1, c[1], jnp.where(grp == 2, c[2], c[3]))))
    v0 = functools.reduce(jnp.maximum, l2)
    i0 = jnp.full_like(grp, EXPERTS_PER_GROUP - 1)
    for e in range(EXPERTS_PER_GROUP - 2, -1, -1):
        i0 = jnp.where(l2[e] == v0, e, i0)
    neg = jnp.float32(-jnp.inf)
    l2m = [jnp.where(i0 == e, neg, l2[e]) for e in range(EXPERTS_PER_GROUP)]
    v1 = functools.reduce(jnp.maximum, l2m)
    i1 = jnp.full_like(grp, EXPERTS_PER_GROUP - 1)
    for e in range(EXPERTS_PER_GROUP - 2, -1, -1):
        i1 = jnp.where((l2m[e] == v1) & (i0 != e), e, i1)
    d = jnp.exp(v1 - v0)
    g0 = pg / (1.0 + d)
    g1 = (pg * d) / (1.0 + d)
    e0 = grp * EXPERTS_PER_GROUP + i0
    e1 = grp * EXPERTS_PER_GROUP + i1

    eiota = lax.broadcasted_iota(jnp.int32, (N_EXPERTS, tm), 0)
    hit0 = eiota == e0
    hit1 = eiota == e1
    onehot = (hit0 | hit1).astype(jnp.bfloat16)
    prefix = jnp.dot(onehot, su_ref[...], preferred_element_type=jnp.float32)
    carry = cnt_ref[...]
    base = prefix + jnp.concatenate([carry] * (tm // LANES), axis=1)
    r0 = jnp.sum(jnp.where(hit0, base, 0.0), axis=0, keepdims=True)
    r1 = jnp.sum(jnp.where(hit1, base, 0.0), axis=0, keepdims=True)
    ones = jnp.ones((tm, LANES), jnp.bfloat16)
    cnt_ref[...] = carry + jnp.dot(onehot, ones, preferred_element_type=jnp.float32)

    meta_ref[0:1, :] = e0
    meta_ref[1:2, :] = e1
    meta_ref[2:3, :] = r0.astype(jnp.int32)
    meta_ref[3:4, :] = r1.astype(jnp.int32)
    meta_ref[4:8, :] = jnp.zeros((4, tm), jnp.int32)

    giota = lax.broadcasted_iota(jnp.int32, (LANES, tm), 0)
    gpad = jnp.where(giota == 0, g0, jnp.where(giota == 1, g1, 0.0))
    gcol_ref[...] = gpad.T


def _mixer_kernel(aws_s_ref, abs_s_ref,
                  x0_ref, xpn_ref, xsn_ref,
                  st_ref, n1g_ref, win_ref, ang_ref, anb_ref, aws_ref, abias_ref, bw_ref, bscale_ref,
                  wout_ref, n2g_ref, rwt_ref, rb_ref, su_ref,
                  x1_ref, h2_ref, meta_ref, gcol_ref, counts_ref, pstate_ref, pp_ref, vp_ref, asg_ref,
                  mix_ref, pcarry_ref, cnt_ref, x_ref, u_ref, v_ref, p_ref, part_meta, dest_vmem, fill_vmem, *dest_smem,
                  n_prompt_tiles, seq_tiles, part_tiles, t_total):
    i = pl.program_id(0)
    is_prompt = i < n_prompt_tiles
    j = i % seq_tiles
    part = i // part_tiles
    tile_in_part = i % part_tiles

    @pl.when((i > 0) & (tile_in_part == 0))
    def _():
        _, _, pad_start = _segment_offsets(cnt_ref[...])
        _stage_inversion(part_meta, pad_start, dest_vmem, dest_smem, fill_vmem, asg_ref)

    def in_proj(x):
        h = _rmsnorm(x, n1g_ref[...]).astype(jnp.bfloat16)
        proj = jnp.dot(h, win_ref[...], preferred_element_type=jnp.float32)
        uv = _gelu(proj[:, : 2 * A_WIDTH])
        u_ref[...] = uv[:, :A_WIDTH]
        v_ref[...] = _layernorm(uv[:, A_WIDTH:], ang_ref[...], anb_ref[...])
        p_ref[...] = proj[:, 2 * A_WIDTH:]
        x_ref[...] = x

    @pl.when(i == 0)
    def _():
        in_proj(x0_ref[0])

    @pl.when(i % part_tiles == 0)
    def _():
        cnt_ref[...] = jnp.zeros_like(cnt_ref)

    @pl.when(is_prompt & (j == 0))
    def _():
        pcarry_ref[...] = jnp.zeros_like(pcarry_ref)

    def finish_tile(x, invert_previous_part):
        x1 = x + jnp.dot(mix_ref[...], wout_ref[...], preferred_element_type=jnp.float32)
        x1_ref[...] = x1
        h2 = _rmsnorm(x1, n2g_ref[...])
        for s in range(ROW_TILES):
            h2_ref[pl.ds(s, TM, stride=ROW_TILES), :] = h2[:, s * LANES:(s + 1) * LANES]
        _route(h2, rwt_ref, rb_ref, su_ref, cnt_ref, meta_ref, gcol_ref)
        counts_ref[0] = cnt_ref[...]
        part_meta[:, pl.ds(pl.multiple_of(tile_in_part * TM, TM), TM)] = meta_ref[0:2 * TOP_K, :]
        if invert_previous_part:
            first_tok = tile_in_part * TM
            for k, dsm in enumerate(dest_smem):
                first_asg = k * t_total + (part - 1) * (part_tiles * TM) + first_tok
                for t in range(TM):
                    asg_ref[0, 0, dsm[0, first_tok + t]] = first_asg + t
        nxt = jnp.minimum(i + 1, pl.num_programs(0) - 1)
        in_proj(jnp.where(nxt < n_prompt_tiles, xpn_ref[0], xsn_ref[...].reshape(TM, D_MODEL)))

    @pl.when(is_prompt & (part == 0))
    def _():
        _prompt_mixers(j, u_ref[...], v_ref[...], p_ref[...], aws_ref, abias_ref, bw_ref, bscale_ref, mix_ref,
                       pcarry_ref, pstate_ref)
        finish_tile(x_ref[...], False)

    @pl.when(is_prompt & (part > 0))
    def _():
        _prompt_mixers(j, u_ref[...], v_ref[...], p_ref[...], aws_ref, abias_ref, bw_ref, bscale_ref, mix_ref,
                       pcarry_ref, pstate_ref)
        finish_tile(x_ref[...], True)

    @pl.when(jnp.logical_not(is_prompt))
    def _():
        vp_ref[...] = v_ref[...].reshape(vp_ref.shape)
        pp_ref[...] = p_ref[...].reshape(pp_ref.shape)
        _sample_mixers(_position_major(u_ref), _position_major(v_ref), _position_major(p_ref), aws_s_ref, abs_s_ref,
                       st_ref, bw_ref, bscale_ref, mix_ref)
        finish_tile(_position_major(x_ref), True)


def _segment_offsets(cnt):
    padded = jnp.floor((cnt + (TE - 1)) * (1.0 / TE)) * TE
    sub = lax.broadcasted_iota(jnp.int32, cnt.shape, 0)
    pad_end = padded
    shift = 1
    while shift < N_EXPERTS:
        pad_end = pad_end + jnp.where(sub >= shift, pltpu.roll(pad_end, shift, 0), 0.0)
        shift *= 2
    return padded, pad_end, pad_end - padded


def _dest_rows(meta_ref, pad_start, dest_ref):
    t_part = meta_ref.shape[1]
    base = pad_start.astype(jnp.int32)
    for k in range(TOP_K):
        e = meta_ref[k:k + 1, :]
        d = meta_ref[TOP_K + k:TOP_K + k + 1, :]
        for ex in range(N_EXPERTS):
            row = jnp.concatenate([base[ex:ex + 1, :]] * (t_part // LANES), axis=1)
            d = d + jnp.where(e == ex, row, 0)
        dest_ref[k] = d


def _stage_inversion(meta_ref, pad_start, dest_vmem, dest_smem, fill_vmem, asg_ref):
    _dest_rows(meta_ref, pad_start, dest_vmem)
    for k, dsm in enumerate(dest_smem):
        pltpu.sync_copy(dest_vmem.at[k], dsm)
    fill_vmem[...] = jnp.full(fill_vmem.shape, -1, jnp.int32)
    pltpu.sync_copy(fill_vmem, asg_ref)


def _tables_kernel(counts_ref, meta_ref, dest_ref, tab_ref):
    h = pl.program_id(0)
    cnt = counts_ref[0]
    sub = lax.broadcasted_iota(jnp.int32, cnt.shape, 0)
    padded, pad_end, pad_start = _segment_offsets(cnt)

    @pl.when(h == pl.num_programs(0) - 1)
    def _():
        _dest_rows(meta_ref, pad_start, dest_ref)

    n_valid = jnp.maximum(pad_end[N_EXPERTS - 1:, :] * (1.0 / TE), 1.0)
    tile = jnp.minimum(lax.broadcasted_iota(jnp.int32, (1, LANES), 1).astype(jnp.float32), n_valid - 1.0)
    tile_start = tile * TE
    tile_e = jnp.minimum(jnp.sum((pad_end <= tile_start).astype(jnp.float32), axis=0, keepdims=True),
                         N_EXPERTS - 1.0)
    is_e = sub.astype(jnp.float32) == tile_e
    seg_start = jnp.sum(jnp.where(is_e, pad_start, 0.0), axis=0, keepdims=True)
    seg_cnt = jnp.sum(jnp.where(is_e, cnt, 0.0), axis=0, keepdims=True)
    rows = jnp.clip(seg_cnt - (tile_start - seg_start), 0.0, TE)
    tab_ref[0, 0:1, :] = tile_e.astype(jnp.int32)
    tab_ref[0, 1:2, :] = rows.astype(jnp.int32)
    tab_ref[0, 2:3, :] = n_valid.astype(jnp.int32)
    tab_ref[0, 3:, :] = jnp.zeros((SUBLANES - 3, LANES), jnp.int32)


def _expert_kernel(te_ref, nv_ref, nrows_ref,
                   src_ref, src_next_ref, dst_ref,
                   dest_k0_ref, dest_k1_ref,
                   h2_ref, w1_ref, w3_ref, w2_ref,
                   out_hbm,
                   xt0, xt1, ob0, ob1, fill_vmem, asg_last, ssem, *, t_total, t_part):
    h = pl.program_id(0)
    i = pl.program_id(1)
    g = h * pl.num_programs(1) + i
    nv = nv_ref[h]
    slab = lambda r: pl.ds(pl.multiple_of(r * ROW_TILES, ROW_TILES), ROW_TILES)
    toks_per_tile = TE // TOP_K
    inv_tiles = t_part // toks_per_tile

    def last_part_src(tile):
        def src(r):
            a = asg_last[0, tile * TE + r]
            return a - jnp.where(a >= t_total, t_total + t_part, t_part)
        return src

    first_tables = (lambda r: src_ref[0, 0, r], lambda r: src_next_ref[0, 0, r],
                    lambda first, uu: dst_ref[0, 0, first + uu])
    last_tables = (last_part_src(i), last_part_src(jnp.minimum(i + 1, nv - 1)),
                   lambda first, uu: asg_last[0, (i * TE + first) + uu])

    def gather(src, xt):
        for r in range(TE):
            xt[r * ROW_TILES:(r + 1) * ROW_TILES, :] = h2_ref[slab(src(r)), :]

    def invert_share():
        first_tok = jnp.minimum(i, inv_tiles - 1) * toks_per_tile
        for k, dest_ref in enumerate((dest_k0_ref, dest_k1_ref)):
            first_asg = k * t_total + t_part + first_tok
            for t in range(toks_per_tile):
                asg_last[0, dest_ref[0, 0, t]] = first_asg + t

    def wait_scatter(tile, ob, other, sem):
        n = nrows_ref[tile] * ROW_TILES
        pltpu.make_async_copy(other.at[pl.ds(0, n), :], ob.at[pl.ds(0, n), :], sem).wait()

    def tile_body(xt, xt_next, ob, ob_other, sem, sem_other, tables, first_part):
        _, src_next, dst = tables

        @pl.when(i >= 2)
        def _():
            wait_scatter(g - 2, ob, ob_other, sem)

        def compute(m):
            gather(src_next, xt_next)
            if first_part:
                invert_share()
            x = jnp.concatenate([_row_slab(xt, s, m) for s in range(ROW_TILES)], axis=1).astype(jnp.bfloat16)
            a = jnp.dot(x, w1_ref[0].astype(jnp.bfloat16), preferred_element_type=jnp.float32)
            b = jnp.dot(x, w3_ref[0].astype(jnp.bfloat16), preferred_element_type=jnp.float32)
            hmid = (a * jax.nn.sigmoid(a)) * b
            o = jnp.dot(hmid.astype(jnp.bfloat16), w2_ref[0].astype(jnp.bfloat16),
                        preferred_element_type=jnp.float32)
            for s in range(ROW_TILES):
                ob[pl.ds(s, m, stride=ROW_TILES), :] = o[:, s * LANES:(s + 1) * LANES]

        n = nrows_ref[g]

        @pl.when(n > TE // 2)
        def _():
            compute(TE)

        @pl.when(n <= TE // 2)
        def _():
            compute(TE // 2)

        n_full = n // DMA_UNROLL
        copy = lambda first, uu: pltpu.make_async_copy(ob.at[slab(first + uu), :], out_hbm.at[dst(first, uu)], sem)

        def body(rb, carry):
            for uu in range(DMA_UNROLL):
                copy(rb * DMA_UNROLL, uu).start(priority=uu % 2)
            return carry
        lax.fori_loop(0, n_full, body, 0)

        def tail(r, carry):
            copy(r, 0).start(priority=1)
            return carry
        lax.fori_loop(n_full * DMA_UNROLL, n, tail, 0)

        @pl.when(i == nv - 1)
        def _():
            @pl.when(i >= 1)
            def _():
                wait_scatter(g - 1, ob_other, ob, sem_other)
            wait_scatter(g, ob, ob_other, sem)

    def part_body(tables, first_part):
        @pl.when(i == 0)
        def _():
            gather(tables[0], xt0)

        @pl.when(i % 2 == 0)
        def _():
            tile_body(xt0, xt1, ob0, ob1, ssem.at[0], ssem.at[1], tables, first_part)

        @pl.when(i % 2 == 1)
        def _():
            tile_body(xt1, xt0, ob1, ob0, ssem.at[1], ssem.at[0], tables, first_part)

    @pl.when((h == 0) & (i == 0))
    def _():
        fill_vmem[...] = jnp.full(fill_vmem.shape, t_part, jnp.int32)
        pltpu.sync_copy(fill_vmem, asg_last)

    @pl.when((i < nv) & (h == 0))
    def _():
        part_body(first_tables, True)

    @pl.when((i < nv) & (h > 0))
    def _():
        part_body(last_tables, False)


def _combine_kernel(x1_ref, gcol_ref, nfg_ref, o0_ref, o1_ref, yp_ref, ys_ref, *, n_prompt_tiles):
    i = pl.program_id(0)
    o0 = jnp.concatenate([_row_slab(o0_ref, s, TC) for s in range(ROW_TILES)], axis=1)
    o1 = jnp.concatenate([_row_slab(o1_ref, s, TC) for s in range(ROW_TILES)], axis=1)
    g = gcol_ref[...]
    moe = g[:, 0:1] * o0 + g[:, 1:2] * o1
    y = _rmsnorm(x1_ref[...] + moe, nfg_ref[...])

    @pl.when(i < n_prompt_tiles)
    def _():
        yp_ref[...] = y

    @pl.when(i >= n_prompt_tiles)
    def _():
        for blk in range(TC // TM):
            for q in range(TM // SEQ_BLK):
                r0 = blk * TM + q * SEQ_BLK
                ys_ref[blk * SEQ_BLK:(blk + 1) * SEQ_BLK, q, :] = y[r0:r0 + SEQ_BLK, :]


def _const_spec(shape):
    return pl.BlockSpec(shape, lambda *_: (0,) * len(shape))


def kernel(x_prompt, x_sample, state_pool, norm1_g, w_in, a_norm_g, a_norm_b, a_ws, a_bs, b_w, b_scale, w_out,
           norm2_g, r1_w, r1_b, r2_w, r2_b, exp_w1, exp_w3, exp_w2, normf_g):
    f32, bf16, i32 = jnp.float32, jnp.bfloat16, jnp.int32
    n_batch, seq, _ = x_prompt.shape
    dec_batch, dec_seq, _ = x_sample.shape
    assert norm1_g.shape[0] == 1 and seq % TM == 0 and TM % CHUNK == 0
    assert dec_seq * SEQ_BLK == TM and dec_batch % SEQ_BLK == 0 and dec_seq <= CHUNK
    t_prompt = n_batch * seq
    t_sample = dec_batch * dec_seq
    t_total = t_prompt + t_sample
    n_tok_tiles = t_total // TM
    n_prompt_tiles = t_prompt // TM
    n_sample_tiles = t_sample // TM
    seq_tiles = seq // TM
    plane_rows = t_total * ROW_TILES
    assert n_tok_tiles % N_PARTS == 0
    part_tiles = n_tok_tiles // N_PARTS
    t_part = part_tiles * TM
    n_exp_tiles = -(-(TOP_K * t_part + N_EXPERTS * (TE - 1)) // TE)
    p_rows = n_exp_tiles * TE

    n1g = norm1_g[0][None, :]
    n2g = norm2_g[0][None, :]
    nfg = normf_g[None, :]
    win = w_in[0].astype(bf16)
    wout = w_out[0].astype(bf16)
    ang = a_norm_g[0][None, :]
    anb = a_norm_b[0][None, :]
    bw = b_w[0].astype(bf16)
    bscale = b_scale[0][None, :]
    abias = jnp.repeat(a_bs[0][:, :CHUNK].T, HEAD_DIM, axis=1)
    rw = jnp.concatenate([r1_w[0], r2_w[0].transpose(1, 0, 2).reshape(D_MODEL, N_EXPERTS),
                          jnp.zeros((D_MODEL, N_ROUTER_ROWS - N_GROUPS - N_EXPERTS), f32)], axis=1)
    rw_hi = rw.astype(bf16)
    rw_lo = (rw - rw_hi.astype(f32)).astype(bf16)
    rwt = jnp.concatenate([rw_hi, rw_lo, jnp.zeros((D_MODEL, LANES - 2 * N_ROUTER_ROWS), bf16)], axis=1)
    rbias = jnp.concatenate([r1_b[0], r2_b[0].reshape(-1),
                             jnp.zeros((N_ROUTER_ROWS - N_GROUPS - N_EXPERTS,), f32)])
    rb = jnp.broadcast_to(rbias[:, None], (N_ROUTER_ROWS, TM))
    su = (jnp.arange(TM)[:, None] < jnp.arange(TM)[None, :]).astype(bf16)
    st_planes = state_pool[0].reshape(n_sample_tiles, SEQ_BLK, POOL_STATE, B_WIDTH).transpose(0, 2, 1, 3)
    aws_s = a_ws[0][:, :dec_seq, :dec_seq].reshape(-1)
    abs_s = a_bs[0][:, :dec_seq].reshape(-1)

    cparams = pltpu.CompilerParams(dimension_semantics=("arbitrary",), vmem_limit_bytes=VMEM_LIMIT)

    p_tile = lambda i: jnp.minimum(i, n_prompt_tiles - 1)
    s_tile = lambda i: jnp.maximum(i - n_prompt_tiles, 0)
    sample_rows_shape = jax.ShapeDtypeStruct((dec_batch, dec_seq, A_WIDTH), f32)
    assert N_PARTS >= 2 and n_prompt_tiles >= part_tiles and t_part % TM == 0
    x1, h2, meta, gcol, counts, pstate, p_rows_sample, v_rows_sample, asg_head = pl.pallas_call(
        functools.partial(_mixer_kernel, n_prompt_tiles=n_prompt_tiles, seq_tiles=seq_tiles, part_tiles=part_tiles,
                          t_total=t_total),
        grid_spec=pltpu.PrefetchScalarGridSpec(
            num_scalar_prefetch=2,
            grid=(n_tok_tiles,),
            in_specs=[
                pl.BlockSpec((1, TM, D_MODEL), lambda i, *_: (0, 0, 0)),
                pl.BlockSpec((1, TM, D_MODEL),
                             lambda i, *_: (p_tile(i + 1) // seq_tiles, p_tile(i + 1) % seq_tiles, 0)),
                pl.BlockSpec((SEQ_BLK, dec_seq, D_MODEL),
                             lambda i, *_: (jnp.minimum(s_tile(i + 1), n_sample_tiles - 1), 0, 0)),
                pl.BlockSpec((1, POOL_STATE, SEQ_BLK, B_WIDTH), lambda i, *_: (s_tile(i), 0, 0, 0)),
                _const_spec((1, D_MODEL)), _const_spec((D_MODEL, 3 * A_WIDTH)), _const_spec((1, A_WIDTH)),
                _const_spec((1, A_WIDTH)), _const_spec((N_HEADS, CHUNK, CHUNK)), _const_spec((CHUNK, A_WIDTH)),
                _const_spec((N_GROUPS, HEAD_DIM, HEAD_DIM)), _const_spec((1, B_WIDTH)),
                _const_spec((D_MODEL, D_MODEL)), _const_spec((1, D_MODEL)), _const_spec((D_MODEL, LANES)),
                _const_spec((N_ROUTER_ROWS, TM)), _const_spec((TM, TM)),
            ],
            out_specs=[
                pl.BlockSpec((TM, D_MODEL), lambda i, *_: (i, 0)),
                pl.BlockSpec((TM * ROW_TILES, LANES), lambda i, *_: (i, 0)),
                pl.BlockSpec((SUBLANES, TM), lambda i, *_: (0, i)),
                pl.BlockSpec((TM, LANES), lambda i, *_: (i, 0)),
                pl.BlockSpec((1, N_EXPERTS, LANES), lambda i, *_: (i // part_tiles, 0, 0)),
                pl.BlockSpec((1, CARRY_ROWS, B_WIDTH), lambda i, *_: (p_tile(i) // seq_tiles, 0, 0)),
                pl.BlockSpec((SEQ_BLK, dec_seq, A_WIDTH), lambda i, *_: (s_tile(i), 0, 0)),
                pl.BlockSpec((SEQ_BLK, dec_seq, A_WIDTH), lambda i, *_: (s_tile(i), 0, 0)),
                pl.BlockSpec((1, 1, p_rows), lambda i, *_: (jnp.maximum(i // part_tiles - 1, 0), 0, 0),
                             memory_space=pltpu.SMEM),
            ],
            scratch_shapes=[pltpu.VMEM((TM, D_MODEL), bf16), pltpu.VMEM((CARRY_ROWS, B_WIDTH), f32),
                            pltpu.VMEM((N_EXPERTS, LANES), f32), pltpu.VMEM((TM, D_MODEL), f32),
                            pltpu.VMEM((TM, A_WIDTH), f32), pltpu.VMEM((TM, A_WIDTH), f32),
                            pltpu.VMEM((TM, B_WIDTH), f32),
                            pltpu.VMEM((2 * TOP_K, t_part), i32), pltpu.VMEM((TOP_K, 1, t_part), i32),
                            pltpu.VMEM((1, 1, p_rows), i32)] + [pltpu.SMEM((1, t_part), i32)] * TOP_K,
        ),
        out_shape=[
            jax.ShapeDtypeStruct((t_total, D_MODEL), f32),
            jax.ShapeDtypeStruct((plane_rows, LANES), f32),
            jax.ShapeDtypeStruct((SUBLANES, t_total), i32),
            jax.ShapeDtypeStruct((t_total, LANES), f32),
            jax.ShapeDtypeStruct((N_PARTS, N_EXPERTS, LANES), f32),
            jax.ShapeDtypeStruct((n_batch, CARRY_ROWS, B_WIDTH), f32),
            sample_rows_shape, sample_rows_shape,
            jax.ShapeDtypeStruct((N_PARTS - 1, 1, p_rows), i32),
        ],
        compiler_params=cparams,
        name="mixer",
    )(aws_s, abs_s, x_prompt, x_prompt, x_sample, st_planes, n1g, win, ang, anb, a_ws[0][:, :CHUNK, :CHUNK], abias,
      bw, bscale, wout, n2g, rwt, rb, su)

    assert n_exp_tiles <= LANES and t_part % LANES == 0
    dest_last, tab = pl.pallas_call(
        _tables_kernel,
        grid=(N_PARTS,),
        in_specs=[pl.BlockSpec((1, N_EXPERTS, LANES), lambda h: (h, 0, 0)),
                  pl.BlockSpec((SUBLANES, t_part), lambda h: (0, h))],
        out_specs=[_const_spec((TOP_K, 1, t_part)),
                   pl.BlockSpec((1, SUBLANES, LANES), lambda h: (h, 0, 0))],
        out_shape=[jax.ShapeDtypeStruct((TOP_K, 1, t_part), i32),
                   jax.ShapeDtypeStruct((N_PARTS, SUBLANES, LANES), i32)],
        compiler_params=cparams,
        name="route_tables",
    )(counts, meta)
    tile_e = tab[:, 0, :n_exp_tiles].reshape(-1)
    tile_rows = tab[:, 1, :n_exp_tiles].reshape(-1)
    n_valid = tab[:, 2, 0]
    assert N_PARTS == 2 and (TOP_K * t_part) % TE == 0
    asg = asg_head.reshape(-1)
    row_tok = jnp.where(asg >= t_total, asg - t_total, asg)
    row_src = jnp.where(asg >= 0, row_tok, 0).reshape(n_exp_tiles, 1, TE)
    row_dst = jnp.maximum(asg, 0).reshape(n_exp_tiles, 1, TE)

    flat = lambda h, i: h * n_exp_tiles + i
    cur_blk = lambda h, i, te, nv, nr: (jnp.minimum(i, nv[0] - 1), 0, 0)
    nxt_blk = lambda h, i, te, nv, nr: (jnp.minimum(i + 1, nv[0] - 1), 0, 0)
    w_blk = lambda h, i, te, nv, nr: (te[flat(h, i)], 0, 0)
    smem_blk = lambda imap: pl.BlockSpec((1, 1, TE), imap, memory_space=pltpu.SMEM)
    last_share = t_part // (TE // TOP_K) - 1
    share_blk = lambda k: pl.BlockSpec(
        (1, 1, TE // TOP_K), lambda h, i, *_: (k, 0, jnp.where(h == 0, jnp.minimum(i, last_share), last_share)),
        memory_space=pltpu.SMEM)
    row_buf = pltpu.VMEM((TE * ROW_TILES, LANES), f32)
    out_tok = pl.pallas_call(
        functools.partial(_expert_kernel, t_total=t_total, t_part=t_part),
        grid_spec=pltpu.PrefetchScalarGridSpec(
            num_scalar_prefetch=3,
            grid=(N_PARTS, n_exp_tiles),
            in_specs=[smem_blk(cur_blk), smem_blk(nxt_blk), smem_blk(cur_blk),
                      share_blk(0), share_blk(1),
                      pl.BlockSpec((t_part * ROW_TILES, LANES), lambda h, i, *_: (h, 0),
                                   pipeline_mode=pl.Buffered(1)),
                      pl.BlockSpec((1, D_MODEL, D_EXPERT), w_blk),
                      pl.BlockSpec((1, D_MODEL, D_EXPERT), w_blk),
                      pl.BlockSpec((1, D_EXPERT, D_MODEL), w_blk)],
            out_specs=pl.BlockSpec(memory_space=pl.ANY),
            scratch_shapes=[row_buf, row_buf, row_buf, row_buf, pltpu.VMEM((1, p_rows), i32),
                            pltpu.SMEM((1, p_rows), i32), pltpu.SemaphoreType.DMA((2,))],
        ),
        out_shape=jax.ShapeDtypeStruct((TOP_K * t_total, ROW_TILES, LANES), f32),
        compiler_params=pltpu.CompilerParams(dimension_semantics=("arbitrary", "arbitrary"),
                                             vmem_limit_bytes=EXPERT_VMEM_LIMIT),
        name="moe_experts",
    )(tile_e, n_valid, tile_rows, row_src, row_src, row_dst, dest_last, dest_last, h2, exp_w1[0], exp_w3[0], exp_w2[0])
    out_tok = out_tok.reshape(TOP_K * plane_rows, LANES)

    assert t_prompt % TC == 0 and t_sample % TC == 0
    n_c_prompt = t_prompt // TC
    y_prompt, y_sample = pl.pallas_call(
        functools.partial(_combine_kernel, n_prompt_tiles=n_c_prompt),
        grid=(t_total // TC,),
        in_specs=[pl.BlockSpec((TC, D_MODEL), lambda i: (i, 0)),
                  pl.BlockSpec((TC, LANES), lambda i: (i, 0)),
                  _const_spec((1, D_MODEL)),
                  pl.BlockSpec((TC * ROW_TILES, LANES), lambda i: (i, 0)),
                  pl.BlockSpec((TC * ROW_TILES, LANES), lambda i: (t_total // TC + i, 0))],
        out_specs=[pl.BlockSpec((TC, D_MODEL), lambda i: (jnp.minimum(i, n_c_prompt - 1), 0)),
                   pl.BlockSpec((TC // dec_seq, dec_seq, D_MODEL), lambda i: (jnp.maximum(i - n_c_prompt, 0), 0, 0))],
        out_shape=[jax.ShapeDtypeStruct((t_prompt, D_MODEL), f32),
                   jax.ShapeDtypeStruct((dec_batch, dec_seq, D_MODEL), f32)],
        compiler_params=cparams,
        name="moe_combine",
    )(x1, gcol, nfg, out_tok, out_tok)

    y_prompt = y_prompt.reshape(n_batch, seq, D_MODEL)
    pool_state_prompt = pstate[None, :, CARRY_ROWS - POOL_STATE:, :]
    pool_state_sample = jnp.concatenate([state_pool[0], p_rows_sample], axis=1)[None, :, -POOL_STATE:, :]
    chunk_v_sample = v_rows_sample[None]
    return (y_prompt, y_sample, pool_state_prompt, pool_state_sample, chunk_v_sample)
```

```python
import functools
import math

import jax
import jax.numpy as jnp
from jax import lax
from jax.experimental import pallas as pl
from jax.experimental.pallas import tpu as pltpu

D_MODEL = 1024
A_WIDTH = 512
B_WIDTH = 512
N_HEADS = 4
HEAD_DIM = 128
CHUNK = 128
POOL_WINDOWS = (2, 4, 8, 16)
POOL_STATE = 15
N_GROUPS = 4
EXPERTS_PER_GROUP = 8
N_EXPERTS = 32
TOP_K = 2
D_EXPERT = 512
EPS = 1e-6

SUBLANES = 8
LANES = 128
ROW_TILES = D_MODEL // LANES

TM = 256
TC = 512
TE = 512
SEQ_BLK = 32
N_ROUTER_ROWS = 40
CARRY_ROWS = 16
DMA_UNROLL = 32
VMEM_LIMIT = 48 * 1024 * 1024
N_PARTS = 2
EXPERT_VMEM_LIMIT = 62 * 1024 * 1024

_INV_SQRT2 = 1.0 / math.sqrt(2.0)


def _rmsnorm(x, g):
    r = lax.rsqrt(jnp.mean(x * x, axis=-1, keepdims=True) + EPS)
    return (x * r) * g


def _gelu(x):
    return 0.5 * x * (1.0 + lax.erf(x * _INV_SQRT2))


def _layernorm(x, g, b):
    mu = jnp.mean(x, axis=-1, keepdims=True)
    xc = x - mu
    var = jnp.mean(xc * xc, axis=-1, keepdims=True)
    return (xc * lax.rsqrt(var + EPS)) * g + b


def _row_slab(ref, s, n):
    return ref[pl.ds(s, n, stride=ROW_TILES), :]


def _pool_project(pooled, g, bw_ref, bscale_ref):
    lo, hi = g * HEAD_DIM, (g + 1) * HEAD_DIM
    hb = jnp.dot(pooled.astype(jnp.bfloat16), bw_ref[g], preferred_element_type=jnp.float32)
    return hb * bscale_ref[:, lo:hi]


def _prompt_mixers(j, u, v, p, aws_ref, abias_ref, bw_ref, bscale_ref, mix_ref, pcarry_ref, pstate_ref):
    tri = (lax.broadcasted_iota(jnp.int32, (CHUNK, CHUNK), 0)
           >= lax.broadcasted_iota(jnp.int32, (CHUNK, CHUNK), 1))
    vb = v.astype(jnp.bfloat16)
    for hd in range(N_HEADS):
        lo, hi = hd * HEAD_DIM, (hd + 1) * HEAD_DIM
        w = jnp.where(tri, aws_ref[hd], 0.0).astype(jnp.bfloat16)
        for c in range(TM // CHUNK):
            r0, r1 = c * CHUNK, (c + 1) * CHUNK
            z = jnp.dot(w, vb[r0:r1, lo:hi], preferred_element_type=jnp.float32) + abias_ref[:, lo:hi]
            mix_ref[r0:r1, lo:hi] = (u[r0:r1, lo:hi] * z).astype(jnp.bfloat16)

    head_pos = j * TM + lax.broadcasted_iota(jnp.int32, (CARRY_ROWS, LANES), 0)
    for g, w in enumerate(POOL_WINDOWS):
        lo, hi = g * HEAD_DIM, (g + 1) * HEAD_DIM
        pg = p[:, lo:hi]
        acc = jnp.concatenate([pcarry_ref[:, lo:hi], pg], axis=0)
        shift = 1
        while shift < w:
            acc = acc + pltpu.roll(acc, shift, 0)
            shift *= 2
        head = acc[CARRY_ROWS:2 * CARRY_ROWS, :] / jnp.minimum(head_pos + 1, w).astype(jnp.float32)
        mean = jnp.concatenate([head, acc[2 * CARRY_ROWS:, :] * (1.0 / w)], axis=0)
        pooled = mean - pg
        mix_ref[:, A_WIDTH + lo:A_WIDTH + hi] = _pool_project(pooled, g, bw_ref, bscale_ref).astype(jnp.bfloat16)
    tail = p[TM - CARRY_ROWS:, :]
    pcarry_ref[...] = tail
    pstate_ref[0] = tail


def _position_major(ref):
    n_pos = TM // SEQ_BLK
    rows = ref.reshape(SEQ_BLK, n_pos, ref.shape[-1])
    return jnp.concatenate([rows[:, q, :] for q in range(n_pos)], axis=0)


def _sample_mixers(u, v, p, aws_ref, abs_ref, st_ref, bw_ref, bscale_ref, mix_ref):
    n_pos = TM // SEQ_BLK

    for hd in range(N_HEADS):
        lo, hi = hd * HEAD_DIM, (hd + 1) * HEAD_DIM
        vplanes = [v[s * SEQ_BLK:(s + 1) * SEQ_BLK, lo:hi] for s in range(n_pos)]
        for i in range(n_pos):
            z = vplanes[0] * aws_ref[(hd * n_pos + i) * n_pos]
            for s in range(1, i + 1):
                z = z + vplanes[s] * aws_ref[(hd * n_pos + i) * n_pos + s]
            z = z + abs_ref[hd * n_pos + i]
            r0, r1 = i * SEQ_BLK, (i + 1) * SEQ_BLK
            mix_ref[r0:r1, lo:hi] = (u[r0:r1, lo:hi] * z).astype(jnp.bfloat16)

    for g, w in enumerate(POOL_WINDOWS):
        lo, hi = g * HEAD_DIM, (g + 1) * HEAD_DIM
        planes = [st_ref[0, k, :, lo:hi] for k in range(POOL_STATE)]
        planes += [p[i * SEQ_BLK:(i + 1) * SEQ_BLK, lo:hi] for i in range(n_pos)]
        pooled = []
        for i in range(n_pos):
            top = POOL_STATE + i
            s = planes[top - w + 1]
            for k in range(top - w + 2, top + 1):
                s = s + planes[k]
            pooled.append(s * (1.0 / w) - planes[top])
        pooled = jnp.concatenate(pooled, axis=0)
        mix_ref[:, A_WIDTH + lo:A_WIDTH + hi] = _pool_project(pooled, g, bw_ref, bscale_ref).astype(jnp.bfloat16)


def _route(h2, rwt_ref, rb_ref, su_ref, cnt_ref, meta_ref, gcol_ref):
    tm = h2.shape[0]
    h_hi = h2.astype(jnp.bfloat16)
    h_lo = (h2 - h_hi.astype(jnp.float32)).astype(jnp.bfloat16)
    s = (jnp.dot(h_hi, rwt_ref[...], preferred_element_type=jnp.float32)
         + jnp.dot(h_lo, rwt_ref[...], preferred_element_type=jnp.float32))
    st = s.T
    lt = st[0:N_ROUTER_ROWS, :] + st[N_ROUTER_ROWS:2 * N_ROUTER_ROWS, :] + rb_ref[...]
    row = lambda i: lt[i:i + 1, :]
    l1 = [row(i) for i in range(N_GROUPS)]
    m1 = jnp.maximum(jnp.maximum(l1[0], l1[1]), jnp.maximum(l1[2], l1[3]))
    grp = jnp.where(l1[0] == m1, 0, jnp.where(l1[1] == m1, 1, jnp.where(l1[2] == m1, 2, 3)))
    se = (jnp.exp(l1[0] - m1) + jnp.exp(l1[1] - m1)) + (jnp.exp(l1[2] - m1) + jnp.exp(l1[3] - m1))
    pg = 1.0 / se
    l2 = []
    for e in range(EXPERTS_PER_GROUP):
        c = [row(N_GROUPS + g * EXPERTS_PER_GROUP + e) for g in range(N_GROUPS)]
        l2.append(jnp.where(grp == 0, c[0], jnp.where(grp == 1, c[1], jnp.where(grp == 2, c[2], c[3]))))
    v0 = functools.reduce(jnp.maximum, l2)
    i0 = jnp.full_like(grp, EXPERTS_PER_GROUP - 1)
    for e in range(EXPERTS_PER_GROUP - 2, -1, -1):
        i0 = jnp.where(l2[e] == v0, e, i0)
    neg = jnp.float32(-jnp.inf)
    l2m = [jnp.where(i0 == e, neg, l2[e]) for e in range(EXPERTS_PER_GROUP)]
    v1 = functools.reduce(jnp.maximum, l2m)
    i1 = jnp.full_like(grp, EXPERTS_PER_GROUP - 1)
    for e in range(EXPERTS_PER_GROUP - 2, -1, -1):
        i1 = jnp.where((l2m[e] == v1) & (i0 != e), e, i1)
    d = jnp.exp(v1 - v0)
    g0 = pg / (1.0 + d)
    g1 = (pg * d) / (1.0 + d)
    e0 = grp * EXPERTS_PER_GROUP + i0
    e1 = grp * EXPERTS_PER_GROUP + i1

    eiota = lax.broadcasted_iota(jnp.int32, (N_EXPERTS, tm), 0)
    hit0 = eiota == e0
    hit1 = eiota == e1
    onehot = (hit0 | hit1).astype(jnp.bfloat16)
    prefix = jnp.dot(onehot, su_ref[...], preferred_element_type=jnp.float32)
    carry = cnt_ref[...]
    base = prefix + jnp.concatenate([carry] * (tm // LANES), axis=1)
    r0 = jnp.sum(jnp.where(hit0, base, 0.0), axis=0, keepdims=True)
    r1 = jnp.sum(jnp.where(hit1, base, 0.0), axis=0, keepdims=True)
    ones = jnp.ones((tm, LANES), jnp.bfloat16)
    cnt_ref[...] = carry + jnp.dot(onehot, ones, preferred_element_type=jnp.float32)

    meta_ref[0:1, :] = e0
    meta_ref[1:2, :] = e1
    meta_ref[2:3, :] = r0.astype(jnp.int32)
    meta_ref[3:4, :] = r1.astype(jnp.int32)
    meta_ref[4:8, :] = jnp.zeros((4, tm), jnp.int32)

    giota = lax.broadcasted_iota(jnp.int32, (LANES, tm), 0)
    gpad = jnp.where(giota == 0, g0, jnp.where(giota == 1, g1, 0.0))
    gcol_ref[...] = gpad.T


def _mixer_kernel(aws_s_ref, abs_s_ref,
                  x0_ref, xpn_ref, xsn_ref,
                  st_ref, n1g_ref, win_ref, ang_ref, anb_ref, aws_ref, abias_ref, bw_ref, bscale_ref,
                  wout_ref, n2g_ref, rwt_ref, rb_ref, su_ref,
                  x1_ref, h2_ref, meta_ref, gcol_ref, counts_ref, pstate_ref, pp_ref, vp_ref, asg_ref,
                  mix_ref, pcarry_ref, cnt_ref, x_ref, u_ref, v_ref, p_ref, part_meta, dest_vmem, fill_vmem, *dest_smem,
                  n_prompt_tiles, seq_tiles, part_tiles, t_total):
    i = pl.program_id(0)
    is_prompt = i < n_prompt_tiles
    j = i % seq_tiles
    part = i // part_tiles
    tile_in_part = i % part_tiles

    @pl.when((i > 0) & (tile_in_part == 0))
    def _():
        _, _, pad_start = _segment_offsets(cnt_ref[...])
        _stage_inversion(part_meta, pad_start, dest_vmem, dest_smem, fill_vmem, asg_ref)

    def in_proj(x):
        h = _rmsnorm(x, n1g_ref[...]).astype(jnp.bfloat16)
        proj = jnp.dot(h, win_ref[...], preferred_element_type=jnp.float32)
        uv = _gelu(proj[:, : 2 * A_WIDTH])
        u_ref[...] = uv[:, :A_WIDTH]
        v_ref[...] = _layernorm(uv[:, A_WIDTH:], ang_ref[...], anb_ref[...])
        p_ref[...] = proj[:, 2 * A_WIDTH:]
        x_ref[...] = x

    @pl.when(i == 0)
    def _():
        in_proj(x0_ref[0])

    @pl.when(i % part_tiles == 0)
    def _():
        cnt_ref[...] = jnp.zeros_like(cnt_ref)

    @pl.when(is_prompt & (j == 0))
    def _():
        pcarry_ref[...] = jnp.zeros_like(pcarry_ref)

    def finish_tile(x, invert_previous_part):
        x1 = x + jnp.dot(mix_ref[...], wout_ref[...], preferred_element_type=jnp.float32)
        x1_ref[...] = x1
        h2 = _rmsnorm(x1, n2g_ref[...])
        for s in range(ROW_TILES):
            h2_ref[pl.ds(s, TM, stride=ROW_TILES), :] = h2[:, s * LANES:(s + 1) * LANES]
        _route(h2, rwt_ref, rb_ref, su_ref, cnt_ref, meta_ref, gcol_ref)
        counts_ref[0] = cnt_ref[...]
        part_meta[:, pl.ds(pl.multiple_of(tile_in_part * TM, TM), TM)] = meta_ref[0:2 * TOP_K, :]
        if invert_previous_part:
            first_tok = tile_in_part * TM
            for k, dsm in enumerate(dest_smem):
                first_asg = k * t_total + (part - 1) * (part_tiles * TM) + first_tok
                for t in range(TM):
                    asg_ref[0, 0, dsm[0, first_tok + t]] = first_asg + t
        nxt = jnp.minimum(i + 1, pl.num_programs(0) - 1)
        in_proj(jnp.where(nxt < n_prompt_tiles, xpn_ref[0], xsn_ref[...].reshape(TM, D_MODEL)))

    @pl.when(is_prompt & (part == 0))
    def _():
        _prompt_mixers(j, u_ref[...], v_ref[...], p_ref[...], aws_ref, abias_ref, bw_ref, bscale_ref, mix_ref,
                       pcarry_ref, pstate_ref)
        finish_tile(x_ref[...], False)

    @pl.when(is_prompt & (part > 0))
    def _():
        _prompt_mixers(j, u_ref[...], v_ref[...], p_ref[...], aws_ref, abias_ref, bw_ref, bscale_ref, mix_ref,
                       pcarry_ref, pstate_ref)
        finish_tile(x_ref[...], True)

    @pl.when(jnp.logical_not(is_prompt))
    def _():
        vp_ref[...] = v_ref[...].reshape(vp_ref.shape)
        pp_ref[...] = p_ref[...].reshape(pp_ref.shape)
        _sample_mixers(_position_major(u_ref), _position_major(v_ref), _position_major(p_ref), aws_s_ref, abs_s_ref,
                       st_ref, bw_ref, bscale_ref, mix_ref)
        finish_tile(_position_major(x_ref), True)


def _segment_offsets(cnt):
    padded = jnp.floor((cnt + (TE - 1)) * (1.0 / TE)) * TE
    sub = lax.broadcasted_iota(jnp.int32, cnt.shape, 0)
    pad_end = padded
    shift = 1
    while shift < N_EXPERTS:
        pad_end = pad_end + jnp.where(sub >= shift, pltpu.roll(pad_end, shift, 0), 0.0)
        shift *= 2
    return padded, pad_end, pad_end - padded


def _dest_rows(meta_ref, pad_start, dest_ref):
    t_part = meta_ref.shape[1]
    base = pad_start.astype(jnp.int32)
    for k in range(TOP_K):
        e = meta_ref[k:k + 1, :]
        d = meta_ref[TOP_K + k:TOP_K + k + 1, :]
        for ex in range(N_EXPERTS):
            row = jnp.concatenate([base[ex:ex + 1, :]] * (t_part // LANES), axis=1)
            d = d + jnp.where(e == ex, row, 0)
        dest_ref[k] = d


def _stage_inversion(meta_ref, pad_start, dest_vmem, dest_smem, fill_vmem, asg_ref):
    _dest_rows(meta_ref, pad_start, dest_vmem)
    for k, dsm in enumerate(dest_smem):
        pltpu.sync_copy(dest_vmem.at[k], dsm)
    fill_vmem[...] = jnp.full(fill_vmem.shape, -1, jnp.int32)
    pltpu.sync_copy(fill_vmem, asg_ref)


def _tables_kernel(counts_ref, meta_ref, dest_ref, tab_ref):
    h = pl.program_id(0)
    cnt = counts_ref[0]
    sub = lax.broadcasted_iota(jnp.int32, cnt.shape, 0)
    padded, pad_end, pad_start = _segment_offsets(cnt)

    @pl.when(h == pl.num_programs(0) - 1)
    def _():
        _dest_rows(meta_ref, pad_start, dest_ref)

    n_valid = jnp.maximum(pad_end[N_EXPERTS - 1:, :] * (1.0 / TE), 1.0)
    tile = jnp.minimum(lax.broadcasted_iota(jnp.int32, (1, LANES), 1).astype(jnp.float32), n_valid - 1.0)
    tile_start = tile * TE
    tile_e = jnp.minimum(jnp.sum((pad_end <= tile_start).astype(jnp.float32), axis=0, keepdims=True),
                         N_EXPERTS - 1.0)
    is_e = sub.astype(jnp.float32) == tile_e
    seg_start = jnp.sum(jnp.where(is_e, pad_start, 0.0), axis=0, keepdims=True)
    seg_cnt = jnp.sum(jnp.where(is_e, cnt, 0.0), axis=0, keepdims=True)
    rows = jnp.clip(seg_cnt - (tile_start - seg_start), 0.0, TE)
    tab_ref[0, 0:1, :] = tile_e.astype(jnp.int32)
    tab_ref[0, 1:2, :] = rows.astype(jnp.int32)
    tab_ref[0, 2:3, :] = n_valid.astype(jnp.int32)
    tab_ref[0, 3:, :] = jnp.zeros((SUBLANES - 3, LANES), jnp.int32)


def _expert_kernel(te_ref, nv_ref, nrows_ref,
                   src_ref, src_next_ref, dst_ref,
                   dest_k0_ref, dest_k1_ref,
                   h2_hbm, w1_ref, w3_ref, w2_ref,
                   out_hbm,
                   xt0, xt1, ob0, ob1, fill_vmem, asg_last, ssem, h2_ref, hsem, *, t_total, t_part):
    h = pl.program_id(0)
    i = pl.program_id(1)
    g = h * pl.num_programs(1) + i
    nv = nv_ref[h]
    slab = lambda r: pl.ds(pl.multiple_of(r * ROW_TILES, ROW_TILES), ROW_TILES)
    toks_per_tile = TE // TOP_K
    inv_tiles = t_part // toks_per_tile
    part_slabs = t_part * ROW_TILES
    fetch_part = lambda part: pltpu.make_async_copy(h2_hbm.at[pl.ds(part * part_slabs, part_slabs), :], h2_ref, hsem)

    def last_part_src(tile):
        def src(r):
            a = asg_last[0, tile * TE + r]
            return a - jnp.where(a >= t_total, t_total + t_part, t_part)
        return src

    first_tables = (lambda r: src_ref[0, 0, r], lambda r: src_next_ref[0, 0, r],
                    lambda first, uu: dst_ref[0, 0, first + uu])
    last_tables = (last_part_src(i), last_part_src(jnp.minimum(i + 1, nv - 1)),
                   lambda first, uu: asg_last[0, (i * TE + first) + uu])

    def gather(src, xt):
        for r in range(TE):
            xt[r * ROW_TILES:(r + 1) * ROW_TILES, :] = h2_ref[slab(src(r)), :]

    def invert_share():
        first_tok = jnp.minimum(i, inv_tiles - 1) * toks_per_tile
        for k, dest_ref in enumerate((dest_k0_ref, dest_k1_ref)):
            first_asg = k * t_total + t_part + first_tok
            for t in range(toks_per_tile):
                asg_last[0, dest_ref[0, 0, t]] = first_asg + t

    def wait_scatter(tile, ob, other, sem):
        n = nrows_ref[tile] * ROW_TILES
        pltpu.make_async_copy(other.at[pl.ds(0, n), :], ob.at[pl.ds(0, n), :], sem).wait()

    def tile_body(xt, xt_next, ob, ob_other, sem, sem_other, tables, first_part):
        _, src_next, dst = tables

        @pl.when(i >= 2)
        def _():
            wait_scatter(g - 2, ob, ob_other, sem)

        def compute(m):
            gather(src_next, xt_next)
            if first_part:
                invert_share()
            x = jnp.concatenate([_row_slab(xt, s, m) for s in range(ROW_TILES)], axis=1).astype(jnp.bfloat16)
            a = jnp.dot(x, w1_ref[0].astype(jnp.bfloat16), preferred_element_type=jnp.float32)
            b = jnp.dot(x, w3_ref[0].astype(jnp.bfloat16), preferred_element_type=jnp.float32)
            hmid = (a * jax.nn.sigmoid(a)) * b
            o = jnp.dot(hmid.astype(jnp.bfloat16), w2_ref[0].astype(jnp.bfloat16),
                        preferred_element_type=jnp.float32)
            for s in range(ROW_TILES):
                ob[pl.ds(s, m, stride=ROW_TILES), :] = o[:, s * LANES:(s + 1) * LANES]

        n = nrows_ref[g]

        @pl.when(n > TE // 2)
        def _():
            compute(TE)

        @pl.when(n <= TE // 2)
        def _():
            compute(TE // 2)

        n_full = n // DMA_UNROLL
        copy = lambda first, uu: pltpu.make_async_copy(ob.at[slab(first + uu), :], out_hbm.at[dst(first, uu)], sem)

        def body(rb, carry):
            for uu in range(DMA_UNROLL):
                copy(rb * DMA_UNROLL, uu).start(priority=uu % 2)
            return carry
        lax.fori_loop(0, n_full, body, 0)

        def tail(r, carry):
            copy(r, 0).start(priority=1)
            return carry
        lax.fori_loop(n_full * DMA_UNROLL, n, tail, 0)

        @pl.when(i == nv - 1)
        def _():
            @pl.when(i >= 1)
            def _():
                wait_scatter(g - 1, ob_other, ob, sem_other)
            wait_scatter(g, ob, ob_other, sem)
            if first_part:
                fetch_part(1).start()

    def part_body(tables, first_part):
        @pl.when(i == 0)
        def _():
            if not first_part:
                fetch_part(1).wait()
            gather(tables[0], xt0)

        @pl.when(i % 2 == 0)
        def _():
            tile_body(xt0, xt1, ob0, ob1, ssem.at[0], ssem.at[1], tables, first_part)

        @pl.when(i % 2 == 1)
        def _():
            tile_body(xt1, xt0, ob1, ob0, ssem.at[1], ssem.at[0], tables, first_part)

    @pl.when((h == 0) & (i == 0))
    def _():
        fetch_part(0).start()
        fill_vmem[...] = jnp.full(fill_vmem.shape, t_part, jnp.int32)
        pltpu.sync_copy(fill_vmem, asg_last)
        fetch_part(0).wait()

    @pl.when((i < nv) & (h == 0))
    def _():
        part_body(first_tables, True)

    @pl.when((i < nv) & (h > 0))
    def _():
        part_body(last_tables, False)


def _combine_kernel(x1_ref, gcol_ref, nfg_ref, o0_ref, o1_ref, yp_ref, ys_ref, *, n_prompt_tiles):
    i = pl.program_id(0)
    o0 = jnp.concatenate([_row_slab(o0_ref, s, TC) for s in range(ROW_TILES)], axis=1)
    o1 = jnp.concatenate([_row_slab(o1_ref, s, TC) for s in range(ROW_TILES)], axis=1)
    g = gcol_ref[...]
    moe = g[:, 0:1] * o0 + g[:, 1:2] * o1
    y = _rmsnorm(x1_ref[...] + moe, nfg_ref[...])

    @pl.when(i < n_prompt_tiles)
    def _():
        yp_ref[...] = y

    @pl.when(i >= n_prompt_tiles)
    def _():
        for blk in range(TC // TM):
            for q in range(TM // SEQ_BLK):
                r0 = blk * TM + q * SEQ_BLK
                ys_ref[blk * SEQ_BLK:(blk + 1) * SEQ_BLK, q, :] = y[r0:r0 + SEQ_BLK, :]


def _const_spec(shape):
    return pl.BlockSpec(shape, lambda *_: (0,) * len(shape))


def kernel(x_prompt, x_sample, state_pool, norm1_g, w_in, a_norm_g, a_norm_b, a_ws, a_bs, b_w, b_scale, w_out,
           norm2_g, r1_w, r1_b, r2_w, r2_b, exp_w1, exp_w3, exp_w2, normf_g):
    f32, bf16, i32 = jnp.float32, jnp.bfloat16, jnp.int32
    n_batch, seq, _ = x_prompt.shape
    dec_batch, dec_seq, _ = x_sample.shape
    assert norm1_g.shape[0] == 1 and seq % TM == 0 and TM % CHUNK == 0
    assert dec_seq * SEQ_BLK == TM and dec_batch % SEQ_BLK == 0 and dec_seq <= CHUNK
    t_prompt = n_batch * seq
    t_sample = dec_batch * dec_seq
    t_total = t_prompt + t_sample
    n_tok_tiles = t_total // TM
    n_prompt_tiles = t_prompt // TM
    n_sample_tiles = t_sample // TM
    seq_tiles = seq // TM
    plane_rows = t_total * ROW_TILES
    assert n_tok_tiles % N_PARTS == 0
    part_tiles = n_tok_tiles // N_PARTS
    t_part = part_tiles * TM
    n_exp_tiles = -(-(TOP_K * t_part + N_EXPERTS * (TE - 1)) // TE)
    p_rows = n_exp_tiles * TE

    n1g = norm1_g[0][None, :]
    n2g = norm2_g[0][None, :]
    nfg = normf_g[None, :]
    win = w_in[0].astype(bf16)
    wout = w_out[0].astype(bf16)
    ang = a_norm_g[0][None, :]
    anb = a_norm_b[0][None, :]
    bw = b_w[0].astype(bf16)
    bscale = b_scale[0][None, :]
    abias = jnp.repeat(a_bs[0][:, :CHUNK].T, HEAD_DIM, axis=1)
    rw = jnp.concatenate([r1_w[0], r2_w[0].transpose(1, 0, 2).reshape(D_MODEL, N_EXPERTS),
                          jnp.zeros((D_MODEL, N_ROUTER_ROWS - N_GROUPS - N_EXPERTS), f32)], axis=1)
    rw_hi = rw.astype(bf16)
    rw_lo = (rw - rw_hi.astype(f32)).astype(bf16)
    rwt = jnp.concatenate([rw_hi, rw_lo, jnp.zeros((D_MODEL, LANES - 2 * N_ROUTER_ROWS), bf16)], axis=1)
    rbias = jnp.concatenate([r1_b[0], r2_b[0].reshape(-1),
                             jnp.zeros((N_ROUTER_ROWS - N_GROUPS - N_EXPERTS,), f32)])
    rb = jnp.broadcast_to(rbias[:, None], (N_ROUTER_ROWS, TM))
    su = (jnp.arange(TM)[:, None] < jnp.arange(TM)[None, :]).astype(bf16)
    st_planes = state_pool[0].reshape(n_sample_tiles, SEQ_BLK, POOL_STATE, B_WIDTH).transpose(0, 2, 1, 3)
    aws_s = a_ws[0][:, :dec_seq, :dec_seq].reshape(-1)
    abs_s = a_bs[0][:, :dec_seq].reshape(-1)

    cparams = pltpu.CompilerParams(dimension_semantics=("arbitrary",), vmem_limit_bytes=VMEM_LIMIT)

    p_tile = lambda i: jnp.minimum(i, n_prompt_tiles - 1)
    s_tile = lambda i: jnp.maximum(i - n_prompt_tiles, 0)
    sample_rows_shape = jax.ShapeDtypeStruct((dec_batch, dec_seq, A_WIDTH), f32)
    assert N_PARTS >= 2 and n_prompt_tiles >= part_tiles and t_part % TM == 0
    x1, h2, meta, gcol, counts, pstate, p_rows_sample, v_rows_sample, asg_head = pl.pallas_call(
        functools.partial(_mixer_kernel, n_prompt_tiles=n_prompt_tiles, seq_tiles=seq_tiles, part_tiles=part_tiles,
                          t_total=t_total),
        grid_spec=pltpu.PrefetchScalarGridSpec(
            num_scalar_prefetch=2,
            grid=(n_tok_tiles,),
            in_specs=[
                pl.BlockSpec((1, TM, D_MODEL), lambda i, *_: (0, 0, 0)),
                pl.BlockSpec((1, TM, D_MODEL),
                             lambda i, *_: (p_tile(i + 1) // seq_tiles, p_tile(i + 1) % seq_tiles, 0)),
                pl.BlockSpec((SEQ_BLK, dec_seq, D_MODEL),
                             lambda i, *_: (jnp.minimum(s_tile(i + 1), n_sample_tiles - 1), 0, 0)),
                pl.BlockSpec((1, POOL_STATE, SEQ_BLK, B_WIDTH), lambda i, *_: (s_tile(i), 0, 0, 0)),
                _const_spec((1, D_MODEL)), _const_spec((D_MODEL, 3 * A_WIDTH)), _const_spec((1, A_WIDTH)),
                _const_spec((1, A_WIDTH)), _const_spec((N_HEADS, CHUNK, CHUNK)), _const_spec((CHUNK, A_WIDTH)),
                _const_spec((N_GROUPS, HEAD_DIM, HEAD_DIM)), _const_spec((1, B_WIDTH)),
                _const_spec((D_MODEL, D_MODEL)), _const_spec((1, D_MODEL)), _const_spec((D_MODEL, LANES)),
                _const_spec((N_ROUTER_ROWS, TM)), _const_spec((TM, TM)),
            ],
            out_specs=[
                pl.BlockSpec((TM, D_MODEL), lambda i, *_: (i, 0)),
                pl.BlockSpec((TM * ROW_TILES, LANES), lambda i, *_: (i, 0)),
                pl.BlockSpec((SUBLANES, TM), lambda i, *_: (0, i)),
                pl.BlockSpec((TM, LANES), lambda i, *_: (i, 0)),
                pl.BlockSpec((1, N_EXPERTS, LANES), lambda i, *_: (i // part_tiles, 0, 0)),
                pl.BlockSpec((1, CARRY_ROWS, B_WIDTH), lambda i, *_: (p_tile(i) // seq_tiles, 0, 0)),
                pl.BlockSpec((SEQ_BLK, dec_seq, A_WIDTH), lambda i, *_: (s_tile(i), 0, 0)),
                pl.BlockSpec((SEQ_BLK, dec_seq, A_WIDTH), lambda i, *_: (s_tile(i), 0, 0)),
                pl.BlockSpec((1, 1, p_rows), lambda i, *_: (jnp.maximum(i // part_tiles - 1, 0), 0, 0),
                             memory_space=pltpu.SMEM),
            ],
            scratch_shapes=[pltpu.VMEM((TM, D_MODEL), bf16), pltpu.VMEM((CARRY_ROWS, B_WIDTH), f32),
                            pltpu.VMEM((N_EXPERTS, LANES), f32), pltpu.VMEM((TM, D_MODEL), f32),
                            pltpu.VMEM((TM, A_WIDTH), f32), pltpu.VMEM((TM, A_WIDTH), f32),
                            pltpu.VMEM((TM, B_WIDTH), f32),
                            pltpu.VMEM((2 * TOP_K, t_part), i32), pltpu.VMEM((TOP_K, 1, t_part), i32),
                            pltpu.VMEM((1, 1, p_rows), i32)] + [pltpu.SMEM((1, t_part), i32)] * TOP_K,
        ),
        out_shape=[
            jax.ShapeDtypeStruct((t_total, D_MODEL), f32),
            jax.ShapeDtypeStruct((plane_rows, LANES), f32),
            jax.ShapeDtypeStruct((SUBLANES, t_total), i32),
            jax.ShapeDtypeStruct((t_total, LANES), f32),
            jax.ShapeDtypeStruct((N_PARTS, N_EXPERTS, LANES), f32),
            jax.ShapeDtypeStruct((n_batch, CARRY_ROWS, B_WIDTH), f32),
            sample_rows_shape, sample_rows_shape,
            jax.ShapeDtypeStruct((N_PARTS - 1, 1, p_rows), i32),
        ],
        compiler_params=cparams,
        name="mixer",
    )(aws_s, abs_s, x_prompt, x_prompt, x_sample, st_planes, n1g, win, ang, anb, a_ws[0][:, :CHUNK, :CHUNK], abias,
      bw, bscale, wout, n2g, rwt, rb, su)

    assert n_exp_tiles <= LANES and t_part % LANES == 0
    dest_last, tab = pl.pallas_call(
        _tables_kernel,
        grid=(N_PARTS,),
        in_specs=[pl.BlockSpec((1, N_EXPERTS, LANES), lambda h: (h, 0, 0)),
                  pl.BlockSpec((SUBLANES, t_part), lambda h: (0, h))],
        out_specs=[_const_spec((TOP_K, 1, t_part)),
                   pl.BlockSpec((1, SUBLANES, LANES), lambda h: (h, 0, 0))],
        out_shape=[jax.ShapeDtypeStruct((TOP_K, 1, t_part), i32),
                   jax.ShapeDtypeStruct((N_PARTS, SUBLANES, LANES), i32)],
        compiler_params=cparams,
        name="route_tables",
    )(counts, meta)
    tile_e = tab[:, 0, :n_exp_tiles].reshape(-1)
    tile_rows = tab[:, 1, :n_exp_tiles].reshape(-1)
    n_valid = tab[:, 2, 0]
    assert N_PARTS == 2 and (TOP_K * t_part) % TE == 0
    asg = asg_head.reshape(-1)
    row_tok = jnp.where(asg >= t_total, asg - t_total, asg)
    row_src = jnp.where(asg >= 0, row_tok, 0).reshape(n_exp_tiles, 1, TE)
    row_dst = jnp.maximum(asg, 0).reshape(n_exp_tiles, 1, TE)

    flat = lambda h, i: h * n_exp_tiles + i
    cur_blk = lambda h, i, te, nv, nr: (jnp.minimum(i, nv[0] - 1), 0, 0)
    nxt_blk = lambda h, i, te, nv, nr: (jnp.minimum(i + 1, nv[0] - 1), 0, 0)
    w_blk = lambda h, i, te, nv, nr: (te[flat(h, i)], 0, 0)
    smem_blk = lambda imap: pl.BlockSpec((1, 1, TE), imap, memory_space=pltpu.SMEM)
    last_share = t_part // (TE // TOP_K) - 1
    share_blk = lambda k: pl.BlockSpec(
        (1, 1, TE // TOP_K), lambda h, i, *_: (k, 0, jnp.where(h == 0, jnp.minimum(i, last_share), last_share)),
        memory_space=pltpu.SMEM)
    row_buf = pltpu.VMEM((TE * ROW_TILES, LANES), f32)
    out_tok = pl.pallas_call(
        functools.partial(_expert_kernel, t_total=t_total, t_part=t_part),
        grid_spec=pltpu.PrefetchScalarGridSpec(
            num_scalar_prefetch=3,
            grid=(N_PARTS, n_exp_tiles),
            in_specs=[smem_blk(cur_blk), smem_blk(nxt_blk), smem_blk(cur_blk),
                      share_blk(0), share_blk(1),
                      pl.BlockSpec(memory_space=pl.ANY),
                      pl.BlockSpec((1, D_MODEL, D_EXPERT), w_blk),
                      pl.BlockSpec((1, D_MODEL, D_EXPERT), w_blk),
                      pl.BlockSpec((1, D_EXPERT, D_MODEL), w_blk)],
            out_specs=pl.BlockSpec(memory_space=pl.ANY),
            scratch_shapes=[row_buf, row_buf, row_buf, row_buf, pltpu.VMEM((1, p_rows), i32),
                            pltpu.SMEM((1, p_rows), i32), pltpu.SemaphoreType.DMA((2,)),
                            pltpu.VMEM((t_part * ROW_TILES, LANES), f32), pltpu.SemaphoreType.DMA(())],
        ),
        out_shape=jax.ShapeDtypeStruct((TOP_K * t_total, ROW_TILES, LANES), f32),
        compiler_params=pltpu.CompilerParams(dimension_semantics=("arbitrary", "arbitrary"),
                                             vmem_limit_bytes=EXPERT_VMEM_LIMIT),
        name="moe_experts",
    )(tile_e, n_valid, tile_rows, row_src, row_src, row_dst, dest_last, dest_last, h2, exp_w1[0], exp_w3[0], exp_w2[0])
    out_tok = out_tok.reshape(TOP_K * plane_rows, LANES)

    assert t_prompt % TC == 0 and t_sample % TC == 0
    n_c_prompt = t_prompt // TC
    y_prompt, y_sample = pl.pallas_call(
        functools.partial(_combine_kernel, n_prompt_tiles=n_c_prompt),
        grid=(t_total // TC,),
        in_specs=[pl.BlockSpec((TC, D_MODEL), lambda i: (i, 0)),
                  pl.BlockSpec((TC, LANES), lambda i: (i, 0)),
                  _const_spec((1, D_MODEL)),
                  pl.BlockSpec((TC * ROW_TILES, LANES), lambda i: (i, 0)),
                  pl.BlockSpec((TC * ROW_TILES, LANES), lambda i: (t_total // TC + i, 0))],
        out_specs=[pl.BlockSpec((TC, D_MODEL), lambda i: (jnp.minimum(i, n_c_prompt - 1), 0)),
                   pl.BlockSpec((TC // dec_seq, dec_seq, D_MODEL), lambda i: (jnp.maximum(i - n_c_prompt, 0), 0, 0))],
        out_shape=[jax.ShapeDtypeStruct((t_prompt, D_MODEL), f32),
                   jax.ShapeDtypeStruct((dec_batch, dec_seq, D_MODEL), f32)],
        compiler_params=cparams,
        name="moe_combine",
    )(x1, gcol, nfg, out_tok, out_tok)

    y_prompt = y_prompt.reshape(n_batch, seq, D_MODEL)
    pool_state_prompt = pstate[None, :, CARRY_ROWS - POOL_STATE:, :]
    pool_state_sample = jnp.concatenate([state_pool[0], p_rows_sample], axis=1)[None, :, -POOL_STATE:, :]
    chunk_v_sample = v_rows_sample[None]
    return (y_prompt, y_sample, pool_state_prompt, pool_state_sample, chunk_v_sample)
```

```python
import functools
import math

import jax
import jax.numpy as jnp
from jax import lax
from jax.experimental import pallas as pl
from jax.experimental.pallas import tpu as pltpu

D_MODEL = 1024
A_WIDTH = 512
B_WIDTH = 512
N_HEADS = 4
HEAD_DIM = 128
CHUNK = 128
POOL_WINDOWS = (2, 4, 8, 16)
POOL_STATE = 15
N_GROUPS = 4
EXPERTS_PER_GROUP = 8
N_EXPERTS = 32
TOP_K = 2
D_EXPERT = 512
EPS = 1e-6

SUBLANES = 8
LANES = 128
ROW_TILES = D_MODEL // LANES

TM = 256
TC = 1024
TE = 512
SEQ_BLK = 32
N_ROUTER_ROWS = 40
CARRY_ROWS = 16
DMA_UNROLL = 32
VMEM_LIMIT = 48 * 1024 * 1024
N_PARTS = 2
EXPERT_VMEM_LIMIT = 62 * 1024 * 1024

_INV_SQRT2 = 1.0 / math.sqrt(2.0)


def _rmsnorm(x, g):
    r = lax.rsqrt(jnp.mean(x * x, axis=-1, keepdims=True) + EPS)
    return (x * r) * g


def _gelu(x):
    return 0.5 * x * (1.0 + lax.erf(x * _INV_SQRT2))


def _layernorm(x, g, b):
    mu = jnp.mean(x, axis=-1, keepdims=True)
    xc = x - mu
    var = jnp.mean(xc * xc, axis=-1, keepdims=True)
    return (xc * lax.rsqrt(var + EPS)) * g + b


def _row_slab(ref, s, n):
    return ref[pl.ds(s, n, stride=ROW_TILES), :]


def _pool_project(pooled, g, bw_ref, bscale_ref):
    lo, hi = g * HEAD_DIM, (g + 1) * HEAD_DIM
    hb = jnp.dot(pooled.astype(jnp.bfloat16), bw_ref[g], preferred_element_type=jnp.float32)
    return hb * bscale_ref[:, lo:hi]


def _prompt_mixers(j, u, v, p, aws_ref, abias_ref, bw_ref, bscale_ref, mix_ref, pcarry_ref, pstate_ref):
    tri = (lax.broadcasted_iota(jnp.int32, (CHUNK, CHUNK), 0)
           >= lax.broadcasted_iota(jnp.int32, (CHUNK, CHUNK), 1))
    vb = v.astype(jnp.bfloat16)
    for hd in range(N_HEADS):
        lo, hi = hd * HEAD_DIM, (hd + 1) * HEAD_DIM
        w = jnp.where(tri, aws_ref[hd], 0.0).astype(jnp.bfloat16)
        for c in range(TM // CHUNK):
            r0, r1 = c * CHUNK, (c + 1) * CHUNK
            z = jnp.dot(w, vb[r0:r1, lo:hi], preferred_element_type=jnp.float32) + abias_ref[:, lo:hi]
            mix_ref[r0:r1, lo:hi] = (u[r0:r1, lo:hi] * z).astype(jnp.bfloat16)

    head_pos = j * TM + lax.broadcasted_iota(jnp.int32, (CARRY_ROWS, LANES), 0)
    for g, w in enumerate(POOL_WINDOWS):
        lo, hi = g * HEAD_DIM, (g + 1) * HEAD_DIM
        pg = p[:, lo:hi]
        acc = jnp.concatenate([pcarry_ref[:, lo:hi], pg], axis=0)
        shift = 1
        while shift < w:
            acc = acc + pltpu.roll(acc, shift, 0)
            shift *= 2
        head = acc[CARRY_ROWS:2 * CARRY_ROWS, :] / jnp.minimum(head_pos + 1, w).astype(jnp.float32)
        mean = jnp.concatenate([head, acc[2 * CARRY_ROWS:, :] * (1.0 / w)], axis=0)
        pooled = mean - pg
        mix_ref[:, A_WIDTH + lo:A_WIDTH + hi] = _pool_project(pooled, g, bw_ref, bscale_ref).astype(jnp.bfloat16)
    tail = p[TM - CARRY_ROWS:, :]
    pcarry_ref[...] = tail
    pstate_ref[0] = tail


def _position_major(ref):
    n_pos = TM // SEQ_BLK
    rows = ref.reshape(SEQ_BLK, n_pos, ref.shape[-1])
    return jnp.concatenate([rows[:, q, :] for q in range(n_pos)], axis=0)


def _sample_mixers(u, v, p, aws_ref, abs_ref, st_ref, bw_ref, bscale_ref, mix_ref):
    n_pos = TM // SEQ_BLK

    for hd in range(N_HEADS):
        lo, hi = hd * HEAD_DIM, (hd + 1) * HEAD_DIM
        vplanes = [v[s * SEQ_BLK:(s + 1) * SEQ_BLK, lo:hi] for s in range(n_pos)]
        for i in range(n_pos):
            z = vplanes[0] * aws_ref[(hd * n_pos + i) * n_pos]
            for s in range(1, i + 1):
                z = z + vplanes[s] * aws_ref[(hd * n_pos + i) * n_pos + s]
            z = z + abs_ref[hd * n_pos + i]
            r0, r1 = i * SEQ_BLK, (i + 1) * SEQ_BLK
            mix_ref[r0:r1, lo:hi] = (u[r0:r1, lo:hi] * z).astype(jnp.bfloat16)

    for g, w in enumerate(POOL_WINDOWS):
        lo, hi = g * HEAD_DIM, (g + 1) * HEAD_DIM
        planes = [st_ref[0, k, :, lo:hi] for k in range(POOL_STATE)]
        planes += [p[i * SEQ_BLK:(i + 1) * SEQ_BLK, lo:hi] for i in range(n_pos)]
        pooled = []
        for i in range(n_pos):
            top = POOL_STATE + i
            s = planes[top - w + 1]
            for k in range(top - w + 2, top + 1):
                s = s + planes[k]
            pooled.append(s * (1.0 / w) - planes[top])
        pooled = jnp.concatenate(pooled, axis=0)
        mix_ref[:, A_WIDTH + lo:A_WIDTH + hi] = _pool_project(pooled, g, bw_ref, bscale_ref).astype(jnp.bfloat16)


def _route(h2, rwt_ref, rb_ref, su_ref, cnt_ref, meta_ref, gcol_ref):
    tm = h2.shape[0]
    h_hi = h2.astype(jnp.bfloat16)
    h_lo = (h2 - h_hi.astype(jnp.float32)).astype(jnp.bfloat16)
    s = (jnp.dot(h_hi, rwt_ref[...], preferred_element_type=jnp.float32)
         + jnp.dot(h_lo, rwt_ref[...], preferred_element_type=jnp.float32))
    st = s.T
    lt = st[0:N_ROUTER_ROWS, :] + st[N_ROUTER_ROWS:2 * N_ROUTER_ROWS, :] + rb_ref[...]
    row = lambda i: lt[i:i + 1, :]
    l1 = [row(i) for i in range(N_GROUPS)]
    m1 = jnp.maximum(jnp.maximum(l1[0], l1[1]), jnp.maximum(l1[2], l1[3]))
    grp = jnp.where(l1[0] == m1, 0, jnp.where(l1[1] == m1, 1, jnp.where(l1[2] == m1, 2, 3)))
    se = (jnp.exp(l1[0] - m1) + jnp.exp(l1[1] - m1)) + (jnp.exp(l1[2] - m1) + jnp.exp(l1[3] - m1))
    pg = 1.0 / se
    l2 = []
    for e in range(EXPERTS_PER_GROUP):
        c = [row(N_GROUPS + g * EXPERTS_PER_GROUP + e) for g in range(N_GROUPS)]
        l2.append(jnp.where(grp == 0, c[0], jnp.where(grp == 1, c[1], jnp.where(grp == 2, c[2], c[3]))))
    v0 = functools.reduce(jnp.maximum, l2)
    i0 = jnp.full_like(grp, EXPERTS_PER_GROUP - 1)
    for e in range(EXPERTS_PER_GROUP - 2, -1, -1):
        i0 = jnp.where(l2[e] == v0, e, i0)
    neg = jnp.float32(-jnp.inf)
    l2m = [jnp.where(i0 == e, neg, l2[e]) for e in range(EXPERTS_PER_GROUP)]
    v1 = functools.reduce(jnp.maximum, l2m)
    i1 = jnp.full_like(grp, EXPERTS_PER_GROUP - 1)
    for e in range(EXPERTS_PER_GROUP - 2, -1, -1):
        i1 = jnp.where((l2m[e] == v1) & (i0 != e), e, i1)
    d = jnp.exp(v1 - v0)
    g0 = pg / (1.0 + d)
    g1 = (pg * d) / (1.0 + d)
    e0 = grp * EXPERTS_PER_GROUP + i0
    e1 = grp * EXPERTS_PER_GROUP + i1

    eiota = lax.broadcasted_iota(jnp.int32, (N_EXPERTS, tm), 0)
    hit0 = eiota == e0
    hit1 = eiota == e1
    onehot = (hit0 | hit1).astype(jnp.bfloat16)
    prefix = jnp.dot(onehot, su_ref[...], preferred_element_type=jnp.float32)
    carry = cnt_ref[...]
    base = prefix + jnp.concatenate([carry] * (tm // LANES), axis=1)
    r0 = jnp.sum(jnp.where(hit0, base, 0.0), axis=0, keepdims=True)
    r1 = jnp.sum(jnp.where(hit1, base, 0.0), axis=0, keepdims=True)
    ones = jnp.ones((tm, LANES), jnp.bfloat16)
    cnt_ref[...] = carry + jnp.dot(onehot, ones, preferred_element_type=jnp.float32)

    meta_ref[0:1, :] = e0
    meta_ref[1:2, :] = e1
    meta_ref[2:3, :] = r0.astype(jnp.int32)
    meta_ref[3:4, :] = r1.astype(jnp.int32)
    meta_ref[4:8, :] = jnp.zeros((4, tm), jnp.int32)

    giota = lax.broadcasted_iota(jnp.int32, (LANES, tm), 0)
    gpad = jnp.where(giota == 0, g0, jnp.where(giota == 1, g1, 0.0))
    gcol_ref[...] = gpad.T


def _mixer_kernel(aws_s_ref, abs_s_ref,
                  x0_ref, xpn_ref, xsn_ref,
                  st_ref, n1g_ref, win_ref, ang_ref, anb_ref, aws_ref, abias_ref, bw_ref, bscale_ref,
                  wout_ref, n2g_ref, rwt_ref, rb_ref, su_ref,
                  x1_ref, h2_ref, meta_ref, gcol_ref, counts_ref, pstate_ref, pp_ref, vp_ref, asg_ref,
                  mix_ref, pcarry_ref, cnt_ref, x_ref, u_ref, v_ref, p_ref, part_meta, dest_vmem, fill_vmem, *dest_smem,
                  n_prompt_tiles, seq_tiles, part_tiles, t_total):
    i = pl.program_id(0)
    is_prompt = i < n_prompt_tiles
    j = i % seq_tiles
    part = i // part_tiles
    tile_in_part = i % part_tiles

    @pl.when((i > 0) & (tile_in_part == 0))
    def _():
        _, _, pad_start = _segment_offsets(cnt_ref[...])
        _stage_inversion(part_meta, pad_start, dest_vmem, dest_smem, fill_vmem, asg_ref)

    def in_proj(x):
        h = _rmsnorm(x, n1g_ref[...]).astype(jnp.bfloat16)
        proj = jnp.dot(h, win_ref[...], preferred_element_type=jnp.float32)
        uv = _gelu(proj[:, : 2 * A_WIDTH])
        u_ref[...] = uv[:, :A_WIDTH]
        v_ref[...] = _layernorm(uv[:, A_WIDTH:], ang_ref[...], anb_ref[...])
        p_ref[...] = proj[:, 2 * A_WIDTH:]
        x_ref[...] = x

    @pl.when(i == 0)
    def _():
        in_proj(x0_ref[0])

    @pl.when(i % part_tiles == 0)
    def _():
        cnt_ref[...] = jnp.zeros_like(cnt_ref)

    @pl.when(is_prompt & (j == 0))
    def _():
        pcarry_ref[...] = jnp.zeros_like(pcarry_ref)

    def finish_tile(x, invert_previous_part):
        x1 = x + jnp.dot(mix_ref[...], wout_ref[...], preferred_element_type=jnp.float32)
        x1_ref[...] = x1
        h2 = _rmsnorm(x1, n2g_ref[...])
        for s in range(ROW_TILES):
            h2_ref[pl.ds(s, TM, stride=ROW_TILES), :] = h2[:, s * LANES:(s + 1) * LANES]
        _route(h2, rwt_ref, rb_ref, su_ref, cnt_ref, meta_ref, gcol_ref)
        counts_ref[0] = cnt_ref[...]
        part_meta[:, pl.ds(pl.multiple_of(tile_in_part * TM, TM), TM)] = meta_ref[0:2 * TOP_K, :]
        if invert_previous_part:
            first_tok = tile_in_part * TM
            for k, dsm in enumerate(dest_smem):
                first_asg = k * t_total + (part - 1) * (part_tiles * TM) + first_tok
                for t in range(TM):
                    asg_ref[0, 0, dsm[0, first_tok + t]] = first_asg + t
        nxt = jnp.minimum(i + 1, pl.num_programs(0) - 1)
        in_proj(jnp.where(nxt < n_prompt_tiles, xpn_ref[0], xsn_ref[...].reshape(TM, D_MODEL)))

    @pl.when(is_prompt & (part == 0))
    def _():
        _prompt_mixers(j, u_ref[...], v_ref[...], p_ref[...], aws_ref, abias_ref, bw_ref, bscale_ref, mix_ref,
                       pcarry_ref, pstate_ref)
        finish_tile(x_ref[...], False)

    @pl.when(is_prompt & (part > 0))
    def _():
        _prompt_mixers(j, u_ref[...], v_ref[...], p_ref[...], aws_ref, abias_ref, bw_ref, bscale_ref, mix_ref,
                       pcarry_ref, pstate_ref)
        finish_tile(x_ref[...], True)

    @pl.when(jnp.logical_not(is_prompt))
    def _():
        vp_ref[...] = v_ref[...].reshape(vp_ref.shape)
        pp_ref[...] = p_ref[...].reshape(pp_ref.shape)
        _sample_mixers(_position_major(u_ref), _position_major(v_ref), _position_major(p_ref), aws_s_ref, abs_s_ref,
                       st_ref, bw_ref, bscale_ref, mix_ref)
        finish_tile(_position_major(x_ref), True)


def _segment_offsets(cnt):
    padded = jnp.floor((cnt + (TE - 1)) * (1.0 / TE)) * TE
    sub = lax.broadcasted_iota(jnp.int32, cnt.shape, 0)
    pad_end = padded
    shift = 1
    while shift < N_EXPERTS:
        pad_end = pad_end + jnp.where(sub >= shift, pltpu.roll(pad_end, shift, 0), 0.0)
        shift *= 2
    return padded, pad_end, pad_end - padded


def _dest_rows(meta_ref, pad_start, dest_ref):
    t_part = meta_ref.shape[1]
    base = pad_start.astype(jnp.int32)
    for k in range(TOP_K):
        e = meta_ref[k:k + 1, :]
        d = meta_ref[TOP_K + k:TOP_K + k + 1, :]
        for ex in range(N_EXPERTS):
            row = jnp.concatenate([base[ex:ex + 1, :]] * (t_part // LANES), axis=1)
            d = d + jnp.where(e == ex, row, 0)
        dest_ref[k] = d


def _stage_inversion(meta_ref, pad_start, dest_vmem, dest_smem, fill_vmem, asg_ref):
    _dest_rows(meta_ref, pad_start, dest_vmem)
    for k, dsm in enumerate(dest_smem):
        pltpu.sync_copy(dest_vmem.at[k], dsm)
    fill_vmem[...] = jnp.full(fill_vmem.shape, -1, jnp.int32)
    pltpu.sync_copy(fill_vmem, asg_ref)


def _tables_kernel(counts_ref, meta_ref, dest_ref, tab_ref):
    h = pl.program_id(0)
    cnt = counts_ref[0]
    sub = lax.broadcasted_iota(jnp.int32, cnt.shape, 0)
    padded, pad_end, pad_start = _segment_offsets(cnt)

    @pl.when(h == pl.num_programs(0) - 1)
    def _():
        _dest_rows(meta_ref, pad_start, dest_ref)

    n_valid = jnp.maximum(pad_end[N_EXPERTS - 1:, :] * (1.0 / TE), 1.0)
    tile = jnp.minimum(lax.broadcasted_iota(jnp.int32, (1, LANES), 1).astype(jnp.float32), n_valid - 1.0)
    tile_start = tile * TE
    tile_e = jnp.minimum(jnp.sum((pad_end <= tile_start).astype(jnp.float32), axis=0, keepdims=True),
                         N_EXPERTS - 1.0)
    is_e = sub.astype(jnp.float32) == tile_e
    seg_start = jnp.sum(jnp.where(is_e, pad_start, 0.0), axis=0, keepdims=True)
    seg_cnt = jnp.sum(jnp.where(is_e, cnt, 0.0), axis=0, keepdims=True)
    rows = jnp.clip(seg_cnt - (tile_start - seg_start), 0.0, TE)
    tab_ref[0, 0:1, :] = tile_e.astype(jnp.int32)
    tab_ref[0, 1:2, :] = rows.astype(jnp.int32)
    tab_ref[0, 2:3, :] = n_valid.astype(jnp.int32)
    tab_ref[0, 3:, :] = jnp.zeros((SUBLANES - 3, LANES), jnp.int32)


def _expert_kernel(te_ref, nv_ref, nrows_ref,
                   src_ref, src_next_ref, dst_ref,
                   dest_k0_ref, dest_k1_ref,
                   h2_hbm, w1_ref, w3_ref, w2_ref,
                   out_hbm,
                   xt0, xt1, ob0, ob1, fill_vmem, asg_last, ssem, h2_ref, hsem, *, t_total, t_part):
    h = pl.program_id(0)
    i = pl.program_id(1)
    g = h * pl.num_programs(1) + i
    nv = nv_ref[h]
    slab = lambda r: pl.ds(pl.multiple_of(r * ROW_TILES, ROW_TILES), ROW_TILES)
    toks_per_tile = TE // TOP_K
    inv_tiles = t_part // toks_per_tile
    part_slabs = t_part * ROW_TILES
    fetch_part = lambda part: pltpu.make_async_copy(h2_hbm.at[pl.ds(part * part_slabs, part_slabs), :], h2_ref, hsem)

    def last_part_src(tile):
        def src(r):
            a = asg_last[0, tile * TE + r]
            return a - jnp.where(a >= t_total, t_total + t_part, t_part)
        return src

    first_tables = (lambda r: src_ref[0, 0, r], lambda r: src_next_ref[0, 0, r],
                    lambda first, uu: dst_ref[0, 0, first + uu])
    last_tables = (last_part_src(i), last_part_src(jnp.minimum(i + 1, nv - 1)),
                   lambda first, uu: asg_last[0, (i * TE + first) + uu])

    def gather(src, xt):
        for r in range(TE):
            xt[r * ROW_TILES:(r + 1) * ROW_TILES, :] = h2_ref[slab(src(r)), :]

    def invert_share():
        first_tok = jnp.minimum(i, inv_tiles - 1) * toks_per_tile
        for k, dest_ref in enumerate((dest_k0_ref, dest_k1_ref)):
            first_asg = k * t_total + t_part + first_tok
            for t in range(toks_per_tile):
                asg_last[0, dest_ref[0, 0, t]] = first_asg + t

    def wait_scatter(tile, ob, other, sem):
        n = nrows_ref[tile] * ROW_TILES
        pltpu.make_async_copy(other.at[pl.ds(0, n), :], ob.at[pl.ds(0, n), :], sem).wait()

    def tile_body(xt, xt_next, ob, ob_other, sem, sem_other, tables, first_part):
        _, src_next, dst = tables

        @pl.when(i >= 2)
        def _():
            wait_scatter(g - 2, ob, ob_other, sem)

        def compute(m):
            gather(src_next, xt_next)
            if first_part:
                invert_share()
            x = jnp.concatenate([_row_slab(xt, s, m) for s in range(ROW_TILES)], axis=1).astype(jnp.bfloat16)
            a = jnp.dot(x, w1_ref[0].astype(jnp.bfloat16), preferred_element_type=jnp.float32)
            b = jnp.dot(x, w3_ref[0].astype(jnp.bfloat16), preferred_element_type=jnp.float32)
            hmid = (a * jax.nn.sigmoid(a)) * b
            o = jnp.dot(hmid.astype(jnp.bfloat16), w2_ref[0].astype(jnp.bfloat16),
                        preferred_element_type=jnp.float32)
            for s in range(ROW_TILES):
                ob[pl.ds(s, m, stride=ROW_TILES), :] = o[:, s * LANES:(s + 1) * LANES]

        n = nrows_ref[g]

        @pl.when(n > TE // 2)
        def _():
            compute(TE)

        @pl.when(n <= TE // 2)
        def _():
            compute(TE // 2)

        n_full = n // DMA_UNROLL
        copy = lambda first, uu: pltpu.make_async_copy(ob.at[slab(first + uu), :], out_hbm.at[dst(first, uu)], sem)

        def body(rb, carry):
            for uu in range(DMA_UNROLL):
                copy(rb * DMA_UNROLL, uu).start(priority=uu % 2)
            return carry
        lax.fori_loop(0, n_full, body, 0)

        def tail(r, carry):
            copy(r, 0).start(priority=1)
            return carry
        lax.fori_loop(n_full * DMA_UNROLL, n, tail, 0)

        @pl.when(i == nv - 1)
        def _():
            @pl.when(i >= 1)
            def _():
                wait_scatter(g - 1, ob_other, ob, sem_other)
            wait_scatter(g, ob, ob_other, sem)
            if first_part:
                fetch_part(1).start()

    def part_body(tables, first_part):
        @pl.when(i == 0)
        def _():
            if not first_part:
                fetch_part(1).wait()
            gather(tables[0], xt0)

        @pl.when(i % 2 == 0)
        def _():
            tile_body(xt0, xt1, ob0, ob1, ssem.at[0], ssem.at[1], tables, first_part)

        @pl.when(i % 2 == 1)
        def _():
            tile_body(xt1, xt0, ob1, ob0, ssem.at[1], ssem.at[0], tables, first_part)

    @pl.when((h == 0) & (i == 0))
    def _():
        fetch_part(0).start()
        fill_vmem[...] = jnp.full(fill_vmem.shape, t_part, jnp.int32)
        pltpu.sync_copy(fill_vmem, asg_last)
        fetch_part(0).wait()

    @pl.when((i < nv) & (h == 0))
    def _():
        part_body(first_tables, True)

    @pl.when((i < nv) & (h > 0))
    def _():
        part_body(last_tables, False)


def _combine_kernel(x1_ref, gcol_ref, nfg_ref, o0_ref, o1_ref, yp_ref, ys_ref, *, n_prompt_tiles):
    i = pl.program_id(0)
    o0 = jnp.concatenate([_row_slab(o0_ref, s, TC) for s in range(ROW_TILES)], axis=1)
    o1 = jnp.concatenate([_row_slab(o1_ref, s, TC) for s in range(ROW_TILES)], axis=1)
    g = gcol_ref[...]
    moe = g[:, 0:1] * o0 + g[:, 1:2] * o1
    y = _rmsnorm(x1_ref[...] + moe, nfg_ref[...])

    @pl.when(i < n_prompt_tiles)
    def _():
        yp_ref[...] = y

    @pl.when(i >= n_prompt_tiles)
    def _():
        for blk in range(TC // TM):
            for q in range(TM // SEQ_BLK):
                r0 = blk * TM + q * SEQ_BLK
                ys_ref[blk * SEQ_BLK:(blk + 1) * SEQ_BLK, q, :] = y[r0:r0 + SEQ_BLK, :]


def _const_spec(shape):
    return pl.BlockSpec(shape, lambda *_: (0,) * len(shape))


def kernel(x_prompt, x_sample, state_pool, norm1_g, w_in, a_norm_g, a_norm_b, a_ws, a_bs, b_w, b_scale, w_out,
           norm2_g, r1_w, r1_b, r2_w, r2_b, exp_w1, exp_w3, exp_w2, normf_g):
    f32, bf16, i32 = jnp.float32, jnp.bfloat16, jnp.int32
    n_batch, seq, _ = x_prompt.shape
    dec_batch, dec_seq, _ = x_sample.shape
    assert norm1_g.shape[0] == 1 and seq % TM == 0 and TM % CHUNK == 0
    assert dec_seq * SEQ_BLK == TM and dec_batch % SEQ_BLK == 0 and dec_seq <= CHUNK
    t_prompt = n_batch * seq
    t_sample = dec_batch * dec_seq
    t_total = t_prompt + t_sample
    n_tok_tiles = t_total // TM
    n_prompt_tiles = t_prompt // TM
    n_sample_tiles = t_sample // TM
    seq_tiles = seq // TM
    plane_rows = t_total * ROW_TILES
    assert n_tok_tiles % N_PARTS == 0
    part_tiles = n_tok_tiles // N_PARTS
    t_part = part_tiles * TM
    n_exp_tiles = -(-(TOP_K * t_part + N_EXPERTS * (TE - 1)) // TE)
    p_rows = n_exp_tiles * TE

    n1g = norm1_g[0][None, :]
    n2g = norm2_g[0][None, :]
    nfg = normf_g[None, :]
    win = w_in[0].astype(bf16)
    wout = w_out[0].astype(bf16)
    ang = a_norm_g[0][None, :]
    anb = a_norm_b[0][None, :]
    bw = b_w[0].astype(bf16)
    bscale = b_scale[0][None, :]
    abias = jnp.repeat(a_bs[0][:, :CHUNK].T, HEAD_DIM, axis=1)
    rw = jnp.concatenate([r1_w[0], r2_w[0].transpose(1, 0, 2).reshape(D_MODEL, N_EXPERTS),
                          jnp.zeros((D_MODEL, N_ROUTER_ROWS - N_GROUPS - N_EXPERTS), f32)], axis=1)
    rw_hi = rw.astype(bf16)
    rw_lo = (rw - rw_hi.astype(f32)).astype(bf16)
    rwt = jnp.concatenate([rw_hi, rw_lo, jnp.zeros((D_MODEL, LANES - 2 * N_ROUTER_ROWS), bf16)], axis=1)
    rbias = jnp.concatenate([r1_b[0], r2_b[0].reshape(-1),
                             jnp.zeros((N_ROUTER_ROWS - N_GROUPS - N_EXPERTS,), f32)])
    rb = jnp.broadcast_to(rbias[:, None], (N_ROUTER_ROWS, TM))
    su = (jnp.arange(TM)[:, None] < jnp.arange(TM)[None, :]).astype(bf16)
    st_planes = state_pool[0].reshape(n_sample_tiles, SEQ_BLK, POOL_STATE, B_WIDTH).transpose(0, 2, 1, 3)
    aws_s = a_ws[0][:, :dec_seq, :dec_seq].reshape(-1)
    abs_s = a_bs[0][:, :dec_seq].reshape(-1)

    cparams = pltpu.CompilerParams(dimension_semantics=("arbitrary",), vmem_limit_bytes=VMEM_LIMIT)

    p_tile = lambda i: jnp.minimum(i, n_prompt_tiles - 1)
    s_tile = lambda i: jnp.maximum(i - n_prompt_tiles, 0)
    sample_rows_shape = jax.ShapeDtypeStruct((dec_batch, dec_seq, A_WIDTH), f32)
    assert N_PARTS >= 2 and n_prompt_tiles >= part_tiles and t_part % TM == 0
    x1, h2, meta, gcol, counts, pstate, p_rows_sample, v_rows_sample, asg_head = pl.pallas_call(
        functools.partial(_mixer_kernel, n_prompt_tiles=n_prompt_tiles, seq_tiles=seq_tiles, part_tiles=part_tiles,
                          t_total=t_total),
        grid_spec=pltpu.PrefetchScalarGridSpec(
            num_scalar_prefetch=2,
            grid=(n_tok_tiles,),
            in_specs=[
                pl.BlockSpec((1, TM, D_MODEL), lambda i, *_: (0, 0, 0)),
                pl.BlockSpec((1, TM, D_MODEL),
                             lambda i, *_: (p_tile(i + 1) // seq_tiles, p_tile(i + 1) % seq_tiles, 0)),
                pl.BlockSpec((SEQ_BLK, dec_seq, D_MODEL),
                             lambda i, *_: (jnp.minimum(s_tile(i + 1), n_sample_tiles - 1), 0, 0)),
                pl.BlockSpec((1, POOL_STATE, SEQ_BLK, B_WIDTH), lambda i, *_: (s_tile(i), 0, 0, 0)),
                _const_spec((1, D_MODEL)), _const_spec((D_MODEL, 3 * A_WIDTH)), _const_spec((1, A_WIDTH)),
                _const_spec((1, A_WIDTH)), _const_spec((N_HEADS, CHUNK, CHUNK)), _const_spec((CHUNK, A_WIDTH)),
                _const_spec((N_GROUPS, HEAD_DIM, HEAD_DIM)), _const_spec((1, B_WIDTH)),
                _const_spec((D_MODEL, D_MODEL)), _const_spec((1, D_MODEL)), _const_spec((D_MODEL, LANES)),
                _const_spec((N_ROUTER_ROWS, TM)), _const_spec((TM, TM)),
            ],
            out_specs=[
                pl.BlockSpec((TM, D_MODEL), lambda i, *_: (i, 0)),
                pl.BlockSpec((TM * ROW_TILES, LANES), lambda i, *_: (i, 0)),
                pl.BlockSpec((SUBLANES, TM), lambda i, *_: (0, i)),
                pl.BlockSpec((TM, LANES), lambda i, *_: (i, 0)),
                pl.BlockSpec((1, N_EXPERTS, LANES), lambda i, *_: (i // part_tiles, 0, 0)),
                pl.BlockSpec((1, CARRY_ROWS, B_WIDTH), lambda i, *_: (p_tile(i) // seq_tiles, 0, 0)),
                pl.BlockSpec((SEQ_BLK, dec_seq, A_WIDTH), lambda i, *_: (s_tile(i), 0, 0)),
                pl.BlockSpec((SEQ_BLK, dec_seq, A_WIDTH), lambda i, *_: (s_tile(i), 0, 0)),
                pl.BlockSpec((1, 1, p_rows), lambda i, *_: (jnp.maximum(i // part_tiles - 1, 0), 0, 0),
                             memory_space=pltpu.SMEM),
            ],
            scratch_shapes=[pltpu.VMEM((TM, D_MODEL), bf16), pltpu.VMEM((CARRY_ROWS, B_WIDTH), f32),
                            pltpu.VMEM((N_EXPERTS, LANES), f32), pltpu.VMEM((TM, D_MODEL), f32),
                            pltpu.VMEM((TM, A_WIDTH), f32), pltpu.VMEM((TM, A_WIDTH), f32),
                            pltpu.VMEM((TM, B_WIDTH), f32),
                            pltpu.VMEM((2 * TOP_K, t_part), i32), pltpu.VMEM((TOP_K, 1, t_part), i32),
                            pltpu.VMEM((1, 1, p_rows), i32)] + [pltpu.SMEM((1, t_part), i32)] * TOP_K,
        ),
        out_shape=[
            jax.ShapeDtypeStruct((t_total, D_MODEL), f32),
            jax.ShapeDtypeStruct((plane_rows, LANES), f32),
            jax.ShapeDtypeStruct((SUBLANES, t_total), i32),
            jax.ShapeDtypeStruct((t_total, LANES), f32),
            jax.ShapeDtypeStruct((N_PARTS, N_EXPERTS, LANES), f32),
            jax.ShapeDtypeStruct((n_batch, CARRY_ROWS, B_WIDTH), f32),
            sample_rows_shape, sample_rows_shape,
            jax.ShapeDtypeStruct((N_PARTS - 1, 1, p_rows), i32),
        ],
        compiler_params=cparams,
        name="mixer",
    )(aws_s, abs_s, x_prompt, x_prompt, x_sample, st_planes, n1g, win, ang, anb, a_ws[0][:, :CHUNK, :CHUNK], abias,
      bw, bscale, wout, n2g, rwt, rb, su)

    assert n_exp_tiles <= LANES and t_part % LANES == 0
    dest_last, tab = pl.pallas_call(
        _tables_kernel,
        grid=(N_PARTS,),
        in_specs=[pl.BlockSpec((1, N_EXPERTS, LANES), lambda h: (h, 0, 0)),
                  pl.BlockSpec((SUBLANES, t_part), lambda h: (0, h))],
        out_specs=[_const_spec((TOP_K, 1, t_part)),
                   pl.BlockSpec((1, SUBLANES, LANES), lambda h: (h, 0, 0))],
        out_shape=[jax.ShapeDtypeStruct((TOP_K, 1, t_part), i32),
                   jax.ShapeDtypeStruct((N_PARTS, SUBLANES, LANES), i32)],
        compiler_params=cparams,
        name="route_tables",
    )(counts, meta)
    tile_e = tab[:, 0, :n_exp_tiles].reshape(-1)
    tile_rows = tab[:, 1, :n_exp_tiles].reshape(-1)
    n_valid = tab[:, 2, 0]
    assert N_PARTS == 2 and (TOP_K * t_part) % TE == 0
    asg = asg_head.reshape(-1)
    row_tok = jnp.where(asg >= t_total, asg - t_total, asg)
    row_src = jnp.where(asg >= 0, row_tok, 0).reshape(n_exp_tiles, 1, TE)
    row_dst = jnp.maximum(asg, 0).reshape(n_exp_tiles, 1, TE)

    flat = lambda h, i: h * n_exp_tiles + i
    cur_blk = lambda h, i, te, nv, nr: (jnp.minimum(i, nv[0] - 1), 0, 0)
    nxt_blk = lambda h, i, te, nv, nr: (jnp.minimum(i + 1, nv[0] - 1), 0, 0)
    w_blk = lambda h, i, te, nv, nr: (te[flat(h, i)], 0, 0)
    smem_blk = lambda imap: pl.BlockSpec((1, 1, TE), imap, memory_space=pltpu.SMEM)
    last_share = t_part // (TE // TOP_K) - 1
    share_blk = lambda k: pl.BlockSpec(
        (1, 1, TE // TOP_K), lambda h, i, *_: (k, 0, jnp.where(h == 0, jnp.minimum(i, last_share), last_share)),
        memory_space=pltpu.SMEM)
    row_buf = pltpu.VMEM((TE * ROW_TILES, LANES), f32)
    out_tok = pl.pallas_call(
        functools.partial(_expert_kernel, t_total=t_total, t_part=t_part),
        grid_spec=pltpu.PrefetchScalarGridSpec(
            num_scalar_prefetch=3,
            grid=(N_PARTS, n_exp_tiles),
            in_specs=[smem_blk(cur_blk), smem_blk(nxt_blk), smem_blk(cur_blk),
                      share_blk(0), share_blk(1),
                      pl.BlockSpec(memory_space=pl.ANY),
                      pl.BlockSpec((1, D_MODEL, D_EXPERT), w_blk),
                      pl.BlockSpec((1, D_MODEL, D_EXPERT), w_blk),
                      pl.BlockSpec((1, D_EXPERT, D_MODEL), w_blk)],
            out_specs=pl.BlockSpec(memory_space=pl.ANY),
            scratch_shapes=[row_buf, row_buf, row_buf, row_buf, pltpu.VMEM((1, p_rows), i32),
                            pltpu.SMEM((1, p_rows), i32), pltpu.SemaphoreType.DMA((2,)),
                            pltpu.VMEM((t_part * ROW_TILES, LANES), f32), pltpu.SemaphoreType.DMA(())],
        ),
        out_shape=jax.ShapeDtypeStruct((TOP_K * t_total, ROW_TILES, LANES), f32),
        compiler_params=pltpu.CompilerParams(dimension_semantics=("arbitrary", "arbitrary"),
                                             vmem_limit_bytes=EXPERT_VMEM_LIMIT),
        name="moe_experts",
    )(tile_e, n_valid, tile_rows, row_src, row_src, row_dst, dest_last, dest_last, h2, exp_w1[0], exp_w3[0], exp_w2[0])
    out_tok = out_tok.reshape(TOP_K * plane_rows, LANES)

    assert t_prompt % TC == 0 and t_sample % TC == 0
    n_c_prompt = t_prompt // TC
    y_prompt, y_sample = pl.pallas_call(
        functools.partial(_combine_kernel, n_prompt_tiles=n_c_prompt),
        grid=(t_total // TC,),
        in_specs=[pl.BlockSpec((TC, D_MODEL), lambda i: (i, 0)),
                  pl.BlockSpec((TC, LANES), lambda i: (i, 0)),
                  _const_spec((1, D_MODEL)),
                  pl.BlockSpec((TC * ROW_TILES, LANES), lambda i: (i, 0)),
                  pl.BlockSpec((TC * ROW_TILES, LANES), lambda i: (t_total // TC + i, 0))],
        out_specs=[pl.BlockSpec((TC, D_MODEL), lambda i: (jnp.minimum(i, n_c_prompt - 1), 0)),
                   pl.BlockSpec((TC // dec_seq, dec_seq, D_MODEL), lambda i: (jnp.maximum(i - n_c_prompt, 0), 0, 0))],
        out_shape=[jax.ShapeDtypeStruct((t_prompt, D_MODEL), f32),
                   jax.ShapeDtypeStruct((dec_batch, dec_seq, D_MODEL), f32)],
        compiler_params=cparams,
        name="moe_combine",
    )(x1, gcol, nfg, out_tok, out_tok)

    y_prompt = y_prompt.reshape(n_batch, seq, D_MODEL)
    pool_state_prompt = pstate[None, :, CARRY_ROWS - POOL_STATE:, :]
    pool_state_sample = jnp.concatenate([state_pool[0], p_rows_sample], axis=1)[None, :, -POOL_STATE:, :]
    chunk_v_sample = v_rows_sample[None]
    return (y_prompt, y_sample, pool_state_prompt, pool_state_sample, chunk_v_sample)
```

```python
import functools
import math

import jax
import jax.numpy as jnp
from jax import lax
from jax.experimental import pallas as pl
from jax.experimental.pallas import tpu as pltpu

D_MODEL = 1024
A_WIDTH = 512
B_WIDTH = 512
N_HEADS = 4
HEAD_DIM = 128
CHUNK = 128
POOL_WINDOWS = (2, 4, 8, 16)
POOL_STATE = 15
N_GROUPS = 4
EXPERTS_PER_GROUP = 8
N_EXPERTS = 32
TOP_K = 2
D_EXPERT = 512
EPS = 1e-6

SUBLANES = 8
LANES = 128
ROW_TILES = D_MODEL // LANES

TM = 256
TC = 1024
TE = 512
SEQ_BLK = 32
N_ROUTER_ROWS = 40
CARRY_ROWS = 16
DMA_UNROLL = 32
VMEM_LIMIT = 48 * 1024 * 1024
N_PARTS = 2
EXPERT_VMEM_LIMIT = 62 * 1024 * 1024

_INV_SQRT2 = 1.0 / math.sqrt(2.0)


def _rmsnorm(x, g):
    r = lax.rsqrt(jnp.mean(x * x, axis=-1, keepdims=True) + EPS)
    return (x * r) * g


def _gelu(x):
    return 0.5 * x * (1.0 + lax.erf(x * _INV_SQRT2))


def _layernorm(x, g, b):
    mu = jnp.mean(x, axis=-1, keepdims=True)
    xc = x - mu
    var = jnp.mean(xc * xc, axis=-1, keepdims=True)
    return (xc * lax.rsqrt(var + EPS)) * g + b


def _row_slab(ref, s, n):
    return ref[pl.ds(s, n, stride=ROW_TILES), :]


def _pool_project(pooled, g, bw_ref, bscale_ref):
    lo, hi = g * HEAD_DIM, (g + 1) * HEAD_DIM
    hb = jnp.dot(pooled.astype(jnp.bfloat16), bw_ref[g], preferred_element_type=jnp.float32)
    return hb * bscale_ref[:, lo:hi]


def _prompt_mixers(j, u, v, p, aws_ref, abias_ref, bw_ref, bscale_ref, mix_ref, pcarry_ref, pstate_ref):
    tri = (lax.broadcasted_iota(jnp.int32, (CHUNK, CHUNK), 0)
           >= lax.broadcasted_iota(jnp.int32, (CHUNK, CHUNK), 1))
    vb = v.astype(jnp.bfloat16)
    for hd in range(N_HEADS):
        lo, hi = hd * HEAD_DIM, (hd + 1) * HEAD_DIM
        w = jnp.where(tri, aws_ref[hd], 0.0).astype(jnp.bfloat16)
        for c in range(TM // CHUNK):
            r0, r1 = c * CHUNK, (c + 1) * CHUNK
            z = jnp.dot(w, vb[r0:r1, lo:hi], preferred_element_type=jnp.float32) + abias_ref[:, lo:hi]
            mix_ref[r0:r1, lo:hi] = (u[r0:r1, lo:hi] * z).astype(jnp.bfloat16)

    head_pos = j * TM + lax.broadcasted_iota(jnp.int32, (CARRY_ROWS, LANES), 0)
    for g, w in enumerate(POOL_WINDOWS):
        lo, hi = g * HEAD_DIM, (g + 1) * HEAD_DIM
        pg = p[:, lo:hi]
        acc = jnp.concatenate([pcarry_ref[:, lo:hi], pg], axis=0)
        shift = 1
        while shift < w:
            acc = acc + pltpu.roll(acc, shift, 0)
            shift *= 2
        head = acc[CARRY_ROWS:2 * CARRY_ROWS, :] / jnp.minimum(head_pos + 1, w).astype(jnp.float32)
        mean = jnp.concatenate([head, acc[2 * CARRY_ROWS:, :] * (1.0 / w)], axis=0)
        pooled = mean - pg
        mix_ref[:, A_WIDTH + lo:A_WIDTH + hi] = _pool_project(pooled, g, bw_ref, bscale_ref).astype(jnp.bfloat16)
    tail = p[TM - CARRY_ROWS:, :]
    pcarry_ref[...] = tail
    pstate_ref[0] = tail


def _position_major(ref):
    n_pos = TM // SEQ_BLK
    rows = ref.reshape(SEQ_BLK, n_pos, ref.shape[-1])
    return jnp.concatenate([rows[:, q, :] for q in range(n_pos)], axis=0)


def _sample_mixers(u, v, p, aws_ref, abs_ref, st_ref, bw_ref, bscale_ref, mix_ref):
    n_pos = TM // SEQ_BLK

    for hd in range(N_HEADS):
        lo, hi = hd * HEAD_DIM, (hd + 1) * HEAD_DIM
        vplanes = [v[s * SEQ_BLK:(s + 1) * SEQ_BLK, lo:hi] for s in range(n_pos)]
        for i in range(n_pos):
            z = vplanes[0] * aws_ref[(hd * n_pos + i) * n_pos]
            for s in range(1, i + 1):
                z = z + vplanes[s] * aws_ref[(hd * n_pos + i) * n_pos + s]
            z = z + abs_ref[hd * n_pos + i]
            r0, r1 = i * SEQ_BLK, (i + 1) * SEQ_BLK
            mix_ref[r0:r1, lo:hi] = (u[r0:r1, lo:hi] * z).astype(jnp.bfloat16)

    for g, w in enumerate(POOL_WINDOWS):
        lo, hi = g * HEAD_DIM, (g + 1) * HEAD_DIM
        planes = [st_ref[0, k, :, lo:hi] for k in range(POOL_STATE)]
        planes += [p[i * SEQ_BLK:(i + 1) * SEQ_BLK, lo:hi] for i in range(n_pos)]
        pooled = []
        for i in range(n_pos):
            top = POOL_STATE + i
            s = planes[top - w + 1]
            for k in range(top - w + 2, top + 1):
                s = s + planes[k]
            pooled.append(s * (1.0 / w) - planes[top])
        pooled = jnp.concatenate(pooled, axis=0)
        mix_ref[:, A_WIDTH + lo:A_WIDTH + hi] = _pool_project(pooled, g, bw_ref, bscale_ref).astype(jnp.bfloat16)


def _route(h2, rwt_ref, rb_ref, su_ref, cnt_ref, meta_ref, gcol_ref):
    tm = h2.shape[0]
    h_hi = h2.astype(jnp.bfloat16)
    h_lo = (h2 - h_hi.astype(jnp.float32)).astype(jnp.bfloat16)
    s = (jnp.dot(h_hi, rwt_ref[...], preferred_element_type=jnp.float32)
         + jnp.dot(h_lo, rwt_ref[...], preferred_element_type=jnp.float32))
    st = s.T
    lt = st[0:N_ROUTER_ROWS, :] + st[N_ROUTER_ROWS:2 * N_ROUTER_ROWS, :] + rb_ref[...]
    row = lambda i: lt[i:i + 1, :]
    l1 = [row(i) for i in range(N_GROUPS)]
    m1 = jnp.maximum(jnp.maximum(l1[0], l1[1]), jnp.maximum(l1[2], l1[3]))
    grp = jnp.where(l1[0] == m1, 0, jnp.where(l1[1] == m1, 1, jnp.where(l1[2] == m1, 2, 3)))
    se = (jnp.exp(l1[0] - m1) + jnp.exp(l1[1] - m1)) + (jnp.exp(l1[2] - m1) + jnp.exp(l1[3] - m1))
    pg = 1.0 / se
    l2 = []
    for e in range(EXPERTS_PER_GROUP):
        c = [row(N_GROUPS + g * EXPERTS_PER_GROUP + e) for g in range(N_GROUPS)]
        l2.append(jnp.where(grp == 0, c[0], jnp.where(grp == 1, c[1], jnp.where(grp == 2, c[2], c[3]))))
    v0 = functools.reduce(jnp.maximum, l2)
    i0 = jnp.full_like(grp, EXPERTS_PER_GROUP - 1)
    for e in range(EXPERTS_PER_GROUP - 2, -1, -1):
        i0 = jnp.where(l2[e] == v0, e, i0)
    neg = jnp.float32(-jnp.inf)
    l2m = [jnp.where(i0 == e, neg, l2[e]) for e in range(EXPERTS_PER_GROUP)]
    v1 = functools.reduce(jnp.maximum, l2m)
    i1 = jnp.full_like(grp, EXPERTS_PER_GROUP - 1)
    for e in range(EXPERTS_PER_GROUP - 2, -1, -1):
        i1 = jnp.where((l2m[e] == v1) & (i0 != e), e, i1)
    d = jnp.exp(v1 - v0)
    g0 = pg / (1.0 + d)
    g1 = (pg * d) / (1.0 + d)
    e0 = grp * EXPERTS_PER_GROUP + i0
    e1 = grp * EXPERTS_PER_GROUP + i1

    eiota = lax.broadcasted_iota(jnp.int32, (N_EXPERTS, tm), 0)
    hit0 = eiota == e0
    hit1 = eiota == e1
    onehot = (hit0 | hit1).astype(jnp.bfloat16)
    prefix = jnp.dot(onehot, su_ref[...], preferred_element_type=jnp.float32)
    carry = cnt_ref[...]
    base = prefix + jnp.concatenate([carry] * (tm // LANES), axis=1)
    r0 = jnp.sum(jnp.where(hit0, base, 0.0), axis=0, keepdims=True)
    r1 = jnp.sum(jnp.where(hit1, base, 0.0), axis=0, keepdims=True)
    ones = jnp.ones((tm, LANES), jnp.bfloat16)
    cnt_ref[...] = carry + jnp.dot(onehot, ones, preferred_element_type=jnp.float32)

    meta_ref[0:1, :] = e0
    meta_ref[1:2, :] = e1
    meta_ref[2:3, :] = r0.astype(jnp.int32)
    meta_ref[3:4, :] = r1.astype(jnp.int32)
    meta_ref[4:8, :] = jnp.zeros((4, tm), jnp.int32)

    giota = lax.broadcasted_iota(jnp.int32, (LANES, tm), 0)
    gpad = jnp.where(giota == 0, g0, jnp.where(giota == 1, g1, 0.0))
    gcol_ref[...] = gpad.T


def _mixer_kernel(aws_s_ref, abs_s_ref,
                  x0_ref, xpn_ref, xsn_ref,
                  st_ref, n1g_ref, win_ref, ang_ref, anb_ref, aws_ref, abias_ref, bw_ref, bscale_ref,
                  wout_ref, n2g_ref, rwt_ref, rb_ref, su_ref,
                  x1_ref, h2_ref, meta_ref, gcol_ref, counts_ref, pstate_ref, pp_ref, vp_ref, asg_ref,
                  mix_ref, pcarry_ref, cnt_ref, x_ref, u_ref, v_ref, p_ref, part_meta, dest_vmem, fill_vmem, *dest_smem,
                  n_prompt_tiles, seq_tiles, part_tiles, t_total):
    i = pl.program_id(0)
    is_prompt = i < n_prompt_tiles
    j = i % seq_tiles
    part = i // part_tiles
    tile_in_part = i % part_tiles

    @pl.when((i > 0) & (tile_in_part == 0))
    def _():
        _, _, pad_start = _segment_offsets(cnt_ref[...])
        _stage_inversion(part_meta, pad_start, dest_vmem, dest_smem, fill_vmem, asg_ref)

    def in_proj(x):
        h = _rmsnorm(x, n1g_ref[...]).astype(jnp.bfloat16)
        proj = jnp.dot(h, win_ref[...], preferred_element_type=jnp.float32)
        uv = _gelu(proj[:, : 2 * A_WIDTH])
        u_ref[...] = uv[:, :A_WIDTH]
        v_ref[...] = _layernorm(uv[:, A_WIDTH:], ang_ref[...], anb_ref[...])
        p_ref[...] = proj[:, 2 * A_WIDTH:]
        x_ref[...] = x

    @pl.when(i == 0)
    def _():
        in_proj(x0_ref[0])

    @pl.when(i % part_tiles == 0)
    def _():
        cnt_ref[...] = jnp.zeros_like(cnt_ref)

    @pl.when(is_prompt & (j == 0))
    def _():
        pcarry_ref[...] = jnp.zeros_like(pcarry_ref)

    def finish_tile(x, invert_previous_part):
        x1 = x + jnp.dot(mix_ref[...], wout_ref[...], preferred_element_type=jnp.float32)
        x1_ref[...] = x1
        h2 = _rmsnorm(x1, n2g_ref[...])
        for s in range(ROW_TILES):
            h2_ref[pl.ds(s, TM, stride=ROW_TILES), :] = h2[:, s * LANES:(s + 1) * LANES]
        _route(h2, rwt_ref, rb_ref, su_ref, cnt_ref, meta_ref, gcol_ref)
        counts_ref[0] = cnt_ref[...]
        part_meta[:, pl.ds(pl.multiple_of(tile_in_part * TM, TM), TM)] = meta_ref[0:2 * TOP_K, :]
        if invert_previous_part:
            first_tok = tile_in_part * TM
            for k, dsm in enumerate(dest_smem):
                first_asg = k * t_total + (part - 1) * (part_tiles * TM) + first_tok
                for t in range(TM):
                    asg_ref[0, 0, dsm[0, first_tok + t]] = first_asg + t
        nxt = jnp.minimum(i + 1, pl.num_programs(0) - 1)
        in_proj(jnp.where(nxt < n_prompt_tiles, xpn_ref[0], xsn_ref[...].reshape(TM, D_MODEL)))

    @pl.when(is_prompt & (part == 0))
    def _():
        _prompt_mixers(j, u_ref[...], v_ref[...], p_ref[...], aws_ref, abias_ref, bw_ref, bscale_ref, mix_ref,
                       pcarry_ref, pstate_ref)
        finish_tile(x_ref[...], False)

    @pl.when(is_prompt & (part > 0))
    def _():
        _prompt_mixers(j, u_ref[...], v_ref[...], p_ref[...], aws_ref, abias_ref, bw_ref, bscale_ref, mix_ref,
                       pcarry_ref, pstate_ref)
        finish_tile(x_ref[...], True)

    @pl.when(jnp.logical_not(is_prompt))
    def _():
        vp_ref[...] = v_ref[...].reshape(vp_ref.shape)
        pp_ref[...] = p_ref[...].reshape(pp_ref.shape)
        _sample_mixers(_position_major(u_ref), _position_major(v_ref), _position_major(p_ref), aws_s_ref, abs_s_ref,
                       st_ref, bw_ref, bscale_ref, mix_ref)
        finish_tile(_position_major(x_ref), True)


def _segment_offsets(cnt):
    padded = jnp.floor((cnt + (TE - 1)) * (1.0 / TE)) * TE
    sub = lax.broadcasted_iota(jnp.int32, cnt.shape, 0)
    pad_end = padded
    shift = 1
    while shift < N_EXPERTS:
        pad_end = pad_end + jnp.where(sub >= shift, pltpu.roll(pad_end, shift, 0), 0.0)
        shift *= 2
    return padded, pad_end, pad_end - padded


def _dest_rows(meta_ref, pad_start, dest_ref):
    t_part = meta_ref.shape[1]
    base = pad_start.astype(jnp.int32)
    for k in range(TOP_K):
        e = meta_ref[k:k + 1, :]
        d = meta_ref[TOP_K + k:TOP_K + k + 1, :]
        for ex in range(N_EXPERTS):
            row = jnp.concatenate([base[ex:ex + 1, :]] * (t_part // LANES), axis=1)
            d = d + jnp.where(e == ex, row, 0)
        dest_ref[k] = d


def _stage_inversion(meta_ref, pad_start, dest_vmem, dest_smem, fill_vmem, asg_ref):
    _dest_rows(meta_ref, pad_start, dest_vmem)
    for k, dsm in enumerate(dest_smem):
        pltpu.sync_copy(dest_vmem.at[k], dsm)
    fill_vmem[...] = jnp.full(fill_vmem.shape, -1, jnp.int32)
    pltpu.sync_copy(fill_vmem, asg_ref)


def _tables_kernel(counts_ref, meta_ref, dest_ref, tab_ref):
    h = pl.program_id(0)
    cnt = counts_ref[0]
    sub = lax.broadcasted_iota(jnp.int32, cnt.shape, 0)
    padded, pad_end, pad_start = _segment_offsets(cnt)

    @pl.when(h == pl.num_programs(0) - 1)
    def _():
        _dest_rows(meta_ref, pad_start, dest_ref)

    n_valid = jnp.maximum(pad_end[N_EXPERTS - 1:, :] * (1.0 / TE), 1.0)
    tile = jnp.minimum(lax.broadcasted_iota(jnp.int32, (1, LANES), 1).astype(jnp.float32), n_valid - 1.0)
    tile_start = tile * TE
    tile_e = jnp.minimum(jnp.sum((pad_end <= tile_start).astype(jnp.float32), axis=0, keepdims=True),
                         N_EXPERTS - 1.0)
    is_e = sub.astype(jnp.float32) == tile_e
    seg_start = jnp.sum(jnp.where(is_e, pad_start, 0.0), axis=0, keepdims=True)
    seg_cnt = jnp.sum(jnp.where(is_e, cnt, 0.0), axis=0, keepdims=True)
    rows = jnp.clip(seg_cnt - (tile_start - seg_start), 0.0, TE)
    tab_ref[0, 0:1, :] = tile_e.astype(jnp.int32)
    tab_ref[0, 1:2, :] = rows.astype(jnp.int32)
    tab_ref[0, 2:3, :] = n_valid.astype(jnp.int32)
    tab_ref[0, 3:, :] = jnp.zeros((SUBLANES - 3, LANES), jnp.int32)


def _expert_kernel(te_ref, nv_ref, nrows_ref,
                   src_ref, src_next_ref, dst_ref,
                   dest_k0_ref, dest_k1_ref,
                   h2_hbm, w1_ref, w3_ref, w2_ref,
                   out_hbm,
                   xt0, xt1, ob0, ob1, fill_vmem, asg_last, ssem, h2_ref, hsem, *, t_total, t_part):
    h = pl.program_id(0)
    i = pl.program_id(1)
    g = h * pl.num_programs(1) + i
    nv = nv_ref[h]
    slab = lambda r: pl.ds(pl.multiple_of(r * ROW_TILES, ROW_TILES), ROW_TILES)
    toks_per_tile = TE // TOP_K
    inv_tiles = t_part // toks_per_tile
    part_slabs = t_part * ROW_TILES
    fetch_part = lambda part: pltpu.make_async_copy(h2_hbm.at[pl.ds(part * part_slabs, part_slabs), :], h2_ref, hsem)

    def last_part_src(tile):
        def src(r):
            a = asg_last[0, tile * TE + r]
            return a - jnp.where(a >= t_total, t_total + t_part, t_part)
        return src

    first_tables = (lambda r: src_ref[0, 0, r], lambda r: src_next_ref[0, 0, r],
                    lambda first, uu: dst_ref[0, 0, first + uu])
    last_tables = (last_part_src(i), last_part_src(jnp.minimum(i + 1, nv - 1)),
                   lambda first, uu: asg_last[0, (i * TE + first) + uu])

    def gather(src, xt):
        for r in range(TE):
            xt[r * ROW_TILES:(r + 1) * ROW_TILES, :] = h2_ref[slab(src(r)), :]

    def invert_share():
        first_tok = jnp.minimum(i, inv_tiles - 1) * toks_per_tile
        for k, dest_ref in enumerate((dest_k0_ref, dest_k1_ref)):
            first_asg = k * t_total + t_part + first_tok
            for t in range(toks_per_tile):
                asg_last[0, dest_ref[0, 0, t]] = first_asg + t

    def wait_scatter(tile, ob, other, sem):
        n = nrows_ref[tile] * ROW_TILES
        pltpu.make_async_copy(other.at[pl.ds(0, n), :], ob.at[pl.ds(0, n), :], sem).wait()

    def tile_body(xt, xt_next, ob, ob_other, sem, sem_other, tables, first_part):
        _, src_next, dst = tables

        @pl.when(i >= 2)
        def _():
            wait_scatter(g - 2, ob, ob_other, sem)

        def compute(m):
            gather(src_next, xt_next)
            if first_part:
                invert_share()
            x = jnp.concatenate([_row_slab(xt, s, m) for s in range(ROW_TILES)], axis=1).astype(jnp.bfloat16)
            a = jnp.dot(x, w1_ref[0].astype(jnp.bfloat16), preferred_element_type=jnp.float32)
            b = jnp.dot(x, w3_ref[0].astype(jnp.bfloat16), preferred_element_type=jnp.float32)
            hmid = (a * jax.nn.sigmoid(a)) * b
            o = jnp.dot(hmid.astype(jnp.bfloat16), w2_ref[0].astype(jnp.bfloat16),
                        preferred_element_type=jnp.float32)
            for s in range(ROW_TILES):
                ob[pl.ds(s, m, stride=ROW_TILES), :] = o[:, s * LANES:(s + 1) * LANES]

        n = nrows_ref[g]

        @pl.when(n > TE // 2)
        def _():
            compute(TE)

        @pl.when(n <= TE // 2)
        def _():
            compute(TE // 2)

        copy = lambda first, uu: pltpu.make_async_copy(ob.at[slab(first + uu), :], out_hbm.at[dst(first, uu)], sem)

        def issue_groups(first_row, n_groups, group):
            def body(rb, carry):
                for uu in range(group):
                    copy(first_row + rb * group, uu).start(priority=uu % 2)
                return carry
            lax.fori_loop(0, n_groups, body, 0)

        n_full = n // DMA_UNROLL
        n_small = n // SUBLANES - n_full * (DMA_UNROLL // SUBLANES)
        issue_groups(0, n_full, DMA_UNROLL)
        issue_groups(n_full * DMA_UNROLL, n_small, SUBLANES)

        def tail(r, carry):
            copy(r, 0).start(priority=1)
            return carry
        lax.fori_loop((n // SUBLANES) * SUBLANES, n, tail, 0)

        @pl.when(i == nv - 1)
        def _():
            @pl.when(i >= 1)
            def _():
                wait_scatter(g - 1, ob_other, ob, sem_other)
            wait_scatter(g, ob, ob_other, sem)
            if first_part:
                fetch_part(1).start()

    def part_body(tables, first_part):
        @pl.when(i == 0)
        def _():
            if not first_part:
                fetch_part(1).wait()
            gather(tables[0], xt0)

        @pl.when(i % 2 == 0)
        def _():
            tile_body(xt0, xt1, ob0, ob1, ssem.at[0], ssem.at[1], tables, first_part)

        @pl.when(i % 2 == 1)
        def _():
            tile_body(xt1, xt0, ob1, ob0, ssem.at[1], ssem.at[0], tables, first_part)

    @pl.when((h == 0) & (i == 0))
    def _():
        fetch_part(0).start()
        fill_vmem[...] = jnp.full(fill_vmem.shape, t_part, jnp.int32)
        pltpu.sync_copy(fill_vmem, asg_last)
        fetch_part(0).wait()

    @pl.when((i < nv) & (h == 0))
    def _():
        part_body(first_tables, True)

    @pl.when((i < nv) & (h > 0))
    def _():
        part_body(last_tables, False)


def _combine_kernel(x1_ref, gcol_ref, nfg_ref, o0_ref, o1_ref, yp_ref, ys_ref, *, n_prompt_tiles):
    i = pl.program_id(0)
    o0 = jnp.concatenate([_row_slab(o0_ref, s, TC) for s in range(ROW_TILES)], axis=1)
    o1 = jnp.concatenate([_row_slab(o1_ref, s, TC) for s in range(ROW_TILES)], axis=1)
    g = gcol_ref[...]
    moe = g[:, 0:1] * o0 + g[:, 1:2] * o1
    y = _rmsnorm(x1_ref[...] + moe, nfg_ref[...])

    @pl.when(i < n_prompt_tiles)
    def _():
        yp_ref[...] = y

    @pl.when(i >= n_prompt_tiles)
    def _():
        for blk in range(TC // TM):
            for q in range(TM // SEQ_BLK):
                r0 = blk * TM + q * SEQ_BLK
                ys_ref[blk * SEQ_BLK:(blk + 1) * SEQ_BLK, q, :] = y[r0:r0 + SEQ_BLK, :]


def _const_spec(shape):
    return pl.BlockSpec(shape, lambda *_: (0,) * len(shape))


def kernel(x_prompt, x_sample, state_pool, norm1_g, w_in, a_norm_g, a_norm_b, a_ws, a_bs, b_w, b_scale, w_out,
           norm2_g, r1_w, r1_b, r2_w, r2_b, exp_w1, exp_w3, exp_w2, normf_g):
    f32, bf16, i32 = jnp.float32, jnp.bfloat16, jnp.int32
    n_batch, seq, _ = x_prompt.shape
    dec_batch, dec_seq, _ = x_sample.shape
    assert norm1_g.shape[0] == 1 and seq % TM == 0 and TM % CHUNK == 0
    assert dec_seq * SEQ_BLK == TM and dec_batch % SEQ_BLK == 0 and dec_seq <= CHUNK
    t_prompt = n_batch * seq
    t_sample = dec_batch * dec_seq
    t_total = t_prompt + t_sample
    n_tok_tiles = t_total // TM
    n_prompt_tiles = t_prompt // TM
    n_sample_tiles = t_sample // TM
    seq_tiles = seq // TM
    plane_rows = t_total * ROW_TILES
    assert n_tok_tiles % N_PARTS == 0
    part_tiles = n_tok_tiles // N_PARTS
    t_part = part_tiles * TM
    n_exp_tiles = -(-(TOP_K * t_part + N_EXPERTS * (TE - 1)) // TE)
    p_rows = n_exp_tiles * TE

    n1g = norm1_g[0][None, :]
    n2g = norm2_g[0][None, :]
    nfg = normf_g[None, :]
    win = w_in[0].astype(bf16)
    wout = w_out[0].astype(bf16)
    ang = a_norm_g[0][None, :]
    anb = a_norm_b[0][None, :]
    bw = b_w[0].astype(bf16)
    bscale = b_scale[0][None, :]
    abias = jnp.repeat(a_bs[0][:, :CHUNK].T, HEAD_DIM, axis=1)
    rw = jnp.concatenate([r1_w[0], r2_w[0].transpose(1, 0, 2).reshape(D_MODEL, N_EXPERTS),
                          jnp.zeros((D_MODEL, N_ROUTER_ROWS - N_GROUPS - N_EXPERTS), f32)], axis=1)
    rw_hi = rw.astype(bf16)
    rw_lo = (rw - rw_hi.astype(f32)).astype(bf16)
    rwt = jnp.concatenate([rw_hi, rw_lo, jnp.zeros((D_MODEL, LANES - 2 * N_ROUTER_ROWS), bf16)], axis=1)
    rbias = jnp.concatenate([r1_b[0], r2_b[0].reshape(-1),
                             jnp.zeros((N_ROUTER_ROWS - N_GROUPS - N_EXPERTS,), f32)])
    rb = jnp.broadcast_to(rbias[:, None], (N_ROUTER_ROWS, TM))
    su = (jnp.arange(TM)[:, None] < jnp.arange(TM)[None, :]).astype(bf16)
    st_planes = state_pool[0].reshape(n_sample_tiles, SEQ_BLK, POOL_STATE, B_WIDTH).transpose(0, 2, 1, 3)
    aws_s = a_ws[0][:, :dec_seq, :dec_seq].reshape(-1)
    abs_s = a_bs[0][:, :dec_seq].reshape(-1)

    cparams = pltpu.CompilerParams(dimension_semantics=("arbitrary",), vmem_limit_bytes=VMEM_LIMIT)

    p_tile = lambda i: jnp.minimum(i, n_prompt_tiles - 1)
    s_tile = lambda i: jnp.maximum(i - n_prompt_tiles, 0)
    sample_rows_shape = jax.ShapeDtypeStruct((dec_batch, dec_seq, A_WIDTH), f32)
    assert N_PARTS >= 2 and n_prompt_tiles >= part_tiles and t_part % TM == 0
    x1, h2, meta, gcol, counts, pstate, p_rows_sample, v_rows_sample, asg_head = pl.pallas_call(
        functools.partial(_mixer_kernel, n_prompt_tiles=n_prompt_tiles, seq_tiles=seq_tiles, part_tiles=part_tiles,
                          t_total=t_total),
        grid_spec=pltpu.PrefetchScalarGridSpec(
            num_scalar_prefetch=2,
            grid=(n_tok_tiles,),
            in_specs=[
                pl.BlockSpec((1, TM, D_MODEL), lambda i, *_: (0, 0, 0)),
                pl.BlockSpec((1, TM, D_MODEL),
                             lambda i, *_: (p_tile(i + 1) // seq_tiles, p_tile(i + 1) % seq_tiles, 0)),
                pl.BlockSpec((SEQ_BLK, dec_seq, D_MODEL),
                             lambda i, *_: (jnp.minimum(s_tile(i + 1), n_sample_tiles - 1), 0, 0)),
                pl.BlockSpec((1, POOL_STATE, SEQ_BLK, B_WIDTH), lambda i, *_: (s_tile(i), 0, 0, 0)),
                _const_spec((1, D_MODEL)), _const_spec((D_MODEL, 3 * A_WIDTH)), _const_spec((1, A_WIDTH)),
                _const_spec((1, A_WIDTH)), _const_spec((N_HEADS, CHUNK, CHUNK)), _const_spec((CHUNK, A_WIDTH)),
                _const_spec((N_GROUPS, HEAD_DIM, HEAD_DIM)), _const_spec((1, B_WIDTH)),
                _const_spec((D_MODEL, D_MODEL)), _const_spec((1, D_MODEL)), _const_spec((D_MODEL, LANES)),
                _const_spec((N_ROUTER_ROWS, TM)), _const_spec((TM, TM)),
            ],
            out_specs=[
                pl.BlockSpec((TM, D_MODEL), lambda i, *_: (i, 0)),
                pl.BlockSpec((TM * ROW_TILES, LANES), lambda i, *_: (i, 0)),
                pl.BlockSpec((SUBLANES, TM), lambda i, *_: (0, i)),
                pl.BlockSpec((TM, LANES), lambda i, *_: (i, 0)),
                pl.BlockSpec((1, N_EXPERTS, LANES), lambda i, *_: (i // part_tiles, 0, 0)),
                pl.BlockSpec((1, CARRY_ROWS, B_WIDTH), lambda i, *_: (p_tile(i) // seq_tiles, 0, 0)),
                pl.BlockSpec((SEQ_BLK, dec_seq, A_WIDTH), lambda i, *_: (s_tile(i), 0, 0)),
                pl.BlockSpec((SEQ_BLK, dec_seq, A_WIDTH), lambda i, *_: (s_tile(i), 0, 0)),
                pl.BlockSpec((1, 1, p_rows), lambda i, *_: (jnp.maximum(i // part_tiles - 1, 0), 0, 0),
                             memory_space=pltpu.SMEM),
            ],
            scratch_shapes=[pltpu.VMEM((TM, D_MODEL), bf16), pltpu.VMEM((CARRY_ROWS, B_WIDTH), f32),
                            pltpu.VMEM((N_EXPERTS, LANES), f32), pltpu.VMEM((TM, D_MODEL), f32),
                            pltpu.VMEM((TM, A_WIDTH), f32), pltpu.VMEM((TM, A_WIDTH), f32),
                            pltpu.VMEM((TM, B_WIDTH), f32),
                            pltpu.VMEM((2 * TOP_K, t_part), i32), pltpu.VMEM((TOP_K, 1, t_part), i32),
                            pltpu.VMEM((1, 1, p_rows), i32)] + [pltpu.SMEM((1, t_part), i32)] * TOP_K,
        ),
        out_shape=[
            jax.ShapeDtypeStruct((t_total, D_MODEL), f32),
            jax.ShapeDtypeStruct((plane_rows, LANES), f32),
            jax.ShapeDtypeStruct((SUBLANES, t_total), i32),
            jax.ShapeDtypeStruct((t_total, LANES), f32),
            jax.ShapeDtypeStruct((N_PARTS, N_EXPERTS, LANES), f32),
            jax.ShapeDtypeStruct((n_batch, CARRY_ROWS, B_WIDTH), f32),
            sample_rows_shape, sample_rows_shape,
            jax.ShapeDtypeStruct((N_PARTS - 1, 1, p_rows), i32),
        ],
        compiler_params=cparams,
        name="mixer",
    )(aws_s, abs_s, x_prompt, x_prompt, x_sample, st_planes, n1g, win, ang, anb, a_ws[0][:, :CHUNK, :CHUNK], abias,
      bw, bscale, wout, n2g, rwt, rb, su)

    assert n_exp_tiles <= LANES and t_part % LANES == 0
    dest_last, tab = pl.pallas_call(
        _tables_kernel,
        grid=(N_PARTS,),
        in_specs=[pl.BlockSpec((1, N_EXPERTS, LANES), lambda h: (h, 0, 0)),
                  pl.BlockSpec((SUBLANES, t_part), lambda h: (0, h))],
        out_specs=[_const_spec((TOP_K, 1, t_part)),
                   pl.BlockSpec((1, SUBLANES, LANES), lambda h: (h, 0, 0))],
        out_shape=[jax.ShapeDtypeStruct((TOP_K, 1, t_part), i32),
                   jax.ShapeDtypeStruct((N_PARTS, SUBLANES, LANES), i32)],
        compiler_params=cparams,
        name="route_tables",
    )(counts, meta)
    tile_e = tab[:, 0, :n_exp_tiles].reshape(-1)
    tile_rows = tab[:, 1, :n_exp_tiles].reshape(-1)
    n_valid = tab[:, 2, 0]
    assert N_PARTS == 2 and (TOP_K * t_part) % TE == 0
    asg = asg_head.reshape(-1)
    row_tok = jnp.where(asg >= t_total, asg - t_total, asg)
    row_src = jnp.where(asg >= 0, row_tok, 0).reshape(n_exp_tiles, 1, TE)
    row_dst = jnp.maximum(asg, 0).reshape(n_exp_tiles, 1, TE)

    flat = lambda h, i: h * n_exp_tiles + i
    cur_blk = lambda h, i, te, nv, nr: (jnp.minimum(i, nv[0] - 1), 0, 0)
    nxt_blk = lambda h, i, te, nv, nr: (jnp.minimum(i + 1, nv[0] - 1), 0, 0)
    w_blk = lambda h, i, te, nv, nr: (te[flat(h, i)], 0, 0)
    smem_blk = lambda imap: pl.BlockSpec((1, 1, TE), imap, memory_space=pltpu.SMEM)
    last_share = t_part // (TE // TOP_K) - 1
    share_blk = lambda k: pl.BlockSpec(
        (1, 1, TE // TOP_K), lambda h, i, *_: (k, 0, jnp.where(h == 0, jnp.minimum(i, last_share), last_share)),
        memory_space=pltpu.SMEM)
    row_buf = pltpu.VMEM((TE * ROW_TILES, LANES), f32)
    out_tok = pl.pallas_call(
        functools.partial(_expert_kernel, t_total=t_total, t_part=t_part),
        grid_spec=pltpu.PrefetchScalarGridSpec(
            num_scalar_prefetch=3,
            grid=(N_PARTS, n_exp_tiles),
            in_specs=[smem_blk(cur_blk), smem_blk(nxt_blk), smem_blk(cur_blk),
                      share_blk(0), share_blk(1),
                      pl.BlockSpec(memory_space=pl.ANY),
                      pl.BlockSpec((1, D_MODEL, D_EXPERT), w_blk),
                      pl.BlockSpec((1, D_MODEL, D_EXPERT), w_blk),
                      pl.BlockSpec((1, D_EXPERT, D_MODEL), w_blk)],
            out_specs=pl.BlockSpec(memory_space=pl.ANY),
            scratch_shapes=[row_buf, row_buf, row_buf, row_buf, pltpu.VMEM((1, p_rows), i32),
                            pltpu.SMEM((1, p_rows), i32), pltpu.SemaphoreType.DMA((2,)),
                            pltpu.VMEM((t_part * ROW_TILES, LANES), f32), pltpu.SemaphoreType.DMA(())],
        ),
        out_shape=jax.ShapeDtypeStruct((TOP_K * t_total, ROW_TILES, LANES), f32),
        compiler_params=pltpu.CompilerParams(dimension_semantics=("arbitrary", "arbitrary"),
                                             vmem_limit_bytes=EXPERT_VMEM_LIMIT),
        name="moe_experts",
    )(tile_e, n_valid, tile_rows, row_src, row_src, row_dst, dest_last, dest_last, h2, exp_w1[0], exp_w3[0], exp_w2[0])
    out_tok = out_tok.reshape(TOP_K * plane_rows, LANES)

    assert t_prompt % TC == 0 and t_sample % TC == 0
    n_c_prompt = t_prompt // TC
    y_prompt, y_sample = pl.pallas_call(
        functools.partial(_combine_kernel, n_prompt_tiles=n_c_prompt),
        grid=(t_total // TC,),
        in_specs=[pl.BlockSpec((TC, D_MODEL), lambda i: (i, 0)),
                  pl.BlockSpec((TC, LANES), lambda i: (i, 0)),
                  _const_spec((1, D_MODEL)),
                  pl.BlockSpec((TC * ROW_TILES, LANES), lambda i: (i, 0)),
                  pl.BlockSpec((TC * ROW_TILES, LANES), lambda i: (t_total // TC + i, 0))],
        out_specs=[pl.BlockSpec((TC, D_MODEL), lambda i: (jnp.minimum(i, n_c_prompt - 1), 0)),
                   pl.BlockSpec((TC // dec_seq, dec_seq, D_MODEL), lambda i: (jnp.maximum(i - n_c_prompt, 0), 0, 0))],
        out_shape=[jax.ShapeDtypeStruct((t_prompt, D_MODEL), f32),
                   jax.ShapeDtypeStruct((dec_batch, dec_seq, D_MODEL), f32)],
        compiler_params=cparams,
        name="moe_combine",
    )(x1, gcol, nfg, out_tok, out_tok)

    y_prompt = y_prompt.reshape(n_batch, seq, D_MODEL)
    pool_state_prompt = pstate[None, :, CARRY_ROWS - POOL_STATE:, :]
    pool_state_sample = jnp.concatenate([state_pool[0], p_rows_sample], axis=1)[None, :, -POOL_STATE:, :]
    chunk_v_sample = v_rows_sample[None]
    return (y_prompt, y_sample, pool_state_prompt, pool_state_sample, chunk_v_sample)
```

```python
import functools
import math

import jax
import jax.numpy as jnp
from jax import lax
from jax.experimental import pallas as pl
from jax.experimental.pallas import tpu as pltpu

D_MODEL = 1024
A_WIDTH = 512
B_WIDTH = 512
N_HEADS = 4
HEAD_DIM = 128
CHUNK = 128
POOL_WINDOWS = (2, 4, 8, 16)
POOL_STATE = 15
N_GROUPS = 4
EXPERTS_PER_GROUP = 8
N_EXPERTS = 32
TOP_K = 2
D_EXPERT = 512
EPS = 1e-6

SUBLANES = 8
LANES = 128
ROW_TILES = D_MODEL // LANES

TM = 256
TC = 1024
TE = 512
SEQ_BLK = 32
N_ROUTER_ROWS = 40
CARRY_ROWS = 16
DMA_UNROLL = 32
FETCH_CHUNKS = 8
VMEM_LIMIT = 48 * 1024 * 1024
N_PARTS = 2
EXPERT_VMEM_LIMIT = 62 * 1024 * 1024

_INV_SQRT2 = 1.0 / math.sqrt(2.0)


def _rmsnorm(x, g):
    r = lax.rsqrt(jnp.mean(x * x, axis=-1, keepdims=True) + EPS)
    return (x * r) * g


def _gelu(x):
    return 0.5 * x * (1.0 + lax.erf(x * _INV_SQRT2))


def _layernorm(x, g, b):
    mu = jnp.mean(x, axis=-1, keepdims=True)
    xc = x - mu
    var = jnp.mean(xc * xc, axis=-1, keepdims=True)
    return (xc * lax.rsqrt(var + EPS)) * g + b


def _row_slab(ref, s, n):
    return ref[pl.ds(s, n, stride=ROW_TILES), :]


def _pool_project(pooled, g, bw_ref, bscale_ref):
    lo, hi = g * HEAD_DIM, (g + 1) * HEAD_DIM
    hb = jnp.dot(pooled.astype(jnp.bfloat16), bw_ref[g], preferred_element_type=jnp.float32)
    return hb * bscale_ref[:, lo:hi]


def _prompt_mixers(j, u, v, p, aws_ref, abias_ref, bw_ref, bscale_ref, mix_ref, pcarry_ref, pstate_ref):
    tri = (lax.broadcasted_iota(jnp.int32, (CHUNK, CHUNK), 0)
           >= lax.broadcasted_iota(jnp.int32, (CHUNK, CHUNK), 1))
    vb = v.astype(jnp.bfloat16)
    for hd in range(N_HEADS):
        lo, hi = hd * HEAD_DIM, (hd + 1) * HEAD_DIM
        w = jnp.where(tri, aws_ref[hd], 0.0).astype(jnp.bfloat16)
        for c in range(TM // CHUNK):
            r0, r1 = c * CHUNK, (c + 1) * CHUNK
            z = jnp.dot(w, vb[r0:r1, lo:hi], preferred_element_type=jnp.float32) + abias_ref[:, lo:hi]
            mix_ref[r0:r1, lo:hi] = (u[r0:r1, lo:hi] * z).astype(jnp.bfloat16)

    head_pos = j * TM + lax.broadcasted_iota(jnp.int32, (CARRY_ROWS, LANES), 0)
    for g, w in enumerate(POOL_WINDOWS):
        lo, hi = g * HEAD_DIM, (g + 1) * HEAD_DIM
        pg = p[:, lo:hi]
        acc = jnp.concatenate([pcarry_ref[:, lo:hi], pg], axis=0)
        shift = 1
        while shift < w:
            acc = acc + pltpu.roll(acc, shift, 0)
            shift *= 2
        head = acc[CARRY_ROWS:2 * CARRY_ROWS, :] / jnp.minimum(head_pos + 1, w).astype(jnp.float32)
        mean = jnp.concatenate([head, acc[2 * CARRY_ROWS:, :] * (1.0 / w)], axis=0)
        pooled = mean - pg
        mix_ref[:, A_WIDTH + lo:A_WIDTH + hi] = _pool_project(pooled, g, bw_ref, bscale_ref).astype(jnp.bfloat16)
    tail = p[TM - CARRY_ROWS:, :]
    pcarry_ref[...] = tail
    pstate_ref[0] = tail


def _position_major(ref):
    n_pos = TM // SEQ_BLK
    rows = ref.reshape(SEQ_BLK, n_pos, ref.shape[-1])
    return jnp.concatenate([rows[:, q, :] for q in range(n_pos)], axis=0)


def _sample_mixers(u, v, p, aws_ref, abs_ref, st_ref, bw_ref, bscale_ref, mix_ref):
    n_pos = TM // SEQ_BLK

    for hd in range(N_HEADS):
        lo, hi = hd * HEAD_DIM, (hd + 1) * HEAD_DIM
        vplanes = [v[s * SEQ_BLK:(s + 1) * SEQ_BLK, lo:hi] for s in range(n_pos)]
        for i in range(n_pos):
            z = vplanes[0] * aws_ref[(hd * n_pos + i) * n_pos]
            for s in range(1, i + 1):
                z = z + vplanes[s] * aws_ref[(hd * n_pos + i) * n_pos + s]
            z = z + abs_ref[hd * n_pos + i]
            r0, r1 = i * SEQ_BLK, (i + 1) * SEQ_BLK
            mix_ref[r0:r1, lo:hi] = (u[r0:r1, lo:hi] * z).astype(jnp.bfloat16)

    for g, w in enumerate(POOL_WINDOWS):
        lo, hi = g * HEAD_DIM, (g + 1) * HEAD_DIM
        planes = [st_ref[0, k, :, lo:hi] for k in range(POOL_STATE)]
        planes += [p[i * SEQ_BLK:(i + 1) * SEQ_BLK, lo:hi] for i in range(n_pos)]
        pooled = []
        for i in range(n_pos):
            top = POOL_STATE + i
            s = planes[top - w + 1]
            for k in range(top - w + 2, top + 1):
                s = s + planes[k]
            pooled.append(s * (1.0 / w) - planes[top])
        pooled = jnp.concatenate(pooled, axis=0)
        mix_ref[:, A_WIDTH + lo:A_WIDTH + hi] = _pool_project(pooled, g, bw_ref, bscale_ref).astype(jnp.bfloat16)


def _route(h2, rwt_ref, rb_ref, su_ref, cnt_ref, meta_ref, gcol_ref):
    tm = h2.shape[0]
    h_hi = h2.astype(jnp.bfloat16)
    h_lo = (h2 - h_hi.astype(jnp.float32)).astype(jnp.bfloat16)
    s = (jnp.dot(h_hi, rwt_ref[...], preferred_element_type=jnp.float32)
         + jnp.dot(h_lo, rwt_ref[...], preferred_element_type=jnp.float32))
    st = s.T
    lt = st[0:N_ROUTER_ROWS, :] + st[N_ROUTER_ROWS:2 * N_ROUTER_ROWS, :] + rb_ref[...]
    row = lambda i: lt[i:i + 1, :]
    l1 = [row(i) for i in range(N_GROUPS)]
    m1 = jnp.maximum(jnp.maximum(l1[0], l1[1]), jnp.maximum(l1[2], l1[3]))
    grp = jnp.where(l1[0] == m1, 0, jnp.where(l1[1] == m1, 1, jnp.where(l1[2] == m1, 2, 3)))
    se = (jnp.exp(l1[0] - m1) + jnp.exp(l1[1] - m1)) + (jnp.exp(l1[2] - m1) + jnp.exp(l1[3] - m1))
    pg = 1.0 / se
    l2 = []
    for e in range(EXPERTS_PER_GROUP):
        c = [row(N_GROUPS + g * EXPERTS_PER_GROUP + e) for g in range(N_GROUPS)]
        l2.append(jnp.where(grp == 0, c[0], jnp.where(grp == 1, c[1], jnp.where(grp == 2, c[2], c[3]))))
    v0 = functools.reduce(jnp.maximum, l2)
    i0 = jnp.full_like(grp, EXPERTS_PER_GROUP - 1)
    for e in range(EXPERTS_PER_GROUP - 2, -1, -1):
        i0 = jnp.where(l2[e] == v0, e, i0)
    neg = jnp.float32(-jnp.inf)
    l2m = [jnp.where(i0 == e, neg, l2[e]) for e in range(EXPERTS_PER_GROUP)]
    v1 = functools.reduce(jnp.maximum, l2m)
    i1 = jnp.full_like(grp, EXPERTS_PER_GROUP - 1)
    for e in range(EXPERTS_PER_GROUP - 2, -1, -1):
        i1 = jnp.where((l2m[e] == v1) & (i0 != e), e, i1)
    d = jnp.exp(v1 - v0)
    g0 = pg / (1.0 + d)
    g1 = (pg * d) / (1.0 + d)
    e0 = grp * EXPERTS_PER_GROUP + i0
    e1 = grp * EXPERTS_PER_GROUP + i1

    eiota = lax.broadcasted_iota(jnp.int32, (N_EXPERTS, tm), 0)
    hit0 = eiota == e0
    hit1 = eiota == e1
    onehot = (hit0 | hit1).astype(jnp.bfloat16)
    prefix = jnp.dot(onehot, su_ref[...], preferred_element_type=jnp.float32)
    carry = cnt_ref[...]
    base = prefix + jnp.concatenate([carry] * (tm // LANES), axis=1)
    r0 = jnp.sum(jnp.where(hit0, base, 0.0), axis=0, keepdims=True)
    r1 = jnp.sum(jnp.where(hit1, base, 0.0), axis=0, keepdims=True)
    ones = jnp.ones((tm, LANES), jnp.bfloat16)
    cnt_ref[...] = carry + jnp.dot(onehot, ones, preferred_element_type=jnp.float32)

    meta_ref[0:1, :] = e0
    meta_ref[1:2, :] = e1
    meta_ref[2:3, :] = r0.astype(jnp.int32)
    meta_ref[3:4, :] = r1.astype(jnp.int32)
    meta_ref[4:8, :] = jnp.zeros((4, tm), jnp.int32)

    giota = lax.broadcasted_iota(jnp.int32, (LANES, tm), 0)
    gpad = jnp.where(giota == 0, g0, jnp.where(giota == 1, g1, 0.0))
    gcol_ref[...] = gpad.T


def _mixer_kernel(aws_s_ref, abs_s_ref,
                  x0_ref, xpn_ref, xsn_ref,
                  st_ref, n1g_ref, win_ref, ang_ref, anb_ref, aws_ref, abias_ref, bw_ref, bscale_ref,
                  wout_ref, n2g_ref, rwt_ref, rb_ref, su_ref,
                  x1_ref, h2_ref, meta_ref, gcol_ref, counts_ref, pstate_ref, pp_ref, vp_ref, asg_ref,
                  mix_ref, pcarry_ref, cnt_ref, x_ref, u_ref, v_ref, p_ref, part_meta, dest_vmem, fill_vmem, *dest_smem,
                  n_prompt_tiles, seq_tiles, part_tiles, t_total):
    i = pl.program_id(0)
    is_prompt = i < n_prompt_tiles
    j = i % seq_tiles
    part = i // part_tiles
    tile_in_part = i % part_tiles

    @pl.when((i > 0) & (tile_in_part == 0))
    def _():
        _, _, pad_start = _segment_offsets(cnt_ref[...])
        _stage_inversion(part_meta, pad_start, dest_vmem, dest_smem, fill_vmem, asg_ref)

    def in_proj(x):
        h = _rmsnorm(x, n1g_ref[...]).astype(jnp.bfloat16)
        proj = jnp.dot(h, win_ref[...], preferred_element_type=jnp.float32)
        uv = _gelu(proj[:, : 2 * A_WIDTH])
        u_ref[...] = uv[:, :A_WIDTH]
        v_ref[...] = _layernorm(uv[:, A_WIDTH:], ang_ref[...], anb_ref[...])
        p_ref[...] = proj[:, 2 * A_WIDTH:]
        x_ref[...] = x

    @pl.when(i == 0)
    def _():
        in_proj(x0_ref[0])

    @pl.when(i % part_tiles == 0)
    def _():
        cnt_ref[...] = jnp.zeros_like(cnt_ref)

    @pl.when(is_prompt & (j == 0))
    def _():
        pcarry_ref[...] = jnp.zeros_like(pcarry_ref)

    def finish_tile(x, invert_previous_part):
        x1 = x + jnp.dot(mix_ref[...], wout_ref[...], preferred_element_type=jnp.float32)
        x1_ref[...] = x1
        h2 = _rmsnorm(x1, n2g_ref[...])
        for s in range(ROW_TILES):
            h2_ref[pl.ds(s, TM, stride=ROW_TILES), :] = h2[:, s * LANES:(s + 1) * LANES]
        _route(h2, rwt_ref, rb_ref, su_ref, cnt_ref, meta_ref, gcol_ref)
        counts_ref[0] = cnt_ref[...]
        part_meta[:, pl.ds(pl.multiple_of(tile_in_part * TM, TM), TM)] = meta_ref[0:2 * TOP_K, :]
        if invert_previous_part:
            first_tok = tile_in_part * TM
            for k, dsm in enumerate(dest_smem):
                first_asg = k * t_total + (part - 1) * (part_tiles * TM) + first_tok
                for t in range(TM):
                    asg_ref[0, 0, dsm[0, first_tok + t]] = first_asg + t
        nxt = jnp.minimum(i + 1, pl.num_programs(0) - 1)
        in_proj(jnp.where(nxt < n_prompt_tiles, xpn_ref[0], xsn_ref[...].reshape(TM, D_MODEL)))

    @pl.when(is_prompt & (part == 0))
    def _():
        _prompt_mixers(j, u_ref[...], v_ref[...], p_ref[...], aws_ref, abias_ref, bw_ref, bscale_ref, mix_ref,
                       pcarry_ref, pstate_ref)
        finish_tile(x_ref[...], False)

    @pl.when(is_prompt & (part > 0))
    def _():
        _prompt_mixers(j, u_ref[...], v_ref[...], p_ref[...], aws_ref, abias_ref, bw_ref, bscale_ref, mix_ref,
                       pcarry_ref, pstate_ref)
        finish_tile(x_ref[...], True)

    @pl.when(jnp.logical_not(is_prompt))
    def _():
        vp_ref[...] = v_ref[...].reshape(vp_ref.shape)
        pp_ref[...] = p_ref[...].reshape(pp_ref.shape)
        _sample_mixers(_position_major(u_ref), _position_major(v_ref), _position_major(p_ref), aws_s_ref, abs_s_ref,
                       st_ref, bw_ref, bscale_ref, mix_ref)
        finish_tile(_position_major(x_ref), True)


def _segment_offsets(cnt):
    padded = jnp.floor((cnt + (TE - 1)) * (1.0 / TE)) * TE
    sub = lax.broadcasted_iota(jnp.int32, cnt.shape, 0)
    pad_end = padded
    shift = 1
    while shift < N_EXPERTS:
        pad_end = pad_end + jnp.where(sub >= shift, pltpu.roll(pad_end, shift, 0), 0.0)
        shift *= 2
    return padded, pad_end, pad_end - padded


def _dest_rows(meta_ref, pad_start, dest_ref):
    t_part = meta_ref.shape[1]
    base = pad_start.astype(jnp.int32)
    for k in range(TOP_K):
        e = meta_ref[k:k + 1, :]
        d = meta_ref[TOP_K + k:TOP_K + k + 1, :]
        for ex in range(N_EXPERTS):
            row = jnp.concatenate([base[ex:ex + 1, :]] * (t_part // LANES), axis=1)
            d = d + jnp.where(e == ex, row, 0)
        dest_ref[k] = d


def _stage_inversion(meta_ref, pad_start, dest_vmem, dest_smem, fill_vmem, asg_ref):
    _dest_rows(meta_ref, pad_start, dest_vmem)
    for k, dsm in enumerate(dest_smem):
        pltpu.sync_copy(dest_vmem.at[k], dsm)
    fill_vmem[...] = jnp.full(fill_vmem.shape, -1, jnp.int32)
    pltpu.sync_copy(fill_vmem, asg_ref)


def _tables_kernel(counts_ref, meta_ref, dest_ref, tab_ref):
    h = pl.program_id(0)
    cnt = counts_ref[0]
    sub = lax.broadcasted_iota(jnp.int32, cnt.shape, 0)
    padded, pad_end, pad_start = _segment_offsets(cnt)

    @pl.when(h == pl.num_programs(0) - 1)
    def _():
        _dest_rows(meta_ref, pad_start, dest_ref)

    n_valid = jnp.maximum(pad_end[N_EXPERTS - 1:, :] * (1.0 / TE), 1.0)
    tile = jnp.minimum(lax.broadcasted_iota(jnp.int32, (1, LANES), 1).astype(jnp.float32), n_valid - 1.0)
    tile_start = tile * TE
    tile_e = jnp.minimum(jnp.sum((pad_end <= tile_start).astype(jnp.float32), axis=0, keepdims=True),
                         N_EXPERTS - 1.0)
    is_e = sub.astype(jnp.float32) == tile_e
    seg_start = jnp.sum(jnp.where(is_e, pad_start, 0.0), axis=0, keepdims=True)
    seg_cnt = jnp.sum(jnp.where(is_e, cnt, 0.0), axis=0, keepdims=True)
    rows = jnp.clip(seg_cnt - (tile_start - seg_start), 0.0, TE)
    tab_ref[0, 0:1, :] = tile_e.astype(jnp.int32)
    tab_ref[0, 1:2, :] = rows.astype(jnp.int32)
    tab_ref[0, 2:3, :] = n_valid.astype(jnp.int32)
    tab_ref[0, 3:, :] = jnp.zeros((SUBLANES - 3, LANES), jnp.int32)


def _expert_kernel(te_ref, nv_ref, nrows_ref,
                   src_ref, src_next_ref, dst_ref,
                   dest_k0_ref, dest_k1_ref,
                   h2_hbm, w1_ref, w3_ref, w2_ref,
                   out_hbm,
                   xt0, xt1, ob0, ob1, fill_vmem, asg_last, ssem, h2_ref, hsem, *, t_total, t_part):
    h = pl.program_id(0)
    i = pl.program_id(1)
    g = h * pl.num_programs(1) + i
    nv = nv_ref[h]
    slab = lambda r: pl.ds(pl.multiple_of(r * ROW_TILES, ROW_TILES), ROW_TILES)
    toks_per_tile = TE // TOP_K
    inv_tiles = t_part // toks_per_tile
    part_slabs = t_part * ROW_TILES
    chunk = part_slabs // FETCH_CHUNKS

    def fetch_part(part, op):
        for c in range(FETCH_CHUNKS):
            cp = pltpu.make_async_copy(h2_hbm.at[pl.ds(part * part_slabs + c * chunk, chunk), :],
                                       h2_ref.at[pl.ds(c * chunk, chunk), :], hsem)
            if op == "start":
                cp.start(priority=c % 2)
            else:
                cp.wait()

    def last_part_src(tile):
        def src(r):
            a = asg_last[0, tile * TE + r]
            return a - jnp.where(a >= t_total, t_total + t_part, t_part)
        return src

    first_tables = (lambda r: src_ref[0, 0, r], lambda r: src_next_ref[0, 0, r],
                    lambda first, uu: dst_ref[0, 0, first + uu])
    last_tables = (last_part_src(i), last_part_src(jnp.minimum(i + 1, nv - 1)),
                   lambda first, uu: asg_last[0, (i * TE + first) + uu])

    def gather(src, xt):
        for r in range(TE):
            xt[r * ROW_TILES:(r + 1) * ROW_TILES, :] = h2_ref[slab(src(r)), :]

    def invert_share():
        first_tok = jnp.minimum(i, inv_tiles - 1) * toks_per_tile
        for k, dest_ref in enumerate((dest_k0_ref, dest_k1_ref)):
            first_asg = k * t_total + t_part + first_tok
            for t in range(toks_per_tile):
                asg_last[0, dest_ref[0, 0, t]] = first_asg + t

    def wait_scatter(tile, ob, other, sem):
        n = nrows_ref[tile] * ROW_TILES
        pltpu.make_async_copy(other.at[pl.ds(0, n), :], ob.at[pl.ds(0, n), :], sem).wait()

    def tile_body(xt, xt_next, ob, ob_other, sem, sem_other, tables, first_part):
        _, src_next, dst = tables

        @pl.when(i >= 2)
        def _():
            wait_scatter(g - 2, ob, ob_other, sem)

        def compute(m):
            gather(src_next, xt_next)
            if first_part:
                invert_share()
            x = jnp.concatenate([_row_slab(xt, s, m) for s in range(ROW_TILES)], axis=1).astype(jnp.bfloat16)
            a = jnp.dot(x, w1_ref[0].astype(jnp.bfloat16), preferred_element_type=jnp.float32)
            b = jnp.dot(x, w3_ref[0].astype(jnp.bfloat16), preferred_element_type=jnp.float32)
            hmid = (a * jax.nn.sigmoid(a)) * b
            o = jnp.dot(hmid.astype(jnp.bfloat16), w2_ref[0].astype(jnp.bfloat16),
                        preferred_element_type=jnp.float32)
            for s in range(ROW_TILES):
                ob[pl.ds(s, m, stride=ROW_TILES), :] = o[:, s * LANES:(s + 1) * LANES]

        n = nrows_ref[g]

        @pl.when(n > TE // 2)
        def _():
            compute(TE)

        @pl.when(n <= TE // 2)
        def _():
            compute(TE // 2)

        copy = lambda first, uu: pltpu.make_async_copy(ob.at[slab(first + uu), :], out_hbm.at[dst(first, uu)], sem)

        def issue_groups(first_row, n_groups, group):
            def body(rb, carry):
                for uu in range(group):
                    copy(first_row + rb * group, uu).start(priority=uu % 2)
                return carry
            lax.fori_loop(0, n_groups, body, 0)

        n_full = n // DMA_UNROLL
        n_small = n // SUBLANES - n_full * (DMA_UNROLL // SUBLANES)
        issue_groups(0, n_full, DMA_UNROLL)
        issue_groups(n_full * DMA_UNROLL, n_small, SUBLANES)

        def tail(r, carry):
            copy(r, 0).start(priority=1)
            return carry
        lax.fori_loop((n // SUBLANES) * SUBLANES, n, tail, 0)

        @pl.when(i == nv - 1)
        def _():
            @pl.when(i >= 1)
            def _():
                wait_scatter(g - 1, ob_other, ob, sem_other)
            wait_scatter(g, ob, ob_other, sem)
            if first_part:
                fetch_part(1, "start")

    def part_body(tables, first_part):
        @pl.when(i == 0)
        def _():
            if not first_part:
                fetch_part(1, "wait")
            gather(tables[0], xt0)

        @pl.when(i % 2 == 0)
        def _():
            tile_body(xt0, xt1, ob0, ob1, ssem.at[0], ssem.at[1], tables, first_part)

        @pl.when(i % 2 == 1)
        def _():
            tile_body(xt1, xt0, ob1, ob0, ssem.at[1], ssem.at[0], tables, first_part)

    @pl.when((h == 0) & (i == 0))
    def _():
        fetch_part(0, "start")
        fill_vmem[...] = jnp.full(fill_vmem.shape, t_part, jnp.int32)
        pltpu.sync_copy(fill_vmem, asg_last)
        fetch_part(0, "wait")

    @pl.when((i < nv) & (h == 0))
    def _():
        part_body(first_tables, True)

    @pl.when((i < nv) & (h > 0))
    def _():
        part_body(last_tables, False)


def _combine_kernel(x1_ref, gcol_ref, nfg_ref, o0_ref, o1_ref, yp_ref, ys_ref, *, n_prompt_tiles):
    i = pl.program_id(0)
    o0 = jnp.concatenate([_row_slab(o0_ref, s, TC) for s in range(ROW_TILES)], axis=1)
    o1 = jnp.concatenate([_row_slab(o1_ref, s, TC) for s in range(ROW_TILES)], axis=1)
    g = gcol_ref[...]
    moe = g[:, 0:1] * o0 + g[:, 1:2] * o1
    y = _rmsnorm(x1_ref[...] + moe, nfg_ref[...])

    @pl.when(i < n_prompt_tiles)
    def _():
        yp_ref[...] = y

    @pl.when(i >= n_prompt_tiles)
    def _():
        for blk in range(TC // TM):
            for q in range(TM // SEQ_BLK):
                r0 = blk * TM + q * SEQ_BLK
                ys_ref[blk * SEQ_BLK:(blk + 1) * SEQ_BLK, q, :] = y[r0:r0 + SEQ_BLK, :]


def _const_spec(shape):
    return pl.BlockSpec(shape, lambda *_: (0,) * len(shape))


def kernel(x_prompt, x_sample, state_pool, norm1_g, w_in, a_norm_g, a_norm_b, a_ws, a_bs, b_w, b_scale, w_out,
           norm2_g, r1_w, r1_b, r2_w, r2_b, exp_w1, exp_w3, exp_w2, normf_g):
    f32, bf16, i32 = jnp.float32, jnp.bfloat16, jnp.int32
    n_batch, seq, _ = x_prompt.shape
    dec_batch, dec_seq, _ = x_sample.shape
    assert norm1_g.shape[0] == 1 and seq % TM == 0 and TM % CHUNK == 0
    assert dec_seq * SEQ_BLK == TM and dec_batch % SEQ_BLK == 0 and dec_seq <= CHUNK
    t_prompt = n_batch * seq
    t_sample = dec_batch * dec_seq
    t_total = t_prompt + t_sample
    n_tok_tiles = t_total // TM
    n_prompt_tiles = t_prompt // TM
    n_sample_tiles = t_sample // TM
    seq_tiles = seq // TM
    plane_rows = t_total * ROW_TILES
    assert n_tok_tiles % N_PARTS == 0
    part_tiles = n_tok_tiles // N_PARTS
    t_part = part_tiles * TM
    n_exp_tiles = -(-(TOP_K * t_part + N_EXPERTS * (TE - 1)) // TE)
    p_rows = n_exp_tiles * TE

    n1g = norm1_g[0][None, :]
    n2g = norm2_g[0][None, :]
    nfg = normf_g[None, :]
    win = w_in[0].astype(bf16)
    wout = w_out[0].astype(bf16)
    ang = a_norm_g[0][None, :]
    anb = a_norm_b[0][None, :]
    bw = b_w[0].astype(bf16)
    bscale = b_scale[0][None, :]
    abias = jnp.repeat(a_bs[0][:, :CHUNK].T, HEAD_DIM, axis=1)
    rw = jnp.concatenate([r1_w[0], r2_w[0].transpose(1, 0, 2).reshape(D_MODEL, N_EXPERTS),
                          jnp.zeros((D_MODEL, N_ROUTER_ROWS - N_GROUPS - N_EXPERTS), f32)], axis=1)
    rw_hi = rw.astype(bf16)
    rw_lo = (rw - rw_hi.astype(f32)).astype(bf16)
    rwt = jnp.concatenate([rw_hi, rw_lo, jnp.zeros((D_MODEL, LANES - 2 * N_ROUTER_ROWS), bf16)], axis=1)
    rbias = jnp.concatenate([r1_b[0], r2_b[0].reshape(-1),
                             jnp.zeros((N_ROUTER_ROWS - N_GROUPS - N_EXPERTS,), f32)])
    rb = jnp.broadcast_to(rbias[:, None], (N_ROUTER_ROWS, TM))
    su = (jnp.arange(TM)[:, None] < jnp.arange(TM)[None, :]).astype(bf16)
    st_planes = state_pool[0].reshape(n_sample_tiles, SEQ_BLK, POOL_STATE, B_WIDTH).transpose(0, 2, 1, 3)
    aws_s = a_ws[0][:, :dec_seq, :dec_seq].reshape(-1)
    abs_s = a_bs[0][:, :dec_seq].reshape(-1)

    cparams = pltpu.CompilerParams(dimension_semantics=("arbitrary",), vmem_limit_bytes=VMEM_LIMIT)

    p_tile = lambda i: jnp.minimum(i, n_prompt_tiles - 1)
    s_tile = lambda i: jnp.maximum(i - n_prompt_tiles, 0)
    sample_rows_shape = jax.ShapeDtypeStruct((dec_batch, dec_seq, A_WIDTH), f32)
    assert N_PARTS >= 2 and n_prompt_tiles >= part_tiles and t_part % TM == 0
    x1, h2, meta, gcol, counts, pstate, p_rows_sample, v_rows_sample, asg_head = pl.pallas_call(
        functools.partial(_mixer_kernel, n_prompt_tiles=n_prompt_tiles, seq_tiles=seq_tiles, part_tiles=part_tiles,
                          t_total=t_total),
        grid_spec=pltpu.PrefetchScalarGridSpec(
            num_scalar_prefetch=2,
            grid=(n_tok_tiles,),
            in_specs=[
                pl.BlockSpec((1, TM, D_MODEL), lambda i, *_: (0, 0, 0)),
                pl.BlockSpec((1, TM, D_MODEL),
                             lambda i, *_: (p_tile(i + 1) // seq_tiles, p_tile(i + 1) % seq_tiles, 0)),
                pl.BlockSpec((SEQ_BLK, dec_seq, D_MODEL),
                             lambda i, *_: (jnp.minimum(s_tile(i + 1), n_sample_tiles - 1), 0, 0)),
                pl.BlockSpec((1, POOL_STATE, SEQ_BLK, B_WIDTH), lambda i, *_: (s_tile(i), 0, 0, 0)),
                _const_spec((1, D_MODEL)), _const_spec((D_MODEL, 3 * A_WIDTH)), _const_spec((1, A_WIDTH)),
                _const_spec((1, A_WIDTH)), _const_spec((N_HEADS, CHUNK, CHUNK)), _const_spec((CHUNK, A_WIDTH)),
                _const_spec((N_GROUPS, HEAD_DIM, HEAD_DIM)), _const_spec((1, B_WIDTH)),
                _const_spec((D_MODEL, D_MODEL)), _const_spec((1, D_MODEL)), _const_spec((D_MODEL, LANES)),
                _const_spec((N_ROUTER_ROWS, TM)), _const_spec((TM, TM)),
            ],
            out_specs=[
                pl.BlockSpec((TM, D_MODEL), lambda i, *_: (i, 0)),
                pl.BlockSpec((TM * ROW_TILES, LANES), lambda i, *_: (i, 0)),
                pl.BlockSpec((SUBLANES, TM), lambda i, *_: (0, i)),
                pl.BlockSpec((TM, LANES), lambda i, *_: (i, 0)),
                pl.BlockSpec((1, N_EXPERTS, LANES), lambda i, *_: (i // part_tiles, 0, 0)),
                pl.BlockSpec((1, CARRY_ROWS, B_WIDTH), lambda i, *_: (p_tile(i) // seq_tiles, 0, 0)),
                pl.BlockSpec((SEQ_BLK, dec_seq, A_WIDTH), lambda i, *_: (s_tile(i), 0, 0)),
                pl.BlockSpec((SEQ_BLK, dec_seq, A_WIDTH), lambda i, *_: (s_tile(i), 0, 0)),
                pl.BlockSpec((1, 1, p_rows), lambda i, *_: (jnp.maximum(i // part_tiles - 1, 0), 0, 0),
                             memory_space=pltpu.SMEM),
            ],
            scratch_shapes=[pltpu.VMEM((TM, D_MODEL), bf16), pltpu.VMEM((CARRY_ROWS, B_WIDTH), f32),
                            pltpu.VMEM((N_EXPERTS, LANES), f32), pltpu.VMEM((TM, D_MODEL), f32),
                            pltpu.VMEM((TM, A_WIDTH), f32), pltpu.VMEM((TM, A_WIDTH), f32),
                            pltpu.VMEM((TM, B_WIDTH), f32),
                            pltpu.VMEM((2 * TOP_K, t_part), i32), pltpu.VMEM((TOP_K, 1, t_part), i32),
                            pltpu.VMEM((1, 1, p_rows), i32)] + [pltpu.SMEM((1, t_part), i32)] * TOP_K,
        ),
        out_shape=[
            jax.ShapeDtypeStruct((t_total, D_MODEL), f32),
            jax.ShapeDtypeStruct((plane_rows, LANES), f32),
            jax.ShapeDtypeStruct((SUBLANES, t_total), i32),
            jax.ShapeDtypeStruct((t_total, LANES), f32),
            jax.ShapeDtypeStruct((N_PARTS, N_EXPERTS, LANES), f32),
            jax.ShapeDtypeStruct((n_batch, CARRY_ROWS, B_WIDTH), f32),
            sample_rows_shape, sample_rows_shape,
            jax.ShapeDtypeStruct((N_PARTS - 1, 1, p_rows), i32),
        ],
        compiler_params=cparams,
        name="mixer",
    )(aws_s, abs_s, x_prompt, x_prompt, x_sample, st_planes, n1g, win, ang, anb, a_ws[0][:, :CHUNK, :CHUNK], abias,
      bw, bscale, wout, n2g, rwt, rb, su)

    assert n_exp_tiles <= LANES and t_part % LANES == 0
    dest_last, tab = pl.pallas_call(
        _tables_kernel,
        grid=(N_PARTS,),
        in_specs=[pl.BlockSpec((1, N_EXPERTS, LANES), lambda h: (h, 0, 0)),
                  pl.BlockSpec((SUBLANES, t_part), lambda h: (0, h))],
        out_specs=[_const_spec((TOP_K, 1, t_part)),
                   pl.BlockSpec((1, SUBLANES, LANES), lambda h: (h, 0, 0))],
        out_shape=[jax.ShapeDtypeStruct((TOP_K, 1, t_part), i32),
                   jax.ShapeDtypeStruct((N_PARTS, SUBLANES, LANES), i32)],
        compiler_params=cparams,
        name="route_tables",
    )(counts, meta)
    tile_e = tab[:, 0, :n_exp_tiles].reshape(-1)
    tile_rows = tab[:, 1, :n_exp_tiles].reshape(-1)
    n_valid = tab[:, 2, 0]
    assert N_PARTS == 2 and (TOP_K * t_part) % TE == 0 and t_part % FETCH_CHUNKS == 0
    asg = asg_head.reshape(-1)
    row_tok = jnp.where(asg >= t_total, asg - t_total, asg)
    row_src = jnp.where(asg >= 0, row_tok, 0).reshape(n_exp_tiles, 1, TE)
    row_dst = jnp.maximum(asg, 0).reshape(n_exp_tiles, 1, TE)

    flat = lambda h, i: h * n_exp_tiles + i
    cur_blk = lambda h, i, te, nv, nr: (jnp.minimum(i, nv[0] - 1), 0, 0)
    nxt_blk = lambda h, i, te, nv, nr: (jnp.minimum(i + 1, nv[0] - 1), 0, 0)
    w_blk = lambda h, i, te, nv, nr: (te[flat(h, i)], 0, 0)
    smem_blk = lambda imap: pl.BlockSpec((1, 1, TE), imap, memory_space=pltpu.SMEM)
    last_share = t_part // (TE // TOP_K) - 1
    share_blk = lambda k: pl.BlockSpec(
        (1, 1, TE // TOP_K), lambda h, i, *_: (k, 0, jnp.where(h == 0, jnp.minimum(i, last_share), last_share)),
        memory_space=pltpu.SMEM)
    row_buf = pltpu.VMEM((TE * ROW_TILES, LANES), f32)
    out_tok = pl.pallas_call(
        functools.partial(_expert_kernel, t_total=t_total, t_part=t_part),
        grid_spec=pltpu.PrefetchScalarGridSpec(
            num_scalar_prefetch=3,
            grid=(N_PARTS, n_exp_tiles),
            in_specs=[smem_blk(cur_blk), smem_blk(nxt_blk), smem_blk(cur_blk),
                      share_blk(0), share_blk(1),
                      pl.BlockSpec(memory_space=pl.ANY),
                      pl.BlockSpec((1, D_MODEL, D_EXPERT), w_blk),
                      pl.BlockSpec((1, D_MODEL, D_EXPERT), w_blk),
                      pl.BlockSpec((1, D_EXPERT, D_MODEL), w_blk)],
            out_specs=pl.BlockSpec(memory_space=pl.ANY),
            scratch_shapes=[row_buf, row_buf, row_buf, row_buf, pltpu.VMEM((1, p_rows), i32),
                            pltpu.SMEM((1, p_rows), i32), pltpu.SemaphoreType.DMA((2,)),
                            pltpu.VMEM((t_part * ROW_TILES, LANES), f32), pltpu.SemaphoreType.DMA(())],
        ),
        out_shape=jax.ShapeDtypeStruct((TOP_K * t_total, ROW_TILES, LANES), f32),
        compiler_params=pltpu.CompilerParams(dimension_semantics=("arbitrary", "arbitrary"),
                                             vmem_limit_bytes=EXPERT_VMEM_LIMIT),
        name="moe_experts",
    )(tile_e, n_valid, tile_rows, row_src, row_src, row_dst, dest_last, dest_last, h2, exp_w1[0], exp_w3[0], exp_w2[0])
    out_tok = out_tok.reshape(TOP_K * plane_rows, LANES)

    assert t_prompt % TC == 0 and t_sample % TC == 0
    n_c_prompt = t_prompt // TC
    y_prompt, y_sample = pl.pallas_call(
        functools.partial(_combine_kernel, n_prompt_tiles=n_c_prompt),
        grid=(t_total // TC,),
        in_specs=[pl.BlockSpec((TC, D_MODEL), lambda i: (i, 0)),
                  pl.BlockSpec((TC, LANES), lambda i: (i, 0)),
                  _const_spec((1, D_MODEL)),
                  pl.BlockSpec((TC * ROW_TILES, LANES), lambda i: (i, 0)),
                  pl.BlockSpec((TC * ROW_TILES, LANES), lambda i: (t_total // TC + i, 0))],
        out_specs=[pl.BlockSpec((TC, D_MODEL), lambda i: (jnp.minimum(i, n_c_prompt - 1), 0)),
                   pl.BlockSpec((TC // dec_seq, dec_seq, D_MODEL), lambda i: (jnp.maximum(i - n_c_prompt, 0), 0, 0))],
        out_shape=[jax.ShapeDtypeStruct((t_prompt, D_MODEL), f32),
                   jax.ShapeDtypeStruct((dec_batch, dec_seq, D_MODEL), f32)],
        compiler_params=cparams,
        name="moe_combine",
    )(x1, gcol, nfg, out_tok, out_tok)

    y_prompt = y_prompt.reshape(n_batch, seq, D_MODEL)
    pool_state_prompt = pstate[None, :, CARRY_ROWS - POOL_STATE:, :]
    pool_state_sample = jnp.concatenate([state_pool[0], p_rows_sample], axis=1)[None, :, -POOL_STATE:, :]
    chunk_v_sample = v_rows_sample[None]
    return (y_prompt, y_sample, pool_state_prompt, pool_state_sample, chunk_v_sample)
```

```python
import functools
import math

import jax
import jax.numpy as jnp
from jax import lax
from jax.experimental import pallas as pl
from jax.experimental.pallas import tpu as pltpu

D_MODEL = 1024
A_WIDTH = 512
B_WIDTH = 512
N_HEADS = 4
HEAD_DIM = 128
CHUNK = 128
POOL_WINDOWS = (2, 4, 8, 16)
POOL_STATE = 15
N_GROUPS = 4
EXPERTS_PER_GROUP = 8
N_EXPERTS = 32
TOP_K = 2
D_EXPERT = 512
EPS = 1e-6

SUBLANES = 8
LANES = 128
ROW_TILES = D_MODEL // LANES

TM = 256
TC = 1024
TE = 512
SEQ_BLK = 32
N_ROUTER_ROWS = 40
CARRY_ROWS = 16
DMA_UNROLL = 32
VMEM_LIMIT = 48 * 1024 * 1024
N_PARTS = 2
EXPERT_VMEM_LIMIT = 62 * 1024 * 1024

_INV_SQRT2 = 1.0 / math.sqrt(2.0)


def _rmsnorm(x, g):
    r = lax.rsqrt(jnp.mean(x * x, axis=-1, keepdims=True) + EPS)
    return (x * r) * g


def _gelu(x):
    return 0.5 * x * (1.0 + lax.erf(x * _INV_SQRT2))


def _layernorm(x, g, b):
    mu = jnp.mean(x, axis=-1, keepdims=True)
    xc = x - mu
    var = jnp.mean(xc * xc, axis=-1, keepdims=True)
    return (xc * lax.rsqrt(var + EPS)) * g + b


def _row_slab(ref, s, n):
    return ref[pl.ds(s, n, stride=ROW_TILES), :]


def _pool_project(pooled, g, bw_ref, bscale_ref):
    lo, hi = g * HEAD_DIM, (g + 1) * HEAD_DIM
    hb = jnp.dot(pooled.astype(jnp.bfloat16), bw_ref[g], preferred_element_type=jnp.float32)
    return hb * bscale_ref[:, lo:hi]


def _prompt_mixers(j, u, v, p, aws_ref, abias_ref, bw_ref, bscale_ref, mix_ref, pcarry_ref, pstate_ref):
    tri = (lax.broadcasted_iota(jnp.int32, (CHUNK, CHUNK), 0)
           >= lax.broadcasted_iota(jnp.int32, (CHUNK, CHUNK), 1))
    vb = v.astype(jnp.bfloat16)
    for hd in range(N_HEADS):
        lo, hi = hd * HEAD_DIM, (hd + 1) * HEAD_DIM
        w = jnp.where(tri, aws_ref[hd], 0.0).astype(jnp.bfloat16)
        for c in range(TM // CHUNK):
            r0, r1 = c * CHUNK, (c + 1) * CHUNK
            z = jnp.dot(w, vb[r0:r1, lo:hi], preferred_element_type=jnp.float32) + abias_ref[:, lo:hi]
            mix_ref[r0:r1, lo:hi] = (u[r0:r1, lo:hi] * z).astype(jnp.bfloat16)

    head_pos = j * TM + lax.broadcasted_iota(jnp.int32, (CARRY_ROWS, LANES), 0)
    for g, w in enumerate(POOL_WINDOWS):
        lo, hi = g * HEAD_DIM, (g + 1) * HEAD_DIM
        pg = p[:, lo:hi]
        acc = jnp.concatenate([pcarry_ref[:, lo:hi], pg], axis=0)
        shift = 1
        while shift < w:
            acc = acc + pltpu.roll(acc, shift, 0)
            shift *= 2
        head = acc[CARRY_ROWS:2 * CARRY_ROWS, :] / jnp.minimum(head_pos + 1, w).astype(jnp.float32)
        mean = jnp.concatenate([head, acc[2 * CARRY_ROWS:, :] * (1.0 / w)], axis=0)
        pooled = mean - pg
        mix_ref[:, A_WIDTH + lo:A_WIDTH + hi] = _pool_project(pooled, g, bw_ref, bscale_ref).astype(jnp.bfloat16)
    tail = p[TM - CARRY_ROWS:, :]
    pcarry_ref[...] = tail
    pstate_ref[0] = tail


def _position_major(ref):
    n_pos = TM // SEQ_BLK
    rows = ref.reshape(SEQ_BLK, n_pos, ref.shape[-1])
    return jnp.concatenate([rows[:, q, :] for q in range(n_pos)], axis=0)


def _sample_mixers(u, v, p, aws_ref, abs_ref, st_ref, bw_ref, bscale_ref, mix_ref):
    n_pos = TM // SEQ_BLK

    for hd in range(N_HEADS):
        lo, hi = hd * HEAD_DIM, (hd + 1) * HEAD_DIM
        vplanes = [v[s * SEQ_BLK:(s + 1) * SEQ_BLK, lo:hi] for s in range(n_pos)]
        for i in range(n_pos):
            z = vplanes[0] * aws_ref[(hd * n_pos + i) * n_pos]
            for s in range(1, i + 1):
                z = z + vplanes[s] * aws_ref[(hd * n_pos + i) * n_pos + s]
            z = z + abs_ref[hd * n_pos + i]
            r0, r1 = i * SEQ_BLK, (i + 1) * SEQ_BLK
            mix_ref[r0:r1, lo:hi] = (u[r0:r1, lo:hi] * z).astype(jnp.bfloat16)

    for g, w in enumerate(POOL_WINDOWS):
        lo, hi = g * HEAD_DIM, (g + 1) * HEAD_DIM
        planes = [st_ref[0, k, :, lo:hi] for k in range(POOL_STATE)]
        planes += [p[i * SEQ_BLK:(i + 1) * SEQ_BLK, lo:hi] for i in range(n_pos)]
        pooled = []
        for i in range(n_pos):
            top = POOL_STATE + i
            s = planes[top - w + 1]
            for k in range(top - w + 2, top + 1):
                s = s + planes[k]
            pooled.append(s * (1.0 / w) - planes[top])
        pooled = jnp.concatenate(pooled, axis=0)
        mix_ref[:, A_WIDTH + lo:A_WIDTH + hi] = _pool_project(pooled, g, bw_ref, bscale_ref).astype(jnp.bfloat16)


def _route(h2, rwt_ref, rb_ref, su_ref, cnt_ref, meta_ref, gcol_ref):
    tm = h2.shape[0]
    h_hi = h2.astype(jnp.bfloat16)
    h_lo = (h2 - h_hi.astype(jnp.float32)).astype(jnp.bfloat16)
    s = (jnp.dot(h_hi, rwt_ref[...], preferred_element_type=jnp.float32)
         + jnp.dot(h_lo, rwt_ref[...], preferred_element_type=jnp.float32))
    st = s.T
    lt = st[0:N_ROUTER_ROWS, :] + st[N_ROUTER_ROWS:2 * N_ROUTER_ROWS, :] + rb_ref[...]
    row = lambda i: lt[i:i + 1, :]
    l1 = [row(i) for i in range(N_GROUPS)]
    m1 = jnp.maximum(jnp.maximum(l1[0], l1[1]), jnp.maximum(l1[2], l1[3]))
    grp = jnp.where(l1[0] == m1, 0, jnp.where(l1[1] == m1, 1, jnp.where(l1[2] == m1, 2, 3)))
    se = (jnp.exp(l1[0] - m1) + jnp.exp(l1[1] - m1)) + (jnp.exp(l1[2] - m1) + jnp.exp(l1[3] - m1))
    pg = 1.0 / se
    l2 = []
    for e in range(EXPERTS_PER_GROUP):
        c = [row(N_GROUPS + g * EXPERTS_PER_GROUP + e) for g in range(N_GROUPS)]
        l2.append(jnp.where(grp == 0, c[0], jnp.where(grp == 1, c[1], jnp.where(grp == 2, c[2], c[3]))))
    v0 = functools.reduce(jnp.maximum, l2)
    i0 = jnp.full_like(grp, EXPERTS_PER_GROUP - 1)
    for e in range(EXPERTS_PER_GROUP - 2, -1, -1):
        i0 = jnp.where(l2[e] == v0, e, i0)
    neg = jnp.float32(-jnp.inf)
    l2m = [jnp.where(i0 == e, neg, l2[e]) for e in range(EXPERTS_PER_GROUP)]
    v1 = functools.reduce(jnp.maximum, l2m)
    i1 = jnp.full_like(grp, EXPERTS_PER_GROUP - 1)
    for e in range(EXPERTS_PER_GROUP - 2, -1, -1):
        i1 = jnp.where((l2m[e] == v1) & (i0 != e), e, i1)
    d = jnp.exp(v1 - v0)
    g0 = pg / (1.0 + d)
    g1 = (pg * d) / (1.0 + d)
    e0 = grp * EXPERTS_PER_GROUP + i0
    e1 = grp * EXPERTS_PER_GROUP + i1

    eiota = lax.broadcasted_iota(jnp.int32, (N_EXPERTS, tm), 0)
    hit0 = eiota == e0
    hit1 = eiota == e1
    onehot = (hit0 | hit1).astype(jnp.bfloat16)
    prefix = jnp.dot(onehot, su_ref[...], preferred_element_type=jnp.float32)
    carry = cnt_ref[...]
    base = prefix + jnp.concatenate([carry] * (tm // LANES), axis=1)
    r0 = jnp.sum(jnp.where(hit0, base, 0.0), axis=0, keepdims=True)
    r1 = jnp.sum(jnp.where(hit1, base, 0.0), axis=0, keepdims=True)
    ones = jnp.ones((tm, LANES), jnp.bfloat16)
    cnt_ref[...] = carry + jnp.dot(onehot, ones, preferred_element_type=jnp.float32)

    meta_ref[0:1, :] = e0
    meta_ref[1:2, :] = e1
    meta_ref[2:3, :] = r0.astype(jnp.int32)
    meta_ref[3:4, :] = r1.astype(jnp.int32)
    meta_ref[4:8, :] = jnp.zeros((4, tm), jnp.int32)

    giota = lax.broadcasted_iota(jnp.int32, (LANES, tm), 0)
    gpad = jnp.where(giota == 0, g0, jnp.where(giota == 1, g1, 0.0))
    gcol_ref[...] = gpad.T


def _mixer_kernel(aws_s_ref, abs_s_ref,
                  x0_ref, xpn_ref, xsn_ref,
                  st_ref, n1g_ref, win_ref, ang_ref, anb_ref, aws_ref, abias_ref, bw_ref, bscale_ref,
                  wout_ref, n2g_ref, rwt_ref, rb_ref, su_ref,
                  x1_ref, h2_ref, meta_ref, gcol_ref, counts_ref, pstate_ref, pp_ref, vp_ref, asg_ref,
                  mix_ref, pcarry_ref, cnt_ref, x_ref, u_ref, v_ref, p_ref, part_meta, dest_vmem, fill_vmem, *dest_smem,
                  n_prompt_tiles, seq_tiles, part_tiles, t_total):
    i = pl.program_id(0)
    is_prompt = i < n_prompt_tiles
    j = i % seq_tiles
    part = i // part_tiles
    tile_in_part = i % part_tiles

    @pl.when((i > 0) & (tile_in_part == 0))
    def _():
        _, _, pad_start = _segment_offsets(cnt_ref[...])
        _stage_inversion(part_meta, pad_start, dest_vmem, dest_smem, fill_vmem, asg_ref)

    def in_proj(x):
        h = _rmsnorm(x, n1g_ref[...]).astype(jnp.bfloat16)
        proj = jnp.dot(h, win_ref[...], preferred_element_type=jnp.float32)
        uv = _gelu(proj[:, : 2 * A_WIDTH])
        u_ref[...] = uv[:, :A_WIDTH]
        v_ref[...] = _layernorm(uv[:, A_WIDTH:], ang_ref[...], anb_ref[...])
        p_ref[...] = proj[:, 2 * A_WIDTH:]
        x_ref[...] = x

    @pl.when(i == 0)
    def _():
        in_proj(x0_ref[0])

    @pl.when(i % part_tiles == 0)
    def _():
        cnt_ref[...] = jnp.zeros_like(cnt_ref)

    @pl.when(is_prompt & (j == 0))
    def _():
        pcarry_ref[...] = jnp.zeros_like(pcarry_ref)

    def finish_tile(x, invert_previous_part):
        x1 = x + jnp.dot(mix_ref[...], wout_ref[...], preferred_element_type=jnp.float32)
        x1_ref[...] = x1
        h2 = _rmsnorm(x1, n2g_ref[...])
        for s in range(ROW_TILES):
            h2_ref[pl.ds(s, TM, stride=ROW_TILES), :] = h2[:, s * LANES:(s + 1) * LANES]
        _route(h2, rwt_ref, rb_ref, su_ref, cnt_ref, meta_ref, gcol_ref)
        counts_ref[0] = cnt_ref[...]
        part_meta[:, pl.ds(pl.multiple_of(tile_in_part * TM, TM), TM)] = meta_ref[0:2 * TOP_K, :]
        if invert_previous_part:
            first_tok = tile_in_part * TM
            for k, dsm in enumerate(dest_smem):
                first_asg = k * t_total + (part - 1) * (part_tiles * TM) + first_tok
                for t in range(TM):
                    asg_ref[0, 0, dsm[0, first_tok + t]] = first_asg + t
        nxt = jnp.minimum(i + 1, pl.num_programs(0) - 1)
        in_proj(jnp.where(nxt < n_prompt_tiles, xpn_ref[0], xsn_ref[...].reshape(TM, D_MODEL)))

    @pl.when(is_prompt & (part == 0))
    def _():
        _prompt_mixers(j, u_ref[...], v_ref[...], p_ref[...], aws_ref, abias_ref, bw_ref, bscale_ref, mix_ref,
                       pcarry_ref, pstate_ref)
        finish_tile(x_ref[...], False)

    @pl.when(is_prompt & (part > 0))
    def _():
        _prompt_mixers(j, u_ref[...], v_ref[...], p_ref[...], aws_ref, abias_ref, bw_ref, bscale_ref, mix_ref,
                       pcarry_ref, pstate_ref)
        finish_tile(x_ref[...], True)

    @pl.when(jnp.logical_not(is_prompt))
    def _():
        vp_ref[...] = v_ref[...].reshape(vp_ref.shape)
        pp_ref[...] = p_ref[...].reshape(pp_ref.shape)
        _sample_mixers(_position_major(u_ref), _position_major(v_ref), _position_major(p_ref), aws_s_ref, abs_s_ref,
                       st_ref, bw_ref, bscale_ref, mix_ref)
        finish_tile(_position_major(x_ref), True)


def _segment_offsets(cnt):
    padded = jnp.floor((cnt + (TE - 1)) * (1.0 / TE)) * TE
    sub = lax.broadcasted_iota(jnp.int32, cnt.shape, 0)
    pad_end = padded
    shift = 1
    while shift < N_EXPERTS:
        pad_end = pad_end + jnp.where(sub >= shift, pltpu.roll(pad_end, shift, 0), 0.0)
        shift *= 2
    return padded, pad_end, pad_end - padded


def _dest_rows(meta_ref, pad_start, dest_ref):
    t_part = meta_ref.shape[1]
    base = pad_start.astype(jnp.int32)
    for k in range(TOP_K):
        e = meta_ref[k:k + 1, :]
        d = meta_ref[TOP_K + k:TOP_K + k + 1, :]
        for ex in range(N_EXPERTS):
            row = jnp.concatenate([base[ex:ex + 1, :]] * (t_part // LANES), axis=1)
            d = d + jnp.where(e == ex, row, 0)
        dest_ref[k] = d


def _stage_inversion(meta_ref, pad_start, dest_vmem, dest_smem, fill_vmem, asg_ref):
    _dest_rows(meta_ref, pad_start, dest_vmem)
    for k, dsm in enumerate(dest_smem):
        pltpu.sync_copy(dest_vmem.at[k], dsm)
    fill_vmem[...] = jnp.full(fill_vmem.shape, -1, jnp.int32)
    pltpu.sync_copy(fill_vmem, asg_ref)


def _tables_kernel(counts_ref, meta_ref, dest_ref, tab_ref):
    h = pl.program_id(0)
    cnt = counts_ref[0]
    sub = lax.broadcasted_iota(jnp.int32, cnt.shape, 0)
    padded, pad_end, pad_start = _segment_offsets(cnt)

    @pl.when(h == pl.num_programs(0) - 1)
    def _():
        _dest_rows(meta_ref, pad_start, dest_ref)

    n_valid = jnp.maximum(pad_end[N_EXPERTS - 1:, :] * (1.0 / TE), 1.0)
    tile = jnp.minimum(lax.broadcasted_iota(jnp.int32, (1, LANES), 1).astype(jnp.float32), n_valid - 1.0)
    tile_start = tile * TE
    tile_e = jnp.minimum(jnp.sum((pad_end <= tile_start).astype(jnp.float32), axis=0, keepdims=True),
                         N_EXPERTS - 1.0)
    is_e = sub.astype(jnp.float32) == tile_e
    seg_start = jnp.sum(jnp.where(is_e, pad_start, 0.0), axis=0, keepdims=True)
    seg_cnt = jnp.sum(jnp.where(is_e, cnt, 0.0), axis=0, keepdims=True)
    rows = jnp.clip(seg_cnt - (tile_start - seg_start), 0.0, TE)
    tab_ref[0, 0:1, :] = tile_e.astype(jnp.int32)
    tab_ref[0, 1:2, :] = rows.astype(jnp.int32)
    tab_ref[0, 2:3, :] = n_valid.astype(jnp.int32)
    tab_ref[0, 3:, :] = jnp.zeros((SUBLANES - 3, LANES), jnp.int32)


def _expert_kernel(te_ref, nv_ref, nrows_ref,
                   src_ref, src_next_ref, dst_ref,
                   dest_k0_ref, dest_k1_ref,
                   h2_hbm, w1_ref, w3_ref, w2_ref,
                   out_hbm,
                   xt0, xt1, ob0, ob1, fill_vmem, asg_last, ssem, h2_ref, hsem, *, t_total, t_part):
    h = pl.program_id(0)
    i = pl.program_id(1)
    g = h * pl.num_programs(1) + i
    nv = nv_ref[h]
    slab = lambda r: pl.ds(pl.multiple_of(r * ROW_TILES, ROW_TILES), ROW_TILES)
    toks_per_tile = TE // TOP_K
    inv_tiles = t_part // toks_per_tile
    part_slabs = t_part * ROW_TILES
    fetch_part = lambda part: pltpu.make_async_copy(h2_hbm.at[pl.ds(part * part_slabs, part_slabs), :], h2_ref, hsem)

    def last_part_src(tile):
        def src(r):
            a = asg_last[0, tile * TE + r]
            return a - jnp.where(a >= t_total, t_total + t_part, t_part)
        return src

    first_tables = (lambda r: src_ref[0, 0, r], lambda r: src_next_ref[0, 0, r],
                    lambda first, uu: dst_ref[0, 0, first + uu])
    last_tables = (last_part_src(i), last_part_src(jnp.minimum(i + 1, nv - 1)),
                   lambda first, uu: asg_last[0, (i * TE + first) + uu])

    def gather(src, xt):
        for r in range(TE):
            xt[r * ROW_TILES:(r + 1) * ROW_TILES, :] = h2_ref[slab(src(r)), :]

    def invert_share():
        first_tok = i * toks_per_tile
        for k, dest_ref in enumerate((dest_k0_ref, dest_k1_ref)):
            first_asg = k * t_total + t_part + first_tok
            for t in range(toks_per_tile):
                asg_last[0, dest_ref[0, 0, t]] = first_asg + t

    def wait_scatter(tile, ob, other, sem):
        n = nrows_ref[tile] * ROW_TILES
        pltpu.make_async_copy(other.at[pl.ds(0, n), :], ob.at[pl.ds(0, n), :], sem).wait()

    def tile_body(xt, xt_next, ob, ob_other, sem, sem_other, tables, first_part, inverts):
        _, src_next, dst = tables

        @pl.when(i >= 2)
        def _():
            wait_scatter(g - 2, ob, ob_other, sem)

        def compute(m):
            gather(src_next, xt_next)
            if inverts:
                invert_share()
            x = jnp.concatenate([_row_slab(xt, s, m) for s in range(ROW_TILES)], axis=1).astype(jnp.bfloat16)
            a = jnp.dot(x, w1_ref[0].astype(jnp.bfloat16), preferred_element_type=jnp.float32)
            b = jnp.dot(x, w3_ref[0].astype(jnp.bfloat16), preferred_element_type=jnp.float32)
            hmid = (a * jax.nn.sigmoid(a)) * b
            o = jnp.dot(hmid.astype(jnp.bfloat16), w2_ref[0].astype(jnp.bfloat16),
                        preferred_element_type=jnp.float32)
            for s in range(ROW_TILES):
                ob[pl.ds(s, m, stride=ROW_TILES), :] = o[:, s * LANES:(s + 1) * LANES]

        n = nrows_ref[g]

        @pl.when(n > TE // 2)
        def _():
            compute(TE)

        @pl.when(n <= TE // 2)
        def _():
            compute(TE // 2)

        copy = lambda first, uu: pltpu.make_async_copy(ob.at[slab(first + uu), :], out_hbm.at[dst(first, uu)], sem)

        def issue_groups(first_row, n_groups, group):
            def body(rb, carry):
                for uu in range(group):
                    copy(first_row + rb * group, uu).start(priority=uu % 2)
                return carry
            lax.fori_loop(0, n_groups, body, 0)

        n_full = n // DMA_UNROLL
        n_small = n // SUBLANES - n_full * (DMA_UNROLL // SUBLANES)
        issue_groups(0, n_full, DMA_UNROLL)
        issue_groups(n_full * DMA_UNROLL, n_small, SUBLANES)

        def tail(r, carry):
            copy(r, 0).start(priority=1)
            return carry
        lax.fori_loop((n // SUBLANES) * SUBLANES, n, tail, 0)

        @pl.when(i == nv - 1)
        def _():
            @pl.when(i >= 1)
            def _():
                wait_scatter(g - 1, ob_other, ob, sem_other)
            wait_scatter(g, ob, ob_other, sem)
            if first_part:
                fetch_part(1).start()

    def part_body(tables, first_part, inverts):
        @pl.when(i == 0)
        def _():
            if not first_part:
                fetch_part(1).wait()
            gather(tables[0], xt0)

        @pl.when(i % 2 == 0)
        def _():
            tile_body(xt0, xt1, ob0, ob1, ssem.at[0], ssem.at[1], tables, first_part, inverts)

        @pl.when(i % 2 == 1)
        def _():
            tile_body(xt1, xt0, ob1, ob0, ssem.at[1], ssem.at[0], tables, first_part, inverts)

    @pl.when((h == 0) & (i == 0))
    def _():
        fetch_part(0).start()
        fill_vmem[...] = jnp.full(fill_vmem.shape, t_part, jnp.int32)
        pltpu.sync_copy(fill_vmem, asg_last)
        fetch_part(0).wait()

    @pl.when((i < inv_tiles) & (h == 0))
    def _():
        part_body(first_tables, True, True)

    @pl.when((i >= inv_tiles) & (i < nv) & (h == 0))
    def _():
        part_body(first_tables, True, False)

    @pl.when((i < nv) & (h > 0))
    def _():
        part_body(last_tables, False, False)


def _combine_kernel(x1_ref, gcol_ref, nfg_ref, o0_ref, o1_ref, yp_ref, ys_ref, *, n_prompt_tiles):
    i = pl.program_id(0)
    o0 = jnp.concatenate([_row_slab(o0_ref, s, TC) for s in range(ROW_TILES)], axis=1)
    o1 = jnp.concatenate([_row_slab(o1_ref, s, TC) for s in range(ROW_TILES)], axis=1)
    g = gcol_ref[...]
    moe = g[:, 0:1] * o0 + g[:, 1:2] * o1
    y = _rmsnorm(x1_ref[...] + moe, nfg_ref[...])

    @pl.when(i < n_prompt_tiles)
    def _():
        yp_ref[...] = y

    @pl.when(i >= n_prompt_tiles)
    def _():
        for blk in range(TC // TM):
            for q in range(TM // SEQ_BLK):
                r0 = blk * TM + q * SEQ_BLK
                ys_ref[blk * SEQ_BLK:(blk + 1) * SEQ_BLK, q, :] = y[r0:r0 + SEQ_BLK, :]


def _const_spec(shape):
    return pl.BlockSpec(shape, lambda *_: (0,) * len(shape))


def kernel(x_prompt, x_sample, state_pool, norm1_g, w_in, a_norm_g, a_norm_b, a_ws, a_bs, b_w, b_scale, w_out,
           norm2_g, r1_w, r1_b, r2_w, r2_b, exp_w1, exp_w3, exp_w2, normf_g):
    f32, bf16, i32 = jnp.float32, jnp.bfloat16, jnp.int32
    n_batch, seq, _ = x_prompt.shape
    dec_batch, dec_seq, _ = x_sample.shape
    assert norm1_g.shape[0] == 1 and seq % TM == 0 and TM % CHUNK == 0
    assert dec_seq * SEQ_BLK == TM and dec_batch % SEQ_BLK == 0 and dec_seq <= CHUNK
    t_prompt = n_batch * seq
    t_sample = dec_batch * dec_seq
    t_total = t_prompt + t_sample
    n_tok_tiles = t_total // TM
    n_prompt_tiles = t_prompt // TM
    n_sample_tiles = t_sample // TM
    seq_tiles = seq // TM
    plane_rows = t_total * ROW_TILES
    assert n_tok_tiles % N_PARTS == 0
    part_tiles = n_tok_tiles // N_PARTS
    t_part = part_tiles * TM
    n_exp_tiles = -(-(TOP_K * t_part + N_EXPERTS * (TE - 1)) // TE)
    p_rows = n_exp_tiles * TE

    n1g = norm1_g[0][None, :]
    n2g = norm2_g[0][None, :]
    nfg = normf_g[None, :]
    win = w_in[0].astype(bf16)
    wout = w_out[0].astype(bf16)
    ang = a_norm_g[0][None, :]
    anb = a_norm_b[0][None, :]
    bw = b_w[0].astype(bf16)
    bscale = b_scale[0][None, :]
    abias = jnp.repeat(a_bs[0][:, :CHUNK].T, HEAD_DIM, axis=1)
    rw = jnp.concatenate([r1_w[0], r2_w[0].transpose(1, 0, 2).reshape(D_MODEL, N_EXPERTS),
                          jnp.zeros((D_MODEL, N_ROUTER_ROWS - N_GROUPS - N_EXPERTS), f32)], axis=1)
    rw_hi = rw.astype(bf16)
    rw_lo = (rw - rw_hi.astype(f32)).astype(bf16)
    rwt = jnp.concatenate([rw_hi, rw_lo, jnp.zeros((D_MODEL, LANES - 2 * N_ROUTER_ROWS), bf16)], axis=1)
    rbias = jnp.concatenate([r1_b[0], r2_b[0].reshape(-1),
                             jnp.zeros((N_ROUTER_ROWS - N_GROUPS - N_EXPERTS,), f32)])
    rb = jnp.broadcast_to(rbias[:, None], (N_ROUTER_ROWS, TM))
    su = (jnp.arange(TM)[:, None] < jnp.arange(TM)[None, :]).astype(bf16)
    st_planes = state_pool[0].reshape(n_sample_tiles, SEQ_BLK, POOL_STATE, B_WIDTH).transpose(0, 2, 1, 3)
    aws_s = a_ws[0][:, :dec_seq, :dec_seq].reshape(-1)
    abs_s = a_bs[0][:, :dec_seq].reshape(-1)

    cparams = pltpu.CompilerParams(dimension_semantics=("arbitrary",), vmem_limit_bytes=VMEM_LIMIT)

    p_tile = lambda i: jnp.minimum(i, n_prompt_tiles - 1)
    s_tile = lambda i: jnp.maximum(i - n_prompt_tiles, 0)
    sample_rows_shape = jax.ShapeDtypeStruct((dec_batch, dec_seq, A_WIDTH), f32)
    assert N_PARTS >= 2 and n_prompt_tiles >= part_tiles and t_part % TM == 0
    x1, h2, meta, gcol, counts, pstate, p_rows_sample, v_rows_sample, asg_head = pl.pallas_call(
        functools.partial(_mixer_kernel, n_prompt_tiles=n_prompt_tiles, seq_tiles=seq_tiles, part_tiles=part_tiles,
                          t_total=t_total),
        grid_spec=pltpu.PrefetchScalarGridSpec(
            num_scalar_prefetch=2,
            grid=(n_tok_tiles,),
            in_specs=[
                pl.BlockSpec((1, TM, D_MODEL), lambda i, *_: (0, 0, 0)),
                pl.BlockSpec((1, TM, D_MODEL),
                             lambda i, *_: (p_tile(i + 1) // seq_tiles, p_tile(i + 1) % seq_tiles, 0)),
                pl.BlockSpec((SEQ_BLK, dec_seq, D_MODEL),
                             lambda i, *_: (jnp.minimum(s_tile(i + 1), n_sample_tiles - 1), 0, 0)),
                pl.BlockSpec((1, POOL_STATE, SEQ_BLK, B_WIDTH), lambda i, *_: (s_tile(i), 0, 0, 0)),
                _const_spec((1, D_MODEL)), _const_spec((D_MODEL, 3 * A_WIDTH)), _const_spec((1, A_WIDTH)),
                _const_spec((1, A_WIDTH)), _const_spec((N_HEADS, CHUNK, CHUNK)), _const_spec((CHUNK, A_WIDTH)),
                _const_spec((N_GROUPS, HEAD_DIM, HEAD_DIM)), _const_spec((1, B_WIDTH)),
                _const_spec((D_MODEL, D_MODEL)), _const_spec((1, D_MODEL)), _const_spec((D_MODEL, LANES)),
                _const_spec((N_ROUTER_ROWS, TM)), _const_spec((TM, TM)),
            ],
            out_specs=[
                pl.BlockSpec((TM, D_MODEL), lambda i, *_: (i, 0)),
                pl.BlockSpec((TM * ROW_TILES, LANES), lambda i, *_: (i, 0)),
                pl.BlockSpec((SUBLANES, TM), lambda i, *_: (0, i)),
                pl.BlockSpec((TM, LANES), lambda i, *_: (i, 0)),
                pl.BlockSpec((1, N_EXPERTS, LANES), lambda i, *_: (i // part_tiles, 0, 0)),
                pl.BlockSpec((1, CARRY_ROWS, B_WIDTH), lambda i, *_: (p_tile(i) // seq_tiles, 0, 0)),
                pl.BlockSpec((SEQ_BLK, dec_seq, A_WIDTH), lambda i, *_: (s_tile(i), 0, 0)),
                pl.BlockSpec((SEQ_BLK, dec_seq, A_WIDTH), lambda i, *_: (s_tile(i), 0, 0)),
                pl.BlockSpec((1, 1, p_rows), lambda i, *_: (jnp.maximum(i // part_tiles - 1, 0), 0, 0),
                             memory_space=pltpu.SMEM),
            ],
            scratch_shapes=[pltpu.VMEM((TM, D_MODEL), bf16), pltpu.VMEM((CARRY_ROWS, B_WIDTH), f32),
                            pltpu.VMEM((N_EXPERTS, LANES), f32), pltpu.VMEM((TM, D_MODEL), f32),
                            pltpu.VMEM((TM, A_WIDTH), f32), pltpu.VMEM((TM, A_WIDTH), f32),
                            pltpu.VMEM((TM, B_WIDTH), f32),
                            pltpu.VMEM((2 * TOP_K, t_part), i32), pltpu.VMEM((TOP_K, 1, t_part), i32),
                            pltpu.VMEM((1, 1, p_rows), i32)] + [pltpu.SMEM((1, t_part), i32)] * TOP_K,
        ),
        out_shape=[
            jax.ShapeDtypeStruct((t_total, D_MODEL), f32),
            jax.ShapeDtypeStruct((plane_rows, LANES), f32),
            jax.ShapeDtypeStruct((SUBLANES, t_total), i32),
            jax.ShapeDtypeStruct((t_total, LANES), f32),
            jax.ShapeDtypeStruct((N_PARTS, N_EXPERTS, LANES), f32),
            jax.ShapeDtypeStruct((n_batch, CARRY_ROWS, B_WIDTH), f32),
            sample_rows_shape, sample_rows_shape,
            jax.ShapeDtypeStruct((N_PARTS - 1, 1, p_rows), i32),
        ],
        compiler_params=cparams,
        name="mixer",
    )(aws_s, abs_s, x_prompt, x_prompt, x_sample, st_planes, n1g, win, ang, anb, a_ws[0][:, :CHUNK, :CHUNK], abias,
      bw, bscale, wout, n2g, rwt, rb, su)

    assert n_exp_tiles <= LANES and t_part % LANES == 0
    dest_last, tab = pl.pallas_call(
        _tables_kernel,
        grid=(N_PARTS,),
        in_specs=[pl.BlockSpec((1, N_EXPERTS, LANES), lambda h: (h, 0, 0)),
                  pl.BlockSpec((SUBLANES, t_part), lambda h: (0, h))],
        out_specs=[_const_spec((TOP_K, 1, t_part)),
                   pl.BlockSpec((1, SUBLANES, LANES), lambda h: (h, 0, 0))],
        out_shape=[jax.ShapeDtypeStruct((TOP_K, 1, t_part), i32),
                   jax.ShapeDtypeStruct((N_PARTS, SUBLANES, LANES), i32)],
        compiler_params=cparams,
        name="route_tables",
    )(counts, meta)
    tile_e = tab[:, 0, :n_exp_tiles].reshape(-1)
    tile_rows = tab[:, 1, :n_exp_tiles].reshape(-1)
    n_valid = tab[:, 2, 0]
    assert N_PARTS == 2 and (TOP_K * t_part) % TE == 0
    asg = asg_head.reshape(-1)
    row_tok = jnp.where(asg >= t_total, asg - t_total, asg)
    row_src = jnp.where(asg >= 0, row_tok, 0).reshape(n_exp_tiles, 1, TE)
    row_dst = jnp.maximum(asg, 0).reshape(n_exp_tiles, 1, TE)

    flat = lambda h, i: h * n_exp_tiles + i
    cur_blk = lambda h, i, te, nv, nr: (jnp.minimum(i, nv[0] - 1), 0, 0)
    nxt_blk = lambda h, i, te, nv, nr: (jnp.minimum(i + 1, nv[0] - 1), 0, 0)
    w_blk = lambda h, i, te, nv, nr: (te[flat(h, i)], 0, 0)
    smem_blk = lambda imap: pl.BlockSpec((1, 1, TE), imap, memory_space=pltpu.SMEM)
    last_share = t_part // (TE // TOP_K) - 1
    share_blk = lambda k: pl.BlockSpec(
        (1, 1, TE // TOP_K), lambda h, i, *_: (k, 0, jnp.where(h == 0, jnp.minimum(i, last_share), last_share)),
        memory_space=pltpu.SMEM)
    row_buf = pltpu.VMEM((TE * ROW_TILES, LANES), f32)
    out_tok = pl.pallas_call(
        functools.partial(_expert_kernel, t_total=t_total, t_part=t_part),
        grid_spec=pltpu.PrefetchScalarGridSpec(
            num_scalar_prefetch=3,
            grid=(N_PARTS, n_exp_tiles),
            in_specs=[smem_blk(cur_blk), smem_blk(nxt_blk), smem_blk(cur_blk),
                      share_blk(0), share_blk(1),
                      pl.BlockSpec(memory_space=pl.ANY),
                      pl.BlockSpec((1, D_MODEL, D_EXPERT), w_blk),
                      pl.BlockSpec((1, D_MODEL, D_EXPERT), w_blk),
                      pl.BlockSpec((1, D_EXPERT, D_MODEL), w_blk)],
            out_specs=pl.BlockSpec(memory_space=pl.ANY),
            scratch_shapes=[row_buf, row_buf, row_buf, row_buf, pltpu.VMEM((1, p_rows), i32),
                            pltpu.SMEM((1, p_rows), i32), pltpu.SemaphoreType.DMA((2,)),
                            pltpu.VMEM((t_part * ROW_TILES, LANES), f32), pltpu.SemaphoreType.DMA(())],
        ),
        out_shape=jax.ShapeDtypeStruct((TOP_K * t_total, ROW_TILES, LANES), f32),
        compiler_params=pltpu.CompilerParams(dimension_semantics=("arbitrary", "arbitrary"),
                                             vmem_limit_bytes=EXPERT_VMEM_LIMIT),
        name="moe_experts",
    )(tile_e, n_valid, tile_rows, row_src, row_src, row_dst, dest_last, dest_last, h2, exp_w1[0], exp_w3[0], exp_w2[0])
    out_tok = out_tok.reshape(TOP_K * plane_rows, LANES)

    assert t_prompt % TC == 0 and t_sample % TC == 0
    n_c_prompt = t_prompt // TC
    y_prompt, y_sample = pl.pallas_call(
        functools.partial(_combine_kernel, n_prompt_tiles=n_c_prompt),
        grid=(t_total // TC,),
        in_specs=[pl.BlockSpec((TC, D_MODEL), lambda i: (i, 0)),
                  pl.BlockSpec((TC, LANES), lambda i: (i, 0)),
                  _const_spec((1, D_MODEL)),
                  pl.BlockSpec((TC * ROW_TILES, LANES), lambda i: (i, 0)),
                  pl.BlockSpec((TC * ROW_TILES, LANES), lambda i: (t_total // TC + i, 0))],
        out_specs=[pl.BlockSpec((TC, D_MODEL), lambda i: (jnp.minimum(i, n_c_prompt - 1), 0)),
                   pl.BlockSpec((TC // dec_seq, dec_seq, D_MODEL), lambda i: (jnp.maximum(i - n_c_prompt, 0), 0, 0))],
        out_shape=[jax.ShapeDtypeStruct((t_prompt, D_MODEL), f32),
                   jax.ShapeDtypeStruct((dec_batch, dec_seq, D_MODEL), f32)],
        compiler_params=cparams,
        name="moe_combine",
    )(x1, gcol, nfg, out_tok, out_tok)

    y_prompt = y_prompt.reshape(n_batch, seq, D_MODEL)
    pool_state_prompt = pstate[None, :, CARRY_ROWS - POOL_STATE:, :]
    pool_state_sample = jnp.concatenate([state_pool[0], p_rows_sample], axis=1)[None, :, -POOL_STATE:, :]
    chunk_v_sample = v_rows_sample[None]
    return (y_prompt, y_sample, pool_state_prompt, pool_state_sample, chunk_v_sample)
```

```python
import functools
import math

import jax
import jax.numpy as jnp
from jax import lax
from jax.experimental import pallas as pl
from jax.experimental.pallas import tpu as pltpu

D_MODEL = 1024
A_WIDTH = 512
B_WIDTH = 512
N_HEADS = 4
HEAD_DIM = 128
CHUNK = 128
POOL_WINDOWS = (2, 4, 8, 16)
POOL_STATE = 15
N_GROUPS = 4
EXPERTS_PER_GROUP = 8
N_EXPERTS = 32
TOP_K = 2
D_EXPERT = 512
EPS = 1e-6

SUBLANES = 8
LANES = 128
ROW_TILES = D_MODEL // LANES

TM = 256
TC = 1024
TE = 512
SEQ_BLK = 32
N_ROUTER_ROWS = 40
CARRY_ROWS = 16
DMA_UNROLL = 32
VMEM_LIMIT = 48 * 1024 * 1024
N_PARTS = 2
EXPERT_VMEM_LIMIT = 62 * 1024 * 1024

_INV_SQRT2 = 1.0 / math.sqrt(2.0)


def _rmsnorm(x, g):
    r = lax.rsqrt(jnp.mean(x * x, axis=-1, keepdims=True) + EPS)
    return (x * r) * g


def _gelu(x):
    return 0.5 * x * (1.0 + lax.erf(x * _INV_SQRT2))


def _layernorm(x, g, b):
    mu = jnp.mean(x, axis=-1, keepdims=True)
    xc = x - mu
    var = jnp.mean(xc * xc, axis=-1, keepdims=True)
    return (xc * lax.rsqrt(var + EPS)) * g + b


def _row_slab(ref, s, n):
    return ref[pl.ds(s, n, stride=ROW_TILES), :]


def _pool_project(pooled, g, bw_ref, bscale_ref):
    lo, hi = g * HEAD_DIM, (g + 1) * HEAD_DIM
    hb = jnp.dot(pooled.astype(jnp.bfloat16), bw_ref[g], preferred_element_type=jnp.float32)
    return hb * bscale_ref[:, lo:hi]


def _prompt_mixers(j, u, v, p, aws_ref, abias_ref, bw_ref, bscale_ref, mix_ref, pcarry_ref, pstate_ref):
    tri = (lax.broadcasted_iota(jnp.int32, (CHUNK, CHUNK), 0)
           >= lax.broadcasted_iota(jnp.int32, (CHUNK, CHUNK), 1))
    vb = v.astype(jnp.bfloat16)
    for hd in range(N_HEADS):
        lo, hi = hd * HEAD_DIM, (hd + 1) * HEAD_DIM
        w = jnp.where(tri, aws_ref[hd], 0.0).astype(jnp.bfloat16)
        for c in range(TM // CHUNK):
            r0, r1 = c * CHUNK, (c + 1) * CHUNK
            z = jnp.dot(w, vb[r0:r1, lo:hi], preferred_element_type=jnp.float32) + abias_ref[:, lo:hi]
            mix_ref[r0:r1, lo:hi] = (u[r0:r1, lo:hi] * z).astype(jnp.bfloat16)

    head_pos = j * TM + lax.broadcasted_iota(jnp.int32, (CARRY_ROWS, LANES), 0)
    for g, w in enumerate(POOL_WINDOWS):
        lo, hi = g * HEAD_DIM, (g + 1) * HEAD_DIM
        pg = p[:, lo:hi]
        acc = jnp.concatenate([pcarry_ref[:, lo:hi], pg], axis=0)
        shift = 1
        while shift < w:
            acc = acc + pltpu.roll(acc, shift, 0)
            shift *= 2
        head = acc[CARRY_ROWS:2 * CARRY_ROWS, :] / jnp.minimum(head_pos + 1, w).astype(jnp.float32)
        mean = jnp.concatenate([head, acc[2 * CARRY_ROWS:, :] * (1.0 / w)], axis=0)
        pooled = mean - pg
        mix_ref[:, A_WIDTH + lo:A_WIDTH + hi] = _pool_project(pooled, g, bw_ref, bscale_ref).astype(jnp.bfloat16)
    tail = p[TM - CARRY_ROWS:, :]
    pcarry_ref[...] = tail
    pstate_ref[0] = tail


def _position_major(ref):
    n_pos = TM // SEQ_BLK
    rows = ref.reshape(SEQ_BLK, n_pos, ref.shape[-1])
    return jnp.concatenate([rows[:, q, :] for q in range(n_pos)], axis=0)


def _sample_mixers(u, v, p, aws_ref, abs_ref, st_ref, bw_ref, bscale_ref, mix_ref):
    n_pos = TM // SEQ_BLK

    for hd in range(N_HEADS):
        lo, hi = hd * HEAD_DIM, (hd + 1) * HEAD_DIM
        vplanes = [v[s * SEQ_BLK:(s + 1) * SEQ_BLK, lo:hi] for s in range(n_pos)]
        for i in range(n_pos):
            z = vplanes[0] * aws_ref[(hd * n_pos + i) * n_pos]
            for s in range(1, i + 1):
                z = z + vplanes[s] * aws_ref[(hd * n_pos + i) * n_pos + s]
            z = z + abs_ref[hd * n_pos + i]
            r0, r1 = i * SEQ_BLK, (i + 1) * SEQ_BLK
            mix_ref[r0:r1, lo:hi] = (u[r0:r1, lo:hi] * z).astype(jnp.bfloat16)

    for g, w in enumerate(POOL_WINDOWS):
        lo, hi = g * HEAD_DIM, (g + 1) * HEAD_DIM
        planes = [st_ref[0, k, :, lo:hi] for k in range(POOL_STATE)]
        planes += [p[i * SEQ_BLK:(i + 1) * SEQ_BLK, lo:hi] for i in range(n_pos)]
        pooled = []
        for i in range(n_pos):
            top = POOL_STATE + i
            s = planes[top - w + 1]
            for k in range(top - w + 2, top + 1):
                s = s + planes[k]
            pooled.append(s * (1.0 / w) - planes[top])
        pooled = jnp.concatenate(pooled, axis=0)
        mix_ref[:, A_WIDTH + lo:A_WIDTH + hi] = _pool_project(pooled, g, bw_ref, bscale_ref).astype(jnp.bfloat16)


def _route(h2, rwt_ref, rb_ref, su_ref, cnt_ref, meta_ref, gcol_ref):
    tm = h2.shape[0]
    h_hi = h2.astype(jnp.bfloat16)
    h_lo = (h2 - h_hi.astype(jnp.float32)).astype(jnp.bfloat16)
    s = (jnp.dot(h_hi, rwt_ref[...], preferred_element_type=jnp.float32)
         + jnp.dot(h_lo, rwt_ref[...], preferred_element_type=jnp.float32))
    st = s.T
    lt = st[0:N_ROUTER_ROWS, :] + st[N_ROUTER_ROWS:2 * N_ROUTER_ROWS, :] + rb_ref[...]
    row = lambda i: lt[i:i + 1, :]
    l1 = [row(i) for i in range(N_GROUPS)]
    m1 = jnp.maximum(jnp.maximum(l1[0], l1[1]), jnp.maximum(l1[2], l1[3]))
    grp = jnp.where(l1[0] == m1, 0, jnp.where(l1[1] == m1, 1, jnp.where(l1[2] == m1, 2, 3)))
    se = (jnp.exp(l1[0] - m1) + jnp.exp(l1[1] - m1)) + (jnp.exp(l1[2] - m1) + jnp.exp(l1[3] - m1))
    pg = 1.0 / se
    l2 = []
    for e in range(EXPERTS_PER_GROUP):
        c = [row(N_GROUPS + g * EXPERTS_PER_GROUP + e) for g in range(N_GROUPS)]
        l2.append(jnp.where(grp == 0, c[0], jnp.where(grp == 1, c[1], jnp.where(grp == 2, c[2], c[3]))))
    v0 = functools.reduce(jnp.maximum, l2)
    i0 = jnp.full_like(grp, EXPERTS_PER_GROUP - 1)
    for e in range(EXPERTS_PER_GROUP - 2, -1, -1):
        i0 = jnp.where(l2[e] == v0, e, i0)
    neg = jnp.float32(-jnp.inf)
    l2m = [jnp.where(i0 == e, neg, l2[e]) for e in range(EXPERTS_PER_GROUP)]
    v1 = functools.reduce(jnp.maximum, l2m)
    i1 = jnp.full_like(grp, EXPERTS_PER_GROUP - 1)
    for e in range(EXPERTS_PER_GROUP - 2, -1, -1):
        i1 = jnp.where((l2m[e] == v1) & (i0 != e), e, i1)
    d = jnp.exp(v1 - v0)
    g0 = pg / (1.0 + d)
    g1 = (pg * d) / (1.0 + d)
    e0 = grp * EXPERTS_PER_GROUP + i0
    e1 = grp * EXPERTS_PER_GROUP + i1

    eiota = lax.broadcasted_iota(jnp.int32, (N_EXPERTS, tm), 0)
    hit0 = eiota == e0
    hit1 = eiota == e1
    onehot = (hit0 | hit1).astype(jnp.bfloat16)
    prefix = jnp.dot(onehot, su_ref[...], preferred_element_type=jnp.float32)
    carry = cnt_ref[...]
    base = prefix + jnp.concatenate([carry] * (tm // LANES), axis=1)
    r0 = jnp.sum(jnp.where(hit0, base, 0.0), axis=0, keepdims=True)
    r1 = jnp.sum(jnp.where(hit1, base, 0.0), axis=0, keepdims=True)
    ones = jnp.ones((tm, LANES), jnp.bfloat16)
    cnt_ref[...] = carry + jnp.dot(onehot, ones, preferred_element_type=jnp.float32)

    meta_ref[0:1, :] = e0
    meta_ref[1:2, :] = e1
    meta_ref[2:3, :] = r0.astype(jnp.int32)
    meta_ref[3:4, :] = r1.astype(jnp.int32)
    meta_ref[4:8, :] = jnp.zeros((4, tm), jnp.int32)

    giota = lax.broadcasted_iota(jnp.int32, (LANES, tm), 0)
    gpad = jnp.where(giota == 0, g0, jnp.where(giota == 1, g1, 0.0))
    gcol_ref[...] = gpad.T


def _mixer_kernel(aws_s_ref, abs_s_ref,
                  x0_ref, xpn_ref, xsn_ref,
                  st_ref, n1g_ref, win_ref, ang_ref, anb_ref, aws_ref, abias_ref, bw_ref, bscale_ref,
                  wout_ref, n2g_ref, rwt_ref, rb_ref, su_ref,
                  x1_ref, h2_ref, meta_ref, gcol_ref, counts_ref, pstate_ref, pp_ref, vp_ref, asg_ref,
                  mix_ref, pcarry_ref, cnt_ref, x_ref, u_ref, v_ref, p_ref, part_meta, dest_vmem, fill_vmem, *dest_smem,
                  n_prompt_tiles, seq_tiles, part_tiles, t_total):
    i = pl.program_id(0)
    is_prompt = i < n_prompt_tiles
    j = i % seq_tiles
    part = i // part_tiles
    tile_in_part = i % part_tiles

    @pl.when((i > 0) & (tile_in_part == 0))
    def _():
        _, _, pad_start = _segment_offsets(cnt_ref[...])
        _stage_inversion(part_meta, pad_start, dest_vmem, dest_smem, fill_vmem, asg_ref)

    def in_proj(x):
        h = _rmsnorm(x, n1g_ref[...]).astype(jnp.bfloat16)
        proj = jnp.dot(h, win_ref[...], preferred_element_type=jnp.float32)
        uv = _gelu(proj[:, : 2 * A_WIDTH])
        u_ref[...] = uv[:, :A_WIDTH]
        v_ref[...] = _layernorm(uv[:, A_WIDTH:], ang_ref[...], anb_ref[...])
        p_ref[...] = proj[:, 2 * A_WIDTH:]
        x_ref[...] = x

    @pl.when(i == 0)
    def _():
        in_proj(x0_ref[0])

    @pl.when(i % part_tiles == 0)
    def _():
        cnt_ref[...] = jnp.zeros_like(cnt_ref)

    @pl.when(is_prompt & (j == 0))
    def _():
        pcarry_ref[...] = jnp.zeros_like(pcarry_ref)

    def finish_tile(x, invert_previous_part):
        x1 = x + jnp.dot(mix_ref[...], wout_ref[...], preferred_element_type=jnp.float32)
        x1_ref[...] = x1
        h2 = _rmsnorm(x1, n2g_ref[...])
        for s in range(ROW_TILES):
            h2_ref[pl.ds(s, TM, stride=ROW_TILES), :] = h2[:, s * LANES:(s + 1) * LANES]
        _route(h2, rwt_ref, rb_ref, su_ref, cnt_ref, meta_ref, gcol_ref)
        counts_ref[0] = cnt_ref[...]
        part_meta[:, pl.ds(pl.multiple_of(tile_in_part * TM, TM), TM)] = meta_ref[0:2 * TOP_K, :]
        if invert_previous_part:
            first_tok = tile_in_part * TM
            for k, dsm in enumerate(dest_smem):
                first_asg = k * t_total + (part - 1) * (part_tiles * TM) + first_tok
                for t in range(TM):
                    asg_ref[0, 0, dsm[0, first_tok + t]] = first_asg + t
        nxt = jnp.minimum(i + 1, pl.num_programs(0) - 1)
        in_proj(jnp.where(nxt < n_prompt_tiles, xpn_ref[0], xsn_ref[...].reshape(TM, D_MODEL)))

    @pl.when(is_prompt & (part == 0))
    def _():
        _prompt_mixers(j, u_ref[...], v_ref[...], p_ref[...], aws_ref, abias_ref, bw_ref, bscale_ref, mix_ref,
                       pcarry_ref, pstate_ref)
        finish_tile(x_ref[...], False)

    @pl.when(is_prompt & (part > 0))
    def _():
        _prompt_mixers(j, u_ref[...], v_ref[...], p_ref[...], aws_ref, abias_ref, bw_ref, bscale_ref, mix_ref,
                       pcarry_ref, pstate_ref)
        finish_tile(x_ref[...], True)

    @pl.when(jnp.logical_not(is_prompt))
    def _():
        vp_ref[...] = v_ref[...].reshape(vp_ref.shape)
        pp_ref[...] = p_ref[...].reshape(pp_ref.shape)
        _sample_mixers(_position_major(u_ref), _position_major(v_ref), _position_major(p_ref), aws_s_ref, abs_s_ref,
                       st_ref, bw_ref, bscale_ref, mix_ref)
        finish_tile(_position_major(x_ref), True)


def _segment_offsets(cnt):
    padded = jnp.floor((cnt + (TE - 1)) * (1.0 / TE)) * TE
    sub = lax.broadcasted_iota(jnp.int32, cnt.shape, 0)
    pad_end = padded
    shift = 1
    while shift < N_EXPERTS:
        pad_end = pad_end + jnp.where(sub >= shift, pltpu.roll(pad_end, shift, 0), 0.0)
        shift *= 2
    return padded, pad_end, pad_end - padded


def _dest_rows(meta_ref, pad_start, dest_ref):
    t_part = meta_ref.shape[1]
    base = pad_start.astype(jnp.int32)
    for k in range(TOP_K):
        e = meta_ref[k:k + 1, :]
        d = meta_ref[TOP_K + k:TOP_K + k + 1, :]
        for ex in range(N_EXPERTS):
            row = jnp.concatenate([base[ex:ex + 1, :]] * (t_part // LANES), axis=1)
            d = d + jnp.where(e == ex, row, 0)
        dest_ref[k] = d


def _stage_inversion(meta_ref, pad_start, dest_vmem, dest_smem, fill_vmem, asg_ref):
    _dest_rows(meta_ref, pad_start, dest_vmem)
    for k, dsm in enumerate(dest_smem):
        pltpu.sync_copy(dest_vmem.at[k], dsm)
    fill_vmem[...] = jnp.full(fill_vmem.shape, -1, jnp.int32)
    pltpu.sync_copy(fill_vmem, asg_ref)


def _tables_kernel(counts_ref, meta_ref, dest_ref, tab_ref):
    h = pl.program_id(0)
    cnt = counts_ref[0]
    sub = lax.broadcasted_iota(jnp.int32, cnt.shape, 0)
    padded, pad_end, pad_start = _segment_offsets(cnt)

    @pl.when(h == pl.num_programs(0) - 1)
    def _():
        _dest_rows(meta_ref, pad_start, dest_ref)

    n_valid = jnp.maximum(pad_end[N_EXPERTS - 1:, :] * (1.0 / TE), 1.0)
    tile = jnp.minimum(lax.broadcasted_iota(jnp.int32, (1, LANES), 1).astype(jnp.float32), n_valid - 1.0)
    tile_start = tile * TE
    tile_e = jnp.minimum(jnp.sum((pad_end <= tile_start).astype(jnp.float32), axis=0, keepdims=True),
                         N_EXPERTS - 1.0)
    is_e = sub.astype(jnp.float32) == tile_e
    seg_start = jnp.sum(jnp.where(is_e, pad_start, 0.0), axis=0, keepdims=True)
    seg_cnt = jnp.sum(jnp.where(is_e, cnt, 0.0), axis=0, keepdims=True)
    rows = jnp.clip(seg_cnt - (tile_start - seg_start), 0.0, TE)
    tab_ref[0, 0:1, :] = tile_e.astype(jnp.int32)
    tab_ref[0, 1:2, :] = rows.astype(jnp.int32)
    tab_ref[0, 2:3, :] = n_valid.astype(jnp.int32)
    tab_ref[0, 3:, :] = jnp.zeros((SUBLANES - 3, LANES), jnp.int32)


def _expert_kernel(te_ref, nv_ref, nrows_ref,
                   src_ref, src_next_ref, dst_ref,
                   dest_k0_ref, dest_k1_ref,
                   h2_hbm, w1_ref, w3_ref, w2_ref,
                   out_hbm,
                   xt0, xt1, ob0, ob1, fill_vmem, asg_last, ssem, h2_ref, hsem, *, t_total, t_part):
    h = pl.program_id(0)
    i = pl.program_id(1)
    g = h * pl.num_programs(1) + i
    nv = nv_ref[h]
    slab = lambda r: pl.ds(pl.multiple_of(r * ROW_TILES, ROW_TILES), ROW_TILES)
    toks_per_tile = TE // TOP_K
    inv_tiles = t_part // toks_per_tile
    part_slabs = t_part * ROW_TILES
    fetch_part = lambda part: pltpu.make_async_copy(h2_hbm.at[pl.ds(part * part_slabs, part_slabs), :], h2_ref, hsem)

    def last_part_src(tile):
        def src(r):
            a = asg_last[0, tile * TE + r]
            return a - jnp.where(a >= t_total, t_total + t_part, t_part)
        return src

    first_tables = (lambda r: src_ref[0, 0, r], lambda r: src_next_ref[0, 0, r],
                    lambda first, uu: dst_ref[0, 0, first + uu])
    last_tables = (last_part_src(i), last_part_src(jnp.minimum(i + 1, nv - 1)),
                   lambda first, uu: asg_last[0, (i * TE + first) + uu])

    def gather(src, xt):
        for r in range(TE):
            xt[r * ROW_TILES:(r + 1) * ROW_TILES, :] = h2_ref[slab(src(r)), :]

    def invert_share():
        first_tok = jnp.minimum(i, inv_tiles - 1) * toks_per_tile
        for k, dest_ref in enumerate((dest_k0_ref, dest_k1_ref)):
            first_asg = k * t_total + t_part + first_tok
            for t in range(toks_per_tile):
                asg_last[0, dest_ref[0, 0, t]] = first_asg + t

    def wait_scatter(tile, ob, other, sem):
        n = nrows_ref[tile] * ROW_TILES
        pltpu.make_async_copy(other.at[pl.ds(0, n), :], ob.at[pl.ds(0, n), :], sem).wait()

    def tile_body(xt, xt_next, ob, ob_other, sem, sem_other, tables, first_part):
        _, src_next, dst = tables

        @pl.when(i >= 2)
        def _():
            wait_scatter(g - 2, ob, ob_other, sem)

        def compute(m):
            gather(src_next, xt_next)
            if first_part:
                invert_share()
            x = jnp.concatenate([_row_slab(xt, s, m) for s in range(ROW_TILES)], axis=1).astype(jnp.bfloat16)
            a = jnp.dot(x, w1_ref[0].astype(jnp.bfloat16), preferred_element_type=jnp.float32)
            b = jnp.dot(x, w3_ref[0].astype(jnp.bfloat16), preferred_element_type=jnp.float32)
            hmid = (a * jax.nn.sigmoid(a)) * b
            o = jnp.dot(hmid.astype(jnp.bfloat16), w2_ref[0].astype(jnp.bfloat16),
                        preferred_element_type=jnp.float32)
            for s in range(ROW_TILES):
                ob[pl.ds(s, m, stride=ROW_TILES), :] = o[:, s * LANES:(s + 1) * LANES]

        n = nrows_ref[g]

        @pl.when(n > TE // 2)
        def _():
            compute(TE)

        @pl.when(n <= TE // 2)
        def _():
            compute(TE // 2)

        if first_part:
            @pl.when(i == nv - 1)
            def _():
                fetch_part(1).start()

        copy = lambda first, uu: pltpu.make_async_copy(ob.at[slab(first + uu), :], out_hbm.at[dst(first, uu)], sem)

        def issue_groups(first_row, n_groups, group):
            def body(rb, carry):
                for uu in range(group):
                    copy(first_row + rb * group, uu).start(priority=uu % 2)
                return carry
            lax.fori_loop(0, n_groups, body, 0)

        n_full = n // DMA_UNROLL
        n_small = n // SUBLANES - n_full * (DMA_UNROLL // SUBLANES)
        issue_groups(0, n_full, DMA_UNROLL)
        issue_groups(n_full * DMA_UNROLL, n_small, SUBLANES)

        def tail(r, carry):
            copy(r, 0).start(priority=1)
            return carry
        lax.fori_loop((n // SUBLANES) * SUBLANES, n, tail, 0)

        @pl.when(i == nv - 1)
        def _():
            @pl.when(i >= 1)
            def _():
                wait_scatter(g - 1, ob_other, ob, sem_other)
            wait_scatter(g, ob, ob_other, sem)

    def part_body(tables, first_part):
        @pl.when(i == 0)
        def _():
            if not first_part:
                fetch_part(1).wait()
            gather(tables[0], xt0)

        @pl.when(i % 2 == 0)
        def _():
            tile_body(xt0, xt1, ob0, ob1, ssem.at[0], ssem.at[1], tables, first_part)

        @pl.when(i % 2 == 1)
        def _():
            tile_body(xt1, xt0, ob1, ob0, ssem.at[1], ssem.at[0], tables, first_part)

    @pl.when((h == 0) & (i == 0))
    def _():
        fetch_part(0).start()
        fill_vmem[...] = jnp.full(fill_vmem.shape, t_part, jnp.int32)
        pltpu.sync_copy(fill_vmem, asg_last)
        fetch_part(0).wait()

    @pl.when((i < nv) & (h == 0))
    def _():
        part_body(first_tables, True)

    @pl.when((i < nv) & (h > 0))
    def _():
        part_body(last_tables, False)


def _combine_kernel(x1_ref, gcol_ref, nfg_ref, o0_ref, o1_ref, yp_ref, ys_ref, *, n_prompt_tiles):
    i = pl.program_id(0)
    o0 = jnp.concatenate([_row_slab(o0_ref, s, TC) for s in range(ROW_TILES)], axis=1)
    o1 = jnp.concatenate([_row_slab(o1_ref, s, TC) for s in range(ROW_TILES)], axis=1)
    g = gcol_ref[...]
    moe = g[:, 0:1] * o0 + g[:, 1:2] * o1
    y = _rmsnorm(x1_ref[...] + moe, nfg_ref[...])

    @pl.when(i < n_prompt_tiles)
    def _():
        yp_ref[...] = y

    @pl.when(i >= n_prompt_tiles)
    def _():
        for blk in range(TC // TM):
            for q in range(TM // SEQ_BLK):
                r0 = blk * TM + q * SEQ_BLK
                ys_ref[blk * SEQ_BLK:(blk + 1) * SEQ_BLK, q, :] = y[r0:r0 + SEQ_BLK, :]


def _const_spec(shape):
    return pl.BlockSpec(shape, lambda *_: (0,) * len(shape))


def kernel(x_prompt, x_sample, state_pool, norm1_g, w_in, a_norm_g, a_norm_b, a_ws, a_bs, b_w, b_scale, w_out,
           norm2_g, r1_w, r1_b, r2_w, r2_b, exp_w1, exp_w3, exp_w2, normf_g):
    f32, bf16, i32 = jnp.float32, jnp.bfloat16, jnp.int32
    n_batch, seq, _ = x_prompt.shape
    dec_batch, dec_seq, _ = x_sample.shape
    assert norm1_g.shape[0] == 1 and seq % TM == 0 and TM % CHUNK == 0
    assert dec_seq * SEQ_BLK == TM and dec_batch % SEQ_BLK == 0 and dec_seq <= CHUNK
    t_prompt = n_batch * seq
    t_sample = dec_batch * dec_seq
    t_total = t_prompt + t_sample
    n_tok_tiles = t_total // TM
    n_prompt_tiles = t_prompt // TM
    n_sample_tiles = t_sample // TM
    seq_tiles = seq // TM
    plane_rows = t_total * ROW_TILES
    assert n_tok_tiles % N_PARTS == 0
    part_tiles = n_tok_tiles // N_PARTS
    t_part = part_tiles * TM
    n_exp_tiles = -(-(TOP_K * t_part + N_EXPERTS * (TE - 1)) // TE)
    p_rows = n_exp_tiles * TE

    n1g = norm1_g[0][None, :]
    n2g = norm2_g[0][None, :]
    nfg = normf_g[None, :]
    win = w_in[0].astype(bf16)
    wout = w_out[0].astype(bf16)
    ang = a_norm_g[0][None, :]
    anb = a_norm_b[0][None, :]
    bw = b_w[0].astype(bf16)
    bscale = b_scale[0][None, :]
    abias = jnp.repeat(a_bs[0][:, :CHUNK].T, HEAD_DIM, axis=1)
    rw = jnp.concatenate([r1_w[0], r2_w[0].transpose(1, 0, 2).reshape(D_MODEL, N_EXPERTS),
                          jnp.zeros((D_MODEL, N_ROUTER_ROWS - N_GROUPS - N_EXPERTS), f32)], axis=1)
    rw_hi = rw.astype(bf16)
    rw_lo = (rw - rw_hi.astype(f32)).astype(bf16)
    rwt = jnp.concatenate([rw_hi, rw_lo, jnp.zeros((D_MODEL, LANES - 2 * N_ROUTER_ROWS), bf16)], axis=1)
    rbias = jnp.concatenate([r1_b[0], r2_b[0].reshape(-1),
                             jnp.zeros((N_ROUTER_ROWS - N_GROUPS - N_EXPERTS,), f32)])
    rb = jnp.broadcast_to(rbias[:, None], (N_ROUTER_ROWS, TM))
    su = (jnp.arange(TM)[:, None] < jnp.arange(TM)[None, :]).astype(bf16)
    st_planes = state_pool[0].reshape(n_sample_tiles, SEQ_BLK, POOL_STATE, B_WIDTH).transpose(0, 2, 1, 3)
    aws_s = a_ws[0][:, :dec_seq, :dec_seq].reshape(-1)
    abs_s = a_bs[0][:, :dec_seq].reshape(-1)

    cparams = pltpu.CompilerParams(dimension_semantics=("arbitrary",), vmem_limit_bytes=VMEM_LIMIT)

    p_tile = lambda i: jnp.minimum(i, n_prompt_tiles - 1)
    s_tile = lambda i: jnp.maximum(i - n_prompt_tiles, 0)
    sample_rows_shape = jax.ShapeDtypeStruct((dec_batch, dec_seq, A_WIDTH), f32)
    assert N_PARTS >= 2 and n_prompt_tiles >= part_tiles and t_part % TM == 0
    x1, h2, meta, gcol, counts, pstate, p_rows_sample, v_rows_sample, asg_head = pl.pallas_call(
        functools.partial(_mixer_kernel, n_prompt_tiles=n_prompt_tiles, seq_tiles=seq_tiles, part_tiles=part_tiles,
                          t_total=t_total),
        grid_spec=pltpu.PrefetchScalarGridSpec(
            num_scalar_prefetch=2,
            grid=(n_tok_tiles,),
            in_specs=[
                pl.BlockSpec((1, TM, D_MODEL), lambda i, *_: (0, 0, 0)),
                pl.BlockSpec((1, TM, D_MODEL),
                             lambda i, *_: (p_tile(i + 1) // seq_tiles, p_tile(i + 1) % seq_tiles, 0)),
                pl.BlockSpec((SEQ_BLK, dec_seq, D_MODEL),
                             lambda i, *_: (jnp.minimum(s_tile(i + 1), n_sample_tiles - 1), 0, 0)),
                pl.BlockSpec((1, POOL_STATE, SEQ_BLK, B_WIDTH), lambda i, *_: (s_tile(i), 0, 0, 0)),
                _const_spec((1, D_MODEL)), _const_spec((D_MODEL, 3 * A_WIDTH)), _const_spec((1, A_WIDTH)),
                _const_spec((1, A_WIDTH)), _const_spec((N_HEADS, CHUNK, CHUNK)), _const_spec((CHUNK, A_WIDTH)),
                _const_spec((N_GROUPS, HEAD_DIM, HEAD_DIM)), _const_spec((1, B_WIDTH)),
                _const_spec((D_MODEL, D_MODEL)), _const_spec((1, D_MODEL)), _const_spec((D_MODEL, LANES)),
                _const_spec((N_ROUTER_ROWS, TM)), _const_spec((TM, TM)),
            ],
            out_specs=[
                pl.BlockSpec((TM, D_MODEL), lambda i, *_: (i, 0)),
                pl.BlockSpec((TM * ROW_TILES, LANES), lambda i, *_: (i, 0)),
                pl.BlockSpec((SUBLANES, TM), lambda i, *_: (0, i)),
                pl.BlockSpec((TM, LANES), lambda i, *_: (i, 0)),
                pl.BlockSpec((1, N_EXPERTS, LANES), lambda i, *_: (i // part_tiles, 0, 0)),
                pl.BlockSpec((1, CARRY_ROWS, B_WIDTH), lambda i, *_: (p_tile(i) // seq_tiles, 0, 0)),
                pl.BlockSpec((SEQ_BLK, dec_seq, A_WIDTH), lambda i, *_: (s_tile(i), 0, 0)),
                pl.BlockSpec((SEQ_BLK, dec_seq, A_WIDTH), lambda i, *_: (s_tile(i), 0, 0)),
                pl.BlockSpec((1, 1, p_rows), lambda i, *_: (jnp.maximum(i // part_tiles - 1, 0), 0, 0),
                             memory_space=pltpu.SMEM),
            ],
            scratch_shapes=[pltpu.VMEM((TM, D_MODEL), bf16), pltpu.VMEM((CARRY_ROWS, B_WIDTH), f32),
                            pltpu.VMEM((N_EXPERTS, LANES), f32), pltpu.VMEM((TM, D_MODEL), f32),
                            pltpu.VMEM((TM, A_WIDTH), f32), pltpu.VMEM((TM, A_WIDTH), f32),
                            pltpu.VMEM((TM, B_WIDTH), f32),
                            pltpu.VMEM((2 * TOP_K, t_part), i32), pltpu.VMEM((TOP_K, 1, t_part), i32),
                            pltpu.VMEM((1, 1, p_rows), i32)] + [pltpu.SMEM((1, t_part), i32)] * TOP_K,
        ),
        out_shape=[
            jax.ShapeDtypeStruct((t_total, D_MODEL), f32),
            jax.ShapeDtypeStruct((plane_rows, LANES), f32),
            jax.ShapeDtypeStruct((SUBLANES, t_total), i32),
            jax.ShapeDtypeStruct((t_total, LANES), f32),
            jax.ShapeDtypeStruct((N_PARTS, N_EXPERTS, LANES), f32),
            jax.ShapeDtypeStruct((n_batch, CARRY_ROWS, B_WIDTH), f32),
            sample_rows_shape, sample_rows_shape,
            jax.ShapeDtypeStruct((N_PARTS - 1, 1, p_rows), i32),
        ],
        compiler_params=cparams,
        name="mixer",
    )(aws_s, abs_s, x_prompt, x_prompt, x_sample, st_planes, n1g, win, ang, anb, a_ws[0][:, :CHUNK, :CHUNK], abias,
      bw, bscale, wout, n2g, rwt, rb, su)

    assert n_exp_tiles <= LANES and t_part % LANES == 0
    dest_last, tab = pl.pallas_call(
        _tables_kernel,
        grid=(N_PARTS,),
        in_specs=[pl.BlockSpec((1, N_EXPERTS, LANES), lambda h: (h, 0, 0)),
                  pl.BlockSpec((SUBLANES, t_part), lambda h: (0, h))],
        out_specs=[_const_spec((TOP_K, 1, t_part)),
                   pl.BlockSpec((1, SUBLANES, LANES), lambda h: (h, 0, 0))],
        out_shape=[jax.ShapeDtypeStruct((TOP_K, 1, t_part), i32),
                   jax.ShapeDtypeStruct((N_PARTS, SUBLANES, LANES), i32)],
        compiler_params=cparams,
        name="route_tables",
    )(counts, meta)
    tile_e = tab[:, 0, :n_exp_tiles].reshape(-1)
    tile_rows = tab[:, 1, :n_exp_tiles].reshape(-1)
    n_valid = tab[:, 2, 0]
    assert N_PARTS == 2 and (TOP_K * t_part) % TE == 0
    asg = asg_head.reshape(-1)
    row_tok = jnp.where(asg >= t_total, asg - t_total, asg)
    row_src = jnp.where(asg >= 0, row_tok, 0).reshape(n_exp_tiles, 1, TE)
    row_dst = jnp.maximum(asg, 0).reshape(n_exp_tiles, 1, TE)

    flat = lambda h, i: h * n_exp_tiles + i
    cur_blk = lambda h, i, te, nv, nr: (jnp.minimum(i, nv[0] - 1), 0, 0)
    nxt_blk = lambda h, i, te, nv, nr: (jnp.minimum(i + 1, nv[0] - 1), 0, 0)
    w_blk = lambda h, i, te, nv, nr: (te[flat(h, i)], 0, 0)
    smem_blk = lambda imap: pl.BlockSpec((1, 1, TE), imap, memory_space=pltpu.SMEM)
    last_share = t_part // (TE // TOP_K) - 1
    share_blk = lambda k: pl.BlockSpec(
        (1, 1, TE // TOP_K), lambda h, i, *_: (k, 0, jnp.where(h == 0, jnp.minimum(i, last_share), last_share)),
        memory_space=pltpu.SMEM)
    row_buf = pltpu.VMEM((TE * ROW_TILES, LANES), f32)
    out_tok = pl.pallas_call(
        functools.partial(_expert_kernel, t_total=t_total, t_part=t_part),
        grid_spec=pltpu.PrefetchScalarGridSpec(
            num_scalar_prefetch=3,
            grid=(N_PARTS, n_exp_tiles),
            in_specs=[smem_blk(cur_blk), smem_blk(nxt_blk), smem_blk(cur_blk),
                      share_blk(0), share_blk(1),
                      pl.BlockSpec(memory_space=pl.ANY),
                      pl.BlockSpec((1, D_MODEL, D_EXPERT), w_blk),
                      pl.BlockSpec((1, D_MODEL, D_EXPERT), w_blk),
                      pl.BlockSpec((1, D_EXPERT, D_MODEL), w_blk)],
            out_specs=pl.BlockSpec(memory_space=pl.ANY),
            scratch_shapes=[row_buf, row_buf, row_buf, row_buf, pltpu.VMEM((1, p_rows), i32),
                            pltpu.SMEM((1, p_rows), i32), pltpu.SemaphoreType.DMA((2,)),
                            pltpu.VMEM((t_part * ROW_TILES, LANES), f32), pltpu.SemaphoreType.DMA(())],
        ),
        out_shape=jax.ShapeDtypeStruct((TOP_K * t_total, ROW_TILES, LANES), f32),
        compiler_params=pltpu.CompilerParams(dimension_semantics=("arbitrary", "arbitrary"),
                                             vmem_limit_bytes=EXPERT_VMEM_LIMIT),
        name="moe_experts",
    )(tile_e, n_valid, tile_rows, row_src, row_src, row_dst, dest_last, dest_last, h2, exp_w1[0], exp_w3[0], exp_w2[0])
    out_tok = out_tok.reshape(TOP_K * plane_rows, LANES)

    assert t_prompt % TC == 0 and t_sample % TC == 0
    n_c_prompt = t_prompt // TC
    y_prompt, y_sample = pl.pallas_call(
        functools.partial(_combine_kernel, n_prompt_tiles=n_c_prompt),
        grid=(t_total // TC,),
        in_specs=[pl.BlockSpec((TC, D_MODEL), lambda i: (i, 0)),
                  pl.BlockSpec((TC, LANES), lambda i: (i, 0)),
                  _const_spec((1, D_MODEL)),
                  pl.BlockSpec((TC * ROW_TILES, LANES), lambda i: (i, 0)),
                  pl.BlockSpec((TC * ROW_TILES, LANES), lambda i: (t_total // TC + i, 0))],
        out_specs=[pl.BlockSpec((TC, D_MODEL), lambda i: (jnp.minimum(i, n_c_prompt - 1), 0)),
                   pl.BlockSpec((TC // dec_seq, dec_seq, D_MODEL), lambda i: (jnp.maximum(i - n_c_prompt, 0), 0, 0))],
        out_shape=[jax.ShapeDtypeStruct((t_prompt, D_MODEL), f32),
                   jax.ShapeDtypeStruct((dec_batch, dec_seq, D_MODEL), f32)],
        compiler_params=cparams,
        name="moe_combine",
    )(x1, gcol, nfg, out_tok, out_tok)

    y_prompt = y_prompt.reshape(n_batch, seq, D_MODEL)
    pool_state_prompt = pstate[None, :, CARRY_ROWS - POOL_STATE:, :]
    pool_state_sample = jnp.concatenate([state_pool[0], p_rows_sample], axis=1)[None, :, -POOL_STATE:, :]
    chunk_v_sample = v_rows_sample[None]
    return (y_prompt, y_sample, pool_state_prompt, pool_state_sample, chunk_v_sample)
```

```python
import functools
import math

import jax
import jax.numpy as jnp
from jax import lax
from jax.experimental import pallas as pl
from jax.experimental.pallas import tpu as pltpu

D_MODEL = 1024
A_WIDTH = 512
B_WIDTH = 512
N_HEADS = 4
HEAD_DIM = 128
CHUNK = 128
POOL_WINDOWS = (2, 4, 8, 16)
POOL_STATE = 15
N_GROUPS = 4
EXPERTS_PER_GROUP = 8
N_EXPERTS = 32
TOP_K = 2
D_EXPERT = 512
EPS = 1e-6

SUBLANES = 8
LANES = 128
ROW_TILES = D_MODEL // LANES

TM = 256
TC = 1024
TE = 512
SEQ_BLK = 32
N_ROUTER_ROWS = 40
CARRY_ROWS = 16
DMA_UNROLL = 32
VMEM_LIMIT = 48 * 1024 * 1024
N_PARTS = 2
WEIGHT_SLOTS = 3
EXPERT_VMEM_LIMIT = 63 * 1024 * 1024

_INV_SQRT2 = 1.0 / math.sqrt(2.0)


def _rmsnorm(x, g):
    r = lax.rsqrt(jnp.mean(x * x, axis=-1, keepdims=True) + EPS)
    return (x * r) * g


def _gelu(x):
    return 0.5 * x * (1.0 + lax.erf(x * _INV_SQRT2))


def _layernorm(x, g, b):
    mu = jnp.mean(x, axis=-1, keepdims=True)
    xc = x - mu
    var = jnp.mean(xc * xc, axis=-1, keepdims=True)
    return (xc * lax.rsqrt(var + EPS)) * g + b


def _row_slab(ref, s, n):
    return ref[pl.ds(s, n, stride=ROW_TILES), :]


def _pool_project(pooled, g, bw_ref, bscale_ref):
    lo, hi = g * HEAD_DIM, (g + 1) * HEAD_DIM
    hb = jnp.dot(pooled.astype(jnp.bfloat16), bw_ref[g], preferred_element_type=jnp.float32)
    return hb * bscale_ref[:, lo:hi]


def _prompt_mixers(j, u, v, p, aws_ref, abias_ref, bw_ref, bscale_ref, mix_ref, pcarry_ref, pstate_ref):
    tri = (lax.broadcasted_iota(jnp.int32, (CHUNK, CHUNK), 0)
           >= lax.broadcasted_iota(jnp.int32, (CHUNK, CHUNK), 1))
    vb = v.astype(jnp.bfloat16)
    for hd in range(N_HEADS):
        lo, hi = hd * HEAD_DIM, (hd + 1) * HEAD_DIM
        w = jnp.where(tri, aws_ref[hd], 0.0).astype(jnp.bfloat16)
        for c in range(TM // CHUNK):
            r0, r1 = c * CHUNK, (c + 1) * CHUNK
            z = jnp.dot(w, vb[r0:r1, lo:hi], preferred_element_type=jnp.float32) + abias_ref[:, lo:hi]
            mix_ref[r0:r1, lo:hi] = (u[r0:r1, lo:hi] * z).astype(jnp.bfloat16)

    head_pos = j * TM + lax.broadcasted_iota(jnp.int32, (CARRY_ROWS, LANES), 0)
    for g, w in enumerate(POOL_WINDOWS):
        lo, hi = g * HEAD_DIM, (g + 1) * HEAD_DIM
        pg = p[:, lo:hi]
        acc = jnp.concatenate([pcarry_ref[:, lo:hi], pg], axis=0)
        shift = 1
        while shift < w:
            acc = acc + pltpu.roll(acc, shift, 0)
            shift *= 2
        head = acc[CARRY_ROWS:2 * CARRY_ROWS, :] / jnp.minimum(head_pos + 1, w).astype(jnp.float32)
        mean = jnp.concatenate([head, acc[2 * CARRY_ROWS:, :] * (1.0 / w)], axis=0)
        pooled = mean - pg
        mix_ref[:, A_WIDTH + lo:A_WIDTH + hi] = _pool_project(pooled, g, bw_ref, bscale_ref).astype(jnp.bfloat16)
    tail = p[TM - CARRY_ROWS:, :]
    pcarry_ref[...] = tail
    pstate_ref[0] = tail


def _position_major(ref):
    n_pos = TM // SEQ_BLK
    rows = ref.reshape(SEQ_BLK, n_pos, ref.shape[-1])
    return jnp.concatenate([rows[:, q, :] for q in range(n_pos)], axis=0)


def _sample_mixers(u, v, p, aws_ref, abs_ref, st_ref, bw_ref, bscale_ref, mix_ref):
    n_pos = TM // SEQ_BLK

    for hd in range(N_HEADS):
        lo, hi = hd * HEAD_DIM, (hd + 1) * HEAD_DIM
        vplanes = [v[s * SEQ_BLK:(s + 1) * SEQ_BLK, lo:hi] for s in range(n_pos)]
        for i in range(n_pos):
            z = vplanes[0] * aws_ref[(hd * n_pos + i) * n_pos]
            for s in range(1, i + 1):
                z = z + vplanes[s] * aws_ref[(hd * n_pos + i) * n_pos + s]
            z = z + abs_ref[hd * n_pos + i]
            r0, r1 = i * SEQ_BLK, (i + 1) * SEQ_BLK
            mix_ref[r0:r1, lo:hi] = (u[r0:r1, lo:hi] * z).astype(jnp.bfloat16)

    for g, w in enumerate(POOL_WINDOWS):
        lo, hi = g * HEAD_DIM, (g + 1) * HEAD_DIM
        planes = [st_ref[0, k, :, lo:hi] for k in range(POOL_STATE)]
        planes += [p[i * SEQ_BLK:(i + 1) * SEQ_BLK, lo:hi] for i in range(n_pos)]
        pooled = []
        for i in range(n_pos):
            top = POOL_STATE + i
            s = planes[top - w + 1]
            for k in range(top - w + 2, top + 1):
                s = s + planes[k]
            pooled.append(s * (1.0 / w) - planes[top])
        pooled = jnp.concatenate(pooled, axis=0)
        mix_ref[:, A_WIDTH + lo:A_WIDTH + hi] = _pool_project(pooled, g, bw_ref, bscale_ref).astype(jnp.bfloat16)


def _route(h2, rwt_ref, rb_ref, su_ref, cnt_ref, meta_ref, gcol_ref):
    tm = h2.shape[0]
    h_hi = h2.astype(jnp.bfloat16)
    h_lo = (h2 - h_hi.astype(jnp.float32)).astype(jnp.bfloat16)
    s = (jnp.dot(h_hi, rwt_ref[...], preferred_element_type=jnp.float32)
         + jnp.dot(h_lo, rwt_ref[...], preferred_element_type=jnp.float32))
    st = s.T
    lt = st[0:N_ROUTER_ROWS, :] + st[N_ROUTER_ROWS:2 * N_ROUTER_ROWS, :] + rb_ref[...]
    row = lambda i: lt[i:i + 1, :]
    l1 = [row(i) for i in range(N_GROUPS)]
    m1 = jnp.maximum(jnp.maximum(l1[0], l1[1]), jnp.maximum(l1[2], l1[3]))
    grp = jnp.where(l1[0] == m1, 0, jnp.where(l1[1] == m1, 1, jnp.where(l1[2] == m1, 2, 3)))
    se = (jnp.exp(l1[0] - m1) + jnp.exp(l1[1] - m1)) + (jnp.exp(l1[2] - m1) + jnp.exp(l1[3] - m1))
    pg = 1.0 / se
    l2 = []
    for e in range(EXPERTS_PER_GROUP):
        c = [row(N_GROUPS + g * EXPERTS_PER_GROUP + e) for g in range(N_GROUPS)]
        l2.append(jnp.where(grp == 0, c[0], jnp.where(grp == 1, c[1], jnp.where(grp == 2, c[2], c[3]))))
    v0 = functools.reduce(jnp.maximum, l2)
    i0 = jnp.full_like(grp, EXPERTS_PER_GROUP - 1)
    for e in range(EXPERTS_PER_GROUP - 2, -1, -1):
        i0 = jnp.where(l2[e] == v0, e, i0)
    neg = jnp.float32(-jnp.inf)
    l2m = [jnp.where(i0 == e, neg, l2[e]) for e in range(EXPERTS_PER_GROUP)]
    v1 = functools.reduce(jnp.maximum, l2m)
    i1 = jnp.full_like(grp, EXPERTS_PER_GROUP - 1)
    for e in range(EXPERTS_PER_GROUP - 2, -1, -1):
        i1 = jnp.where((l2m[e] == v1) & (i0 != e), e, i1)
    d = jnp.exp(v1 - v0)
    g0 = pg / (1.0 + d)
    g1 = (pg * d) / (1.0 + d)
    e0 = grp * EXPERTS_PER_GROUP + i0
    e1 = grp * EXPERTS_PER_GROUP + i1

    eiota = lax.broadcasted_iota(jnp.int32, (N_EXPERTS, tm), 0)
    hit0 = eiota == e0
    hit1 = eiota == e1
    onehot = (hit0 | hit1).astype(jnp.bfloat16)
    prefix = jnp.dot(onehot, su_ref[...], preferred_element_type=jnp.float32)
    carry = cnt_ref[...]
    base = prefix + jnp.concatenate([carry] * (tm // LANES), axis=1)
    r0 = jnp.sum(jnp.where(hit0, base, 0.0), axis=0, keepdims=True)
    r1 = jnp.sum(jnp.where(hit1, base, 0.0), axis=0, keepdims=True)
    ones = jnp.ones((tm, LANES), jnp.bfloat16)
    cnt_ref[...] = carry + jnp.dot(onehot, ones, preferred_element_type=jnp.float32)

    meta_ref[0:1, :] = e0
    meta_ref[1:2, :] = e1
    meta_ref[2:3, :] = r0.astype(jnp.int32)
    meta_ref[3:4, :] = r1.astype(jnp.int32)
    meta_ref[4:8, :] = jnp.zeros((4, tm), jnp.int32)

    giota = lax.broadcasted_iota(jnp.int32, (LANES, tm), 0)
    gpad = jnp.where(giota == 0, g0, jnp.where(giota == 1, g1, 0.0))
    gcol_ref[...] = gpad.T


def _mixer_kernel(aws_s_ref, abs_s_ref,
                  x0_ref, xpn_ref, xsn_ref,
                  st_ref, n1g_ref, win_ref, ang_ref, anb_ref, aws_ref, abias_ref, bw_ref, bscale_ref,
                  wout_ref, n2g_ref, rwt_ref, rb_ref, su_ref,
                  x1_ref, h2_ref, meta_ref, gcol_ref, counts_ref, pstate_ref, pp_ref, vp_ref, asg_ref,
                  mix_ref, pcarry_ref, cnt_ref, x_ref, u_ref, v_ref, p_ref, part_meta, dest_vmem, fill_vmem, *dest_smem,
                  n_prompt_tiles, seq_tiles, part_tiles, t_total):
    i = pl.program_id(0)
    is_prompt = i < n_prompt_tiles
    j = i % seq_tiles
    part = i // part_tiles
    tile_in_part = i % part_tiles

    @pl.when((i > 0) & (tile_in_part == 0))
    def _():
        _, _, pad_start = _segment_offsets(cnt_ref[...])
        _stage_inversion(part_meta, pad_start, dest_vmem, dest_smem, fill_vmem, asg_ref)

    def in_proj(x):
        h = _rmsnorm(x, n1g_ref[...]).astype(jnp.bfloat16)
        proj = jnp.dot(h, win_ref[...], preferred_element_type=jnp.float32)
        uv = _gelu(proj[:, : 2 * A_WIDTH])
        u_ref[...] = uv[:, :A_WIDTH]
        v_ref[...] = _layernorm(uv[:, A_WIDTH:], ang_ref[...], anb_ref[...])
        p_ref[...] = proj[:, 2 * A_WIDTH:]
        x_ref[...] = x

    @pl.when(i == 0)
    def _():
        in_proj(x0_ref[0])

    @pl.when(i % part_tiles == 0)
    def _():
        cnt_ref[...] = jnp.zeros_like(cnt_ref)

    @pl.when(is_prompt & (j == 0))
    def _():
        pcarry_ref[...] = jnp.zeros_like(pcarry_ref)

    def finish_tile(x, invert_previous_part):
        x1 = x + jnp.dot(mix_ref[...], wout_ref[...], preferred_element_type=jnp.float32)
        x1_ref[...] = x1
        h2 = _rmsnorm(x1, n2g_ref[...])
        for s in range(ROW_TILES):
            h2_ref[pl.ds(s, TM, stride=ROW_TILES), :] = h2[:, s * LANES:(s + 1) * LANES]
        _route(h2, rwt_ref, rb_ref, su_ref, cnt_ref, meta_ref, gcol_ref)
        counts_ref[0] = cnt_ref[...]
        part_meta[:, pl.ds(pl.multiple_of(tile_in_part * TM, TM), TM)] = meta_ref[0:2 * TOP_K, :]
        if invert_previous_part:
            first_tok = tile_in_part * TM
            for k, dsm in enumerate(dest_smem):
                first_asg = k * t_total + (part - 1) * (part_tiles * TM) + first_tok
                for t in range(TM):
                    asg_ref[0, 0, dsm[0, first_tok + t]] = first_asg + t
        nxt = jnp.minimum(i + 1, pl.num_programs(0) - 1)
        in_proj(jnp.where(nxt < n_prompt_tiles, xpn_ref[0], xsn_ref[...].reshape(TM, D_MODEL)))

    @pl.when(is_prompt & (part == 0))
    def _():
        _prompt_mixers(j, u_ref[...], v_ref[...], p_ref[...], aws_ref, abias_ref, bw_ref, bscale_ref, mix_ref,
                       pcarry_ref, pstate_ref)
        finish_tile(x_ref[...], False)

    @pl.when(is_prompt & (part > 0))
    def _():
        _prompt_mixers(j, u_ref[...], v_ref[...], p_ref[...], aws_ref, abias_ref, bw_ref, bscale_ref, mix_ref,
                       pcarry_ref, pstate_ref)
        finish_tile(x_ref[...], True)

    @pl.when(jnp.logical_not(is_prompt))
    def _():
        vp_ref[...] = v_ref[...].reshape(vp_ref.shape)
        pp_ref[...] = p_ref[...].reshape(pp_ref.shape)
        _sample_mixers(_position_major(u_ref), _position_major(v_ref), _position_major(p_ref), aws_s_ref, abs_s_ref,
                       st_ref, bw_ref, bscale_ref, mix_ref)
        finish_tile(_position_major(x_ref), True)


def _segment_offsets(cnt):
    padded = jnp.floor((cnt + (TE - 1)) * (1.0 / TE)) * TE
    sub = lax.broadcasted_iota(jnp.int32, cnt.shape, 0)
    pad_end = padded
    shift = 1
    while shift < N_EXPERTS:
        pad_end = pad_end + jnp.where(sub >= shift, pltpu.roll(pad_end, shift, 0), 0.0)
        shift *= 2
    return padded, pad_end, pad_end - padded


def _dest_rows(meta_ref, pad_start, dest_ref):
    t_part = meta_ref.shape[1]
    base = pad_start.astype(jnp.int32)
    for k in range(TOP_K):
        e = meta_ref[k:k + 1, :]
        d = meta_ref[TOP_K + k:TOP_K + k + 1, :]
        for ex in range(N_EXPERTS):
            row = jnp.concatenate([base[ex:ex + 1, :]] * (t_part // LANES), axis=1)
            d = d + jnp.where(e == ex, row, 0)
        dest_ref[k] = d


def _stage_inversion(meta_ref, pad_start, dest_vmem, dest_smem, fill_vmem, asg_ref):
    _dest_rows(meta_ref, pad_start, dest_vmem)
    for k, dsm in enumerate(dest_smem):
        pltpu.sync_copy(dest_vmem.at[k], dsm)
    fill_vmem[...] = jnp.full(fill_vmem.shape, -1, jnp.int32)
    pltpu.sync_copy(fill_vmem, asg_ref)


def _tables_kernel(counts_ref, meta_ref, dest_ref, tab_ref):
    h = pl.program_id(0)
    cnt = counts_ref[0]
    sub = lax.broadcasted_iota(jnp.int32, cnt.shape, 0)
    padded, pad_end, pad_start = _segment_offsets(cnt)

    @pl.when(h == pl.num_programs(0) - 1)
    def _():
        _dest_rows(meta_ref, pad_start, dest_ref)

    n_valid = jnp.maximum(pad_end[N_EXPERTS - 1:, :] * (1.0 / TE), 1.0)
    tile = jnp.minimum(lax.broadcasted_iota(jnp.int32, (1, LANES), 1).astype(jnp.float32), n_valid - 1.0)
    tile_start = tile * TE
    tile_e = jnp.minimum(jnp.sum((pad_end <= tile_start).astype(jnp.float32), axis=0, keepdims=True),
                         N_EXPERTS - 1.0)
    is_e = sub.astype(jnp.float32) == tile_e
    seg_start = jnp.sum(jnp.where(is_e, pad_start, 0.0), axis=0, keepdims=True)
    seg_cnt = jnp.sum(jnp.where(is_e, cnt, 0.0), axis=0, keepdims=True)
    rows = jnp.clip(seg_cnt - (tile_start - seg_start), 0.0, TE)
    tab_ref[0, 0:1, :] = tile_e.astype(jnp.int32)
    tab_ref[0, 1:2, :] = rows.astype(jnp.int32)
    tab_ref[0, 2:3, :] = n_valid.astype(jnp.int32)
    tab_ref[0, 3:, :] = jnp.zeros((SUBLANES - 3, LANES), jnp.int32)


def _expert_kernel(te_ref, nv_ref, nrows_ref,
                   run_ref, first_ref, run_e_ref, n_runs_ref,
                   src_ref, src_next_ref, dst_ref,
                   dest_k0_ref, dest_k1_ref,
                   h2_hbm, w1_hbm, w3_hbm, w2_hbm,
                   out_hbm,
                   xt0, xt1, ob0, ob1, fill_vmem, asg_last, ssem, h2_ref, hsem, w1_buf, w3_buf, w2_buf, wsem,
                   *, t_total, t_part):
    h = pl.program_id(0)
    i = pl.program_id(1)
    g = h * pl.num_programs(1) + i
    nv = nv_ref[h]
    slab = lambda r: pl.ds(pl.multiple_of(r * ROW_TILES, ROW_TILES), ROW_TILES)
    toks_per_tile = TE // TOP_K
    inv_tiles = t_part // toks_per_tile
    part_slabs = t_part * ROW_TILES
    fetch_part = lambda part: pltpu.make_async_copy(h2_hbm.at[pl.ds(part * part_slabs, part_slabs), :], h2_ref, hsem)

    n_runs = n_runs_ref[0]
    run = run_ref[g]
    slot = run % WEIGHT_SLOTS

    def fetch_run(r, start):
        s, e = r % WEIGHT_SLOTS, run_e_ref[r]
        for hbm, buf in ((w1_hbm, w1_buf), (w3_hbm, w3_buf), (w2_hbm, w2_buf)):
            cp = pltpu.make_async_copy(hbm.at[e], buf.at[s], wsem.at[s])
            cp.start() if start else cp.wait()

    @pl.when((h == 0) & (i == 0))
    def _():
        for r in range(WEIGHT_SLOTS - 1):
            @pl.when(r < n_runs)
            def _():
                fetch_run(r, True)

    @pl.when((i < nv) & (first_ref[g] == 1))
    def _():
        fetch_run(run, False)

        @pl.when(run + (WEIGHT_SLOTS - 1) < n_runs)
        def _():
            fetch_run(run + (WEIGHT_SLOTS - 1), True)

    def last_part_src(tile):
        def src(r):
            a = asg_last[0, tile * TE + r]
            return a - jnp.where(a >= t_total, t_total + t_part, t_part)
        return src

    first_tables = (lambda r: src_ref[0, 0, r], lambda r: src_next_ref[0, 0, r],
                    lambda first, uu: dst_ref[0, 0, first + uu])
    last_tables = (last_part_src(i), last_part_src(jnp.minimum(i + 1, nv - 1)),
                   lambda first, uu: asg_last[0, (i * TE + first) + uu])

    def gather(src, xt):
        for r in range(TE):
            xt[r * ROW_TILES:(r + 1) * ROW_TILES, :] = h2_ref[slab(src(r)), :]

    def invert_share():
        first_tok = jnp.minimum(i, inv_tiles - 1) * toks_per_tile
        for k, dest_ref in enumerate((dest_k0_ref, dest_k1_ref)):
            first_asg = k * t_total + t_part + first_tok
            for t in range(toks_per_tile):
                asg_last[0, dest_ref[0, 0, t]] = first_asg + t

    def wait_scatter(tile, ob, other, sem):
        n = nrows_ref[tile] * ROW_TILES
        pltpu.make_async_copy(other.at[pl.ds(0, n), :], ob.at[pl.ds(0, n), :], sem).wait()

    def tile_body(xt, xt_next, ob, ob_other, sem, sem_other, tables, first_part):
        _, src_next, dst = tables

        @pl.when(i >= 2)
        def _():
            wait_scatter(g - 2, ob, ob_other, sem)

        def compute(m):
            gather(src_next, xt_next)
            if first_part:
                invert_share()
            x = jnp.concatenate([_row_slab(xt, s, m) for s in range(ROW_TILES)], axis=1).astype(jnp.bfloat16)
            a = jnp.dot(x, w1_buf[slot].astype(jnp.bfloat16), preferred_element_type=jnp.float32)
            b = jnp.dot(x, w3_buf[slot].astype(jnp.bfloat16), preferred_element_type=jnp.float32)
            hmid = (a * jax.nn.sigmoid(a)) * b
            o = jnp.dot(hmid.astype(jnp.bfloat16), w2_buf[slot].astype(jnp.bfloat16),
                        preferred_element_type=jnp.float32)
            for s in range(ROW_TILES):
                ob[pl.ds(s, m, stride=ROW_TILES), :] = o[:, s * LANES:(s + 1) * LANES]

        n = nrows_ref[g]

        @pl.when(n > TE // 2)
        def _():
            compute(TE)

        @pl.when(n <= TE // 2)
        def _():
            compute(TE // 2)

        if first_part:
            @pl.when(i == nv - 1)
            def _():
                fetch_part(1).start()

        copy = lambda first, uu: pltpu.make_async_copy(ob.at[slab(first + uu), :], out_hbm.at[dst(first, uu)], sem)

        def issue_groups(first_row, n_groups, group):
            def body(rb, carry):
                for uu in range(group):
                    copy(first_row + rb * group, uu).start(priority=uu % 2)
                return carry
            lax.fori_loop(0, n_groups, body, 0)

        n_full = n // DMA_UNROLL
        n_small = n // SUBLANES - n_full * (DMA_UNROLL // SUBLANES)
        issue_groups(0, n_full, DMA_UNROLL)
        issue_groups(n_full * DMA_UNROLL, n_small, SUBLANES)

        def tail(r, carry):
            copy(r, 0).start(priority=1)
            return carry
        lax.fori_loop((n // SUBLANES) * SUBLANES, n, tail, 0)

        @pl.when(i == nv - 1)
        def _():
            @pl.when(i >= 1)
            def _():
                wait_scatter(g - 1, ob_other, ob, sem_other)
            wait_scatter(g, ob, ob_other, sem)

    def part_body(tables, first_part):
        @pl.when(i == 0)
        def _():
            if not first_part:
                fetch_part(1).wait()
            gather(tables[0], xt0)

        @pl.when(i % 2 == 0)
        def _():
            tile_body(xt0, xt1, ob0, ob1, ssem.at[0], ssem.at[1], tables, first_part)

        @pl.when(i % 2 == 1)
        def _():
            tile_body(xt1, xt0, ob1, ob0, ssem.at[1], ssem.at[0], tables, first_part)

    @pl.when((h == 0) & (i == 0))
    def _():
        fetch_part(0).start()
        fill_vmem[...] = jnp.full(fill_vmem.shape, t_part, jnp.int32)
        pltpu.sync_copy(fill_vmem, asg_last)
        fetch_part(0).wait()

    @pl.when((i < nv) & (h == 0))
    def _():
        part_body(first_tables, True)

    @pl.when((i < nv) & (h > 0))
    def _():
        part_body(last_tables, False)


def _combine_kernel(x1_ref, gcol_ref, nfg_ref, o0_ref, o1_ref, yp_ref, ys_ref, *, n_prompt_tiles):
    i = pl.program_id(0)
    o0 = jnp.concatenate([_row_slab(o0_ref, s, TC) for s in range(ROW_TILES)], axis=1)
    o1 = jnp.concatenate([_row_slab(o1_ref, s, TC) for s in range(ROW_TILES)], axis=1)
    g = gcol_ref[...]
    moe = g[:, 0:1] * o0 + g[:, 1:2] * o1
    y = _rmsnorm(x1_ref[...] + moe, nfg_ref[...])

    @pl.when(i < n_prompt_tiles)
    def _():
        yp_ref[...] = y

    @pl.when(i >= n_prompt_tiles)
    def _():
        for blk in range(TC // TM):
            for q in range(TM // SEQ_BLK):
                r0 = blk * TM + q * SEQ_BLK
                ys_ref[blk * SEQ_BLK:(blk + 1) * SEQ_BLK, q, :] = y[r0:r0 + SEQ_BLK, :]


def _const_spec(shape):
    return pl.BlockSpec(shape, lambda *_: (0,) * len(shape))


def kernel(x_prompt, x_sample, state_pool, norm1_g, w_in, a_norm_g, a_norm_b, a_ws, a_bs, b_w, b_scale, w_out,
           norm2_g, r1_w, r1_b, r2_w, r2_b, exp_w1, exp_w3, exp_w2, normf_g):
    f32, bf16, i32 = jnp.float32, jnp.bfloat16, jnp.int32
    n_batch, seq, _ = x_prompt.shape
    dec_batch, dec_seq, _ = x_sample.shape
    assert norm1_g.shape[0] == 1 and seq % TM == 0 and TM % CHUNK == 0
    assert dec_seq * SEQ_BLK == TM and dec_batch % SEQ_BLK == 0 and dec_seq <= CHUNK
    t_prompt = n_batch * seq
    t_sample = dec_batch * dec_seq
    t_total = t_prompt + t_sample
    n_tok_tiles = t_total // TM
    n_prompt_tiles = t_prompt // TM
    n_sample_tiles = t_sample // TM
    seq_tiles = seq // TM
    plane_rows = t_total * ROW_TILES
    assert n_tok_tiles % N_PARTS == 0
    part_tiles = n_tok_tiles // N_PARTS
    t_part = part_tiles * TM
    n_exp_tiles = -(-(TOP_K * t_part + N_EXPERTS * (TE - 1)) // TE)
    p_rows = n_exp_tiles * TE

    n1g = norm1_g[0][None, :]
    n2g = norm2_g[0][None, :]
    nfg = normf_g[None, :]
    win = w_in[0].astype(bf16)
    wout = w_out[0].astype(bf16)
    ang = a_norm_g[0][None, :]
    anb = a_norm_b[0][None, :]
    bw = b_w[0].astype(bf16)
    bscale = b_scale[0][None, :]
    abias = jnp.repeat(a_bs[0][:, :CHUNK].T, HEAD_DIM, axis=1)
    rw = jnp.concatenate([r1_w[0], r2_w[0].transpose(1, 0, 2).reshape(D_MODEL, N_EXPERTS),
                          jnp.zeros((D_MODEL, N_ROUTER_ROWS - N_GROUPS - N_EXPERTS), f32)], axis=1)
    rw_hi = rw.astype(bf16)
    rw_lo = (rw - rw_hi.astype(f32)).astype(bf16)
    rwt = jnp.concatenate([rw_hi, rw_lo, jnp.zeros((D_MODEL, LANES - 2 * N_ROUTER_ROWS), bf16)], axis=1)
    rbias = jnp.concatenate([r1_b[0], r2_b[0].reshape(-1),
                             jnp.zeros((N_ROUTER_ROWS - N_GROUPS - N_EXPERTS,), f32)])
    rb = jnp.broadcast_to(rbias[:, None], (N_ROUTER_ROWS, TM))
    su = (jnp.arange(TM)[:, None] < jnp.arange(TM)[None, :]).astype(bf16)
    st_planes = state_pool[0].reshape(n_sample_tiles, SEQ_BLK, POOL_STATE, B_WIDTH).transpose(0, 2, 1, 3)
    aws_s = a_ws[0][:, :dec_seq, :dec_seq].reshape(-1)
    abs_s = a_bs[0][:, :dec_seq].reshape(-1)

    cparams = pltpu.CompilerParams(dimension_semantics=("arbitrary",), vmem_limit_bytes=VMEM_LIMIT)

    p_tile = lambda i: jnp.minimum(i, n_prompt_tiles - 1)
    s_tile = lambda i: jnp.maximum(i - n_prompt_tiles, 0)
    sample_rows_shape = jax.ShapeDtypeStruct((dec_batch, dec_seq, A_WIDTH), f32)
    assert N_PARTS >= 2 and n_prompt_tiles >= part_tiles and t_part % TM == 0
    x1, h2, meta, gcol, counts, pstate, p_rows_sample, v_rows_sample, asg_head = pl.pallas_call(
        functools.partial(_mixer_kernel, n_prompt_tiles=n_prompt_tiles, seq_tiles=seq_tiles, part_tiles=part_tiles,
                          t_total=t_total),
        grid_spec=pltpu.PrefetchScalarGridSpec(
            num_scalar_prefetch=2,
            grid=(n_tok_tiles,),
            in_specs=[
                pl.BlockSpec((1, TM, D_MODEL), lambda i, *_: (0, 0, 0)),
                pl.BlockSpec((1, TM, D_MODEL),
                             lambda i, *_: (p_tile(i + 1) // seq_tiles, p_tile(i + 1) % seq_tiles, 0)),
                pl.BlockSpec((SEQ_BLK, dec_seq, D_MODEL),
                             lambda i, *_: (jnp.minimum(s_tile(i + 1), n_sample_tiles - 1), 0, 0)),
                pl.BlockSpec((1, POOL_STATE, SEQ_BLK, B_WIDTH), lambda i, *_: (s_tile(i), 0, 0, 0)),
                _const_spec((1, D_MODEL)), _const_spec((D_MODEL, 3 * A_WIDTH)), _const_spec((1, A_WIDTH)),
                _const_spec((1, A_WIDTH)), _const_spec((N_HEADS, CHUNK, CHUNK)), _const_spec((CHUNK, A_WIDTH)),
                _const_spec((N_GROUPS, HEAD_DIM, HEAD_DIM)), _const_spec((1, B_WIDTH)),
                _const_spec((D_MODEL, D_MODEL)), _const_spec((1, D_MODEL)), _const_spec((D_MODEL, LANES)),
                _const_spec((N_ROUTER_ROWS, TM)), _const_spec((TM, TM)),
            ],
            out_specs=[
                pl.BlockSpec((TM, D_MODEL), lambda i, *_: (i, 0)),
                pl.BlockSpec((TM * ROW_TILES, LANES), lambda i, *_: (i, 0)),
                pl.BlockSpec((SUBLANES, TM), lambda i, *_: (0, i)),
                pl.BlockSpec((TM, LANES), lambda i, *_: (i, 0)),
                pl.BlockSpec((1, N_EXPERTS, LANES), lambda i, *_: (i // part_tiles, 0, 0)),
                pl.BlockSpec((1, CARRY_ROWS, B_WIDTH), lambda i, *_: (p_tile(i) // seq_tiles, 0, 0)),
                pl.BlockSpec((SEQ_BLK, dec_seq, A_WIDTH), lambda i, *_: (s_tile(i), 0, 0)),
                pl.BlockSpec((SEQ_BLK, dec_seq, A_WIDTH), lambda i, *_: (s_tile(i), 0, 0)),
                pl.BlockSpec((1, 1, p_rows), lambda i, *_: (jnp.maximum(i // part_tiles - 1, 0), 0, 0),
                             memory_space=pltpu.SMEM),
            ],
            scratch_shapes=[pltpu.VMEM((TM, D_MODEL), bf16), pltpu.VMEM((CARRY_ROWS, B_WIDTH), f32),
                            pltpu.VMEM((N_EXPERTS, LANES), f32), pltpu.VMEM((TM, D_MODEL), f32),
                            pltpu.VMEM((TM, A_WIDTH), f32), pltpu.VMEM((TM, A_WIDTH), f32),
                            pltpu.VMEM((TM, B_WIDTH), f32),
                            pltpu.VMEM((2 * TOP_K, t_part), i32), pltpu.VMEM((TOP_K, 1, t_part), i32),
                            pltpu.VMEM((1, 1, p_rows), i32)] + [pltpu.SMEM((1, t_part), i32)] * TOP_K,
        ),
        out_shape=[
            jax.ShapeDtypeStruct((t_total, D_MODEL), f32),
            jax.ShapeDtypeStruct((plane_rows, LANES), f32),
            jax.ShapeDtypeStruct((SUBLANES, t_total), i32),
            jax.ShapeDtypeStruct((t_total, LANES), f32),
            jax.ShapeDtypeStruct((N_PARTS, N_EXPERTS, LANES), f32),
            jax.ShapeDtypeStruct((n_batch, CARRY_ROWS, B_WIDTH), f32),
            sample_rows_shape, sample_rows_shape,
            jax.ShapeDtypeStruct((N_PARTS - 1, 1, p_rows), i32),
        ],
        compiler_params=cparams,
        name="mixer",
    )(aws_s, abs_s, x_prompt, x_prompt, x_sample, st_planes, n1g, win, ang, anb, a_ws[0][:, :CHUNK, :CHUNK], abias,
      bw, bscale, wout, n2g, rwt, rb, su)

    assert n_exp_tiles <= LANES and t_part % LANES == 0
    dest_last, tab = pl.pallas_call(
        _tables_kernel,
        grid=(N_PARTS,),
        in_specs=[pl.BlockSpec((1, N_EXPERTS, LANES), lambda h: (h, 0, 0)),
                  pl.BlockSpec((SUBLANES, t_part), lambda h: (0, h))],
        out_specs=[_const_spec((TOP_K, 1, t_part)),
                   pl.BlockSpec((1, SUBLANES, LANES), lambda h: (h, 0, 0))],
        out_shape=[jax.ShapeDtypeStruct((TOP_K, 1, t_part), i32),
                   jax.ShapeDtypeStruct((N_PARTS, SUBLANES, LANES), i32)],
        compiler_params=cparams,
        name="route_tables",
    )(counts, meta)
    tile_e = tab[:, 0, :n_exp_tiles].reshape(-1)
    tile_rows = tab[:, 1, :n_exp_tiles].reshape(-1)
    n_valid = tab[:, 2, 0]
    assert N_PARTS == 2 and (TOP_K * t_part) % TE == 0
    asg = asg_head.reshape(-1)
    row_tok = jnp.where(asg >= t_total, asg - t_total, asg)
    row_src = jnp.where(asg >= 0, row_tok, 0).reshape(n_exp_tiles, 1, TE)
    row_dst = jnp.maximum(asg, 0).reshape(n_exp_tiles, 1, TE)

    cur_blk = lambda h, i, te, nv, *_: (jnp.minimum(i, nv[0] - 1), 0, 0)
    nxt_blk = lambda h, i, te, nv, *_: (jnp.minimum(i + 1, nv[0] - 1), 0, 0)
    n_flat = N_PARTS * n_exp_tiles
    run_starts = jnp.concatenate([jnp.ones((1,), i32), (tile_e[1:] != tile_e[:-1]).astype(i32)])
    flat_ids = jnp.arange(n_flat, dtype=i32)
    tile_run = jnp.sum(jnp.where(flat_ids[None, :] <= flat_ids[:, None], run_starts[None, :], 0), axis=1) - 1
    n_runs = tile_run[-1:] + 1
    run_owner = (tile_run[None, :] == flat_ids[:, None]) & (run_starts[None, :] == 1)
    run_e = jnp.sum(jnp.where(run_owner, tile_e[None, :], 0), axis=1)
    smem_blk = lambda imap: pl.BlockSpec((1, 1, TE), imap, memory_space=pltpu.SMEM)
    last_share = t_part // (TE // TOP_K) - 1
    share_blk = lambda k: pl.BlockSpec(
        (1, 1, TE // TOP_K), lambda h, i, *_: (k, 0, jnp.where(h == 0, jnp.minimum(i, last_share), last_share)),
        memory_space=pltpu.SMEM)
    row_buf = pltpu.VMEM((TE * ROW_TILES, LANES), f32)
    out_tok = pl.pallas_call(
        functools.partial(_expert_kernel, t_total=t_total, t_part=t_part),
        grid_spec=pltpu.PrefetchScalarGridSpec(
            num_scalar_prefetch=7,
            grid=(N_PARTS, n_exp_tiles),
            in_specs=[smem_blk(cur_blk), smem_blk(nxt_blk), smem_blk(cur_blk),
                      share_blk(0), share_blk(1),
                      pl.BlockSpec(memory_space=pl.ANY), pl.BlockSpec(memory_space=pl.ANY),
                      pl.BlockSpec(memory_space=pl.ANY), pl.BlockSpec(memory_space=pl.ANY)],
            out_specs=pl.BlockSpec(memory_space=pl.ANY),
            scratch_shapes=[row_buf, row_buf, row_buf, row_buf, pltpu.VMEM((1, p_rows), i32),
                            pltpu.SMEM((1, p_rows), i32), pltpu.SemaphoreType.DMA((2,)),
                            pltpu.VMEM((t_part * ROW_TILES, LANES), f32), pltpu.SemaphoreType.DMA(()),
                            pltpu.VMEM((WEIGHT_SLOTS, D_MODEL, D_EXPERT), f32),
                            pltpu.VMEM((WEIGHT_SLOTS, D_MODEL, D_EXPERT), f32),
                            pltpu.VMEM((WEIGHT_SLOTS, D_EXPERT, D_MODEL), f32),
                            pltpu.SemaphoreType.DMA((WEIGHT_SLOTS,))],
        ),
        out_shape=jax.ShapeDtypeStruct((TOP_K * t_total, ROW_TILES, LANES), f32),
        compiler_params=pltpu.CompilerParams(dimension_semantics=("arbitrary", "arbitrary"),
                                             vmem_limit_bytes=EXPERT_VMEM_LIMIT),
        name="moe_experts",
    )(tile_e, n_valid, tile_rows, tile_run, run_starts, run_e, n_runs, row_src, row_src, row_dst, dest_last, dest_last, h2, exp_w1[0], exp_w3[0], exp_w2[0])
    out_tok = out_tok.reshape(TOP_K * plane_rows, LANES)

    assert t_prompt % TC == 0 and t_sample % TC == 0
    n_c_prompt = t_prompt // TC
    y_prompt, y_sample = pl.pallas_call(
        functools.partial(_combine_kernel, n_prompt_tiles=n_c_prompt),
        grid=(t_total // TC,),
        in_specs=[pl.BlockSpec((TC, D_MODEL), lambda i: (i, 0)),
                  pl.BlockSpec((TC, LANES), lambda i: (i, 0)),
                  _const_spec((1, D_MODEL)),
                  pl.BlockSpec((TC * ROW_TILES, LANES), lambda i: (i, 0)),
                  pl.BlockSpec((TC * ROW_TILES, LANES), lambda i: (t_total // TC + i, 0))],
        out_specs=[pl.BlockSpec((TC, D_MODEL), lambda i: (jnp.minimum(i, n_c_prompt - 1), 0)),
                   pl.BlockSpec((TC // dec_seq, dec_seq, D_MODEL), lambda i: (jnp.maximum(i - n_c_prompt, 0), 0, 0))],
        out_shape=[jax.ShapeDtypeStruct((t_prompt, D_MODEL), f32),
                   jax.ShapeDtypeStruct((dec_batch, dec_seq, D_MODEL), f32)],
        compiler_params=cparams,
        name="moe_combine",
    )(x1, gcol, nfg, out_tok, out_tok)

    y_prompt = y_prompt.reshape(n_batch, seq, D_MODEL)
    pool_state_prompt = pstate[None, :, CARRY_ROWS - POOL_STATE:, :]
    pool_state_sample = jnp.concatenate([state_pool[0], p_rows_sample], axis=1)[None, :, -POOL_STATE:, :]
    chunk_v_sample = v_rows_sample[None]
    return (y_prompt, y_sample, pool_state_prompt, pool_state_sample, chunk_v_sample)
```
